```python
import jax, jax.numpy as jnp
from jax import lax
import numpy as np

D_MODEL = 2048
BATCH = 8
SEQ = 8192
DEPTH = 1

PLE_DIM = 256
CONV_A_WIDTH = 1024
CONV_A_K = 3
CONF_WIDTH = 1024
CONF_K = 31
D_FF = -(-8 * D_MODEL // (3 * 256)) * 256
EPS = 1e-6
LN_EPS = 1e-5

IN_SPLIT_SIZES = (
    CONV_A_WIDTH,
    CONV_A_WIDTH,
    CONV_A_WIDTH,
    CONF_WIDTH,
    CONF_WIDTH,
    D_MODEL,
    D_MODEL,
)
IN_COLS = sum(IN_SPLIT_SIZES)

kernel_name = "hybrid_gated_conv_conformer_block"


def rmsnorm(x, g):
    xf = x.astype(jnp.float32)
    y = xf * lax.rsqrt(jnp.mean(xf * xf, axis=-1, keepdims=True) + EPS)
    return (y * g.astype(jnp.float32)).astype(x.dtype)


def layernorm(x, g, b):
    xf = x.astype(jnp.float32)
    mu = jnp.mean(xf, axis=-1, keepdims=True)
    xc = xf - mu
    var = jnp.mean(xc * xc, axis=-1, keepdims=True)
    y = xc * lax.rsqrt(var + LN_EPS)
    return (y * g.astype(jnp.float32) + b.astype(jnp.float32)).astype(x.dtype)


def causal_depthwise_conv(u, w):
    k, c = w.shape
    return lax.conv_general_dilated(
        u, w[:, None, :].astype(u.dtype),
        window_strides=(1,), padding=[(k - 1, 0)],
        dimension_numbers=("NWC", "WIO", "NWC"),
        feature_group_count=c)


def _fwd_setup_inputs(seed: int = 0) -> dict:
    key = jax.random.key(seed)
    ks = jax.random.split(key, 24)
    f32 = jnp.float32
    L = DEPTH

    def nrm(k, shape, scale):
        return jax.random.normal(k, shape, f32) * scale

    def gain(k, shape):
        return 1.0 + 0.02 * jax.random.normal(k, shape, f32)

    return {
        "x": jax.random.normal(ks[0], (BATCH, SEQ, D_MODEL), f32),
        "p": jax.random.normal(ks[1], (DEPTH, BATCH, SEQ, PLE_DIM), f32),
        "g_mix": gain(ks[2], (L, D_MODEL)),
        "w_in": nrm(ks[3], (L, D_MODEL, IN_COLS), D_MODEL ** -0.5),
        "conv_a_w": nrm(ks[4], (L, CONV_A_K, CONV_A_WIDTH), CONV_A_K ** -0.5),
        "w_out_a": nrm(ks[5], (L, CONV_A_WIDTH, D_MODEL), CONV_A_WIDTH ** -0.5),
        "b_glu": nrm(ks[6], (L, 2 * CONF_WIDTH), 0.02),
        "conf_dw_w": nrm(ks[7], (L, CONF_K, CONF_WIDTH), CONF_K ** -0.5),
        "conf_dw_b": nrm(ks[8], (L, CONF_WIDTH), 0.02),
        "conf_ln_g": gain(ks[9], (L, CONF_WIDTH)),
        "conf_ln_b": nrm(ks[10], (L, CONF_WIDTH), 0.02),
        "w_pw_b": nrm(ks[11], (L, CONF_WIDTH, D_MODEL), CONF_WIDTH ** -0.5),
        "b_pw_b": nrm(ks[12], (L, D_MODEL), 0.02),
        "w_o": nrm(ks[13], (L, D_MODEL, D_MODEL), D_MODEL ** -0.5),
        "g_ffn": gain(ks[14], (L, D_MODEL)),
        "w_gate": nrm(ks[15], (L, D_MODEL, D_FF), D_MODEL ** -0.5),
        "w_up": nrm(ks[16], (L, D_MODEL, D_FF), D_MODEL ** -0.5),
        "w_down": nrm(ks[17], (L, D_FF, D_MODEL), D_FF ** -0.5),
        "g_ple": gain(ks[18], (L, D_MODEL)),
        "w_ple_gate": nrm(ks[19], (L, D_MODEL, D_MODEL), D_MODEL ** -0.5),
        "w_ple_proj": nrm(ks[20], (L, PLE_DIM, D_MODEL), PLE_DIM ** -0.5),
        "g_final": gain(ks[21], (D_MODEL,)),
    }


def _fwd_reference(x, p, g_mix, w_in, conv_a_w, w_out_a, b_glu, conf_dw_w, conf_dw_b,
              conf_ln_g, conf_ln_b, w_pw_b, b_pw_b, w_o, g_ffn, w_gate, w_up, w_down,
              g_ple, w_ple_gate, w_ple_proj, g_final):
    h = x
    cuts = np.cumsum(IN_SPLIT_SIZES)[:-1].tolist()
    for i in range(DEPTH):
        n = rmsnorm(h, g_mix[i])
        proj = jnp.einsum("bsd,dc->bsc", n, w_in[i])
        a_h, a_b, a_c, glu_v, glu_g, gate_a, gate_b = jnp.split(proj, cuts, axis=-1)

        y_a = a_b * causal_depthwise_conv(a_c * a_h, conv_a_w[i])
        y_a = jnp.einsum("bsc,cd->bsd", y_a, w_out_a[i])

        bv, bg = jnp.split(b_glu[i], 2)
        u = (glu_v + bv) * jax.nn.sigmoid(glu_g + bg)
        v = causal_depthwise_conv(u, conf_dw_w[i]) + conf_dw_b[i]
        v = jax.nn.silu(layernorm(v, conf_ln_g[i], conf_ln_b[i]))
        y_b = jnp.einsum("bsc,cd->bsd", v, w_pw_b[i]) + b_pw_b[i]

        m = jax.nn.sigmoid(gate_a) * y_a + jax.nn.sigmoid(gate_b) * y_b
        h = h + jnp.einsum("bsd,de->bse", m, w_o[i])

        n2 = rmsnorm(h, g_ffn[i])
        f = jax.nn.silu(jnp.einsum("bsd,df->bsf", n2, w_gate[i])) * jnp.einsum("bsd,df->bsf", n2, w_up[i])
        h = h + jnp.einsum("bsf,fd->bsd", f, w_down[i])

        n3 = rmsnorm(h, g_ple[i])
        ple = jnp.einsum("bse,ed->bsd", p[i].astype(h.dtype), w_ple_proj[i])
        h = h + jax.nn.sigmoid(jnp.einsum("bsd,de->bse", n3, w_ple_gate[i])) * ple

    return rmsnorm(h, g_final)


import jax as _jax
import jax.numpy as _jnp

TWIN_FORMAT = 'train_step'
FWD_PARAMS = ['x', 'p', 'g_mix', 'w_in', 'conv_a_w', 'w_out_a', 'b_glu', 'conf_dw_w', 'conf_dw_b', 'conf_ln_g', 'conf_ln_b', 'w_pw_b', 'b_pw_b', 'w_o', 'g_ffn', 'w_gate', 'w_up', 'w_down', 'g_ple', 'w_ple_gate', 'w_ple_proj', 'g_final']
TWIN_WEIGHTS = ['g_mix', 'w_in', 'conv_a_w', 'w_out_a', 'b_glu', 'conf_dw_w', 'conf_dw_b', 'conf_ln_g', 'conf_ln_b', 'w_pw_b', 'b_pw_b', 'w_o', 'g_ffn', 'w_gate', 'w_up', 'w_down', 'g_ple', 'w_ple_gate', 'w_ple_proj', 'g_final']
TWIN_DIFF_INPUT = 'x'
TWIN_INPUTS = ['x', 'p', 'g_mix', 'w_in', 'conv_a_w', 'w_out_a', 'b_glu', 'conf_dw_w', 'conf_dw_b', 'conf_ln_g', 'conf_ln_b', 'w_pw_b', 'b_pw_b', 'w_o', 'g_ffn', 'w_gate', 'w_up', 'w_down', 'g_ple', 'w_ple_gate', 'w_ple_proj', 'g_final', 'loss_target', 'm_g_mix', 'm_w_in', 'm_conv_a_w', 'm_w_out_a', 'm_b_glu', 'm_conf_dw_w', 'm_conf_dw_b', 'm_conf_ln_g', 'm_conf_ln_b', 'm_w_pw_b', 'm_b_pw_b', 'm_w_o', 'm_g_ffn', 'm_w_gate', 'm_w_up', 'm_w_down', 'm_g_ple', 'm_w_ple_gate', 'm_w_ple_proj', 'm_g_final', 'v_g_mix', 'v_w_in', 'v_conv_a_w', 'v_w_out_a', 'v_b_glu', 'v_conf_dw_w', 'v_conf_dw_b', 'v_conf_ln_g', 'v_conf_ln_b', 'v_w_pw_b', 'v_b_pw_b', 'v_w_o', 'v_g_ffn', 'v_w_gate', 'v_w_up', 'v_w_down', 'v_g_ple', 'v_w_ple_gate', 'v_w_ple_proj', 'v_g_final']
TWIN_OUTPUTS = ['loss', 'grad_x', 'grad_g_mix', 'grad_w_in', 'grad_conv_a_w', 'grad_w_out_a', 'grad_b_glu', 'grad_conf_dw_w', 'grad_conf_dw_b', 'grad_conf_ln_g', 'grad_conf_ln_b', 'grad_w_pw_b', 'grad_b_pw_b', 'grad_w_o', 'grad_g_ffn', 'grad_w_gate', 'grad_w_up', 'grad_w_down', 'grad_g_ple', 'grad_w_ple_gate', 'grad_w_ple_proj', 'grad_g_final', 'delta_g_mix', 'delta_w_in', 'delta_conv_a_w', 'delta_w_out_a', 'delta_b_glu', 'delta_conf_dw_w', 'delta_conf_dw_b', 'delta_conf_ln_g', 'delta_conf_ln_b', 'delta_w_pw_b', 'delta_b_pw_b', 'delta_w_o', 'delta_g_ffn', 'delta_w_gate', 'delta_w_up', 'delta_w_down', 'delta_g_ple', 'delta_w_ple_gate', 'delta_w_ple_proj', 'delta_g_final', 'new_m_g_mix', 'new_m_w_in', 'new_m_conv_a_w', 'new_m_w_out_a', 'new_m_b_glu', 'new_m_conf_dw_w', 'new_m_conf_dw_b', 'new_m_conf_ln_g', 'new_m_conf_ln_b', 'new_m_w_pw_b', 'new_m_b_pw_b', 'new_m_w_o', 'new_m_g_ffn', 'new_m_w_gate', 'new_m_w_up', 'new_m_w_down', 'new_m_g_ple', 'new_m_w_ple_gate', 'new_m_w_ple_proj', 'new_m_g_final', 'new_v_g_mix', 'new_v_w_in', 'new_v_conv_a_w', 'new_v_w_out_a', 'new_v_b_glu', 'new_v_conf_dw_w', 'new_v_conf_dw_b', 'new_v_conf_ln_g', 'new_v_conf_ln_b', 'new_v_w_pw_b', 'new_v_b_pw_b', 'new_v_w_o', 'new_v_g_ffn', 'new_v_w_gate', 'new_v_w_up', 'new_v_w_down', 'new_v_g_ple', 'new_v_w_ple_gate', 'new_v_w_ple_proj', 'new_v_g_final']
TWIN_LEAF_KINDS = {'loss': 'loss', 'grad_x': 'grad_x', 'grad_g_mix': 'grad_w', 'grad_w_in': 'grad_w', 'grad_conv_a_w': 'grad_w', 'grad_w_out_a': 'grad_w', 'grad_b_glu': 'grad_w', 'grad_conf_dw_w': 'grad_w', 'grad_conf_dw_b': 'grad_w', 'grad_conf_ln_g': 'grad_w', 'grad_conf_ln_b': 'grad_w', 'grad_w_pw_b': 'grad_w', 'grad_b_pw_b': 'grad_w', 'grad_w_o': 'grad_w', 'grad_g_ffn': 'grad_w', 'grad_w_gate': 'grad_w', 'grad_w_up': 'grad_w', 'grad_w_down': 'grad_w', 'grad_g_ple': 'grad_w', 'grad_w_ple_gate': 'grad_w', 'grad_w_ple_proj': 'grad_w', 'grad_g_final': 'grad_w', 'delta_g_mix': 'delta_w', 'delta_w_in': 'delta_w', 'delta_conv_a_w': 'delta_w', 'delta_w_out_a': 'delta_w', 'delta_b_glu': 'delta_w', 'delta_conf_dw_w': 'delta_w', 'delta_conf_dw_b': 'delta_w', 'delta_conf_ln_g': 'delta_w', 'delta_conf_ln_b': 'delta_w', 'delta_w_pw_b': 'delta_w', 'delta_b_pw_b': 'delta_w', 'delta_w_o': 'delta_w', 'delta_g_ffn': 'delta_w', 'delta_w_gate': 'delta_w', 'delta_w_up': 'delta_w', 'delta_w_down': 'delta_w', 'delta_g_ple': 'delta_w', 'delta_w_ple_gate': 'delta_w', 'delta_w_ple_proj': 'delta_w', 'delta_g_final': 'delta_w', 'new_m_g_mix': 'new_m', 'new_m_w_in': 'new_m', 'new_m_conv_a_w': 'new_m', 'new_m_w_out_a': 'new_m', 'new_m_b_glu': 'new_m', 'new_m_conf_dw_w': 'new_m', 'new_m_conf_dw_b': 'new_m', 'new_m_conf_ln_g': 'new_m', 'new_m_conf_ln_b': 'new_m', 'new_m_w_pw_b': 'new_m', 'new_m_b_pw_b': 'new_m', 'new_m_w_o': 'new_m', 'new_m_g_ffn': 'new_m', 'new_m_w_gate': 'new_m', 'new_m_w_up': 'new_m', 'new_m_w_down': 'new_m', 'new_m_g_ple': 'new_m', 'new_m_w_ple_gate': 'new_m', 'new_m_w_ple_proj': 'new_m', 'new_m_g_final': 'new_m', 'new_v_g_mix': 'new_v', 'new_v_w_in': 'new_v', 'new_v_conv_a_w': 'new_v', 'new_v_w_out_a': 'new_v', 'new_v_b_glu': 'new_v', 'new_v_conf_dw_w': 'new_v', 'new_v_conf_dw_b': 'new_v', 'new_v_conf_ln_g': 'new_v', 'new_v_conf_ln_b': 'new_v', 'new_v_w_pw_b': 'new_v', 'new_v_b_pw_b': 'new_v', 'new_v_w_o': 'new_v', 'new_v_g_ffn': 'new_v', 'new_v_w_gate': 'new_v', 'new_v_w_up': 'new_v', 'new_v_w_down': 'new_v', 'new_v_g_ple': 'new_v', 'new_v_w_ple_gate': 'new_v', 'new_v_w_ple_proj': 'new_v', 'new_v_g_final': 'new_v'}


def _forward(args):
    return _fwd_reference(*[args[k] for k in FWD_PARAMS])


def _output_shape():
    def fwd():
        inp = _fwd_setup_inputs(0)
        return _fwd_reference(*[inp[k] for k in FWD_PARAMS])
    out = _jax.eval_shape(fwd)
    return out.shape, out.dtype

N_MICROBATCH = 1
ADAM_LR = 0.001
ADAM_B1 = 0.9
ADAM_B2 = 0.999
ADAM_EPS = 1e-08
ADAM_WD = 0.01
ADAM_STEP = 10
PER_EXAMPLE_BATCH_AXIS = {'x': 0, 'p': 1, 'loss_target': 0}
SHARED_INPUTS = []
_WEIGHT_DTYPES = {'g_mix': _jnp.float32, 'w_in': _jnp.float32, 'conv_a_w': _jnp.float32, 'w_out_a': _jnp.float32, 'b_glu': _jnp.float32, 'conf_dw_w': _jnp.float32, 'conf_dw_b': _jnp.float32, 'conf_ln_g': _jnp.float32, 'conf_ln_b': _jnp.float32, 'w_pw_b': _jnp.float32, 'b_pw_b': _jnp.float32, 'w_o': _jnp.float32, 'g_ffn': _jnp.float32, 'w_gate': _jnp.float32, 'w_up': _jnp.float32, 'w_down': _jnp.float32, 'g_ple': _jnp.float32, 'w_ple_gate': _jnp.float32, 'w_ple_proj': _jnp.float32, 'g_final': _jnp.float32}
MOMENT_SCALE = {'g_mix': 1.181460e-01, 'w_in': 5.349757e-02, 'conv_a_w': 8.369774e-02, 'w_out_a': 5.929294e-02, 'b_glu': 4.269974e-02, 'conf_dw_w': 5.266922e-02, 'conf_dw_b': 1.108954e-01, 'conf_ln_g': 6.100427e-02, 'conf_ln_b': 5.928769e-02, 'w_pw_b': 3.613973e-02, 'b_pw_b': 6.708051e-02, 'w_o': 6.964562e-02, 'g_ffn': 8.300622e-02, 'w_gate': 3.291950e-02, 'w_up': 3.188134e-02, 'w_down': 5.287672e-02, 'g_ple': 1.853061e-02, 'w_ple_gate': 1.875987e-02, 'w_ple_proj': 4.771562e-02, 'g_final': 3.199880e+01}


def _to_microbatches(a, axis):
    t = _jnp.moveaxis(a, axis, 0)
    t = t.reshape((N_MICROBATCH, t.shape[0] // N_MICROBATCH) + t.shape[1:])
    return _jnp.moveaxis(t, 1, axis + 1)


def setup_inputs(seed: int = 0) -> dict:
    inp = _fwd_setup_inputs(seed)
    key = _jax.random.fold_in(_jax.random.key(seed), 7919)
    shape, _ = _output_shape()
    out = dict(inp)
    out["loss_target"] = _jax.random.normal(_jax.random.fold_in(key, 0), shape, _jnp.float32)
    for i, name in enumerate(TWIN_WEIGHTS):
        w = inp[name].astype(_jnp.float32)
        if MOMENT_SCALE is None:
            s = _jnp.sqrt(_jnp.mean(_jnp.square(w)) + 1e-30)
        else:
            s = MOMENT_SCALE[name]
        km, kv = _jax.random.split(_jax.random.fold_in(key, i + 1))
        out[name] = w
        out["m_" + name] = s * _jax.random.normal(km, w.shape, _jnp.float32)
        out["v_" + name] = (s * s) * _jax.random.uniform(kv, w.shape, _jnp.float32, 0.5, 1.5)
    if N_MICROBATCH > 1:
        for name, axis in PER_EXAMPLE_BATCH_AXIS.items():
            out[name] = _to_microbatches(out[name], axis)
    return {'x': out['x'], 'p': out['p'], 'g_mix': out['g_mix'], 'w_in': out['w_in'], 'conv_a_w': out['conv_a_w'], 'w_out_a': out['w_out_a'], 'b_glu': out['b_glu'], 'conf_dw_w': out['conf_dw_w'], 'conf_dw_b': out['conf_dw_b'], 'conf_ln_g': out['conf_ln_g'], 'conf_ln_b': out['conf_ln_b'], 'w_pw_b': out['w_pw_b'], 'b_pw_b': out['b_pw_b'], 'w_o': out['w_o'], 'g_ffn': out['g_ffn'], 'w_gate': out['w_gate'], 'w_up': out['w_up'], 'w_down': out['w_down'], 'g_ple': out['g_ple'], 'w_ple_gate': out['w_ple_gate'], 'w_ple_proj': out['w_ple_proj'], 'g_final': out['g_final'], 'loss_target': out['loss_target'], 'm_g_mix': out['m_g_mix'], 'm_w_in': out['m_w_in'], 'm_conv_a_w': out['m_conv_a_w'], 'm_w_out_a': out['m_w_out_a'], 'm_b_glu': out['m_b_glu'], 'm_conf_dw_w': out['m_conf_dw_w'], 'm_conf_dw_b': out['m_conf_dw_b'], 'm_conf_ln_g': out['m_conf_ln_g'], 'm_conf_ln_b': out['m_conf_ln_b'], 'm_w_pw_b': out['m_w_pw_b'], 'm_b_pw_b': out['m_b_pw_b'], 'm_w_o': out['m_w_o'], 'm_g_ffn': out['m_g_ffn'], 'm_w_gate': out['m_w_gate'], 'm_w_up': out['m_w_up'], 'm_w_down': out['m_w_down'], 'm_g_ple': out['m_g_ple'], 'm_w_ple_gate': out['m_w_ple_gate'], 'm_w_ple_proj': out['m_w_ple_proj'], 'm_g_final': out['m_g_final'], 'v_g_mix': out['v_g_mix'], 'v_w_in': out['v_w_in'], 'v_conv_a_w': out['v_conv_a_w'], 'v_w_out_a': out['v_w_out_a'], 'v_b_glu': out['v_b_glu'], 'v_conf_dw_w': out['v_conf_dw_w'], 'v_conf_dw_b': out['v_conf_dw_b'], 'v_conf_ln_g': out['v_conf_ln_g'], 'v_conf_ln_b': out['v_conf_ln_b'], 'v_w_pw_b': out['v_w_pw_b'], 'v_b_pw_b': out['v_b_pw_b'], 'v_w_o': out['v_w_o'], 'v_g_ffn': out['v_g_ffn'], 'v_w_gate': out['v_w_gate'], 'v_w_up': out['v_w_up'], 'v_w_down': out['v_w_down'], 'v_g_ple': out['v_g_ple'], 'v_w_ple_gate': out['v_w_ple_gate'], 'v_w_ple_proj': out['v_w_ple_proj'], 'v_g_final': out['v_g_final']}


def _loss(weights, diff, rest, loss_target):
    with _jax.named_scope("forward"):
        args = {**rest, TWIN_DIFF_INPUT: diff, **{k: w.astype(_WEIGHT_DTYPES[k]) for k, w in weights.items()}}
        y = _forward(args)
    with _jax.named_scope("loss_head"):
        err = _jnp.square(y.astype(_jnp.float32) - loss_target)
        return 0.5 * _jnp.sum(_jnp.mean(err, axis=-1)) if err.ndim else 0.5 * err


def _adamw(w, g, m, v):
    m = ADAM_B1 * m + (1.0 - ADAM_B1) * g
    v = ADAM_B2 * v + (1.0 - ADAM_B2) * _jnp.square(g)
    m_hat = m / (1.0 - ADAM_B1 ** ADAM_STEP)
    v_hat = v / (1.0 - ADAM_B2 ** ADAM_STEP)
    delta = -ADAM_LR * (m_hat / (_jnp.sqrt(v_hat) + ADAM_EPS) + ADAM_WD * w)
    return delta, m, v


def reference(x, p, g_mix, w_in, conv_a_w, w_out_a, b_glu, conf_dw_w, conf_dw_b, conf_ln_g, conf_ln_b, w_pw_b, b_pw_b, w_o, g_ffn, w_gate, w_up, w_down, g_ple, w_ple_gate, w_ple_proj, g_final, loss_target, m_g_mix, m_w_in, m_conv_a_w, m_w_out_a, m_b_glu, m_conf_dw_w, m_conf_dw_b, m_conf_ln_g, m_conf_ln_b, m_w_pw_b, m_b_pw_b, m_w_o, m_g_ffn, m_w_gate, m_w_up, m_w_down, m_g_ple, m_w_ple_gate, m_w_ple_proj, m_g_final, v_g_mix, v_w_in, v_conv_a_w, v_w_out_a, v_b_glu, v_conf_dw_w, v_conf_dw_b, v_conf_ln_g, v_conf_ln_b, v_w_pw_b, v_b_pw_b, v_w_o, v_g_ffn, v_w_gate, v_w_up, v_w_down, v_g_ple, v_w_ple_gate, v_w_ple_proj, v_g_final):
    given = dict(x=x, p=p, g_mix=g_mix, w_in=w_in, conv_a_w=conv_a_w, w_out_a=w_out_a, b_glu=b_glu, conf_dw_w=conf_dw_w, conf_dw_b=conf_dw_b, conf_ln_g=conf_ln_g, conf_ln_b=conf_ln_b, w_pw_b=w_pw_b, b_pw_b=b_pw_b, w_o=w_o, g_ffn=g_ffn, w_gate=w_gate, w_up=w_up, w_down=w_down, g_ple=g_ple, w_ple_gate=w_ple_gate, w_ple_proj=w_ple_proj, g_final=g_final, loss_target=loss_target, m_g_mix=m_g_mix, m_w_in=m_w_in, m_conv_a_w=m_conv_a_w, m_w_out_a=m_w_out_a, m_b_glu=m_b_glu, m_conf_dw_w=m_conf_dw_w, m_conf_dw_b=m_conf_dw_b, m_conf_ln_g=m_conf_ln_g, m_conf_ln_b=m_conf_ln_b, m_w_pw_b=m_w_pw_b, m_b_pw_b=m_b_pw_b, m_w_o=m_w_o, m_g_ffn=m_g_ffn, m_w_gate=m_w_gate, m_w_up=m_w_up, m_w_down=m_w_down, m_g_ple=m_g_ple, m_w_ple_gate=m_w_ple_gate, m_w_ple_proj=m_w_ple_proj, m_g_final=m_g_final, v_g_mix=v_g_mix, v_w_in=v_w_in, v_conv_a_w=v_conv_a_w, v_w_out_a=v_w_out_a, v_b_glu=v_b_glu, v_conf_dw_w=v_conf_dw_w, v_conf_dw_b=v_conf_dw_b, v_conf_ln_g=v_conf_ln_g, v_conf_ln_b=v_conf_ln_b, v_w_pw_b=v_w_pw_b, v_b_pw_b=v_b_pw_b, v_w_o=v_w_o, v_g_ffn=v_g_ffn, v_w_gate=v_w_gate, v_w_up=v_w_up, v_w_down=v_w_down, v_g_ple=v_g_ple, v_w_ple_gate=v_w_ple_gate, v_w_ple_proj=v_w_ple_proj, v_g_final=v_g_final)
    weights = {n: given[n] for n in TWIN_WEIGHTS}
    shared = {n: given[n] for n in SHARED_INPUTS}
    per_example = {n: given[n] for n in ['x', 'p']}
    grad_fn = _jax.value_and_grad(_loss, argnums=(0, 1))

    def one_microbatch(ex, loss_target):
        ex = dict(ex)
        diff = ex.pop(TWIN_DIFF_INPUT)
        return grad_fn(weights, diff, {**shared, **ex}, loss_target)

    if N_MICROBATCH == 1:
        loss, (grad_w, grad_x) = one_microbatch(per_example, given["loss_target"])
    else:
        def body(carry, xs):
            loss_sum, grad_sum = carry
            l_k, (gw_k, gx_k) = one_microbatch(xs[0], xs[1])
            with _jax.named_scope("update"):
                return (loss_sum + l_k, _jax.tree.map(_jnp.add, grad_sum, gw_k)), gx_k

        init = (_jnp.zeros((), _jnp.float32), _jax.tree.map(_jnp.zeros_like, weights))
        (loss, grad_w), grad_x = _jax.lax.scan(body, init, (per_example, given["loss_target"]))
    with _jax.named_scope("update"):
        delta_w, new_m, new_v = {}, {}, {}
        for n in TWIN_WEIGHTS:
            delta_w[n], new_m[n], new_v[n] = _adamw(weights[n], grad_w[n], given["m_" + n], given["v_" + n])
    return (loss, grad_x, *[grad_w[n] for n in TWIN_WEIGHTS], *[delta_w[n] for n in TWIN_WEIGHTS],
            *[new_m[n] for n in TWIN_WEIGHTS], *[new_v[n] for n in TWIN_WEIGHTS])
```

```python
import jax
import jax.numpy as jnp
from jax import lax
from jax.experimental import pallas as pl
from jax.experimental.pallas import tpu as pltpu

F32, BF16 = jnp.float32, jnp.bfloat16
EPS, LN_EPS = 1e-6, 1e-5
ADAM_LR, ADAM_B1, ADAM_B2, ADAM_EPS, ADAM_WD, ADAM_STEP = 0.001, 0.9, 0.999, 1e-08, 0.01, 10
CONV_A_K, CONF_K = 3, 31
NDEV = 8
NN = (((1,), (0,)), ((), ()))
NT = (((1,), (1,)), ((), ()))
TN = (((0,), (0,)), ((), ()))
V7X_VMEM_LIMIT_BYTES = 56 * 1024 * 1024
MESH = pl.DeviceIdType.MESH
SDS = jax.ShapeDtypeStruct
HALO_A, HALO_B = 16, 32
CONV_ROWS = 32


def _tile(n, pref):
    t = min(n, pref)
    while n % t:
        t -= 8
    return t


def _sigmoid(x):
    return jax.nn.sigmoid(x)


def _params(sem=None):
    return pltpu.CompilerParams(vmem_limit_bytes=V7X_VMEM_LIMIT_BYTES, dimension_semantics=sem)


def _fmm(name, grid, operands, terms, acc_shapes, extras, outs, epilogue):
    n_p, n_e, n_o, n_a = len(operands), len(extras), len(outs), len(acc_shapes)
    nk = grid[-1]
    kax = len(grid) - 1
    simple = nk == 1 and all(t[4] is None for t in terms)

    def dot(refs, term):
        a = refs[term[0]][...]
        b = refs[term[1]][...]
        if a.dtype != BF16:
            a = a.astype(BF16)
        if b.dtype != BF16:
            b = b.astype(BF16)
        return lax.dot_general(a, b, term[2], preferred_element_type=F32)

    def body(*refs):
        ex = refs[n_p : n_p + n_e]
        os_ = refs[n_p + n_e : n_p + n_e + n_o]
        accs = refs[n_p + n_e + n_o :]
        if simple:
            parts = [None] * n_a
            for term in terms:
                d = dot(refs, term)
                parts[term[3]] = d if parts[term[3]] is None else parts[term[3]] + d
            epilogue(parts, ex, os_)
            return
        ids = [pl.program_id(ax) for ax in range(len(grid))]
        k = ids[kax]

        @pl.when(k == 0)
        def _():
            for acc in accs:
                acc[...] = jnp.zeros(acc.shape, F32)

        for term in terms:

            def add(term=term):
                accs[term[3]][...] += dot(refs, term)

            if term[4] is None:
                add()
            else:
                pl.when(term[4](ids))(add)

        @pl.when(k == nk - 1)
        def _():
            epilogue([acc[...] for acc in accs], ex, os_)

    res = pl.pallas_call(
        body,
        name=name,
        grid=grid,
        in_specs=[o[1] for o in operands] + [e[1] for e in extras],
        out_specs=[o[1] for o in outs],
        out_shape=[o[0] for o in outs],
        scratch_shapes=[] if simple else [pltpu.VMEM(s, F32) for s in acc_shapes],
        compiler_params=_params(("parallel",) * kax + ("arbitrary",)),
    )(*[o[0] for o in operands], *[e[0] for e in extras])
    return res


def _rms_bwd(dn_raw, h, g):
    r = lax.rsqrt(jnp.mean(h * h, axis=-1, keepdims=True) + EPS)
    hn = h * r
    dg = jnp.sum(dn_raw * hn, axis=0, keepdims=True)
    dn = dn_raw * g
    dh = r * (dn - hn * jnp.mean(dn * hn, axis=-1, keepdims=True))
    return dh, dg


def _rms_fwd(name, h, g):
    s, d = h.shape
    ts = _tile(s, 512)

    def body(h_ref, g_ref, o_ref):
        x = h_ref[...]
        r = lax.rsqrt(jnp.mean(x * x, axis=-1, keepdims=True) + EPS)
        o_ref[...] = (x * r * g_ref[...]).astype(BF16)

    return pl.pallas_call(
        body,
        name=name,
        grid=(s // ts,),
        in_specs=[pl.BlockSpec((ts, d), lambda i: (i, 0)), pl.BlockSpec((1, d), lambda i: (0, 0))],
        out_specs=pl.BlockSpec((ts, d), lambda i: (i, 0)),
        out_shape=SDS((s, d), BF16),
        compiler_params=_params(("parallel",)),
    )(h, g)


def _prev_halo(ts, hb):
    r = ts // hb
    return lambda i: jnp.maximum(i * r - 1, 0)


def _next_halo(ts, hb, s):
    r = ts // hb
    last = s // hb - 1
    return lambda i: jnp.minimum((i + 1) * r, last)


def _mix_a_fwd(proj, wa, s, c):
    ts, hb = _tile(s, 256), HALO_A
    prev = _prev_halo(ts, hb)

    def body(ah, ab, ac, hh, hc, w, o, buf):
        i = pl.program_id(0)
        zh = hc[...].astype(F32) * hh[...].astype(F32)
        buf[pl.ds(0, hb), :] = jnp.where(i == 0, 0.0, zh)
        buf[pl.ds(hb, ts), :] = ac[...].astype(F32) * ah[...].astype(F32)
        for r0 in range(0, ts, CONV_ROWS):
            cz = jnp.zeros((CONV_ROWS, c), F32)
            for k in range(CONV_A_K):
                cz = cz + w[k : k + 1, :] * buf[pl.ds(hb + r0 - (CONV_A_K - 1) + k, CONV_ROWS), :]
            o[pl.ds(r0, CONV_ROWS), :] = (ab[pl.ds(r0, CONV_ROWS), :].astype(F32) * cz).astype(BF16)

    main = lambda cb: pl.BlockSpec((ts, c), lambda i: (i, cb))
    halo = lambda cb: pl.BlockSpec((hb, c), lambda i: (prev(i), cb))
    return pl.pallas_call(
        body,
        name="mix_a_fwd",
        grid=(s // ts,),
        in_specs=[main(0), main(1), main(2), halo(0), halo(2), pl.BlockSpec(wa.shape, lambda i: (0, 0))],
        out_specs=pl.BlockSpec((ts, c), lambda i: (i, 0)),
        out_shape=SDS((s, c), BF16),
        scratch_shapes=[pltpu.VMEM((hb + ts, c), F32)],
        compiler_params=_params(("parallel",)),
    )(proj, proj, proj, proj, proj, wa)


def _mix_b_fwd(proj, b_glu, wd, bd, lg, lb, s, c):
    ts, hb = _tile(s, 256), HALO_B
    prev = _prev_halo(ts, hb)

    def body(gv, gg, hv, hg, bglu, w, bd_r, lg_r, lb_r, v_o, u_o, cv_o, buf):
        i = pl.program_id(0)
        bv, bg = bglu[:, 0:c], bglu[:, c : 2 * c]
        uh = (hv[...].astype(F32) + bv) * _sigmoid(hg[...].astype(F32) + bg)
        buf[pl.ds(0, hb), :] = jnp.where(i == 0, 0.0, uh)
        u = (gv[...].astype(F32) + bv) * _sigmoid(gg[...].astype(F32) + bg)
        buf[pl.ds(hb, ts), :] = u
        u_o[...] = u.astype(BF16)
        for r0 in range(0, ts, CONV_ROWS):
            acc = jnp.zeros((CONV_ROWS, c), F32)
            for k in range(CONF_K):
                acc = acc + w[k : k + 1, :] * buf[pl.ds(hb + r0 - (CONF_K - 1) + k, CONV_ROWS), :]
            cv_o[pl.ds(r0, CONV_ROWS), :] = acc + bd_r[...]
        cv = cv_o[...]
        mu = jnp.mean(cv, axis=-1, keepdims=True)
        xc = cv - mu
        rs = lax.rsqrt(jnp.mean(xc * xc, axis=-1, keepdims=True) + LN_EPS)
        ln = xc * rs * lg_r[...] + lb_r[...]
        v_o[...] = (ln * _sigmoid(ln)).astype(BF16)

    main = lambda cb: pl.BlockSpec((ts, c), lambda i: (i, cb))
    halo = lambda cb: pl.BlockSpec((hb, c), lambda i: (prev(i), cb))
    full = lambda a: pl.BlockSpec(a.shape, lambda i: (0, 0))
    out = pl.BlockSpec((ts, c), lambda i: (i, 0))
    return pl.pallas_call(
        body,
        name="mix_b_fwd",
        grid=(s // ts,),
        in_specs=[main(3), main(4), halo(3), halo(4), full(b_glu), full(wd), full(bd), full(lg), full(lb)],
        out_specs=[out, out, out],
        out_shape=[SDS((s, c), BF16), SDS((s, c), BF16), SDS((s, c), F32)],
        scratch_shapes=[pltpu.VMEM((hb + ts, c), F32)],
        compiler_params=_params(("parallel",)),
    )(proj, proj, proj, proj, b_glu, wd, bd, lg, lb)


def _mix_b_bwd1(d_v, cv, lg, lb, s, c):
    ts = _tile(s, 256)

    def body(dv_r, cv_r, lg_r, lb_r, dcv_o, part_o):
        cv_ = cv_r[...]
        mu = jnp.mean(cv_, axis=-1, keepdims=True)
        xc = cv_ - mu
        rs = lax.rsqrt(jnp.mean(xc * xc, axis=-1, keepdims=True) + LN_EPS)
        xh = xc * rs
        ln = xh * lg_r[...] + lb_r[...]
        sg = _sigmoid(ln)
        d_ln = dv_r[...].astype(F32) * (sg * (1.0 + ln * (1.0 - sg)))
        dy = d_ln * lg_r[...]
        d_cv = rs * (dy - jnp.mean(dy, axis=-1, keepdims=True) - xh * jnp.mean(dy * xh, axis=-1, keepdims=True))
        dcv_o[...] = d_cv
        part_o[0:1, :] = jnp.sum(d_ln * xh, axis=0, keepdims=True)
        part_o[1:2, :] = jnp.sum(d_ln, axis=0, keepdims=True)
        part_o[2:3, :] = jnp.sum(d_cv, axis=0, keepdims=True)

    blk = pl.BlockSpec((ts, c), lambda i: (i, 0))
    full = lambda a: pl.BlockSpec(a.shape, lambda i: (0, 0))
    return pl.pallas_call(
        body,
        name="mix_b_bwd_ln",
        grid=(s // ts,),
        in_specs=[blk, blk, full(lg), full(lb)],
        out_specs=[blk, pl.BlockSpec((None, 3, c), lambda i: (i, 0, 0))],
        out_shape=[SDS((s, c), F32), SDS((s // ts, 3, c), F32)],
        compiler_params=_params(("parallel",)),
    )(d_v, cv, lg, lb)


def _mix_b_bwd2(d_cv, u, proj, b_glu, wd, s, c):
    ts, hb = _tile(s, 256), HALO_B
    prev, nxt = _prev_halo(ts, hb), _next_halo(ts, hb, s)
    n_t = s // ts
    kp = wd.shape[0]

    def body(dcv, dcv_n, u_m, u_p, gv, gg, bglu, w, d_o, dwd_o, dbglu_o, dbuf, ubuf, dub):
        i = pl.program_id(0)
        dbuf[pl.ds(0, ts), :] = dcv[...]
        dbuf[pl.ds(ts, hb), :] = jnp.where(i == n_t - 1, 0.0, dcv_n[...])
        ubuf[pl.ds(0, hb), :] = jnp.where(i == 0, 0.0, u_p[...].astype(F32))
        ubuf[pl.ds(hb, ts), :] = u_m[...].astype(F32)
        dw_rows = [jnp.zeros((1, c), F32) for _ in range(CONF_K)]
        for r0 in range(0, ts, CONV_ROWS):
            acc = jnp.zeros((CONV_ROWS, c), F32)
            dc = dbuf[pl.ds(r0, CONV_ROWS), :]
            for k in range(CONF_K):
                acc = acc + w[k : k + 1, :] * dbuf[pl.ds(r0 + (CONF_K - 1) - k, CONV_ROWS), :]
                uk = ubuf[pl.ds(hb + r0 - (CONF_K - 1) + k, CONV_ROWS), :]
                dw_rows[k] = dw_rows[k] + jnp.sum(dc * uk, axis=0, keepdims=True)
            dub[pl.ds(r0, CONV_ROWS), :] = acc
        for k in range(CONF_K):
            dwd_o[k : k + 1, :] = dw_rows[k]
        dwd_o[CONF_K:kp, :] = jnp.zeros((kp - CONF_K, c), F32)
        bv, bg = bglu[:, 0:c], bglu[:, c : 2 * c]
        d_u = dub[...]
        sg = _sigmoid(gg[...].astype(F32) + bg)
        d_gv = d_u * sg
        d_gg = d_u * (gv[...].astype(F32) + bv) * sg * (1.0 - sg)
        d_o[:, 0:c] = d_gv.astype(BF16)
        d_o[:, c : 2 * c] = d_gg.astype(BF16)
        dbglu_o[:, 0:c] = jnp.sum(d_gv, axis=0, keepdims=True)
        dbglu_o[:, c : 2 * c] = jnp.sum(d_gg, axis=0, keepdims=True)

    blk = lambda cb: pl.BlockSpec((ts, c), lambda i: (i, cb))
    full = lambda a: pl.BlockSpec(a.shape, lambda i: (0, 0))
    return pl.pallas_call(
        body,
        name="mix_b_bwd_conv",
        grid=(n_t,),
        in_specs=[
            blk(0),
            pl.BlockSpec((hb, c), lambda i: (nxt(i), 0)),
            blk(0),
            pl.BlockSpec((hb, c), lambda i: (prev(i), 0)),
            blk(3),
            blk(4),
            full(b_glu),
            full(wd),
        ],
        out_specs=[
            pl.BlockSpec((ts, 2 * c), lambda i: (i, 0)),
            pl.BlockSpec((None, kp, c), lambda i: (i, 0, 0)),
            pl.BlockSpec((None, 1, 2 * c), lambda i: (i, 0, 0)),
        ],
        out_shape=[SDS((s, 2 * c), BF16), SDS((n_t, kp, c), F32), SDS((n_t, 1, 2 * c), F32)],
        scratch_shapes=[pltpu.VMEM((ts + hb, c), F32), pltpu.VMEM((hb + ts, c), F32), pltpu.VMEM((ts, c), F32)],
        compiler_params=_params(("parallel",)),
    )(d_cv, d_cv, u, u, proj, proj, b_glu, wd)


def _mix_a_bwd(d_ya, proj, wa, s, c):
    ts, hb = _tile(s, 256), HALO_A
    prev, nxt = _prev_halo(ts, hb), _next_halo(ts, hb, s)
    n_t = s // ts
    kp = wa.shape[0]

    def body(dya, dya_n, ah, ab, ac, ah_p, ac_p, ab_n, w, d_o, dwa_o, zbuf, dbuf, dzb):
        i = pl.program_id(0)
        zbuf[pl.ds(0, hb), :] = jnp.where(i == 0, 0.0, ac_p[...].astype(F32) * ah_p[...].astype(F32))
        zbuf[pl.ds(hb, ts), :] = ac[...].astype(F32) * ah[...].astype(F32)
        dbuf[pl.ds(0, ts), :] = dya[...].astype(F32) * ab[...].astype(F32)
        dbuf[pl.ds(ts, hb), :] = jnp.where(i == n_t - 1, 0.0, dya_n[...].astype(F32) * ab_n[...].astype(F32))
        dw_rows = [jnp.zeros((1, c), F32) for _ in range(CONV_A_K)]
        for r0 in range(0, ts, CONV_ROWS):
            cz = jnp.zeros((CONV_ROWS, c), F32)
            dz = jnp.zeros((CONV_ROWS, c), F32)
            dc = dbuf[pl.ds(r0, CONV_ROWS), :]
            for k in range(CONV_A_K):
                zk = zbuf[pl.ds(hb + r0 - (CONV_A_K - 1) + k, CONV_ROWS), :]
                cz = cz + w[k : k + 1, :] * zk
                dz = dz + w[k : k + 1, :] * dbuf[pl.ds(r0 + (CONV_A_K - 1) - k, CONV_ROWS), :]
                dw_rows[k] = dw_rows[k] + jnp.sum(dc * zk, axis=0, keepdims=True)
            d_o[pl.ds(r0, CONV_ROWS), c : 2 * c] = (dya[pl.ds(r0, CONV_ROWS), :].astype(F32) * cz).astype(BF16)
            dzb[pl.ds(r0, CONV_ROWS), :] = dz
        d_z = dzb[...]
        d_o[:, 0:c] = (d_z * ac[...].astype(F32)).astype(BF16)
        d_o[:, 2 * c : 3 * c] = (d_z * ah[...].astype(F32)).astype(BF16)
        for k in range(CONV_A_K):
            dwa_o[k : k + 1, :] = dw_rows[k]
        dwa_o[CONV_A_K:kp, :] = jnp.zeros((kp - CONV_A_K, c), F32)

    blk = lambda cb: pl.BlockSpec((ts, c), lambda i: (i, cb))
    hp = lambda cb: pl.BlockSpec((hb, c), lambda i: (prev(i), cb))
    hn = lambda cb: pl.BlockSpec((hb, c), lambda i: (nxt(i), cb))
    return pl.pallas_call(
        body,
        name="mix_a_bwd",
        grid=(n_t,),
        in_specs=[blk(0), hn(0), blk(0), blk(1), blk(2), hp(0), hp(2), hn(1), pl.BlockSpec(wa.shape, lambda i: (0, 0))],
        out_specs=[pl.BlockSpec((ts, 3 * c), lambda i: (i, 0)), pl.BlockSpec((None, kp, c), lambda i: (i, 0, 0))],
        out_shape=[SDS((s, 3 * c), BF16), SDS((n_t, kp, c), F32)],
        scratch_shapes=[pltpu.VMEM((hb + ts, c), F32), pltpu.VMEM((ts + hb, c), F32), pltpu.VMEM((ts, c), F32)],
        compiler_params=_params(("parallel",)),
    )(d_ya, d_ya, proj, proj, proj, proj, proj, proj, wa)


def _ep_bf16(accs, ex, os_):
    os_[0][...] = accs[0].astype(BF16)


def _mm_tn(name, a, b, tm=2048, tn=1024, tk=1024):
    m, k1 = a.shape
    n = b.shape[1]
    tm, tn, tk = _tile(k1, tm), _tile(n, tn), _tile(m, tk)
    return _fmm(
        name,
        (k1 // tm, n // tn, m // tk),
        [(a, pl.BlockSpec((tk, tm), lambda i, j, k: (k, i))), (b, pl.BlockSpec((tk, tn), lambda i, j, k: (k, j)))],
        [(0, 1, TN, 0, None)],
        [(tm, tn)],
        [],
        [(SDS((k1, n), BF16), pl.BlockSpec((tm, tn), lambda i, j, k: (i, j)))],
        _ep_bf16,
    )[0]


def _mm_nt(name, a, b, tm=1024, tn=1024):
    m, kk = a.shape
    n = b.shape[0]
    tm, tn = _tile(m, tm), _tile(n, tn)
    return _fmm(
        name,
        (m // tm, n // tn, 1),
        [(a, pl.BlockSpec((tm, kk), lambda i, j, k: (i, 0))), (b, pl.BlockSpec((tn, kk), lambda i, j, k: (j, 0)))],
        [(0, 1, NT, 0, None)],
        [(tm, tn)],
        [],
        [(SDS((m, n), BF16), pl.BlockSpec((tm, tn), lambda i, j, k: (i, j)))],
        _ep_bf16,
    )[0]


def _dev_index(dev):
    return 4 * dev[0] + 2 * dev[1] + dev[2]


def _region(ref, kind, j, shard_shape):
    if kind == "col":
        ns = shard_shape[1]
        return ref.at[:, pl.ds(pl.multiple_of(j * ns, 128), ns)]
    if kind == "row":
        rs = shard_shape[0]
        return ref.at[pl.ds(pl.multiple_of(j * rs, 8), rs), :]
    return ref.at[j]


def _whole_shape(kind, shard_shape):
    if kind == "col":
        return (shard_shape[0], NDEV * shard_shape[1])
    if kind == "row":
        return (NDEV * shard_shape[0], shard_shape[1])
    return (NDEV,) + tuple(shard_shape)


def _place():
    return lax.axis_index("x"), lax.axis_index("y"), lax.axis_index("c")


def _all_gather(shards, kinds):
    n_t = len(shards)
    shapes = [tuple(sh.shape) for sh in shards]

    def body(*refs):
        srcs, dsts = refs[:n_t], refs[n_t : 2 * n_t]
        send_sems, recv_sems, local_sems = refs[2 * n_t :]
        x, y, c = _place()
        me, sib = (x, y, c), (x, y, 1 - c)
        chips = [(1 - x, y), (x, 1 - y), (1 - x, 1 - y)]

        def reg(t, dev):
            return _region(dsts[t], kinds[t], _dev_index(dev), shapes[t])

        def copy(t, k, block, to, src=None):
            return pltpu.make_async_remote_copy(
                src_ref=reg(t, block) if src is None else src,
                dst_ref=reg(t, block),
                send_sem=send_sems.at[t, k],
                recv_sem=recv_sems.at[t, k],
                device_id=to,
                device_id_type=MESH,
            )

        mine = [pltpu.make_async_copy(srcs[t], reg(t, me), local_sems.at[t]) for t in range(n_t)]
        first = []
        for t in range(n_t):
            mine[t].start()
            first.append(copy(t, 0, me, sib, src=srcs[t]))
            first += [copy(t, 1 + j, me, (*chip, c), src=srcs[t]) for j, chip in enumerate(chips)]
        for cp in first:
            cp.start()
        passed = []
        for t in range(n_t):
            for j, chip in enumerate(chips):
                copy(t, 1 + j, (*chip, c), me).wait_recv()
                p = copy(t, 4 + j, (*chip, c), sib)
                p.start()
                passed.append(p)
        for t in range(n_t):
            copy(t, 0, sib, me).wait_recv()
            for j, chip in enumerate(chips):
                copy(t, 4 + j, (*chip, 1 - c), me).wait_recv()
        for cp in first + passed:
            cp.wait_send()
        for cp in mine:
            cp.wait()

    hbm = pl.BlockSpec(memory_space=pltpu.HBM)
    return pl.pallas_call(
        body,
        name="all_gather_weights",
        in_specs=[hbm] * n_t,
        out_specs=[hbm] * n_t,
        out_shape=[SDS(_whole_shape(kinds[t], shapes[t]), shards[t].dtype) for t in range(n_t)],
        scratch_shapes=[
            pltpu.SemaphoreType.DMA((n_t, 7)),
            pltpu.SemaphoreType.DMA((n_t, 7)),
            pltpu.SemaphoreType.DMA((n_t,)),
        ],
    )(*shards)


def _peer(me, r):
    x, y, c = me
    return (1 - x if r & 4 else x, 1 - y if r & 2 else y, 1 - c if r & 1 else c)


def _exchange_grads(wholes, kinds, shard_shapes):
    n_t = len(wholes)

    def body(*refs):
        srcs, lands = refs[:n_t], refs[n_t : 2 * n_t]
        send_sems, recv_sems, local_sems = refs[2 * n_t :]
        me = _place()
        my = _dev_index(me)

        def copy(t, r):
            peer = _peer(me, r)
            return pltpu.make_async_remote_copy(
                src_ref=_region(srcs[t], kinds[t], _dev_index(peer), shard_shapes[t]),
                dst_ref=lands[t].at[my],
                send_sem=send_sems.at[t, r - 1],
                recv_sem=recv_sems.at[t, r - 1],
                device_id=peer,
                device_id_type=MESH,
            )

        def arrival(t, r):
            peer = _peer(me, r)
            slab = lands[t].at[_dev_index(peer)]
            return pltpu.make_async_remote_copy(
                src_ref=slab,
                dst_ref=slab,
                send_sem=send_sems.at[t, r - 1],
                recv_sem=recv_sems.at[t, r - 1],
                device_id=peer,
                device_id_type=MESH,
            )

        mine = [
            pltpu.make_async_copy(_region(srcs[t], kinds[t], my, shard_shapes[t]), lands[t].at[my], local_sems.at[t])
            for t in range(n_t)
        ]
        sends = [copy(t, r) for t in range(n_t) for r in range(1, NDEV)]
        for cp in mine + sends:
            cp.start()
        for t in range(n_t):
            for r in range(1, NDEV):
                arrival(t, r).wait_recv()
        for cp in sends:
            cp.wait_send()
        for cp in mine:
            cp.wait()

    hbm = pl.BlockSpec(memory_space=pltpu.HBM)
    return pl.pallas_call(
        body,
        name="exchange_grads",
        in_specs=[hbm] * n_t,
        out_specs=[hbm] * n_t,
        out_shape=[SDS((NDEV,) + tuple(shard_shapes[t]), wholes[t].dtype) for t in range(n_t)],
        scratch_shapes=[
            pltpu.SemaphoreType.DMA((n_t, 7)),
            pltpu.SemaphoreType.DMA((n_t, 7)),
            pltpu.SemaphoreType.DMA((n_t,)),
        ],
    )(*wholes)


def _all_reduce_small(part):
    r_, c_ = part.shape

    def body(src, land, total, send_sems, recv_sems):
        me = _place()
        my = _dev_index(me)
        land[my] = src[...]

        def copy(r):
            peer = _peer(me, r)
            return pltpu.make_async_remote_copy(
                src_ref=src,
                dst_ref=land.at[my],
                send_sem=send_sems.at[r - 1],
                recv_sem=recv_sems.at[r - 1],
                device_id=peer,
                device_id_type=MESH,
            )

        def arrival(r):
            peer = _peer(me, r)
            slab = land.at[_dev_index(peer)]
            return pltpu.make_async_remote_copy(
                src_ref=slab,
                dst_ref=slab,
                send_sem=send_sems.at[r - 1],
                recv_sem=recv_sems.at[r - 1],
                device_id=peer,
                device_id_type=MESH,
            )

        sends = [copy(r) for r in range(1, NDEV)]
        for cp in sends:
            cp.start()
        for r in range(1, NDEV):
            arrival(r).wait_recv()
        for cp in sends:
            cp.wait_send()
        acc = land[0]
        for d in range(1, NDEV):
            acc = acc + land[d]
        total[...] = acc

    vmem = pl.BlockSpec(memory_space=pltpu.VMEM)
    return pl.pallas_call(
        body,
        name="all_reduce_small",
        in_specs=[vmem],
        out_specs=[vmem, vmem],
        out_shape=[SDS((NDEV, r_, c_), F32), SDS((r_, c_), F32)],
        scratch_shapes=[pltpu.SemaphoreType.DMA((7,)), pltpu.SemaphoreType.DMA((7,))],
    )(part)[1]


def _adamw_math(g, w, m, v):
    m2 = ADAM_B1 * m + (1.0 - ADAM_B1) * g
    v2 = ADAM_B2 * v + (1.0 - ADAM_B2) * (g * g)
    m_hat = m2 / (1.0 - ADAM_B1**ADAM_STEP)
    v_hat = v2 / (1.0 - ADAM_B2**ADAM_STEP)
    delta = -ADAM_LR * (m_hat / (jnp.sqrt(v_hat) + ADAM_EPS) + ADAM_WD * w)
    return delta, m2, v2


def _adamw_big(name, land, w, m, v):
    rows, cols = w.shape
    tr = _tile(rows, 256)

    def body(l_ref, w_ref, m_ref, v_ref, g_o, d_o, m_o, v_o):
        g = l_ref[0].astype(F32)
        for d in range(1, NDEV):
            g = g + l_ref[d].astype(F32)
        delta, m2, v2 = _adamw_math(g, w_ref[...], m_ref[...], v_ref[...])
        g_o[...] = g
        d_o[...] = delta
        m_o[...] = m2
        v_o[...] = v2

    blk = pl.BlockSpec((tr, cols), lambda i: (i, 0))
    return pl.pallas_call(
        body,
        name=name,
        grid=(rows // tr,),
        in_specs=[pl.BlockSpec((NDEV, tr, cols), lambda i: (0, i, 0)), blk, blk, blk],
        out_specs=[blk] * 4,
        out_shape=[SDS((rows, cols), F32)] * 4,
        compiler_params=_params(("parallel",)),
    )(land, w, m, v)


def _adamw_small(g, w, m, v):
    def body(g_ref, w_ref, m_ref, v_ref, d_o, m_o, v_o):
        delta, m2, v2 = _adamw_math(g_ref[...], w_ref[...], m_ref[...], v_ref[...])
        d_o[...] = delta
        m_o[...] = m2
        v_o[...] = v2

    vmem = pl.BlockSpec(memory_space=pltpu.VMEM)
    return pl.pallas_call(
        body,
        name="adamw_small",
        in_specs=[vmem] * 4,
        out_specs=[vmem] * 3,
        out_shape=[SDS(g.shape, F32)] * 3,
    )(g, w, m, v)


def _pack(pieces, width):
    flat = jnp.concatenate([p.reshape(-1) for p in pieces])
    rows = -(-flat.shape[0] // (8 * width)) * 8
    flat = jnp.pad(flat, (0, rows * width - flat.shape[0]))
    return flat.reshape(rows, width)


def _unpack(packed, shapes):
    flat = packed.reshape(-1)
    out, off = [], 0
    for shp in shapes:
        n = 1
        for d in shp:
            n *= d
        out.append(flat[off : off + n].reshape(shp))
        off += n
    return out


def kernel(x, p, g_mix, w_in, conv_a_w, w_out_a, b_glu, conf_dw_w, conf_dw_b, conf_ln_g, conf_ln_b, w_pw_b, b_pw_b, w_o, g_ffn, w_gate, w_up, w_down, g_ple, w_ple_gate, w_ple_proj, g_final, loss_target, m_g_mix, m_w_in, m_conv_a_w, m_w_out_a, m_b_glu, m_conf_dw_w, m_conf_dw_b, m_conf_ln_g, m_conf_ln_b, m_w_pw_b, m_b_pw_b, m_w_o, m_g_ffn, m_w_gate, m_w_up, m_w_down, m_g_ple, m_w_ple_gate, m_w_ple_proj, m_g_final, v_g_mix, v_w_in, v_conv_a_w, v_w_out_a, v_b_glu, v_conf_dw_w, v_conf_dw_b, v_conf_ln_g, v_conf_ln_b, v_w_pw_b, v_b_pw_b, v_w_o, v_g_ffn, v_w_gate, v_w_up, v_w_down, v_g_ple, v_w_ple_gate, v_w_ple_proj, v_g_final):
    s, d = x.shape[1], x.shape[2]
    c = conf_ln_g.shape[-1]
    pdim = w_ple_proj.shape[1]
    fs = w_gate.shape[-1]
    nin = NDEV * w_in.shape[-1]
    assert d == 2 * c and nin == 5 * c + 2 * d, (d, c, nin)
    x2, p2, tgt = x[0], p[0, 0], loss_target[0]
    gfin = g_final.reshape(1, d)

    kpa, kpb = 8, HALO_B
    wa_sh = jnp.pad(conv_a_w[0], ((0, kpa - CONV_A_K), (0, 0)))
    wd_sh = jnp.pad(conf_dw_w[0], ((0, kpb - CONF_K), (0, 0)))
    big = [w_in, w_out_a, w_pw_b, w_ple_proj, w_o, w_ple_gate, w_gate, w_up, w_down]
    big_kinds = ["col", "col", "col", "col", "row", "row", "blk", "blk", "blk"]
    shards = [w[0].astype(BF16) for w in big] + [wa_sh, wd_sh]
    gathered = _all_gather(shards, big_kinds + ["col", "col"])
    win, wouta, wpw, wpp, wo, wpg, wg, wu, wdn, wa, wd = gathered

    tm = _tile(s, 1024)
    tn = _tile(d, 1024)
    assert (5 * c) % tn == 0 and d % tn == 0 and c % tn == 0
    ga_blk, gb_blk = (5 * c) // tn, (5 * c + d) // tn
    ij = lambda i, j, k: (i, j)
    row_i = lambda i, j, k: (i, 0)

    n1 = _rms_fwd("rms1", x2, g_mix)
    proj = _fmm(
        "proj", (s // tm, nin // tn, 1),
        [(n1, pl.BlockSpec((tm, d), row_i)), (win, pl.BlockSpec((d, tn), lambda i, j, k: (0, j)))],
        [(0, 1, NN, 0, None)], [(tm, tn)], [],
        [(SDS((s, nin), BF16), pl.BlockSpec((tm, tn), ij))], _ep_bf16,
    )[0]
    ya_in = _mix_a_fwd(proj, wa, s, c)
    v_act, u_act, cv = _mix_b_fwd(proj, b_glu, wd, conf_dw_b, conf_ln_g, conf_ln_b, s, c)

    def ep_merge(accs, ex, os_):
        sa = _sigmoid(ex[0][...].astype(F32))
        sb = _sigmoid(ex[1][...].astype(F32))
        ya = accs[0]
        yb = accs[1] + ex[2][...]
        os_[0][...] = (sa * ya + sb * yb).astype(BF16)
        os_[1][...] = ya.astype(BF16)
        os_[2][...] = yb.astype(BF16)

    gate_a_spec = pl.BlockSpec((tm, tn), lambda i, j, k: (i, ga_blk + j))
    gate_b_spec = pl.BlockSpec((tm, tn), lambda i, j, k: (i, gb_blk + j))
    out_sd = (SDS((s, d), BF16), pl.BlockSpec((tm, tn), ij))
    m_act, ya, yb = _fmm(
        "merge", (s // tm, d // tn, 1),
        [(ya_in, pl.BlockSpec((tm, c), row_i)), (wouta, pl.BlockSpec((c, tn), lambda i, j, k: (0, j))),
         (v_act, pl.BlockSpec((tm, c), row_i)), (wpw, pl.BlockSpec((c, tn), lambda i, j, k: (0, j)))],
        [(0, 1, NN, 0, None), (2, 3, NN, 1, None)], [(tm, tn), (tm, tn)],
        [(proj, gate_a_spec), (proj, gate_b_spec), (b_pw_b, pl.BlockSpec((1, tn), lambda i, j, k: (0, j)))],
        [out_sd, out_sd, out_sd], ep_merge,
    )

    def ep_residual(accs, ex, os_):
        os_[0][...] = accs[0] + ex[0][...]

    h1 = _fmm(
        "w_o", (s // tm, d // tn, 1),
        [(m_act, pl.BlockSpec((tm, d), row_i)), (wo, pl.BlockSpec((d, tn), lambda i, j, k: (0, j)))],
        [(0, 1, NN, 0, None)], [(tm, tn)], [(x2, pl.BlockSpec((tm, tn), ij))],
        [(SDS((s, d), F32), pl.BlockSpec((tm, tn), ij))], ep_residual,
    )[0]
    n2 = _rms_fwd("rms2", h1, g_ffn)

    def ep_gateup(accs, ex, os_):
        g, u = accs
        os_[0][...] = g.astype(BF16)
        os_[1][...] = u.astype(BF16)
        os_[2][...] = (g * _sigmoid(g) * u).astype(BF16)

    ff_sd = (SDS((NDEV, s, fs), BF16), pl.BlockSpec((None, tm, fs), lambda i, j, k: (j, i, 0)))
    w_col_blk = pl.BlockSpec((None, d, fs), lambda i, j, k: (j, 0, 0))
    g_act, u_ff, f_act = _fmm(
        "gate_up", (s // tm, NDEV, 1),
        [(n2, pl.BlockSpec((tm, d), row_i)), (wg, w_col_blk), (wu, w_col_blk)],
        [(0, 1, NN, 0, None), (0, 2, NN, 1, None)], [(tm, fs), (tm, fs)], [],
        [ff_sd, ff_sd, ff_sd], ep_gateup,
    )
    h2 = _fmm(
        "down", (s // tm, d // tn, NDEV),
        [(f_act, pl.BlockSpec((None, tm, fs), lambda i, j, k: (k, i, 0))),
         (wdn, pl.BlockSpec((None, fs, tn), lambda i, j, k: (k, 0, j)))],
        [(0, 1, NN, 0, None)], [(tm, tn)], [(h1, pl.BlockSpec((tm, tn), ij))],
        [(SDS((s, d), F32), pl.BlockSpec((tm, tn), ij))], ep_residual,
    )[0]
    n3 = _rms_fwd("rms3", h2, g_ple)

    tr = _tile(s, 256)
    n_r = s // tr
    rows = lambda i, j, k: (i, 0)
    whole = lambda i, j, k: (0, 0)
    part_spec = lambda nrow: pl.BlockSpec((None, nrow, d), lambda i, j, k: (i, 0, 0))

    def ep_ple(accs, ex, os_):
        h2_, t_, gf = ex[0][...], ex[1][...], ex[2][...]
        ple = accs[0]
        s3 = _sigmoid(accs[1])
        h3 = h2_ + s3 * ple
        r = lax.rsqrt(jnp.mean(h3 * h3, axis=-1, keepdims=True) + EPS)
        hn = h3 * r
        e = hn * gf - t_
        loss = 0.5 * jnp.sum(jnp.mean(e * e, axis=-1, keepdims=True), axis=0, keepdims=True)
        dy = e * (1.0 / d)
        dn = dy * gf
        dh3 = r * (dn - hn * jnp.mean(dn * hn, axis=-1, keepdims=True))
        os_[0][...] = dh3
        os_[1][...] = (dh3 * s3).astype(BF16)
        os_[2][...] = (dh3 * ple * s3 * (1.0 - s3)).astype(BF16)
        os_[3][0:1, :] = jnp.sum(dy * hn, axis=0, keepdims=True)
        os_[3][1:2, :] = jnp.broadcast_to(loss, (1, d))

    dh3, d_ple, d_g3, part_fin = _fmm(
        "ple_loss", (n_r, 1, 1),
        [(p2, pl.BlockSpec((tr, pdim), rows)), (wpp, pl.BlockSpec((pdim, d), whole)),
         (n3, pl.BlockSpec((tr, d), rows)), (wpg, pl.BlockSpec((d, d), whole))],
        [(0, 1, NN, 0, None), (2, 3, NN, 1, None)], [(tr, d), (tr, d)],
        [(h2, pl.BlockSpec((tr, d), rows)), (tgt, pl.BlockSpec((tr, d), rows)), (gfin, pl.BlockSpec((1, d), whole))],
        [(SDS((s, d), F32), pl.BlockSpec((tr, d), rows)), (SDS((s, d), BF16), pl.BlockSpec((tr, d), rows)),
         (SDS((s, d), BF16), pl.BlockSpec((tr, d), rows)), (SDS((n_r, 2, d), F32), part_spec(2))],
        ep_ple,
    )

    g_wpp = _mm_tn("d_w_ple_proj", p2, d_ple)
    g_wpg = _mm_tn("d_w_ple_gate", n3, d_g3)

    def ep_norm_bwd(accs, ex, os_):
        dh, dg = _rms_bwd(accs[0], ex[0][...], ex[2][...])
        dh = ex[1][...] + dh
        os_[0][...] = dh
        os_[1][...] = dh.astype(BF16)
        os_[2][...] = dg

    norm_outs = lambda t: [
        (SDS((s, d), F32), pl.BlockSpec((t, d), rows)), (SDS((s, d), BF16), pl.BlockSpec((t, d), rows)),
        (SDS((s // t, 1, d), F32), part_spec(1)),
    ]
    dh2, dh2b, part_ple = _fmm(
        "d_n3", (n_r, 1, 1),
        [(d_g3, pl.BlockSpec((tr, d), rows)), (wpg, pl.BlockSpec((d, d), whole))],
        [(0, 1, NT, 0, None)], [(tr, d)],
        [(h2, pl.BlockSpec((tr, d), rows)), (dh3, pl.BlockSpec((tr, d), rows)), (g_ple, pl.BlockSpec((1, d), whole))],
        norm_outs(tr), ep_norm_bwd,
    )

    def ep_ddown(accs, ex, os_):
        g = ex[0][...].astype(F32)
        u = ex[1][...].astype(F32)
        sg = _sigmoid(g)
        df = accs[0]
        os_[0][...] = (df * u * sg * (1.0 + g * (1.0 - sg))).astype(BF16)
        os_[1][...] = (df * g * sg).astype(BF16)

    ff_in = pl.BlockSpec((None, tm, fs), lambda i, j, k: (j, i, 0))
    d_g, d_u = _fmm(
        "d_down", (s // tm, NDEV, 1),
        [(dh2b, pl.BlockSpec((tm, d), row_i)), (wdn, pl.BlockSpec((None, fs, d), lambda i, j, k: (j, 0, 0)))],
        [(0, 1, NT, 0, None)], [(tm, fs)], [(g_act, ff_in), (u_ff, ff_in)],
        [ff_sd, ff_sd], ep_ddown,
    )
    tk = _tile(s, 1024)
    g_wdn = _fmm(
        "d_w_down", (NDEV, 1, s // tk),
        [(f_act, pl.BlockSpec((None, tk, fs), lambda i, j, k: (i, k, 0))), (dh2b, pl.BlockSpec((tk, d), lambda i, j, k: (k, 0)))],
        [(0, 1, TN, 0, None)], [(fs, d)], [],
        [(SDS((NDEV, fs, d), BF16), pl.BlockSpec((None, fs, d), lambda i, j, k: (i, 0, 0)))], _ep_bf16,
    )[0]

    def ep_two_bf16(accs, ex, os_):
        os_[0][...] = accs[0].astype(BF16)
        os_[1][...] = accs[1].astype(BF16)

    ff_k = pl.BlockSpec((None, tk, fs), lambda i, j, k: (i, k, 0))
    wcol_sd = (SDS((NDEV, d, fs), BF16), pl.BlockSpec((None, d, fs), lambda i, j, k: (i, 0, 0)))
    g_wg, g_wu = _fmm(
        "d_w_gate_up", (NDEV, 1, s // tk),
        [(n2, pl.BlockSpec((tk, d), lambda i, j, k: (k, 0))), (d_g, ff_k), (d_u, ff_k)],
        [(0, 1, TN, 0, None), (0, 2, TN, 1, None)], [(d, fs), (d, fs)], [],
        [wcol_sd, wcol_sd], ep_two_bf16,
    )
    th = _tile(s, 256)
    ff_a = pl.BlockSpec((None, th, fs), lambda i, j, k: (k, i, 0))
    w_k = pl.BlockSpec((None, d, fs), lambda i, j, k: (k, 0, 0))
    dh1, dh1b, part_ffn = _fmm(
        "d_n2", (s // th, 1, NDEV),
        [(d_g, ff_a), (wg, w_k), (d_u, ff_a), (wu, w_k)],
        [(0, 1, NT, 0, None), (2, 3, NT, 0, None)], [(th, d)],
        [(h1, pl.BlockSpec((th, d), rows)), (dh2, pl.BlockSpec((th, d), rows)), (g_ffn, pl.BlockSpec((1, d), whole))],
        norm_outs(th), ep_norm_bwd,
    )
    g_wo = _mm_tn("d_w_o", m_act, dh1b)

    def ep_dm(accs, ex, os_):
        ya_, yb_ = ex[0][...].astype(F32), ex[1][...].astype(F32)
        sa = _sigmoid(ex[2][...].astype(F32))
        sb = _sigmoid(ex[3][...].astype(F32))
        dm = accs[0]
        d_yb = dm * sb
        os_[0][...] = (dm * sa).astype(BF16)
        os_[1][...] = d_yb.astype(BF16)
        os_[2][...] = (dm * ya_ * sa * (1.0 - sa)).astype(BF16)
        os_[3][...] = (dm * yb_ * sb * (1.0 - sb)).astype(BF16)
        os_[4][...] = jnp.sum(d_yb, axis=0, keepdims=True)

    tile_ij = pl.BlockSpec((tm, tn), ij)
    d_ya, d_yb, d_ga, d_gb, part_bpw = _fmm(
        "d_merge", (s // tm, d // tn, 1),
        [(dh1b, pl.BlockSpec((tm, d), row_i)), (wo, pl.BlockSpec((tn, d), lambda i, j, k: (j, 0)))],
        [(0, 1, NT, 0, None)], [(tm, tn)],
        [(ya, tile_ij), (yb, tile_ij), (proj, gate_a_spec), (proj, gate_b_spec)],
        [out_sd, out_sd, out_sd, out_sd,
         (SDS((s // tm, 1, d), F32), pl.BlockSpec((None, 1, tn), lambda i, j, k: (i, 0, j)))],
        ep_dm,
    )
    g_wouta = _mm_tn("d_w_out_a", ya_in, d_ya)
    g_wpw = _mm_tn("d_w_pw_b", v_act, d_yb)
    d_ya_in = _mm_nt("d_ya_in", d_ya, wouta)
    d_v = _mm_nt("d_v", d_yb, wpw)
    d_cv, part_ln = _mix_b_bwd1(d_v, cv, conf_ln_g, conf_ln_b, s, c)
    d_b, part_wd, part_bglu = _mix_b_bwd2(d_cv, u_act, proj, b_glu, wd, s, c)
    d_a, part_wa = _mix_a_bwd(d_ya_in, proj, wa, s, c)

    nb = nin // c
    gblk = d // c
    lo = [0, 3, 5, 5 + gblk]
    hi = [3, 5, 5 + gblk, 5 + 2 * gblk]
    pieces = [d_a, d_b, d_ga, d_gb]

    def active(q, ax):
        return lambda ids: jnp.logical_and(ids[ax] >= lo[q], ids[ax] < hi[q])

    def piece_spec(q, rows_, ax):
        def index(i, j, k):
            ids = (i, j, k)
            col = jnp.clip(ids[ax] - lo[q], 0, hi[q] - lo[q] - 1)
            row = i if ax == 2 else jnp.where(active(q, ax)(ids), k, 0)
            return (row, col)

        return pl.BlockSpec((rows_, c), index)

    tkw = _tile(s, 1024)
    g_win = _fmm(
        "d_w_in", (1, nb, s // tkw),
        [(n1, pl.BlockSpec((tkw, d), lambda i, j, k: (k, 0)))]
        + [(pieces[q], piece_spec(q, tkw, 1)) for q in range(4)],
        [(0, 1 + q, TN, 0, active(q, 1)) for q in range(4)], [(d, c)], [],
        [(SDS((d, nin), BF16), pl.BlockSpec((d, c), lambda i, j, k: (0, j)))], _ep_bf16,
    )[0]

    def ep_dx(accs, ex, os_):
        dh, dg = _rms_bwd(accs[0], ex[0][...], ex[2][...])
        os_[0][...] = ex[1][...] + dh
        os_[1][...] = dg

    dx, part_mix = _fmm(
        "d_n1", (s // th, 1, nb),
        [(pieces[q], piece_spec(q, th, 2)) for q in range(4)]
        + [(win, pl.BlockSpec((d, c), lambda i, j, k: (0, k)))],
        [(q, 4, NT, 0, active(q, 2)) for q in range(4)], [(th, d)],
        [(x2, pl.BlockSpec((th, d), rows)), (dh1, pl.BlockSpec((th, d), rows)), (g_mix, pl.BlockSpec((1, d), whole))],
        [(SDS((s, d), F32), pl.BlockSpec((th, d), rows)), (SDS((s // th, 1, d), F32), part_spec(1))],
        ep_dx,
    )

    small_parts = [
        jnp.sum(part_mix, axis=0),
        jnp.sum(part_bglu, axis=0),
        jnp.sum(part_ln[:, 2], axis=0),
        jnp.sum(part_ln[:, 0], axis=0),
        jnp.sum(part_ln[:, 1], axis=0),
        jnp.sum(part_bpw, axis=0),
        jnp.sum(part_ffn, axis=0),
        jnp.sum(part_ple, axis=0),
        jnp.sum(part_fin[:, 0], axis=0),
        jnp.sum(part_wa, axis=0),
        jnp.sum(part_wd, axis=0),
        jnp.broadcast_to(jnp.sum(part_fin[:, 1, 0]), (c,)),
    ]
    small_shapes = [(1, d), (1, 2 * c), (1, c), (1, c), (1, c), (1, d), (1, d), (1, d), (d,), (kpa, c), (kpb, c), (c,)]
    total = _all_reduce_small(_pack(small_parts, c))
    (gr_g_mix, gr_b_glu, gr_dw_b, gr_ln_g, gr_ln_b, gr_b_pw, gr_g_ffn, gr_g_ple, gr_g_final, gr_wa, gr_wd, loss_row) = _unpack(total, small_shapes)
    loss = loss_row[0]
    my = _dev_index(_place())
    csh = conv_a_w.shape[-1]
    gr_conv_a = lax.dynamic_slice_in_dim(gr_wa[:CONV_A_K], my * csh, csh, axis=1)[None]
    gr_conf_dw = lax.dynamic_slice_in_dim(gr_wd[:CONF_K], my * csh, csh, axis=1)[None]

    wholes = [g_win, g_wouta, g_wpw, g_wpp, g_wo, g_wpg, g_wg, g_wu, g_wdn]
    shard_shapes = [tuple(w.shape[1:]) for w in big]
    lands = _exchange_grads(wholes, big_kinds, shard_shapes)
    big_m = [m_w_in, m_w_out_a, m_w_pw_b, m_w_ple_proj, m_w_o, m_w_ple_gate, m_w_gate, m_w_up, m_w_down]
    big_v = [v_w_in, v_w_out_a, v_w_pw_b, v_w_ple_proj, v_w_o, v_w_ple_gate, v_w_gate, v_w_up, v_w_down]
    big_names = ["w_in", "w_out_a", "w_pw_b", "w_ple_proj", "w_o", "w_ple_gate", "w_gate", "w_up", "w_down"]
    big_out = {}
    for t, nm in enumerate(big_names):
        res = _adamw_big("adamw_" + nm, lands[t], big[t][0], big_m[t][0], big_v[t][0])
        big_out[nm] = [r[None] for r in res]

    small_names = ["g_mix", "conv_a_w", "b_glu", "conf_dw_w", "conf_dw_b", "conf_ln_g", "conf_ln_b", "b_pw_b", "g_ffn", "g_ple", "g_final"]
    small_g = [gr_g_mix, gr_conv_a, gr_b_glu, gr_conf_dw, gr_dw_b, gr_ln_g, gr_ln_b, gr_b_pw, gr_g_ffn, gr_g_ple, gr_g_final]
    small_w = [g_mix, conv_a_w, b_glu, conf_dw_w, conf_dw_b, conf_ln_g, conf_ln_b, b_pw_b, g_ffn, g_ple, g_final]
    small_m = [m_g_mix, m_conv_a_w, m_b_glu, m_conf_dw_w, m_conf_dw_b, m_conf_ln_g, m_conf_ln_b, m_b_pw_b, m_g_ffn, m_g_ple, m_g_final]
    small_v = [v_g_mix, v_conv_a_w, v_b_glu, v_conf_dw_w, v_conf_dw_b, v_conf_ln_g, v_conf_ln_b, v_b_pw_b, v_g_ffn, v_g_ple, v_g_final]
    shp = [tuple(w.shape) for w in small_w]
    small_g = [g.reshape(sh) for g, sh in zip(small_g, shp)]
    sd, sm, sv = _adamw_small(_pack(small_g, 128), _pack(small_w, 128), _pack(small_m, 128), _pack(small_v, 128))
    small_out = {}
    for nm, g, dl, mm, vv in zip(small_names, small_g, _unpack(sd, shp), _unpack(sm, shp), _unpack(sv, shp)):
        small_out[nm] = [g, dl, mm, vv]

    order = ["g_mix", "w_in", "conv_a_w", "w_out_a", "b_glu", "conf_dw_w", "conf_dw_b", "conf_ln_g", "conf_ln_b", "w_pw_b", "b_pw_b", "w_o", "g_ffn", "w_gate", "w_up", "w_down", "g_ple", "w_ple_gate", "w_ple_proj", "g_final"]
    allo = {**big_out, **small_out}
    outs = [loss, dx[None]]
    for q in range(4):
        outs += [allo[nm][q] for nm in order]
    return tuple(outs)
```

```python
import jax
import jax.numpy as jnp
from jax import lax
from jax.experimental import pallas as pl
from jax.experimental.pallas import tpu as pltpu

F32, BF16 = jnp.float32, jnp.bfloat16
EPS, LN_EPS = 1e-6, 1e-5
ADAM_LR, ADAM_B1, ADAM_B2, ADAM_EPS, ADAM_WD, ADAM_STEP = 0.001, 0.9, 0.999, 1e-08, 0.01, 10
CONV_A_K, CONF_K = 3, 31
NDEV = 8
NN = (((1,), (0,)), ((), ()))
NT = (((1,), (1,)), ((), ()))
TN = (((0,), (0,)), ((), ()))
V7X_VMEM_LIMIT_BYTES = 56 * 1024 * 1024
MESH = pl.DeviceIdType.MESH
SDS = jax.ShapeDtypeStruct
HALO_A, HALO_B = 16, 32
CONV_ROWS = 32


def _tile(n, pref):
    t = min(n, pref)
    while n % t:
        t -= 8
    return t


def _sigmoid(x):
    return jax.nn.sigmoid(x)


def _params(sem=None):
    return pltpu.CompilerParams(vmem_limit_bytes=V7X_VMEM_LIMIT_BYTES, dimension_semantics=sem)


def _edge(grid, last):
    cond = None
    for ax, n in enumerate(grid):
        here = pl.program_id(ax) == (n - 1 if last else 0)
        cond = here if cond is None else jnp.logical_and(cond, here)
    return cond


def _join(*comms):
    ins, outs, alias, sems, spans = [], [], {}, [], []
    for cm in comms:
        spans.append((len(ins), len(outs), len(sems)))
        for i, o in cm["alias"].items():
            alias[len(ins) + i] = len(outs) + o
        ins += cm["ins"]
        outs += cm["outs"]
        sems += cm["sems"]

    def run(which):
        def f(i_refs, o_refs, s_refs):
            for cm, (a, b, c_) in zip(comms, spans):
                cm[which](
                    i_refs[a : a + len(cm["ins"])], o_refs[b : b + len(cm["outs"])], s_refs[c_ : c_ + len(cm["sems"])]
                )

        return f

    return dict(ins=ins, outs=outs, alias=alias, sems=sems, start=run("start"), finish=run("finish"))


def _call(body, name, grid, in_specs, args, out_specs, out_shape, scratch=(), sem=None, comm=None):
    n_in, n_out, n_s = len(args), len(out_shape), len(scratch)
    if comm is None:
        res = pl.pallas_call(
            body, name=name, grid=grid, in_specs=list(in_specs), out_specs=list(out_specs), out_shape=list(out_shape),
            scratch_shapes=list(scratch), compiler_params=_params(sem),
        )(*args)
        return list(res), []
    n_ci, n_co = len(comm["ins"]), len(comm["outs"])

    def wrapped(*refs):
        ins = refs[:n_in]
        ci = refs[n_in : n_in + n_ci]
        o0 = n_in + n_ci
        outs = refs[o0 : o0 + n_out]
        co = refs[o0 + n_out : o0 + n_out + n_co]
        s0 = o0 + n_out + n_co
        sc = refs[s0 : s0 + n_s]
        cs = refs[s0 + n_s :]
        pl.when(_edge(grid, False))(lambda: comm["start"](ci, co, cs))
        body(*ins, *outs, *sc)
        pl.when(_edge(grid, True))(lambda: comm["finish"](ci, co, cs))

    hbm = pl.BlockSpec(memory_space=pl.ANY)
    res = pl.pallas_call(
        wrapped,
        name=name,
        grid=grid,
        in_specs=list(in_specs) + [hbm] * n_ci,
        out_specs=list(out_specs) + [hbm] * n_co,
        out_shape=list(out_shape) + list(comm["outs"]),
        scratch_shapes=list(scratch) + list(comm["sems"]),
        input_output_aliases={n_in + i: n_out + o for i, o in comm["alias"].items()},
        compiler_params=_params(("arbitrary",) * len(grid)),
    )(*args, *comm["ins"])
    return list(res[:n_out]), list(res[n_out:])


def _fmm(name, grid, operands, terms, acc_shapes, extras, outs, epilogue, comm=None):
    n_p, n_e, n_o, n_a = len(operands), len(extras), len(outs), len(acc_shapes)
    nk = grid[-1]
    kax = len(grid) - 1
    simple = nk == 1 and all(t[4] is None for t in terms)

    def dot(a, b, dims):
        if a.dtype != BF16:
            a = a.astype(BF16)
        if b.dtype != BF16:
            b = b.astype(BF16)
        return lax.dot_general(a, b, dims, preferred_element_type=F32)

    def value(refs, term):
        slabs = term[5] if len(term) > 5 else 0
        if not slabs:
            return dot(refs[term[0]][...], refs[term[1]][...], term[2])
        tot = None
        for sl in range(slabs):
            d = dot(refs[term[0]][sl], refs[term[1]][sl], term[2])
            tot = d if tot is None else tot + d
        return tot

    def always(refs):
        parts = [None] * n_a
        for term in terms:
            if term[4] is None:
                d = value(refs, term)
                parts[term[3]] = d if parts[term[3]] is None else parts[term[3]] + d
        return parts

    def body(*refs):
        ex = refs[n_p : n_p + n_e]
        os_ = refs[n_p + n_e : n_p + n_e + n_o]
        accs = refs[n_p + n_e + n_o :]
        if simple:
            epilogue(always(refs), ex, os_)
            return
        ids = [pl.program_id(ax) for ax in range(len(grid))]
        k = ids[kax]

        @pl.when(k == 0)
        def _():
            for acc in accs:
                acc[...] = jnp.zeros(acc.shape, F32)

        for ai, part in enumerate(always(refs)):
            if part is not None:
                accs[ai][...] += part
        for term in terms:
            if term[4] is not None:

                def add(term=term):
                    accs[term[3]][...] += value(refs, term)

                pl.when(term[4](ids))(add)

        @pl.when(k == nk - 1)
        def _():
            epilogue([acc[...] for acc in accs], ex, os_)

    return _call(
        body,
        name,
        grid,
        [o[1] for o in operands] + [e[1] for e in extras],
        [o[0] for o in operands] + [e[0] for e in extras],
        [o[1] for o in outs],
        [o[0] for o in outs],
        scratch=[] if simple else [pltpu.VMEM(s, F32) for s in acc_shapes],
        sem=("parallel",) * kax + ("arbitrary",),
        comm=comm,
    )


def _rms_bwd(dn_raw, h, g):
    r = lax.rsqrt(jnp.mean(h * h, axis=-1, keepdims=True) + EPS)
    hn = h * r
    dg = jnp.sum(dn_raw * hn, axis=0, keepdims=True)
    dn = dn_raw * g
    dh = r * (dn - hn * jnp.mean(dn * hn, axis=-1, keepdims=True))
    return dh, dg


def _rms_fwd(name, h, g):
    s, d = h.shape
    ts = _tile(s, 512)

    def body(h_ref, g_ref, o_ref):
        x = h_ref[...]
        r = lax.rsqrt(jnp.mean(x * x, axis=-1, keepdims=True) + EPS)
        o_ref[...] = (x * r * g_ref[...]).astype(BF16)

    return pl.pallas_call(
        body,
        name=name,
        grid=(s // ts,),
        in_specs=[pl.BlockSpec((ts, d), lambda i: (i, 0)), pl.BlockSpec((1, d), lambda i: (0, 0))],
        out_specs=pl.BlockSpec((ts, d), lambda i: (i, 0)),
        out_shape=SDS((s, d), BF16),
        compiler_params=_params(("parallel",)),
    )(h, g)


def _prev_halo(ts, hb):
    r = ts // hb
    return lambda i: jnp.maximum(i * r - 1, 0)


def _next_halo(ts, hb, s):
    r = ts // hb
    last = s // hb - 1
    return lambda i: jnp.minimum((i + 1) * r, last)


def _mix_a_fwd(proj, wa, s, c):
    ts, hb = _tile(s, 256), HALO_A
    prev = _prev_halo(ts, hb)

    def body(ah, ab, ac, hh, hc, w, o, buf):
        i = pl.program_id(0)
        zh = hc[...].astype(F32) * hh[...].astype(F32)
        buf[pl.ds(0, hb), :] = jnp.where(i == 0, 0.0, zh)
        buf[pl.ds(hb, ts), :] = ac[...].astype(F32) * ah[...].astype(F32)
        for r0 in range(0, ts, CONV_ROWS):
            cz = jnp.zeros((CONV_ROWS, c), F32)
            for k in range(CONV_A_K):
                cz = cz + w[k : k + 1, :] * buf[pl.ds(hb + r0 - (CONV_A_K - 1) + k, CONV_ROWS), :]
            o[pl.ds(r0, CONV_ROWS), :] = (ab[pl.ds(r0, CONV_ROWS), :].astype(F32) * cz).astype(BF16)

    main = lambda cb: pl.BlockSpec((ts, c), lambda i: (i, cb))
    halo = lambda cb: pl.BlockSpec((hb, c), lambda i: (prev(i), cb))
    return pl.pallas_call(
        body,
        name="mix_a_fwd",
        grid=(s // ts,),
        in_specs=[main(0), main(1), main(2), halo(0), halo(2), pl.BlockSpec(wa.shape, lambda i: (0, 0))],
        out_specs=pl.BlockSpec((ts, c), lambda i: (i, 0)),
        out_shape=SDS((s, c), BF16),
        scratch_shapes=[pltpu.VMEM((hb + ts, c), F32)],
        compiler_params=_params(("parallel",)),
    )(proj, proj, proj, proj, proj, wa)


def _mix_b_fwd(proj, b_glu, wd, bd, lg, lb, s, c, comm=None):
    ts, hb = _tile(s, 256), HALO_B
    prev = _prev_halo(ts, hb)

    def body(gv, gg, hv, hg, bglu, w, bd_r, lg_r, lb_r, v_o, u_o, cv_o, buf):
        i = pl.program_id(0)
        bv, bg = bglu[:, 0:c], bglu[:, c : 2 * c]
        uh = (hv[...].astype(F32) + bv) * _sigmoid(hg[...].astype(F32) + bg)
        buf[pl.ds(0, hb), :] = jnp.where(i == 0, 0.0, uh)
        u = (gv[...].astype(F32) + bv) * _sigmoid(gg[...].astype(F32) + bg)
        buf[pl.ds(hb, ts), :] = u
        u_o[...] = u.astype(BF16)
        for r0 in range(0, ts, CONV_ROWS):
            acc = jnp.zeros((CONV_ROWS, c), F32)
            for k in range(CONF_K):
                acc = acc + w[k : k + 1, :] * buf[pl.ds(hb + r0 - (CONF_K - 1) + k, CONV_ROWS), :]
            cv_o[pl.ds(r0, CONV_ROWS), :] = acc + bd_r[...]
        cv = cv_o[...]
        mu = jnp.mean(cv, axis=-1, keepdims=True)
        xc = cv - mu
        rs = lax.rsqrt(jnp.mean(xc * xc, axis=-1, keepdims=True) + LN_EPS)
        ln = xc * rs * lg_r[...] + lb_r[...]
        v_o[...] = (ln * _sigmoid(ln)).astype(BF16)

    main = lambda cb: pl.BlockSpec((ts, c), lambda i: (i, cb))
    halo = lambda cb: pl.BlockSpec((hb, c), lambda i: (prev(i), cb))
    full = lambda a: pl.BlockSpec(a.shape, lambda i: (0, 0))
    out = pl.BlockSpec((ts, c), lambda i: (i, 0))
    return _call(
        body,
        "mix_b_fwd",
        (s // ts,),
        [main(3), main(4), halo(3), halo(4), full(b_glu), full(wd), full(bd), full(lg), full(lb)],
        [proj, proj, proj, proj, b_glu, wd, bd, lg, lb],
        [out, out, out],
        [SDS((s, c), BF16), SDS((s, c), BF16), SDS((s, c), F32)],
        scratch=[pltpu.VMEM((hb + ts, c), F32)],
        sem=("parallel",),
        comm=comm,
    )


def _mix_b_bwd1(d_v, cv, lg, lb, s, c):
    ts = _tile(s, 256)

    def body(dv_r, cv_r, lg_r, lb_r, dcv_o, part_o):
        cv_ = cv_r[...]
        mu = jnp.mean(cv_, axis=-1, keepdims=True)
        xc = cv_ - mu
        rs = lax.rsqrt(jnp.mean(xc * xc, axis=-1, keepdims=True) + LN_EPS)
        xh = xc * rs
        ln = xh * lg_r[...] + lb_r[...]
        sg = _sigmoid(ln)
        d_ln = dv_r[...].astype(F32) * (sg * (1.0 + ln * (1.0 - sg)))
        dy = d_ln * lg_r[...]
        d_cv = rs * (dy - jnp.mean(dy, axis=-1, keepdims=True) - xh * jnp.mean(dy * xh, axis=-1, keepdims=True))
        dcv_o[...] = d_cv
        part_o[0:1, :] = jnp.sum(d_ln * xh, axis=0, keepdims=True)
        part_o[1:2, :] = jnp.sum(d_ln, axis=0, keepdims=True)
        part_o[2:3, :] = jnp.sum(d_cv, axis=0, keepdims=True)

    blk = pl.BlockSpec((ts, c), lambda i: (i, 0))
    full = lambda a: pl.BlockSpec(a.shape, lambda i: (0, 0))
    return pl.pallas_call(
        body,
        name="mix_b_bwd_ln",
        grid=(s // ts,),
        in_specs=[blk, blk, full(lg), full(lb)],
        out_specs=[blk, pl.BlockSpec((None, 3, c), lambda i: (i, 0, 0))],
        out_shape=[SDS((s, c), F32), SDS((s // ts, 3, c), F32)],
        compiler_params=_params(("parallel",)),
    )(d_v, cv, lg, lb)


def _mix_b_bwd2(d_cv, u, proj, b_glu, wd, s, c, comm=None):
    ts, hb = _tile(s, 256), HALO_B
    prev, nxt = _prev_halo(ts, hb), _next_halo(ts, hb, s)
    n_t = s // ts
    kp = wd.shape[0]

    def body(dcv, dcv_n, u_m, u_p, gv, gg, bglu, w, d_o, dwd_o, dbglu_o, dbuf, ubuf, dub):
        i = pl.program_id(0)
        dbuf[pl.ds(0, ts), :] = dcv[...]
        dbuf[pl.ds(ts, hb), :] = jnp.where(i == n_t - 1, 0.0, dcv_n[...])
        ubuf[pl.ds(0, hb), :] = jnp.where(i == 0, 0.0, u_p[...].astype(F32))
        ubuf[pl.ds(hb, ts), :] = u_m[...].astype(F32)
        dw_rows = [jnp.zeros((1, c), F32) for _ in range(CONF_K)]
        for r0 in range(0, ts, CONV_ROWS):
            acc = jnp.zeros((CONV_ROWS, c), F32)
            dc = dbuf[pl.ds(r0, CONV_ROWS), :]
            for k in range(CONF_K):
                acc = acc + w[k : k + 1, :] * dbuf[pl.ds(r0 + (CONF_K - 1) - k, CONV_ROWS), :]
                uk = ubuf[pl.ds(hb + r0 - (CONF_K - 1) + k, CONV_ROWS), :]
                dw_rows[k] = dw_rows[k] + jnp.sum(dc * uk, axis=0, keepdims=True)
            dub[pl.ds(r0, CONV_ROWS), :] = acc
        for k in range(CONF_K):
            dwd_o[k : k + 1, :] = dw_rows[k]
        dwd_o[CONF_K:kp, :] = jnp.zeros((kp - CONF_K, c), F32)
        bv, bg = bglu[:, 0:c], bglu[:, c : 2 * c]
        d_u = dub[...]
        sg = _sigmoid(gg[...].astype(F32) + bg)
        d_gv = d_u * sg
        d_gg = d_u * (gv[...].astype(F32) + bv) * sg * (1.0 - sg)
        d_o[:, 0:c] = d_gv.astype(BF16)
        d_o[:, c : 2 * c] = d_gg.astype(BF16)
        dbglu_o[:, 0:c] = jnp.sum(d_gv, axis=0, keepdims=True)
        dbglu_o[:, c : 2 * c] = jnp.sum(d_gg, axis=0, keepdims=True)

    blk = lambda cb: pl.BlockSpec((ts, c), lambda i: (i, cb))
    full = lambda a: pl.BlockSpec(a.shape, lambda i: (0, 0))
    return _call(
        body,
        "mix_b_bwd_conv",
        (n_t,),
        [
            blk(0),
            pl.BlockSpec((hb, c), lambda i: (nxt(i), 0)),
            blk(0),
            pl.BlockSpec((hb, c), lambda i: (prev(i), 0)),
            blk(3),
            blk(4),
            full(b_glu),
            full(wd),
        ],
        [d_cv, d_cv, u, u, proj, proj, b_glu, wd],
        [
            pl.BlockSpec((ts, 2 * c), lambda i: (i, 0)),
            pl.BlockSpec((None, kp, c), lambda i: (i, 0, 0)),
            pl.BlockSpec((None, 1, 2 * c), lambda i: (i, 0, 0)),
        ],
        [SDS((s, 2 * c), BF16), SDS((n_t, kp, c), F32), SDS((n_t, 1, 2 * c), F32)],
        scratch=[pltpu.VMEM((ts + hb, c), F32), pltpu.VMEM((hb + ts, c), F32), pltpu.VMEM((ts, c), F32)],
        sem=("parallel",),
        comm=comm,
    )


def _mix_a_bwd(d_ya, proj, wa, s, c):
    ts, hb = _tile(s, 256), HALO_A
    prev, nxt = _prev_halo(ts, hb), _next_halo(ts, hb, s)
    n_t = s // ts
    kp = wa.shape[0]

    def body(dya, dya_n, ah, ab, ac, ah_p, ac_p, ab_n, w, d_o, dwa_o, zbuf, dbuf, dzb):
        i = pl.program_id(0)
        zbuf[pl.ds(0, hb), :] = jnp.where(i == 0, 0.0, ac_p[...].astype(F32) * ah_p[...].astype(F32))
        zbuf[pl.ds(hb, ts), :] = ac[...].astype(F32) * ah[...].astype(F32)
        dbuf[pl.ds(0, ts), :] = dya[...].astype(F32) * ab[...].astype(F32)
        dbuf[pl.ds(ts, hb), :] = jnp.where(i == n_t - 1, 0.0, dya_n[...].astype(F32) * ab_n[...].astype(F32))
        dw_rows = [jnp.zeros((1, c), F32) for _ in range(CONV_A_K)]
        for r0 in range(0, ts, CONV_ROWS):
            cz = jnp.zeros((CONV_ROWS, c), F32)
            dz = jnp.zeros((CONV_ROWS, c), F32)
            dc = dbuf[pl.ds(r0, CONV_ROWS), :]
            for k in range(CONV_A_K):
                zk = zbuf[pl.ds(hb + r0 - (CONV_A_K - 1) + k, CONV_ROWS), :]
                cz = cz + w[k : k + 1, :] * zk
                dz = dz + w[k : k + 1, :] * dbuf[pl.ds(r0 + (CONV_A_K - 1) - k, CONV_ROWS), :]
                dw_rows[k] = dw_rows[k] + jnp.sum(dc * zk, axis=0, keepdims=True)
            d_o[pl.ds(r0, CONV_ROWS), c : 2 * c] = (dya[pl.ds(r0, CONV_ROWS), :].astype(F32) * cz).astype(BF16)
            dzb[pl.ds(r0, CONV_ROWS), :] = dz
        d_z = dzb[...]
        d_o[:, 0:c] = (d_z * ac[...].astype(F32)).astype(BF16)
        d_o[:, 2 * c : 3 * c] = (d_z * ah[...].astype(F32)).astype(BF16)
        for k in range(CONV_A_K):
            dwa_o[k : k + 1, :] = dw_rows[k]
        dwa_o[CONV_A_K:kp, :] = jnp.zeros((kp - CONV_A_K, c), F32)

    blk = lambda cb: pl.BlockSpec((ts, c), lambda i: (i, cb))
    hp = lambda cb: pl.BlockSpec((hb, c), lambda i: (prev(i), cb))
    hn = lambda cb: pl.BlockSpec((hb, c), lambda i: (nxt(i), cb))
    return pl.pallas_call(
        body,
        name="mix_a_bwd",
        grid=(n_t,),
        in_specs=[blk(0), hn(0), blk(0), blk(1), blk(2), hp(0), hp(2), hn(1), pl.BlockSpec(wa.shape, lambda i: (0, 0))],
        out_specs=[pl.BlockSpec((ts, 3 * c), lambda i: (i, 0)), pl.BlockSpec((None, kp, c), lambda i: (i, 0, 0))],
        out_shape=[SDS((s, 3 * c), BF16), SDS((n_t, kp, c), F32)],
        scratch_shapes=[pltpu.VMEM((hb + ts, c), F32), pltpu.VMEM((ts + hb, c), F32), pltpu.VMEM((ts, c), F32)],
        compiler_params=_params(("parallel",)),
    )(d_ya, d_ya, proj, proj, proj, proj, proj, proj, wa)


def _ep_bf16(accs, ex, os_):
    os_[0][...] = accs[0].astype(BF16)


def _mm_tn(name, a, b, tm=2048, tn=1024, tk=1024):
    m, k1 = a.shape
    n = b.shape[1]
    tm, tn, tk = _tile(k1, tm), _tile(n, tn), _tile(m, tk)
    return _fmm(
        name,
        (k1 // tm, n // tn, m // tk),
        [(a, pl.BlockSpec((tk, tm), lambda i, j, k: (k, i))), (b, pl.BlockSpec((tk, tn), lambda i, j, k: (k, j)))],
        [(0, 1, TN, 0, None)],
        [(tm, tn)],
        [],
        [(SDS((k1, n), BF16), pl.BlockSpec((tm, tn), lambda i, j, k: (i, j)))],
        _ep_bf16,
    )[0][0]


def _mm_nt(name, a, b, tm=1024, tn=1024, comm=None):
    m, kk = a.shape
    n = b.shape[0]
    tm, tn = _tile(m, tm), _tile(n, tn)
    outs, couts = _fmm(
        name,
        (m // tm, n // tn, 1),
        [(a, pl.BlockSpec((tm, kk), lambda i, j, k: (i, 0))), (b, pl.BlockSpec((tn, kk), lambda i, j, k: (j, 0)))],
        [(0, 1, NT, 0, None)],
        [(tm, tn)],
        [],
        [(SDS((m, n), BF16), pl.BlockSpec((tm, tn), lambda i, j, k: (i, j)))],
        _ep_bf16,
        comm=comm,
    )
    return outs[0], couts


def _dev_index(dev):
    return 4 * dev[0] + 2 * dev[1] + dev[2]


def _region(ref, kind, j, shard_shape):
    if kind == "col":
        ns = shard_shape[1]
        return ref.at[:, pl.ds(pl.multiple_of(j * ns, 128), ns)]
    if kind == "row":
        rs = shard_shape[0]
        return ref.at[pl.ds(pl.multiple_of(j * rs, 8), rs), :]
    return ref.at[j]


def _whole_shape(kind, shard_shape):
    if kind == "col":
        return (shard_shape[0], NDEV * shard_shape[1])
    if kind == "row":
        return (NDEV * shard_shape[0], shard_shape[1])
    return (NDEV,) + tuple(shard_shape)


def _place():
    return lax.axis_index("x"), lax.axis_index("y"), lax.axis_index("c")


def _all_gather(shards, kinds):
    n_t = len(shards)
    shapes = [tuple(sh.shape) for sh in shards]

    def body(*refs):
        srcs, dsts = refs[:n_t], refs[n_t : 2 * n_t]
        send_sems, recv_sems, local_sems = refs[2 * n_t :]
        x, y, c = _place()
        me, sib = (x, y, c), (x, y, 1 - c)
        chips = [(1 - x, y), (x, 1 - y), (1 - x, 1 - y)]

        def reg(t, dev):
            return _region(dsts[t], kinds[t], _dev_index(dev), shapes[t])

        def copy(t, k, block, to, src=None):
            return pltpu.make_async_remote_copy(
                src_ref=reg(t, block) if src is None else src,
                dst_ref=reg(t, block),
                send_sem=send_sems.at[t, k],
                recv_sem=recv_sems.at[t, k],
                device_id=to,
                device_id_type=MESH,
            )

        mine = [pltpu.make_async_copy(srcs[t], reg(t, me), local_sems.at[t]) for t in range(n_t)]
        first = []
        for t in range(n_t):
            mine[t].start()
            first.append(copy(t, 0, me, sib, src=srcs[t]))
            first += [copy(t, 1 + j, me, (*chip, c), src=srcs[t]) for j, chip in enumerate(chips)]
        for cp in first:
            cp.start()
        passed = []
        for t in range(n_t):
            for j, chip in enumerate(chips):
                copy(t, 1 + j, (*chip, c), me).wait_recv()
                p = copy(t, 4 + j, (*chip, c), sib)
                p.start()
                passed.append(p)
        for t in range(n_t):
            copy(t, 0, sib, me).wait_recv()
            for j, chip in enumerate(chips):
                copy(t, 4 + j, (*chip, 1 - c), me).wait_recv()
        for cp in first + passed:
            cp.wait_send()
        for cp in mine:
            cp.wait()

    hbm = pl.BlockSpec(memory_space=pltpu.HBM)
    return pl.pallas_call(
        body,
        name="all_gather_weights",
        in_specs=[hbm] * n_t,
        out_specs=[hbm] * n_t,
        out_shape=[SDS(_whole_shape(kinds[t], shapes[t]), shards[t].dtype) for t in range(n_t)],
        scratch_shapes=[
            pltpu.SemaphoreType.DMA((n_t, 7)),
            pltpu.SemaphoreType.DMA((n_t, 7)),
            pltpu.SemaphoreType.DMA((n_t,)),
        ],
    )(*shards)


def _peer(me, r):
    x, y, c = me
    return (1 - x if r & 4 else x, 1 - y if r & 2 else y, 1 - c if r & 1 else c)


def _remote(src, dst, send_sem, recv_sem, to):
    return lambda: pltpu.make_async_remote_copy(
        src_ref=src, dst_ref=dst, send_sem=send_sem, recv_sem=recv_sem, device_id=to, device_id_type=MESH
    )


def _run(pairs, locals_, start):
    if start:
        for cp in locals_:
            cp.start()
        for snd, _ in pairs:
            snd().start()
    else:
        for snd, arr in pairs:
            arr().wait_recv()
            snd().wait_send()
        for cp in locals_:
            cp.wait()


def _stage(ins, outs, alias, sems, build):
    return dict(
        ins=list(ins), outs=list(outs), alias=alias, sems=list(sems),
        start=lambda i, o, s: _run(*build(i, o, s), True),
        finish=lambda i, o, s: _run(*build(i, o, s), False),
    )


def _ag1(shards, kinds):
    n_t = len(shards)
    shapes = [tuple(sh.shape) for sh in shards]

    def build(srcs, dsts, sems):
        send, recv, loc = sems
        x, y, c = _place()
        me = (x, y, c)
        peers = [(x, y, 1 - c), (1 - x, y, c), (x, 1 - y, c), (1 - x, 1 - y, c)]
        reg = lambda t, dev: _region(dsts[t], kinds[t], _dev_index(dev), shapes[t])
        pairs = []
        for t in range(n_t):
            for k, peer in enumerate(peers):
                snd = _remote(srcs[t], reg(t, me), send.at[t, k], recv.at[t, k], peer)
                arr = _remote(reg(t, peer), reg(t, peer), send.at[t, k], recv.at[t, k], peer)
                pairs.append((snd, arr))
        mine = [pltpu.make_async_copy(srcs[t], reg(t, me), loc.at[t]) for t in range(n_t)]
        return pairs, mine

    outs = [SDS(_whole_shape(kinds[t], shapes[t]), shards[t].dtype) for t in range(n_t)]
    dma = pltpu.SemaphoreType.DMA
    return _stage(shards, outs, {}, [dma((n_t, 4)), dma((n_t, 4)), dma((n_t,))], build)


def _ag2(wholes, kinds, shapes):
    n_t = len(wholes)

    def build(_, dsts, sems):
        send, recv = sems
        x, y, c = _place()
        sib = (x, y, 1 - c)
        chips = [(1 - x, y), (x, 1 - y), (1 - x, 1 - y)]
        reg = lambda t, dev: _region(dsts[t], kinds[t], _dev_index(dev), shapes[t])
        pairs = []
        for t in range(n_t):
            for j, chip in enumerate(chips):
                snd = _remote(reg(t, (*chip, c)), reg(t, (*chip, c)), send.at[t, j], recv.at[t, j], sib)
                arr = _remote(reg(t, (*chip, 1 - c)), reg(t, (*chip, 1 - c)), send.at[t, j], recv.at[t, j], sib)
                pairs.append((snd, arr))
        return pairs, []

    outs = [SDS(w.shape, w.dtype) for w in wholes]
    dma = pltpu.SemaphoreType.DMA
    return _stage(wholes, outs, {t: t for t in range(n_t)}, [dma((n_t, 3)), dma((n_t, 3))], build)


def _chip_of(q):
    return (q >> 1, q & 1)


def _rs1(wholes, kinds, shapes):
    n_t = len(wholes)

    def build(srcs, outs, sems):
        send, recv, loc = sems
        x, y, c = _place()
        sib = (x, y, 1 - c)
        pairs, mine = [], []
        for t in range(n_t):
            own, got = outs[t], outs[n_t + t]
            for q in range(4):
                theirs = _region(srcs[t], kinds[t], _dev_index((*_chip_of(q), 1 - c)), shapes[t])
                ours = _region(srcs[t], kinds[t], _dev_index((*_chip_of(q), c)), shapes[t])
                pairs.append((
                    _remote(theirs, got.at[q], send.at[t, q], recv.at[t, q], sib),
                    _remote(got.at[q], got.at[q], send.at[t, q], recv.at[t, q], sib),
                ))
                mine.append(pltpu.make_async_copy(ours, own.at[q], loc.at[t, q]))
        return pairs, mine

    slabs = [SDS((4,) + tuple(shapes[t]), wholes[t].dtype) for t in range(n_t)]
    dma = pltpu.SemaphoreType.DMA
    return _stage(wholes, slabs + slabs, {}, [dma((n_t, 4)), dma((n_t, 4)), dma((n_t, 4))], build)


def _rs2(pair_sums):
    n_t = len(pair_sums)

    def build(srcs, lands, sems):
        send, recv, loc = sems
        x, y, c = _place()
        my_chip = 2 * x + y
        pairs, mine = [], []
        for t in range(n_t):
            for j, (px, py) in enumerate([(1 - x, y), (x, 1 - y), (1 - x, 1 - y)]):
                q = 2 * px + py
                pairs.append((
                    _remote(srcs[t].at[q], lands[t].at[my_chip], send.at[t, j], recv.at[t, j], (px, py, c)),
                    _remote(lands[t].at[q], lands[t].at[q], send.at[t, j], recv.at[t, j], (px, py, c)),
                ))
            mine.append(pltpu.make_async_copy(srcs[t].at[my_chip], lands[t].at[my_chip], loc.at[t]))
        return pairs, mine

    outs = [SDS(q.shape, q.dtype) for q in pair_sums]
    dma = pltpu.SemaphoreType.DMA
    return _stage(pair_sums, outs, {}, [dma((n_t, 3)), dma((n_t, 3)), dma((n_t,))], build)


def _comm_only(name, comm):
    def body(*refs):
        n_i, n_o = len(comm["ins"]), len(comm["outs"])
        i_refs, o_refs, s_refs = refs[:n_i], refs[n_i : n_i + n_o], refs[n_i + n_o :]
        comm["start"](i_refs, o_refs, s_refs)
        comm["finish"](i_refs, o_refs, s_refs)

    hbm = pl.BlockSpec(memory_space=pltpu.HBM)
    return pl.pallas_call(
        body,
        name=name,
        in_specs=[hbm] * len(comm["ins"]),
        out_specs=[hbm] * len(comm["outs"]),
        out_shape=list(comm["outs"]),
        scratch_shapes=list(comm["sems"]),
        input_output_aliases=dict(comm["alias"]),
    )(*comm["ins"])


def _pair_sum(name, own, got):
    _, rows, cols = own.shape
    tr = _tile(rows, 256)

    def body(a, b, o):
        o[...] = (a[...].astype(F32) + b[...].astype(F32)).astype(BF16)

    blk = pl.BlockSpec((4, tr, cols), lambda i: (0, i, 0))
    return pl.pallas_call(
        body,
        name=name,
        grid=(rows // tr,),
        in_specs=[blk, blk],
        out_specs=blk,
        out_shape=SDS(own.shape, BF16),
        compiler_params=_params(("parallel",)),
    )(own, got)


def _all_reduce_small(part):
    r_, c_ = part.shape

    def body(src, land, total, send_sems, recv_sems):
        me = _place()
        my = _dev_index(me)
        land[my] = src[...]

        def copy(r):
            peer = _peer(me, r)
            return pltpu.make_async_remote_copy(
                src_ref=src,
                dst_ref=land.at[my],
                send_sem=send_sems.at[r - 1],
                recv_sem=recv_sems.at[r - 1],
                device_id=peer,
                device_id_type=MESH,
            )

        def arrival(r):
            peer = _peer(me, r)
            slab = land.at[_dev_index(peer)]
            return pltpu.make_async_remote_copy(
                src_ref=slab,
                dst_ref=slab,
                send_sem=send_sems.at[r - 1],
                recv_sem=recv_sems.at[r - 1],
                device_id=peer,
                device_id_type=MESH,
            )

        sends = [copy(r) for r in range(1, NDEV)]
        for cp in sends:
            cp.start()
        for r in range(1, NDEV):
            arrival(r).wait_recv()
        for cp in sends:
            cp.wait_send()
        acc = land[0]
        for d in range(1, NDEV):
            acc = acc + land[d]
        total[...] = acc

    vmem = pl.BlockSpec(memory_space=pltpu.VMEM)
    return pl.pallas_call(
        body,
        name="all_reduce_small",
        in_specs=[vmem],
        out_specs=[vmem, vmem],
        out_shape=[SDS((NDEV, r_, c_), F32), SDS((r_, c_), F32)],
        scratch_shapes=[pltpu.SemaphoreType.DMA((7,)), pltpu.SemaphoreType.DMA((7,))],
    )(part)[1]


def _adamw_math(g, w, m, v):
    m2 = ADAM_B1 * m + (1.0 - ADAM_B1) * g
    v2 = ADAM_B2 * v + (1.0 - ADAM_B2) * (g * g)
    m_hat = m2 / (1.0 - ADAM_B1**ADAM_STEP)
    v_hat = v2 / (1.0 - ADAM_B2**ADAM_STEP)
    delta = -ADAM_LR * (m_hat / (jnp.sqrt(v_hat) + ADAM_EPS) + ADAM_WD * w)
    return delta, m2, v2


def _adamw_big(name, land, w, m, v):
    rows, cols = w.shape
    tr = _tile(rows, 256)
    n_slab = land.shape[0]

    def body(l_ref, w_ref, m_ref, v_ref, g_o, d_o, m_o, v_o):
        g = l_ref[0].astype(F32)
        for d in range(1, n_slab):
            g = g + l_ref[d].astype(F32)
        delta, m2, v2 = _adamw_math(g, w_ref[...], m_ref[...], v_ref[...])
        g_o[...] = g
        d_o[...] = delta
        m_o[...] = m2
        v_o[...] = v2

    blk = pl.BlockSpec((tr, cols), lambda i: (i, 0))
    return pl.pallas_call(
        body,
        name=name,
        grid=(rows // tr,),
        in_specs=[pl.BlockSpec((n_slab, tr, cols), lambda i: (0, i, 0)), blk, blk, blk],
        out_specs=[blk] * 4,
        out_shape=[SDS((rows, cols), F32)] * 4,
        compiler_params=_params(("parallel",)),
    )(land, w, m, v)


def _adamw_small(g, w, m, v):
    def body(g_ref, w_ref, m_ref, v_ref, d_o, m_o, v_o):
        delta, m2, v2 = _adamw_math(g_ref[...], w_ref[...], m_ref[...], v_ref[...])
        d_o[...] = delta
        m_o[...] = m2
        v_o[...] = v2

    vmem = pl.BlockSpec(memory_space=pltpu.VMEM)
    return pl.pallas_call(
        body,
        name="adamw_small",
        in_specs=[vmem] * 4,
        out_specs=[vmem] * 3,
        out_shape=[SDS(g.shape, F32)] * 3,
    )(g, w, m, v)


def _pack(pieces, width):
    flat = jnp.concatenate([p.reshape(-1) for p in pieces])
    rows = -(-flat.shape[0] // (8 * width)) * 8
    flat = jnp.pad(flat, (0, rows * width - flat.shape[0]))
    return flat.reshape(rows, width)


def _unpack(packed, shapes):
    flat = packed.reshape(-1)
    out, off = [], 0
    for shp in shapes:
        n = 1
        for d in shp:
            n *= d
        out.append(flat[off : off + n].reshape(shp))
        off += n
    return out


def kernel(x, p, g_mix, w_in, conv_a_w, w_out_a, b_glu, conf_dw_w, conf_dw_b, conf_ln_g, conf_ln_b, w_pw_b, b_pw_b, w_o, g_ffn, w_gate, w_up, w_down, g_ple, w_ple_gate, w_ple_proj, g_final, loss_target, m_g_mix, m_w_in, m_conv_a_w, m_w_out_a, m_b_glu, m_conf_dw_w, m_conf_dw_b, m_conf_ln_g, m_conf_ln_b, m_w_pw_b, m_b_pw_b, m_w_o, m_g_ffn, m_w_gate, m_w_up, m_w_down, m_g_ple, m_w_ple_gate, m_w_ple_proj, m_g_final, v_g_mix, v_w_in, v_conv_a_w, v_w_out_a, v_b_glu, v_conf_dw_w, v_conf_dw_b, v_conf_ln_g, v_conf_ln_b, v_w_pw_b, v_b_pw_b, v_w_o, v_g_ffn, v_w_gate, v_w_up, v_w_down, v_g_ple, v_w_ple_gate, v_w_ple_proj, v_g_final):
    s, d = x.shape[1], x.shape[2]
    c = conf_ln_g.shape[-1]
    pdim = w_ple_proj.shape[1]
    fs = w_gate.shape[-1]
    nin = NDEV * w_in.shape[-1]
    assert d == 2 * c and nin == 5 * c + 2 * d, (d, c, nin)
    x2, p2, tgt = x[0], p[0, 0], loss_target[0]
    gfin = g_final.reshape(1, d)

    kpa, kpb = 8, HALO_B
    wa_sh = jnp.pad(conv_a_w[0], ((0, kpa - CONV_A_K), (0, 0)))
    wd_sh = jnp.pad(conf_dw_w[0], ((0, kpb - CONF_K), (0, 0)))
    kind_of = dict(w_in="col", w_out_a="col", w_pw_b="col", w_ple_proj="col", w_o="row", w_ple_gate="row",
                   w_gate="blk", w_up="blk", w_down="blk")
    weight = dict(w_in=w_in, w_out_a=w_out_a, w_pw_b=w_pw_b, w_ple_proj=w_ple_proj, w_o=w_o, w_ple_gate=w_ple_gate,
                  w_gate=w_gate, w_up=w_up, w_down=w_down)
    shard_of = {nm: tuple(w.shape[1:]) for nm, w in weight.items()}
    bf16_shard = lambda nm: weight[nm][0].astype(BF16)
    kinds_ = lambda grp: [kind_of[nm] for nm in grp]
    shapes_ = lambda grp: [shard_of[nm] for nm in grp]
    win, wa, wd = _all_gather([bf16_shard("w_in"), wa_sh, wd_sh], ["col", "col", "col"])
    grp_a = ["w_out_a", "w_pw_b", "w_o", "w_gate"]
    grp_b = ["w_up", "w_ple_gate", "w_ple_proj"]
    grp_c = ["w_down"]

    tm = _tile(s, 1024)
    tn = _tile(d, 1024)
    assert (5 * c) % tn == 0 and d % tn == 0 and c % tn == 0
    ga_blk, gb_blk = (5 * c) // tn, (5 * c + d) // tn
    ij = lambda i, j, k: (i, j)
    row_i = lambda i, j, k: (i, 0)

    n1 = _rms_fwd("rms1", x2, g_mix)
    (proj,), part_a = _fmm(
        "proj", (s // tm, nin // tn, 1),
        [(n1, pl.BlockSpec((tm, d), row_i)), (win, pl.BlockSpec((d, tn), lambda i, j, k: (0, j)))],
        [(0, 1, NN, 0, None)], [(tm, tn)], [],
        [(SDS((s, nin), BF16), pl.BlockSpec((tm, tn), ij))], _ep_bf16,
        comm=_ag1([bf16_shard(nm) for nm in grp_a], kinds_(grp_a)),
    )
    ya_in = _mix_a_fwd(proj, wa, s, c)
    (v_act, u_act, cv), got = _mix_b_fwd(
        proj, b_glu, wd, conf_dw_b, conf_ln_g, conf_ln_b, s, c,
        comm=_join(_ag2(part_a, kinds_(grp_a), shapes_(grp_a)), _ag1([bf16_shard(nm) for nm in grp_b], kinds_(grp_b))),
    )
    (wouta, wpw, wo, wg), part_b = got[: len(grp_a)], got[len(grp_a) :]

    def ep_merge(accs, ex, os_):
        sa = _sigmoid(ex[0][...].astype(F32))
        sb = _sigmoid(ex[1][...].astype(F32))
        ya = accs[0]
        yb = accs[1] + ex[2][...]
        os_[0][...] = (sa * ya + sb * yb).astype(BF16)
        os_[1][...] = ya.astype(BF16)
        os_[2][...] = yb.astype(BF16)

    gate_a_spec = pl.BlockSpec((tm, tn), lambda i, j, k: (i, ga_blk + j))
    gate_b_spec = pl.BlockSpec((tm, tn), lambda i, j, k: (i, gb_blk + j))
    out_sd = (SDS((s, d), BF16), pl.BlockSpec((tm, tn), ij))
    (m_act, ya, yb), got = _fmm(
        "merge", (s // tm, d // tn, 1),
        [(ya_in, pl.BlockSpec((tm, c), row_i)), (wouta, pl.BlockSpec((c, tn), lambda i, j, k: (0, j))),
         (v_act, pl.BlockSpec((tm, c), row_i)), (wpw, pl.BlockSpec((c, tn), lambda i, j, k: (0, j)))],
        [(0, 1, NN, 0, None), (2, 3, NN, 1, None)], [(tm, tn), (tm, tn)],
        [(proj, gate_a_spec), (proj, gate_b_spec), (b_pw_b, pl.BlockSpec((1, tn), lambda i, j, k: (0, j)))],
        [out_sd, out_sd, out_sd], ep_merge,
        comm=_join(_ag2(part_b, kinds_(grp_b), shapes_(grp_b)), _ag1([bf16_shard(nm) for nm in grp_c], kinds_(grp_c))),
    )
    (wu, wpg, wpp), part_c = got[: len(grp_b)], got[len(grp_b) :]

    def ep_residual(accs, ex, os_):
        os_[0][...] = accs[0] + ex[0][...]

    (h1,), (wdn,) = _fmm(
        "w_o", (s // tm, d // tn, 1),
        [(m_act, pl.BlockSpec((tm, d), row_i)), (wo, pl.BlockSpec((d, tn), lambda i, j, k: (0, j)))],
        [(0, 1, NN, 0, None)], [(tm, tn)], [(x2, pl.BlockSpec((tm, tn), ij))],
        [(SDS((s, d), F32), pl.BlockSpec((tm, tn), ij))], ep_residual,
        comm=_ag2(part_c, kinds_(grp_c), shapes_(grp_c)),
    )
    n2 = _rms_fwd("rms2", h1, g_ffn)

    def ep_gateup(accs, ex, os_):
        g, u = accs
        os_[0][...] = g.astype(BF16)
        os_[1][...] = u.astype(BF16)
        os_[2][...] = (g * _sigmoid(g) * u).astype(BF16)

    ff_sd = (SDS((NDEV, s, fs), BF16), pl.BlockSpec((None, tm, fs), lambda i, j, k: (j, i, 0)))
    w_col_blk = pl.BlockSpec((None, d, fs), lambda i, j, k: (j, 0, 0))
    (g_act, u_ff, f_act), _ = _fmm(
        "gate_up", (s // tm, NDEV, 1),
        [(n2, pl.BlockSpec((tm, d), row_i)), (wg, w_col_blk), (wu, w_col_blk)],
        [(0, 1, NN, 0, None), (0, 2, NN, 1, None)], [(tm, fs), (tm, fs)], [],
        [ff_sd, ff_sd, ff_sd], ep_gateup,
    )
    pair = 2
    (h2,), _ = _fmm(
        "down", (s // tm, d // tn, NDEV // pair),
        [(f_act, pl.BlockSpec((pair, tm, fs), lambda i, j, k: (k, i, 0))),
         (wdn, pl.BlockSpec((pair, fs, tn), lambda i, j, k: (k, 0, j)))],
        [(0, 1, NN, 0, None, pair)], [(tm, tn)], [(h1, pl.BlockSpec((tm, tn), ij))],
        [(SDS((s, d), F32), pl.BlockSpec((tm, tn), ij))], ep_residual,
    )
    n3 = _rms_fwd("rms3", h2, g_ple)

    tr = _tile(s, 256)
    n_r = s // tr
    rows = lambda i, j, k: (i, 0)
    whole = lambda i, j, k: (0, 0)
    part_spec = lambda nrow: pl.BlockSpec((None, nrow, d), lambda i, j, k: (i, 0, 0))

    def ep_ple(accs, ex, os_):
        h2_, t_, gf = ex[0][...], ex[1][...], ex[2][...]
        ple = accs[0]
        s3 = _sigmoid(accs[1])
        h3 = h2_ + s3 * ple
        r = lax.rsqrt(jnp.mean(h3 * h3, axis=-1, keepdims=True) + EPS)
        hn = h3 * r
        e = hn * gf - t_
        loss = 0.5 * jnp.sum(jnp.mean(e * e, axis=-1, keepdims=True), axis=0, keepdims=True)
        dy = e * (1.0 / d)
        dn = dy * gf
        dh3 = r * (dn - hn * jnp.mean(dn * hn, axis=-1, keepdims=True))
        os_[0][...] = dh3
        os_[1][...] = (dh3 * s3).astype(BF16)
        os_[2][...] = (dh3 * ple * s3 * (1.0 - s3)).astype(BF16)
        os_[3][0:1, :] = jnp.sum(dy * hn, axis=0, keepdims=True)
        os_[3][1:2, :] = jnp.broadcast_to(loss, (1, d))

    (dh3, d_ple, d_g3, part_fin), _ = _fmm(
        "ple_loss", (n_r, 1, 1),
        [(p2, pl.BlockSpec((tr, pdim), rows)), (wpp, pl.BlockSpec((pdim, d), whole)),
         (n3, pl.BlockSpec((tr, d), rows)), (wpg, pl.BlockSpec((d, d), whole))],
        [(0, 1, NN, 0, None), (2, 3, NN, 1, None)], [(tr, d), (tr, d)],
        [(h2, pl.BlockSpec((tr, d), rows)), (tgt, pl.BlockSpec((tr, d), rows)), (gfin, pl.BlockSpec((1, d), whole))],
        [(SDS((s, d), F32), pl.BlockSpec((tr, d), rows)), (SDS((s, d), BF16), pl.BlockSpec((tr, d), rows)),
         (SDS((s, d), BF16), pl.BlockSpec((tr, d), rows)), (SDS((n_r, 2, d), F32), part_spec(2))],
        ep_ple,
    )

    g_wpp = _mm_tn("d_w_ple_proj", p2, d_ple)
    g_wpg = _mm_tn("d_w_ple_gate", n3, d_g3)

    def ep_norm_bwd(accs, ex, os_):
        dh, dg = _rms_bwd(accs[0], ex[0][...], ex[2][...])
        dh = ex[1][...] + dh
        os_[0][...] = dh
        os_[1][...] = dh.astype(BF16)
        os_[2][...] = dg

    norm_outs = lambda t: [
        (SDS((s, d), F32), pl.BlockSpec((t, d), rows)), (SDS((s, d), BF16), pl.BlockSpec((t, d), rows)),
        (SDS((s // t, 1, d), F32), part_spec(1)),
    ]
    def exchange1(names, wholes):
        return _rs1(wholes, kinds_(names), shapes_(names))

    def pair_sums(names, got):
        n_ = len(names)
        return [_pair_sum("pair_sum_" + nm, got[t], got[n_ + t]) for t, nm in enumerate(names)]

    lands = {}
    grp1 = ["w_ple_proj", "w_ple_gate"]
    (dh2, dh2b, part_ple), got = _fmm(
        "d_n3", (n_r, 1, 1),
        [(d_g3, pl.BlockSpec((tr, d), rows)), (wpg, pl.BlockSpec((d, d), whole))],
        [(0, 1, NT, 0, None)], [(tr, d)],
        [(h2, pl.BlockSpec((tr, d), rows)), (dh3, pl.BlockSpec((tr, d), rows)), (g_ple, pl.BlockSpec((1, d), whole))],
        norm_outs(tr), ep_norm_bwd,
        comm=exchange1(grp1, [g_wpp, g_wpg]),
    )
    sums1 = pair_sums(grp1, got)

    def ep_ddown(accs, ex, os_):
        g = ex[0][...].astype(F32)
        u = ex[1][...].astype(F32)
        sg = _sigmoid(g)
        df = accs[0]
        os_[0][...] = (df * u * sg * (1.0 + g * (1.0 - sg))).astype(BF16)
        os_[1][...] = (df * g * sg).astype(BF16)

    ff_in = pl.BlockSpec((None, tm, fs), lambda i, j, k: (j, i, 0))
    (d_g, d_u), got = _fmm(
        "d_down", (s // tm, NDEV, 1),
        [(dh2b, pl.BlockSpec((tm, d), row_i)), (wdn, pl.BlockSpec((None, fs, d), lambda i, j, k: (j, 0, 0)))],
        [(0, 1, NT, 0, None)], [(tm, fs)], [(g_act, ff_in), (u_ff, ff_in)],
        [ff_sd, ff_sd], ep_ddown,
        comm=_rs2(sums1),
    )
    lands.update(zip(grp1, got))
    tk = _tile(s, 1024)
    (g_wdn,), _ = _fmm(
        "d_w_down", (NDEV, 1, s // tk),
        [(f_act, pl.BlockSpec((None, tk, fs), lambda i, j, k: (i, k, 0))), (dh2b, pl.BlockSpec((tk, d), lambda i, j, k: (k, 0)))],
        [(0, 1, TN, 0, None)], [(fs, d)], [],
        [(SDS((NDEV, fs, d), BF16), pl.BlockSpec((None, fs, d), lambda i, j, k: (i, 0, 0)))], _ep_bf16,
    )

    def ep_two_bf16(accs, ex, os_):
        os_[0][...] = accs[0].astype(BF16)
        os_[1][...] = accs[1].astype(BF16)

    ff_k = pl.BlockSpec((None, tk, fs), lambda i, j, k: (i, k, 0))
    wcol_sd = (SDS((NDEV, d, fs), BF16), pl.BlockSpec((None, d, fs), lambda i, j, k: (i, 0, 0)))
    grp2 = ["w_down"]
    (g_wg, g_wu), got = _fmm(
        "d_w_gate_up", (NDEV, 1, s // tk),
        [(n2, pl.BlockSpec((tk, d), lambda i, j, k: (k, 0))), (d_g, ff_k), (d_u, ff_k)],
        [(0, 1, TN, 0, None), (0, 2, TN, 1, None)], [(d, fs), (d, fs)], [],
        [wcol_sd, wcol_sd], ep_two_bf16,
        comm=exchange1(grp2, [g_wdn]),
    )
    sums2 = pair_sums(grp2, got)
    grp3 = ["w_gate", "w_up"]
    th = _tile(s, 512)
    once = lambda shape, index: pl.BlockSpec(shape, index, pipeline_mode=pl.Buffered(1))
    ff_a = pl.BlockSpec((None, th, fs), lambda i, j, k: (k, i, 0))
    w_k = pl.BlockSpec((None, d, fs), lambda i, j, k: (k, 0, 0))
    (dh1, dh1b, part_ffn), got = _fmm(
        "d_n2", (s // th, 1, NDEV),
        [(d_g, ff_a), (wg, w_k), (d_u, ff_a), (wu, w_k)],
        [(0, 1, NT, 0, None), (2, 3, NT, 0, None)], [(th, d)],
        [(h1, once((th, d), rows)), (dh2, once((th, d), rows)), (g_ffn, pl.BlockSpec((1, d), whole))],
        norm_outs(th), ep_norm_bwd,
        comm=_join(_rs2(sums2), exchange1(grp3, [g_wg, g_wu])),
    )
    lands.update(zip(grp2, got[:1]))
    sums3 = pair_sums(grp3, got[1:])
    g_wo = _mm_tn("d_w_o", m_act, dh1b)

    def ep_dm(accs, ex, os_):
        ya_, yb_ = ex[0][...].astype(F32), ex[1][...].astype(F32)
        sa = _sigmoid(ex[2][...].astype(F32))
        sb = _sigmoid(ex[3][...].astype(F32))
        dm = accs[0]
        d_yb = dm * sb
        os_[0][...] = (dm * sa).astype(BF16)
        os_[1][...] = d_yb.astype(BF16)
        os_[2][...] = (dm * ya_ * sa * (1.0 - sa)).astype(BF16)
        os_[3][...] = (dm * yb_ * sb * (1.0 - sb)).astype(BF16)
        os_[4][...] = jnp.sum(d_yb, axis=0, keepdims=True)

    tile_ij = pl.BlockSpec((tm, tn), ij)
    grp4 = ["w_o"]
    (d_ya, d_yb, d_ga, d_gb, part_bpw), got = _fmm(
        "d_merge", (s // tm, d // tn, 1),
        [(dh1b, pl.BlockSpec((tm, d), row_i)), (wo, pl.BlockSpec((tn, d), lambda i, j, k: (j, 0)))],
        [(0, 1, NT, 0, None)], [(tm, tn)],
        [(ya, tile_ij), (yb, tile_ij), (proj, gate_a_spec), (proj, gate_b_spec)],
        [out_sd, out_sd, out_sd, out_sd,
         (SDS((s // tm, 1, d), F32), pl.BlockSpec((None, 1, tn), lambda i, j, k: (i, 0, j)))],
        ep_dm,
        comm=exchange1(grp4, [g_wo]),
    )
    sums4 = pair_sums(grp4, got)
    g_wouta = _mm_tn("d_w_out_a", ya_in, d_ya)
    g_wpw = _mm_tn("d_w_pw_b", v_act, d_yb)
    grp5 = ["w_out_a", "w_pw_b"]
    d_ya_in, got = _mm_nt("d_ya_in", d_ya, wouta, comm=exchange1(grp5, [g_wouta, g_wpw]))
    sums5 = pair_sums(grp5, got)
    d_v, _ = _mm_nt("d_v", d_yb, wpw)
    d_cv, part_ln = _mix_b_bwd1(d_v, cv, conf_ln_g, conf_ln_b, s, c)
    (d_b, part_wd, part_bglu), got = _mix_b_bwd2(d_cv, u_act, proj, b_glu, wd, s, c, comm=_rs2(sums3))
    lands.update(zip(grp3, got))
    d_a, part_wa = _mix_a_bwd(d_ya_in, proj, wa, s, c)

    nb = nin // c
    gblk = d // c
    lo = [0, 3, 5, 5 + gblk]
    hi = [3, 5, 5 + gblk, 5 + 2 * gblk]
    pieces = [d_a, d_b, d_ga, d_gb]

    def active(q, ax):
        return lambda ids: jnp.logical_and(ids[ax] >= lo[q], ids[ax] < hi[q])

    def piece_spec(q, rows_, ax):
        def index(i, j, k):
            ids = (i, j, k)
            col = jnp.clip(ids[ax] - lo[q], 0, hi[q] - lo[q] - 1)
            row = i if ax == 2 else jnp.where(active(q, ax)(ids), k, 0)
            return (row, col)

        return pl.BlockSpec((rows_, c), index)

    tkw = _tile(s, 1024)
    (g_win,), got = _fmm(
        "d_w_in", (1, nb, s // tkw),
        [(n1, pl.BlockSpec((tkw, d), lambda i, j, k: (k, 0)))]
        + [(pieces[q], piece_spec(q, tkw, 1)) for q in range(4)],
        [(0, 1 + q, TN, 0, active(q, 1)) for q in range(4)], [(d, c)], [],
        [(SDS((d, nin), BF16), pl.BlockSpec((d, c), lambda i, j, k: (0, j)))], _ep_bf16,
        comm=_rs2(sums4 + sums5),
    )
    lands.update(zip(grp4 + grp5, got))

    def ep_dx(accs, ex, os_):
        dh, dg = _rms_bwd(accs[0], ex[0][...], ex[2][...])
        os_[0][...] = ex[1][...] + dh
        os_[1][...] = dg

    (dx, part_mix), _ = _fmm(
        "d_n1", (s // th, 1, nb),
        [(pieces[q], piece_spec(q, th, 2)) for q in range(4)]
        + [(win, pl.BlockSpec((d, c), lambda i, j, k: (0, k)))],
        [(q, 4, NT, 0, active(q, 2)) for q in range(4)], [(th, d)],
        [(x2, once((th, d), rows)), (dh1, once((th, d), rows)), (g_mix, pl.BlockSpec((1, d), whole))],
        [(SDS((s, d), F32), pl.BlockSpec((th, d), rows)), (SDS((s // th, 1, d), F32), part_spec(1))],
        ep_dx,
    )
    grp6 = ["w_in"]
    got = _comm_only("exchange_w_in_d2d", exchange1(grp6, [g_win]))
    lands.update(zip(grp6, _comm_only("exchange_w_in_ici", _rs2(pair_sums(grp6, got)))))

    small_parts = [
        jnp.sum(part_mix, axis=0),
        jnp.sum(part_bglu, axis=0),
        jnp.sum(part_ln[:, 2], axis=0),
        jnp.sum(part_ln[:, 0], axis=0),
        jnp.sum(part_ln[:, 1], axis=0),
        jnp.sum(part_bpw, axis=0),
        jnp.sum(part_ffn, axis=0),
        jnp.sum(part_ple, axis=0),
        jnp.sum(part_fin[:, 0], axis=0),
        jnp.sum(part_wa, axis=0),
        jnp.sum(part_wd, axis=0),
        jnp.broadcast_to(jnp.sum(part_fin[:, 1, 0]), (c,)),
    ]
    small_shapes = [(1, d), (1, 2 * c), (1, c), (1, c), (1, c), (1, d), (1, d), (1, d), (d,), (kpa, c), (kpb, c), (c,)]
    total = _all_reduce_small(_pack(small_parts, c))
    (gr_g_mix, gr_b_glu, gr_dw_b, gr_ln_g, gr_ln_b, gr_b_pw, gr_g_ffn, gr_g_ple, gr_g_final, gr_wa, gr_wd, loss_row) = _unpack(total, small_shapes)
    loss = loss_row[0]
    my = _dev_index(_place())
    csh = conv_a_w.shape[-1]
    gr_conv_a = lax.dynamic_slice_in_dim(gr_wa[:CONV_A_K], my * csh, csh, axis=1)[None]
    gr_conf_dw = lax.dynamic_slice_in_dim(gr_wd[:CONF_K], my * csh, csh, axis=1)[None]

    big_m = dict(w_in=m_w_in, w_out_a=m_w_out_a, w_pw_b=m_w_pw_b, w_ple_proj=m_w_ple_proj, w_o=m_w_o,
                 w_ple_gate=m_w_ple_gate, w_gate=m_w_gate, w_up=m_w_up, w_down=m_w_down)
    big_v = dict(w_in=v_w_in, w_out_a=v_w_out_a, w_pw_b=v_w_pw_b, w_ple_proj=v_w_ple_proj, w_o=v_w_o,
                 w_ple_gate=v_w_ple_gate, w_gate=v_w_gate, w_up=v_w_up, w_down=v_w_down)
    big_out = {}
    for nm in weight:
        res = _adamw_big("adamw_" + nm, lands[nm], weight[nm][0], big_m[nm][0], big_v[nm][0])
        big_out[nm] = [r[None] for r in res]

    small_names = ["g_mix", "conv_a_w", "b_glu", "conf_dw_w", "conf_dw_b", "conf_ln_g", "conf_ln_b", "b_pw_b", "g_ffn", "g_ple", "g_final"]
    small_g = [gr_g_mix, gr_conv_a, gr_b_glu, gr_conf_dw, gr_dw_b, gr_ln_g, gr_ln_b, gr_b_pw, gr_g_ffn, gr_g_ple, gr_g_final]
    small_w = [g_mix, conv_a_w, b_glu, conf_dw_w, conf_dw_b, conf_ln_g, conf_ln_b, b_pw_b, g_ffn, g_ple, g_final]
    small_m = [m_g_mix, m_conv_a_w, m_b_glu, m_conf_dw_w, m_conf_dw_b, m_conf_ln_g, m_conf_ln_b, m_b_pw_b, m_g_ffn, m_g_ple, m_g_final]
    small_v = [v_g_mix, v_conv_a_w, v_b_glu, v_conf_dw_w, v_conf_dw_b, v_conf_ln_g, v_conf_ln_b, v_b_pw_b, v_g_ffn, v_g_ple, v_g_final]
    shp = [tuple(w.shape) for w in small_w]
    small_g = [g.reshape(sh) for g, sh in zip(small_g, shp)]
    sd, sm, sv = _adamw_small(_pack(small_g, 128), _pack(small_w, 128), _pack(small_m, 128), _pack(small_v, 128))
    small_out = {}
    for nm, g, dl, mm, vv in zip(small_names, small_g, _unpack(sd, shp), _unpack(sm, shp), _unpack(sv, shp)):
        small_out[nm] = [g, dl, mm, vv]

    order = ["g_mix", "w_in", "conv_a_w", "w_out_a", "b_glu", "conf_dw_w", "conf_dw_b", "conf_ln_g", "conf_ln_b", "w_pw_b", "b_pw_b", "w_o", "g_ffn", "w_gate", "w_up", "w_down", "g_ple", "w_ple_gate", "w_ple_proj", "g_final"]
    allo = {**big_out, **small_out}
    outs = [loss, dx[None]]
    for q in range(4):
        outs += [allo[nm][q] for nm in order]
    return tuple(outs)
```

```python
import jax
import jax.numpy as jnp
from jax import lax
from jax.experimental import pallas as pl
from jax.experimental.pallas import tpu as pltpu

F32, BF16 = jnp.float32, jnp.bfloat16
EPS, LN_EPS = 1e-6, 1e-5
ADAM_LR, ADAM_B1, ADAM_B2, ADAM_EPS, ADAM_WD, ADAM_STEP = 0.001, 0.9, 0.999, 1e-08, 0.01, 10
CONV_A_K, CONF_K = 3, 31
NDEV = 8
NN = (((1,), (0,)), ((), ()))
NT = (((1,), (1,)), ((), ()))
TN = (((0,), (0,)), ((), ()))
V7X_VMEM_LIMIT_BYTES = 56 * 1024 * 1024
MESH = pl.DeviceIdType.MESH
SDS = jax.ShapeDtypeStruct
HALO_A, HALO_B = 16, 32
CONV_ROWS = 32
CONF_ROWS = 16


def _tile(n, pref):
    t = min(n, pref)
    while n % t:
        t -= 8
    return t


def _sigmoid(x):
    return jax.nn.sigmoid(x)


def _params(sem=None):
    return pltpu.CompilerParams(vmem_limit_bytes=V7X_VMEM_LIMIT_BYTES, dimension_semantics=sem)


def _edge(grid, last):
    cond = None
    for ax, n in enumerate(grid):
        here = pl.program_id(ax) == (n - 1 if last else 0)
        cond = here if cond is None else jnp.logical_and(cond, here)
    return cond


def _join(*comms):
    ins, outs, alias, sems, spans = [], [], {}, [], []
    for cm in comms:
        spans.append((len(ins), len(outs), len(sems)))
        for i, o in cm["alias"].items():
            alias[len(ins) + i] = len(outs) + o
        ins += cm["ins"]
        outs += cm["outs"]
        sems += cm["sems"]

    def run(which):
        def f(i_refs, o_refs, s_refs):
            for cm, (a, b, c_) in zip(comms, spans):
                cm[which](
                    i_refs[a : a + len(cm["ins"])], o_refs[b : b + len(cm["outs"])], s_refs[c_ : c_ + len(cm["sems"])]
                )

        return f

    return dict(ins=ins, outs=outs, alias=alias, sems=sems, start=run("start"), finish=run("finish"))


def _call(body, name, grid, in_specs, args, out_specs, out_shape, scratch=(), sem=None, comm=None):
    n_in, n_out, n_s = len(args), len(out_shape), len(scratch)
    if comm is None:
        res = pl.pallas_call(
            body, name=name, grid=grid, in_specs=list(in_specs), out_specs=list(out_specs), out_shape=list(out_shape),
            scratch_shapes=list(scratch), compiler_params=_params(sem),
        )(*args)
        return list(res), []
    n_ci, n_co = len(comm["ins"]), len(comm["outs"])

    def wrapped(*refs):
        ins = refs[:n_in]
        ci = refs[n_in : n_in + n_ci]
        o0 = n_in + n_ci
        outs = refs[o0 : o0 + n_out]
        co = refs[o0 + n_out : o0 + n_out + n_co]
        s0 = o0 + n_out + n_co
        sc = refs[s0 : s0 + n_s]
        cs = refs[s0 + n_s :]
        pl.when(_edge(grid, False))(lambda: comm["start"](ci, co, cs))
        body(*ins, *outs, *sc)
        pl.when(_edge(grid, True))(lambda: comm["finish"](ci, co, cs))

    hbm = pl.BlockSpec(memory_space=pl.ANY)
    res = pl.pallas_call(
        wrapped,
        name=name,
        grid=grid,
        in_specs=list(in_specs) + [hbm] * n_ci,
        out_specs=list(out_specs) + [hbm] * n_co,
        out_shape=list(out_shape) + list(comm["outs"]),
        scratch_shapes=list(scratch) + list(comm["sems"]),
        input_output_aliases={n_in + i: n_out + o for i, o in comm["alias"].items()},
        compiler_params=_params(("arbitrary",) * len(grid)),
    )(*args, *comm["ins"])
    return list(res[:n_out]), list(res[n_out:])


def _fmm(name, grid, operands, terms, acc_shapes, extras, outs, epilogue, comm=None):
    n_p, n_e, n_o, n_a = len(operands), len(extras), len(outs), len(acc_shapes)
    nk = grid[-1]
    kax = len(grid) - 1
    simple = nk == 1 and all(t[4] is None for t in terms)

    def dot(a, b, dims):
        if a.dtype != BF16:
            a = a.astype(BF16)
        if b.dtype != BF16:
            b = b.astype(BF16)
        return lax.dot_general(a, b, dims, preferred_element_type=F32)

    def value(refs, term):
        slabs = term[5] if len(term) > 5 else 0
        if not slabs:
            return dot(refs[term[0]][...], refs[term[1]][...], term[2])
        tot = None
        for sl in range(slabs):
            d = dot(refs[term[0]][sl], refs[term[1]][sl], term[2])
            tot = d if tot is None else tot + d
        return tot

    def always(refs):
        parts = [None] * n_a
        for term in terms:
            if term[4] is None:
                d = value(refs, term)
                parts[term[3]] = d if parts[term[3]] is None else parts[term[3]] + d
        return parts

    def body(*refs):
        ex = refs[n_p : n_p + n_e]
        os_ = refs[n_p + n_e : n_p + n_e + n_o]
        accs = refs[n_p + n_e + n_o :]
        if simple:
            epilogue(always(refs), ex, os_)
            return
        ids = [pl.program_id(ax) for ax in range(len(grid))]
        k = ids[kax]

        @pl.when(k == 0)
        def _():
            for acc in accs:
                acc[...] = jnp.zeros(acc.shape, F32)

        for ai, part in enumerate(always(refs)):
            if part is not None:
                accs[ai][...] += part
        for term in terms:
            if term[4] is not None:

                def add(term=term):
                    accs[term[3]][...] += value(refs, term)

                pl.when(term[4](ids))(add)

        @pl.when(k == nk - 1)
        def _():
            epilogue([acc[...] for acc in accs], ex, os_)

    return _call(
        body,
        name,
        grid,
        [o[1] for o in operands] + [e[1] for e in extras],
        [o[0] for o in operands] + [e[0] for e in extras],
        [o[1] for o in outs],
        [o[0] for o in outs],
        scratch=[] if simple else [pltpu.VMEM(s, F32) for s in acc_shapes],
        sem=("parallel",) * kax + ("arbitrary",),
        comm=comm,
    )


def _rms_bwd(dn_raw, h, g):
    r = lax.rsqrt(jnp.mean(h * h, axis=-1, keepdims=True) + EPS)
    hn = h * r
    dg = jnp.sum(dn_raw * hn, axis=0, keepdims=True)
    dn = dn_raw * g
    dh = r * (dn - hn * jnp.mean(dn * hn, axis=-1, keepdims=True))
    return dh, dg


def _rms_fwd(name, h, g):
    s, d = h.shape
    ts = _tile(s, 512)

    def body(h_ref, g_ref, o_ref):
        x = h_ref[...]
        r = lax.rsqrt(jnp.mean(x * x, axis=-1, keepdims=True) + EPS)
        o_ref[...] = (x * r * g_ref[...]).astype(BF16)

    return pl.pallas_call(
        body,
        name=name,
        grid=(s // ts,),
        in_specs=[pl.BlockSpec((ts, d), lambda i: (i, 0)), pl.BlockSpec((1, d), lambda i: (0, 0))],
        out_specs=pl.BlockSpec((ts, d), lambda i: (i, 0)),
        out_shape=SDS((s, d), BF16),
        compiler_params=_params(("parallel",)),
    )(h, g)


def _norm_bwd(name, dn, h, dres, g, want_bf16):
    s, d = h.shape
    ts = _tile(s, 512)

    def body(dn_r, h_r, dres_r, g_r, *outs):
        dh, dg = _rms_bwd(dn_r[...].astype(F32), h_r[...], g_r[...])
        dh = dres_r[...] + dh
        outs[0][...] = dh
        if want_bf16:
            outs[1][...] = dh.astype(BF16)
        outs[-1][...] = dg

    blk = pl.BlockSpec((ts, d), lambda i: (i, 0))
    part = pl.BlockSpec((None, 1, d), lambda i: (i, 0, 0))
    return pl.pallas_call(
        body,
        name=name,
        grid=(s // ts,),
        in_specs=[blk, blk, blk, pl.BlockSpec((1, d), lambda i: (0, 0))],
        out_specs=[blk] + ([blk] if want_bf16 else []) + [part],
        out_shape=[SDS((s, d), F32)] + ([SDS((s, d), BF16)] if want_bf16 else []) + [SDS((s // ts, 1, d), F32)],
        compiler_params=_params(("parallel",)),
    )(dn, h, dres, g)


def _prev_halo(ts, hb):
    r = ts // hb
    return lambda i: jnp.maximum(i * r - 1, 0)


def _next_halo(ts, hb, s):
    r = ts // hb
    last = s // hb - 1
    return lambda i: jnp.minimum((i + 1) * r, last)


def _shift_copies(buf, sh):
    n = sh.shape[1]
    for j in range(1, 8):
        sh[j - 1, pl.ds(0, n), :] = buf[pl.ds(j, n), :]


def _tap(buf, sh, r0, off, rows):
    j = off % 8
    start = pl.multiple_of(r0 + (off - j), 8)
    if j == 0:
        return buf[pl.ds(start, rows), :]
    return sh[j - 1, pl.ds(start, rows), :]


def _mix_a_fwd(proj, wa, s, c):
    ts, hb = _tile(s, 256), HALO_A
    prev = _prev_halo(ts, hb)

    def body(ah, ab, ac, hh, hc, w, o, buf):
        i = pl.program_id(0)
        zh = hc[...].astype(F32) * hh[...].astype(F32)
        buf[pl.ds(0, hb), :] = jnp.where(i == 0, 0.0, zh)
        buf[pl.ds(hb, ts), :] = ac[...].astype(F32) * ah[...].astype(F32)
        for r0 in range(0, ts, CONV_ROWS):
            cz = jnp.zeros((CONV_ROWS, c), F32)
            for k in range(CONV_A_K):
                cz = cz + w[k : k + 1, :] * buf[pl.ds(hb + r0 - (CONV_A_K - 1) + k, CONV_ROWS), :]
            o[pl.ds(r0, CONV_ROWS), :] = (ab[pl.ds(r0, CONV_ROWS), :].astype(F32) * cz).astype(BF16)

    main = lambda cb: pl.BlockSpec((ts, c), lambda i: (i, cb))
    halo = lambda cb: pl.BlockSpec((hb, c), lambda i: (prev(i), cb))
    return pl.pallas_call(
        body,
        name="mix_a_fwd",
        grid=(s // ts,),
        in_specs=[main(0), main(1), main(2), halo(0), halo(2), pl.BlockSpec(wa.shape, lambda i: (0, 0))],
        out_specs=pl.BlockSpec((ts, c), lambda i: (i, 0)),
        out_shape=SDS((s, c), BF16),
        scratch_shapes=[pltpu.VMEM((hb + ts, c), F32)],
        compiler_params=_params(("parallel",)),
    )(proj, proj, proj, proj, proj, wa)


def _mix_b_fwd(proj, b_glu, wd, bd, lg, lb, s, c, comm=None):
    ts, hb = _tile(s, 256), HALO_B
    prev = _prev_halo(ts, hb)

    def body(gv, gg, hv, hg, bglu, w, bd_r, lg_r, lb_r, v_o, u_o, cv_o, buf, sh):
        i = pl.program_id(0)
        bv, bg = bglu[:, 0:c], bglu[:, c : 2 * c]
        uh = (hv[...].astype(F32) + bv) * _sigmoid(hg[...].astype(F32) + bg)
        buf[pl.ds(0, hb), :] = jnp.where(i == 0, 0.0, uh)
        u = (gv[...].astype(F32) + bv) * _sigmoid(gg[...].astype(F32) + bg)
        buf[pl.ds(hb, ts), :] = u
        u_o[...] = u.astype(BF16)
        _shift_copies(buf, sh)

        def chunk(ci, carry):
            r0 = pl.multiple_of(ci * CONF_ROWS, CONF_ROWS)
            acc = jnp.zeros((CONF_ROWS, c), F32)
            for k in range(CONF_K):
                acc = acc + w[k : k + 1, :] * _tap(buf, sh, r0, hb - (CONF_K - 1) + k, CONF_ROWS)
            cv_o[pl.ds(r0, CONF_ROWS), :] = acc + bd_r[...]
            return carry

        lax.fori_loop(0, ts // CONF_ROWS, chunk, 0)
        cv = cv_o[...]
        mu = jnp.mean(cv, axis=-1, keepdims=True)
        xc = cv - mu
        rs = lax.rsqrt(jnp.mean(xc * xc, axis=-1, keepdims=True) + LN_EPS)
        ln = xc * rs * lg_r[...] + lb_r[...]
        v_o[...] = (ln * _sigmoid(ln)).astype(BF16)

    main = lambda cb: pl.BlockSpec((ts, c), lambda i: (i, cb))
    halo = lambda cb: pl.BlockSpec((hb, c), lambda i: (prev(i), cb))
    full = lambda a: pl.BlockSpec(a.shape, lambda i: (0, 0))
    out = pl.BlockSpec((ts, c), lambda i: (i, 0))
    return _call(
        body,
        "mix_b_fwd",
        (s // ts,),
        [main(3), main(4), halo(3), halo(4), full(b_glu), full(wd), full(bd), full(lg), full(lb)],
        [proj, proj, proj, proj, b_glu, wd, bd, lg, lb],
        [out, out, out],
        [SDS((s, c), BF16), SDS((s, c), BF16), SDS((s, c), F32)],
        scratch=[pltpu.VMEM((hb + ts, c), F32), pltpu.VMEM((7, hb + ts - 8, c), F32)],
        sem=("parallel",),
        comm=comm,
    )


def _mix_b_bwd1(d_v, cv, lg, lb, s, c):
    ts = _tile(s, 256)

    def body(dv_r, cv_r, lg_r, lb_r, dcv_o, part_o):
        cv_ = cv_r[...]
        mu = jnp.mean(cv_, axis=-1, keepdims=True)
        xc = cv_ - mu
        rs = lax.rsqrt(jnp.mean(xc * xc, axis=-1, keepdims=True) + LN_EPS)
        xh = xc * rs
        ln = xh * lg_r[...] + lb_r[...]
        sg = _sigmoid(ln)
        d_ln = dv_r[...].astype(F32) * (sg * (1.0 + ln * (1.0 - sg)))
        dy = d_ln * lg_r[...]
        d_cv = rs * (dy - jnp.mean(dy, axis=-1, keepdims=True) - xh * jnp.mean(dy * xh, axis=-1, keepdims=True))
        dcv_o[...] = d_cv
        part_o[0:1, :] = jnp.sum(d_ln * xh, axis=0, keepdims=True)
        part_o[1:2, :] = jnp.sum(d_ln, axis=0, keepdims=True)
        part_o[2:3, :] = jnp.sum(d_cv, axis=0, keepdims=True)

    blk = pl.BlockSpec((ts, c), lambda i: (i, 0))
    full = lambda a: pl.BlockSpec(a.shape, lambda i: (0, 0))
    return pl.pallas_call(
        body,
        name="mix_b_bwd_ln",
        grid=(s // ts,),
        in_specs=[blk, blk, full(lg), full(lb)],
        out_specs=[blk, pl.BlockSpec((None, 3, c), lambda i: (i, 0, 0))],
        out_shape=[SDS((s, c), F32), SDS((s // ts, 3, c), F32)],
        compiler_params=_params(("parallel",)),
    )(d_v, cv, lg, lb)


def _mix_b_bwd2(d_cv, u, proj, b_glu, wd, s, c, comm=None):
    ts, hb = _tile(s, 256), HALO_B
    prev, nxt = _prev_halo(ts, hb), _next_halo(ts, hb, s)
    n_t = s // ts
    kp = wd.shape[0]

    def body(dcv, dcv_n, u_m, u_p, gv, gg, bglu, w, d_o, dwd_o, dbglu_o, dbuf, ubuf, dub, dsh, ush, dwacc):
        i = pl.program_id(0)
        dbuf[pl.ds(0, ts), :] = dcv[...]
        dbuf[pl.ds(ts, hb), :] = jnp.where(i == n_t - 1, 0.0, dcv_n[...])
        ubuf[pl.ds(0, hb), :] = jnp.where(i == 0, 0.0, u_p[...].astype(F32))
        ubuf[pl.ds(hb, ts), :] = u_m[...].astype(F32)
        _shift_copies(dbuf, dsh)
        _shift_copies(ubuf, ush)
        dwacc[...] = jnp.zeros(dwacc.shape, F32)

        def chunk(ci, carry):
            r0 = pl.multiple_of(ci * CONF_ROWS, CONF_ROWS)
            acc = jnp.zeros((CONF_ROWS, c), F32)
            dc = dbuf[pl.ds(r0, CONF_ROWS), :]
            for k in range(CONF_K):
                acc = acc + w[k : k + 1, :] * _tap(dbuf, dsh, r0, (CONF_K - 1) - k, CONF_ROWS)
                prod = dc * _tap(ubuf, ush, r0, hb - (CONF_K - 1) + k, CONF_ROWS)
                fold = prod[0:8]
                for a in range(1, CONF_ROWS // 8):
                    fold = fold + prod[8 * a : 8 * a + 8]
                dwacc[pl.ds(8 * k, 8), :] += fold
            dub[pl.ds(r0, CONF_ROWS), :] = acc
            return carry

        lax.fori_loop(0, ts // CONF_ROWS, chunk, 0)
        for k in range(CONF_K):
            dwd_o[k : k + 1, :] = jnp.sum(dwacc[pl.ds(8 * k, 8), :], axis=0, keepdims=True)
        dwd_o[CONF_K:kp, :] = jnp.zeros((kp - CONF_K, c), F32)
        bv, bg = bglu[:, 0:c], bglu[:, c : 2 * c]
        d_u = dub[...]
        sg = _sigmoid(gg[...].astype(F32) + bg)
        d_gv = d_u * sg
        d_gg = d_u * (gv[...].astype(F32) + bv) * sg * (1.0 - sg)
        d_o[:, 0:c] = d_gv.astype(BF16)
        d_o[:, c : 2 * c] = d_gg.astype(BF16)
        dbglu_o[:, 0:c] = jnp.sum(d_gv, axis=0, keepdims=True)
        dbglu_o[:, c : 2 * c] = jnp.sum(d_gg, axis=0, keepdims=True)

    blk = lambda cb: pl.BlockSpec((ts, c), lambda i: (i, cb))
    full = lambda a: pl.BlockSpec(a.shape, lambda i: (0, 0))
    return _call(
        body,
        "mix_b_bwd_conv",
        (n_t,),
        [
            blk(0),
            pl.BlockSpec((hb, c), lambda i: (nxt(i), 0)),
            blk(0),
            pl.BlockSpec((hb, c), lambda i: (prev(i), 0)),
            blk(3),
            blk(4),
            full(b_glu),
            full(wd),
        ],
        [d_cv, d_cv, u, u, proj, proj, b_glu, wd],
        [
            pl.BlockSpec((ts, 2 * c), lambda i: (i, 0)),
            pl.BlockSpec((None, kp, c), lambda i: (i, 0, 0)),
            pl.BlockSpec((None, 1, 2 * c), lambda i: (i, 0, 0)),
        ],
        [SDS((s, 2 * c), BF16), SDS((n_t, kp, c), F32), SDS((n_t, 1, 2 * c), F32)],
        scratch=[
            pltpu.VMEM((ts + hb, c), F32), pltpu.VMEM((hb + ts, c), F32), pltpu.VMEM((ts, c), F32),
            pltpu.VMEM((7, hb + ts - 8, c), F32), pltpu.VMEM((7, hb + ts - 8, c), F32), pltpu.VMEM((8 * CONF_K, c), F32),
        ],
        sem=("parallel",),
        comm=comm,
    )


def _mix_a_bwd(d_ya, proj, wa, s, c):
    ts, hb = _tile(s, 256), HALO_A
    prev, nxt = _prev_halo(ts, hb), _next_halo(ts, hb, s)
    n_t = s // ts
    kp = wa.shape[0]

    def body(dya, dya_n, ah, ab, ac, ah_p, ac_p, ab_n, w, d_o, dwa_o, zbuf, dbuf, dzb):
        i = pl.program_id(0)
        zbuf[pl.ds(0, hb), :] = jnp.where(i == 0, 0.0, ac_p[...].astype(F32) * ah_p[...].astype(F32))
        zbuf[pl.ds(hb, ts), :] = ac[...].astype(F32) * ah[...].astype(F32)
        dbuf[pl.ds(0, ts), :] = dya[...].astype(F32) * ab[...].astype(F32)
        dbuf[pl.ds(ts, hb), :] = jnp.where(i == n_t - 1, 0.0, dya_n[...].astype(F32) * ab_n[...].astype(F32))
        dw_rows = [jnp.zeros((1, c), F32) for _ in range(CONV_A_K)]
        for r0 in range(0, ts, CONV_ROWS):
            cz = jnp.zeros((CONV_ROWS, c), F32)
            dz = jnp.zeros((CONV_ROWS, c), F32)
            dc = dbuf[pl.ds(r0, CONV_ROWS), :]
            for k in range(CONV_A_K):
                zk = zbuf[pl.ds(hb + r0 - (CONV_A_K - 1) + k, CONV_ROWS), :]
                cz = cz + w[k : k + 1, :] * zk
                dz = dz + w[k : k + 1, :] * dbuf[pl.ds(r0 + (CONV_A_K - 1) - k, CONV_ROWS), :]
                dw_rows[k] = dw_rows[k] + jnp.sum(dc * zk, axis=0, keepdims=True)
            d_o[pl.ds(r0, CONV_ROWS), c : 2 * c] = (dya[pl.ds(r0, CONV_ROWS), :].astype(F32) * cz).astype(BF16)
            dzb[pl.ds(r0, CONV_ROWS), :] = dz
        d_z = dzb[...]
        d_o[:, 0:c] = (d_z * ac[...].astype(F32)).astype(BF16)
        d_o[:, 2 * c : 3 * c] = (d_z * ah[...].astype(F32)).astype(BF16)
        for k in range(CONV_A_K):
            dwa_o[k : k + 1, :] = dw_rows[k]
        dwa_o[CONV_A_K:kp, :] = jnp.zeros((kp - CONV_A_K, c), F32)

    blk = lambda cb: pl.BlockSpec((ts, c), lambda i: (i, cb))
    hp = lambda cb: pl.BlockSpec((hb, c), lambda i: (prev(i), cb))
    hn = lambda cb: pl.BlockSpec((hb, c), lambda i: (nxt(i), cb))
    return pl.pallas_call(
        body,
        name="mix_a_bwd",
        grid=(n_t,),
        in_specs=[blk(0), hn(0), blk(0), blk(1), blk(2), hp(0), hp(2), hn(1), pl.BlockSpec(wa.shape, lambda i: (0, 0))],
        out_specs=[pl.BlockSpec((ts, 3 * c), lambda i: (i, 0)), pl.BlockSpec((None, kp, c), lambda i: (i, 0, 0))],
        out_shape=[SDS((s, 3 * c), BF16), SDS((n_t, kp, c), F32)],
        scratch_shapes=[pltpu.VMEM((hb + ts, c), F32), pltpu.VMEM((ts + hb, c), F32), pltpu.VMEM((ts, c), F32)],
        compiler_params=_params(("parallel",)),
    )(d_ya, d_ya, proj, proj, proj, proj, proj, proj, wa)


def _ep_bf16(accs, ex, os_):
    os_[0][...] = accs[0].astype(BF16)


def _mm_tn(name, a, b, tm=2048, tn=1024, tk=1024):
    m, k1 = a.shape
    n = b.shape[1]
    tm, tn, tk = _tile(k1, tm), _tile(n, tn), _tile(m, tk)
    return _fmm(
        name,
        (k1 // tm, n // tn, m // tk),
        [(a, pl.BlockSpec((tk, tm), lambda i, j, k: (k, i))), (b, pl.BlockSpec((tk, tn), lambda i, j, k: (k, j)))],
        [(0, 1, TN, 0, None)],
        [(tm, tn)],
        [],
        [(SDS((k1, n), BF16), pl.BlockSpec((tm, tn), lambda i, j, k: (i, j)))],
        _ep_bf16,
    )[0][0]


def _mm_nt(name, a, b, tm=1024, tn=1024, comm=None):
    m, kk = a.shape
    n = b.shape[0]
    tm, tn = _tile(m, tm), _tile(n, tn)
    outs, couts = _fmm(
        name,
        (m // tm, n // tn, 1),
        [(a, pl.BlockSpec((tm, kk), lambda i, j, k: (i, 0))), (b, pl.BlockSpec((tn, kk), lambda i, j, k: (j, 0)))],
        [(0, 1, NT, 0, None)],
        [(tm, tn)],
        [],
        [(SDS((m, n), BF16), pl.BlockSpec((tm, tn), lambda i, j, k: (i, j)))],
        _ep_bf16,
        comm=comm,
    )
    return outs[0], couts


def _dev_index(dev):
    return 4 * dev[0] + 2 * dev[1] + dev[2]


def _region(ref, kind, j, shard_shape):
    if kind == "col":
        ns = shard_shape[1]
        return ref.at[:, pl.ds(pl.multiple_of(j * ns, 128), ns)]
    if kind == "row":
        rs = shard_shape[0]
        return ref.at[pl.ds(pl.multiple_of(j * rs, 8), rs), :]
    return ref.at[j]


def _whole_shape(kind, shard_shape):
    if kind == "col":
        return (shard_shape[0], NDEV * shard_shape[1])
    if kind == "row":
        return (NDEV * shard_shape[0], shard_shape[1])
    return (NDEV,) + tuple(shard_shape)


def _place():
    return lax.axis_index("x"), lax.axis_index("y"), lax.axis_index("c")


def _all_gather(shards, kinds):
    n_t = len(shards)
    shapes = [tuple(sh.shape) for sh in shards]

    def body(*refs):
        srcs, dsts = refs[:n_t], refs[n_t : 2 * n_t]
        send_sems, recv_sems, local_sems = refs[2 * n_t :]
        x, y, c = _place()
        me, sib = (x, y, c), (x, y, 1 - c)
        chips = [(1 - x, y), (x, 1 - y), (1 - x, 1 - y)]

        def reg(t, dev):
            return _region(dsts[t], kinds[t], _dev_index(dev), shapes[t])

        def copy(t, k, block, to, src=None):
            return pltpu.make_async_remote_copy(
                src_ref=reg(t, block) if src is None else src,
                dst_ref=reg(t, block),
                send_sem=send_sems.at[t, k],
                recv_sem=recv_sems.at[t, k],
                device_id=to,
                device_id_type=MESH,
            )

        mine = [pltpu.make_async_copy(srcs[t], reg(t, me), local_sems.at[t]) for t in range(n_t)]
        first = []
        for t in range(n_t):
            mine[t].start()
            first.append(copy(t, 0, me, sib, src=srcs[t]))
            first += [copy(t, 1 + j, me, (*chip, c), src=srcs[t]) for j, chip in enumerate(chips)]
        for cp in first:
            cp.start()
        passed = []
        for t in range(n_t):
            for j, chip in enumerate(chips):
                copy(t, 1 + j, (*chip, c), me).wait_recv()
                p = copy(t, 4 + j, (*chip, c), sib)
                p.start()
                passed.append(p)
        for t in range(n_t):
            copy(t, 0, sib, me).wait_recv()
            for j, chip in enumerate(chips):
                copy(t, 4 + j, (*chip, 1 - c), me).wait_recv()
        for cp in first + passed:
            cp.wait_send()
        for cp in mine:
            cp.wait()

    hbm = pl.BlockSpec(memory_space=pltpu.HBM)
    return pl.pallas_call(
        body,
        name="all_gather_weights",
        in_specs=[hbm] * n_t,
        out_specs=[hbm] * n_t,
        out_shape=[SDS(_whole_shape(kinds[t], shapes[t]), shards[t].dtype) for t in range(n_t)],
        scratch_shapes=[
            pltpu.SemaphoreType.DMA((n_t, 7)),
            pltpu.SemaphoreType.DMA((n_t, 7)),
            pltpu.SemaphoreType.DMA((n_t,)),
        ],
    )(*shards)


def _peer(me, r):
    x, y, c = me
    return (1 - x if r & 4 else x, 1 - y if r & 2 else y, 1 - c if r & 1 else c)


def _remote(src, dst, send_sem, recv_sem, to):
    return lambda: pltpu.make_async_remote_copy(
        src_ref=src, dst_ref=dst, send_sem=send_sem, recv_sem=recv_sem, device_id=to, device_id_type=MESH
    )


def _run(pairs, locals_, start):
    if start:
        for cp in locals_:
            cp.start()
        for snd, _ in pairs:
            snd().start()
    else:
        for snd, arr in pairs:
            arr().wait_recv()
            snd().wait_send()
        for cp in locals_:
            cp.wait()


def _stage(ins, outs, alias, sems, build):
    return dict(
        ins=list(ins), outs=list(outs), alias=alias, sems=list(sems),
        start=lambda i, o, s: _run(*build(i, o, s), True),
        finish=lambda i, o, s: _run(*build(i, o, s), False),
    )


def _ag1(shards, kinds):
    n_t = len(shards)
    shapes = [tuple(sh.shape) for sh in shards]

    def build(srcs, dsts, sems):
        send, recv, loc = sems
        x, y, c = _place()
        me = (x, y, c)
        peers = [(x, y, 1 - c), (1 - x, y, c), (x, 1 - y, c), (1 - x, 1 - y, c)]
        reg = lambda t, dev: _region(dsts[t], kinds[t], _dev_index(dev), shapes[t])
        pairs = []
        for t in range(n_t):
            for k, peer in enumerate(peers):
                snd = _remote(srcs[t], reg(t, me), send.at[t, k], recv.at[t, k], peer)
                arr = _remote(reg(t, peer), reg(t, peer), send.at[t, k], recv.at[t, k], peer)
                pairs.append((snd, arr))
        mine = [pltpu.make_async_copy(srcs[t], reg(t, me), loc.at[t]) for t in range(n_t)]
        return pairs, mine

    outs = [SDS(_whole_shape(kinds[t], shapes[t]), shards[t].dtype) for t in range(n_t)]
    dma = pltpu.SemaphoreType.DMA
    return _stage(shards, outs, {}, [dma((n_t, 4)), dma((n_t, 4)), dma((n_t,))], build)


def _ag2(wholes, kinds, shapes):
    n_t = len(wholes)

    def build(_, dsts, sems):
        send, recv = sems
        x, y, c = _place()
        sib = (x, y, 1 - c)
        chips = [(1 - x, y), (x, 1 - y), (1 - x, 1 - y)]
        reg = lambda t, dev: _region(dsts[t], kinds[t], _dev_index(dev), shapes[t])
        pairs = []
        for t in range(n_t):
            for j, chip in enumerate(chips):
                snd = _remote(reg(t, (*chip, c)), reg(t, (*chip, c)), send.at[t, j], recv.at[t, j], sib)
                arr = _remote(reg(t, (*chip, 1 - c)), reg(t, (*chip, 1 - c)), send.at[t, j], recv.at[t, j], sib)
                pairs.append((snd, arr))
        return pairs, []

    outs = [SDS(w.shape, w.dtype) for w in wholes]
    dma = pltpu.SemaphoreType.DMA
    return _stage(wholes, outs, {t: t for t in range(n_t)}, [dma((n_t, 3)), dma((n_t, 3))], build)


def _chip_of(q):
    return (q >> 1, q & 1)


def _rs1(wholes, kinds, shapes):
    n_t = len(wholes)

    def build(srcs, outs, sems):
        send, recv = sems
        x, y, c = _place()
        sib = (x, y, 1 - c)
        pairs = []
        for t in range(n_t):
            for q in range(4):
                theirs = _region(srcs[t], kinds[t], _dev_index((*_chip_of(q), 1 - c)), shapes[t])
                pairs.append((
                    _remote(theirs, outs[t].at[q], send.at[t, q], recv.at[t, q], sib),
                    _remote(outs[t].at[q], outs[t].at[q], send.at[t, q], recv.at[t, q], sib),
                ))
        return pairs, []

    slabs = [SDS((4,) + tuple(shapes[t]), wholes[t].dtype) for t in range(n_t)]
    dma = pltpu.SemaphoreType.DMA
    return _stage(wholes, slabs, {}, [dma((n_t, 4)), dma((n_t, 4))], build)


def _rs2(pair_sums):
    n_t = len(pair_sums)

    def build(srcs, lands, sems):
        send, recv, loc = sems
        x, y, c = _place()
        my_chip = 2 * x + y
        pairs, mine = [], []
        for t in range(n_t):
            for j, (px, py) in enumerate([(1 - x, y), (x, 1 - y), (1 - x, 1 - y)]):
                q = 2 * px + py
                pairs.append((
                    _remote(srcs[t].at[q], lands[t].at[my_chip], send.at[t, j], recv.at[t, j], (px, py, c)),
                    _remote(lands[t].at[q], lands[t].at[q], send.at[t, j], recv.at[t, j], (px, py, c)),
                ))
            mine.append(pltpu.make_async_copy(srcs[t].at[my_chip], lands[t].at[my_chip], loc.at[t]))
        return pairs, mine

    outs = [SDS(q.shape, q.dtype) for q in pair_sums]
    dma = pltpu.SemaphoreType.DMA
    return _stage(pair_sums, outs, {}, [dma((n_t, 3)), dma((n_t, 3)), dma((n_t,))], build)


def _comm_only(name, comm):
    def body(*refs):
        n_i, n_o = len(comm["ins"]), len(comm["outs"])
        i_refs, o_refs, s_refs = refs[:n_i], refs[n_i : n_i + n_o], refs[n_i + n_o :]
        comm["start"](i_refs, o_refs, s_refs)
        comm["finish"](i_refs, o_refs, s_refs)

    hbm = pl.BlockSpec(memory_space=pltpu.HBM)
    return pl.pallas_call(
        body,
        name=name,
        in_specs=[hbm] * len(comm["ins"]),
        out_specs=[hbm] * len(comm["outs"]),
        out_shape=list(comm["outs"]),
        scratch_shapes=list(comm["sems"]),
        input_output_aliases=dict(comm["alias"]),
    )(*comm["ins"])


def _pair_sum(name, whole, kind, got):
    _, rows, cols = got.shape
    tr = _tile(rows, 256)
    n_r = rows // tr
    core = lax.axis_index("c").astype(jnp.int32).reshape(1)

    def body(_, a, b, o):
        o[...] = (a[...].astype(F32) + b[...].astype(F32)).astype(BF16)

    if kind == "col":
        own = pl.BlockSpec((tr, cols), lambda q, i, c_ref: (i, 2 * q + c_ref[0]))
    elif kind == "row":
        own = pl.BlockSpec((tr, cols), lambda q, i, c_ref: ((2 * q + c_ref[0]) * n_r + i, 0))
    else:
        own = pl.BlockSpec((None, tr, cols), lambda q, i, c_ref: (2 * q + c_ref[0], i, 0))
    slab = pl.BlockSpec((None, tr, cols), lambda q, i, c_ref: (q, i, 0))
    return pl.pallas_call(
        body,
        name=name,
        grid_spec=pltpu.PrefetchScalarGridSpec(
            num_scalar_prefetch=1, grid=(4, n_r), in_specs=[own, slab], out_specs=slab
        ),
        out_shape=SDS(got.shape, BF16),
        compiler_params=_params(("parallel", "parallel")),
    )(core, whole, got)


def _all_reduce_small(part):
    r_, c_ = part.shape

    def body(src, land, total, send_sems, recv_sems):
        me = _place()
        my = _dev_index(me)
        land[my] = src[...]

        def copy(r):
            peer = _peer(me, r)
            return pltpu.make_async_remote_copy(
                src_ref=src,
                dst_ref=land.at[my],
                send_sem=send_sems.at[r - 1],
                recv_sem=recv_sems.at[r - 1],
                device_id=peer,
                device_id_type=MESH,
            )

        def arrival(r):
            peer = _peer(me, r)
            slab = land.at[_dev_index(peer)]
            return pltpu.make_async_remote_copy(
                src_ref=slab,
                dst_ref=slab,
                send_sem=send_sems.at[r - 1],
                recv_sem=recv_sems.at[r - 1],
                device_id=peer,
                device_id_type=MESH,
            )

        sends = [copy(r) for r in range(1, NDEV)]
        for cp in sends:
            cp.start()
        for r in range(1, NDEV):
            arrival(r).wait_recv()
        for cp in sends:
            cp.wait_send()
        acc = land[0]
        for d in range(1, NDEV):
            acc = acc + land[d]
        total[...] = acc

    vmem = pl.BlockSpec(memory_space=pltpu.VMEM)
    return pl.pallas_call(
        body,
        name="all_reduce_small",
        in_specs=[vmem],
        out_specs=[vmem, vmem],
        out_shape=[SDS((NDEV, r_, c_), F32), SDS((r_, c_), F32)],
        scratch_shapes=[pltpu.SemaphoreType.DMA((7,)), pltpu.SemaphoreType.DMA((7,))],
    )(part)[1]


def _adamw_math(g, w, m, v):
    m2 = ADAM_B1 * m + (1.0 - ADAM_B1) * g
    v2 = ADAM_B2 * v + (1.0 - ADAM_B2) * (g * g)
    m_hat = m2 / (1.0 - ADAM_B1**ADAM_STEP)
    v_hat = v2 / (1.0 - ADAM_B2**ADAM_STEP)
    delta = -ADAM_LR * (m_hat / (jnp.sqrt(v_hat) + ADAM_EPS) + ADAM_WD * w)
    return delta, m2, v2


def _adamw_big(name, land, w, m, v):
    rows, cols = w.shape
    tr = _tile(rows, 256)
    n_slab = land.shape[0]

    def body(l_ref, w_ref, m_ref, v_ref, g_o, d_o, m_o, v_o):
        g = l_ref[0].astype(F32)
        for d in range(1, n_slab):
            g = g + l_ref[d].astype(F32)
        delta, m2, v2 = _adamw_math(g, w_ref[...], m_ref[...], v_ref[...])
        g_o[...] = g
        d_o[...] = delta
        m_o[...] = m2
        v_o[...] = v2

    blk = pl.BlockSpec((tr, cols), lambda i: (i, 0))
    return pl.pallas_call(
        body,
        name=name,
        grid=(rows // tr,),
        in_specs=[pl.BlockSpec((n_slab, tr, cols), lambda i: (0, i, 0)), blk, blk, blk],
        out_specs=[blk] * 4,
        out_shape=[SDS((rows, cols), F32)] * 4,
        compiler_params=_params(("parallel",)),
    )(land, w, m, v)


def _adamw_small(g, w, m, v):
    def body(g_ref, w_ref, m_ref, v_ref, d_o, m_o, v_o):
        delta, m2, v2 = _adamw_math(g_ref[...], w_ref[...], m_ref[...], v_ref[...])
        d_o[...] = delta
        m_o[...] = m2
        v_o[...] = v2

    vmem = pl.BlockSpec(memory_space=pltpu.VMEM)
    return pl.pallas_call(
        body,
        name="adamw_small",
        in_specs=[vmem] * 4,
        out_specs=[vmem] * 3,
        out_shape=[SDS(g.shape, F32)] * 3,
    )(g, w, m, v)


def _pack(pieces, width):
    flat = jnp.concatenate([p.reshape(-1) for p in pieces])
    rows = -(-flat.shape[0] // (8 * width)) * 8
    flat = jnp.pad(flat, (0, rows * width - flat.shape[0]))
    return flat.reshape(rows, width)


def _unpack(packed, shapes):
    flat = packed.reshape(-1)
    out, off = [], 0
    for shp in shapes:
        n = 1
        for d in shp:
            n *= d
        out.append(flat[off : off + n].reshape(shp))
        off += n
    return out


def kernel(x, p, g_mix, w_in, conv_a_w, w_out_a, b_glu, conf_dw_w, conf_dw_b, conf_ln_g, conf_ln_b, w_pw_b, b_pw_b, w_o, g_ffn, w_gate, w_up, w_down, g_ple, w_ple_gate, w_ple_proj, g_final, loss_target, m_g_mix, m_w_in, m_conv_a_w, m_w_out_a, m_b_glu, m_conf_dw_w, m_conf_dw_b, m_conf_ln_g, m_conf_ln_b, m_w_pw_b, m_b_pw_b, m_w_o, m_g_ffn, m_w_gate, m_w_up, m_w_down, m_g_ple, m_w_ple_gate, m_w_ple_proj, m_g_final, v_g_mix, v_w_in, v_conv_a_w, v_w_out_a, v_b_glu, v_conf_dw_w, v_conf_dw_b, v_conf_ln_g, v_conf_ln_b, v_w_pw_b, v_b_pw_b, v_w_o, v_g_ffn, v_w_gate, v_w_up, v_w_down, v_g_ple, v_w_ple_gate, v_w_ple_proj, v_g_final):
    s, d = x.shape[1], x.shape[2]
    c = conf_ln_g.shape[-1]
    pdim = w_ple_proj.shape[1]
    fs = w_gate.shape[-1]
    nin = NDEV * w_in.shape[-1]
    assert d == 2 * c and nin == 5 * c + 2 * d, (d, c, nin)
    x2, p2, tgt = x[0], p[0, 0], loss_target[0]
    gfin = g_final.reshape(1, d)

    kpa, kpb = 8, HALO_B
    wa_sh = jnp.pad(conv_a_w[0], ((0, kpa - CONV_A_K), (0, 0)))
    wd_sh = jnp.pad(conf_dw_w[0], ((0, kpb - CONF_K), (0, 0)))
    kind_of = dict(w_in="col", w_out_a="col", w_pw_b="col", w_ple_proj="col", w_o="row", w_ple_gate="row",
                   w_gate="blk", w_up="blk", w_down="blk")
    weight = dict(w_in=w_in, w_out_a=w_out_a, w_pw_b=w_pw_b, w_ple_proj=w_ple_proj, w_o=w_o, w_ple_gate=w_ple_gate,
                  w_gate=w_gate, w_up=w_up, w_down=w_down)
    shard_of = {nm: tuple(w.shape[1:]) for nm, w in weight.items()}
    bf16_shard = lambda nm: weight[nm][0].astype(BF16)
    kinds_ = lambda grp: [kind_of[nm] for nm in grp]
    shapes_ = lambda grp: [shard_of[nm] for nm in grp]
    win, wa, wd = _all_gather([bf16_shard("w_in"), wa_sh, wd_sh], ["col", "col", "col"])
    grp_a = ["w_out_a", "w_pw_b", "w_o", "w_gate"]
    grp_b = ["w_up", "w_ple_gate", "w_ple_proj"]
    grp_c = ["w_down"]

    tm = _tile(s, 1024)
    tn = _tile(d, 1024)
    assert (5 * c) % tn == 0 and d % tn == 0 and c % tn == 0
    ga_blk, gb_blk = (5 * c) // tn, (5 * c + d) // tn
    ij = lambda i, j, k: (i, j)
    row_i = lambda i, j, k: (i, 0)

    n1 = _rms_fwd("rms1", x2, g_mix)
    (proj,), part_a = _fmm(
        "proj", (s // tm, nin // tn, 1),
        [(n1, pl.BlockSpec((tm, d), row_i)), (win, pl.BlockSpec((d, tn), lambda i, j, k: (0, j)))],
        [(0, 1, NN, 0, None)], [(tm, tn)], [],
        [(SDS((s, nin), BF16), pl.BlockSpec((tm, tn), ij))], _ep_bf16,
        comm=_ag1([bf16_shard(nm) for nm in grp_a], kinds_(grp_a)),
    )
    ya_in = _mix_a_fwd(proj, wa, s, c)
    (v_act, u_act, cv), got = _mix_b_fwd(
        proj, b_glu, wd, conf_dw_b, conf_ln_g, conf_ln_b, s, c,
        comm=_join(_ag2(part_a, kinds_(grp_a), shapes_(grp_a)), _ag1([bf16_shard(nm) for nm in grp_b], kinds_(grp_b))),
    )
    (wouta, wpw, wo, wg), part_b = got[: len(grp_a)], got[len(grp_a) :]

    def ep_merge(accs, ex, os_):
        sa = _sigmoid(ex[0][...].astype(F32))
        sb = _sigmoid(ex[1][...].astype(F32))
        ya = accs[0]
        yb = accs[1] + ex[2][...]
        os_[0][...] = (sa * ya + sb * yb).astype(BF16)
        os_[1][...] = ya.astype(BF16)
        os_[2][...] = yb.astype(BF16)

    gate_a_spec = pl.BlockSpec((tm, tn), lambda i, j, k: (i, ga_blk + j))
    gate_b_spec = pl.BlockSpec((tm, tn), lambda i, j, k: (i, gb_blk + j))
    out_sd = (SDS((s, d), BF16), pl.BlockSpec((tm, tn), ij))
    (m_act, ya, yb), got = _fmm(
        "merge", (s // tm, d // tn, 1),
        [(ya_in, pl.BlockSpec((tm, c), row_i)), (wouta, pl.BlockSpec((c, tn), lambda i, j, k: (0, j))),
         (v_act, pl.BlockSpec((tm, c), row_i)), (wpw, pl.BlockSpec((c, tn), lambda i, j, k: (0, j)))],
        [(0, 1, NN, 0, None), (2, 3, NN, 1, None)], [(tm, tn), (tm, tn)],
        [(proj, gate_a_spec), (proj, gate_b_spec), (b_pw_b, pl.BlockSpec((1, tn), lambda i, j, k: (0, j)))],
        [out_sd, out_sd, out_sd], ep_merge,
        comm=_join(_ag2(part_b, kinds_(grp_b), shapes_(grp_b)), _ag1([bf16_shard(nm) for nm in grp_c], kinds_(grp_c))),
    )
    (wu, wpg, wpp), part_c = got[: len(grp_b)], got[len(grp_b) :]

    def ep_residual(accs, ex, os_):
        os_[0][...] = accs[0] + ex[0][...]

    (h1,), (wdn,) = _fmm(
        "w_o", (s // tm, d // tn, 1),
        [(m_act, pl.BlockSpec((tm, d), row_i)), (wo, pl.BlockSpec((d, tn), lambda i, j, k: (0, j)))],
        [(0, 1, NN, 0, None)], [(tm, tn)], [(x2, pl.BlockSpec((tm, tn), ij))],
        [(SDS((s, d), F32), pl.BlockSpec((tm, tn), ij))], ep_residual,
        comm=_ag2(part_c, kinds_(grp_c), shapes_(grp_c)),
    )
    n2 = _rms_fwd("rms2", h1, g_ffn)

    def ep_gateup(accs, ex, os_):
        g, u = accs
        os_[0][...] = g.astype(BF16)
        os_[1][...] = u.astype(BF16)
        os_[2][...] = (g * _sigmoid(g) * u).astype(BF16)

    ff_sd = (SDS((NDEV, s, fs), BF16), pl.BlockSpec((None, tm, fs), lambda i, j, k: (j, i, 0)))
    w_col_blk = pl.BlockSpec((None, d, fs), lambda i, j, k: (j, 0, 0))
    (g_act, u_ff, f_act), _ = _fmm(
        "gate_up", (s // tm, NDEV, 1),
        [(n2, pl.BlockSpec((tm, d), row_i)), (wg, w_col_blk), (wu, w_col_blk)],
        [(0, 1, NN, 0, None), (0, 2, NN, 1, None)], [(tm, fs), (tm, fs)], [],
        [ff_sd, ff_sd, ff_sd], ep_gateup,
    )
    pair = 2
    (h2,), _ = _fmm(
        "down", (s // tm, d // tn, NDEV // pair),
        [(f_act, pl.BlockSpec((pair, tm, fs), lambda i, j, k: (k, i, 0))),
         (wdn, pl.BlockSpec((pair, fs, tn), lambda i, j, k: (k, 0, j)))],
        [(0, 1, NN, 0, None, pair)], [(tm, tn)], [(h1, pl.BlockSpec((tm, tn), ij))],
        [(SDS((s, d), F32), pl.BlockSpec((tm, tn), ij))], ep_residual,
    )
    n3 = _rms_fwd("rms3", h2, g_ple)

    tr = _tile(s, 256)
    n_r = s // tr
    rows = lambda i, j, k: (i, 0)
    whole = lambda i, j, k: (0, 0)
    part_spec = lambda nrow: pl.BlockSpec((None, nrow, d), lambda i, j, k: (i, 0, 0))

    def ep_ple(accs, ex, os_):
        h2_, t_, gf = ex[0][...], ex[1][...], ex[2][...]
        ple = accs[0]
        s3 = _sigmoid(accs[1])
        h3 = h2_ + s3 * ple
        r = lax.rsqrt(jnp.mean(h3 * h3, axis=-1, keepdims=True) + EPS)
        hn = h3 * r
        e = hn * gf - t_
        loss = 0.5 * jnp.sum(jnp.mean(e * e, axis=-1, keepdims=True), axis=0, keepdims=True)
        dy = e * (1.0 / d)
        dn = dy * gf
        dh3 = r * (dn - hn * jnp.mean(dn * hn, axis=-1, keepdims=True))
        os_[0][...] = dh3
        os_[1][...] = (dh3 * s3).astype(BF16)
        os_[2][...] = (dh3 * ple * s3 * (1.0 - s3)).astype(BF16)
        os_[3][0:1, :] = jnp.sum(dy * hn, axis=0, keepdims=True)
        os_[3][1:2, :] = jnp.broadcast_to(loss, (1, d))

    (dh3, d_ple, d_g3, part_fin), _ = _fmm(
        "ple_loss", (n_r, 1, 1),
        [(p2, pl.BlockSpec((tr, pdim), rows)), (wpp, pl.BlockSpec((pdim, d), whole)),
         (n3, pl.BlockSpec((tr, d), rows)), (wpg, pl.BlockSpec((d, d), whole))],
        [(0, 1, NN, 0, None), (2, 3, NN, 1, None)], [(tr, d), (tr, d)],
        [(h2, pl.BlockSpec((tr, d), rows)), (tgt, pl.BlockSpec((tr, d), rows)), (gfin, pl.BlockSpec((1, d), whole))],
        [(SDS((s, d), F32), pl.BlockSpec((tr, d), rows)), (SDS((s, d), BF16), pl.BlockSpec((tr, d), rows)),
         (SDS((s, d), BF16), pl.BlockSpec((tr, d), rows)), (SDS((n_r, 2, d), F32), part_spec(2))],
        ep_ple,
    )

    g_wpp = _mm_tn("d_w_ple_proj", p2, d_ple)
    g_wpg = _mm_tn("d_w_ple_gate", n3, d_g3)

    def ep_norm_bwd(accs, ex, os_):
        dh, dg = _rms_bwd(accs[0], ex[0][...], ex[2][...])
        dh = ex[1][...] + dh
        os_[0][...] = dh
        os_[1][...] = dh.astype(BF16)
        os_[2][...] = dg

    norm_outs = lambda t: [
        (SDS((s, d), F32), pl.BlockSpec((t, d), rows)), (SDS((s, d), BF16), pl.BlockSpec((t, d), rows)),
        (SDS((s // t, 1, d), F32), part_spec(1)),
    ]
    def exchange1(names, wholes):
        return _rs1(wholes, kinds_(names), shapes_(names))

    def pair_sums(names, wholes, got):
        return [_pair_sum("pair_sum_" + nm, wholes[t], kind_of[nm], got[t]) for t, nm in enumerate(names)]

    lands = {}
    grp1 = ["w_ple_proj", "w_ple_gate"]
    (dh2, dh2b, part_ple), got = _fmm(
        "d_n3", (n_r, 1, 1),
        [(d_g3, pl.BlockSpec((tr, d), rows)), (wpg, pl.BlockSpec((d, d), whole))],
        [(0, 1, NT, 0, None)], [(tr, d)],
        [(h2, pl.BlockSpec((tr, d), rows)), (dh3, pl.BlockSpec((tr, d), rows)), (g_ple, pl.BlockSpec((1, d), whole))],
        norm_outs(tr), ep_norm_bwd,
        comm=exchange1(grp1, [g_wpp, g_wpg]),
    )
    sums1 = pair_sums(grp1, [g_wpp, g_wpg], got)

    def ep_ddown(accs, ex, os_):
        g = ex[0][...].astype(F32)
        u = ex[1][...].astype(F32)
        sg = _sigmoid(g)
        df = accs[0]
        os_[0][...] = (df * u * sg * (1.0 + g * (1.0 - sg))).astype(BF16)
        os_[1][...] = (df * g * sg).astype(BF16)

    ff_in = pl.BlockSpec((None, tm, fs), lambda i, j, k: (j, i, 0))
    (d_g, d_u), got = _fmm(
        "d_down", (s // tm, NDEV, 1),
        [(dh2b, pl.BlockSpec((tm, d), row_i)), (wdn, pl.BlockSpec((None, fs, d), lambda i, j, k: (j, 0, 0)))],
        [(0, 1, NT, 0, None)], [(tm, fs)], [(g_act, ff_in), (u_ff, ff_in)],
        [ff_sd, ff_sd], ep_ddown,
        comm=_rs2(sums1),
    )
    lands.update(zip(grp1, got))
    tk = _tile(s, 1024)
    (g_wdn,), _ = _fmm(
        "d_w_down", (NDEV, 1, s // tk),
        [(f_act, pl.BlockSpec((None, tk, fs), lambda i, j, k: (i, k, 0))), (dh2b, pl.BlockSpec((tk, d), lambda i, j, k: (k, 0)))],
        [(0, 1, TN, 0, None)], [(fs, d)], [],
        [(SDS((NDEV, fs, d), BF16), pl.BlockSpec((None, fs, d), lambda i, j, k: (i, 0, 0)))], _ep_bf16,
    )

    def ep_two_bf16(accs, ex, os_):
        os_[0][...] = accs[0].astype(BF16)
        os_[1][...] = accs[1].astype(BF16)

    ff_k = pl.BlockSpec((None, tk, fs), lambda i, j, k: (i, k, 0))
    wcol_sd = (SDS((NDEV, d, fs), BF16), pl.BlockSpec((None, d, fs), lambda i, j, k: (i, 0, 0)))
    grp2 = ["w_down"]
    (g_wg, g_wu), got = _fmm(
        "d_w_gate_up", (NDEV, 1, s // tk),
        [(n2, pl.BlockSpec((tk, d), lambda i, j, k: (k, 0))), (d_g, ff_k), (d_u, ff_k)],
        [(0, 1, TN, 0, None), (0, 2, TN, 1, None)], [(d, fs), (d, fs)], [],
        [wcol_sd, wcol_sd], ep_two_bf16,
        comm=exchange1(grp2, [g_wdn]),
    )
    sums2 = pair_sums(grp2, [g_wdn], got)
    grp3 = ["w_gate", "w_up"]
    th = _tile(s, 1024)
    ff_a = pl.BlockSpec((None, th, fs), lambda i, j, k: (k, i, 0))
    w_k = pl.BlockSpec((None, d, fs), lambda i, j, k: (k, 0, 0))
    (d_n2,), got = _fmm(
        "d_n2", (s // th, 1, NDEV),
        [(d_g, ff_a), (wg, w_k), (d_u, ff_a), (wu, w_k)],
        [(0, 1, NT, 0, None), (2, 3, NT, 0, None)], [(th, d)], [],
        [(SDS((s, d), BF16), pl.BlockSpec((th, d), rows))], _ep_bf16,
        comm=_join(_rs2(sums2), exchange1(grp3, [g_wg, g_wu])),
    )
    lands.update(zip(grp2, got[:1]))
    sums3 = pair_sums(grp3, [g_wg, g_wu], got[1:])
    dh1, dh1b, part_ffn = _norm_bwd("d_h1", d_n2, h1, dh2, g_ffn, True)
    g_wo = _mm_tn("d_w_o", m_act, dh1b)

    def ep_dm(accs, ex, os_):
        ya_, yb_ = ex[0][...].astype(F32), ex[1][...].astype(F32)
        sa = _sigmoid(ex[2][...].astype(F32))
        sb = _sigmoid(ex[3][...].astype(F32))
        dm = accs[0]
        d_yb = dm * sb
        os_[0][...] = (dm * sa).astype(BF16)
        os_[1][...] = d_yb.astype(BF16)
        os_[2][...] = (dm * ya_ * sa * (1.0 - sa)).astype(BF16)
        os_[3][...] = (dm * yb_ * sb * (1.0 - sb)).astype(BF16)
        os_[4][...] = jnp.sum(d_yb, axis=0, keepdims=True)

    tile_ij = pl.BlockSpec((tm, tn), ij)
    grp4 = ["w_o"]
    (d_ya, d_yb, d_ga, d_gb, part_bpw), got = _fmm(
        "d_merge", (s // tm, d // tn, 1),
        [(dh1b, pl.BlockSpec((tm, d), row_i)), (wo, pl.BlockSpec((tn, d), lambda i, j, k: (j, 0)))],
        [(0, 1, NT, 0, None)], [(tm, tn)],
        [(ya, tile_ij), (yb, tile_ij), (proj, gate_a_spec), (proj, gate_b_spec)],
        [out_sd, out_sd, out_sd, out_sd,
         (SDS((s // tm, 1, d), F32), pl.BlockSpec((None, 1, tn), lambda i, j, k: (i, 0, j)))],
        ep_dm,
        comm=exchange1(grp4, [g_wo]),
    )
    sums4 = pair_sums(grp4, [g_wo], got)
    g_wouta = _mm_tn("d_w_out_a", ya_in, d_ya)
    g_wpw = _mm_tn("d_w_pw_b", v_act, d_yb)
    grp5 = ["w_out_a", "w_pw_b"]
    d_ya_in, got = _mm_nt("d_ya_in", d_ya, wouta, comm=exchange1(grp5, [g_wouta, g_wpw]))
    sums5 = pair_sums(grp5, [g_wouta, g_wpw], got)
    d_v, _ = _mm_nt("d_v", d_yb, wpw)
    d_cv, part_ln = _mix_b_bwd1(d_v, cv, conf_ln_g, conf_ln_b, s, c)
    (d_b, part_wd, part_bglu), got = _mix_b_bwd2(d_cv, u_act, proj, b_glu, wd, s, c, comm=_rs2(sums3))
    lands.update(zip(grp3, got))
    d_a, part_wa = _mix_a_bwd(d_ya_in, proj, wa, s, c)

    nb = nin // c
    gblk = d // c
    lo = [0, 3, 5, 5 + gblk]
    hi = [3, 5, 5 + gblk, 5 + 2 * gblk]
    pieces = [d_a, d_b, d_ga, d_gb]

    def active(q, ax):
        return lambda ids: jnp.logical_and(ids[ax] >= lo[q], ids[ax] < hi[q])

    def piece_spec(q, rows_, ax):
        def index(i, j, k):
            ids = (i, j, k)
            col = jnp.clip(ids[ax] - lo[q], 0, hi[q] - lo[q] - 1)
            row = i if ax == 2 else jnp.where(active(q, ax)(ids), k, 0)
            return (row, col)

        return pl.BlockSpec((rows_, c), index)

    tkw = _tile(s, 1024)
    (g_win,), got = _fmm(
        "d_w_in", (1, nb, s // tkw),
        [(n1, pl.BlockSpec((tkw, d), lambda i, j, k: (k, 0)))]
        + [(pieces[q], piece_spec(q, tkw, 1)) for q in range(4)],
        [(0, 1 + q, TN, 0, active(q, 1)) for q in range(4)], [(d, c)], [],
        [(SDS((d, nin), BF16), pl.BlockSpec((d, c), lambda i, j, k: (0, j)))], _ep_bf16,
        comm=_rs2(sums4 + sums5),
    )
    lands.update(zip(grp4 + grp5, got))

    grp6 = ["w_in"]
    (d_n1,), got = _fmm(
        "d_n1", (s // th, 1, nb),
        [(pieces[q], piece_spec(q, th, 2)) for q in range(4)]
        + [(win, pl.BlockSpec((d, c), lambda i, j, k: (0, k)))],
        [(q, 4, NT, 0, active(q, 2)) for q in range(4)], [(th, d)], [],
        [(SDS((s, d), BF16), pl.BlockSpec((th, d), rows))], _ep_bf16,
        comm=exchange1(grp6, [g_win]),
    )
    dx, part_mix = _norm_bwd("d_x", d_n1, x2, dh1, g_mix, False)
    lands.update(zip(grp6, _comm_only("exchange_w_in_ici", _rs2(pair_sums(grp6, [g_win], got)))))

    small_parts = [
        jnp.sum(part_mix, axis=0),
        jnp.sum(part_bglu, axis=0),
        jnp.sum(part_ln[:, 2], axis=0),
        jnp.sum(part_ln[:, 0], axis=0),
        jnp.sum(part_ln[:, 1], axis=0),
        jnp.sum(part_bpw, axis=0),
        jnp.sum(part_ffn, axis=0),
        jnp.sum(part_ple, axis=0),
        jnp.sum(part_fin[:, 0], axis=0),
        jnp.sum(part_wa, axis=0),
        jnp.sum(part_wd, axis=0),
        jnp.broadcast_to(jnp.sum(part_fin[:, 1, 0]), (c,)),
    ]
    small_shapes = [(1, d), (1, 2 * c), (1, c), (1, c), (1, c), (1, d), (1, d), (1, d), (d,), (kpa, c), (kpb, c), (c,)]
    total = _all_reduce_small(_pack(small_parts, c))
    (gr_g_mix, gr_b_glu, gr_dw_b, gr_ln_g, gr_ln_b, gr_b_pw, gr_g_ffn, gr_g_ple, gr_g_final, gr_wa, gr_wd, loss_row) = _unpack(total, small_shapes)
    loss = loss_row[0]
    my = _dev_index(_place())
    csh = conv_a_w.shape[-1]
    gr_conv_a = lax.dynamic_slice_in_dim(gr_wa[:CONV_A_K], my * csh, csh, axis=1)[None]
    gr_conf_dw = lax.dynamic_slice_in_dim(gr_wd[:CONF_K], my * csh, csh, axis=1)[None]

    big_m = dict(w_in=m_w_in, w_out_a=m_w_out_a, w_pw_b=m_w_pw_b, w_ple_proj=m_w_ple_proj, w_o=m_w_o,
                 w_ple_gate=m_w_ple_gate, w_gate=m_w_gate, w_up=m_w_up, w_down=m_w_down)
    big_v = dict(w_in=v_w_in, w_out_a=v_w_out_a, w_pw_b=v_w_pw_b, w_ple_proj=v_w_ple_proj, w_o=v_w_o,
                 w_ple_gate=v_w_ple_gate, w_gate=v_w_gate, w_up=v_w_up, w_down=v_w_down)
    big_out = {}
    for nm in weight:
        res = _adamw_big("adamw_" + nm, lands[nm], weight[nm][0], big_m[nm][0], big_v[nm][0])
        big_out[nm] = [r[None] for r in res]

    small_names = ["g_mix", "conv_a_w", "b_glu", "conf_dw_w", "conf_dw_b", "conf_ln_g", "conf_ln_b", "b_pw_b", "g_ffn", "g_ple", "g_final"]
    small_g = [gr_g_mix, gr_conv_a, gr_b_glu, gr_conf_dw, gr_dw_b, gr_ln_g, gr_ln_b, gr_b_pw, gr_g_ffn, gr_g_ple, gr_g_final]
    small_w = [g_mix, conv_a_w, b_glu, conf_dw_w, conf_dw_b, conf_ln_g, conf_ln_b, b_pw_b, g_ffn, g_ple, g_final]
    small_m = [m_g_mix, m_conv_a_w, m_b_glu, m_conf_dw_w, m_conf_dw_b, m_conf_ln_g, m_conf_ln_b, m_b_pw_b, m_g_ffn, m_g_ple, m_g_final]
    small_v = [v_g_mix, v_conv_a_w, v_b_glu, v_conf_dw_w, v_conf_dw_b, v_conf_ln_g, v_conf_ln_b, v_b_pw_b, v_g_ffn, v_g_ple, v_g_final]
    shp = [tuple(w.shape) for w in small_w]
    small_g = [g.reshape(sh) for g, sh in zip(small_g, shp)]
    sd, sm, sv = _adamw_small(_pack(small_g, 128), _pack(small_w, 128), _pack(small_m, 128), _pack(small_v, 128))
    small_out = {}
    for nm, g, dl, mm, vv in zip(small_names, small_g, _unpack(sd, shp), _unpack(sm, shp), _unpack(sv, shp)):
        small_out[nm] = [g, dl, mm, vv]

    order = ["g_mix", "w_in", "conv_a_w", "w_out_a", "b_glu", "conf_dw_w", "conf_dw_b", "conf_ln_g", "conf_ln_b", "w_pw_b", "b_pw_b", "w_o", "g_ffn", "w_gate", "w_up", "w_down", "g_ple", "w_ple_gate", "w_ple_proj", "g_final"]
    allo = {**big_out, **small_out}
    outs = [loss, dx[None]]
    for q in range(4):
        outs += [allo[nm][q] for nm in order]
    return tuple(outs)
```

```python
import jax
import jax.numpy as jnp
from jax import lax
from jax.experimental import pallas as pl
from jax.experimental.pallas import tpu as pltpu

F32, BF16 = jnp.float32, jnp.bfloat16
EPS, LN_EPS = 1e-6, 1e-5
ADAM_LR, ADAM_B1, ADAM_B2, ADAM_EPS, ADAM_WD, ADAM_STEP = 0.001, 0.9, 0.999, 1e-08, 0.01, 10
CONV_A_K, CONF_K = 3, 31
NDEV = 8
NN = (((1,), (0,)), ((), ()))
NT = (((1,), (1,)), ((), ()))
TN = (((0,), (0,)), ((), ()))
V7X_VMEM_LIMIT_BYTES = 56 * 1024 * 1024
MESH = pl.DeviceIdType.MESH
SDS = jax.ShapeDtypeStruct
HALO_A, HALO_B = 16, 32
EPILOGUE_CHUNK = 256
CONV_ROWS = 32
CONF_ROWS = 16


def _tile(n, pref):
    t = min(n, pref)
    while n % t:
        t -= 8
    return t


def _sigmoid(x):
    return jax.nn.sigmoid(x)


def _params(sem=None):
    return pltpu.CompilerParams(vmem_limit_bytes=V7X_VMEM_LIMIT_BYTES, dimension_semantics=sem)


def _edge(grid, last):
    cond = None
    for ax, n in enumerate(grid):
        here = pl.program_id(ax) == (n - 1 if last else 0)
        cond = here if cond is None else jnp.logical_and(cond, here)
    return cond


def _join(*comms):
    ins, outs, alias, sems, spans = [], [], {}, [], []
    for cm in comms:
        spans.append((len(ins), len(outs), len(sems)))
        for i, o in cm["alias"].items():
            alias[len(ins) + i] = len(outs) + o
        ins += cm["ins"]
        outs += cm["outs"]
        sems += cm["sems"]

    def run(which):
        def f(i_refs, o_refs, s_refs):
            for cm, (a, b, c_) in zip(comms, spans):
                cm[which](
                    i_refs[a : a + len(cm["ins"])], o_refs[b : b + len(cm["outs"])], s_refs[c_ : c_ + len(cm["sems"])]
                )

        return f

    return dict(ins=ins, outs=outs, alias=alias, sems=sems, start=run("start"), finish=run("finish"))


def _call(body, name, grid, in_specs, args, out_specs, out_shape, scratch=(), sem=None, comm=None, alias=None):
    n_in, n_out, n_s = len(args), len(out_shape), len(scratch)
    alias = dict(alias or {})
    if comm is None:
        res = pl.pallas_call(
            body, name=name, grid=grid, in_specs=list(in_specs), out_specs=list(out_specs), out_shape=list(out_shape),
            scratch_shapes=list(scratch), input_output_aliases=alias, compiler_params=_params(sem),
        )(*args)
        return list(res), []
    n_ci, n_co = len(comm["ins"]), len(comm["outs"])

    def wrapped(*refs):
        ins = refs[:n_in]
        ci = refs[n_in : n_in + n_ci]
        o0 = n_in + n_ci
        outs = refs[o0 : o0 + n_out]
        co = refs[o0 + n_out : o0 + n_out + n_co]
        s0 = o0 + n_out + n_co
        sc = refs[s0 : s0 + n_s]
        cs = refs[s0 + n_s :]
        pl.when(_edge(grid, False))(lambda: comm["start"](ci, co, cs))
        body(*ins, *outs, *sc)
        pl.when(_edge(grid, True))(lambda: comm["finish"](ci, co, cs))

    hbm = pl.BlockSpec(memory_space=pl.ANY)
    res = pl.pallas_call(
        wrapped,
        name=name,
        grid=grid,
        in_specs=list(in_specs) + [hbm] * n_ci,
        out_specs=list(out_specs) + [hbm] * n_co,
        out_shape=list(out_shape) + list(comm["outs"]),
        scratch_shapes=list(scratch) + list(comm["sems"]),
        input_output_aliases={**alias, **{n_in + i: n_out + o for i, o in comm["alias"].items()}},
        compiler_params=_params(("arbitrary",) * len(grid)),
    )(*args, *comm["ins"])
    return list(res[:n_out]), list(res[n_out:])


def _col_chunks(n, pref):
    out, c0 = [], 0
    while c0 < n:
        w = min(pref, n - c0)
        out.append((c0, w))
        c0 += w
    return out


def _fmm(name, grid, operands, terms, acc_shapes, extras, outs, epilogue, comm=None, csplit=None, into=None):
    n_p, n_e, n_o, n_a = len(operands), len(extras), len(outs), len(acc_shapes)
    nk = grid[-1]
    kax = len(grid) - 1
    simple = nk == 1 and all(t[4] is None for t in terms)
    alias = None
    if into is not None:
        extras = list(extras) + [(into, pl.BlockSpec(memory_space=pl.ANY))]
        alias = {n_p + n_e: 0}
        n_e += 1
    if csplit is not None:
        assert simple and into is None and all(t[2] in (NN, NT) and (len(t) <= 5 or not t[5]) for t in terms)
        tn_ = acc_shapes[0][1]
        chunks = _col_chunks(tn_, csplit)

    def dot(a, b, dims):
        if a.dtype != BF16:
            a = a.astype(BF16)
        if b.dtype != BF16:
            b = b.astype(BF16)
        return lax.dot_general(a, b, dims, preferred_element_type=F32)

    def value(refs, term):
        slabs = term[5] if len(term) > 5 else 0
        if not slabs:
            return dot(refs[term[0]][...], refs[term[1]][...], term[2])
        tot = None
        for sl in range(slabs):
            d = dot(refs[term[0]][sl], refs[term[1]][sl], term[2])
            tot = d if tot is None else tot + d
        return tot

    def always(refs):
        parts = [None] * n_a
        for term in terms:
            if term[4] is None:
                d = value(refs, term)
                parts[term[3]] = d if parts[term[3]] is None else parts[term[3]] + d
        return parts

    def chunked(refs, ex, os_):
        cols = lambda ref, c0, w: ref.at[:, pl.ds(c0, w)] if ref.shape[-1] == tn_ else ref
        for c0, w in chunks:
            parts = [None] * n_a
            for term in terms:
                b_ref = refs[term[1]]
                b = b_ref[:, pl.ds(c0, w)] if term[2] == NN else b_ref[pl.ds(c0, w), :]
                d = dot(refs[term[0]][...], b, term[2])
                parts[term[3]] = d if parts[term[3]] is None else parts[term[3]] + d
            epilogue(parts, [cols(e, c0, w) for e in ex], [cols(o, c0, w) for o in os_])

    def body(*refs):
        ex = refs[n_p : n_p + n_e]
        os_ = refs[n_p + n_e : n_p + n_e + n_o]
        accs = refs[n_p + n_e + n_o :]
        if simple and csplit is not None:
            chunked(refs, ex, os_)
            return
        if simple:
            epilogue(always(refs), ex, os_)
            return
        ids = [pl.program_id(ax) for ax in range(len(grid))]
        k = ids[kax]

        @pl.when(k == 0)
        def _():
            for acc in accs:
                acc[...] = jnp.zeros(acc.shape, F32)

        for ai, part in enumerate(always(refs)):
            if part is not None:
                accs[ai][...] += part
        for term in terms:
            if term[4] is not None:

                def add(term=term):
                    accs[term[3]][...] += value(refs, term)

                pl.when(term[4](ids))(add)

        @pl.when(k == nk - 1)
        def _():
            epilogue([acc[...] for acc in accs], ex, os_)

    return _call(
        body,
        name,
        grid,
        [o[1] for o in operands] + [e[1] for e in extras],
        [o[0] for o in operands] + [e[0] for e in extras],
        [o[1] for o in outs],
        [o[0] for o in outs],
        scratch=[] if simple else [pltpu.VMEM(s, F32) for s in acc_shapes],
        sem=("parallel",) * kax + ("arbitrary",),
        comm=comm,
        alias=alias,
    )


def _rms_bwd(dn_raw, h, g):
    r = lax.rsqrt(jnp.mean(h * h, axis=-1, keepdims=True) + EPS)
    hn = h * r
    dg = jnp.sum(dn_raw * hn, axis=0, keepdims=True)
    dn = dn_raw * g
    dh = r * (dn - hn * jnp.mean(dn * hn, axis=-1, keepdims=True))
    return dh, dg


def _rms_fwd(name, h, g):
    s, d = h.shape
    ts = _tile(s, 512)

    def body(h_ref, g_ref, o_ref):
        x = h_ref[...]
        r = lax.rsqrt(jnp.mean(x * x, axis=-1, keepdims=True) + EPS)
        o_ref[...] = (x * r * g_ref[...]).astype(BF16)

    return pl.pallas_call(
        body,
        name=name,
        grid=(s // ts,),
        in_specs=[pl.BlockSpec((ts, d), lambda i: (i, 0)), pl.BlockSpec((1, d), lambda i: (0, 0))],
        out_specs=pl.BlockSpec((ts, d), lambda i: (i, 0)),
        out_shape=SDS((s, d), BF16),
        compiler_params=_params(("parallel",)),
    )(h, g)


def _norm_bwd(name, dn, h, dres, g, want_bf16):
    s, d = h.shape
    ts = _tile(s, 512)

    def body(dn_r, h_r, dres_r, g_r, *outs):
        dh, dg = _rms_bwd(dn_r[...].astype(F32), h_r[...], g_r[...])
        dh = dres_r[...] + dh
        outs[0][...] = dh
        if want_bf16:
            outs[1][...] = dh.astype(BF16)
        outs[-1][...] = dg

    blk = pl.BlockSpec((ts, d), lambda i: (i, 0))
    part = pl.BlockSpec((None, 1, d), lambda i: (i, 0, 0))
    return pl.pallas_call(
        body,
        name=name,
        grid=(s // ts,),
        in_specs=[blk, blk, blk, pl.BlockSpec((1, d), lambda i: (0, 0))],
        out_specs=[blk] + ([blk] if want_bf16 else []) + [part],
        out_shape=[SDS((s, d), F32)] + ([SDS((s, d), BF16)] if want_bf16 else []) + [SDS((s // ts, 1, d), F32)],
        compiler_params=_params(("parallel",)),
    )(dn, h, dres, g)


def _prev_halo(ts, hb):
    r = ts // hb
    return lambda i: jnp.maximum(i * r - 1, 0)


def _next_halo(ts, hb, s):
    r = ts // hb
    last = s // hb - 1
    return lambda i: jnp.minimum((i + 1) * r, last)


def _shift_copies(buf, sh):
    n = sh.shape[1]
    for j in range(1, 8):
        sh[j - 1, pl.ds(0, n), :] = buf[pl.ds(j, n), :]


def _tap(buf, sh, r0, off, rows):
    j = off % 8
    start = pl.multiple_of(r0 + (off - j), 8)
    if j == 0:
        return buf[pl.ds(start, rows), :]
    return sh[j - 1, pl.ds(start, rows), :]


def _mix_a_fwd(proj, wa, s, c):
    ts, hb = _tile(s, 256), HALO_A
    prev = _prev_halo(ts, hb)

    def body(ah, ab, ac, hh, hc, w, o, buf):
        i = pl.program_id(0)
        zh = hc[...].astype(F32) * hh[...].astype(F32)
        buf[pl.ds(0, hb), :] = jnp.where(i == 0, 0.0, zh)
        buf[pl.ds(hb, ts), :] = ac[...].astype(F32) * ah[...].astype(F32)
        for r0 in range(0, ts, CONV_ROWS):
            cz = jnp.zeros((CONV_ROWS, c), F32)
            for k in range(CONV_A_K):
                cz = cz + w[k : k + 1, :] * buf[pl.ds(hb + r0 - (CONV_A_K - 1) + k, CONV_ROWS), :]
            o[pl.ds(r0, CONV_ROWS), :] = (ab[pl.ds(r0, CONV_ROWS), :].astype(F32) * cz).astype(BF16)

    main = lambda cb: pl.BlockSpec((ts, c), lambda i: (i, cb))
    halo = lambda cb: pl.BlockSpec((hb, c), lambda i: (prev(i), cb))
    return pl.pallas_call(
        body,
        name="mix_a_fwd",
        grid=(s // ts,),
        in_specs=[main(0), main(1), main(2), halo(0), halo(2), pl.BlockSpec(wa.shape, lambda i: (0, 0))],
        out_specs=pl.BlockSpec((ts, c), lambda i: (i, 0)),
        out_shape=SDS((s, c), BF16),
        scratch_shapes=[pltpu.VMEM((hb + ts, c), F32)],
        compiler_params=_params(("parallel",)),
    )(proj, proj, proj, proj, proj, wa)


def _mix_b_fwd(proj, b_glu, wd, bd, lg, lb, s, c, comm=None):
    ts, hb = _tile(s, 256), HALO_B
    prev = _prev_halo(ts, hb)

    def body(gv, gg, hv, hg, bglu, w, bd_r, lg_r, lb_r, v_o, u_o, cv_o, buf, sh):
        i = pl.program_id(0)
        bv, bg = bglu[:, 0:c], bglu[:, c : 2 * c]
        uh = (hv[...].astype(F32) + bv) * _sigmoid(hg[...].astype(F32) + bg)
        buf[pl.ds(0, hb), :] = jnp.where(i == 0, 0.0, uh)
        u = (gv[...].astype(F32) + bv) * _sigmoid(gg[...].astype(F32) + bg)
        buf[pl.ds(hb, ts), :] = u
        u_o[...] = u.astype(BF16)
        _shift_copies(buf, sh)

        def chunk(ci, carry):
            r0 = pl.multiple_of(ci * CONF_ROWS, CONF_ROWS)
            acc = jnp.zeros((CONF_ROWS, c), F32)
            for k in range(CONF_K):
                acc = acc + w[k : k + 1, :] * _tap(buf, sh, r0, hb - (CONF_K - 1) + k, CONF_ROWS)
            cv_o[pl.ds(r0, CONF_ROWS), :] = acc + bd_r[...]
            return carry

        lax.fori_loop(0, ts // CONF_ROWS, chunk, 0)
        cv = cv_o[...]
        mu = jnp.mean(cv, axis=-1, keepdims=True)
        xc = cv - mu
        rs = lax.rsqrt(jnp.mean(xc * xc, axis=-1, keepdims=True) + LN_EPS)
        ln = xc * rs * lg_r[...] + lb_r[...]
        v_o[...] = (ln * _sigmoid(ln)).astype(BF16)

    main = lambda cb: pl.BlockSpec((ts, c), lambda i: (i, cb))
    halo = lambda cb: pl.BlockSpec((hb, c), lambda i: (prev(i), cb))
    full = lambda a: pl.BlockSpec(a.shape, lambda i: (0, 0))
    out = pl.BlockSpec((ts, c), lambda i: (i, 0))
    return _call(
        body,
        "mix_b_fwd",
        (s // ts,),
        [main(3), main(4), halo(3), halo(4), full(b_glu), full(wd), full(bd), full(lg), full(lb)],
        [proj, proj, proj, proj, b_glu, wd, bd, lg, lb],
        [out, out, out],
        [SDS((s, c), BF16), SDS((s, c), BF16), SDS((s, c), F32)],
        scratch=[pltpu.VMEM((hb + ts, c), F32), pltpu.VMEM((7, hb + ts - 8, c), F32)],
        sem=("parallel",),
        comm=comm,
    )


def _mix_b_bwd1(d_v, cv, lg, lb, s, c):
    ts = _tile(s, 256)

    def body(dv_r, cv_r, lg_r, lb_r, dcv_o, part_o):
        cv_ = cv_r[...]
        mu = jnp.mean(cv_, axis=-1, keepdims=True)
        xc = cv_ - mu
        rs = lax.rsqrt(jnp.mean(xc * xc, axis=-1, keepdims=True) + LN_EPS)
        xh = xc * rs
        ln = xh * lg_r[...] + lb_r[...]
        sg = _sigmoid(ln)
        d_ln = dv_r[...].astype(F32) * (sg * (1.0 + ln * (1.0 - sg)))
        dy = d_ln * lg_r[...]
        d_cv = rs * (dy - jnp.mean(dy, axis=-1, keepdims=True) - xh * jnp.mean(dy * xh, axis=-1, keepdims=True))
        dcv_o[...] = d_cv
        part_o[0:1, :] = jnp.sum(d_ln * xh, axis=0, keepdims=True)
        part_o[1:2, :] = jnp.sum(d_ln, axis=0, keepdims=True)
        part_o[2:3, :] = jnp.sum(d_cv, axis=0, keepdims=True)

    blk = pl.BlockSpec((ts, c), lambda i: (i, 0))
    full = lambda a: pl.BlockSpec(a.shape, lambda i: (0, 0))
    return pl.pallas_call(
        body,
        name="mix_b_bwd_ln",
        grid=(s // ts,),
        in_specs=[blk, blk, full(lg), full(lb)],
        out_specs=[blk, pl.BlockSpec((None, 3, c), lambda i: (i, 0, 0))],
        out_shape=[SDS((s, c), F32), SDS((s // ts, 3, c), F32)],
        compiler_params=_params(("parallel",)),
    )(d_v, cv, lg, lb)


def _mix_b_bwd2(d_cv, u, proj, b_glu, wd, s, c, comm=None):
    ts, hb = _tile(s, 256), HALO_B
    prev, nxt = _prev_halo(ts, hb), _next_halo(ts, hb, s)
    n_t = s // ts
    kp = wd.shape[0]

    def body(dcv, dcv_n, u_m, u_p, gv, gg, bglu, w, d_o, dwd_o, dbglu_o, dbuf, ubuf, dub, dsh, ush, dwacc):
        i = pl.program_id(0)
        dbuf[pl.ds(0, ts), :] = dcv[...]
        dbuf[pl.ds(ts, hb), :] = jnp.where(i == n_t - 1, 0.0, dcv_n[...])
        ubuf[pl.ds(0, hb), :] = jnp.where(i == 0, 0.0, u_p[...].astype(F32))
        ubuf[pl.ds(hb, ts), :] = u_m[...].astype(F32)
        _shift_copies(dbuf, dsh)
        _shift_copies(ubuf, ush)
        dwacc[...] = jnp.zeros(dwacc.shape, F32)

        def chunk(ci, carry):
            r0 = pl.multiple_of(ci * CONF_ROWS, CONF_ROWS)
            acc = jnp.zeros((CONF_ROWS, c), F32)
            dc = dbuf[pl.ds(r0, CONF_ROWS), :]
            for k in range(CONF_K):
                acc = acc + w[k : k + 1, :] * _tap(dbuf, dsh, r0, (CONF_K - 1) - k, CONF_ROWS)
                prod = dc * _tap(ubuf, ush, r0, hb - (CONF_K - 1) + k, CONF_ROWS)
                fold = prod[0:8]
                for a in range(1, CONF_ROWS // 8):
                    fold = fold + prod[8 * a : 8 * a + 8]
                dwacc[pl.ds(8 * k, 8), :] += fold
            dub[pl.ds(r0, CONF_ROWS), :] = acc
            return carry

        lax.fori_loop(0, ts // CONF_ROWS, chunk, 0)
        for k in range(CONF_K):
            dwd_o[k : k + 1, :] = jnp.sum(dwacc[pl.ds(8 * k, 8), :], axis=0, keepdims=True)
        dwd_o[CONF_K:kp, :] = jnp.zeros((kp - CONF_K, c), F32)
        bv, bg = bglu[:, 0:c], bglu[:, c : 2 * c]
        d_u = dub[...]
        sg = _sigmoid(gg[...].astype(F32) + bg)
        d_gv = d_u * sg
        d_gg = d_u * (gv[...].astype(F32) + bv) * sg * (1.0 - sg)
        d_o[:, 0:c] = d_gv.astype(BF16)
        d_o[:, c : 2 * c] = d_gg.astype(BF16)
        dbglu_o[:, 0:c] = jnp.sum(d_gv, axis=0, keepdims=True)
        dbglu_o[:, c : 2 * c] = jnp.sum(d_gg, axis=0, keepdims=True)

    blk = lambda cb: pl.BlockSpec((ts, c), lambda i: (i, cb))
    full = lambda a: pl.BlockSpec(a.shape, lambda i: (0, 0))
    return _call(
        body,
        "mix_b_bwd_conv",
        (n_t,),
        [
            blk(0),
            pl.BlockSpec((hb, c), lambda i: (nxt(i), 0)),
            blk(0),
            pl.BlockSpec((hb, c), lambda i: (prev(i), 0)),
            blk(3),
            blk(4),
            full(b_glu),
            full(wd),
        ],
        [d_cv, d_cv, u, u, proj, proj, b_glu, wd],
        [
            pl.BlockSpec((ts, 2 * c), lambda i: (i, 0)),
            pl.BlockSpec((None, kp, c), lambda i: (i, 0, 0)),
            pl.BlockSpec((None, 1, 2 * c), lambda i: (i, 0, 0)),
        ],
        [SDS((s, 2 * c), BF16), SDS((n_t, kp, c), F32), SDS((n_t, 1, 2 * c), F32)],
        scratch=[
            pltpu.VMEM((ts + hb, c), F32), pltpu.VMEM((hb + ts, c), F32), pltpu.VMEM((ts, c), F32),
            pltpu.VMEM((7, hb + ts - 8, c), F32), pltpu.VMEM((7, hb + ts - 8, c), F32), pltpu.VMEM((8 * CONF_K, c), F32),
        ],
        sem=("parallel",),
        comm=comm,
    )


def _mix_a_bwd(d_ya, proj, wa, s, c):
    ts, hb = _tile(s, 256), HALO_A
    prev, nxt = _prev_halo(ts, hb), _next_halo(ts, hb, s)
    n_t = s // ts
    kp = wa.shape[0]

    def body(dya, dya_n, ah, ab, ac, ah_p, ac_p, ab_n, w, d_o, dwa_o, zbuf, dbuf, dzb):
        i = pl.program_id(0)
        zbuf[pl.ds(0, hb), :] = jnp.where(i == 0, 0.0, ac_p[...].astype(F32) * ah_p[...].astype(F32))
        zbuf[pl.ds(hb, ts), :] = ac[...].astype(F32) * ah[...].astype(F32)
        dbuf[pl.ds(0, ts), :] = dya[...].astype(F32) * ab[...].astype(F32)
        dbuf[pl.ds(ts, hb), :] = jnp.where(i == n_t - 1, 0.0, dya_n[...].astype(F32) * ab_n[...].astype(F32))
        dw_rows = [jnp.zeros((1, c), F32) for _ in range(CONV_A_K)]
        for r0 in range(0, ts, CONV_ROWS):
            cz = jnp.zeros((CONV_ROWS, c), F32)
            dz = jnp.zeros((CONV_ROWS, c), F32)
            dc = dbuf[pl.ds(r0, CONV_ROWS), :]
            for k in range(CONV_A_K):
                zk = zbuf[pl.ds(hb + r0 - (CONV_A_K - 1) + k, CONV_ROWS), :]
                cz = cz + w[k : k + 1, :] * zk
                dz = dz + w[k : k + 1, :] * dbuf[pl.ds(r0 + (CONV_A_K - 1) - k, CONV_ROWS), :]
                dw_rows[k] = dw_rows[k] + jnp.sum(dc * zk, axis=0, keepdims=True)
            d_o[pl.ds(r0, CONV_ROWS), c : 2 * c] = (dya[pl.ds(r0, CONV_ROWS), :].astype(F32) * cz).astype(BF16)
            dzb[pl.ds(r0, CONV_ROWS), :] = dz
        d_z = dzb[...]
        d_o[:, 0:c] = (d_z * ac[...].astype(F32)).astype(BF16)
        d_o[:, 2 * c : 3 * c] = (d_z * ah[...].astype(F32)).astype(BF16)
        for k in range(CONV_A_K):
            dwa_o[k : k + 1, :] = dw_rows[k]
        dwa_o[CONV_A_K:kp, :] = jnp.zeros((kp - CONV_A_K, c), F32)

    blk = lambda cb: pl.BlockSpec((ts, c), lambda i: (i, cb))
    hp = lambda cb: pl.BlockSpec((hb, c), lambda i: (prev(i), cb))
    hn = lambda cb: pl.BlockSpec((hb, c), lambda i: (nxt(i), cb))
    return pl.pallas_call(
        body,
        name="mix_a_bwd",
        grid=(n_t,),
        in_specs=[blk(0), hn(0), blk(0), blk(1), blk(2), hp(0), hp(2), hn(1), pl.BlockSpec(wa.shape, lambda i: (0, 0))],
        out_specs=[pl.BlockSpec((ts, 3 * c), lambda i: (i, 0)), pl.BlockSpec((None, kp, c), lambda i: (i, 0, 0))],
        out_shape=[SDS((s, 3 * c), BF16), SDS((n_t, kp, c), F32)],
        scratch_shapes=[pltpu.VMEM((hb + ts, c), F32), pltpu.VMEM((ts + hb, c), F32), pltpu.VMEM((ts, c), F32)],
        compiler_params=_params(("parallel",)),
    )(d_ya, d_ya, proj, proj, proj, proj, proj, proj, wa)


def _ep_bf16(accs, ex, os_):
    os_[0][...] = accs[0].astype(BF16)


def _mm_tn(name, a, b, tm=2048, tn=1024, tk=1024):
    m, k1 = a.shape
    n = b.shape[1]
    tm, tn, tk = _tile(k1, tm), _tile(n, tn), _tile(m, tk)
    return _fmm(
        name,
        (k1 // tm, n // tn, m // tk),
        [(a, pl.BlockSpec((tk, tm), lambda i, j, k: (k, i))), (b, pl.BlockSpec((tk, tn), lambda i, j, k: (k, j)))],
        [(0, 1, TN, 0, None)],
        [(tm, tn)],
        [],
        [(SDS((k1, n), BF16), pl.BlockSpec((tm, tn), lambda i, j, k: (i, j)))],
        _ep_bf16,
    )[0][0]


def _mm_nt(name, a, b, tm=1024, tn=1024, comm=None):
    m, kk = a.shape
    n = b.shape[0]
    tm, tn = _tile(m, tm), _tile(n, tn)
    outs, couts = _fmm(
        name,
        (m // tm, n // tn, 1),
        [(a, pl.BlockSpec((tm, kk), lambda i, j, k: (i, 0))), (b, pl.BlockSpec((tn, kk), lambda i, j, k: (j, 0)))],
        [(0, 1, NT, 0, None)],
        [(tm, tn)],
        [],
        [(SDS((m, n), BF16), pl.BlockSpec((tm, tn), lambda i, j, k: (i, j)))],
        _ep_bf16,
        comm=comm,
    )
    return outs[0], couts


def _dev_index(dev):
    return 4 * dev[0] + 2 * dev[1] + dev[2]


def _region(ref, kind, j, shard_shape):
    if kind == "col":
        ns = shard_shape[1]
        return ref.at[:, pl.ds(pl.multiple_of(j * ns, 128), ns)]
    if kind == "row":
        rs = shard_shape[0]
        return ref.at[pl.ds(pl.multiple_of(j * rs, 8), rs), :]
    return ref.at[j]


def _whole_shape(kind, shard_shape):
    if kind == "col":
        return (shard_shape[0], NDEV * shard_shape[1])
    if kind == "row":
        return (NDEV * shard_shape[0], shard_shape[1])
    return (NDEV,) + tuple(shard_shape)


def _place():
    return lax.axis_index("x"), lax.axis_index("y"), lax.axis_index("c")


def _all_gather(shards, kinds):
    n_t = len(shards)
    shapes = [tuple(sh.shape) for sh in shards]

    def body(*refs):
        srcs, dsts = refs[:n_t], refs[n_t : 2 * n_t]
        send_sems, recv_sems, local_sems = refs[2 * n_t :]
        x, y, c = _place()
        me, sib = (x, y, c), (x, y, 1 - c)
        chips = [(1 - x, y), (x, 1 - y), (1 - x, 1 - y)]

        def reg(t, dev):
            return _region(dsts[t], kinds[t], _dev_index(dev), shapes[t])

        def copy(t, k, block, to, src=None):
            return pltpu.make_async_remote_copy(
                src_ref=reg(t, block) if src is None else src,
                dst_ref=reg(t, block),
                send_sem=send_sems.at[t, k],
                recv_sem=recv_sems.at[t, k],
                device_id=to,
                device_id_type=MESH,
            )

        mine = [pltpu.make_async_copy(srcs[t], reg(t, me), local_sems.at[t]) for t in range(n_t)]
        first = []
        for t in range(n_t):
            mine[t].start()
            first.append(copy(t, 0, me, sib, src=srcs[t]))
            first += [copy(t, 1 + j, me, (*chip, c), src=srcs[t]) for j, chip in enumerate(chips)]
        for cp in first:
            cp.start()
        passed = []
        for t in range(n_t):
            for j, chip in enumerate(chips):
                copy(t, 1 + j, (*chip, c), me).wait_recv()
                p = copy(t, 4 + j, (*chip, c), sib)
                p.start()
                passed.append(p)
        for t in range(n_t):
            copy(t, 0, sib, me).wait_recv()
            for j, chip in enumerate(chips):
                copy(t, 4 + j, (*chip, 1 - c), me).wait_recv()
        for cp in first + passed:
            cp.wait_send()
        for cp in mine:
            cp.wait()

    hbm = pl.BlockSpec(memory_space=pltpu.HBM)
    return pl.pallas_call(
        body,
        name="all_gather_weights",
        in_specs=[hbm] * n_t,
        out_specs=[hbm] * n_t,
        out_shape=[SDS(_whole_shape(kinds[t], shapes[t]), shards[t].dtype) for t in range(n_t)],
        scratch_shapes=[
            pltpu.SemaphoreType.DMA((n_t, 7)),
            pltpu.SemaphoreType.DMA((n_t, 7)),
            pltpu.SemaphoreType.DMA((n_t,)),
        ],
    )(*shards)


def _peer(me, r):
    x, y, c = me
    return (1 - x if r & 4 else x, 1 - y if r & 2 else y, 1 - c if r & 1 else c)


def _remote(src, dst, send_sem, recv_sem, to):
    return lambda: pltpu.make_async_remote_copy(
        src_ref=src, dst_ref=dst, send_sem=send_sem, recv_sem=recv_sem, device_id=to, device_id_type=MESH
    )


def _run(pairs, locals_, start):
    if start:
        for cp in locals_:
            cp.start()
        for snd, _ in pairs:
            snd().start()
    else:
        for snd, arr in pairs:
            arr().wait_recv()
            snd().wait_send()
        for cp in locals_:
            cp.wait()


def _stage(ins, outs, alias, sems, build):
    return dict(
        ins=list(ins), outs=list(outs), alias=alias, sems=list(sems),
        start=lambda i, o, s: _run(*build(i, o, s), True),
        finish=lambda i, o, s: _run(*build(i, o, s), False),
    )


def _ag1(shards, kinds):
    n_t = len(shards)
    shapes = [tuple(sh.shape) for sh in shards]

    def build(srcs, dsts, sems):
        send, recv, loc = sems
        x, y, c = _place()
        me = (x, y, c)
        peers = [(x, y, 1 - c), (1 - x, y, c), (x, 1 - y, c), (1 - x, 1 - y, c)]
        reg = lambda t, dev: _region(dsts[t], kinds[t], _dev_index(dev), shapes[t])
        pairs = []
        for t in range(n_t):
            for k, peer in enumerate(peers):
                snd = _remote(srcs[t], reg(t, me), send.at[t, k], recv.at[t, k], peer)
                arr = _remote(reg(t, peer), reg(t, peer), send.at[t, k], recv.at[t, k], peer)
                pairs.append((snd, arr))
        mine = [pltpu.make_async_copy(srcs[t], reg(t, me), loc.at[t]) for t in range(n_t)]
        return pairs, mine

    outs = [SDS(_whole_shape(kinds[t], shapes[t]), shards[t].dtype) for t in range(n_t)]
    dma = pltpu.SemaphoreType.DMA
    return _stage(shards, outs, {}, [dma((n_t, 4)), dma((n_t, 4)), dma((n_t,))], build)


def _ag2(wholes, kinds, shapes):
    n_t = len(wholes)

    def build(_, dsts, sems):
        send, recv = sems
        x, y, c = _place()
        sib = (x, y, 1 - c)
        chips = [(1 - x, y), (x, 1 - y), (1 - x, 1 - y)]
        reg = lambda t, dev: _region(dsts[t], kinds[t], _dev_index(dev), shapes[t])
        pairs = []
        for t in range(n_t):
            for j, chip in enumerate(chips):
                snd = _remote(reg(t, (*chip, c)), reg(t, (*chip, c)), send.at[t, j], recv.at[t, j], sib)
                arr = _remote(reg(t, (*chip, 1 - c)), reg(t, (*chip, 1 - c)), send.at[t, j], recv.at[t, j], sib)
                pairs.append((snd, arr))
        return pairs, []

    outs = [SDS(w.shape, w.dtype) for w in wholes]
    dma = pltpu.SemaphoreType.DMA
    return _stage(wholes, outs, {t: t for t in range(n_t)}, [dma((n_t, 3)), dma((n_t, 3))], build)


def _chip_of(q):
    return (q >> 1, q & 1)


def _rs1(wholes, kinds, shapes):
    n_t = len(wholes)

    def build(srcs, outs, sems):
        send, recv = sems
        x, y, c = _place()
        sib = (x, y, 1 - c)
        pairs = []
        for t in range(n_t):
            for q in range(4):
                theirs = _region(srcs[t], kinds[t], _dev_index((*_chip_of(q), 1 - c)), shapes[t])
                pairs.append((
                    _remote(theirs, outs[t].at[q], send.at[t, q], recv.at[t, q], sib),
                    _remote(outs[t].at[q], outs[t].at[q], send.at[t, q], recv.at[t, q], sib),
                ))
        return pairs, []

    slabs = [SDS((4,) + tuple(shapes[t]), wholes[t].dtype) for t in range(n_t)]
    dma = pltpu.SemaphoreType.DMA
    return _stage(wholes, slabs, {}, [dma((n_t, 4)), dma((n_t, 4))], build)


def _rs2(pair_sums):
    n_t = len(pair_sums)

    def build(srcs, lands, sems):
        send, recv, loc = sems
        x, y, c = _place()
        my_chip = 2 * x + y
        pairs, mine = [], []
        for t in range(n_t):
            for j, (px, py) in enumerate([(1 - x, y), (x, 1 - y), (1 - x, 1 - y)]):
                q = 2 * px + py
                pairs.append((
                    _remote(srcs[t].at[q], lands[t].at[my_chip], send.at[t, j], recv.at[t, j], (px, py, c)),
                    _remote(lands[t].at[q], lands[t].at[q], send.at[t, j], recv.at[t, j], (px, py, c)),
                ))
            mine.append(pltpu.make_async_copy(srcs[t].at[my_chip], lands[t].at[my_chip], loc.at[t]))
        return pairs, mine

    outs = [SDS(q.shape, q.dtype) for q in pair_sums]
    dma = pltpu.SemaphoreType.DMA
    return _stage(pair_sums, outs, {}, [dma((n_t, 3)), dma((n_t, 3)), dma((n_t,))], build)


def _pair_sum(name, whole, kind, got):
    _, rows, cols = got.shape
    tr = _tile(rows, 256)
    n_r = rows // tr
    core = lax.axis_index("c").astype(jnp.int32).reshape(1)

    def body(_, a, b, o):
        o[...] = (a[...].astype(F32) + b[...].astype(F32)).astype(BF16)

    if kind == "col":
        own = pl.BlockSpec((tr, cols), lambda q, i, c_ref: (i, 2 * q + c_ref[0]))
    elif kind == "row":
        own = pl.BlockSpec((tr, cols), lambda q, i, c_ref: ((2 * q + c_ref[0]) * n_r + i, 0))
    else:
        own = pl.BlockSpec((None, tr, cols), lambda q, i, c_ref: (2 * q + c_ref[0], i, 0))
    slab = pl.BlockSpec((None, tr, cols), lambda q, i, c_ref: (q, i, 0))
    return pl.pallas_call(
        body,
        name=name,
        grid_spec=pltpu.PrefetchScalarGridSpec(
            num_scalar_prefetch=1, grid=(4, n_r), in_specs=[own, slab], out_specs=slab
        ),
        out_shape=SDS(got.shape, BF16),
        compiler_params=_params(("parallel", "parallel")),
    )(core, whole, got)


def _all_reduce_small(part):
    r_, c_ = part.shape

    def body(src, land, total, send_sems, recv_sems):
        me = _place()
        my = _dev_index(me)
        land[my] = src[...]

        def copy(r):
            peer = _peer(me, r)
            return pltpu.make_async_remote_copy(
                src_ref=src,
                dst_ref=land.at[my],
                send_sem=send_sems.at[r - 1],
                recv_sem=recv_sems.at[r - 1],
                device_id=peer,
                device_id_type=MESH,
            )

        def arrival(r):
            peer = _peer(me, r)
            slab = land.at[_dev_index(peer)]
            return pltpu.make_async_remote_copy(
                src_ref=slab,
                dst_ref=slab,
                send_sem=send_sems.at[r - 1],
                recv_sem=recv_sems.at[r - 1],
                device_id=peer,
                device_id_type=MESH,
            )

        sends = [copy(r) for r in range(1, NDEV)]
        for cp in sends:
            cp.start()
        for r in range(1, NDEV):
            arrival(r).wait_recv()
        for cp in sends:
            cp.wait_send()
        acc = land[0]
        for d in range(1, NDEV):
            acc = acc + land[d]
        total[...] = acc

    vmem = pl.BlockSpec(memory_space=pltpu.VMEM)
    return pl.pallas_call(
        body,
        name="all_reduce_small",
        in_specs=[vmem],
        out_specs=[vmem, vmem],
        out_shape=[SDS((NDEV, r_, c_), F32), SDS((r_, c_), F32)],
        scratch_shapes=[pltpu.SemaphoreType.DMA((7,)), pltpu.SemaphoreType.DMA((7,))],
    )(part)[1]


def _adamw_math(g, w, m, v):
    m2 = ADAM_B1 * m + (1.0 - ADAM_B1) * g
    v2 = ADAM_B2 * v + (1.0 - ADAM_B2) * (g * g)
    m_hat = m2 / (1.0 - ADAM_B1**ADAM_STEP)
    v_hat = v2 / (1.0 - ADAM_B2**ADAM_STEP)
    delta = -ADAM_LR * (m_hat / (jnp.sqrt(v_hat) + ADAM_EPS) + ADAM_WD * w)
    return delta, m2, v2


def _adamw_big(name, land, w, m, v):
    rows, cols = w.shape
    tr = _tile(rows, 256)
    n_slab = land.shape[0]

    def body(l_ref, w_ref, m_ref, v_ref, g_o, d_o, m_o, v_o):
        g = l_ref[0].astype(F32)
        for d in range(1, n_slab):
            g = g + l_ref[d].astype(F32)
        delta, m2, v2 = _adamw_math(g, w_ref[...], m_ref[...], v_ref[...])
        g_o[...] = g
        d_o[...] = delta
        m_o[...] = m2
        v_o[...] = v2

    blk = pl.BlockSpec((tr, cols), lambda i: (i, 0))
    return pl.pallas_call(
        body,
        name=name,
        grid=(rows // tr,),
        in_specs=[pl.BlockSpec((n_slab, tr, cols), lambda i: (0, i, 0)), blk, blk, blk],
        out_specs=[blk] * 4,
        out_shape=[SDS((rows, cols), F32)] * 4,
        compiler_params=_params(("parallel",)),
    )(land, w, m, v)


def _adamw_small(g, w, m, v):
    def body(g_ref, w_ref, m_ref, v_ref, d_o, m_o, v_o):
        delta, m2, v2 = _adamw_math(g_ref[...], w_ref[...], m_ref[...], v_ref[...])
        d_o[...] = delta
        m_o[...] = m2
        v_o[...] = v2

    vmem = pl.BlockSpec(memory_space=pltpu.VMEM)
    return pl.pallas_call(
        body,
        name="adamw_small",
        in_specs=[vmem] * 4,
        out_specs=[vmem] * 3,
        out_shape=[SDS(g.shape, F32)] * 3,
    )(g, w, m, v)


def _pack(pieces, width):
    flat = jnp.concatenate([p.reshape(-1) for p in pieces])
    rows = -(-flat.shape[0] // (8 * width)) * 8
    flat = jnp.pad(flat, (0, rows * width - flat.shape[0]))
    return flat.reshape(rows, width)


def _unpack(packed, shapes):
    flat = packed.reshape(-1)
    out, off = [], 0
    for shp in shapes:
        n = 1
        for d in shp:
            n *= d
        out.append(flat[off : off + n].reshape(shp))
        off += n
    return out


def kernel(x, p, g_mix, w_in, conv_a_w, w_out_a, b_glu, conf_dw_w, conf_dw_b, conf_ln_g, conf_ln_b, w_pw_b, b_pw_b, w_o, g_ffn, w_gate, w_up, w_down, g_ple, w_ple_gate, w_ple_proj, g_final, loss_target, m_g_mix, m_w_in, m_conv_a_w, m_w_out_a, m_b_glu, m_conf_dw_w, m_conf_dw_b, m_conf_ln_g, m_conf_ln_b, m_w_pw_b, m_b_pw_b, m_w_o, m_g_ffn, m_w_gate, m_w_up, m_w_down, m_g_ple, m_w_ple_gate, m_w_ple_proj, m_g_final, v_g_mix, v_w_in, v_conv_a_w, v_w_out_a, v_b_glu, v_conf_dw_w, v_conf_dw_b, v_conf_ln_g, v_conf_ln_b, v_w_pw_b, v_b_pw_b, v_w_o, v_g_ffn, v_w_gate, v_w_up, v_w_down, v_g_ple, v_w_ple_gate, v_w_ple_proj, v_g_final):
    s, d = x.shape[1], x.shape[2]
    c = conf_ln_g.shape[-1]
    pdim = w_ple_proj.shape[1]
    fs = w_gate.shape[-1]
    nin = NDEV * w_in.shape[-1]
    assert d == 2 * c and nin == 5 * c + 2 * d, (d, c, nin)
    x2, p2, tgt = x[0], p[0, 0], loss_target[0]
    gfin = g_final.reshape(1, d)

    kpa, kpb = 8, HALO_B
    wa_sh = jnp.pad(conv_a_w[0], ((0, kpa - CONV_A_K), (0, 0)))
    wd_sh = jnp.pad(conf_dw_w[0], ((0, kpb - CONF_K), (0, 0)))
    kind_of = dict(w_in="col", w_out_a="col", w_pw_b="col", w_ple_proj="col", w_o="row", w_ple_gate="row",
                   w_gate="blk", w_up="blk", w_down="blk")
    weight = dict(w_in=w_in, w_out_a=w_out_a, w_pw_b=w_pw_b, w_ple_proj=w_ple_proj, w_o=w_o, w_ple_gate=w_ple_gate,
                  w_gate=w_gate, w_up=w_up, w_down=w_down)
    shard_of = {nm: tuple(w.shape[1:]) for nm, w in weight.items()}
    bf16_shard = lambda nm: weight[nm][0].astype(BF16)
    kinds_ = lambda grp: [kind_of[nm] for nm in grp]
    shapes_ = lambda grp: [shard_of[nm] for nm in grp]
    win, wa, wd = _all_gather([bf16_shard("w_in"), wa_sh, wd_sh], ["col", "col", "col"])
    grp_a = ["w_out_a", "w_pw_b", "w_o", "w_gate"]
    grp_b = ["w_up", "w_ple_gate", "w_ple_proj"]
    grp_c = ["w_down"]

    tm = _tile(s, 1024)
    tn = _tile(d, 1024)
    assert (5 * c) % tn == 0 and d % tn == 0 and c % tn == 0
    ga_blk, gb_blk = (5 * c) // tn, (5 * c + d) // tn
    ij = lambda i, j, k: (i, j)
    row_i = lambda i, j, k: (i, 0)

    n1 = _rms_fwd("rms1", x2, g_mix)
    (proj,), part_a = _fmm(
        "proj", (s // tm, nin // tn, 1),
        [(n1, pl.BlockSpec((tm, d), row_i)), (win, pl.BlockSpec((d, tn), lambda i, j, k: (0, j)))],
        [(0, 1, NN, 0, None)], [(tm, tn)], [],
        [(SDS((s, nin), BF16), pl.BlockSpec((tm, tn), ij))], _ep_bf16,
        comm=_ag1([bf16_shard(nm) for nm in grp_a], kinds_(grp_a)),
    )
    ya_in = _mix_a_fwd(proj, wa, s, c)
    (v_act, u_act, cv), got = _mix_b_fwd(
        proj, b_glu, wd, conf_dw_b, conf_ln_g, conf_ln_b, s, c,
        comm=_join(_ag2(part_a, kinds_(grp_a), shapes_(grp_a)), _ag1([bf16_shard(nm) for nm in grp_b], kinds_(grp_b))),
    )
    (wouta, wpw, wo, wg), part_b = got[: len(grp_a)], got[len(grp_a) :]

    def ep_merge(accs, ex, os_):
        sa = _sigmoid(ex[0][...].astype(F32))
        sb = _sigmoid(ex[1][...].astype(F32))
        ya = accs[0]
        yb = accs[1] + ex[2][...]
        os_[0][...] = (sa * ya + sb * yb).astype(BF16)
        os_[1][...] = ya.astype(BF16)
        os_[2][...] = yb.astype(BF16)

    gate_a_spec = pl.BlockSpec((tm, tn), lambda i, j, k: (i, ga_blk + j))
    gate_b_spec = pl.BlockSpec((tm, tn), lambda i, j, k: (i, gb_blk + j))
    out_sd = (SDS((s, d), BF16), pl.BlockSpec((tm, tn), ij))
    (m_act, ya, yb), got = _fmm(
        "merge", (s // tm, d // tn, 1),
        [(ya_in, pl.BlockSpec((tm, c), row_i)), (wouta, pl.BlockSpec((c, tn), lambda i, j, k: (0, j))),
         (v_act, pl.BlockSpec((tm, c), row_i)), (wpw, pl.BlockSpec((c, tn), lambda i, j, k: (0, j)))],
        [(0, 1, NN, 0, None), (2, 3, NN, 1, None)], [(tm, tn), (tm, tn)],
        [(proj, gate_a_spec), (proj, gate_b_spec), (b_pw_b, pl.BlockSpec((1, tn), lambda i, j, k: (0, j)))],
        [out_sd, out_sd, out_sd], ep_merge, csplit=EPILOGUE_CHUNK,
        comm=_join(_ag2(part_b, kinds_(grp_b), shapes_(grp_b)), _ag1([bf16_shard(nm) for nm in grp_c], kinds_(grp_c))),
    )
    (wu, wpg, wpp), part_c = got[: len(grp_b)], got[len(grp_b) :]

    def ep_residual(accs, ex, os_):
        os_[0][...] = accs[0] + ex[0][...]

    (h1,), (wdn,) = _fmm(
        "w_o", (s // tm, d // tn, 1),
        [(m_act, pl.BlockSpec((tm, d), row_i)), (wo, pl.BlockSpec((d, tn), lambda i, j, k: (0, j)))],
        [(0, 1, NN, 0, None)], [(tm, tn)], [(x2, pl.BlockSpec((tm, tn), ij))],
        [(SDS((s, d), F32), pl.BlockSpec((tm, tn), ij))], ep_residual, csplit=EPILOGUE_CHUNK,
        comm=_ag2(part_c, kinds_(grp_c), shapes_(grp_c)),
    )
    n2 = _rms_fwd("rms2", h1, g_ffn)

    def ep_gateup(accs, ex, os_):
        g, u = accs
        os_[0][...] = g.astype(BF16)
        os_[1][...] = u.astype(BF16)
        os_[2][...] = (g * _sigmoid(g) * u).astype(BF16)

    ff_sd = (SDS((NDEV, s, fs), BF16), pl.BlockSpec((None, tm, fs), lambda i, j, k: (j, i, 0)))
    w_col_blk = pl.BlockSpec((None, d, fs), lambda i, j, k: (j, 0, 0))
    (g_act, u_ff, f_act), _ = _fmm(
        "gate_up", (s // tm, NDEV, 1),
        [(n2, pl.BlockSpec((tm, d), row_i)), (wg, w_col_blk), (wu, w_col_blk)],
        [(0, 1, NN, 0, None), (0, 2, NN, 1, None)], [(tm, fs), (tm, fs)], [],
        [ff_sd, ff_sd, ff_sd], ep_gateup, csplit=EPILOGUE_CHUNK,
    )
    pair = 2
    (h2,), _ = _fmm(
        "down", (s // tm, d // tn, NDEV // pair),
        [(f_act, pl.BlockSpec((pair, tm, fs), lambda i, j, k: (k, i, 0))),
         (wdn, pl.BlockSpec((pair, fs, tn), lambda i, j, k: (k, 0, j)))],
        [(0, 1, NN, 0, None, pair)], [(tm, tn)], [(h1, pl.BlockSpec((tm, tn), ij))],
        [(SDS((s, d), F32), pl.BlockSpec((tm, tn), ij))], ep_residual,
    )
    n3 = _rms_fwd("rms3", h2, g_ple)

    tr = _tile(s, 256)
    n_r = s // tr
    rows = lambda i, j, k: (i, 0)
    whole = lambda i, j, k: (0, 0)
    part_spec = lambda nrow: pl.BlockSpec((None, nrow, d), lambda i, j, k: (i, 0, 0))

    def ep_ple(accs, ex, os_):
        h2_, t_, gf = ex[0][...], ex[1][...], ex[2][...]
        ple = accs[0]
        s3 = _sigmoid(accs[1])
        h3 = h2_ + s3 * ple
        r = lax.rsqrt(jnp.mean(h3 * h3, axis=-1, keepdims=True) + EPS)
        hn = h3 * r
        e = hn * gf - t_
        loss = 0.5 * jnp.sum(jnp.mean(e * e, axis=-1, keepdims=True), axis=0, keepdims=True)
        dy = e * (1.0 / d)
        dn = dy * gf
        dh3 = r * (dn - hn * jnp.mean(dn * hn, axis=-1, keepdims=True))
        os_[0][...] = dh3
        os_[1][...] = (dh3 * s3).astype(BF16)
        os_[2][...] = (dh3 * ple * s3 * (1.0 - s3)).astype(BF16)
        os_[3][0:1, :] = jnp.sum(dy * hn, axis=0, keepdims=True)
        os_[3][1:2, :] = jnp.broadcast_to(loss, (1, d))

    (dh3, d_ple, d_g3, part_fin), _ = _fmm(
        "ple_loss", (n_r, 1, 1),
        [(p2, pl.BlockSpec((tr, pdim), rows)), (wpp, pl.BlockSpec((pdim, d), whole)),
         (n3, pl.BlockSpec((tr, d), rows)), (wpg, pl.BlockSpec((d, d), whole))],
        [(0, 1, NN, 0, None), (2, 3, NN, 1, None)], [(tr, d), (tr, d)],
        [(h2, pl.BlockSpec((tr, d), rows)), (tgt, pl.BlockSpec((tr, d), rows)), (gfin, pl.BlockSpec((1, d), whole))],
        [(SDS((s, d), F32), pl.BlockSpec((tr, d), rows)), (SDS((s, d), BF16), pl.BlockSpec((tr, d), rows)),
         (SDS((s, d), BF16), pl.BlockSpec((tr, d), rows)), (SDS((n_r, 2, d), F32), part_spec(2))],
        ep_ple,
    )

    g_wpp = _mm_tn("d_w_ple_proj", p2, d_ple)
    g_wpg = _mm_tn("d_w_ple_gate", n3, d_g3)

    def ep_norm_bwd(accs, ex, os_):
        dh, dg = _rms_bwd(accs[0], ex[0][...], ex[2][...])
        dh = ex[1][...] + dh
        os_[0][...] = dh
        os_[1][...] = dh.astype(BF16)
        os_[2][...] = dg

    norm_outs = lambda t: [
        (SDS((s, d), F32), pl.BlockSpec((t, d), rows)), (SDS((s, d), BF16), pl.BlockSpec((t, d), rows)),
        (SDS((s // t, 1, d), F32), part_spec(1)),
    ]
    def exchange1(names, wholes):
        return _rs1(wholes, kinds_(names), shapes_(names))

    def pair_sums(names, wholes, got):
        return [_pair_sum("pair_sum_" + nm, wholes[t], kind_of[nm], got[t]) for t, nm in enumerate(names)]

    lands = {}
    grp1 = ["w_ple_proj", "w_ple_gate"]
    (dh2, dh2b, part_ple), got = _fmm(
        "d_n3", (n_r, 1, 1),
        [(d_g3, pl.BlockSpec((tr, d), rows)), (wpg, pl.BlockSpec((d, d), whole))],
        [(0, 1, NT, 0, None)], [(tr, d)],
        [(h2, pl.BlockSpec((tr, d), rows)), (dh3, pl.BlockSpec((tr, d), rows)), (g_ple, pl.BlockSpec((1, d), whole))],
        norm_outs(tr), ep_norm_bwd,
        comm=exchange1(grp1, [g_wpp, g_wpg]),
    )
    sums1 = pair_sums(grp1, [g_wpp, g_wpg], got)

    def ep_ddown(accs, ex, os_):
        g = ex[0][...].astype(F32)
        u = ex[1][...].astype(F32)
        sg = _sigmoid(g)
        df = accs[0]
        os_[0][...] = (df * u * sg * (1.0 + g * (1.0 - sg))).astype(BF16)
        os_[1][...] = (df * g * sg).astype(BF16)

    ff_in = pl.BlockSpec((None, tm, fs), lambda i, j, k: (j, i, 0))
    (d_g, d_u), got = _fmm(
        "d_down", (s // tm, NDEV, 1),
        [(dh2b, pl.BlockSpec((tm, d), row_i)), (wdn, pl.BlockSpec((None, fs, d), lambda i, j, k: (j, 0, 0)))],
        [(0, 1, NT, 0, None)], [(tm, fs)], [(g_act, ff_in), (u_ff, ff_in)],
        [ff_sd, ff_sd], ep_ddown, csplit=EPILOGUE_CHUNK,
        comm=_rs2(sums1),
    )
    lands.update(zip(grp1, got))
    tk = _tile(s, 1024)
    (g_wdn,), _ = _fmm(
        "d_w_down", (NDEV, 1, s // tk),
        [(f_act, pl.BlockSpec((None, tk, fs), lambda i, j, k: (i, k, 0))), (dh2b, pl.BlockSpec((tk, d), lambda i, j, k: (k, 0)))],
        [(0, 1, TN, 0, None)], [(fs, d)], [],
        [(SDS((NDEV, fs, d), BF16), pl.BlockSpec((None, fs, d), lambda i, j, k: (i, 0, 0)))], _ep_bf16,
    )

    def ep_two_bf16(accs, ex, os_):
        os_[0][...] = accs[0].astype(BF16)
        os_[1][...] = accs[1].astype(BF16)

    ff_k = pl.BlockSpec((None, tk, fs), lambda i, j, k: (i, k, 0))
    wcol_sd = (SDS((NDEV, d, fs), BF16), pl.BlockSpec((None, d, fs), lambda i, j, k: (i, 0, 0)))
    grp2 = ["w_down"]
    (g_wg, g_wu), got = _fmm(
        "d_w_gate_up", (NDEV, 1, s // tk),
        [(n2, pl.BlockSpec((tk, d), lambda i, j, k: (k, 0))), (d_g, ff_k), (d_u, ff_k)],
        [(0, 1, TN, 0, None), (0, 2, TN, 1, None)], [(d, fs), (d, fs)], [],
        [wcol_sd, wcol_sd], ep_two_bf16,
        comm=exchange1(grp2, [g_wdn]),
    )
    sums2 = pair_sums(grp2, [g_wdn], got)
    grp3 = ["w_gate", "w_up"]
    th = _tile(s // 2, 1024)
    ff_a = pl.BlockSpec((None, th, fs), lambda i, j, k: (k, i, 0))
    w_k = pl.BlockSpec((None, d, fs), lambda i, j, k: (k, 0, 0))
    (d_n2,), got = _fmm(
        "d_n2", (s // th, 1, NDEV),
        [(d_g, ff_a), (wg, w_k), (d_u, ff_a), (wu, w_k)],
        [(0, 1, NT, 0, None), (2, 3, NT, 0, None)], [(th, d)], [],
        [(SDS((s, d), BF16), pl.BlockSpec((th, d), rows))], _ep_bf16,
        comm=_join(_rs2(sums2), exchange1(grp3, [g_wg, g_wu])),
    )
    lands.update(zip(grp2, got[:1]))
    sums3 = pair_sums(grp3, [g_wg, g_wu], got[1:])
    dh1, dh1b, part_ffn = _norm_bwd("d_h1", d_n2, h1, dh2, g_ffn, True)
    g_wo = _mm_tn("d_w_o", m_act, dh1b)

    def ep_dm(accs, ex, os_):
        ya_, yb_ = ex[0][...].astype(F32), ex[1][...].astype(F32)
        sa = _sigmoid(ex[2][...].astype(F32))
        sb = _sigmoid(ex[3][...].astype(F32))
        dm = accs[0]
        d_yb = dm * sb
        os_[0][...] = (dm * sa).astype(BF16)
        os_[1][...] = d_yb.astype(BF16)
        os_[2][...] = (dm * ya_ * sa * (1.0 - sa)).astype(BF16)
        os_[3][...] = (dm * yb_ * sb * (1.0 - sb)).astype(BF16)
        os_[4][...] = jnp.sum(d_yb, axis=0, keepdims=True)

    tile_ij = pl.BlockSpec((tm, tn), ij)
    grp4 = ["w_o"]
    (d_ya, d_yb, d_ga, d_gb, part_bpw), got = _fmm(
        "d_merge", (s // tm, d // tn, 1),
        [(dh1b, pl.BlockSpec((tm, d), row_i)), (wo, pl.BlockSpec((tn, d), lambda i, j, k: (j, 0)))],
        [(0, 1, NT, 0, None)], [(tm, tn)],
        [(ya, tile_ij), (yb, tile_ij), (proj, gate_a_spec), (proj, gate_b_spec)],
        [out_sd, out_sd, out_sd, out_sd,
         (SDS((s // tm, 1, d), F32), pl.BlockSpec((None, 1, tn), lambda i, j, k: (i, 0, j)))],
        ep_dm, csplit=EPILOGUE_CHUNK,
        comm=exchange1(grp4, [g_wo]),
    )
    sums4 = pair_sums(grp4, [g_wo], got)
    g_wouta = _mm_tn("d_w_out_a", ya_in, d_ya)
    g_wpw = _mm_tn("d_w_pw_b", v_act, d_yb)
    grp5 = ["w_out_a", "w_pw_b"]
    d_ya_in, got = _mm_nt("d_ya_in", d_ya, wouta, comm=exchange1(grp5, [g_wouta, g_wpw]))
    sums5 = pair_sums(grp5, [g_wouta, g_wpw], got)
    d_v, _ = _mm_nt("d_v", d_yb, wpw)
    d_cv, part_ln = _mix_b_bwd1(d_v, cv, conf_ln_g, conf_ln_b, s, c)
    (d_b, part_wd, part_bglu), got = _mix_b_bwd2(d_cv, u_act, proj, b_glu, wd, s, c, comm=_rs2(sums3))
    lands.update(zip(grp3, got))
    d_a, part_wa = _mix_a_bwd(d_ya_in, proj, wa, s, c)

    nb = nin // c
    gblk = d // c
    lo = [0, 3, 5, 5 + gblk]
    hi = [3, 5, 5 + gblk, 5 + 2 * gblk]
    pieces = [d_a, d_b, d_ga, d_gb]

    def active(q, ax):
        return lambda ids: jnp.logical_and(ids[ax] >= lo[q], ids[ax] < hi[q])

    def piece_spec(q, rows_, ax, row0=0):
        def index(i, j, k):
            ids = (i, j, k)
            col = jnp.clip(ids[ax] - lo[q], 0, hi[q] - lo[q] - 1)
            row = i + row0 if ax == 2 else jnp.where(active(q, ax)(ids), k, 0)
            return (row, col)

        return pl.BlockSpec((rows_, c), index)

    tkw = _tile(s, 1024)
    (g_win,), got = _fmm(
        "d_w_in", (1, nb, s // tkw),
        [(n1, pl.BlockSpec((tkw, d), lambda i, j, k: (k, 0)))]
        + [(pieces[q], piece_spec(q, tkw, 1)) for q in range(4)],
        [(0, 1 + q, TN, 0, active(q, 1)) for q in range(4)], [(d, c)], [],
        [(SDS((d, nin), BF16), pl.BlockSpec((d, c), lambda i, j, k: (0, j)))], _ep_bf16,
        comm=_rs2(sums4 + sums5),
    )
    lands.update(zip(grp4 + grp5, got))

    grp6 = ["w_in"]
    n_half = (s // th) // 2

    def d_n1_rows(name, row0, n_tiles, comm, into):
        return _fmm(
            name, (n_tiles, 1, nb),
            [(pieces[q], piece_spec(q, th, 2, row0)) for q in range(4)]
            + [(win, pl.BlockSpec((d, c), lambda i, j, k: (0, k)))],
            [(q, 4, NT, 0, active(q, 2)) for q in range(4)], [(th, d)], [],
            [(SDS((s, d), BF16), pl.BlockSpec((th, d), lambda i, j, k: (i + row0, 0)))], _ep_bf16,
            comm=comm, into=into,
        )

    (d_n1,), got = d_n1_rows("d_n1_a", 0, n_half, exchange1(grp6, [g_win]), None)
    (d_n1,), got = d_n1_rows("d_n1_b", n_half, s // th - n_half, _rs2(pair_sums(grp6, [g_win], got)), d_n1)
    lands.update(zip(grp6, got))
    dx, part_mix = _norm_bwd("d_x", d_n1, x2, dh1, g_mix, False)

    small_parts = [
        jnp.sum(part_mix, axis=0),
        jnp.sum(part_bglu, axis=0),
        jnp.sum(part_ln[:, 2], axis=0),
        jnp.sum(part_ln[:, 0], axis=0),
        jnp.sum(part_ln[:, 1], axis=0),
        jnp.sum(part_bpw, axis=0),
        jnp.sum(part_ffn, axis=0),
        jnp.sum(part_ple, axis=0),
        jnp.sum(part_fin[:, 0], axis=0),
        jnp.sum(part_wa, axis=0),
        jnp.sum(part_wd, axis=0),
        jnp.broadcast_to(jnp.sum(part_fin[:, 1, 0]), (c,)),
    ]
    small_shapes = [(1, d), (1, 2 * c), (1, c), (1, c), (1, c), (1, d), (1, d), (1, d), (d,), (kpa, c), (kpb, c), (c,)]
    total = _all_reduce_small(_pack(small_parts, c))
    (gr_g_mix, gr_b_glu, gr_dw_b, gr_ln_g, gr_ln_b, gr_b_pw, gr_g_ffn, gr_g_ple, gr_g_final, gr_wa, gr_wd, loss_row) = _unpack(total, small_shapes)
    loss = loss_row[0]
    my = _dev_index(_place())
    csh = conv_a_w.shape[-1]
    gr_conv_a = lax.dynamic_slice_in_dim(gr_wa[:CONV_A_K], my * csh, csh, axis=1)[None]
    gr_conf_dw = lax.dynamic_slice_in_dim(gr_wd[:CONF_K], my * csh, csh, axis=1)[None]

    big_m = dict(w_in=m_w_in, w_out_a=m_w_out_a, w_pw_b=m_w_pw_b, w_ple_proj=m_w_ple_proj, w_o=m_w_o,
                 w_ple_gate=m_w_ple_gate, w_gate=m_w_gate, w_up=m_w_up, w_down=m_w_down)
    big_v = dict(w_in=v_w_in, w_out_a=v_w_out_a, w_pw_b=v_w_pw_b, w_ple_proj=v_w_ple_proj, w_o=v_w_o,
                 w_ple_gate=v_w_ple_gate, w_gate=v_w_gate, w_up=v_w_up, w_down=v_w_down)
    big_out = {}
    for nm in weight:
        res = _adamw_big("adamw_" + nm, lands[nm], weight[nm][0], big_m[nm][0], big_v[nm][0])
        big_out[nm] = [r[None] for r in res]

    small_names = ["g_mix", "conv_a_w", "b_glu", "conf_dw_w", "conf_dw_b", "conf_ln_g", "conf_ln_b", "b_pw_b", "g_ffn", "g_ple", "g_final"]
    small_g = [gr_g_mix, gr_conv_a, gr_b_glu, gr_conf_dw, gr_dw_b, gr_ln_g, gr_ln_b, gr_b_pw, gr_g_ffn, gr_g_ple, gr_g_final]
    small_w = [g_mix, conv_a_w, b_glu, conf_dw_w, conf_dw_b, conf_ln_g, conf_ln_b, b_pw_b, g_ffn, g_ple, g_final]
    small_m = [m_g_mix, m_conv_a_w, m_b_glu, m_conf_dw_w, m_conf_dw_b, m_conf_ln_g, m_conf_ln_b, m_b_pw_b, m_g_ffn, m_g_ple, m_g_final]
    small_v = [v_g_mix, v_conv_a_w, v_b_glu, v_conf_dw_w, v_conf_dw_b, v_conf_ln_g, v_conf_ln_b, v_b_pw_b, v_g_ffn, v_g_ple, v_g_final]
    shp = [tuple(w.shape) for w in small_w]
    small_g = [g.reshape(sh) for g, sh in zip(small_g, shp)]
    sd, sm, sv = _adamw_small(_pack(small_g, 128), _pack(small_w, 128), _pack(small_m, 128), _pack(small_v, 128))
    small_out = {}
    for nm, g, dl, mm, vv in zip(small_names, small_g, _unpack(sd, shp), _unpack(sm, shp), _unpack(sv, shp)):
        small_out[nm] = [g, dl, mm, vv]

    order = ["g_mix", "w_in", "conv_a_w", "w_out_a", "b_glu", "conf_dw_w", "conf_dw_b", "conf_ln_g", "conf_ln_b", "w_pw_b", "b_pw_b", "w_o", "g_ffn", "w_gate", "w_up", "w_down", "g_ple", "w_ple_gate", "w_ple_proj", "g_final"]
    allo = {**big_out, **small_out}
    outs = [loss, dx[None]]
    for q in range(4):
        outs += [allo[nm][q] for nm in order]
    return tuple(outs)
```

```python
import jax
import jax.numpy as jnp
from jax import lax
from jax.experimental import pallas as pl
from jax.experimental.pallas import tpu as pltpu

F32, BF16 = jnp.float32, jnp.bfloat16
EPS, LN_EPS = 1e-6, 1e-5
ADAM_LR, ADAM_B1, ADAM_B2, ADAM_EPS, ADAM_WD, ADAM_STEP = 0.001, 0.9, 0.999, 1e-08, 0.01, 10
CONV_A_K, CONF_K = 3, 31
NDEV = 8
NN = (((1,), (0,)), ((), ()))
NT = (((1,), (1,)), ((), ()))
TN = (((0,), (0,)), ((), ()))
V7X_VMEM_LIMIT_BYTES = 56 * 1024 * 1024
MESH = pl.DeviceIdType.MESH
SDS = jax.ShapeDtypeStruct
HALO_A, HALO_B = 16, 32
EPILOGUE_CHUNK = 256
CONV_ROWS = 32
CONF_ROWS = 16


def _tile(n, pref):
    t = min(n, pref)
    while n % t:
        t -= 8
    return t


def _sigmoid(x):
    return jax.nn.sigmoid(x)


def _params(sem=None):
    return pltpu.CompilerParams(vmem_limit_bytes=V7X_VMEM_LIMIT_BYTES, dimension_semantics=sem)


def _edge(grid, last):
    cond = None
    for ax, n in enumerate(grid):
        here = pl.program_id(ax) == (n - 1 if last else 0)
        cond = here if cond is None else jnp.logical_and(cond, here)
    return cond


def _join(*comms):
    ins, outs, alias, sems, spans = [], [], {}, [], []
    for cm in comms:
        spans.append((len(ins), len(outs), len(sems)))
        for i, o in cm["alias"].items():
            alias[len(ins) + i] = len(outs) + o
        ins += cm["ins"]
        outs += cm["outs"]
        sems += cm["sems"]

    def run(which):
        def f(i_refs, o_refs, s_refs):
            for cm, (a, b, c_) in zip(comms, spans):
                cm[which](
                    i_refs[a : a + len(cm["ins"])], o_refs[b : b + len(cm["outs"])], s_refs[c_ : c_ + len(cm["sems"])]
                )

        return f

    return dict(ins=ins, outs=outs, alias=alias, sems=sems, start=run("start"), finish=run("finish"))


def _call(body, name, grid, in_specs, args, out_specs, out_shape, scratch=(), sem=None, comm=None, alias=None):
    n_in, n_out, n_s = len(args), len(out_shape), len(scratch)
    alias = dict(alias or {})
    if comm is None:
        res = pl.pallas_call(
            body, name=name, grid=grid, in_specs=list(in_specs), out_specs=list(out_specs), out_shape=list(out_shape),
            scratch_shapes=list(scratch), input_output_aliases=alias, compiler_params=_params(sem),
        )(*args)
        return list(res), []
    n_ci, n_co = len(comm["ins"]), len(comm["outs"])

    def wrapped(*refs):
        ins = refs[:n_in]
        ci = refs[n_in : n_in + n_ci]
        o0 = n_in + n_ci
        outs = refs[o0 : o0 + n_out]
        co = refs[o0 + n_out : o0 + n_out + n_co]
        s0 = o0 + n_out + n_co
        sc = refs[s0 : s0 + n_s]
        cs = refs[s0 + n_s :]
        pl.when(_edge(grid, False))(lambda: comm["start"](ci, co, cs))
        body(*ins, *outs, *sc)
        pl.when(_edge(grid, True))(lambda: comm["finish"](ci, co, cs))

    hbm = pl.BlockSpec(memory_space=pl.ANY)
    res = pl.pallas_call(
        wrapped,
        name=name,
        grid=grid,
        in_specs=list(in_specs) + [hbm] * n_ci,
        out_specs=list(out_specs) + [hbm] * n_co,
        out_shape=list(out_shape) + list(comm["outs"]),
        scratch_shapes=list(scratch) + list(comm["sems"]),
        input_output_aliases={**alias, **{n_in + i: n_out + o for i, o in comm["alias"].items()}},
        compiler_params=_params(("arbitrary",) * len(grid)),
    )(*args, *comm["ins"])
    return list(res[:n_out]), list(res[n_out:])


def _col_chunks(n, pref):
    out, c0 = [], 0
    while c0 < n:
        w = min(pref, n - c0)
        out.append((c0, w))
        c0 += w
    return out


def _fmm(name, grid, operands, terms, acc_shapes, extras, outs, epilogue, comm=None, csplit=None, into=None):
    n_p, n_e, n_o, n_a = len(operands), len(extras), len(outs), len(acc_shapes)
    nk = grid[-1]
    kax = len(grid) - 1
    simple = nk == 1 and all(t[4] is None for t in terms)
    alias = None
    if into is not None:
        extras = list(extras) + [(into, pl.BlockSpec(memory_space=pl.ANY))]
        alias = {n_p + n_e: 0}
        n_e += 1
    if csplit is not None:
        assert simple and into is None and all(t[2] in (NN, NT) and (len(t) <= 5 or not t[5]) for t in terms)
        tn_ = acc_shapes[0][1]
        chunks = _col_chunks(tn_, csplit)

    def dot(a, b, dims):
        if a.dtype != BF16:
            a = a.astype(BF16)
        if b.dtype != BF16:
            b = b.astype(BF16)
        return lax.dot_general(a, b, dims, preferred_element_type=F32)

    def value(refs, term):
        slabs = term[5] if len(term) > 5 else 0
        if not slabs:
            return dot(refs[term[0]][...], refs[term[1]][...], term[2])
        tot = None
        for sl in range(slabs):
            d = dot(refs[term[0]][sl], refs[term[1]][sl], term[2])
            tot = d if tot is None else tot + d
        return tot

    def always(refs):
        parts = [None] * n_a
        for term in terms:
            if term[4] is None:
                d = value(refs, term)
                parts[term[3]] = d if parts[term[3]] is None else parts[term[3]] + d
        return parts

    def chunked(refs, ex, os_):
        cols = lambda ref, c0, w: ref.at[:, pl.ds(c0, w)] if ref.shape[-1] == tn_ else ref
        for c0, w in chunks:
            parts = [None] * n_a
            for term in terms:
                b_ref = refs[term[1]]
                b = b_ref[:, pl.ds(c0, w)] if term[2] == NN else b_ref[pl.ds(c0, w), :]
                d = dot(refs[term[0]][...], b, term[2])
                parts[term[3]] = d if parts[term[3]] is None else parts[term[3]] + d
            epilogue(parts, [cols(e, c0, w) for e in ex], [cols(o, c0, w) for o in os_])

    def body(*refs):
        ex = refs[n_p : n_p + n_e]
        os_ = refs[n_p + n_e : n_p + n_e + n_o]
        accs = refs[n_p + n_e + n_o :]
        if simple and csplit is not None:
            chunked(refs, ex, os_)
            return
        if simple:
            epilogue(always(refs), ex, os_)
            return
        ids = [pl.program_id(ax) for ax in range(len(grid))]
        k = ids[kax]

        @pl.when(k == 0)
        def _():
            for acc in accs:
                acc[...] = jnp.zeros(acc.shape, F32)

        for ai, part in enumerate(always(refs)):
            if part is not None:
                accs[ai][...] += part
        for term in terms:
            if term[4] is not None:

                def add(term=term):
                    accs[term[3]][...] += value(refs, term)

                pl.when(term[4](ids))(add)

        @pl.when(k == nk - 1)
        def _():
            epilogue([acc[...] for acc in accs], ex, os_)

    return _call(
        body,
        name,
        grid,
        [o[1] for o in operands] + [e[1] for e in extras],
        [o[0] for o in operands] + [e[0] for e in extras],
        [o[1] for o in outs],
        [o[0] for o in outs],
        scratch=[] if simple else [pltpu.VMEM(s, F32) for s in acc_shapes],
        sem=("parallel",) * kax + ("arbitrary",),
        comm=comm,
        alias=alias,
    )


def _rms_bwd(dn_raw, h, g):
    r = lax.rsqrt(jnp.mean(h * h, axis=-1, keepdims=True) + EPS)
    hn = h * r
    dg = jnp.sum(dn_raw * hn, axis=0, keepdims=True)
    dn = dn_raw * g
    dh = r * (dn - hn * jnp.mean(dn * hn, axis=-1, keepdims=True))
    return dh, dg


def _rms_fwd(name, h, g):
    s, d = h.shape
    ts = _tile(s, 512)

    def body(h_ref, g_ref, o_ref):
        x = h_ref[...]
        r = lax.rsqrt(jnp.mean(x * x, axis=-1, keepdims=True) + EPS)
        o_ref[...] = (x * r * g_ref[...]).astype(BF16)

    return pl.pallas_call(
        body,
        name=name,
        grid=(s // ts,),
        in_specs=[pl.BlockSpec((ts, d), lambda i: (i, 0)), pl.BlockSpec((1, d), lambda i: (0, 0))],
        out_specs=pl.BlockSpec((ts, d), lambda i: (i, 0)),
        out_shape=SDS((s, d), BF16),
        compiler_params=_params(("parallel",)),
    )(h, g)


def _norm_bwd(name, dn, h, dres, g, want_bf16):
    s, d = h.shape
    ts = _tile(s, 512)

    def body(dn_r, h_r, dres_r, g_r, *outs):
        dh, dg = _rms_bwd(dn_r[...].astype(F32), h_r[...], g_r[...])
        dh = dres_r[...] + dh
        outs[0][...] = dh
        if want_bf16:
            outs[1][...] = dh.astype(BF16)
        outs[-1][...] = dg

    blk = pl.BlockSpec((ts, d), lambda i: (i, 0))
    part = pl.BlockSpec((None, 1, d), lambda i: (i, 0, 0))
    return pl.pallas_call(
        body,
        name=name,
        grid=(s // ts,),
        in_specs=[blk, blk, blk, pl.BlockSpec((1, d), lambda i: (0, 0))],
        out_specs=[blk] + ([blk] if want_bf16 else []) + [part],
        out_shape=[SDS((s, d), F32)] + ([SDS((s, d), BF16)] if want_bf16 else []) + [SDS((s // ts, 1, d), F32)],
        compiler_params=_params(("parallel",)),
    )(dn, h, dres, g)


def _prev_halo(ts, hb):
    r = ts // hb
    return lambda i: jnp.maximum(i * r - 1, 0)


def _next_halo(ts, hb, s):
    r = ts // hb
    last = s // hb - 1
    return lambda i: jnp.minimum((i + 1) * r, last)


def _shift_copies(buf, sh):
    n = sh.shape[1]
    for j in range(1, 8):
        sh[j - 1, pl.ds(0, n), :] = buf[pl.ds(j, n), :]


def _tap(buf, sh, r0, off, rows):
    j = off % 8
    start = pl.multiple_of(r0 + (off - j), 8)
    if j == 0:
        return buf[pl.ds(start, rows), :]
    return sh[j - 1, pl.ds(start, rows), :]


def _mix_a_fwd(proj, wa, s, c):
    ts, hb = _tile(s, 256), HALO_A
    prev = _prev_halo(ts, hb)

    def body(ah, ab, ac, hh, hc, w, o, buf):
        i = pl.program_id(0)
        zh = hc[...].astype(F32) * hh[...].astype(F32)
        buf[pl.ds(0, hb), :] = jnp.where(i == 0, 0.0, zh)
        buf[pl.ds(hb, ts), :] = ac[...].astype(F32) * ah[...].astype(F32)
        for r0 in range(0, ts, CONV_ROWS):
            cz = jnp.zeros((CONV_ROWS, c), F32)
            for k in range(CONV_A_K):
                cz = cz + w[k : k + 1, :] * buf[pl.ds(hb + r0 - (CONV_A_K - 1) + k, CONV_ROWS), :]
            o[pl.ds(r0, CONV_ROWS), :] = (ab[pl.ds(r0, CONV_ROWS), :].astype(F32) * cz).astype(BF16)

    main = lambda cb: pl.BlockSpec((ts, c), lambda i: (i, cb))
    halo = lambda cb: pl.BlockSpec((hb, c), lambda i: (prev(i), cb))
    return pl.pallas_call(
        body,
        name="mix_a_fwd",
        grid=(s // ts,),
        in_specs=[main(0), main(1), main(2), halo(0), halo(2), pl.BlockSpec(wa.shape, lambda i: (0, 0))],
        out_specs=pl.BlockSpec((ts, c), lambda i: (i, 0)),
        out_shape=SDS((s, c), BF16),
        scratch_shapes=[pltpu.VMEM((hb + ts, c), F32)],
        compiler_params=_params(("parallel",)),
    )(proj, proj, proj, proj, proj, wa)


def _mix_b_fwd(proj, b_glu, wd, bd, lg, lb, s, c, comm=None):
    ts, hb = _tile(s, 256), HALO_B
    prev = _prev_halo(ts, hb)

    def body(gv, gg, hv, hg, bglu, w, bd_r, lg_r, lb_r, v_o, u_o, cv_o, buf, sh):
        i = pl.program_id(0)
        bv, bg = bglu[:, 0:c], bglu[:, c : 2 * c]
        uh = (hv[...].astype(F32) + bv) * _sigmoid(hg[...].astype(F32) + bg)
        buf[pl.ds(0, hb), :] = jnp.where(i == 0, 0.0, uh)
        u = (gv[...].astype(F32) + bv) * _sigmoid(gg[...].astype(F32) + bg)
        buf[pl.ds(hb, ts), :] = u
        u_o[...] = u.astype(BF16)
        _shift_copies(buf, sh)

        def chunk(ci, carry):
            r0 = pl.multiple_of(ci * CONF_ROWS, CONF_ROWS)
            acc = jnp.zeros((CONF_ROWS, c), F32)
            for k in range(CONF_K):
                acc = acc + w[k : k + 1, :] * _tap(buf, sh, r0, hb - (CONF_K - 1) + k, CONF_ROWS)
            cv_o[pl.ds(r0, CONF_ROWS), :] = acc + bd_r[...]
            return carry

        lax.fori_loop(0, ts // CONF_ROWS, chunk, 0)
        cv = cv_o[...]
        mu = jnp.mean(cv, axis=-1, keepdims=True)
        xc = cv - mu
        rs = lax.rsqrt(jnp.mean(xc * xc, axis=-1, keepdims=True) + LN_EPS)
        ln = xc * rs * lg_r[...] + lb_r[...]
        v_o[...] = (ln * _sigmoid(ln)).astype(BF16)

    main = lambda cb: pl.BlockSpec((ts, c), lambda i: (i, cb))
    halo = lambda cb: pl.BlockSpec((hb, c), lambda i: (prev(i), cb))
    full = lambda a: pl.BlockSpec(a.shape, lambda i: (0, 0))
    out = pl.BlockSpec((ts, c), lambda i: (i, 0))
    return _call(
        body,
        "mix_b_fwd",
        (s // ts,),
        [main(3), main(4), halo(3), halo(4), full(b_glu), full(wd), full(bd), full(lg), full(lb)],
        [proj, proj, proj, proj, b_glu, wd, bd, lg, lb],
        [out, out, out],
        [SDS((s, c), BF16), SDS((s, c), BF16), SDS((s, c), F32)],
        scratch=[pltpu.VMEM((hb + ts, c), F32), pltpu.VMEM((7, hb + ts - 8, c), F32)],
        sem=("parallel",),
        comm=comm,
    )


def _mix_b_bwd1(d_v, cv, lg, lb, s, c):
    ts = _tile(s, 256)

    def body(dv_r, cv_r, lg_r, lb_r, dcv_o, part_o):
        cv_ = cv_r[...]
        mu = jnp.mean(cv_, axis=-1, keepdims=True)
        xc = cv_ - mu
        rs = lax.rsqrt(jnp.mean(xc * xc, axis=-1, keepdims=True) + LN_EPS)
        xh = xc * rs
        ln = xh * lg_r[...] + lb_r[...]
        sg = _sigmoid(ln)
        d_ln = dv_r[...].astype(F32) * (sg * (1.0 + ln * (1.0 - sg)))
        dy = d_ln * lg_r[...]
        d_cv = rs * (dy - jnp.mean(dy, axis=-1, keepdims=True) - xh * jnp.mean(dy * xh, axis=-1, keepdims=True))
        dcv_o[...] = d_cv
        part_o[0:1, :] = jnp.sum(d_ln * xh, axis=0, keepdims=True)
        part_o[1:2, :] = jnp.sum(d_ln, axis=0, keepdims=True)
        part_o[2:3, :] = jnp.sum(d_cv, axis=0, keepdims=True)

    blk = pl.BlockSpec((ts, c), lambda i: (i, 0))
    full = lambda a: pl.BlockSpec(a.shape, lambda i: (0, 0))
    return pl.pallas_call(
        body,
        name="mix_b_bwd_ln",
        grid=(s // ts,),
        in_specs=[blk, blk, full(lg), full(lb)],
        out_specs=[blk, pl.BlockSpec((None, 3, c), lambda i: (i, 0, 0))],
        out_shape=[SDS((s, c), F32), SDS((s // ts, 3, c), F32)],
        compiler_params=_params(("parallel",)),
    )(d_v, cv, lg, lb)


def _mix_b_bwd2(d_cv, u, proj, b_glu, wd, s, c, comm=None):
    ts, hb = _tile(s, 256), HALO_B
    prev, nxt = _prev_halo(ts, hb), _next_halo(ts, hb, s)
    n_t = s // ts
    kp = wd.shape[0]

    def body(dcv, dcv_n, u_m, u_p, gv, gg, bglu, w, d_o, dwd_o, dbglu_o, dbuf, ubuf, dub, dsh, ush, dwacc):
        i = pl.program_id(0)
        dbuf[pl.ds(0, ts), :] = dcv[...]
        dbuf[pl.ds(ts, hb), :] = jnp.where(i == n_t - 1, 0.0, dcv_n[...])
        ubuf[pl.ds(0, hb), :] = jnp.where(i == 0, 0.0, u_p[...].astype(F32))
        ubuf[pl.ds(hb, ts), :] = u_m[...].astype(F32)
        _shift_copies(dbuf, dsh)
        _shift_copies(ubuf, ush)
        dwacc[...] = jnp.zeros(dwacc.shape, F32)

        def chunk(ci, carry):
            r0 = pl.multiple_of(ci * CONF_ROWS, CONF_ROWS)
            acc = jnp.zeros((CONF_ROWS, c), F32)
            dc = dbuf[pl.ds(r0, CONF_ROWS), :]
            for k in range(CONF_K):
                acc = acc + w[k : k + 1, :] * _tap(dbuf, dsh, r0, (CONF_K - 1) - k, CONF_ROWS)
                prod = dc * _tap(ubuf, ush, r0, hb - (CONF_K - 1) + k, CONF_ROWS)
                fold = prod[0:8]
                for a in range(1, CONF_ROWS // 8):
                    fold = fold + prod[8 * a : 8 * a + 8]
                dwacc[pl.ds(8 * k, 8), :] += fold
            dub[pl.ds(r0, CONF_ROWS), :] = acc
            return carry

        lax.fori_loop(0, ts // CONF_ROWS, chunk, 0)
        for k in range(CONF_K):
            dwd_o[k : k + 1, :] = jnp.sum(dwacc[pl.ds(8 * k, 8), :], axis=0, keepdims=True)
        dwd_o[CONF_K:kp, :] = jnp.zeros((kp - CONF_K, c), F32)
        bv, bg = bglu[:, 0:c], bglu[:, c : 2 * c]
        d_u = dub[...]
        sg = _sigmoid(gg[...].astype(F32) + bg)
        d_gv = d_u * sg
        d_gg = d_u * (gv[...].astype(F32) + bv) * sg * (1.0 - sg)
        d_o[:, 0:c] = d_gv.astype(BF16)
        d_o[:, c : 2 * c] = d_gg.astype(BF16)
        dbglu_o[:, 0:c] = jnp.sum(d_gv, axis=0, keepdims=True)
        dbglu_o[:, c : 2 * c] = jnp.sum(d_gg, axis=0, keepdims=True)

    blk = lambda cb: pl.BlockSpec((ts, c), lambda i: (i, cb))
    full = lambda a: pl.BlockSpec(a.shape, lambda i: (0, 0))
    return _call(
        body,
        "mix_b_bwd_conv",
        (n_t,),
        [
            blk(0),
            pl.BlockSpec((hb, c), lambda i: (nxt(i), 0)),
            blk(0),
            pl.BlockSpec((hb, c), lambda i: (prev(i), 0)),
            blk(3),
            blk(4),
            full(b_glu),
            full(wd),
        ],
        [d_cv, d_cv, u, u, proj, proj, b_glu, wd],
        [
            pl.BlockSpec((ts, 2 * c), lambda i: (i, 0)),
            pl.BlockSpec((None, kp, c), lambda i: (i, 0, 0)),
            pl.BlockSpec((None, 1, 2 * c), lambda i: (i, 0, 0)),
        ],
        [SDS((s, 2 * c), BF16), SDS((n_t, kp, c), F32), SDS((n_t, 1, 2 * c), F32)],
        scratch=[
            pltpu.VMEM((ts + hb, c), F32), pltpu.VMEM((hb + ts, c), F32), pltpu.VMEM((ts, c), F32),
            pltpu.VMEM((7, hb + ts - 8, c), F32), pltpu.VMEM((7, hb + ts - 8, c), F32), pltpu.VMEM((8 * CONF_K, c), F32),
        ],
        sem=("parallel",),
        comm=comm,
    )


def _mix_a_bwd(d_ya, proj, wa, s, c):
    ts, hb = _tile(s, 256), HALO_A
    prev, nxt = _prev_halo(ts, hb), _next_halo(ts, hb, s)
    n_t = s // ts
    kp = wa.shape[0]

    def body(dya, dya_n, ah, ab, ac, ah_p, ac_p, ab_n, w, d_o, dwa_o, zbuf, dbuf, dzb):
        i = pl.program_id(0)
        zbuf[pl.ds(0, hb), :] = jnp.where(i == 0, 0.0, ac_p[...].astype(F32) * ah_p[...].astype(F32))
        zbuf[pl.ds(hb, ts), :] = ac[...].astype(F32) * ah[...].astype(F32)
        dbuf[pl.ds(0, ts), :] = dya[...].astype(F32) * ab[...].astype(F32)
        dbuf[pl.ds(ts, hb), :] = jnp.where(i == n_t - 1, 0.0, dya_n[...].astype(F32) * ab_n[...].astype(F32))
        dw_rows = [jnp.zeros((1, c), F32) for _ in range(CONV_A_K)]
        for r0 in range(0, ts, CONV_ROWS):
            cz = jnp.zeros((CONV_ROWS, c), F32)
            dz = jnp.zeros((CONV_ROWS, c), F32)
            dc = dbuf[pl.ds(r0, CONV_ROWS), :]
            for k in range(CONV_A_K):
                zk = zbuf[pl.ds(hb + r0 - (CONV_A_K - 1) + k, CONV_ROWS), :]
                cz = cz + w[k : k + 1, :] * zk
                dz = dz + w[k : k + 1, :] * dbuf[pl.ds(r0 + (CONV_A_K - 1) - k, CONV_ROWS), :]
                dw_rows[k] = dw_rows[k] + jnp.sum(dc * zk, axis=0, keepdims=True)
            d_o[pl.ds(r0, CONV_ROWS), c : 2 * c] = (dya[pl.ds(r0, CONV_ROWS), :].astype(F32) * cz).astype(BF16)
            dzb[pl.ds(r0, CONV_ROWS), :] = dz
        d_z = dzb[...]
        d_o[:, 0:c] = (d_z * ac[...].astype(F32)).astype(BF16)
        d_o[:, 2 * c : 3 * c] = (d_z * ah[...].astype(F32)).astype(BF16)
        for k in range(CONV_A_K):
            dwa_o[k : k + 1, :] = dw_rows[k]
        dwa_o[CONV_A_K:kp, :] = jnp.zeros((kp - CONV_A_K, c), F32)

    blk = lambda cb: pl.BlockSpec((ts, c), lambda i: (i, cb))
    hp = lambda cb: pl.BlockSpec((hb, c), lambda i: (prev(i), cb))
    hn = lambda cb: pl.BlockSpec((hb, c), lambda i: (nxt(i), cb))
    return pl.pallas_call(
        body,
        name="mix_a_bwd",
        grid=(n_t,),
        in_specs=[blk(0), hn(0), blk(0), blk(1), blk(2), hp(0), hp(2), hn(1), pl.BlockSpec(wa.shape, lambda i: (0, 0))],
        out_specs=[pl.BlockSpec((ts, 3 * c), lambda i: (i, 0)), pl.BlockSpec((None, kp, c), lambda i: (i, 0, 0))],
        out_shape=[SDS((s, 3 * c), BF16), SDS((n_t, kp, c), F32)],
        scratch_shapes=[pltpu.VMEM((hb + ts, c), F32), pltpu.VMEM((ts + hb, c), F32), pltpu.VMEM((ts, c), F32)],
        compiler_params=_params(("parallel",)),
    )(d_ya, d_ya, proj, proj, proj, proj, proj, proj, wa)


def _ep_bf16(accs, ex, os_):
    os_[0][...] = accs[0].astype(BF16)


def _mm_tn(name, a, b, tm=2048, tn=1024, tk=1024):
    m, k1 = a.shape
    n = b.shape[1]
    tm, tn, tk = _tile(k1, tm), _tile(n, tn), _tile(m, tk)
    return _fmm(
        name,
        (k1 // tm, n // tn, m // tk),
        [(a, pl.BlockSpec((tk, tm), lambda i, j, k: (k, i))), (b, pl.BlockSpec((tk, tn), lambda i, j, k: (k, j)))],
        [(0, 1, TN, 0, None)],
        [(tm, tn)],
        [],
        [(SDS((k1, n), BF16), pl.BlockSpec((tm, tn), lambda i, j, k: (i, j)))],
        _ep_bf16,
    )[0][0]


def _mm_nt(name, a, b, tm=1024, tn=1024, comm=None):
    m, kk = a.shape
    n = b.shape[0]
    tm, tn = _tile(m, tm), _tile(n, tn)
    outs, couts = _fmm(
        name,
        (m // tm, n // tn, 1),
        [(a, pl.BlockSpec((tm, kk), lambda i, j, k: (i, 0))), (b, pl.BlockSpec((tn, kk), lambda i, j, k: (j, 0)))],
        [(0, 1, NT, 0, None)],
        [(tm, tn)],
        [],
        [(SDS((m, n), BF16), pl.BlockSpec((tm, tn), lambda i, j, k: (i, j)))],
        _ep_bf16,
        comm=comm,
    )
    return outs[0], couts


def _dev_index(dev):
    return 4 * dev[0] + 2 * dev[1] + dev[2]


def _region(ref, kind, j, shard_shape):
    if kind == "col":
        ns = shard_shape[1]
        return ref.at[:, pl.ds(pl.multiple_of(j * ns, 128), ns)]
    if kind == "row":
        rs = shard_shape[0]
        return ref.at[pl.ds(pl.multiple_of(j * rs, 8), rs), :]
    return ref.at[j]


def _whole_shape(kind, shard_shape):
    if kind == "col":
        return (shard_shape[0], NDEV * shard_shape[1])
    if kind == "row":
        return (NDEV * shard_shape[0], shard_shape[1])
    return (NDEV,) + tuple(shard_shape)


def _place():
    return lax.axis_index("x"), lax.axis_index("y"), lax.axis_index("c")


def _proj_gather(n1, w_shard, comm):
    s, d = n1.shape
    ns = w_shard.shape[1]
    pw = 2 * ns
    tm = _tile(s // 2, 1024)
    n_i = s // tm
    assert n_i >= 2 and not comm["alias"]
    x0, y0, _ = _place()
    order = jnp.stack([2 * x0 + y0, 2 * x0 + (1 - y0), 2 * (1 - x0) + y0, 2 * (1 - x0) + (1 - y0)]).astype(jnp.int32)
    n_ci, n_co = len(comm["ins"]), len(comm["outs"])

    def body(order_ref, n1_ref, wsh_ref, *rest):
        ci = rest[:n_ci]
        proj_ref, win_ref = rest[n_ci], rest[n_ci + 1]
        co = rest[n_ci + 2 : n_ci + 2 + n_co]
        wbuf, send, recv, fsend, frecv, loc, lsem = rest[n_ci + 2 + n_co : n_ci + 9 + n_co]
        cs = rest[n_ci + 9 + n_co :]
        u, i = pl.program_id(0), pl.program_id(1)
        x, y, c = _place()
        me, sib = (x, y, c), (x, y, 1 - c)
        chips = [(x, 1 - y), (1 - x, y), (1 - x, 1 - y)]
        peers = [sib] + [(*ch, c) for ch in chips]
        reg = lambda dev: win_ref.at[:, pl.ds(pl.multiple_of(_dev_index(dev) * ns, 128), ns)]
        sends = [_remote(wsh_ref, reg(me), send.at[k], recv.at[k], peers[k]) for k in range(4)]
        arrivals = [_remote(reg(peers[k]), reg(peers[k]), send.at[k], recv.at[k], peers[k]) for k in range(4)]
        passes = [_remote(reg((*ch, c)), reg((*ch, c)), fsend.at[j], frecv.at[j], sib) for j, ch in enumerate(chips)]
        passed = [_remote(reg((*ch, 1 - c)), reg((*ch, 1 - c)), fsend.at[j], frecv.at[j], sib) for j, ch in enumerate(chips)]
        mine = lambda: pltpu.make_async_copy(wsh_ref, reg(me), loc.at[0])

        def load(unit):
            col0 = pl.multiple_of(order_ref[unit] * pw, 128)
            return pltpu.make_async_copy(win_ref.at[:, pl.ds(col0, pw)], wbuf.at[unit % 2], lsem.at[unit % 2])

        @pl.when(jnp.logical_and(u == 0, i == 0))
        def _():
            mine().start()
            for snd in sends:
                snd().start()
            comm["start"](ci, co, cs)
            mine().wait()
            arrivals[0]().wait_recv()
            load(0).start()
            load(0).wait()

        for nxt in range(1, 4):

            @pl.when(jnp.logical_and(u == nxt - 1, i == n_i - 1))
            def _(nxt=nxt):
                passed[nxt - 1]().wait_recv()
                load(nxt).start()

            @pl.when(jnp.logical_and(u == nxt, i == 0))
            def _(nxt=nxt):
                load(nxt).wait()

        proj_ref[...] = jnp.dot(n1_ref[...], wbuf[u % 2], preferred_element_type=F32).astype(BF16)

        for nxt in range(1, 4):

            @pl.when(jnp.logical_and(u == nxt - 1, i == n_i - 2))
            def _(nxt=nxt):
                arrivals[nxt]().wait_recv()
                passes[nxt - 1]().start()

        @pl.when(jnp.logical_and(u == 3, i == n_i - 1))
        def _():
            for snd in sends + passes:
                snd().wait_send()
            comm["finish"](ci, co, cs)

    hbm = pl.BlockSpec(memory_space=pl.ANY)
    dma = pltpu.SemaphoreType.DMA
    res = pl.pallas_call(
        body,
        name="proj",
        grid_spec=pltpu.PrefetchScalarGridSpec(
            num_scalar_prefetch=1,
            grid=(4, n_i),
            in_specs=[pl.BlockSpec((tm, d), lambda u, i, order_ref: (i, 0)), hbm] + [hbm] * n_ci,
            out_specs=[pl.BlockSpec((tm, pw), lambda u, i, order_ref: (i, order_ref[u])), hbm] + [hbm] * n_co,
            scratch_shapes=[pltpu.VMEM((2, d, pw), BF16), dma((4,)), dma((4,)), dma((3,)), dma((3,)), dma((1,)), dma((2,))]
            + list(comm["sems"]),
        ),
        out_shape=[SDS((s, NDEV * ns), BF16), SDS((d, NDEV * ns), BF16)] + list(comm["outs"]),
        compiler_params=_params(("arbitrary", "arbitrary")),
    )(order, n1, w_shard, *comm["ins"])
    return res[0], res[1], list(res[2:])


def _peer(me, r):
    x, y, c = me
    return (1 - x if r & 4 else x, 1 - y if r & 2 else y, 1 - c if r & 1 else c)


def _remote(src, dst, send_sem, recv_sem, to):
    return lambda: pltpu.make_async_remote_copy(
        src_ref=src, dst_ref=dst, send_sem=send_sem, recv_sem=recv_sem, device_id=to, device_id_type=MESH
    )


def _run(pairs, locals_, start):
    if start:
        for cp in locals_:
            cp.start()
        for snd, _ in pairs:
            snd().start()
    else:
        for snd, arr in pairs:
            arr().wait_recv()
            snd().wait_send()
        for cp in locals_:
            cp.wait()


def _stage(ins, outs, alias, sems, build):
    return dict(
        ins=list(ins), outs=list(outs), alias=alias, sems=list(sems),
        start=lambda i, o, s: _run(*build(i, o, s), True),
        finish=lambda i, o, s: _run(*build(i, o, s), False),
    )


def _ag1(shards, kinds):
    n_t = len(shards)
    shapes = [tuple(sh.shape) for sh in shards]

    def build(srcs, dsts, sems):
        send, recv, loc = sems
        x, y, c = _place()
        me = (x, y, c)
        peers = [(x, y, 1 - c), (1 - x, y, c), (x, 1 - y, c), (1 - x, 1 - y, c)]
        reg = lambda t, dev: _region(dsts[t], kinds[t], _dev_index(dev), shapes[t])
        pairs = []
        for t in range(n_t):
            for k, peer in enumerate(peers):
                snd = _remote(srcs[t], reg(t, me), send.at[t, k], recv.at[t, k], peer)
                arr = _remote(reg(t, peer), reg(t, peer), send.at[t, k], recv.at[t, k], peer)
                pairs.append((snd, arr))
        mine = [pltpu.make_async_copy(srcs[t], reg(t, me), loc.at[t]) for t in range(n_t)]
        return pairs, mine

    outs = [SDS(_whole_shape(kinds[t], shapes[t]), shards[t].dtype) for t in range(n_t)]
    dma = pltpu.SemaphoreType.DMA
    return _stage(shards, outs, {}, [dma((n_t, 4)), dma((n_t, 4)), dma((n_t,))], build)


def _ag_direct(shards, kinds):
    n_t = len(shards)
    shapes = [tuple(sh.shape) for sh in shards]

    def build(srcs, dsts, sems):
        send, recv, loc = sems
        me = _place()
        reg = lambda t, dev: _region(dsts[t], kinds[t], _dev_index(dev), shapes[t])
        pairs = []
        for t in range(n_t):
            for r in range(1, NDEV):
                peer = _peer(me, r)
                snd = _remote(srcs[t], reg(t, me), send.at[t, r - 1], recv.at[t, r - 1], peer)
                arr = _remote(reg(t, peer), reg(t, peer), send.at[t, r - 1], recv.at[t, r - 1], peer)
                pairs.append((snd, arr))
        mine = [pltpu.make_async_copy(srcs[t], reg(t, me), loc.at[t]) for t in range(n_t)]
        return pairs, mine

    outs = [SDS(_whole_shape(kinds[t], shapes[t]), shards[t].dtype) for t in range(n_t)]
    dma = pltpu.SemaphoreType.DMA
    return _stage(shards, outs, {}, [dma((n_t, 7)), dma((n_t, 7)), dma((n_t,))], build)


def _ag2(wholes, kinds, shapes):
    n_t = len(wholes)

    def build(_, dsts, sems):
        send, recv = sems
        x, y, c = _place()
        sib = (x, y, 1 - c)
        chips = [(1 - x, y), (x, 1 - y), (1 - x, 1 - y)]
        reg = lambda t, dev: _region(dsts[t], kinds[t], _dev_index(dev), shapes[t])
        pairs = []
        for t in range(n_t):
            for j, chip in enumerate(chips):
                snd = _remote(reg(t, (*chip, c)), reg(t, (*chip, c)), send.at[t, j], recv.at[t, j], sib)
                arr = _remote(reg(t, (*chip, 1 - c)), reg(t, (*chip, 1 - c)), send.at[t, j], recv.at[t, j], sib)
                pairs.append((snd, arr))
        return pairs, []

    outs = [SDS(w.shape, w.dtype) for w in wholes]
    dma = pltpu.SemaphoreType.DMA
    return _stage(wholes, outs, {t: t for t in range(n_t)}, [dma((n_t, 3)), dma((n_t, 3))], build)


def _chip_of(q):
    return (q >> 1, q & 1)


def _rs1(wholes, kinds, shapes):
    n_t = len(wholes)

    def build(srcs, outs, sems):
        send, recv = sems
        x, y, c = _place()
        sib = (x, y, 1 - c)
        pairs = []
        for t in range(n_t):
            for q in range(4):
                theirs = _region(srcs[t], kinds[t], _dev_index((*_chip_of(q), 1 - c)), shapes[t])
                pairs.append((
                    _remote(theirs, outs[t].at[q], send.at[t, q], recv.at[t, q], sib),
                    _remote(outs[t].at[q], outs[t].at[q], send.at[t, q], recv.at[t, q], sib),
                ))
        return pairs, []

    slabs = [SDS((4,) + tuple(shapes[t]), wholes[t].dtype) for t in range(n_t)]
    dma = pltpu.SemaphoreType.DMA
    return _stage(wholes, slabs, {}, [dma((n_t, 4)), dma((n_t, 4))], build)


def _rs2(pair_sums):
    n_t = len(pair_sums)

    def build(srcs, lands, sems):
        send, recv, loc = sems
        x, y, c = _place()
        my_chip = 2 * x + y
        pairs, mine = [], []
        for t in range(n_t):
            for j, (px, py) in enumerate([(1 - x, y), (x, 1 - y), (1 - x, 1 - y)]):
                q = 2 * px + py
                pairs.append((
                    _remote(srcs[t].at[q], lands[t].at[my_chip], send.at[t, j], recv.at[t, j], (px, py, c)),
                    _remote(lands[t].at[q], lands[t].at[q], send.at[t, j], recv.at[t, j], (px, py, c)),
                ))
            mine.append(pltpu.make_async_copy(srcs[t].at[my_chip], lands[t].at[my_chip], loc.at[t]))
        return pairs, mine

    outs = [SDS(q.shape, q.dtype) for q in pair_sums]
    dma = pltpu.SemaphoreType.DMA
    return _stage(pair_sums, outs, {}, [dma((n_t, 3)), dma((n_t, 3)), dma((n_t,))], build)


def _pair_sum(name, whole, kind, got):
    _, rows, cols = got.shape
    tr = _tile(rows, 256)
    n_r = rows // tr
    core = lax.axis_index("c").astype(jnp.int32).reshape(1)

    def body(_, a, b, o):
        o[...] = (a[...].astype(F32) + b[...].astype(F32)).astype(BF16)

    if kind == "col":
        own = pl.BlockSpec((tr, cols), lambda q, i, c_ref: (i, 2 * q + c_ref[0]))
    elif kind == "row":
        own = pl.BlockSpec((tr, cols), lambda q, i, c_ref: ((2 * q + c_ref[0]) * n_r + i, 0))
    else:
        own = pl.BlockSpec((None, tr, cols), lambda q, i, c_ref: (2 * q + c_ref[0], i, 0))
    slab = pl.BlockSpec((None, tr, cols), lambda q, i, c_ref: (q, i, 0))
    return pl.pallas_call(
        body,
        name=name,
        grid_spec=pltpu.PrefetchScalarGridSpec(
            num_scalar_prefetch=1, grid=(4, n_r), in_specs=[own, slab], out_specs=slab
        ),
        out_shape=SDS(got.shape, BF16),
        compiler_params=_params(("parallel", "parallel")),
    )(core, whole, got)


def _all_reduce_small(part):
    r_, c_ = part.shape

    def body(src, land, total, send_sems, recv_sems):
        me = _place()
        my = _dev_index(me)
        land[my] = src[...]

        def copy(r):
            peer = _peer(me, r)
            return pltpu.make_async_remote_copy(
                src_ref=src,
                dst_ref=land.at[my],
                send_sem=send_sems.at[r - 1],
                recv_sem=recv_sems.at[r - 1],
                device_id=peer,
                device_id_type=MESH,
            )

        def arrival(r):
            peer = _peer(me, r)
            slab = land.at[_dev_index(peer)]
            return pltpu.make_async_remote_copy(
                src_ref=slab,
                dst_ref=slab,
                send_sem=send_sems.at[r - 1],
                recv_sem=recv_sems.at[r - 1],
                device_id=peer,
                device_id_type=MESH,
            )

        sends = [copy(r) for r in range(1, NDEV)]
        for cp in sends:
            cp.start()
        for r in range(1, NDEV):
            arrival(r).wait_recv()
        for cp in sends:
            cp.wait_send()
        acc = land[0]
        for d in range(1, NDEV):
            acc = acc + land[d]
        total[...] = acc

    vmem = pl.BlockSpec(memory_space=pltpu.VMEM)
    return pl.pallas_call(
        body,
        name="all_reduce_small",
        in_specs=[vmem],
        out_specs=[vmem, vmem],
        out_shape=[SDS((NDEV, r_, c_), F32), SDS((r_, c_), F32)],
        scratch_shapes=[pltpu.SemaphoreType.DMA((7,)), pltpu.SemaphoreType.DMA((7,))],
    )(part)[1]


def _adamw_math(g, w, m, v):
    m2 = ADAM_B1 * m + (1.0 - ADAM_B1) * g
    v2 = ADAM_B2 * v + (1.0 - ADAM_B2) * (g * g)
    m_hat = m2 / (1.0 - ADAM_B1**ADAM_STEP)
    v_hat = v2 / (1.0 - ADAM_B2**ADAM_STEP)
    delta = -ADAM_LR * (m_hat / (jnp.sqrt(v_hat) + ADAM_EPS) + ADAM_WD * w)
    return delta, m2, v2


def _adamw_big(name, land, w, m, v):
    rows, cols = w.shape
    tr = _tile(rows, 256)
    n_slab = land.shape[0]

    def body(l_ref, w_ref, m_ref, v_ref, g_o, d_o, m_o, v_o):
        g = l_ref[0].astype(F32)
        for d in range(1, n_slab):
            g = g + l_ref[d].astype(F32)
        delta, m2, v2 = _adamw_math(g, w_ref[...], m_ref[...], v_ref[...])
        g_o[...] = g
        d_o[...] = delta
        m_o[...] = m2
        v_o[...] = v2

    blk = pl.BlockSpec((tr, cols), lambda i: (i, 0))
    return pl.pallas_call(
        body,
        name=name,
        grid=(rows // tr,),
        in_specs=[pl.BlockSpec((n_slab, tr, cols), lambda i: (0, i, 0)), blk, blk, blk],
        out_specs=[blk] * 4,
        out_shape=[SDS((rows, cols), F32)] * 4,
        compiler_params=_params(("parallel",)),
    )(land, w, m, v)


def _adamw_small(g, w, m, v):
    def body(g_ref, w_ref, m_ref, v_ref, d_o, m_o, v_o):
        delta, m2, v2 = _adamw_math(g_ref[...], w_ref[...], m_ref[...], v_ref[...])
        d_o[...] = delta
        m_o[...] = m2
        v_o[...] = v2

    vmem = pl.BlockSpec(memory_space=pltpu.VMEM)
    return pl.pallas_call(
        body,
        name="adamw_small",
        in_specs=[vmem] * 4,
        out_specs=[vmem] * 3,
        out_shape=[SDS(g.shape, F32)] * 3,
    )(g, w, m, v)


def _pack(pieces, width):
    flat = jnp.concatenate([p.reshape(-1) for p in pieces])
    rows = -(-flat.shape[0] // (8 * width)) * 8
    flat = jnp.pad(flat, (0, rows * width - flat.shape[0]))
    return flat.reshape(rows, width)


def _unpack(packed, shapes):
    flat = packed.reshape(-1)
    out, off = [], 0
    for shp in shapes:
        n = 1
        for d in shp:
            n *= d
        out.append(flat[off : off + n].reshape(shp))
        off += n
    return out


def kernel(x, p, g_mix, w_in, conv_a_w, w_out_a, b_glu, conf_dw_w, conf_dw_b, conf_ln_g, conf_ln_b, w_pw_b, b_pw_b, w_o, g_ffn, w_gate, w_up, w_down, g_ple, w_ple_gate, w_ple_proj, g_final, loss_target, m_g_mix, m_w_in, m_conv_a_w, m_w_out_a, m_b_glu, m_conf_dw_w, m_conf_dw_b, m_conf_ln_g, m_conf_ln_b, m_w_pw_b, m_b_pw_b, m_w_o, m_g_ffn, m_w_gate, m_w_up, m_w_down, m_g_ple, m_w_ple_gate, m_w_ple_proj, m_g_final, v_g_mix, v_w_in, v_conv_a_w, v_w_out_a, v_b_glu, v_conf_dw_w, v_conf_dw_b, v_conf_ln_g, v_conf_ln_b, v_w_pw_b, v_b_pw_b, v_w_o, v_g_ffn, v_w_gate, v_w_up, v_w_down, v_g_ple, v_w_ple_gate, v_w_ple_proj, v_g_final):
    s, d = x.shape[1], x.shape[2]
    c = conf_ln_g.shape[-1]
    pdim = w_ple_proj.shape[1]
    fs = w_gate.shape[-1]
    nin = NDEV * w_in.shape[-1]
    assert d == 2 * c and nin == 5 * c + 2 * d, (d, c, nin)
    x2, p2, tgt = x[0], p[0, 0], loss_target[0]
    gfin = g_final.reshape(1, d)

    kpa, kpb = 8, HALO_B
    wa_sh = jnp.pad(conv_a_w[0], ((0, kpa - CONV_A_K), (0, 0)))
    wd_sh = jnp.pad(conf_dw_w[0], ((0, kpb - CONF_K), (0, 0)))
    kind_of = dict(w_in="col", w_out_a="col", w_pw_b="col", w_ple_proj="col", w_o="row", w_ple_gate="row",
                   w_gate="blk", w_up="blk", w_down="blk")
    weight = dict(w_in=w_in, w_out_a=w_out_a, w_pw_b=w_pw_b, w_ple_proj=w_ple_proj, w_o=w_o, w_ple_gate=w_ple_gate,
                  w_gate=w_gate, w_up=w_up, w_down=w_down)
    shard_of = {nm: tuple(w.shape[1:]) for nm, w in weight.items()}
    bf16_shard = lambda nm: weight[nm][0].astype(BF16)
    kinds_ = lambda grp: [kind_of[nm] for nm in grp]
    shapes_ = lambda grp: [shard_of[nm] for nm in grp]
    first_stage = lambda grp: _ag1([bf16_shard(nm) for nm in grp], kinds_(grp))
    second_stage = lambda grp, parts: _ag2(parts, kinds_(grp), shapes_(grp))
    grp_1 = ["w_out_a", "w_pw_b"]
    grp_2 = ["w_o", "w_gate"]
    grp_3 = ["w_up"]
    grp_4 = ["w_down"]
    grp_5 = ["w_ple_gate", "w_ple_proj"]

    tm = _tile(s, 1024)
    tn = _tile(d, 1024)
    assert (5 * c) % tn == 0 and d % tn == 0 and c % tn == 0
    ga_blk, gb_blk = (5 * c) // tn, (5 * c + d) // tn
    ij = lambda i, j, k: (i, j)
    row_i = lambda i, j, k: (i, 0)

    n1 = _rms_fwd("rms1", x2, g_mix)
    proj, win, got = _proj_gather(
        n1, bf16_shard("w_in"), _join(_ag_direct([wa_sh, wd_sh], ["col", "col"]), first_stage(grp_1))
    )
    (wa, wd), part_1 = got[:2], got[2:]
    ya_in = _mix_a_fwd(proj, wa, s, c)
    (v_act, u_act, cv), got = _mix_b_fwd(
        proj, b_glu, wd, conf_dw_b, conf_ln_g, conf_ln_b, s, c,
        comm=_join(second_stage(grp_1, part_1), first_stage(grp_2)),
    )
    (wouta, wpw), part_2 = got[: len(grp_1)], got[len(grp_1) :]

    def ep_merge(accs, ex, os_):
        sa = _sigmoid(ex[0][...].astype(F32))
        sb = _sigmoid(ex[1][...].astype(F32))
        ya = accs[0]
        yb = accs[1] + ex[2][...]
        os_[0][...] = (sa * ya + sb * yb).astype(BF16)
        os_[1][...] = ya.astype(BF16)
        os_[2][...] = yb.astype(BF16)

    gate_a_spec = pl.BlockSpec((tm, tn), lambda i, j, k: (i, ga_blk + j))
    gate_b_spec = pl.BlockSpec((tm, tn), lambda i, j, k: (i, gb_blk + j))
    out_sd = (SDS((s, d), BF16), pl.BlockSpec((tm, tn), ij))
    (m_act, ya, yb), got = _fmm(
        "merge", (s // tm, d // tn, 1),
        [(ya_in, pl.BlockSpec((tm, c), row_i)), (wouta, pl.BlockSpec((c, tn), lambda i, j, k: (0, j))),
         (v_act, pl.BlockSpec((tm, c), row_i)), (wpw, pl.BlockSpec((c, tn), lambda i, j, k: (0, j)))],
        [(0, 1, NN, 0, None), (2, 3, NN, 1, None)], [(tm, tn), (tm, tn)],
        [(proj, gate_a_spec), (proj, gate_b_spec), (b_pw_b, pl.BlockSpec((1, tn), lambda i, j, k: (0, j)))],
        [out_sd, out_sd, out_sd], ep_merge, csplit=EPILOGUE_CHUNK,
        comm=_join(second_stage(grp_2, part_2), first_stage(grp_3)),
    )
    (wo, wg), part_3 = got[: len(grp_2)], got[len(grp_2) :]

    def ep_residual(accs, ex, os_):
        os_[0][...] = accs[0] + ex[0][...]

    (h1,), got = _fmm(
        "w_o", (s // tm, d // tn, 1),
        [(m_act, pl.BlockSpec((tm, d), row_i)), (wo, pl.BlockSpec((d, tn), lambda i, j, k: (0, j)))],
        [(0, 1, NN, 0, None)], [(tm, tn)], [(x2, pl.BlockSpec((tm, tn), ij))],
        [(SDS((s, d), F32), pl.BlockSpec((tm, tn), ij))], ep_residual, csplit=EPILOGUE_CHUNK,
        comm=_join(second_stage(grp_3, part_3), first_stage(grp_4)),
    )
    (wu,), part_4 = got[: len(grp_3)], got[len(grp_3) :]
    n2 = _rms_fwd("rms2", h1, g_ffn)

    def ep_gateup(accs, ex, os_):
        g, u = accs
        os_[0][...] = g.astype(BF16)
        os_[1][...] = u.astype(BF16)
        os_[2][...] = (g * _sigmoid(g) * u).astype(BF16)

    ff_sd = (SDS((NDEV, s, fs), BF16), pl.BlockSpec((None, tm, fs), lambda i, j, k: (j, i, 0)))
    w_col_blk = pl.BlockSpec((None, d, fs), lambda i, j, k: (j, 0, 0))
    (g_act, u_ff, f_act), got = _fmm(
        "gate_up", (s // tm, NDEV, 1),
        [(n2, pl.BlockSpec((tm, d), row_i)), (wg, w_col_blk), (wu, w_col_blk)],
        [(0, 1, NN, 0, None), (0, 2, NN, 1, None)], [(tm, fs), (tm, fs)], [],
        [ff_sd, ff_sd, ff_sd], ep_gateup, csplit=EPILOGUE_CHUNK,
        comm=_join(second_stage(grp_4, part_4), first_stage(grp_5)),
    )
    (wdn,), part_5 = got[: len(grp_4)], got[len(grp_4) :]
    pair = 2
    (h2,), (wpg, wpp) = _fmm(
        "down", (s // tm, d // tn, NDEV // pair),
        [(f_act, pl.BlockSpec((pair, tm, fs), lambda i, j, k: (k, i, 0))),
         (wdn, pl.BlockSpec((pair, fs, tn), lambda i, j, k: (k, 0, j)))],
        [(0, 1, NN, 0, None, pair)], [(tm, tn)], [(h1, pl.BlockSpec((tm, tn), ij))],
        [(SDS((s, d), F32), pl.BlockSpec((tm, tn), ij))], ep_residual,
        comm=second_stage(grp_5, part_5),
    )
    n3 = _rms_fwd("rms3", h2, g_ple)

    tr = _tile(s, 256)
    n_r = s // tr
    rows = lambda i, j, k: (i, 0)
    whole = lambda i, j, k: (0, 0)
    part_spec = lambda nrow: pl.BlockSpec((None, nrow, d), lambda i, j, k: (i, 0, 0))

    def ep_ple(accs, ex, os_):
        h2_, t_, gf = ex[0][...], ex[1][...], ex[2][...]
        ple = accs[0]
        s3 = _sigmoid(accs[1])
        h3 = h2_ + s3 * ple
        r = lax.rsqrt(jnp.mean(h3 * h3, axis=-1, keepdims=True) + EPS)
        hn = h3 * r
        e = hn * gf - t_
        loss = 0.5 * jnp.sum(jnp.mean(e * e, axis=-1, keepdims=True), axis=0, keepdims=True)
        dy = e * (1.0 / d)
        dn = dy * gf
        dh3 = r * (dn - hn * jnp.mean(dn * hn, axis=-1, keepdims=True))
        os_[0][...] = dh3
        os_[1][...] = (dh3 * s3).astype(BF16)
        os_[2][...] = (dh3 * ple * s3 * (1.0 - s3)).astype(BF16)
        os_[3][0:1, :] = jnp.sum(dy * hn, axis=0, keepdims=True)
        os_[3][1:2, :] = jnp.broadcast_to(loss, (1, d))

    (dh3, d_ple, d_g3, part_fin), _ = _fmm(
        "ple_loss", (n_r, 1, 1),
        [(p2, pl.BlockSpec((tr, pdim), rows)), (wpp, pl.BlockSpec((pdim, d), whole)),
         (n3, pl.BlockSpec((tr, d), rows)), (wpg, pl.BlockSpec((d, d), whole))],
        [(0, 1, NN, 0, None), (2, 3, NN, 1, None)], [(tr, d), (tr, d)],
        [(h2, pl.BlockSpec((tr, d), rows)), (tgt, pl.BlockSpec((tr, d), rows)), (gfin, pl.BlockSpec((1, d), whole))],
        [(SDS((s, d), F32), pl.BlockSpec((tr, d), rows)), (SDS((s, d), BF16), pl.BlockSpec((tr, d), rows)),
         (SDS((s, d), BF16), pl.BlockSpec((tr, d), rows)), (SDS((n_r, 2, d), F32), part_spec(2))],
        ep_ple,
    )

    g_wpp = _mm_tn("d_w_ple_proj", p2, d_ple)
    g_wpg = _mm_tn("d_w_ple_gate", n3, d_g3)

    def ep_norm_bwd(accs, ex, os_):
        dh, dg = _rms_bwd(accs[0], ex[0][...], ex[2][...])
        dh = ex[1][...] + dh
        os_[0][...] = dh
        os_[1][...] = dh.astype(BF16)
        os_[2][...] = dg

    norm_outs = lambda t: [
        (SDS((s, d), F32), pl.BlockSpec((t, d), rows)), (SDS((s, d), BF16), pl.BlockSpec((t, d), rows)),
        (SDS((s // t, 1, d), F32), part_spec(1)),
    ]
    def exchange1(names, wholes):
        return _rs1(wholes, kinds_(names), shapes_(names))

    def pair_sums(names, wholes, got):
        return [_pair_sum("pair_sum_" + nm, wholes[t], kind_of[nm], got[t]) for t, nm in enumerate(names)]

    lands = {}
    grp1 = ["w_ple_proj", "w_ple_gate"]
    (dh2, dh2b, part_ple), got = _fmm(
        "d_n3", (n_r, 1, 1),
        [(d_g3, pl.BlockSpec((tr, d), rows)), (wpg, pl.BlockSpec((d, d), whole))],
        [(0, 1, NT, 0, None)], [(tr, d)],
        [(h2, pl.BlockSpec((tr, d), rows)), (dh3, pl.BlockSpec((tr, d), rows)), (g_ple, pl.BlockSpec((1, d), whole))],
        norm_outs(tr), ep_norm_bwd,
        comm=exchange1(grp1, [g_wpp, g_wpg]),
    )
    sums1 = pair_sums(grp1, [g_wpp, g_wpg], got)

    def ep_ddown(accs, ex, os_):
        g = ex[0][...].astype(F32)
        u = ex[1][...].astype(F32)
        sg = _sigmoid(g)
        df = accs[0]
        os_[0][...] = (df * u * sg * (1.0 + g * (1.0 - sg))).astype(BF16)
        os_[1][...] = (df * g * sg).astype(BF16)

    ff_in = pl.BlockSpec((None, tm, fs), lambda i, j, k: (j, i, 0))
    (d_g, d_u), got = _fmm(
        "d_down", (s // tm, NDEV, 1),
        [(dh2b, pl.BlockSpec((tm, d), row_i)), (wdn, pl.BlockSpec((None, fs, d), lambda i, j, k: (j, 0, 0)))],
        [(0, 1, NT, 0, None)], [(tm, fs)], [(g_act, ff_in), (u_ff, ff_in)],
        [ff_sd, ff_sd], ep_ddown, csplit=EPILOGUE_CHUNK,
        comm=_rs2(sums1),
    )
    lands.update(zip(grp1, got))
    tk = _tile(s, 1024)
    (g_wdn,), _ = _fmm(
        "d_w_down", (NDEV, 1, s // tk),
        [(f_act, pl.BlockSpec((None, tk, fs), lambda i, j, k: (i, k, 0))), (dh2b, pl.BlockSpec((tk, d), lambda i, j, k: (k, 0)))],
        [(0, 1, TN, 0, None)], [(fs, d)], [],
        [(SDS((NDEV, fs, d), BF16), pl.BlockSpec((None, fs, d), lambda i, j, k: (i, 0, 0)))], _ep_bf16,
    )

    def ep_two_bf16(accs, ex, os_):
        os_[0][...] = accs[0].astype(BF16)
        os_[1][...] = accs[1].astype(BF16)

    ff_k = pl.BlockSpec((None, tk, fs), lambda i, j, k: (i, k, 0))
    wcol_sd = (SDS((NDEV, d, fs), BF16), pl.BlockSpec((None, d, fs), lambda i, j, k: (i, 0, 0)))
    grp2 = ["w_down"]
    (g_wg, g_wu), got = _fmm(
        "d_w_gate_up", (NDEV, 1, s // tk),
        [(n2, pl.BlockSpec((tk, d), lambda i, j, k: (k, 0))), (d_g, ff_k), (d_u, ff_k)],
        [(0, 1, TN, 0, None), (0, 2, TN, 1, None)], [(d, fs), (d, fs)], [],
        [wcol_sd, wcol_sd], ep_two_bf16,
        comm=exchange1(grp2, [g_wdn]),
    )
    sums2 = pair_sums(grp2, [g_wdn], got)
    grp3 = ["w_gate", "w_up"]
    th = _tile(s // 2, 1024)
    ff_a = pl.BlockSpec((None, th, fs), lambda i, j, k: (k, i, 0))
    w_k = pl.BlockSpec((None, d, fs), lambda i, j, k: (k, 0, 0))
    (d_n2,), got = _fmm(
        "d_n2", (s // th, 1, NDEV),
        [(d_g, ff_a), (wg, w_k), (d_u, ff_a), (wu, w_k)],
        [(0, 1, NT, 0, None), (2, 3, NT, 0, None)], [(th, d)], [],
        [(SDS((s, d), BF16), pl.BlockSpec((th, d), rows))], _ep_bf16,
        comm=_join(_rs2(sums2), exchange1(grp3, [g_wg, g_wu])),
    )
    lands.update(zip(grp2, got[:1]))
    sums3 = pair_sums(grp3, [g_wg, g_wu], got[1:])
    dh1, dh1b, part_ffn = _norm_bwd("d_h1", d_n2, h1, dh2, g_ffn, True)
    g_wo = _mm_tn("d_w_o", m_act, dh1b)

    def ep_dm(accs, ex, os_):
        ya_, yb_ = ex[0][...].astype(F32), ex[1][...].astype(F32)
        sa = _sigmoid(ex[2][...].astype(F32))
        sb = _sigmoid(ex[3][...].astype(F32))
        dm = accs[0]
        d_yb = dm * sb
        os_[0][...] = (dm * sa).astype(BF16)
        os_[1][...] = d_yb.astype(BF16)
        os_[2][...] = (dm * ya_ * sa * (1.0 - sa)).astype(BF16)
        os_[3][...] = (dm * yb_ * sb * (1.0 - sb)).astype(BF16)
        os_[4][...] = jnp.sum(d_yb, axis=0, keepdims=True)

    tile_ij = pl.BlockSpec((tm, tn), ij)
    grp4 = ["w_o"]
    (d_ya, d_yb, d_ga, d_gb, part_bpw), got = _fmm(
        "d_merge", (s // tm, d // tn, 1),
        [(dh1b, pl.BlockSpec((tm, d), row_i)), (wo, pl.BlockSpec((tn, d), lambda i, j, k: (j, 0)))],
        [(0, 1, NT, 0, None)], [(tm, tn)],
        [(ya, tile_ij), (yb, tile_ij), (proj, gate_a_spec), (proj, gate_b_spec)],
        [out_sd, out_sd, out_sd, out_sd,
         (SDS((s // tm, 1, d), F32), pl.BlockSpec((None, 1, tn), lambda i, j, k: (i, 0, j)))],
        ep_dm, csplit=EPILOGUE_CHUNK,
        comm=exchange1(grp4, [g_wo]),
    )
    sums4 = pair_sums(grp4, [g_wo], got)
    g_wouta = _mm_tn("d_w_out_a", ya_in, d_ya)
    g_wpw = _mm_tn("d_w_pw_b", v_act, d_yb)
    grp5 = ["w_out_a", "w_pw_b"]
    d_ya_in, got = _mm_nt("d_ya_in", d_ya, wouta, comm=exchange1(grp5, [g_wouta, g_wpw]))
    sums5 = pair_sums(grp5, [g_wouta, g_wpw], got)
    d_v, _ = _mm_nt("d_v", d_yb, wpw)
    d_cv, part_ln = _mix_b_bwd1(d_v, cv, conf_ln_g, conf_ln_b, s, c)
    (d_b, part_wd, part_bglu), got = _mix_b_bwd2(d_cv, u_act, proj, b_glu, wd, s, c, comm=_rs2(sums3))
    lands.update(zip(grp3, got))
    d_a, part_wa = _mix_a_bwd(d_ya_in, proj, wa, s, c)

    nb = nin // c
    gblk = d // c
    lo = [0, 3, 5, 5 + gblk]
    hi = [3, 5, 5 + gblk, 5 + 2 * gblk]
    pieces = [d_a, d_b, d_ga, d_gb]

    def active(q, ax):
        return lambda ids: jnp.logical_and(ids[ax] >= lo[q], ids[ax] < hi[q])

    def piece_spec(q, rows_, ax, row0=0):
        def index(i, j, k):
            ids = (i, j, k)
            col = jnp.clip(ids[ax] - lo[q], 0, hi[q] - lo[q] - 1)
            row = i + row0 if ax == 2 else jnp.where(active(q, ax)(ids), k, 0)
            return (row, col)

        return pl.BlockSpec((rows_, c), index)

    tkw = _tile(s, 1024)
    (g_win,), got = _fmm(
        "d_w_in", (1, nb, s // tkw),
        [(n1, pl.BlockSpec((tkw, d), lambda i, j, k: (k, 0)))]
        + [(pieces[q], piece_spec(q, tkw, 1)) for q in range(4)],
        [(0, 1 + q, TN, 0, active(q, 1)) for q in range(4)], [(d, c)], [],
        [(SDS((d, nin), BF16), pl.BlockSpec((d, c), lambda i, j, k: (0, j)))], _ep_bf16,
        comm=_rs2(sums4 + sums5),
    )
    lands.update(zip(grp4 + grp5, got))

    grp6 = ["w_in"]
    n_half = (s // th) // 2

    def d_n1_rows(name, row0, n_tiles, comm, into):
        return _fmm(
            name, (n_tiles, 1, nb),
            [(pieces[q], piece_spec(q, th, 2, row0)) for q in range(4)]
            + [(win, pl.BlockSpec((d, c), lambda i, j, k: (0, k)))],
            [(q, 4, NT, 0, active(q, 2)) for q in range(4)], [(th, d)], [],
            [(SDS((s, d), BF16), pl.BlockSpec((th, d), lambda i, j, k: (i + row0, 0)))], _ep_bf16,
            comm=comm, into=into,
        )

    (d_n1,), got = d_n1_rows("d_n1_a", 0, n_half, exchange1(grp6, [g_win]), None)
    (d_n1,), got = d_n1_rows("d_n1_b", n_half, s // th - n_half, _rs2(pair_sums(grp6, [g_win], got)), d_n1)
    lands.update(zip(grp6, got))
    dx, part_mix = _norm_bwd("d_x", d_n1, x2, dh1, g_mix, False)

    small_parts = [
        jnp.sum(part_mix, axis=0),
        jnp.sum(part_bglu, axis=0),
        jnp.sum(part_ln[:, 2], axis=0),
        jnp.sum(part_ln[:, 0], axis=0),
        jnp.sum(part_ln[:, 1], axis=0),
        jnp.sum(part_bpw, axis=0),
        jnp.sum(part_ffn, axis=0),
        jnp.sum(part_ple, axis=0),
        jnp.sum(part_fin[:, 0], axis=0),
        jnp.sum(part_wa, axis=0),
        jnp.sum(part_wd, axis=0),
        jnp.broadcast_to(jnp.sum(part_fin[:, 1, 0]), (c,)),
    ]
    small_shapes = [(1, d), (1, 2 * c), (1, c), (1, c), (1, c), (1, d), (1, d), (1, d), (d,), (kpa, c), (kpb, c), (c,)]
    total = _all_reduce_small(_pack(small_parts, c))
    (gr_g_mix, gr_b_glu, gr_dw_b, gr_ln_g, gr_ln_b, gr_b_pw, gr_g_ffn, gr_g_ple, gr_g_final, gr_wa, gr_wd, loss_row) = _unpack(total, small_shapes)
    loss = loss_row[0]
    my = _dev_index(_place())
    csh = conv_a_w.shape[-1]
    gr_conv_a = lax.dynamic_slice_in_dim(gr_wa[:CONV_A_K], my * csh, csh, axis=1)[None]
    gr_conf_dw = lax.dynamic_slice_in_dim(gr_wd[:CONF_K], my * csh, csh, axis=1)[None]

    big_m = dict(w_in=m_w_in, w_out_a=m_w_out_a, w_pw_b=m_w_pw_b, w_ple_proj=m_w_ple_proj, w_o=m_w_o,
                 w_ple_gate=m_w_ple_gate, w_gate=m_w_gate, w_up=m_w_up, w_down=m_w_down)
    big_v = dict(w_in=v_w_in, w_out_a=v_w_out_a, w_pw_b=v_w_pw_b, w_ple_proj=v_w_ple_proj, w_o=v_w_o,
                 w_ple_gate=v_w_ple_gate, w_gate=v_w_gate, w_up=v_w_up, w_down=v_w_down)
    big_out = {}
    for nm in weight:
        res = _adamw_big("adamw_" + nm, lands[nm], weight[nm][0], big_m[nm][0], big_v[nm][0])
        big_out[nm] = [r[None] for r in res]

    small_names = ["g_mix", "conv_a_w", "b_glu", "conf_dw_w", "conf_dw_b", "conf_ln_g", "conf_ln_b", "b_pw_b", "g_ffn", "g_ple", "g_final"]
    small_g = [gr_g_mix, gr_conv_a, gr_b_glu, gr_conf_dw, gr_dw_b, gr_ln_g, gr_ln_b, gr_b_pw, gr_g_ffn, gr_g_ple, gr_g_final]
    small_w = [g_mix, conv_a_w, b_glu, conf_dw_w, conf_dw_b, conf_ln_g, conf_ln_b, b_pw_b, g_ffn, g_ple, g_final]
    small_m = [m_g_mix, m_conv_a_w, m_b_glu, m_conf_dw_w, m_conf_dw_b, m_conf_ln_g, m_conf_ln_b, m_b_pw_b, m_g_ffn, m_g_ple, m_g_final]
    small_v = [v_g_mix, v_conv_a_w, v_b_glu, v_conf_dw_w, v_conf_dw_b, v_conf_ln_g, v_conf_ln_b, v_b_pw_b, v_g_ffn, v_g_ple, v_g_final]
    shp = [tuple(w.shape) for w in small_w]
    small_g = [g.reshape(sh) for g, sh in zip(small_g, shp)]
    sd, sm, sv = _adamw_small(_pack(small_g, 128), _pack(small_w, 128), _pack(small_m, 128), _pack(small_v, 128))
    small_out = {}
    for nm, g, dl, mm, vv in zip(small_names, small_g, _unpack(sd, shp), _unpack(sm, shp), _unpack(sv, shp)):
        small_out[nm] = [g, dl, mm, vv]

    order = ["g_mix", "w_in", "conv_a_w", "w_out_a", "b_glu", "conf_dw_w", "conf_dw_b", "conf_ln_g", "conf_ln_b", "w_pw_b", "b_pw_b", "w_o", "g_ffn", "w_gate", "w_up", "w_down", "g_ple", "w_ple_gate", "w_ple_proj", "g_final"]
    allo = {**big_out, **small_out}
    outs = [loss, dx[None]]
    for q in range(4):
        outs += [allo[nm][q] for nm in order]
    return tuple(outs)
```

```python
import jax
import jax.numpy as jnp
from jax import lax
from jax.experimental import pallas as pl
from jax.experimental.pallas import tpu as pltpu

F32, BF16 = jnp.float32, jnp.bfloat16
EPS, LN_EPS = 1e-6, 1e-5
ADAM_LR, ADAM_B1, ADAM_B2, ADAM_EPS, ADAM_WD, ADAM_STEP = 0.001, 0.9, 0.999, 1e-08, 0.01, 10
CONV_A_K, CONF_K = 3, 31
NDEV = 8
NN = (((1,), (0,)), ((), ()))
NT = (((1,), (1,)), ((), ()))
TN = (((0,), (0,)), ((), ()))
V7X_VMEM_LIMIT_BYTES = 56 * 1024 * 1024
MESH = pl.DeviceIdType.MESH
SDS = jax.ShapeDtypeStruct
HALO_A, HALO_B = 16, 32
EPILOGUE_CHUNK = 256
CONV_ROWS = 32
CONF_ROWS = 16


def _tile(n, pref):
    t = min(n, pref)
    while n % t:
        t -= 8
    return t


def _sigmoid(x):
    return jax.nn.sigmoid(x)


def _params(sem=None):
    return pltpu.CompilerParams(vmem_limit_bytes=V7X_VMEM_LIMIT_BYTES, dimension_semantics=sem)


def _edge(grid, last):
    cond = None
    for ax, n in enumerate(grid):
        here = pl.program_id(ax) == (n - 1 if last else 0)
        cond = here if cond is None else jnp.logical_and(cond, here)
    return cond


def _join(*comms):
    ins, outs, alias, sems, spans = [], [], {}, [], []
    for cm in comms:
        spans.append((len(ins), len(outs), len(sems)))
        for i, o in cm["alias"].items():
            alias[len(ins) + i] = len(outs) + o
        ins += cm["ins"]
        outs += cm["outs"]
        sems += cm["sems"]

    def run(which):
        def f(i_refs, o_refs, s_refs):
            for cm, (a, b, c_) in zip(comms, spans):
                cm[which](
                    i_refs[a : a + len(cm["ins"])], o_refs[b : b + len(cm["outs"])], s_refs[c_ : c_ + len(cm["sems"])]
                )

        return f

    return dict(ins=ins, outs=outs, alias=alias, sems=sems, start=run("start"), finish=run("finish"))


def _call(body, name, grid, in_specs, args, out_specs, out_shape, scratch=(), sem=None, comm=None, alias=None):
    n_in, n_out, n_s = len(args), len(out_shape), len(scratch)
    alias = dict(alias or {})
    if comm is None:
        res = pl.pallas_call(
            body, name=name, grid=grid, in_specs=list(in_specs), out_specs=list(out_specs), out_shape=list(out_shape),
            scratch_shapes=list(scratch), input_output_aliases=alias, compiler_params=_params(sem),
        )(*args)
        return list(res), []
    n_ci, n_co = len(comm["ins"]), len(comm["outs"])

    def wrapped(*refs):
        ins = refs[:n_in]
        ci = refs[n_in : n_in + n_ci]
        o0 = n_in + n_ci
        outs = refs[o0 : o0 + n_out]
        co = refs[o0 + n_out : o0 + n_out + n_co]
        s0 = o0 + n_out + n_co
        sc = refs[s0 : s0 + n_s]
        cs = refs[s0 + n_s :]
        pl.when(_edge(grid, False))(lambda: comm["start"](ci, co, cs))
        body(*ins, *outs, *sc)
        pl.when(_edge(grid, True))(lambda: comm["finish"](ci, co, cs))

    hbm = pl.BlockSpec(memory_space=pl.ANY)
    res = pl.pallas_call(
        wrapped,
        name=name,
        grid=grid,
        in_specs=list(in_specs) + [hbm] * n_ci,
        out_specs=list(out_specs) + [hbm] * n_co,
        out_shape=list(out_shape) + list(comm["outs"]),
        scratch_shapes=list(scratch) + list(comm["sems"]),
        input_output_aliases={**alias, **{n_in + i: n_out + o for i, o in comm["alias"].items()}},
        compiler_params=_params(("arbitrary",) * len(grid)),
    )(*args, *comm["ins"])
    return list(res[:n_out]), list(res[n_out:])


def _col_chunks(n, pref):
    out, c0 = [], 0
    while c0 < n:
        w = min(pref, n - c0)
        out.append((c0, w))
        c0 += w
    return out


def _fmm(name, grid, operands, terms, acc_shapes, extras, outs, epilogue, comm=None, csplit=None, into=None):
    n_p, n_e, n_o, n_a = len(operands), len(extras), len(outs), len(acc_shapes)
    nk = grid[-1]
    kax = len(grid) - 1
    simple = nk == 1 and all(t[4] is None for t in terms)
    alias = None
    if into is not None:
        extras = list(extras) + [(into, pl.BlockSpec(memory_space=pl.ANY))]
        alias = {n_p + n_e: 0}
        n_e += 1
    if csplit is not None:
        assert simple and into is None and all(t[2] in (NN, NT) and (len(t) <= 5 or not t[5]) for t in terms)
        tn_ = acc_shapes[0][1]
        chunks = _col_chunks(tn_, csplit)

    def dot(a, b, dims):
        if a.dtype != BF16:
            a = a.astype(BF16)
        if b.dtype != BF16:
            b = b.astype(BF16)
        return lax.dot_general(a, b, dims, preferred_element_type=F32)

    def value(refs, term):
        slabs = term[5] if len(term) > 5 else 0
        if not slabs:
            return dot(refs[term[0]][...], refs[term[1]][...], term[2])
        tot = None
        for sl in range(slabs):
            d = dot(refs[term[0]][sl], refs[term[1]][sl], term[2])
            tot = d if tot is None else tot + d
        return tot

    def always(refs):
        parts = [None] * n_a
        for term in terms:
            if term[4] is None:
                d = value(refs, term)
                parts[term[3]] = d if parts[term[3]] is None else parts[term[3]] + d
        return parts

    def chunked(refs, ex, os_):
        cols = lambda ref, c0, w: ref.at[:, pl.ds(c0, w)] if ref.shape[-1] == tn_ else ref
        for c0, w in chunks:
            parts = [None] * n_a
            for term in terms:
                b_ref = refs[term[1]]
                b = b_ref[:, pl.ds(c0, w)] if term[2] == NN else b_ref[pl.ds(c0, w), :]
                d = dot(refs[term[0]][...], b, term[2])
                parts[term[3]] = d if parts[term[3]] is None else parts[term[3]] + d
            epilogue(parts, [cols(e, c0, w) for e in ex], [cols(o, c0, w) for o in os_])

    def body(*refs):
        ex = refs[n_p : n_p + n_e]
        os_ = refs[n_p + n_e : n_p + n_e + n_o]
        accs = refs[n_p + n_e + n_o :]
        if simple and csplit is not None:
            chunked(refs, ex, os_)
            return
        if simple:
            epilogue(always(refs), ex, os_)
            return
        ids = [pl.program_id(ax) for ax in range(len(grid))]
        k = ids[kax]

        @pl.when(k == 0)
        def _():
            for acc in accs:
                acc[...] = jnp.zeros(acc.shape, F32)

        for ai, part in enumerate(always(refs)):
            if part is not None:
                accs[ai][...] += part
        for term in terms:
            if term[4] is not None:

                def add(term=term):
                    accs[term[3]][...] += value(refs, term)

                pl.when(term[4](ids))(add)

        @pl.when(k == nk - 1)
        def _():
            epilogue([acc[...] for acc in accs], ex, os_)

    return _call(
        body,
        name,
        grid,
        [o[1] for o in operands] + [e[1] for e in extras],
        [o[0] for o in operands] + [e[0] for e in extras],
        [o[1] for o in outs],
        [o[0] for o in outs],
        scratch=[] if simple else [pltpu.VMEM(s, F32) for s in acc_shapes],
        sem=("parallel",) * kax + ("arbitrary",),
        comm=comm,
        alias=alias,
    )


def _rms_bwd(dn_raw, h, g):
    r = lax.rsqrt(jnp.mean(h * h, axis=-1, keepdims=True) + EPS)
    hn = h * r
    dg = jnp.sum(dn_raw * hn, axis=0, keepdims=True)
    dn = dn_raw * g
    dh = r * (dn - hn * jnp.mean(dn * hn, axis=-1, keepdims=True))
    return dh, dg


def _rms_fwd(name, h, g):
    s, d = h.shape
    ts = _tile(s, 512)

    def body(h_ref, g_ref, o_ref):
        x = h_ref[...]
        r = lax.rsqrt(jnp.mean(x * x, axis=-1, keepdims=True) + EPS)
        o_ref[...] = (x * r * g_ref[...]).astype(BF16)

    return pl.pallas_call(
        body,
        name=name,
        grid=(s // ts,),
        in_specs=[pl.BlockSpec((ts, d), lambda i: (i, 0)), pl.BlockSpec((1, d), lambda i: (0, 0))],
        out_specs=pl.BlockSpec((ts, d), lambda i: (i, 0)),
        out_shape=SDS((s, d), BF16),
        compiler_params=_params(("parallel",)),
    )(h, g)


def _norm_bwd(name, dn, h, dres, g, want_bf16):
    s, d = h.shape
    ts = _tile(s, 512)

    def body(dn_r, h_r, dres_r, g_r, *outs):
        dh, dg = _rms_bwd(dn_r[...].astype(F32), h_r[...], g_r[...])
        dh = dres_r[...] + dh
        outs[0][...] = dh
        if want_bf16:
            outs[1][...] = dh.astype(BF16)
        outs[-1][...] = dg

    blk = pl.BlockSpec((ts, d), lambda i: (i, 0))
    part = pl.BlockSpec((None, 1, d), lambda i: (i, 0, 0))
    return pl.pallas_call(
        body,
        name=name,
        grid=(s // ts,),
        in_specs=[blk, blk, blk, pl.BlockSpec((1, d), lambda i: (0, 0))],
        out_specs=[blk] + ([blk] if want_bf16 else []) + [part],
        out_shape=[SDS((s, d), F32)] + ([SDS((s, d), BF16)] if want_bf16 else []) + [SDS((s // ts, 1, d), F32)],
        compiler_params=_params(("parallel",)),
    )(dn, h, dres, g)


def _prev_halo(ts, hb):
    r = ts // hb
    return lambda i: jnp.maximum(i * r - 1, 0)


def _next_halo(ts, hb, s):
    r = ts // hb
    last = s // hb - 1
    return lambda i: jnp.minimum((i + 1) * r, last)


def _shift_copies(buf, sh):
    n = sh.shape[1]
    for j in range(1, 8):
        sh[j - 1, pl.ds(0, n), :] = buf[pl.ds(j, n), :]


def _tap(buf, sh, r0, off, rows):
    j = off % 8
    start = pl.multiple_of(r0 + (off - j), 8)
    if j == 0:
        return buf[pl.ds(start, rows), :]
    return sh[j - 1, pl.ds(start, rows), :]


def _mix_a_fwd(proj, wa, s, c):
    ts, hb = _tile(s, 256), HALO_A
    prev = _prev_halo(ts, hb)

    def body(ah, ab, ac, hh, hc, w, o, buf):
        i = pl.program_id(0)
        zh = hc[...].astype(F32) * hh[...].astype(F32)
        buf[pl.ds(0, hb), :] = jnp.where(i == 0, 0.0, zh)
        buf[pl.ds(hb, ts), :] = ac[...].astype(F32) * ah[...].astype(F32)
        for r0 in range(0, ts, CONV_ROWS):
            cz = jnp.zeros((CONV_ROWS, c), F32)
            for k in range(CONV_A_K):
                cz = cz + w[k : k + 1, :] * buf[pl.ds(hb + r0 - (CONV_A_K - 1) + k, CONV_ROWS), :]
            o[pl.ds(r0, CONV_ROWS), :] = (ab[pl.ds(r0, CONV_ROWS), :].astype(F32) * cz).astype(BF16)

    main = lambda cb: pl.BlockSpec((ts, c), lambda i: (i, cb))
    halo = lambda cb: pl.BlockSpec((hb, c), lambda i: (prev(i), cb))
    return pl.pallas_call(
        body,
        name="mix_a_fwd",
        grid=(s // ts,),
        in_specs=[main(0), main(1), main(2), halo(0), halo(2), pl.BlockSpec(wa.shape, lambda i: (0, 0))],
        out_specs=pl.BlockSpec((ts, c), lambda i: (i, 0)),
        out_shape=SDS((s, c), BF16),
        scratch_shapes=[pltpu.VMEM((hb + ts, c), F32)],
        compiler_params=_params(("parallel",)),
    )(proj, proj, proj, proj, proj, wa)


def _mix_b_fwd(proj, b_glu, wd, bd, lg, lb, s, c, comm=None):
    ts, hb = _tile(s, 256), HALO_B
    prev = _prev_halo(ts, hb)

    def body(gv, gg, hv, hg, bglu, w, bd_r, lg_r, lb_r, v_o, u_o, cv_o, buf, sh):
        i = pl.program_id(0)
        bv, bg = bglu[:, 0:c], bglu[:, c : 2 * c]
        uh = (hv[...].astype(F32) + bv) * _sigmoid(hg[...].astype(F32) + bg)
        buf[pl.ds(0, hb), :] = jnp.where(i == 0, 0.0, uh)
        u = (gv[...].astype(F32) + bv) * _sigmoid(gg[...].astype(F32) + bg)
        buf[pl.ds(hb, ts), :] = u
        u_o[...] = u.astype(BF16)
        _shift_copies(buf, sh)

        def chunk(ci, carry):
            r0 = pl.multiple_of(ci * CONF_ROWS, CONF_ROWS)
            acc = jnp.zeros((CONF_ROWS, c), F32)
            for k in range(CONF_K):
                acc = acc + w[k : k + 1, :] * _tap(buf, sh, r0, hb - (CONF_K - 1) + k, CONF_ROWS)
            cv_o[pl.ds(r0, CONF_ROWS), :] = acc + bd_r[...]
            return carry

        lax.fori_loop(0, ts // CONF_ROWS, chunk, 0)
        cv = cv_o[...]
        mu = jnp.mean(cv, axis=-1, keepdims=True)
        xc = cv - mu
        rs = lax.rsqrt(jnp.mean(xc * xc, axis=-1, keepdims=True) + LN_EPS)
        ln = xc * rs * lg_r[...] + lb_r[...]
        v_o[...] = (ln * _sigmoid(ln)).astype(BF16)

    main = lambda cb: pl.BlockSpec((ts, c), lambda i: (i, cb))
    halo = lambda cb: pl.BlockSpec((hb, c), lambda i: (prev(i), cb))
    full = lambda a: pl.BlockSpec(a.shape, lambda i: (0, 0))
    out = pl.BlockSpec((ts, c), lambda i: (i, 0))
    return _call(
        body,
        "mix_b_fwd",
        (s // ts,),
        [main(3), main(4), halo(3), halo(4), full(b_glu), full(wd), full(bd), full(lg), full(lb)],
        [proj, proj, proj, proj, b_glu, wd, bd, lg, lb],
        [out, out, out],
        [SDS((s, c), BF16), SDS((s, c), BF16), SDS((s, c), F32)],
        scratch=[pltpu.VMEM((hb + ts, c), F32), pltpu.VMEM((7, hb + ts - 8, c), F32)],
        sem=("parallel",),
        comm=comm,
    )


def _mix_b_bwd1(d_v, cv, lg, lb, s, c):
    ts = _tile(s, 256)

    def body(dv_r, cv_r, lg_r, lb_r, dcv_o, part_o):
        cv_ = cv_r[...]
        mu = jnp.mean(cv_, axis=-1, keepdims=True)
        xc = cv_ - mu
        rs = lax.rsqrt(jnp.mean(xc * xc, axis=-1, keepdims=True) + LN_EPS)
        xh = xc * rs
        ln = xh * lg_r[...] + lb_r[...]
        sg = _sigmoid(ln)
        d_ln = dv_r[...].astype(F32) * (sg * (1.0 + ln * (1.0 - sg)))
        dy = d_ln * lg_r[...]
        d_cv = rs * (dy - jnp.mean(dy, axis=-1, keepdims=True) - xh * jnp.mean(dy * xh, axis=-1, keepdims=True))
        dcv_o[...] = d_cv
        part_o[0:1, :] = jnp.sum(d_ln * xh, axis=0, keepdims=True)
        part_o[1:2, :] = jnp.sum(d_ln, axis=0, keepdims=True)
        part_o[2:3, :] = jnp.sum(d_cv, axis=0, keepdims=True)

    blk = pl.BlockSpec((ts, c), lambda i: (i, 0))
    full = lambda a: pl.BlockSpec(a.shape, lambda i: (0, 0))
    return pl.pallas_call(
        body,
        name="mix_b_bwd_ln",
        grid=(s // ts,),
        in_specs=[blk, blk, full(lg), full(lb)],
        out_specs=[blk, pl.BlockSpec((None, 3, c), lambda i: (i, 0, 0))],
        out_shape=[SDS((s, c), F32), SDS((s // ts, 3, c), F32)],
        compiler_params=_params(("parallel",)),
    )(d_v, cv, lg, lb)


def _mix_b_bwd2(d_cv, u, proj, b_glu, wd, s, c, comm=None):
    ts, hb = _tile(s, 256), HALO_B
    prev, nxt = _prev_halo(ts, hb), _next_halo(ts, hb, s)
    n_t = s // ts
    kp = wd.shape[0]

    def body(dcv, dcv_n, u_m, u_p, gv, gg, bglu, w, d_o, dwd_o, dbglu_o, dbuf, ubuf, dub, dsh, ush, dwacc):
        i = pl.program_id(0)
        dbuf[pl.ds(0, ts), :] = dcv[...]
        dbuf[pl.ds(ts, hb), :] = jnp.where(i == n_t - 1, 0.0, dcv_n[...])
        ubuf[pl.ds(0, hb), :] = jnp.where(i == 0, 0.0, u_p[...].astype(F32))
        ubuf[pl.ds(hb, ts), :] = u_m[...].astype(F32)
        _shift_copies(dbuf, dsh)
        _shift_copies(ubuf, ush)
        dwacc[...] = jnp.zeros(dwacc.shape, F32)

        def chunk(ci, carry):
            r0 = pl.multiple_of(ci * CONF_ROWS, CONF_ROWS)
            acc = jnp.zeros((CONF_ROWS, c), F32)
            dc = dbuf[pl.ds(r0, CONF_ROWS), :]
            for k in range(CONF_K):
                acc = acc + w[k : k + 1, :] * _tap(dbuf, dsh, r0, (CONF_K - 1) - k, CONF_ROWS)
                prod = dc * _tap(ubuf, ush, r0, hb - (CONF_K - 1) + k, CONF_ROWS)
                fold = prod[0:8]
                for a in range(1, CONF_ROWS // 8):
                    fold = fold + prod[8 * a : 8 * a + 8]
                dwacc[pl.ds(8 * k, 8), :] += fold
            dub[pl.ds(r0, CONF_ROWS), :] = acc
            return carry

        lax.fori_loop(0, ts // CONF_ROWS, chunk, 0)
        for k in range(CONF_K):
            dwd_o[k : k + 1, :] = jnp.sum(dwacc[pl.ds(8 * k, 8), :], axis=0, keepdims=True)
        dwd_o[CONF_K:kp, :] = jnp.zeros((kp - CONF_K, c), F32)
        bv, bg = bglu[:, 0:c], bglu[:, c : 2 * c]
        d_u = dub[...]
        sg = _sigmoid(gg[...].astype(F32) + bg)
        d_gv = d_u * sg
        d_gg = d_u * (gv[...].astype(F32) + bv) * sg * (1.0 - sg)
        d_o[:, 0:c] = d_gv.astype(BF16)
        d_o[:, c : 2 * c] = d_gg.astype(BF16)
        dbglu_o[:, 0:c] = jnp.sum(d_gv, axis=0, keepdims=True)
        dbglu_o[:, c : 2 * c] = jnp.sum(d_gg, axis=0, keepdims=True)

    blk = lambda cb: pl.BlockSpec((ts, c), lambda i: (i, cb))
    full = lambda a: pl.BlockSpec(a.shape, lambda i: (0, 0))
    return _call(
        body,
        "mix_b_bwd_conv",
        (n_t,),
        [
            blk(0),
            pl.BlockSpec((hb, c), lambda i: (nxt(i), 0)),
            blk(0),
            pl.BlockSpec((hb, c), lambda i: (prev(i), 0)),
            blk(3),
            blk(4),
            full(b_glu),
            full(wd),
        ],
        [d_cv, d_cv, u, u, proj, proj, b_glu, wd],
        [
            pl.BlockSpec((ts, 2 * c), lambda i: (i, 0)),
            pl.BlockSpec((None, kp, c), lambda i: (i, 0, 0)),
            pl.BlockSpec((None, 1, 2 * c), lambda i: (i, 0, 0)),
        ],
        [SDS((s, 2 * c), BF16), SDS((n_t, kp, c), F32), SDS((n_t, 1, 2 * c), F32)],
        scratch=[
            pltpu.VMEM((ts + hb, c), F32), pltpu.VMEM((hb + ts, c), F32), pltpu.VMEM((ts, c), F32),
            pltpu.VMEM((7, hb + ts - 8, c), F32), pltpu.VMEM((7, hb + ts - 8, c), F32), pltpu.VMEM((8 * CONF_K, c), F32),
        ],
        sem=("parallel",),
        comm=comm,
    )


def _mix_a_bwd(d_ya, proj, wa, s, c):
    ts, hb = _tile(s, 256), HALO_A
    prev, nxt = _prev_halo(ts, hb), _next_halo(ts, hb, s)
    n_t = s // ts
    kp = wa.shape[0]

    def body(dya, dya_n, ah, ab, ac, ah_p, ac_p, ab_n, w, d_o, dwa_o, zbuf, dbuf, dzb):
        i = pl.program_id(0)
        zbuf[pl.ds(0, hb), :] = jnp.where(i == 0, 0.0, ac_p[...].astype(F32) * ah_p[...].astype(F32))
        zbuf[pl.ds(hb, ts), :] = ac[...].astype(F32) * ah[...].astype(F32)
        dbuf[pl.ds(0, ts), :] = dya[...].astype(F32) * ab[...].astype(F32)
        dbuf[pl.ds(ts, hb), :] = jnp.where(i == n_t - 1, 0.0, dya_n[...].astype(F32) * ab_n[...].astype(F32))
        dw_rows = [jnp.zeros((1, c), F32) for _ in range(CONV_A_K)]
        for r0 in range(0, ts, CONV_ROWS):
            cz = jnp.zeros((CONV_ROWS, c), F32)
            dz = jnp.zeros((CONV_ROWS, c), F32)
            dc = dbuf[pl.ds(r0, CONV_ROWS), :]
            for k in range(CONV_A_K):
                zk = zbuf[pl.ds(hb + r0 - (CONV_A_K - 1) + k, CONV_ROWS), :]
                cz = cz + w[k : k + 1, :] * zk
                dz = dz + w[k : k + 1, :] * dbuf[pl.ds(r0 + (CONV_A_K - 1) - k, CONV_ROWS), :]
                dw_rows[k] = dw_rows[k] + jnp.sum(dc * zk, axis=0, keepdims=True)
            d_o[pl.ds(r0, CONV_ROWS), c : 2 * c] = (dya[pl.ds(r0, CONV_ROWS), :].astype(F32) * cz).astype(BF16)
            dzb[pl.ds(r0, CONV_ROWS), :] = dz
        d_z = dzb[...]
        d_o[:, 0:c] = (d_z * ac[...].astype(F32)).astype(BF16)
        d_o[:, 2 * c : 3 * c] = (d_z * ah[...].astype(F32)).astype(BF16)
        for k in range(CONV_A_K):
            dwa_o[k : k + 1, :] = dw_rows[k]
        dwa_o[CONV_A_K:kp, :] = jnp.zeros((kp - CONV_A_K, c), F32)

    blk = lambda cb: pl.BlockSpec((ts, c), lambda i: (i, cb))
    hp = lambda cb: pl.BlockSpec((hb, c), lambda i: (prev(i), cb))
    hn = lambda cb: pl.BlockSpec((hb, c), lambda i: (nxt(i), cb))
    return pl.pallas_call(
        body,
        name="mix_a_bwd",
        grid=(n_t,),
        in_specs=[blk(0), hn(0), blk(0), blk(1), blk(2), hp(0), hp(2), hn(1), pl.BlockSpec(wa.shape, lambda i: (0, 0))],
        out_specs=[pl.BlockSpec((ts, 3 * c), lambda i: (i, 0)), pl.BlockSpec((None, kp, c), lambda i: (i, 0, 0))],
        out_shape=[SDS((s, 3 * c), BF16), SDS((n_t, kp, c), F32)],
        scratch_shapes=[pltpu.VMEM((hb + ts, c), F32), pltpu.VMEM((ts + hb, c), F32), pltpu.VMEM((ts, c), F32)],
        compiler_params=_params(("parallel",)),
    )(d_ya, d_ya, proj, proj, proj, proj, proj, proj, wa)


def _ep_bf16(accs, ex, os_):
    os_[0][...] = accs[0].astype(BF16)


def _mm_tn(name, a, b, tm=2048, tn=1024, tk=1024):
    m, k1 = a.shape
    n = b.shape[1]
    tm, tn, tk = _tile(k1, tm), _tile(n, tn), _tile(m, tk)
    return _fmm(
        name,
        (k1 // tm, n // tn, m // tk),
        [(a, pl.BlockSpec((tk, tm), lambda i, j, k: (k, i))), (b, pl.BlockSpec((tk, tn), lambda i, j, k: (k, j)))],
        [(0, 1, TN, 0, None)],
        [(tm, tn)],
        [],
        [(SDS((k1, n), BF16), pl.BlockSpec((tm, tn), lambda i, j, k: (i, j)))],
        _ep_bf16,
    )[0][0]


def _mm_nt(name, a, b, tm=1024, tn=1024, comm=None):
    m, kk = a.shape
    n = b.shape[0]
    tm, tn = _tile(m, tm), _tile(n, tn)
    outs, couts = _fmm(
        name,
        (m // tm, n // tn, 1),
        [(a, pl.BlockSpec((tm, kk), lambda i, j, k: (i, 0))), (b, pl.BlockSpec((tn, kk), lambda i, j, k: (j, 0)))],
        [(0, 1, NT, 0, None)],
        [(tm, tn)],
        [],
        [(SDS((m, n), BF16), pl.BlockSpec((tm, tn), lambda i, j, k: (i, j)))],
        _ep_bf16,
        comm=comm,
    )
    return outs[0], couts


def _dev_index(dev):
    return 4 * dev[0] + 2 * dev[1] + dev[2]


def _region(ref, kind, j, shard_shape):
    if kind == "col":
        ns = shard_shape[1]
        return ref.at[:, pl.ds(pl.multiple_of(j * ns, 128), ns)]
    if kind == "row":
        rs = shard_shape[0]
        return ref.at[pl.ds(pl.multiple_of(j * rs, 8), rs), :]
    return ref.at[j]


def _whole_shape(kind, shard_shape):
    if kind == "col":
        return (shard_shape[0], NDEV * shard_shape[1])
    if kind == "row":
        return (NDEV * shard_shape[0], shard_shape[1])
    return (NDEV,) + tuple(shard_shape)


def _place():
    return lax.axis_index("x"), lax.axis_index("y"), lax.axis_index("c")


def _proj_gather(n1, w_shard, early, late):
    s, d = n1.shape
    ns = w_shard.shape[1]
    pw = 2 * ns
    tm = _tile(s // 2, 1024)
    n_i = s // tm
    comm = _join(early, late)
    n_early = (len(early["ins"]), len(early["outs"]), len(early["sems"]))
    assert n_i >= 2 and not comm["alias"]
    x0, y0, _ = _place()
    order = jnp.stack([2 * x0 + y0, 2 * x0 + (1 - y0), 2 * (1 - x0) + y0, 2 * (1 - x0) + (1 - y0)]).astype(jnp.int32)
    n_ci, n_co = len(comm["ins"]), len(comm["outs"])

    def body(order_ref, n1_ref, wsh_ref, *rest):
        ci = rest[:n_ci]
        proj_ref, win_ref = rest[n_ci], rest[n_ci + 1]
        co = rest[n_ci + 2 : n_ci + 2 + n_co]
        wbuf, send, recv, fsend, frecv, loc, lsem = rest[n_ci + 2 + n_co : n_ci + 9 + n_co]
        cs = rest[n_ci + 9 + n_co :]
        u, i = pl.program_id(0), pl.program_id(1)
        x, y, c = _place()
        me, sib = (x, y, c), (x, y, 1 - c)
        chips = [(x, 1 - y), (1 - x, y), (1 - x, 1 - y)]
        peers = [sib] + [(*ch, c) for ch in chips]
        reg = lambda dev: win_ref.at[:, pl.ds(pl.multiple_of(_dev_index(dev) * ns, 128), ns)]
        sends = [_remote(wsh_ref, reg(me), send.at[k], recv.at[k], peers[k]) for k in range(4)]
        arrivals = [_remote(reg(peers[k]), reg(peers[k]), send.at[k], recv.at[k], peers[k]) for k in range(4)]
        passes = [_remote(reg((*ch, c)), reg((*ch, c)), fsend.at[j], frecv.at[j], sib) for j, ch in enumerate(chips)]
        passed = [_remote(reg((*ch, 1 - c)), reg((*ch, 1 - c)), fsend.at[j], frecv.at[j], sib) for j, ch in enumerate(chips)]
        mine = lambda: pltpu.make_async_copy(wsh_ref, reg(me), loc.at[0])

        def load(unit):
            col0 = pl.multiple_of(order_ref[unit] * pw, 128)
            return pltpu.make_async_copy(win_ref.at[:, pl.ds(col0, pw)], wbuf.at[unit % 2], lsem.at[unit % 2])

        a, b, e = n_early
        early_refs = (ci[:a], co[:b], cs[:e])
        late_refs = (ci[a:], co[b:], cs[e:])

        @pl.when(jnp.logical_and(u == 0, i == 0))
        def _():
            mine().start()
            for snd in sends[:3]:
                snd().start()
            early["start"](*early_refs)
            mine().wait()
            arrivals[0]().wait_recv()
            load(0).start()
            load(0).wait()

        @pl.when(jnp.logical_and(u == 1, i == 0))
        def _():
            sends[3]().start()

        @pl.when(jnp.logical_and(u == 2, i == 0))
        def _():
            late["start"](*late_refs)

        for nxt in range(1, 4):

            @pl.when(jnp.logical_and(u == nxt - 1, i == n_i - 1))
            def _(nxt=nxt):
                passed[nxt - 1]().wait_recv()
                load(nxt).start()

            @pl.when(jnp.logical_and(u == nxt, i == 0))
            def _(nxt=nxt):
                load(nxt).wait()

        proj_ref[...] = jnp.dot(n1_ref[...], wbuf[u % 2], preferred_element_type=F32).astype(BF16)

        for nxt in range(1, 4):

            @pl.when(jnp.logical_and(u == nxt - 1, i == n_i - 2))
            def _(nxt=nxt):
                arrivals[nxt]().wait_recv()
                passes[nxt - 1]().start()

        @pl.when(jnp.logical_and(u == 3, i == n_i - 1))
        def _():
            for snd in sends + passes:
                snd().wait_send()
            comm["finish"](ci, co, cs)

    hbm = pl.BlockSpec(memory_space=pl.ANY)
    dma = pltpu.SemaphoreType.DMA
    res = pl.pallas_call(
        body,
        name="proj",
        grid_spec=pltpu.PrefetchScalarGridSpec(
            num_scalar_prefetch=1,
            grid=(4, n_i),
            in_specs=[pl.BlockSpec((tm, d), lambda u, i, order_ref: (i, 0)), hbm] + [hbm] * n_ci,
            out_specs=[pl.BlockSpec((tm, pw), lambda u, i, order_ref: (i, order_ref[u])), hbm] + [hbm] * n_co,
            scratch_shapes=[pltpu.VMEM((2, d, pw), BF16), dma((4,)), dma((4,)), dma((3,)), dma((3,)), dma((1,)), dma((2,))]
            + list(comm["sems"]),
        ),
        out_shape=[SDS((s, NDEV * ns), BF16), SDS((d, NDEV * ns), BF16)] + list(comm["outs"]),
        compiler_params=_params(("arbitrary", "arbitrary")),
    )(order, n1, w_shard, *comm["ins"])
    return res[0], res[1], list(res[2:])


def _peer(me, r):
    x, y, c = me
    return (1 - x if r & 4 else x, 1 - y if r & 2 else y, 1 - c if r & 1 else c)


def _remote(src, dst, send_sem, recv_sem, to):
    return lambda: pltpu.make_async_remote_copy(
        src_ref=src, dst_ref=dst, send_sem=send_sem, recv_sem=recv_sem, device_id=to, device_id_type=MESH
    )


def _run(pairs, locals_, start):
    if start:
        for cp in locals_:
            cp.start()
        for snd, _ in pairs:
            snd().start()
    else:
        for snd, arr in pairs:
            arr().wait_recv()
            snd().wait_send()
        for cp in locals_:
            cp.wait()


def _stage(ins, outs, alias, sems, build):
    return dict(
        ins=list(ins), outs=list(outs), alias=alias, sems=list(sems),
        start=lambda i, o, s: _run(*build(i, o, s), True),
        finish=lambda i, o, s: _run(*build(i, o, s), False),
    )


def _ag1(shards, kinds):
    n_t = len(shards)
    shapes = [tuple(sh.shape) for sh in shards]

    def build(srcs, dsts, sems):
        send, recv, loc = sems
        x, y, c = _place()
        me = (x, y, c)
        peers = [(x, y, 1 - c), (1 - x, y, c), (x, 1 - y, c), (1 - x, 1 - y, c)]
        reg = lambda t, dev: _region(dsts[t], kinds[t], _dev_index(dev), shapes[t])
        pairs = []
        for t in range(n_t):
            for k, peer in enumerate(peers):
                snd = _remote(srcs[t], reg(t, me), send.at[t, k], recv.at[t, k], peer)
                arr = _remote(reg(t, peer), reg(t, peer), send.at[t, k], recv.at[t, k], peer)
                pairs.append((snd, arr))
        mine = [pltpu.make_async_copy(srcs[t], reg(t, me), loc.at[t]) for t in range(n_t)]
        return pairs, mine

    outs = [SDS(_whole_shape(kinds[t], shapes[t]), shards[t].dtype) for t in range(n_t)]
    dma = pltpu.SemaphoreType.DMA
    return _stage(shards, outs, {}, [dma((n_t, 4)), dma((n_t, 4)), dma((n_t,))], build)


def _ag_direct(shards, kinds):
    n_t = len(shards)
    shapes = [tuple(sh.shape) for sh in shards]

    def build(srcs, dsts, sems):
        send, recv, loc = sems
        me = _place()
        reg = lambda t, dev: _region(dsts[t], kinds[t], _dev_index(dev), shapes[t])
        pairs = []
        for t in range(n_t):
            for r in range(1, NDEV):
                peer = _peer(me, r)
                snd = _remote(srcs[t], reg(t, me), send.at[t, r - 1], recv.at[t, r - 1], peer)
                arr = _remote(reg(t, peer), reg(t, peer), send.at[t, r - 1], recv.at[t, r - 1], peer)
                pairs.append((snd, arr))
        mine = [pltpu.make_async_copy(srcs[t], reg(t, me), loc.at[t]) for t in range(n_t)]
        return pairs, mine

    outs = [SDS(_whole_shape(kinds[t], shapes[t]), shards[t].dtype) for t in range(n_t)]
    dma = pltpu.SemaphoreType.DMA
    return _stage(shards, outs, {}, [dma((n_t, 7)), dma((n_t, 7)), dma((n_t,))], build)


def _ag2(wholes, kinds, shapes):
    n_t = len(wholes)

    def build(_, dsts, sems):
        send, recv = sems
        x, y, c = _place()
        sib = (x, y, 1 - c)
        chips = [(1 - x, y), (x, 1 - y), (1 - x, 1 - y)]
        reg = lambda t, dev: _region(dsts[t], kinds[t], _dev_index(dev), shapes[t])
        pairs = []
        for t in range(n_t):
            for j, chip in enumerate(chips):
                snd = _remote(reg(t, (*chip, c)), reg(t, (*chip, c)), send.at[t, j], recv.at[t, j], sib)
                arr = _remote(reg(t, (*chip, 1 - c)), reg(t, (*chip, 1 - c)), send.at[t, j], recv.at[t, j], sib)
                pairs.append((snd, arr))
        return pairs, []

    outs = [SDS(w.shape, w.dtype) for w in wholes]
    dma = pltpu.SemaphoreType.DMA
    return _stage(wholes, outs, {t: t for t in range(n_t)}, [dma((n_t, 3)), dma((n_t, 3))], build)


def _chip_of(q):
    return (q >> 1, q & 1)


def _rs1(wholes, kinds, shapes):
    n_t = len(wholes)

    def build(srcs, outs, sems):
        send, recv = sems
        x, y, c = _place()
        sib = (x, y, 1 - c)
        pairs = []
        for t in range(n_t):
            for q in range(4):
                theirs = _region(srcs[t], kinds[t], _dev_index((*_chip_of(q), 1 - c)), shapes[t])
                pairs.append((
                    _remote(theirs, outs[t].at[q], send.at[t, q], recv.at[t, q], sib),
                    _remote(outs[t].at[q], outs[t].at[q], send.at[t, q], recv.at[t, q], sib),
                ))
        return pairs, []

    slabs = [SDS((4,) + tuple(shapes[t]), wholes[t].dtype) for t in range(n_t)]
    dma = pltpu.SemaphoreType.DMA
    return _stage(wholes, slabs, {}, [dma((n_t, 4)), dma((n_t, 4))], build)


def _rs2(pair_sums):
    n_t = len(pair_sums)

    def build(srcs, lands, sems):
        send, recv, loc = sems
        x, y, c = _place()
        my_chip = 2 * x + y
        pairs, mine = [], []
        for t in range(n_t):
            for j, (px, py) in enumerate([(1 - x, y), (x, 1 - y), (1 - x, 1 - y)]):
                q = 2 * px + py
                pairs.append((
                    _remote(srcs[t].at[q], lands[t].at[my_chip], send.at[t, j], recv.at[t, j], (px, py, c)),
                    _remote(lands[t].at[q], lands[t].at[q], send.at[t, j], recv.at[t, j], (px, py, c)),
                ))
            mine.append(pltpu.make_async_copy(srcs[t].at[my_chip], lands[t].at[my_chip], loc.at[t]))
        return pairs, mine

    outs = [SDS(q.shape, q.dtype) for q in pair_sums]
    dma = pltpu.SemaphoreType.DMA
    return _stage(pair_sums, outs, {}, [dma((n_t, 3)), dma((n_t, 3)), dma((n_t,))], build)


def _pair_sum(name, whole, kind, got):
    _, rows, cols = got.shape
    tr = _tile(rows, 256)
    n_r = rows // tr
    core = lax.axis_index("c").astype(jnp.int32).reshape(1)

    def body(_, a, b, o):
        o[...] = (a[...].astype(F32) + b[...].astype(F32)).astype(BF16)

    if kind == "col":
        own = pl.BlockSpec((tr, cols), lambda q, i, c_ref: (i, 2 * q + c_ref[0]))
    elif kind == "row":
        own = pl.BlockSpec((tr, cols), lambda q, i, c_ref: ((2 * q + c_ref[0]) * n_r + i, 0))
    else:
        own = pl.BlockSpec((None, tr, cols), lambda q, i, c_ref: (2 * q + c_ref[0], i, 0))
    slab = pl.BlockSpec((None, tr, cols), lambda q, i, c_ref: (q, i, 0))
    return pl.pallas_call(
        body,
        name=name,
        grid_spec=pltpu.PrefetchScalarGridSpec(
            num_scalar_prefetch=1, grid=(4, n_r), in_specs=[own, slab], out_specs=slab
        ),
        out_shape=SDS(got.shape, BF16),
        compiler_params=_params(("parallel", "parallel")),
    )(core, whole, got)


def _all_reduce_small(part):
    r_, c_ = part.shape

    def body(src, land, total, send_sems, recv_sems):
        me = _place()
        my = _dev_index(me)
        land[my] = src[...]

        def copy(r):
            peer = _peer(me, r)
            return pltpu.make_async_remote_copy(
                src_ref=src,
                dst_ref=land.at[my],
                send_sem=send_sems.at[r - 1],
                recv_sem=recv_sems.at[r - 1],
                device_id=peer,
                device_id_type=MESH,
            )

        def arrival(r):
            peer = _peer(me, r)
            slab = land.at[_dev_index(peer)]
            return pltpu.make_async_remote_copy(
                src_ref=slab,
                dst_ref=slab,
                send_sem=send_sems.at[r - 1],
                recv_sem=recv_sems.at[r - 1],
                device_id=peer,
                device_id_type=MESH,
            )

        sends = [copy(r) for r in range(1, NDEV)]
        for cp in sends:
            cp.start()
        for r in range(1, NDEV):
            arrival(r).wait_recv()
        for cp in sends:
            cp.wait_send()
        acc = land[0]
        for d in range(1, NDEV):
            acc = acc + land[d]
        total[...] = acc

    vmem = pl.BlockSpec(memory_space=pltpu.VMEM)
    return pl.pallas_call(
        body,
        name="all_reduce_small",
        in_specs=[vmem],
        out_specs=[vmem, vmem],
        out_shape=[SDS((NDEV, r_, c_), F32), SDS((r_, c_), F32)],
        scratch_shapes=[pltpu.SemaphoreType.DMA((7,)), pltpu.SemaphoreType.DMA((7,))],
    )(part)[1]


def _adamw_math(g, w, m, v):
    m2 = ADAM_B1 * m + (1.0 - ADAM_B1) * g
    v2 = ADAM_B2 * v + (1.0 - ADAM_B2) * (g * g)
    m_hat = m2 / (1.0 - ADAM_B1**ADAM_STEP)
    v_hat = v2 / (1.0 - ADAM_B2**ADAM_STEP)
    delta = -ADAM_LR * (m_hat / (jnp.sqrt(v_hat) + ADAM_EPS) + ADAM_WD * w)
    return delta, m2, v2


def _adamw_big(name, land, w, m, v):
    rows, cols = w.shape
    tr = _tile(rows, 256)
    n_slab = land.shape[0]

    def body(l_ref, w_ref, m_ref, v_ref, g_o, d_o, m_o, v_o):
        g = l_ref[0].astype(F32)
        for d in range(1, n_slab):
            g = g + l_ref[d].astype(F32)
        delta, m2, v2 = _adamw_math(g, w_ref[...], m_ref[...], v_ref[...])
        g_o[...] = g
        d_o[...] = delta
        m_o[...] = m2
        v_o[...] = v2

    blk = pl.BlockSpec((tr, cols), lambda i: (i, 0))
    return pl.pallas_call(
        body,
        name=name,
        grid=(rows // tr,),
        in_specs=[pl.BlockSpec((n_slab, tr, cols), lambda i: (0, i, 0)), blk, blk, blk],
        out_specs=[blk] * 4,
        out_shape=[SDS((rows, cols), F32)] * 4,
        compiler_params=_params(("parallel",)),
    )(land, w, m, v)


def _adamw_small(g, w, m, v):
    def body(g_ref, w_ref, m_ref, v_ref, d_o, m_o, v_o):
        delta, m2, v2 = _adamw_math(g_ref[...], w_ref[...], m_ref[...], v_ref[...])
        d_o[...] = delta
        m_o[...] = m2
        v_o[...] = v2

    vmem = pl.BlockSpec(memory_space=pltpu.VMEM)
    return pl.pallas_call(
        body,
        name="adamw_small",
        in_specs=[vmem] * 4,
        out_specs=[vmem] * 3,
        out_shape=[SDS(g.shape, F32)] * 3,
    )(g, w, m, v)


def _pack(pieces, width):
    flat = jnp.concatenate([p.reshape(-1) for p in pieces])
    rows = -(-flat.shape[0] // (8 * width)) * 8
    flat = jnp.pad(flat, (0, rows * width - flat.shape[0]))
    return flat.reshape(rows, width)


def _unpack(packed, shapes):
    flat = packed.reshape(-1)
    out, off = [], 0
    for shp in shapes:
        n = 1
        for d in shp:
            n *= d
        out.append(flat[off : off + n].reshape(shp))
        off += n
    return out


def kernel(x, p, g_mix, w_in, conv_a_w, w_out_a, b_glu, conf_dw_w, conf_dw_b, conf_ln_g, conf_ln_b, w_pw_b, b_pw_b, w_o, g_ffn, w_gate, w_up, w_down, g_ple, w_ple_gate, w_ple_proj, g_final, loss_target, m_g_mix, m_w_in, m_conv_a_w, m_w_out_a, m_b_glu, m_conf_dw_w, m_conf_dw_b, m_conf_ln_g, m_conf_ln_b, m_w_pw_b, m_b_pw_b, m_w_o, m_g_ffn, m_w_gate, m_w_up, m_w_down, m_g_ple, m_w_ple_gate, m_w_ple_proj, m_g_final, v_g_mix, v_w_in, v_conv_a_w, v_w_out_a, v_b_glu, v_conf_dw_w, v_conf_dw_b, v_conf_ln_g, v_conf_ln_b, v_w_pw_b, v_b_pw_b, v_w_o, v_g_ffn, v_w_gate, v_w_up, v_w_down, v_g_ple, v_w_ple_gate, v_w_ple_proj, v_g_final):
    s, d = x.shape[1], x.shape[2]
    c = conf_ln_g.shape[-1]
    pdim = w_ple_proj.shape[1]
    fs = w_gate.shape[-1]
    nin = NDEV * w_in.shape[-1]
    assert d == 2 * c and nin == 5 * c + 2 * d, (d, c, nin)
    x2, p2, tgt = x[0], p[0, 0], loss_target[0]
    gfin = g_final.reshape(1, d)

    kpa, kpb = 8, HALO_B
    wa_sh = jnp.pad(conv_a_w[0], ((0, kpa - CONV_A_K), (0, 0)))
    wd_sh = jnp.pad(conf_dw_w[0], ((0, kpb - CONF_K), (0, 0)))
    kind_of = dict(w_in="col", w_out_a="col", w_pw_b="col", w_ple_proj="col", w_o="row", w_ple_gate="row",
                   w_gate="blk", w_up="blk", w_down="blk")
    weight = dict(w_in=w_in, w_out_a=w_out_a, w_pw_b=w_pw_b, w_ple_proj=w_ple_proj, w_o=w_o, w_ple_gate=w_ple_gate,
                  w_gate=w_gate, w_up=w_up, w_down=w_down)
    shard_of = {nm: tuple(w.shape[1:]) for nm, w in weight.items()}
    bf16_shard = lambda nm: weight[nm][0].astype(BF16)
    kinds_ = lambda grp: [kind_of[nm] for nm in grp]
    shapes_ = lambda grp: [shard_of[nm] for nm in grp]
    first_stage = lambda grp: _ag1([bf16_shard(nm) for nm in grp], kinds_(grp))
    second_stage = lambda grp, parts: _ag2(parts, kinds_(grp), shapes_(grp))
    grp_1 = ["w_out_a", "w_pw_b"]
    grp_2 = ["w_o", "w_gate"]
    grp_3 = ["w_up"]
    grp_4 = ["w_down"]
    grp_5 = ["w_ple_gate", "w_ple_proj"]

    tm = _tile(s, 1024)
    tn = _tile(d, 1024)
    assert (5 * c) % tn == 0 and d % tn == 0 and c % tn == 0
    ga_blk, gb_blk = (5 * c) // tn, (5 * c + d) // tn
    ij = lambda i, j, k: (i, j)
    row_i = lambda i, j, k: (i, 0)

    n1 = _rms_fwd("rms1", x2, g_mix)
    proj, win, got = _proj_gather(
        n1, bf16_shard("w_in"),
        _join(_ag_direct([wa_sh, wd_sh], ["col", "col"]), first_stage(grp_1)), first_stage(grp_2),
    )
    (wa, wd), part_12 = got[:2], got[2:]
    grp_12 = grp_1 + grp_2
    ya_in = _mix_a_fwd(proj, wa, s, c)
    (v_act, u_act, cv), got = _mix_b_fwd(
        proj, b_glu, wd, conf_dw_b, conf_ln_g, conf_ln_b, s, c,
        comm=_join(second_stage(grp_12, part_12), first_stage(grp_3)),
    )
    (wouta, wpw, wo, wg), part_3 = got[: len(grp_12)], got[len(grp_12) :]

    def ep_merge(accs, ex, os_):
        sa = _sigmoid(ex[0][...].astype(F32))
        sb = _sigmoid(ex[1][...].astype(F32))
        ya = accs[0]
        yb = accs[1] + ex[2][...]
        os_[0][...] = (sa * ya + sb * yb).astype(BF16)
        os_[1][...] = ya.astype(BF16)
        os_[2][...] = yb.astype(BF16)

    gate_a_spec = pl.BlockSpec((tm, tn), lambda i, j, k: (i, ga_blk + j))
    gate_b_spec = pl.BlockSpec((tm, tn), lambda i, j, k: (i, gb_blk + j))
    out_sd = (SDS((s, d), BF16), pl.BlockSpec((tm, tn), ij))
    (m_act, ya, yb), got = _fmm(
        "merge", (s // tm, d // tn, 1),
        [(ya_in, pl.BlockSpec((tm, c), row_i)), (wouta, pl.BlockSpec((c, tn), lambda i, j, k: (0, j))),
         (v_act, pl.BlockSpec((tm, c), row_i)), (wpw, pl.BlockSpec((c, tn), lambda i, j, k: (0, j)))],
        [(0, 1, NN, 0, None), (2, 3, NN, 1, None)], [(tm, tn), (tm, tn)],
        [(proj, gate_a_spec), (proj, gate_b_spec), (b_pw_b, pl.BlockSpec((1, tn), lambda i, j, k: (0, j)))],
        [out_sd, out_sd, out_sd], ep_merge, csplit=EPILOGUE_CHUNK,
        comm=_join(second_stage(grp_3, part_3), first_stage(grp_4)),
    )
    (wu,), part_4 = got[: len(grp_3)], got[len(grp_3) :]

    def ep_residual(accs, ex, os_):
        os_[0][...] = accs[0] + ex[0][...]

    (h1,), got = _fmm(
        "w_o", (s // tm, d // tn, 1),
        [(m_act, pl.BlockSpec((tm, d), row_i)), (wo, pl.BlockSpec((d, tn), lambda i, j, k: (0, j)))],
        [(0, 1, NN, 0, None)], [(tm, tn)], [(x2, pl.BlockSpec((tm, tn), ij))],
        [(SDS((s, d), F32), pl.BlockSpec((tm, tn), ij))], ep_residual, csplit=EPILOGUE_CHUNK,
        comm=_join(second_stage(grp_4, part_4), first_stage(grp_5)),
    )
    (wdn,), part_5 = got[: len(grp_4)], got[len(grp_4) :]
    n2 = _rms_fwd("rms2", h1, g_ffn)

    def ep_gateup(accs, ex, os_):
        g, u = accs
        os_[0][...] = g.astype(BF16)
        os_[1][...] = u.astype(BF16)
        os_[2][...] = (g * _sigmoid(g) * u).astype(BF16)

    ff_sd = (SDS((NDEV, s, fs), BF16), pl.BlockSpec((None, tm, fs), lambda i, j, k: (j, i, 0)))
    w_col_blk = pl.BlockSpec((None, d, fs), lambda i, j, k: (j, 0, 0))
    (g_act, u_ff, f_act), (wpg, wpp) = _fmm(
        "gate_up", (s // tm, NDEV, 1),
        [(n2, pl.BlockSpec((tm, d), row_i)), (wg, w_col_blk), (wu, w_col_blk)],
        [(0, 1, NN, 0, None), (0, 2, NN, 1, None)], [(tm, fs), (tm, fs)], [],
        [ff_sd, ff_sd, ff_sd], ep_gateup, csplit=EPILOGUE_CHUNK,
        comm=second_stage(grp_5, part_5),
    )
    pair = 2
    (h2,), _ = _fmm(
        "down", (s // tm, d // tn, NDEV // pair),
        [(f_act, pl.BlockSpec((pair, tm, fs), lambda i, j, k: (k, i, 0))),
         (wdn, pl.BlockSpec((pair, fs, tn), lambda i, j, k: (k, 0, j)))],
        [(0, 1, NN, 0, None, pair)], [(tm, tn)], [(h1, pl.BlockSpec((tm, tn), ij))],
        [(SDS((s, d), F32), pl.BlockSpec((tm, tn), ij))], ep_residual,
    )
    n3 = _rms_fwd("rms3", h2, g_ple)

    tr = _tile(s, 256)
    n_r = s // tr
    rows = lambda i, j, k: (i, 0)
    whole = lambda i, j, k: (0, 0)
    part_spec = lambda nrow: pl.BlockSpec((None, nrow, d), lambda i, j, k: (i, 0, 0))

    def ep_ple(accs, ex, os_):
        h2_, t_, gf = ex[0][...], ex[1][...], ex[2][...]
        ple = accs[0]
        s3 = _sigmoid(accs[1])
        h3 = h2_ + s3 * ple
        r = lax.rsqrt(jnp.mean(h3 * h3, axis=-1, keepdims=True) + EPS)
        hn = h3 * r
        e = hn * gf - t_
        loss = 0.5 * jnp.sum(jnp.mean(e * e, axis=-1, keepdims=True), axis=0, keepdims=True)
        dy = e * (1.0 / d)
        dn = dy * gf
        dh3 = r * (dn - hn * jnp.mean(dn * hn, axis=-1, keepdims=True))
        os_[0][...] = dh3
        os_[1][...] = (dh3 * s3).astype(BF16)
        os_[2][...] = (dh3 * ple * s3 * (1.0 - s3)).astype(BF16)
        os_[3][0:1, :] = jnp.sum(dy * hn, axis=0, keepdims=True)
        os_[3][1:2, :] = jnp.broadcast_to(loss, (1, d))

    (dh3, d_ple, d_g3, part_fin), _ = _fmm(
        "ple_loss", (n_r, 1, 1),
        [(p2, pl.BlockSpec((tr, pdim), rows)), (wpp, pl.BlockSpec((pdim, d), whole)),
         (n3, pl.BlockSpec((tr, d), rows)), (wpg, pl.BlockSpec((d, d), whole))],
        [(0, 1, NN, 0, None), (2, 3, NN, 1, None)], [(tr, d), (tr, d)],
        [(h2, pl.BlockSpec((tr, d), rows)), (tgt, pl.BlockSpec((tr, d), rows)), (gfin, pl.BlockSpec((1, d), whole))],
        [(SDS((s, d), F32), pl.BlockSpec((tr, d), rows)), (SDS((s, d), BF16), pl.BlockSpec((tr, d), rows)),
         (SDS((s, d), BF16), pl.BlockSpec((tr, d), rows)), (SDS((n_r, 2, d), F32), part_spec(2))],
        ep_ple,
    )

    g_wpp = _mm_tn("d_w_ple_proj", p2, d_ple)
    g_wpg = _mm_tn("d_w_ple_gate", n3, d_g3)

    def ep_norm_bwd(accs, ex, os_):
        dh, dg = _rms_bwd(accs[0], ex[0][...], ex[2][...])
        dh = ex[1][...] + dh
        os_[0][...] = dh
        os_[1][...] = dh.astype(BF16)
        os_[2][...] = dg

    norm_outs = lambda t: [
        (SDS((s, d), F32), pl.BlockSpec((t, d), rows)), (SDS((s, d), BF16), pl.BlockSpec((t, d), rows)),
        (SDS((s // t, 1, d), F32), part_spec(1)),
    ]
    def exchange1(names, wholes):
        return _rs1(wholes, kinds_(names), shapes_(names))

    def pair_sums(names, wholes, got):
        return [_pair_sum("pair_sum_" + nm, wholes[t], kind_of[nm], got[t]) for t, nm in enumerate(names)]

    lands = {}
    grp1 = ["w_ple_proj", "w_ple_gate"]
    (dh2, dh2b, part_ple), got = _fmm(
        "d_n3", (n_r, 1, 1),
        [(d_g3, pl.BlockSpec((tr, d), rows)), (wpg, pl.BlockSpec((d, d), whole))],
        [(0, 1, NT, 0, None)], [(tr, d)],
        [(h2, pl.BlockSpec((tr, d), rows)), (dh3, pl.BlockSpec((tr, d), rows)), (g_ple, pl.BlockSpec((1, d), whole))],
        norm_outs(tr), ep_norm_bwd,
        comm=exchange1(grp1, [g_wpp, g_wpg]),
    )
    sums1 = pair_sums(grp1, [g_wpp, g_wpg], got)

    def ep_ddown(accs, ex, os_):
        g = ex[0][...].astype(F32)
        u = ex[1][...].astype(F32)
        sg = _sigmoid(g)
        df = accs[0]
        os_[0][...] = (df * u * sg * (1.0 + g * (1.0 - sg))).astype(BF16)
        os_[1][...] = (df * g * sg).astype(BF16)

    ff_in = pl.BlockSpec((None, tm, fs), lambda i, j, k: (j, i, 0))
    (d_g, d_u), got = _fmm(
        "d_down", (s // tm, NDEV, 1),
        [(dh2b, pl.BlockSpec((tm, d), row_i)), (wdn, pl.BlockSpec((None, fs, d), lambda i, j, k: (j, 0, 0)))],
        [(0, 1, NT, 0, None)], [(tm, fs)], [(g_act, ff_in), (u_ff, ff_in)],
        [ff_sd, ff_sd], ep_ddown, csplit=EPILOGUE_CHUNK,
        comm=_rs2(sums1),
    )
    lands.update(zip(grp1, got))
    tk = _tile(s, 1024)
    (g_wdn,), _ = _fmm(
        "d_w_down", (NDEV, 1, s // tk),
        [(f_act, pl.BlockSpec((None, tk, fs), lambda i, j, k: (i, k, 0))), (dh2b, pl.BlockSpec((tk, d), lambda i, j, k: (k, 0)))],
        [(0, 1, TN, 0, None)], [(fs, d)], [],
        [(SDS((NDEV, fs, d), BF16), pl.BlockSpec((None, fs, d), lambda i, j, k: (i, 0, 0)))], _ep_bf16,
    )

    def ep_two_bf16(accs, ex, os_):
        os_[0][...] = accs[0].astype(BF16)
        os_[1][...] = accs[1].astype(BF16)

    ff_k = pl.BlockSpec((None, tk, fs), lambda i, j, k: (i, k, 0))
    wcol_sd = (SDS((NDEV, d, fs), BF16), pl.BlockSpec((None, d, fs), lambda i, j, k: (i, 0, 0)))
    grp2 = ["w_down"]
    (g_wg, g_wu), got = _fmm(
        "d_w_gate_up", (NDEV, 1, s // tk),
        [(n2, pl.BlockSpec((tk, d), lambda i, j, k: (k, 0))), (d_g, ff_k), (d_u, ff_k)],
        [(0, 1, TN, 0, None), (0, 2, TN, 1, None)], [(d, fs), (d, fs)], [],
        [wcol_sd, wcol_sd], ep_two_bf16,
        comm=exchange1(grp2, [g_wdn]),
    )
    sums2 = pair_sums(grp2, [g_wdn], got)
    grp3 = ["w_gate", "w_up"]
    th = _tile(s // 2, 1024)
    ff_a = pl.BlockSpec((None, th, fs), lambda i, j, k: (k, i, 0))
    w_k = pl.BlockSpec((None, d, fs), lambda i, j, k: (k, 0, 0))
    (d_n2,), got = _fmm(
        "d_n2", (s // th, 1, NDEV),
        [(d_g, ff_a), (wg, w_k), (d_u, ff_a), (wu, w_k)],
        [(0, 1, NT, 0, None), (2, 3, NT, 0, None)], [(th, d)], [],
        [(SDS((s, d), BF16), pl.BlockSpec((th, d), rows))], _ep_bf16,
        comm=_join(_rs2(sums2), exchange1(grp3, [g_wg, g_wu])),
    )
    lands.update(zip(grp2, got[:1]))
    sums3 = pair_sums(grp3, [g_wg, g_wu], got[1:])
    dh1, dh1b, part_ffn = _norm_bwd("d_h1", d_n2, h1, dh2, g_ffn, True)
    g_wo = _mm_tn("d_w_o", m_act, dh1b)

    def ep_dm(accs, ex, os_):
        ya_, yb_ = ex[0][...].astype(F32), ex[1][...].astype(F32)
        sa = _sigmoid(ex[2][...].astype(F32))
        sb = _sigmoid(ex[3][...].astype(F32))
        dm = accs[0]
        d_yb = dm * sb
        os_[0][...] = (dm * sa).astype(BF16)
        os_[1][...] = d_yb.astype(BF16)
        os_[2][...] = (dm * ya_ * sa * (1.0 - sa)).astype(BF16)
        os_[3][...] = (dm * yb_ * sb * (1.0 - sb)).astype(BF16)
        os_[4][...] = jnp.sum(d_yb, axis=0, keepdims=True)

    tile_ij = pl.BlockSpec((tm, tn), ij)
    grp4 = ["w_o"]
    (d_ya, d_yb, d_ga, d_gb, part_bpw), got = _fmm(
        "d_merge", (s // tm, d // tn, 1),
        [(dh1b, pl.BlockSpec((tm, d), row_i)), (wo, pl.BlockSpec((tn, d), lambda i, j, k: (j, 0)))],
        [(0, 1, NT, 0, None)], [(tm, tn)],
        [(ya, tile_ij), (yb, tile_ij), (proj, gate_a_spec), (proj, gate_b_spec)],
        [out_sd, out_sd, out_sd, out_sd,
         (SDS((s // tm, 1, d), F32), pl.BlockSpec((None, 1, tn), lambda i, j, k: (i, 0, j)))],
        ep_dm, csplit=EPILOGUE_CHUNK,
        comm=exchange1(grp4, [g_wo]),
    )
    sums4 = pair_sums(grp4, [g_wo], got)
    g_wouta = _mm_tn("d_w_out_a", ya_in, d_ya)
    g_wpw = _mm_tn("d_w_pw_b", v_act, d_yb)
    grp5 = ["w_out_a", "w_pw_b"]
    d_ya_in, got = _mm_nt("d_ya_in", d_ya, wouta, comm=exchange1(grp5, [g_wouta, g_wpw]))
    sums5 = pair_sums(grp5, [g_wouta, g_wpw], got)
    d_v, _ = _mm_nt("d_v", d_yb, wpw)
    d_cv, part_ln = _mix_b_bwd1(d_v, cv, conf_ln_g, conf_ln_b, s, c)
    (d_b, part_wd, part_bglu), got = _mix_b_bwd2(d_cv, u_act, proj, b_glu, wd, s, c, comm=_rs2(sums3))
    lands.update(zip(grp3, got))
    d_a, part_wa = _mix_a_bwd(d_ya_in, proj, wa, s, c)

    nb = nin // c
    gblk = d // c
    lo = [0, 3, 5, 5 + gblk]
    hi = [3, 5, 5 + gblk, 5 + 2 * gblk]
    pieces = [d_a, d_b, d_ga, d_gb]

    def active(q, ax):
        return lambda ids: jnp.logical_and(ids[ax] >= lo[q], ids[ax] < hi[q])

    def piece_spec(q, rows_, ax, row0=0):
        def index(i, j, k):
            ids = (i, j, k)
            col = jnp.clip(ids[ax] - lo[q], 0, hi[q] - lo[q] - 1)
            row = i + row0 if ax == 2 else jnp.where(active(q, ax)(ids), k, 0)
            return (row, col)

        return pl.BlockSpec((rows_, c), index)

    tkw = _tile(s, 1024)
    (g_win,), got = _fmm(
        "d_w_in", (1, nb, s // tkw),
        [(n1, pl.BlockSpec((tkw, d), lambda i, j, k: (k, 0)))]
        + [(pieces[q], piece_spec(q, tkw, 1)) for q in range(4)],
        [(0, 1 + q, TN, 0, active(q, 1)) for q in range(4)], [(d, c)], [],
        [(SDS((d, nin), BF16), pl.BlockSpec((d, c), lambda i, j, k: (0, j)))], _ep_bf16,
        comm=_rs2(sums4 + sums5),
    )
    lands.update(zip(grp4 + grp5, got))

    grp6 = ["w_in"]
    n_half = (s // th) // 2

    def d_n1_rows(name, row0, n_tiles, comm, into):
        return _fmm(
            name, (n_tiles, 1, nb),
            [(pieces[q], piece_spec(q, th, 2, row0)) for q in range(4)]
            + [(win, pl.BlockSpec((d, c), lambda i, j, k: (0, k)))],
            [(q, 4, NT, 0, active(q, 2)) for q in range(4)], [(th, d)], [],
            [(SDS((s, d), BF16), pl.BlockSpec((th, d), lambda i, j, k: (i + row0, 0)))], _ep_bf16,
            comm=comm, into=into,
        )

    (d_n1,), got = d_n1_rows("d_n1_a", 0, n_half, exchange1(grp6, [g_win]), None)
    (d_n1,), got = d_n1_rows("d_n1_b", n_half, s // th - n_half, _rs2(pair_sums(grp6, [g_win], got)), d_n1)
    lands.update(zip(grp6, got))
    dx, part_mix = _norm_bwd("d_x", d_n1, x2, dh1, g_mix, False)

    small_parts = [
        jnp.sum(part_mix, axis=0),
        jnp.sum(part_bglu, axis=0),
        jnp.sum(part_ln[:, 2], axis=0),
        jnp.sum(part_ln[:, 0], axis=0),
        jnp.sum(part_ln[:, 1], axis=0),
        jnp.sum(part_bpw, axis=0),
        jnp.sum(part_ffn, axis=0),
        jnp.sum(part_ple, axis=0),
        jnp.sum(part_fin[:, 0], axis=0),
        jnp.sum(part_wa, axis=0),
        jnp.sum(part_wd, axis=0),
        jnp.broadcast_to(jnp.sum(part_fin[:, 1, 0]), (c,)),
    ]
    small_shapes = [(1, d), (1, 2 * c), (1, c), (1, c), (1, c), (1, d), (1, d), (1, d), (d,), (kpa, c), (kpb, c), (c,)]
    total = _all_reduce_small(_pack(small_parts, c))
    (gr_g_mix, gr_b_glu, gr_dw_b, gr_ln_g, gr_ln_b, gr_b_pw, gr_g_ffn, gr_g_ple, gr_g_final, gr_wa, gr_wd, loss_row) = _unpack(total, small_shapes)
    loss = loss_row[0]
    my = _dev_index(_place())
    csh = conv_a_w.shape[-1]
    gr_conv_a = lax.dynamic_slice_in_dim(gr_wa[:CONV_A_K], my * csh, csh, axis=1)[None]
    gr_conf_dw = lax.dynamic_slice_in_dim(gr_wd[:CONF_K], my * csh, csh, axis=1)[None]

    big_m = dict(w_in=m_w_in, w_out_a=m_w_out_a, w_pw_b=m_w_pw_b, w_ple_proj=m_w_ple_proj, w_o=m_w_o,
                 w_ple_gate=m_w_ple_gate, w_gate=m_w_gate, w_up=m_w_up, w_down=m_w_down)
    big_v = dict(w_in=v_w_in, w_out_a=v_w_out_a, w_pw_b=v_w_pw_b, w_ple_proj=v_w_ple_proj, w_o=v_w_o,
                 w_ple_gate=v_w_ple_gate, w_gate=v_w_gate, w_up=v_w_up, w_down=v_w_down)
    big_out = {}
    for nm in weight:
        res = _adamw_big("adamw_" + nm, lands[nm], weight[nm][0], big_m[nm][0], big_v[nm][0])
        big_out[nm] = [r[None] for r in res]

    small_names = ["g_mix", "conv_a_w", "b_glu", "conf_dw_w", "conf_dw_b", "conf_ln_g", "conf_ln_b", "b_pw_b", "g_ffn", "g_ple", "g_final"]
    small_g = [gr_g_mix, gr_conv_a, gr_b_glu, gr_conf_dw, gr_dw_b, gr_ln_g, gr_ln_b, gr_b_pw, gr_g_ffn, gr_g_ple, gr_g_final]
    small_w = [g_mix, conv_a_w, b_glu, conf_dw_w, conf_dw_b, conf_ln_g, conf_ln_b, b_pw_b, g_ffn, g_ple, g_final]
    small_m = [m_g_mix, m_conv_a_w, m_b_glu, m_conf_dw_w, m_conf_dw_b, m_conf_ln_g, m_conf_ln_b, m_b_pw_b, m_g_ffn, m_g_ple, m_g_final]
    small_v = [v_g_mix, v_conv_a_w, v_b_glu, v_conf_dw_w, v_conf_dw_b, v_conf_ln_g, v_conf_ln_b, v_b_pw_b, v_g_ffn, v_g_ple, v_g_final]
    shp = [tuple(w.shape) for w in small_w]
    small_g = [g.reshape(sh) for g, sh in zip(small_g, shp)]
    sd, sm, sv = _adamw_small(_pack(small_g, 128), _pack(small_w, 128), _pack(small_m, 128), _pack(small_v, 128))
    small_out = {}
    for nm, g, dl, mm, vv in zip(small_names, small_g, _unpack(sd, shp), _unpack(sm, shp), _unpack(sv, shp)):
        small_out[nm] = [g, dl, mm, vv]

    order = ["g_mix", "w_in", "conv_a_w", "w_out_a", "b_glu", "conf_dw_w", "conf_dw_b", "conf_ln_g", "conf_ln_b", "w_pw_b", "b_pw_b", "w_o", "g_ffn", "w_gate", "w_up", "w_down", "g_ple", "w_ple_gate", "w_ple_proj", "g_final"]
    allo = {**big_out, **small_out}
    outs = [loss, dx[None]]
    for q in range(4):
        outs += [allo[nm][q] for nm in order]
    return tuple(outs)
```

```python
import jax
import jax.numpy as jnp
from jax import lax
from jax.experimental import pallas as pl
from jax.experimental.pallas import tpu as pltpu

F32, BF16 = jnp.float32, jnp.bfloat16
EPS, LN_EPS = 1e-6, 1e-5
ADAM_LR, ADAM_B1, ADAM_B2, ADAM_EPS, ADAM_WD, ADAM_STEP = 0.001, 0.9, 0.999, 1e-08, 0.01, 10
CONV_A_K, CONF_K = 3, 31
NDEV = 8
NN = (((1,), (0,)), ((), ()))
NT = (((1,), (1,)), ((), ()))
TN = (((0,), (0,)), ((), ()))
V7X_VMEM_LIMIT_BYTES = 56 * 1024 * 1024
MESH = pl.DeviceIdType.MESH
SDS = jax.ShapeDtypeStruct
HALO_A, HALO_B = 16, 32
EPILOGUE_CHUNK = 256
CONV_ROWS = 32
CONF_ROWS = 16


def _tile(n, pref):
    t = min(n, pref)
    while n % t:
        t -= 8
    return t


def _sigmoid(x):
    return jax.nn.sigmoid(x)


def _params(sem=None):
    return pltpu.CompilerParams(vmem_limit_bytes=V7X_VMEM_LIMIT_BYTES, dimension_semantics=sem)


def _edge(grid, last):
    cond = None
    for ax, n in enumerate(grid):
        here = pl.program_id(ax) == (n - 1 if last else 0)
        cond = here if cond is None else jnp.logical_and(cond, here)
    return cond


def _join(*comms):
    ins, outs, alias, sems, spans = [], [], {}, [], []
    for cm in comms:
        spans.append((len(ins), len(outs), len(sems)))
        for i, o in cm["alias"].items():
            alias[len(ins) + i] = len(outs) + o
        ins += cm["ins"]
        outs += cm["outs"]
        sems += cm["sems"]

    def run(which):
        def f(i_refs, o_refs, s_refs):
            for cm, (a, b, c_) in zip(comms, spans):
                cm[which](
                    i_refs[a : a + len(cm["ins"])], o_refs[b : b + len(cm["outs"])], s_refs[c_ : c_ + len(cm["sems"])]
                )

        return f

    return dict(ins=ins, outs=outs, alias=alias, sems=sems, start=run("start"), finish=run("finish"))


def _call(body, name, grid, in_specs, args, out_specs, out_shape, scratch=(), sem=None, comm=None, alias=None):
    n_in, n_out, n_s = len(args), len(out_shape), len(scratch)
    alias = dict(alias or {})
    if comm is None:
        res = pl.pallas_call(
            body, name=name, grid=grid, in_specs=list(in_specs), out_specs=list(out_specs), out_shape=list(out_shape),
            scratch_shapes=list(scratch), input_output_aliases=alias, compiler_params=_params(sem),
        )(*args)
        return list(res), []
    n_ci, n_co = len(comm["ins"]), len(comm["outs"])

    def wrapped(*refs):
        ins = refs[:n_in]
        ci = refs[n_in : n_in + n_ci]
        o0 = n_in + n_ci
        outs = refs[o0 : o0 + n_out]
        co = refs[o0 + n_out : o0 + n_out + n_co]
        s0 = o0 + n_out + n_co
        sc = refs[s0 : s0 + n_s]
        cs = refs[s0 + n_s :]
        pl.when(_edge(grid, False))(lambda: comm["start"](ci, co, cs))
        body(*ins, *outs, *sc)
        pl.when(_edge(grid, True))(lambda: comm["finish"](ci, co, cs))

    hbm = pl.BlockSpec(memory_space=pl.ANY)
    res = pl.pallas_call(
        wrapped,
        name=name,
        grid=grid,
        in_specs=list(in_specs) + [hbm] * n_ci,
        out_specs=list(out_specs) + [hbm] * n_co,
        out_shape=list(out_shape) + list(comm["outs"]),
        scratch_shapes=list(scratch) + list(comm["sems"]),
        input_output_aliases={**alias, **{n_in + i: n_out + o for i, o in comm["alias"].items()}},
        compiler_params=_params(("arbitrary",) * len(grid)),
    )(*args, *comm["ins"])
    return list(res[:n_out]), list(res[n_out:])


def _col_chunks(n, pref):
    out, c0 = [], 0
    while c0 < n:
        w = min(pref, n - c0)
        out.append((c0, w))
        c0 += w
    return out


def _fmm(name, grid, operands, terms, acc_shapes, extras, outs, epilogue, comm=None, csplit=None, into=None):
    n_p, n_e, n_o, n_a = len(operands), len(extras), len(outs), len(acc_shapes)
    nk = grid[-1]
    kax = len(grid) - 1
    simple = nk == 1 and all(t[4] is None for t in terms)
    alias = None
    if into is not None:
        extras = list(extras) + [(into, pl.BlockSpec(memory_space=pl.ANY))]
        alias = {n_p + n_e: 0}
        n_e += 1
    if csplit is not None:
        assert simple and into is None and all(t[2] in (NN, NT) and (len(t) <= 5 or not t[5]) for t in terms)
        tn_ = acc_shapes[0][1]
        chunks = _col_chunks(tn_, csplit)

    def dot(a, b, dims):
        if a.dtype != BF16:
            a = a.astype(BF16)
        if b.dtype != BF16:
            b = b.astype(BF16)
        return lax.dot_general(a, b, dims, preferred_element_type=F32)

    def value(refs, term):
        slabs = term[5] if len(term) > 5 else 0
        if not slabs:
            return dot(refs[term[0]][...], refs[term[1]][...], term[2])
        tot = None
        for sl in range(slabs):
            d = dot(refs[term[0]][sl], refs[term[1]][sl], term[2])
            tot = d if tot is None else tot + d
        return tot

    def always(refs):
        parts = [None] * n_a
        for term in terms:
            if term[4] is None:
                d = value(refs, term)
                parts[term[3]] = d if parts[term[3]] is None else parts[term[3]] + d
        return parts

    def chunked(refs, ex, os_):
        cols = lambda ref, c0, w: ref.at[:, pl.ds(c0, w)] if ref.shape[-1] == tn_ else ref
        for c0, w in chunks:
            parts = [None] * n_a
            for term in terms:
                b_ref = refs[term[1]]
                b = b_ref[:, pl.ds(c0, w)] if term[2] == NN else b_ref[pl.ds(c0, w), :]
                d = dot(refs[term[0]][...], b, term[2])
                parts[term[3]] = d if parts[term[3]] is None else parts[term[3]] + d
            epilogue(parts, [cols(e, c0, w) for e in ex], [cols(o, c0, w) for o in os_])

    def body(*refs):
        ex = refs[n_p : n_p + n_e]
        os_ = refs[n_p + n_e : n_p + n_e + n_o]
        accs = refs[n_p + n_e + n_o :]
        if simple and csplit is not None:
            chunked(refs, ex, os_)
            return
        if simple:
            epilogue(always(refs), ex, os_)
            return
        ids = [pl.program_id(ax) for ax in range(len(grid))]
        k = ids[kax]

        @pl.when(k == 0)
        def _():
            for acc in accs:
                acc[...] = jnp.zeros(acc.shape, F32)

        for ai, part in enumerate(always(refs)):
            if part is not None:
                accs[ai][...] += part
        for term in terms:
            if term[4] is not None:

                def add(term=term):
                    accs[term[3]][...] += value(refs, term)

                pl.when(term[4](ids))(add)

        @pl.when(k == nk - 1)
        def _():
            epilogue([acc[...] for acc in accs], ex, os_)

    return _call(
        body,
        name,
        grid,
        [o[1] for o in operands] + [e[1] for e in extras],
        [o[0] for o in operands] + [e[0] for e in extras],
        [o[1] for o in outs],
        [o[0] for o in outs],
        scratch=[] if simple else [pltpu.VMEM(s, F32) for s in acc_shapes],
        sem=("parallel",) * kax + ("arbitrary",),
        comm=comm,
        alias=alias,
    )


def _rms_bwd(dn_raw, h, g):
    r = lax.rsqrt(jnp.mean(h * h, axis=-1, keepdims=True) + EPS)
    hn = h * r
    dg = jnp.sum(dn_raw * hn, axis=0, keepdims=True)
    dn = dn_raw * g
    dh = r * (dn - hn * jnp.mean(dn * hn, axis=-1, keepdims=True))
    return dh, dg


def _rms_fwd(name, h, g):
    s, d = h.shape
    ts = _tile(s, 512)

    def body(h_ref, g_ref, o_ref):
        x = h_ref[...]
        r = lax.rsqrt(jnp.mean(x * x, axis=-1, keepdims=True) + EPS)
        o_ref[...] = (x * r * g_ref[...]).astype(BF16)

    return pl.pallas_call(
        body,
        name=name,
        grid=(s // ts,),
        in_specs=[pl.BlockSpec((ts, d), lambda i: (i, 0)), pl.BlockSpec((1, d), lambda i: (0, 0))],
        out_specs=pl.BlockSpec((ts, d), lambda i: (i, 0)),
        out_shape=SDS((s, d), BF16),
        compiler_params=_params(("parallel",)),
    )(h, g)


def _norm_bwd(name, dn, h, dres, g, want_bf16):
    s, d = h.shape
    ts = _tile(s, 512)

    def body(dn_r, h_r, dres_r, g_r, *outs):
        dh, dg = _rms_bwd(dn_r[...].astype(F32), h_r[...], g_r[...])
        dh = dres_r[...] + dh
        outs[0][...] = dh
        if want_bf16:
            outs[1][...] = dh.astype(BF16)
        outs[-1][...] = dg

    blk = pl.BlockSpec((ts, d), lambda i: (i, 0))
    part = pl.BlockSpec((None, 1, d), lambda i: (i, 0, 0))
    return pl.pallas_call(
        body,
        name=name,
        grid=(s // ts,),
        in_specs=[blk, blk, blk, pl.BlockSpec((1, d), lambda i: (0, 0))],
        out_specs=[blk] + ([blk] if want_bf16 else []) + [part],
        out_shape=[SDS((s, d), F32)] + ([SDS((s, d), BF16)] if want_bf16 else []) + [SDS((s // ts, 1, d), F32)],
        compiler_params=_params(("parallel",)),
    )(dn, h, dres, g)


def _prev_halo(ts, hb):
    r = ts // hb
    return lambda i: jnp.maximum(i * r - 1, 0)


def _next_halo(ts, hb, s):
    r = ts // hb
    last = s // hb - 1
    return lambda i: jnp.minimum((i + 1) * r, last)


def _shift_copies(buf, sh):
    n = sh.shape[1]
    for j in range(1, 8):
        sh[j - 1, pl.ds(0, n), :] = buf[pl.ds(j, n), :]


def _tap(buf, sh, r0, off, rows):
    j = off % 8
    start = pl.multiple_of(r0 + (off - j), 8)
    if j == 0:
        return buf[pl.ds(start, rows), :]
    return sh[j - 1, pl.ds(start, rows), :]


def _mix_a_fwd(proj, wa, s, c):
    ts, hb = _tile(s, 256), HALO_A
    prev = _prev_halo(ts, hb)

    def body(ah, ab, ac, hh, hc, w, o, buf):
        i = pl.program_id(0)
        zh = hc[...].astype(F32) * hh[...].astype(F32)
        buf[pl.ds(0, hb), :] = jnp.where(i == 0, 0.0, zh)
        buf[pl.ds(hb, ts), :] = ac[...].astype(F32) * ah[...].astype(F32)
        for r0 in range(0, ts, CONV_ROWS):
            cz = jnp.zeros((CONV_ROWS, c), F32)
            for k in range(CONV_A_K):
                cz = cz + w[k : k + 1, :] * buf[pl.ds(hb + r0 - (CONV_A_K - 1) + k, CONV_ROWS), :]
            o[pl.ds(r0, CONV_ROWS), :] = (ab[pl.ds(r0, CONV_ROWS), :].astype(F32) * cz).astype(BF16)

    main = lambda cb: pl.BlockSpec((ts, c), lambda i: (i, cb))
    halo = lambda cb: pl.BlockSpec((hb, c), lambda i: (prev(i), cb))
    return pl.pallas_call(
        body,
        name="mix_a_fwd",
        grid=(s // ts,),
        in_specs=[main(0), main(1), main(2), halo(0), halo(2), pl.BlockSpec(wa.shape, lambda i: (0, 0))],
        out_specs=pl.BlockSpec((ts, c), lambda i: (i, 0)),
        out_shape=SDS((s, c), BF16),
        scratch_shapes=[pltpu.VMEM((hb + ts, c), F32)],
        compiler_params=_params(("parallel",)),
    )(proj, proj, proj, proj, proj, wa)


def _mix_b_fwd(proj, b_glu, wd, bd, lg, lb, s, c, comm=None):
    ts, hb = _tile(s, 256), HALO_B
    prev = _prev_halo(ts, hb)

    def body(gv, gg, hv, hg, bglu, w, bd_r, lg_r, lb_r, v_o, u_o, cv_o, buf, sh):
        i = pl.program_id(0)
        bv, bg = bglu[:, 0:c], bglu[:, c : 2 * c]
        uh = (hv[...].astype(F32) + bv) * _sigmoid(hg[...].astype(F32) + bg)
        buf[pl.ds(0, hb), :] = jnp.where(i == 0, 0.0, uh)
        u = (gv[...].astype(F32) + bv) * _sigmoid(gg[...].astype(F32) + bg)
        buf[pl.ds(hb, ts), :] = u
        u_o[...] = u.astype(BF16)
        _shift_copies(buf, sh)

        def chunk(ci, carry):
            r0 = pl.multiple_of(ci * CONF_ROWS, CONF_ROWS)
            acc = jnp.zeros((CONF_ROWS, c), F32)
            for k in range(CONF_K):
                acc = acc + w[k : k + 1, :] * _tap(buf, sh, r0, hb - (CONF_K - 1) + k, CONF_ROWS)
            cv_o[pl.ds(r0, CONF_ROWS), :] = acc + bd_r[...]
            return carry

        lax.fori_loop(0, ts // CONF_ROWS, chunk, 0)
        cv = cv_o[...]
        mu = jnp.mean(cv, axis=-1, keepdims=True)
        xc = cv - mu
        rs = lax.rsqrt(jnp.mean(xc * xc, axis=-1, keepdims=True) + LN_EPS)
        ln = xc * rs * lg_r[...] + lb_r[...]
        v_o[...] = (ln * _sigmoid(ln)).astype(BF16)

    main = lambda cb: pl.BlockSpec((ts, c), lambda i: (i, cb))
    halo = lambda cb: pl.BlockSpec((hb, c), lambda i: (prev(i), cb))
    full = lambda a: pl.BlockSpec(a.shape, lambda i: (0, 0))
    out = pl.BlockSpec((ts, c), lambda i: (i, 0))
    return _call(
        body,
        "mix_b_fwd",
        (s // ts,),
        [main(3), main(4), halo(3), halo(4), full(b_glu), full(wd), full(bd), full(lg), full(lb)],
        [proj, proj, proj, proj, b_glu, wd, bd, lg, lb],
        [out, out, out],
        [SDS((s, c), BF16), SDS((s, c), BF16), SDS((s, c), F32)],
        scratch=[pltpu.VMEM((hb + ts, c), F32), pltpu.VMEM((7, hb + ts - 8, c), F32)],
        sem=("parallel",),
        comm=comm,
    )


def _mix_b_bwd1(d_v, cv, lg, lb, s, c):
    ts = _tile(s, 256)

    def body(dv_r, cv_r, lg_r, lb_r, dcv_o, part_o):
        cv_ = cv_r[...]
        mu = jnp.mean(cv_, axis=-1, keepdims=True)
        xc = cv_ - mu
        rs = lax.rsqrt(jnp.mean(xc * xc, axis=-1, keepdims=True) + LN_EPS)
        xh = xc * rs
        ln = xh * lg_r[...] + lb_r[...]
        sg = _sigmoid(ln)
        d_ln = dv_r[...].astype(F32) * (sg * (1.0 + ln * (1.0 - sg)))
        dy = d_ln * lg_r[...]
        d_cv = rs * (dy - jnp.mean(dy, axis=-1, keepdims=True) - xh * jnp.mean(dy * xh, axis=-1, keepdims=True))
        dcv_o[...] = d_cv
        part_o[0:1, :] = jnp.sum(d_ln * xh, axis=0, keepdims=True)
        part_o[1:2, :] = jnp.sum(d_ln, axis=0, keepdims=True)
        part_o[2:3, :] = jnp.sum(d_cv, axis=0, keepdims=True)

    blk = pl.BlockSpec((ts, c), lambda i: (i, 0))
    full = lambda a: pl.BlockSpec(a.shape, lambda i: (0, 0))
    return pl.pallas_call(
        body,
        name="mix_b_bwd_ln",
        grid=(s // ts,),
        in_specs=[blk, blk, full(lg), full(lb)],
        out_specs=[blk, pl.BlockSpec((None, 3, c), lambda i: (i, 0, 0))],
        out_shape=[SDS((s, c), F32), SDS((s // ts, 3, c), F32)],
        compiler_params=_params(("parallel",)),
    )(d_v, cv, lg, lb)


def _mix_b_bwd2(d_cv, u, proj, b_glu, wd, s, c, comm=None):
    ts, hb = _tile(s, 256), HALO_B
    prev, nxt = _prev_halo(ts, hb), _next_halo(ts, hb, s)
    n_t = s // ts
    kp = wd.shape[0]

    def body(dcv, dcv_n, u_m, u_p, gv, gg, bglu, w, d_o, dwd_o, dbglu_o, dbuf, ubuf, dub, dsh, ush, dwacc):
        i = pl.program_id(0)
        dbuf[pl.ds(0, ts), :] = dcv[...]
        dbuf[pl.ds(ts, hb), :] = jnp.where(i == n_t - 1, 0.0, dcv_n[...])
        ubuf[pl.ds(0, hb), :] = jnp.where(i == 0, 0.0, u_p[...].astype(F32))
        ubuf[pl.ds(hb, ts), :] = u_m[...].astype(F32)
        _shift_copies(dbuf, dsh)
        _shift_copies(ubuf, ush)
        dwacc[...] = jnp.zeros(dwacc.shape, F32)

        def chunk(ci, carry):
            r0 = pl.multiple_of(ci * CONF_ROWS, CONF_ROWS)
            acc = jnp.zeros((CONF_ROWS, c), F32)
            dc = dbuf[pl.ds(r0, CONF_ROWS), :]
            for k in range(CONF_K):
                acc = acc + w[k : k + 1, :] * _tap(dbuf, dsh, r0, (CONF_K - 1) - k, CONF_ROWS)
                prod = dc * _tap(ubuf, ush, r0, hb - (CONF_K - 1) + k, CONF_ROWS)
                fold = prod[0:8]
                for a in range(1, CONF_ROWS // 8):
                    fold = fold + prod[8 * a : 8 * a + 8]
                dwacc[pl.ds(8 * k, 8), :] += fold
            dub[pl.ds(r0, CONF_ROWS), :] = acc
            return carry

        lax.fori_loop(0, ts // CONF_ROWS, chunk, 0)
        for k in range(CONF_K):
            dwd_o[k : k + 1, :] = jnp.sum(dwacc[pl.ds(8 * k, 8), :], axis=0, keepdims=True)
        dwd_o[CONF_K:kp, :] = jnp.zeros((kp - CONF_K, c), F32)
        bv, bg = bglu[:, 0:c], bglu[:, c : 2 * c]
        d_u = dub[...]
        sg = _sigmoid(gg[...].astype(F32) + bg)
        d_gv = d_u * sg
        d_gg = d_u * (gv[...].astype(F32) + bv) * sg * (1.0 - sg)
        d_o[:, 0:c] = d_gv.astype(BF16)
        d_o[:, c : 2 * c] = d_gg.astype(BF16)
        dbglu_o[:, 0:c] = jnp.sum(d_gv, axis=0, keepdims=True)
        dbglu_o[:, c : 2 * c] = jnp.sum(d_gg, axis=0, keepdims=True)

    blk = lambda cb: pl.BlockSpec((ts, c), lambda i: (i, cb))
    full = lambda a: pl.BlockSpec(a.shape, lambda i: (0, 0))
    return _call(
        body,
        "mix_b_bwd_conv",
        (n_t,),
        [
            blk(0),
            pl.BlockSpec((hb, c), lambda i: (nxt(i), 0)),
            blk(0),
            pl.BlockSpec((hb, c), lambda i: (prev(i), 0)),
            blk(3),
            blk(4),
            full(b_glu),
            full(wd),
        ],
        [d_cv, d_cv, u, u, proj, proj, b_glu, wd],
        [
            pl.BlockSpec((ts, 2 * c), lambda i: (i, 0)),
            pl.BlockSpec((None, kp, c), lambda i: (i, 0, 0)),
            pl.BlockSpec((None, 1, 2 * c), lambda i: (i, 0, 0)),
        ],
        [SDS((s, 2 * c), BF16), SDS((n_t, kp, c), F32), SDS((n_t, 1, 2 * c), F32)],
        scratch=[
            pltpu.VMEM((ts + hb, c), F32), pltpu.VMEM((hb + ts, c), F32), pltpu.VMEM((ts, c), F32),
            pltpu.VMEM((7, hb + ts - 8, c), F32), pltpu.VMEM((7, hb + ts - 8, c), F32), pltpu.VMEM((8 * CONF_K, c), F32),
        ],
        sem=("parallel",),
        comm=comm,
    )


def _mix_a_bwd(d_ya, proj, wa, s, c):
    ts, hb = _tile(s, 256), HALO_A
    prev, nxt = _prev_halo(ts, hb), _next_halo(ts, hb, s)
    n_t = s // ts
    kp = wa.shape[0]

    def body(dya, dya_n, ah, ab, ac, ah_p, ac_p, ab_n, w, d_o, dwa_o, zbuf, dbuf, dzb):
        i = pl.program_id(0)
        zbuf[pl.ds(0, hb), :] = jnp.where(i == 0, 0.0, ac_p[...].astype(F32) * ah_p[...].astype(F32))
        zbuf[pl.ds(hb, ts), :] = ac[...].astype(F32) * ah[...].astype(F32)
        dbuf[pl.ds(0, ts), :] = dya[...].astype(F32) * ab[...].astype(F32)
        dbuf[pl.ds(ts, hb), :] = jnp.where(i == n_t - 1, 0.0, dya_n[...].astype(F32) * ab_n[...].astype(F32))
        dw_rows = [jnp.zeros((1, c), F32) for _ in range(CONV_A_K)]
        for r0 in range(0, ts, CONV_ROWS):
            cz = jnp.zeros((CONV_ROWS, c), F32)
            dz = jnp.zeros((CONV_ROWS, c), F32)
            dc = dbuf[pl.ds(r0, CONV_ROWS), :]
            for k in range(CONV_A_K):
                zk = zbuf[pl.ds(hb + r0 - (CONV_A_K - 1) + k, CONV_ROWS), :]
                cz = cz + w[k : k + 1, :] * zk
                dz = dz + w[k : k + 1, :] * dbuf[pl.ds(r0 + (CONV_A_K - 1) - k, CONV_ROWS), :]
                dw_rows[k] = dw_rows[k] + jnp.sum(dc * zk, axis=0, keepdims=True)
            d_o[pl.ds(r0, CONV_ROWS), c : 2 * c] = (dya[pl.ds(r0, CONV_ROWS), :].astype(F32) * cz).astype(BF16)
            dzb[pl.ds(r0, CONV_ROWS), :] = dz
        d_z = dzb[...]
        d_o[:, 0:c] = (d_z * ac[...].astype(F32)).astype(BF16)
        d_o[:, 2 * c : 3 * c] = (d_z * ah[...].astype(F32)).astype(BF16)
        for k in range(CONV_A_K):
            dwa_o[k : k + 1, :] = dw_rows[k]
        dwa_o[CONV_A_K:kp, :] = jnp.zeros((kp - CONV_A_K, c), F32)

    blk = lambda cb: pl.BlockSpec((ts, c), lambda i: (i, cb))
    hp = lambda cb: pl.BlockSpec((hb, c), lambda i: (prev(i), cb))
    hn = lambda cb: pl.BlockSpec((hb, c), lambda i: (nxt(i), cb))
    return pl.pallas_call(
        body,
        name="mix_a_bwd",
        grid=(n_t,),
        in_specs=[blk(0), hn(0), blk(0), blk(1), blk(2), hp(0), hp(2), hn(1), pl.BlockSpec(wa.shape, lambda i: (0, 0))],
        out_specs=[pl.BlockSpec((ts, 3 * c), lambda i: (i, 0)), pl.BlockSpec((None, kp, c), lambda i: (i, 0, 0))],
        out_shape=[SDS((s, 3 * c), BF16), SDS((n_t, kp, c), F32)],
        scratch_shapes=[pltpu.VMEM((hb + ts, c), F32), pltpu.VMEM((ts + hb, c), F32), pltpu.VMEM((ts, c), F32)],
        compiler_params=_params(("parallel",)),
    )(d_ya, d_ya, proj, proj, proj, proj, proj, proj, wa)


def _ep_bf16(accs, ex, os_):
    os_[0][...] = accs[0].astype(BF16)


def _mm_tn(name, a, b, tm=2048, tn=1024, tk=1024):
    m, k1 = a.shape
    n = b.shape[1]
    tm, tn, tk = _tile(k1, tm), _tile(n, tn), _tile(m, tk)
    return _fmm(
        name,
        (k1 // tm, n // tn, m // tk),
        [(a, pl.BlockSpec((tk, tm), lambda i, j, k: (k, i))), (b, pl.BlockSpec((tk, tn), lambda i, j, k: (k, j)))],
        [(0, 1, TN, 0, None)],
        [(tm, tn)],
        [],
        [(SDS((k1, n), BF16), pl.BlockSpec((tm, tn), lambda i, j, k: (i, j)))],
        _ep_bf16,
    )[0][0]


def _mm_nt(name, a, b, tm=1024, tn=1024, comm=None):
    m, kk = a.shape
    n = b.shape[0]
    tm, tn = _tile(m, tm), _tile(n, tn)
    outs, couts = _fmm(
        name,
        (m // tm, n // tn, 1),
        [(a, pl.BlockSpec((tm, kk), lambda i, j, k: (i, 0))), (b, pl.BlockSpec((tn, kk), lambda i, j, k: (j, 0)))],
        [(0, 1, NT, 0, None)],
        [(tm, tn)],
        [],
        [(SDS((m, n), BF16), pl.BlockSpec((tm, tn), lambda i, j, k: (i, j)))],
        _ep_bf16,
        comm=comm,
    )
    return outs[0], couts


def _dev_index(dev):
    return 4 * dev[0] + 2 * dev[1] + dev[2]


def _region(ref, kind, j, shard_shape):
    if kind == "col":
        ns = shard_shape[1]
        return ref.at[:, pl.ds(pl.multiple_of(j * ns, 128), ns)]
    if kind == "row":
        rs = shard_shape[0]
        return ref.at[pl.ds(pl.multiple_of(j * rs, 8), rs), :]
    return ref.at[j]


def _whole_shape(kind, shard_shape):
    if kind == "col":
        return (shard_shape[0], NDEV * shard_shape[1])
    if kind == "row":
        return (NDEV * shard_shape[0], shard_shape[1])
    return (NDEV,) + tuple(shard_shape)


def _place():
    return lax.axis_index("x"), lax.axis_index("y"), lax.axis_index("c")


def _proj_gather(n1, w_shard, early, late):
    s, d = n1.shape
    ns = w_shard.shape[1]
    pw = 2 * ns
    tm = _tile(s // 2, 512)
    n_i = s // tm
    comm = _join(early, late)
    n_early = (len(early["ins"]), len(early["outs"]), len(early["sems"]))
    assert n_i >= 2 and not comm["alias"]
    x0, y0, _ = _place()
    order = jnp.stack([2 * x0 + y0, 2 * x0 + (1 - y0), 2 * (1 - x0) + y0, 2 * (1 - x0) + (1 - y0)]).astype(jnp.int32)
    n_ci, n_co = len(comm["ins"]), len(comm["outs"])

    def body(order_ref, n1_ref, wsh_ref, *rest):
        ci = rest[:n_ci]
        proj_ref, win_ref = rest[n_ci], rest[n_ci + 1]
        co = rest[n_ci + 2 : n_ci + 2 + n_co]
        wfull, send, recv, fsend, frecv, loc, osem = rest[n_ci + 2 + n_co : n_ci + 9 + n_co]
        cs = rest[n_ci + 9 + n_co :]
        u, i = pl.program_id(0), pl.program_id(1)
        x, y, c = _place()
        sib = (x, y, 1 - c)
        chips = [(x, y), (x, 1 - y), (1 - x, y), (1 - x, 1 - y)]
        peers = [sib] + [(*ch, c) for ch in chips[1:]]
        blk = lambda ch, core: wfull.at[2 * ch[0] + ch[1], :, pl.ds(pl.multiple_of(core * ns, 128), ns)]
        sends = [_remote(blk(chips[0], c), blk(chips[0], c), send.at[k], recv.at[k], peers[k]) for k in range(4)]
        arrivals = [_remote(blk(chips[0], 1 - c), blk(chips[0], 1 - c), send.at[0], recv.at[0], sib)] + [
            _remote(blk(chips[k], c), blk(chips[k], c), send.at[k], recv.at[k], peers[k]) for k in range(1, 4)
        ]
        passes = [_remote(blk(chips[k], c), blk(chips[k], c), fsend.at[k - 1], frecv.at[k - 1], sib) for k in range(1, 4)]
        passed = [_remote(blk(chips[k], 1 - c), blk(chips[k], 1 - c), fsend.at[k - 1], frecv.at[k - 1], sib) for k in range(1, 4)]
        mine = lambda: pltpu.make_async_copy(wsh_ref, blk(chips[0], c), loc.at[0])

        def to_hbm(unit):
            q = order_ref[unit]
            return pltpu.make_async_copy(wfull.at[q], win_ref.at[:, pl.ds(pl.multiple_of(q * pw, 128), pw)], osem.at[unit])

        a, b, e = n_early
        early_refs = (ci[:a], co[:b], cs[:e])
        late_refs = (ci[a:], co[b:], cs[e:])

        @pl.when(jnp.logical_and(u == 0, i == 0))
        def _():
            mine().start()
            mine().wait()
            for snd in sends[:3]:
                snd().start()
            early["start"](*early_refs)
            arrivals[0]().wait_recv()

        @pl.when(jnp.logical_and(u == 1, i == 0))
        def _():
            sends[3]().start()

        @pl.when(jnp.logical_and(u == 2, i == 0))
        def _():
            late["start"](*late_refs)

        for nxt in range(1, 4):

            @pl.when(jnp.logical_and(u == nxt - 1, i == n_i - 1))
            def _(nxt=nxt):
                passed[nxt - 1]().wait_recv()

        proj_ref[...] = jnp.dot(n1_ref[...], wfull[order_ref[u]], preferred_element_type=F32).astype(BF16)

        for nxt in range(1, 4):

            @pl.when(jnp.logical_and(u == nxt - 1, i == n_i - 2))
            def _(nxt=nxt):
                arrivals[nxt]().wait_recv()
                passes[nxt - 1]().start()

        for unit in range(4):

            @pl.when(jnp.logical_and(u == unit, i == n_i - 1))
            def _(unit=unit):
                to_hbm(unit).start()

        @pl.when(jnp.logical_and(u == 3, i == n_i - 1))
        def _():
            for snd in sends + passes:
                snd().wait_send()
            for unit in range(4):
                to_hbm(unit).wait()
            comm["finish"](ci, co, cs)

    hbm = pl.BlockSpec(memory_space=pl.ANY)
    dma = pltpu.SemaphoreType.DMA
    res = pl.pallas_call(
        body,
        name="proj",
        grid_spec=pltpu.PrefetchScalarGridSpec(
            num_scalar_prefetch=1,
            grid=(4, n_i),
            in_specs=[pl.BlockSpec((tm, d), lambda u, i, order_ref: (i, 0)), hbm] + [hbm] * n_ci,
            out_specs=[pl.BlockSpec((tm, pw), lambda u, i, order_ref: (i, order_ref[u])), hbm] + [hbm] * n_co,
            scratch_shapes=[pltpu.VMEM((4, d, pw), BF16), dma((4,)), dma((4,)), dma((3,)), dma((3,)), dma((1,)), dma((4,))]
            + list(comm["sems"]),
        ),
        out_shape=[SDS((s, NDEV * ns), BF16), SDS((d, NDEV * ns), BF16)] + list(comm["outs"]),
        compiler_params=_params(("arbitrary", "arbitrary")),
    )(order, n1, w_shard, *comm["ins"])
    return res[0], res[1], list(res[2:])


def _peer(me, r):
    x, y, c = me
    return (1 - x if r & 4 else x, 1 - y if r & 2 else y, 1 - c if r & 1 else c)


def _remote(src, dst, send_sem, recv_sem, to):
    return lambda: pltpu.make_async_remote_copy(
        src_ref=src, dst_ref=dst, send_sem=send_sem, recv_sem=recv_sem, device_id=to, device_id_type=MESH
    )


def _run(pairs, locals_, start):
    if start:
        for cp in locals_:
            cp.start()
        for snd, _ in pairs:
            snd().start()
    else:
        for snd, arr in pairs:
            arr().wait_recv()
            snd().wait_send()
        for cp in locals_:
            cp.wait()


def _stage(ins, outs, alias, sems, build):
    return dict(
        ins=list(ins), outs=list(outs), alias=alias, sems=list(sems),
        start=lambda i, o, s: _run(*build(i, o, s), True),
        finish=lambda i, o, s: _run(*build(i, o, s), False),
    )


def _ag1(shards, kinds):
    n_t = len(shards)
    shapes = [tuple(sh.shape) for sh in shards]

    def build(srcs, dsts, sems):
        send, recv, loc = sems
        x, y, c = _place()
        me = (x, y, c)
        peers = [(x, y, 1 - c), (1 - x, y, c), (x, 1 - y, c), (1 - x, 1 - y, c)]
        reg = lambda t, dev: _region(dsts[t], kinds[t], _dev_index(dev), shapes[t])
        pairs = []
        for t in range(n_t):
            for k, peer in enumerate(peers):
                snd = _remote(srcs[t], reg(t, me), send.at[t, k], recv.at[t, k], peer)
                arr = _remote(reg(t, peer), reg(t, peer), send.at[t, k], recv.at[t, k], peer)
                pairs.append((snd, arr))
        mine = [pltpu.make_async_copy(srcs[t], reg(t, me), loc.at[t]) for t in range(n_t)]
        return pairs, mine

    outs = [SDS(_whole_shape(kinds[t], shapes[t]), shards[t].dtype) for t in range(n_t)]
    dma = pltpu.SemaphoreType.DMA
    return _stage(shards, outs, {}, [dma((n_t, 4)), dma((n_t, 4)), dma((n_t,))], build)


def _ag_direct(shards, kinds):
    n_t = len(shards)
    shapes = [tuple(sh.shape) for sh in shards]

    def build(srcs, dsts, sems):
        send, recv, loc = sems
        me = _place()
        reg = lambda t, dev: _region(dsts[t], kinds[t], _dev_index(dev), shapes[t])
        pairs = []
        for t in range(n_t):
            for r in range(1, NDEV):
                peer = _peer(me, r)
                snd = _remote(srcs[t], reg(t, me), send.at[t, r - 1], recv.at[t, r - 1], peer)
                arr = _remote(reg(t, peer), reg(t, peer), send.at[t, r - 1], recv.at[t, r - 1], peer)
                pairs.append((snd, arr))
        mine = [pltpu.make_async_copy(srcs[t], reg(t, me), loc.at[t]) for t in range(n_t)]
        return pairs, mine

    outs = [SDS(_whole_shape(kinds[t], shapes[t]), shards[t].dtype) for t in range(n_t)]
    dma = pltpu.SemaphoreType.DMA
    return _stage(shards, outs, {}, [dma((n_t, 7)), dma((n_t, 7)), dma((n_t,))], build)


def _ag2(wholes, kinds, shapes):
    n_t = len(wholes)

    def build(_, dsts, sems):
        send, recv = sems
        x, y, c = _place()
        sib = (x, y, 1 - c)
        chips = [(1 - x, y), (x, 1 - y), (1 - x, 1 - y)]
        reg = lambda t, dev: _region(dsts[t], kinds[t], _dev_index(dev), shapes[t])
        pairs = []
        for t in range(n_t):
            for j, chip in enumerate(chips):
                snd = _remote(reg(t, (*chip, c)), reg(t, (*chip, c)), send.at[t, j], recv.at[t, j], sib)
                arr = _remote(reg(t, (*chip, 1 - c)), reg(t, (*chip, 1 - c)), send.at[t, j], recv.at[t, j], sib)
                pairs.append((snd, arr))
        return pairs, []

    outs = [SDS(w.shape, w.dtype) for w in wholes]
    dma = pltpu.SemaphoreType.DMA
    return _stage(wholes, outs, {t: t for t in range(n_t)}, [dma((n_t, 3)), dma((n_t, 3))], build)


def _chip_of(q):
    return (q >> 1, q & 1)


def _rs1(wholes, kinds, shapes):
    n_t = len(wholes)

    def build(srcs, outs, sems):
        send, recv = sems
        x, y, c = _place()
        sib = (x, y, 1 - c)
        pairs = []
        for t in range(n_t):
            for q in range(4):
                theirs = _region(srcs[t], kinds[t], _dev_index((*_chip_of(q), 1 - c)), shapes[t])
                pairs.append((
                    _remote(theirs, outs[t].at[q], send.at[t, q], recv.at[t, q], sib),
                    _remote(outs[t].at[q], outs[t].at[q], send.at[t, q], recv.at[t, q], sib),
                ))
        return pairs, []

    slabs = [SDS((4,) + tuple(shapes[t]), wholes[t].dtype) for t in range(n_t)]
    dma = pltpu.SemaphoreType.DMA
    return _stage(wholes, slabs, {}, [dma((n_t, 4)), dma((n_t, 4))], build)


def _rs2(pair_sums):
    n_t = len(pair_sums)

    def build(srcs, lands, sems):
        send, recv, loc = sems
        x, y, c = _place()
        my_chip = 2 * x + y
        pairs, mine = [], []
        for t in range(n_t):
            for j, (px, py) in enumerate([(1 - x, y), (x, 1 - y), (1 - x, 1 - y)]):
                q = 2 * px + py
                pairs.append((
                    _remote(srcs[t].at[q], lands[t].at[my_chip], send.at[t, j], recv.at[t, j], (px, py, c)),
                    _remote(lands[t].at[q], lands[t].at[q], send.at[t, j], recv.at[t, j], (px, py, c)),
                ))
            mine.append(pltpu.make_async_copy(srcs[t].at[my_chip], lands[t].at[my_chip], loc.at[t]))
        return pairs, mine

    outs = [SDS(q.shape, q.dtype) for q in pair_sums]
    dma = pltpu.SemaphoreType.DMA
    return _stage(pair_sums, outs, {}, [dma((n_t, 3)), dma((n_t, 3)), dma((n_t,))], build)


def _pair_sum(name, whole, kind, got):
    _, rows, cols = got.shape
    tr = _tile(rows, 256)
    n_r = rows // tr
    core = lax.axis_index("c").astype(jnp.int32).reshape(1)

    def body(_, a, b, o):
        o[...] = (a[...].astype(F32) + b[...].astype(F32)).astype(BF16)

    if kind == "col":
        own = pl.BlockSpec((tr, cols), lambda q, i, c_ref: (i, 2 * q + c_ref[0]))
    elif kind == "row":
        own = pl.BlockSpec((tr, cols), lambda q, i, c_ref: ((2 * q + c_ref[0]) * n_r + i, 0))
    else:
        own = pl.BlockSpec((None, tr, cols), lambda q, i, c_ref: (2 * q + c_ref[0], i, 0))
    slab = pl.BlockSpec((None, tr, cols), lambda q, i, c_ref: (q, i, 0))
    return pl.pallas_call(
        body,
        name=name,
        grid_spec=pltpu.PrefetchScalarGridSpec(
            num_scalar_prefetch=1, grid=(4, n_r), in_specs=[own, slab], out_specs=slab
        ),
        out_shape=SDS(got.shape, BF16),
        compiler_params=_params(("parallel", "parallel")),
    )(core, whole, got)


def _all_reduce_small(part):
    r_, c_ = part.shape

    def body(src, land, total, send_sems, recv_sems):
        me = _place()
        my = _dev_index(me)
        land[my] = src[...]

        def copy(r):
            peer = _peer(me, r)
            return pltpu.make_async_remote_copy(
                src_ref=src,
                dst_ref=land.at[my],
                send_sem=send_sems.at[r - 1],
                recv_sem=recv_sems.at[r - 1],
                device_id=peer,
                device_id_type=MESH,
            )

        def arrival(r):
            peer = _peer(me, r)
            slab = land.at[_dev_index(peer)]
            return pltpu.make_async_remote_copy(
                src_ref=slab,
                dst_ref=slab,
                send_sem=send_sems.at[r - 1],
                recv_sem=recv_sems.at[r - 1],
                device_id=peer,
                device_id_type=MESH,
            )

        sends = [copy(r) for r in range(1, NDEV)]
        for cp in sends:
            cp.start()
        for r in range(1, NDEV):
            arrival(r).wait_recv()
        for cp in sends:
            cp.wait_send()
        acc = land[0]
        for d in range(1, NDEV):
            acc = acc + land[d]
        total[...] = acc

    vmem = pl.BlockSpec(memory_space=pltpu.VMEM)
    return pl.pallas_call(
        body,
        name="all_reduce_small",
        in_specs=[vmem],
        out_specs=[vmem, vmem],
        out_shape=[SDS((NDEV, r_, c_), F32), SDS((r_, c_), F32)],
        scratch_shapes=[pltpu.SemaphoreType.DMA((7,)), pltpu.SemaphoreType.DMA((7,))],
    )(part)[1]


def _adamw_math(g, w, m, v):
    m2 = ADAM_B1 * m + (1.0 - ADAM_B1) * g
    v2 = ADAM_B2 * v + (1.0 - ADAM_B2) * (g * g)
    m_hat = m2 / (1.0 - ADAM_B1**ADAM_STEP)
    v_hat = v2 / (1.0 - ADAM_B2**ADAM_STEP)
    delta = -ADAM_LR * (m_hat / (jnp.sqrt(v_hat) + ADAM_EPS) + ADAM_WD * w)
    return delta, m2, v2


def _adamw_big(name, land, w, m, v):
    rows, cols = w.shape
    tr = _tile(rows, 256)
    n_slab = land.shape[0]

    def body(l_ref, w_ref, m_ref, v_ref, g_o, d_o, m_o, v_o):
        g = l_ref[0].astype(F32)
        for d in range(1, n_slab):
            g = g + l_ref[d].astype(F32)
        delta, m2, v2 = _adamw_math(g, w_ref[...], m_ref[...], v_ref[...])
        g_o[...] = g
        d_o[...] = delta
        m_o[...] = m2
        v_o[...] = v2

    blk = pl.BlockSpec((tr, cols), lambda i: (i, 0))
    return pl.pallas_call(
        body,
        name=name,
        grid=(rows // tr,),
        in_specs=[pl.BlockSpec((n_slab, tr, cols), lambda i: (0, i, 0)), blk, blk, blk],
        out_specs=[blk] * 4,
        out_shape=[SDS((rows, cols), F32)] * 4,
        compiler_params=_params(("parallel",)),
    )(land, w, m, v)


def _adamw_small(g, w, m, v):
    def body(g_ref, w_ref, m_ref, v_ref, d_o, m_o, v_o):
        delta, m2, v2 = _adamw_math(g_ref[...], w_ref[...], m_ref[...], v_ref[...])
        d_o[...] = delta
        m_o[...] = m2
        v_o[...] = v2

    vmem = pl.BlockSpec(memory_space=pltpu.VMEM)
    return pl.pallas_call(
        body,
        name="adamw_small",
        in_specs=[vmem] * 4,
        out_specs=[vmem] * 3,
        out_shape=[SDS(g.shape, F32)] * 3,
    )(g, w, m, v)


def _pack(pieces, width):
    flat = jnp.concatenate([p.reshape(-1) for p in pieces])
    rows = -(-flat.shape[0] // (8 * width)) * 8
    flat = jnp.pad(flat, (0, rows * width - flat.shape[0]))
    return flat.reshape(rows, width)


def _unpack(packed, shapes):
    flat = packed.reshape(-1)
    out, off = [], 0
    for shp in shapes:
        n = 1
        for d in shp:
            n *= d
        out.append(flat[off : off + n].reshape(shp))
        off += n
    return out


def kernel(x, p, g_mix, w_in, conv_a_w, w_out_a, b_glu, conf_dw_w, conf_dw_b, conf_ln_g, conf_ln_b, w_pw_b, b_pw_b, w_o, g_ffn, w_gate, w_up, w_down, g_ple, w_ple_gate, w_ple_proj, g_final, loss_target, m_g_mix, m_w_in, m_conv_a_w, m_w_out_a, m_b_glu, m_conf_dw_w, m_conf_dw_b, m_conf_ln_g, m_conf_ln_b, m_w_pw_b, m_b_pw_b, m_w_o, m_g_ffn, m_w_gate, m_w_up, m_w_down, m_g_ple, m_w_ple_gate, m_w_ple_proj, m_g_final, v_g_mix, v_w_in, v_conv_a_w, v_w_out_a, v_b_glu, v_conf_dw_w, v_conf_dw_b, v_conf_ln_g, v_conf_ln_b, v_w_pw_b, v_b_pw_b, v_w_o, v_g_ffn, v_w_gate, v_w_up, v_w_down, v_g_ple, v_w_ple_gate, v_w_ple_proj, v_g_final):
    s, d = x.shape[1], x.shape[2]
    c = conf_ln_g.shape[-1]
    pdim = w_ple_proj.shape[1]
    fs = w_gate.shape[-1]
    nin = NDEV * w_in.shape[-1]
    assert d == 2 * c and nin == 5 * c + 2 * d, (d, c, nin)
    x2, p2, tgt = x[0], p[0, 0], loss_target[0]
    gfin = g_final.reshape(1, d)

    kpa, kpb = 8, HALO_B
    wa_sh = jnp.pad(conv_a_w[0], ((0, kpa - CONV_A_K), (0, 0)))
    wd_sh = jnp.pad(conf_dw_w[0], ((0, kpb - CONF_K), (0, 0)))
    kind_of = dict(w_in="col", w_out_a="col", w_pw_b="col", w_ple_proj="col", w_o="row", w_ple_gate="row",
                   w_gate="blk", w_up="blk", w_down="blk")
    weight = dict(w_in=w_in, w_out_a=w_out_a, w_pw_b=w_pw_b, w_ple_proj=w_ple_proj, w_o=w_o, w_ple_gate=w_ple_gate,
                  w_gate=w_gate, w_up=w_up, w_down=w_down)
    shard_of = {nm: tuple(w.shape[1:]) for nm, w in weight.items()}
    bf16_shard = lambda nm: weight[nm][0].astype(BF16)
    kinds_ = lambda grp: [kind_of[nm] for nm in grp]
    shapes_ = lambda grp: [shard_of[nm] for nm in grp]
    first_stage = lambda grp: _ag1([bf16_shard(nm) for nm in grp], kinds_(grp))
    second_stage = lambda grp, parts: _ag2(parts, kinds_(grp), shapes_(grp))
    grp_1 = ["w_out_a", "w_pw_b"]
    grp_2 = ["w_o", "w_gate"]
    grp_3 = ["w_up"]
    grp_4 = ["w_down"]
    grp_5 = ["w_ple_gate", "w_ple_proj"]

    tm = _tile(s, 1024)
    tn = _tile(d, 1024)
    assert (5 * c) % tn == 0 and d % tn == 0 and c % tn == 0
    ga_blk, gb_blk = (5 * c) // tn, (5 * c + d) // tn
    ij = lambda i, j, k: (i, j)
    row_i = lambda i, j, k: (i, 0)

    n1 = _rms_fwd("rms1", x2, g_mix)
    proj, win, got = _proj_gather(
        n1, bf16_shard("w_in"),
        _join(_ag_direct([wa_sh, wd_sh], ["col", "col"]), first_stage(grp_1)), first_stage(grp_2),
    )
    (wa, wd), part_12 = got[:2], got[2:]
    grp_12 = grp_1 + grp_2
    ya_in = _mix_a_fwd(proj, wa, s, c)
    (v_act, u_act, cv), got = _mix_b_fwd(
        proj, b_glu, wd, conf_dw_b, conf_ln_g, conf_ln_b, s, c,
        comm=_join(second_stage(grp_12, part_12), first_stage(grp_3)),
    )
    (wouta, wpw, wo, wg), part_3 = got[: len(grp_12)], got[len(grp_12) :]

    def ep_merge(accs, ex, os_):
        sa = _sigmoid(ex[0][...].astype(F32))
        sb = _sigmoid(ex[1][...].astype(F32))
        ya = accs[0]
        yb = accs[1] + ex[2][...]
        os_[0][...] = (sa * ya + sb * yb).astype(BF16)
        os_[1][...] = ya.astype(BF16)
        os_[2][...] = yb.astype(BF16)

    gate_a_spec = pl.BlockSpec((tm, tn), lambda i, j, k: (i, ga_blk + j))
    gate_b_spec = pl.BlockSpec((tm, tn), lambda i, j, k: (i, gb_blk + j))
    out_sd = (SDS((s, d), BF16), pl.BlockSpec((tm, tn), ij))
    (m_act, ya, yb), got = _fmm(
        "merge", (s // tm, d // tn, 1),
        [(ya_in, pl.BlockSpec((tm, c), row_i)), (wouta, pl.BlockSpec((c, tn), lambda i, j, k: (0, j))),
         (v_act, pl.BlockSpec((tm, c), row_i)), (wpw, pl.BlockSpec((c, tn), lambda i, j, k: (0, j)))],
        [(0, 1, NN, 0, None), (2, 3, NN, 1, None)], [(tm, tn), (tm, tn)],
        [(proj, gate_a_spec), (proj, gate_b_spec), (b_pw_b, pl.BlockSpec((1, tn), lambda i, j, k: (0, j)))],
        [out_sd, out_sd, out_sd], ep_merge, csplit=EPILOGUE_CHUNK,
        comm=_join(second_stage(grp_3, part_3), first_stage(grp_4)),
    )
    (wu,), part_4 = got[: len(grp_3)], got[len(grp_3) :]

    def ep_residual(accs, ex, os_):
        os_[0][...] = accs[0] + ex[0][...]

    (h1,), got = _fmm(
        "w_o", (s // tm, d // tn, 1),
        [(m_act, pl.BlockSpec((tm, d), row_i)), (wo, pl.BlockSpec((d, tn), lambda i, j, k: (0, j)))],
        [(0, 1, NN, 0, None)], [(tm, tn)], [(x2, pl.BlockSpec((tm, tn), ij))],
        [(SDS((s, d), F32), pl.BlockSpec((tm, tn), ij))], ep_residual, csplit=EPILOGUE_CHUNK,
        comm=_join(second_stage(grp_4, part_4), first_stage(grp_5)),
    )
    (wdn,), part_5 = got[: len(grp_4)], got[len(grp_4) :]
    n2 = _rms_fwd("rms2", h1, g_ffn)

    def ep_gateup(accs, ex, os_):
        g, u = accs
        os_[0][...] = g.astype(BF16)
        os_[1][...] = u.astype(BF16)
        os_[2][...] = (g * _sigmoid(g) * u).astype(BF16)

    ff_sd = (SDS((NDEV, s, fs), BF16), pl.BlockSpec((None, tm, fs), lambda i, j, k: (j, i, 0)))
    w_col_blk = pl.BlockSpec((None, d, fs), lambda i, j, k: (j, 0, 0))
    (g_act, u_ff, f_act), (wpg, wpp) = _fmm(
        "gate_up", (s // tm, NDEV, 1),
        [(n2, pl.BlockSpec((tm, d), row_i)), (wg, w_col_blk), (wu, w_col_blk)],
        [(0, 1, NN, 0, None), (0, 2, NN, 1, None)], [(tm, fs), (tm, fs)], [],
        [ff_sd, ff_sd, ff_sd], ep_gateup, csplit=EPILOGUE_CHUNK,
        comm=second_stage(grp_5, part_5),
    )
    pair = 2
    (h2,), _ = _fmm(
        "down", (s // tm, d // tn, NDEV // pair),
        [(f_act, pl.BlockSpec((pair, tm, fs), lambda i, j, k: (k, i, 0))),
         (wdn, pl.BlockSpec((pair, fs, tn), lambda i, j, k: (k, 0, j)))],
        [(0, 1, NN, 0, None, pair)], [(tm, tn)], [(h1, pl.BlockSpec((tm, tn), ij))],
        [(SDS((s, d), F32), pl.BlockSpec((tm, tn), ij))], ep_residual,
    )
    n3 = _rms_fwd("rms3", h2, g_ple)

    tr = _tile(s, 256)
    n_r = s // tr
    rows = lambda i, j, k: (i, 0)
    whole = lambda i, j, k: (0, 0)
    part_spec = lambda nrow: pl.BlockSpec((None, nrow, d), lambda i, j, k: (i, 0, 0))

    def ep_ple(accs, ex, os_):
        h2_, t_, gf = ex[0][...], ex[1][...], ex[2][...]
        ple = accs[0]
        s3 = _sigmoid(accs[1])
        h3 = h2_ + s3 * ple
        r = lax.rsqrt(jnp.mean(h3 * h3, axis=-1, keepdims=True) + EPS)
        hn = h3 * r
        e = hn * gf - t_
        loss = 0.5 * jnp.sum(jnp.mean(e * e, axis=-1, keepdims=True), axis=0, keepdims=True)
        dy = e * (1.0 / d)
        dn = dy * gf
        dh3 = r * (dn - hn * jnp.mean(dn * hn, axis=-1, keepdims=True))
        os_[0][...] = dh3
        os_[1][...] = (dh3 * s3).astype(BF16)
        os_[2][...] = (dh3 * ple * s3 * (1.0 - s3)).astype(BF16)
        os_[3][0:1, :] = jnp.sum(dy * hn, axis=0, keepdims=True)
        os_[3][1:2, :] = jnp.broadcast_to(loss, (1, d))

    (dh3, d_ple, d_g3, part_fin), _ = _fmm(
        "ple_loss", (n_r, 1, 1),
        [(p2, pl.BlockSpec((tr, pdim), rows)), (wpp, pl.BlockSpec((pdim, d), whole)),
         (n3, pl.BlockSpec((tr, d), rows)), (wpg, pl.BlockSpec((d, d), whole))],
        [(0, 1, NN, 0, None), (2, 3, NN, 1, None)], [(tr, d), (tr, d)],
        [(h2, pl.BlockSpec((tr, d), rows)), (tgt, pl.BlockSpec((tr, d), rows)), (gfin, pl.BlockSpec((1, d), whole))],
        [(SDS((s, d), F32), pl.BlockSpec((tr, d), rows)), (SDS((s, d), BF16), pl.BlockSpec((tr, d), rows)),
         (SDS((s, d), BF16), pl.BlockSpec((tr, d), rows)), (SDS((n_r, 2, d), F32), part_spec(2))],
        ep_ple,
    )

    g_wpp = _mm_tn("d_w_ple_proj", p2, d_ple)
    g_wpg = _mm_tn("d_w_ple_gate", n3, d_g3)

    def ep_norm_bwd(accs, ex, os_):
        dh, dg = _rms_bwd(accs[0], ex[0][...], ex[2][...])
        dh = ex[1][...] + dh
        os_[0][...] = dh
        os_[1][...] = dh.astype(BF16)
        os_[2][...] = dg

    norm_outs = lambda t: [
        (SDS((s, d), F32), pl.BlockSpec((t, d), rows)), (SDS((s, d), BF16), pl.BlockSpec((t, d), rows)),
        (SDS((s // t, 1, d), F32), part_spec(1)),
    ]
    def exchange1(names, wholes):
        return _rs1(wholes, kinds_(names), shapes_(names))

    def pair_sums(names, wholes, got):
        return [_pair_sum("pair_sum_" + nm, wholes[t], kind_of[nm], got[t]) for t, nm in enumerate(names)]

    lands = {}
    grp1 = ["w_ple_proj", "w_ple_gate"]
    (dh2, dh2b, part_ple), got = _fmm(
        "d_n3", (n_r, 1, 1),
        [(d_g3, pl.BlockSpec((tr, d), rows)), (wpg, pl.BlockSpec((d, d), whole))],
        [(0, 1, NT, 0, None)], [(tr, d)],
        [(h2, pl.BlockSpec((tr, d), rows)), (dh3, pl.BlockSpec((tr, d), rows)), (g_ple, pl.BlockSpec((1, d), whole))],
        norm_outs(tr), ep_norm_bwd,
        comm=exchange1(grp1, [g_wpp, g_wpg]),
    )
    sums1 = pair_sums(grp1, [g_wpp, g_wpg], got)

    def ep_ddown(accs, ex, os_):
        g = ex[0][...].astype(F32)
        u = ex[1][...].astype(F32)
        sg = _sigmoid(g)
        df = accs[0]
        os_[0][...] = (df * u * sg * (1.0 + g * (1.0 - sg))).astype(BF16)
        os_[1][...] = (df * g * sg).astype(BF16)

    ff_in = pl.BlockSpec((None, tm, fs), lambda i, j, k: (j, i, 0))
    (d_g, d_u), got = _fmm(
        "d_down", (s // tm, NDEV, 1),
        [(dh2b, pl.BlockSpec((tm, d), row_i)), (wdn, pl.BlockSpec((None, fs, d), lambda i, j, k: (j, 0, 0)))],
        [(0, 1, NT, 0, None)], [(tm, fs)], [(g_act, ff_in), (u_ff, ff_in)],
        [ff_sd, ff_sd], ep_ddown, csplit=EPILOGUE_CHUNK,
        comm=_rs2(sums1),
    )
    lands.update(zip(grp1, got))
    tk = _tile(s, 1024)
    (g_wdn,), _ = _fmm(
        "d_w_down", (NDEV, 1, s // tk),
        [(f_act, pl.BlockSpec((None, tk, fs), lambda i, j, k: (i, k, 0))), (dh2b, pl.BlockSpec((tk, d), lambda i, j, k: (k, 0)))],
        [(0, 1, TN, 0, None)], [(fs, d)], [],
        [(SDS((NDEV, fs, d), BF16), pl.BlockSpec((None, fs, d), lambda i, j, k: (i, 0, 0)))], _ep_bf16,
    )

    def ep_two_bf16(accs, ex, os_):
        os_[0][...] = accs[0].astype(BF16)
        os_[1][...] = accs[1].astype(BF16)

    ff_k = pl.BlockSpec((None, tk, fs), lambda i, j, k: (i, k, 0))
    wcol_sd = (SDS((NDEV, d, fs), BF16), pl.BlockSpec((None, d, fs), lambda i, j, k: (i, 0, 0)))
    grp2 = ["w_down"]
    (g_wg, g_wu), got = _fmm(
        "d_w_gate_up", (NDEV, 1, s // tk),
        [(n2, pl.BlockSpec((tk, d), lambda i, j, k: (k, 0))), (d_g, ff_k), (d_u, ff_k)],
        [(0, 1, TN, 0, None), (0, 2, TN, 1, None)], [(d, fs), (d, fs)], [],
        [wcol_sd, wcol_sd], ep_two_bf16,
        comm=exchange1(grp2, [g_wdn]),
    )
    sums2 = pair_sums(grp2, [g_wdn], got)
    grp3 = ["w_gate", "w_up"]
    th = _tile(s // 2, 1024)
    ff_a = pl.BlockSpec((None, th, fs), lambda i, j, k: (k, i, 0))
    w_k = pl.BlockSpec((None, d, fs), lambda i, j, k: (k, 0, 0))
    (d_n2,), got = _fmm(
        "d_n2", (s // th, 1, NDEV),
        [(d_g, ff_a), (wg, w_k), (d_u, ff_a), (wu, w_k)],
        [(0, 1, NT, 0, None), (2, 3, NT, 0, None)], [(th, d)], [],
        [(SDS((s, d), BF16), pl.BlockSpec((th, d), rows))], _ep_bf16,
        comm=_join(_rs2(sums2), exchange1(grp3, [g_wg, g_wu])),
    )
    lands.update(zip(grp2, got[:1]))
    sums3 = pair_sums(grp3, [g_wg, g_wu], got[1:])
    dh1, dh1b, part_ffn = _norm_bwd("d_h1", d_n2, h1, dh2, g_ffn, True)
    g_wo = _mm_tn("d_w_o", m_act, dh1b)

    def ep_dm(accs, ex, os_):
        ya_, yb_ = ex[0][...].astype(F32), ex[1][...].astype(F32)
        sa = _sigmoid(ex[2][...].astype(F32))
        sb = _sigmoid(ex[3][...].astype(F32))
        dm = accs[0]
        d_yb = dm * sb
        os_[0][...] = (dm * sa).astype(BF16)
        os_[1][...] = d_yb.astype(BF16)
        os_[2][...] = (dm * ya_ * sa * (1.0 - sa)).astype(BF16)
        os_[3][...] = (dm * yb_ * sb * (1.0 - sb)).astype(BF16)
        os_[4][...] = jnp.sum(d_yb, axis=0, keepdims=True)

    tile_ij = pl.BlockSpec((tm, tn), ij)
    grp4 = ["w_o"]
    (d_ya, d_yb, d_ga, d_gb, part_bpw), got = _fmm(
        "d_merge", (s // tm, d // tn, 1),
        [(dh1b, pl.BlockSpec((tm, d), row_i)), (wo, pl.BlockSpec((tn, d), lambda i, j, k: (j, 0)))],
        [(0, 1, NT, 0, None)], [(tm, tn)],
        [(ya, tile_ij), (yb, tile_ij), (proj, gate_a_spec), (proj, gate_b_spec)],
        [out_sd, out_sd, out_sd, out_sd,
         (SDS((s // tm, 1, d), F32), pl.BlockSpec((None, 1, tn), lambda i, j, k: (i, 0, j)))],
        ep_dm, csplit=EPILOGUE_CHUNK,
        comm=exchange1(grp4, [g_wo]),
    )
    sums4 = pair_sums(grp4, [g_wo], got)
    g_wouta = _mm_tn("d_w_out_a", ya_in, d_ya)
    g_wpw = _mm_tn("d_w_pw_b", v_act, d_yb)
    grp5 = ["w_out_a", "w_pw_b"]
    d_ya_in, got = _mm_nt("d_ya_in", d_ya, wouta, comm=exchange1(grp5, [g_wouta, g_wpw]))
    sums5 = pair_sums(grp5, [g_wouta, g_wpw], got)
    d_v, _ = _mm_nt("d_v", d_yb, wpw)
    d_cv, part_ln = _mix_b_bwd1(d_v, cv, conf_ln_g, conf_ln_b, s, c)
    (d_b, part_wd, part_bglu), got = _mix_b_bwd2(d_cv, u_act, proj, b_glu, wd, s, c, comm=_rs2(sums3))
    lands.update(zip(grp3, got))
    d_a, part_wa = _mix_a_bwd(d_ya_in, proj, wa, s, c)

    nb = nin // c
    gblk = d // c
    lo = [0, 3, 5, 5 + gblk]
    hi = [3, 5, 5 + gblk, 5 + 2 * gblk]
    pieces = [d_a, d_b, d_ga, d_gb]

    def active(q, ax):
        return lambda ids: jnp.logical_and(ids[ax] >= lo[q], ids[ax] < hi[q])

    def piece_spec(q, rows_, ax, row0=0):
        def index(i, j, k):
            ids = (i, j, k)
            col = jnp.clip(ids[ax] - lo[q], 0, hi[q] - lo[q] - 1)
            row = i + row0 if ax == 2 else jnp.where(active(q, ax)(ids), k, 0)
            return (row, col)

        return pl.BlockSpec((rows_, c), index)

    tkw = _tile(s, 1024)
    (g_win,), got = _fmm(
        "d_w_in", (1, nb, s // tkw),
        [(n1, pl.BlockSpec((tkw, d), lambda i, j, k: (k, 0)))]
        + [(pieces[q], piece_spec(q, tkw, 1)) for q in range(4)],
        [(0, 1 + q, TN, 0, active(q, 1)) for q in range(4)], [(d, c)], [],
        [(SDS((d, nin), BF16), pl.BlockSpec((d, c), lambda i, j, k: (0, j)))], _ep_bf16,
        comm=_rs2(sums4 + sums5),
    )
    lands.update(zip(grp4 + grp5, got))

    grp6 = ["w_in"]
    n_half = (s // th) // 2

    def d_n1_rows(name, row0, n_tiles, comm, into):
        return _fmm(
            name, (n_tiles, 1, nb),
            [(pieces[q], piece_spec(q, th, 2, row0)) for q in range(4)]
            + [(win, pl.BlockSpec((d, c), lambda i, j, k: (0, k)))],
            [(q, 4, NT, 0, active(q, 2)) for q in range(4)], [(th, d)], [],
            [(SDS((s, d), BF16), pl.BlockSpec((th, d), lambda i, j, k: (i + row0, 0)))], _ep_bf16,
            comm=comm, into=into,
        )

    (d_n1,), got = d_n1_rows("d_n1_a", 0, n_half, exchange1(grp6, [g_win]), None)
    (d_n1,), got = d_n1_rows("d_n1_b", n_half, s // th - n_half, _rs2(pair_sums(grp6, [g_win], got)), d_n1)
    lands.update(zip(grp6, got))
    dx, part_mix = _norm_bwd("d_x", d_n1, x2, dh1, g_mix, False)

    small_parts = [
        jnp.sum(part_mix, axis=0),
        jnp.sum(part_bglu, axis=0),
        jnp.sum(part_ln[:, 2], axis=0),
        jnp.sum(part_ln[:, 0], axis=0),
        jnp.sum(part_ln[:, 1], axis=0),
        jnp.sum(part_bpw, axis=0),
        jnp.sum(part_ffn, axis=0),
        jnp.sum(part_ple, axis=0),
        jnp.sum(part_fin[:, 0], axis=0),
        jnp.sum(part_wa, axis=0),
        jnp.sum(part_wd, axis=0),
        jnp.broadcast_to(jnp.sum(part_fin[:, 1, 0]), (c,)),
    ]
    small_shapes = [(1, d), (1, 2 * c), (1, c), (1, c), (1, c), (1, d), (1, d), (1, d), (d,), (kpa, c), (kpb, c), (c,)]
    total = _all_reduce_small(_pack(small_parts, c))
    (gr_g_mix, gr_b_glu, gr_dw_b, gr_ln_g, gr_ln_b, gr_b_pw, gr_g_ffn, gr_g_ple, gr_g_final, gr_wa, gr_wd, loss_row) = _unpack(total, small_shapes)
    loss = loss_row[0]
    my = _dev_index(_place())
    csh = conv_a_w.shape[-1]
    gr_conv_a = lax.dynamic_slice_in_dim(gr_wa[:CONV_A_K], my * csh, csh, axis=1)[None]
    gr_conf_dw = lax.dynamic_slice_in_dim(gr_wd[:CONF_K], my * csh, csh, axis=1)[None]

    big_m = dict(w_in=m_w_in, w_out_a=m_w_out_a, w_pw_b=m_w_pw_b, w_ple_proj=m_w_ple_proj, w_o=m_w_o,
                 w_ple_gate=m_w_ple_gate, w_gate=m_w_gate, w_up=m_w_up, w_down=m_w_down)
    big_v = dict(w_in=v_w_in, w_out_a=v_w_out_a, w_pw_b=v_w_pw_b, w_ple_proj=v_w_ple_proj, w_o=v_w_o,
                 w_ple_gate=v_w_ple_gate, w_gate=v_w_gate, w_up=v_w_up, w_down=v_w_down)
    big_out = {}
    for nm in weight:
        res = _adamw_big("adamw_" + nm, lands[nm], weight[nm][0], big_m[nm][0], big_v[nm][0])
        big_out[nm] = [r[None] for r in res]

    small_names = ["g_mix", "conv_a_w", "b_glu", "conf_dw_w", "conf_dw_b", "conf_ln_g", "conf_ln_b", "b_pw_b", "g_ffn", "g_ple", "g_final"]
    small_g = [gr_g_mix, gr_conv_a, gr_b_glu, gr_conf_dw, gr_dw_b, gr_ln_g, gr_ln_b, gr_b_pw, gr_g_ffn, gr_g_ple, gr_g_final]
    small_w = [g_mix, conv_a_w, b_glu, conf_dw_w, conf_dw_b, conf_ln_g, conf_ln_b, b_pw_b, g_ffn, g_ple, g_final]
    small_m = [m_g_mix, m_conv_a_w, m_b_glu, m_conf_dw_w, m_conf_dw_b, m_conf_ln_g, m_conf_ln_b, m_b_pw_b, m_g_ffn, m_g_ple, m_g_final]
    small_v = [v_g_mix, v_conv_a_w, v_b_glu, v_conf_dw_w, v_conf_dw_b, v_conf_ln_g, v_conf_ln_b, v_b_pw_b, v_g_ffn, v_g_ple, v_g_final]
    shp = [tuple(w.shape) for w in small_w]
    small_g = [g.reshape(sh) for g, sh in zip(small_g, shp)]
    sd, sm, sv = _adamw_small(_pack(small_g, 128), _pack(small_w, 128), _pack(small_m, 128), _pack(small_v, 128))
    small_out = {}
    for nm, g, dl, mm, vv in zip(small_names, small_g, _unpack(sd, shp), _unpack(sm, shp), _unpack(sv, shp)):
        small_out[nm] = [g, dl, mm, vv]

    order = ["g_mix", "w_in", "conv_a_w", "w_out_a", "b_glu", "conf_dw_w", "conf_dw_b", "conf_ln_g", "conf_ln_b", "w_pw_b", "b_pw_b", "w_o", "g_ffn", "w_gate", "w_up", "w_down", "g_ple", "w_ple_gate", "w_ple_proj", "g_final"]
    allo = {**big_out, **small_out}
    outs = [loss, dx[None]]
    for q in range(4):
        outs += [allo[nm][q] for nm in order]
    return tuple(outs)
```

```python
import jax
import jax.numpy as jnp
from jax import lax
from jax.experimental import pallas as pl
from jax.experimental.pallas import tpu as pltpu

F32, BF16 = jnp.float32, jnp.bfloat16
EPS, LN_EPS = 1e-6, 1e-5
ADAM_LR, ADAM_B1, ADAM_B2, ADAM_EPS, ADAM_WD, ADAM_STEP = 0.001, 0.9, 0.999, 1e-08, 0.01, 10
CONV_A_K, CONF_K = 3, 31
NDEV = 8
NN = (((1,), (0,)), ((), ()))
NT = (((1,), (1,)), ((), ()))
TN = (((0,), (0,)), ((), ()))
V7X_VMEM_LIMIT_BYTES = 56 * 1024 * 1024
MESH = pl.DeviceIdType.MESH
SDS = jax.ShapeDtypeStruct
HALO_A, HALO_B = 16, 32
EPILOGUE_CHUNK = 256
CONV_ROWS = 32
CONF_ROWS = 16


def _tile(n, pref):
    t = min(n, pref)
    while n % t:
        t -= 8
    return t


def _sigmoid(x):
    return jax.nn.sigmoid(x)


def _params(sem=None):
    return pltpu.CompilerParams(vmem_limit_bytes=V7X_VMEM_LIMIT_BYTES, dimension_semantics=sem)


def _edge(grid, last):
    cond = None
    for ax, n in enumerate(grid):
        here = pl.program_id(ax) == (n - 1 if last else 0)
        cond = here if cond is None else jnp.logical_and(cond, here)
    return cond


def _join(*comms):
    ins, outs, alias, sems, spans = [], [], {}, [], []
    for cm in comms:
        spans.append((len(ins), len(outs), len(sems)))
        for i, o in cm["alias"].items():
            alias[len(ins) + i] = len(outs) + o
        ins += cm["ins"]
        outs += cm["outs"]
        sems += cm["sems"]

    def run(which):
        def f(i_refs, o_refs, s_refs):
            for cm, (a, b, c_) in zip(comms, spans):
                cm[which](
                    i_refs[a : a + len(cm["ins"])], o_refs[b : b + len(cm["outs"])], s_refs[c_ : c_ + len(cm["sems"])]
                )

        return f

    return dict(ins=ins, outs=outs, alias=alias, sems=sems, start=run("start"), finish=run("finish"))


def _call(body, name, grid, in_specs, args, out_specs, out_shape, scratch=(), sem=None, comm=None, alias=None):
    n_in, n_out, n_s = len(args), len(out_shape), len(scratch)
    alias = dict(alias or {})
    if comm is None:
        res = pl.pallas_call(
            body, name=name, grid=grid, in_specs=list(in_specs), out_specs=list(out_specs), out_shape=list(out_shape),
            scratch_shapes=list(scratch), input_output_aliases=alias, compiler_params=_params(sem),
        )(*args)
        return list(res), []
    n_ci, n_co = len(comm["ins"]), len(comm["outs"])

    def wrapped(*refs):
        ins = refs[:n_in]
        ci = refs[n_in : n_in + n_ci]
        o0 = n_in + n_ci
        outs = refs[o0 : o0 + n_out]
        co = refs[o0 + n_out : o0 + n_out + n_co]
        s0 = o0 + n_out + n_co
        sc = refs[s0 : s0 + n_s]
        cs = refs[s0 + n_s :]
        pl.when(_edge(grid, False))(lambda: comm["start"](ci, co, cs))
        body(*ins, *outs, *sc)
        pl.when(_edge(grid, True))(lambda: comm["finish"](ci, co, cs))

    hbm = pl.BlockSpec(memory_space=pl.ANY)
    res = pl.pallas_call(
        wrapped,
        name=name,
        grid=grid,
        in_specs=list(in_specs) + [hbm] * n_ci,
        out_specs=list(out_specs) + [hbm] * n_co,
        out_shape=list(out_shape) + list(comm["outs"]),
        scratch_shapes=list(scratch) + list(comm["sems"]),
        input_output_aliases={**alias, **{n_in + i: n_out + o for i, o in comm["alias"].items()}},
        compiler_params=_params(("arbitrary",) * len(grid)),
    )(*args, *comm["ins"])
    return list(res[:n_out]), list(res[n_out:])


def _col_chunks(n, pref):
    widths = [pref] * (n // pref)
    rest = n - pref * len(widths)
    if rest == 0 and len(widths) > 1:
        rest = widths.pop()
    while rest > 0:
        w = 128 if rest > 128 else rest
        if rest > 256:
            w = rest // 256 * 128
        widths.append(w)
        rest -= w
    out, c0 = [], 0
    for w in widths:
        out.append((c0, w))
        c0 += w
    return out


def _fmm(name, grid, operands, terms, acc_shapes, extras, outs, epilogue, comm=None, csplit=None, into=None):
    n_p, n_e, n_o, n_a = len(operands), len(extras), len(outs), len(acc_shapes)
    nk = grid[-1]
    kax = len(grid) - 1
    simple = nk == 1 and all(t[4] is None for t in terms)
    alias = None
    if into is not None:
        extras = list(extras) + [(into, pl.BlockSpec(memory_space=pl.ANY))]
        alias = {n_p + n_e: 0}
        n_e += 1
    if csplit is not None:
        assert simple and into is None and all(t[2] in (NN, NT) and (len(t) <= 5 or not t[5]) for t in terms)
        tn_ = acc_shapes[0][1]
        chunks = _col_chunks(tn_, csplit)

    def dot(a, b, dims):
        if a.dtype != BF16:
            a = a.astype(BF16)
        if b.dtype != BF16:
            b = b.astype(BF16)
        return lax.dot_general(a, b, dims, preferred_element_type=F32)

    def value(refs, term):
        slabs = term[5] if len(term) > 5 else 0
        if not slabs:
            return dot(refs[term[0]][...], refs[term[1]][...], term[2])
        tot = None
        for sl in range(slabs):
            d = dot(refs[term[0]][sl], refs[term[1]][sl], term[2])
            tot = d if tot is None else tot + d
        return tot

    def always(refs):
        parts = [None] * n_a
        for term in terms:
            if term[4] is None:
                d = value(refs, term)
                parts[term[3]] = d if parts[term[3]] is None else parts[term[3]] + d
        return parts

    def chunked(refs, ex, os_, accs):
        cols = lambda ref, c0, w: ref.at[:, pl.ds(c0, w)] if ref.shape[-1] == tn_ else ref

        def dots(k):
            c0, w = chunks[k]
            parts = [None] * n_a
            for term in terms:
                b_ref = refs[term[1]]
                b = b_ref[:, pl.ds(c0, w)] if term[2] == NN else b_ref[pl.ds(c0, w), :]
                d = dot(refs[term[0]][...], b, term[2])
                parts[term[3]] = d if parts[term[3]] is None else parts[term[3]] + d
            for ai in range(n_a):
                accs[ai][k % 2, :, pl.ds(0, w)] = parts[ai]

        def finish(k):
            c0, w = chunks[k]
            vals = [accs[ai][k % 2, :, pl.ds(0, w)] for ai in range(n_a)]
            epilogue(vals, [cols(e, c0, w) for e in ex], [cols(o, c0, w) for o in os_])

        dots(0)
        for k in range(1, len(chunks)):
            dots(k)
            finish(k - 1)
        finish(len(chunks) - 1)

    def body(*refs):
        ex = refs[n_p : n_p + n_e]
        os_ = refs[n_p + n_e : n_p + n_e + n_o]
        accs = refs[n_p + n_e + n_o :]
        if simple and csplit is not None:
            chunked(refs, ex, os_, accs)
            return
        if simple:
            epilogue(always(refs), ex, os_)
            return
        ids = [pl.program_id(ax) for ax in range(len(grid))]
        k = ids[kax]

        @pl.when(k == 0)
        def _():
            for acc in accs:
                acc[...] = jnp.zeros(acc.shape, F32)

        for ai, part in enumerate(always(refs)):
            if part is not None:
                accs[ai][...] += part
        for term in terms:
            if term[4] is not None:

                def add(term=term):
                    accs[term[3]][...] += value(refs, term)

                pl.when(term[4](ids))(add)

        @pl.when(k == nk - 1)
        def _():
            epilogue([acc[...] for acc in accs], ex, os_)

    return _call(
        body,
        name,
        grid,
        [o[1] for o in operands] + [e[1] for e in extras],
        [o[0] for o in operands] + [e[0] for e in extras],
        [o[1] for o in outs],
        [o[0] for o in outs],
        scratch=[pltpu.VMEM((2, s[0], csplit), F32) for s in acc_shapes] if csplit is not None
        else [] if simple else [pltpu.VMEM(s, F32) for s in acc_shapes],
        sem=("parallel",) * kax + ("arbitrary",),
        comm=comm,
        alias=alias,
    )


def _rms_bwd(dn_raw, h, g):
    r = lax.rsqrt(jnp.mean(h * h, axis=-1, keepdims=True) + EPS)
    hn = h * r
    dg = jnp.sum(dn_raw * hn, axis=0, keepdims=True)
    dn = dn_raw * g
    dh = r * (dn - hn * jnp.mean(dn * hn, axis=-1, keepdims=True))
    return dh, dg


def _rms_fwd(name, h, g):
    s, d = h.shape
    ts = _tile(s, 512)

    def body(h_ref, g_ref, o_ref):
        x = h_ref[...]
        r = lax.rsqrt(jnp.mean(x * x, axis=-1, keepdims=True) + EPS)
        o_ref[...] = (x * r * g_ref[...]).astype(BF16)

    return pl.pallas_call(
        body,
        name=name,
        grid=(s // ts,),
        in_specs=[pl.BlockSpec((ts, d), lambda i: (i, 0)), pl.BlockSpec((1, d), lambda i: (0, 0))],
        out_specs=pl.BlockSpec((ts, d), lambda i: (i, 0)),
        out_shape=SDS((s, d), BF16),
        compiler_params=_params(("parallel",)),
    )(h, g)


def _norm_bwd(name, dn, h, dres, g, want_bf16):
    s, d = h.shape
    ts = _tile(s, 512)

    def body(dn_r, h_r, dres_r, g_r, *outs):
        dh, dg = _rms_bwd(dn_r[...].astype(F32), h_r[...], g_r[...])
        dh = dres_r[...] + dh
        outs[0][...] = dh
        if want_bf16:
            outs[1][...] = dh.astype(BF16)
        outs[-1][...] = dg

    blk = pl.BlockSpec((ts, d), lambda i: (i, 0))
    part = pl.BlockSpec((None, 1, d), lambda i: (i, 0, 0))
    return pl.pallas_call(
        body,
        name=name,
        grid=(s // ts,),
        in_specs=[blk, blk, blk, pl.BlockSpec((1, d), lambda i: (0, 0))],
        out_specs=[blk] + ([blk] if want_bf16 else []) + [part],
        out_shape=[SDS((s, d), F32)] + ([SDS((s, d), BF16)] if want_bf16 else []) + [SDS((s // ts, 1, d), F32)],
        compiler_params=_params(("parallel",)),
    )(dn, h, dres, g)


def _prev_halo(ts, hb):
    r = ts // hb
    return lambda i: jnp.maximum(i * r - 1, 0)


def _next_halo(ts, hb, s):
    r = ts // hb
    last = s // hb - 1
    return lambda i: jnp.minimum((i + 1) * r, last)


def _shift_copies(buf, sh):
    n = sh.shape[1]
    for j in range(1, 8):
        sh[j - 1, pl.ds(0, n), :] = buf[pl.ds(j, n), :]


def _tap(buf, sh, r0, off, rows):
    j = off % 8
    start = pl.multiple_of(r0 + (off - j), 8)
    if j == 0:
        return buf[pl.ds(start, rows), :]
    return sh[j - 1, pl.ds(start, rows), :]


def _mix_a_fwd(proj, wa, s, c):
    ts, hb = _tile(s, 256), HALO_A
    prev = _prev_halo(ts, hb)

    def body(ah, ab, ac, hh, hc, w, o, buf):
        i = pl.program_id(0)
        zh = hc[...].astype(F32) * hh[...].astype(F32)
        buf[pl.ds(0, hb), :] = jnp.where(i == 0, 0.0, zh)
        buf[pl.ds(hb, ts), :] = ac[...].astype(F32) * ah[...].astype(F32)
        for r0 in range(0, ts, CONV_ROWS):
            cz = jnp.zeros((CONV_ROWS, c), F32)
            for k in range(CONV_A_K):
                cz = cz + w[k : k + 1, :] * buf[pl.ds(hb + r0 - (CONV_A_K - 1) + k, CONV_ROWS), :]
            o[pl.ds(r0, CONV_ROWS), :] = (ab[pl.ds(r0, CONV_ROWS), :].astype(F32) * cz).astype(BF16)

    main = lambda cb: pl.BlockSpec((ts, c), lambda i: (i, cb))
    halo = lambda cb: pl.BlockSpec((hb, c), lambda i: (prev(i), cb))
    return pl.pallas_call(
        body,
        name="mix_a_fwd",
        grid=(s // ts,),
        in_specs=[main(0), main(1), main(2), halo(0), halo(2), pl.BlockSpec(wa.shape, lambda i: (0, 0))],
        out_specs=pl.BlockSpec((ts, c), lambda i: (i, 0)),
        out_shape=SDS((s, c), BF16),
        scratch_shapes=[pltpu.VMEM((hb + ts, c), F32)],
        compiler_params=_params(("parallel",)),
    )(proj, proj, proj, proj, proj, wa)


def _mix_b_fwd(proj, b_glu, wd, bd, lg, lb, s, c, comm=None):
    ts, hb = _tile(s, 256), HALO_B
    prev = _prev_halo(ts, hb)

    def body(gv, gg, hv, hg, bglu, w, bd_r, lg_r, lb_r, v_o, u_o, cv_o, buf, sh):
        i = pl.program_id(0)
        bv, bg = bglu[:, 0:c], bglu[:, c : 2 * c]
        uh = (hv[...].astype(F32) + bv) * _sigmoid(hg[...].astype(F32) + bg)
        buf[pl.ds(0, hb), :] = jnp.where(i == 0, 0.0, uh)
        u = (gv[...].astype(F32) + bv) * _sigmoid(gg[...].astype(F32) + bg)
        buf[pl.ds(hb, ts), :] = u
        u_o[...] = u.astype(BF16)
        _shift_copies(buf, sh)

        def chunk(ci, carry):
            r0 = pl.multiple_of(ci * CONF_ROWS, CONF_ROWS)
            acc = jnp.zeros((CONF_ROWS, c), F32)
            for k in range(CONF_K):
                acc = acc + w[k : k + 1, :] * _tap(buf, sh, r0, hb - (CONF_K - 1) + k, CONF_ROWS)
            cv_o[pl.ds(r0, CONF_ROWS), :] = acc + bd_r[...]
            return carry

        lax.fori_loop(0, ts // CONF_ROWS, chunk, 0)
        cv = cv_o[...]
        mu = jnp.mean(cv, axis=-1, keepdims=True)
        xc = cv - mu
        rs = lax.rsqrt(jnp.mean(xc * xc, axis=-1, keepdims=True) + LN_EPS)
        ln = xc * rs * lg_r[...] + lb_r[...]
        v_o[...] = (ln * _sigmoid(ln)).astype(BF16)

    main = lambda cb: pl.BlockSpec((ts, c), lambda i: (i, cb))
    halo = lambda cb: pl.BlockSpec((hb, c), lambda i: (prev(i), cb))
    full = lambda a: pl.BlockSpec(a.shape, lambda i: (0, 0))
    out = pl.BlockSpec((ts, c), lambda i: (i, 0))
    return _call(
        body,
        "mix_b_fwd",
        (s // ts,),
        [main(3), main(4), halo(3), halo(4), full(b_glu), full(wd), full(bd), full(lg), full(lb)],
        [proj, proj, proj, proj, b_glu, wd, bd, lg, lb],
        [out, out, out],
        [SDS((s, c), BF16), SDS((s, c), BF16), SDS((s, c), F32)],
        scratch=[pltpu.VMEM((hb + ts, c), F32), pltpu.VMEM((7, hb + ts - 8, c), F32)],
        sem=("parallel",),
        comm=comm,
    )


def _mix_b_bwd1(d_v, cv, lg, lb, s, c):
    ts = _tile(s, 256)

    def body(dv_r, cv_r, lg_r, lb_r, dcv_o, part_o):
        cv_ = cv_r[...]
        mu = jnp.mean(cv_, axis=-1, keepdims=True)
        xc = cv_ - mu
        rs = lax.rsqrt(jnp.mean(xc * xc, axis=-1, keepdims=True) + LN_EPS)
        xh = xc * rs
        ln = xh * lg_r[...] + lb_r[...]
        sg = _sigmoid(ln)
        d_ln = dv_r[...].astype(F32) * (sg * (1.0 + ln * (1.0 - sg)))
        dy = d_ln * lg_r[...]
        d_cv = rs * (dy - jnp.mean(dy, axis=-1, keepdims=True) - xh * jnp.mean(dy * xh, axis=-1, keepdims=True))
        dcv_o[...] = d_cv
        part_o[0:1, :] = jnp.sum(d_ln * xh, axis=0, keepdims=True)
        part_o[1:2, :] = jnp.sum(d_ln, axis=0, keepdims=True)
        part_o[2:3, :] = jnp.sum(d_cv, axis=0, keepdims=True)

    blk = pl.BlockSpec((ts, c), lambda i: (i, 0))
    full = lambda a: pl.BlockSpec(a.shape, lambda i: (0, 0))
    return pl.pallas_call(
        body,
        name="mix_b_bwd_ln",
        grid=(s // ts,),
        in_specs=[blk, blk, full(lg), full(lb)],
        out_specs=[blk, pl.BlockSpec((None, 3, c), lambda i: (i, 0, 0))],
        out_shape=[SDS((s, c), F32), SDS((s // ts, 3, c), F32)],
        compiler_params=_params(("parallel",)),
    )(d_v, cv, lg, lb)


def _mix_b_bwd2(d_cv, u, proj, b_glu, wd, s, c, comm=None):
    ts, hb = _tile(s, 256), HALO_B
    prev, nxt = _prev_halo(ts, hb), _next_halo(ts, hb, s)
    n_t = s // ts
    kp = wd.shape[0]

    def body(dcv, dcv_n, u_m, u_p, gv, gg, bglu, w, d_o, dwd_o, dbglu_o, dbuf, ubuf, dub, dsh, ush, dwacc):
        i = pl.program_id(0)
        dbuf[pl.ds(0, ts), :] = dcv[...]
        dbuf[pl.ds(ts, hb), :] = jnp.where(i == n_t - 1, 0.0, dcv_n[...])
        ubuf[pl.ds(0, hb), :] = jnp.where(i == 0, 0.0, u_p[...].astype(F32))
        ubuf[pl.ds(hb, ts), :] = u_m[...].astype(F32)
        _shift_copies(dbuf, dsh)
        _shift_copies(ubuf, ush)
        dwacc[...] = jnp.zeros(dwacc.shape, F32)

        def chunk(ci, carry):
            r0 = pl.multiple_of(ci * CONF_ROWS, CONF_ROWS)
            acc = jnp.zeros((CONF_ROWS, c), F32)
            dc = dbuf[pl.ds(r0, CONF_ROWS), :]
            for k in range(CONF_K):
                acc = acc + w[k : k + 1, :] * _tap(dbuf, dsh, r0, (CONF_K - 1) - k, CONF_ROWS)
                prod = dc * _tap(ubuf, ush, r0, hb - (CONF_K - 1) + k, CONF_ROWS)
                fold = prod[0:8]
                for a in range(1, CONF_ROWS // 8):
                    fold = fold + prod[8 * a : 8 * a + 8]
                dwacc[pl.ds(8 * k, 8), :] += fold
            dub[pl.ds(r0, CONF_ROWS), :] = acc
            return carry

        lax.fori_loop(0, ts // CONF_ROWS, chunk, 0)
        for k in range(CONF_K):
            dwd_o[k : k + 1, :] = jnp.sum(dwacc[pl.ds(8 * k, 8), :], axis=0, keepdims=True)
        dwd_o[CONF_K:kp, :] = jnp.zeros((kp - CONF_K, c), F32)
        bv, bg = bglu[:, 0:c], bglu[:, c : 2 * c]
        d_u = dub[...]
        sg = _sigmoid(gg[...].astype(F32) + bg)
        d_gv = d_u * sg
        d_gg = d_u * (gv[...].astype(F32) + bv) * sg * (1.0 - sg)
        d_o[:, 0:c] = d_gv.astype(BF16)
        d_o[:, c : 2 * c] = d_gg.astype(BF16)
        dbglu_o[:, 0:c] = jnp.sum(d_gv, axis=0, keepdims=True)
        dbglu_o[:, c : 2 * c] = jnp.sum(d_gg, axis=0, keepdims=True)

    blk = lambda cb: pl.BlockSpec((ts, c), lambda i: (i, cb))
    full = lambda a: pl.BlockSpec(a.shape, lambda i: (0, 0))
    return _call(
        body,
        "mix_b_bwd_conv",
        (n_t,),
        [
            blk(0),
            pl.BlockSpec((hb, c), lambda i: (nxt(i), 0)),
            blk(0),
            pl.BlockSpec((hb, c), lambda i: (prev(i), 0)),
            blk(3),
            blk(4),
            full(b_glu),
            full(wd),
        ],
        [d_cv, d_cv, u, u, proj, proj, b_glu, wd],
        [
            pl.BlockSpec((ts, 2 * c), lambda i: (i, 0)),
            pl.BlockSpec((None, kp, c), lambda i: (i, 0, 0)),
            pl.BlockSpec((None, 1, 2 * c), lambda i: (i, 0, 0)),
        ],
        [SDS((s, 2 * c), BF16), SDS((n_t, kp, c), F32), SDS((n_t, 1, 2 * c), F32)],
        scratch=[
            pltpu.VMEM((ts + hb, c), F32), pltpu.VMEM((hb + ts, c), F32), pltpu.VMEM((ts, c), F32),
            pltpu.VMEM((7, hb + ts - 8, c), F32), pltpu.VMEM((7, hb + ts - 8, c), F32), pltpu.VMEM((8 * CONF_K, c), F32),
        ],
        sem=("parallel",),
        comm=comm,
    )


def _mix_a_bwd(d_ya, proj, wa, s, c):
    ts, hb = _tile(s, 256), HALO_A
    prev, nxt = _prev_halo(ts, hb), _next_halo(ts, hb, s)
    n_t = s // ts
    kp = wa.shape[0]

    def body(dya, dya_n, ah, ab, ac, ah_p, ac_p, ab_n, w, d_o, dwa_o, zbuf, dbuf, dzb):
        i = pl.program_id(0)
        zbuf[pl.ds(0, hb), :] = jnp.where(i == 0, 0.0, ac_p[...].astype(F32) * ah_p[...].astype(F32))
        zbuf[pl.ds(hb, ts), :] = ac[...].astype(F32) * ah[...].astype(F32)
        dbuf[pl.ds(0, ts), :] = dya[...].astype(F32) * ab[...].astype(F32)
        dbuf[pl.ds(ts, hb), :] = jnp.where(i == n_t - 1, 0.0, dya_n[...].astype(F32) * ab_n[...].astype(F32))
        dw_rows = [jnp.zeros((1, c), F32) for _ in range(CONV_A_K)]
        for r0 in range(0, ts, CONV_ROWS):
            cz = jnp.zeros((CONV_ROWS, c), F32)
            dz = jnp.zeros((CONV_ROWS, c), F32)
            dc = dbuf[pl.ds(r0, CONV_ROWS), :]
            for k in range(CONV_A_K):
                zk = zbuf[pl.ds(hb + r0 - (CONV_A_K - 1) + k, CONV_ROWS), :]
                cz = cz + w[k : k + 1, :] * zk
                dz = dz + w[k : k + 1, :] * dbuf[pl.ds(r0 + (CONV_A_K - 1) - k, CONV_ROWS), :]
                dw_rows[k] = dw_rows[k] + jnp.sum(dc * zk, axis=0, keepdims=True)
            d_o[pl.ds(r0, CONV_ROWS), c : 2 * c] = (dya[pl.ds(r0, CONV_ROWS), :].astype(F32) * cz).astype(BF16)
            dzb[pl.ds(r0, CONV_ROWS), :] = dz
        d_z = dzb[...]
        d_o[:, 0:c] = (d_z * ac[...].astype(F32)).astype(BF16)
        d_o[:, 2 * c : 3 * c] = (d_z * ah[...].astype(F32)).astype(BF16)
        for k in range(CONV_A_K):
            dwa_o[k : k + 1, :] = dw_rows[k]
        dwa_o[CONV_A_K:kp, :] = jnp.zeros((kp - CONV_A_K, c), F32)

    blk = lambda cb: pl.BlockSpec((ts, c), lambda i: (i, cb))
    hp = lambda cb: pl.BlockSpec((hb, c), lambda i: (prev(i), cb))
    hn = lambda cb: pl.BlockSpec((hb, c), lambda i: (nxt(i), cb))
    return pl.pallas_call(
        body,
        name="mix_a_bwd",
        grid=(n_t,),
        in_specs=[blk(0), hn(0), blk(0), blk(1), blk(2), hp(0), hp(2), hn(1), pl.BlockSpec(wa.shape, lambda i: (0, 0))],
        out_specs=[pl.BlockSpec((ts, 3 * c), lambda i: (i, 0)), pl.BlockSpec((None, kp, c), lambda i: (i, 0, 0))],
        out_shape=[SDS((s, 3 * c), BF16), SDS((n_t, kp, c), F32)],
        scratch_shapes=[pltpu.VMEM((hb + ts, c), F32), pltpu.VMEM((ts + hb, c), F32), pltpu.VMEM((ts, c), F32)],
        compiler_params=_params(("parallel",)),
    )(d_ya, d_ya, proj, proj, proj, proj, proj, proj, wa)


def _ep_bf16(accs, ex, os_):
    os_[0][...] = accs[0].astype(BF16)


def _mm_tn(name, a, b, tm=2048, tn=1024, tk=1024):
    m, k1 = a.shape
    n = b.shape[1]
    tm, tn, tk = _tile(k1, tm), _tile(n, tn), _tile(m, tk)
    return _fmm(
        name,
        (k1 // tm, n // tn, m // tk),
        [(a, pl.BlockSpec((tk, tm), lambda i, j, k: (k, i))), (b, pl.BlockSpec((tk, tn), lambda i, j, k: (k, j)))],
        [(0, 1, TN, 0, None)],
        [(tm, tn)],
        [],
        [(SDS((k1, n), BF16), pl.BlockSpec((tm, tn), lambda i, j, k: (i, j)))],
        _ep_bf16,
    )[0][0]


def _mm_nt(name, a, b, tm=1024, tn=1024, comm=None):
    m, kk = a.shape
    n = b.shape[0]
    tm, tn = _tile(m, tm), _tile(n, tn)
    outs, couts = _fmm(
        name,
        (m // tm, n // tn, 1),
        [(a, pl.BlockSpec((tm, kk), lambda i, j, k: (i, 0))), (b, pl.BlockSpec((tn, kk), lambda i, j, k: (j, 0)))],
        [(0, 1, NT, 0, None)],
        [(tm, tn)],
        [],
        [(SDS((m, n), BF16), pl.BlockSpec((tm, tn), lambda i, j, k: (i, j)))],
        _ep_bf16,
        comm=comm,
    )
    return outs[0], couts


def _dev_index(dev):
    return 4 * dev[0] + 2 * dev[1] + dev[2]


def _region(ref, kind, j, shard_shape):
    if kind == "col":
        ns = shard_shape[1]
        return ref.at[:, pl.ds(pl.multiple_of(j * ns, 128), ns)]
    if kind == "row":
        rs = shard_shape[0]
        return ref.at[pl.ds(pl.multiple_of(j * rs, 8), rs), :]
    return ref.at[j]


def _whole_shape(kind, shard_shape):
    if kind == "col":
        return (shard_shape[0], NDEV * shard_shape[1])
    if kind == "row":
        return (NDEV * shard_shape[0], shard_shape[1])
    return (NDEV,) + tuple(shard_shape)


def _place():
    return lax.axis_index("x"), lax.axis_index("y"), lax.axis_index("c")


def _proj_gather(n1, w_shard, early, late):
    s, d = n1.shape
    ns = w_shard.shape[1]
    pw = 2 * ns
    tm = _tile(s // 2, 512)
    n_i = s // tm
    comm = _join(early, late)
    n_early = (len(early["ins"]), len(early["outs"]), len(early["sems"]))
    assert n_i >= 2 and not comm["alias"]
    x0, y0, _ = _place()
    order = jnp.stack([2 * x0 + y0, 2 * x0 + (1 - y0), 2 * (1 - x0) + y0, 2 * (1 - x0) + (1 - y0)]).astype(jnp.int32)
    n_ci, n_co = len(comm["ins"]), len(comm["outs"])

    def body(order_ref, n1_ref, wsh_ref, *rest):
        ci = rest[:n_ci]
        proj_ref, win_ref = rest[n_ci], rest[n_ci + 1]
        co = rest[n_ci + 2 : n_ci + 2 + n_co]
        wfull, send, recv, fsend, frecv, loc, osem = rest[n_ci + 2 + n_co : n_ci + 9 + n_co]
        cs = rest[n_ci + 9 + n_co :]
        u, i = pl.program_id(0), pl.program_id(1)
        x, y, c = _place()
        sib = (x, y, 1 - c)
        chips = [(x, y), (x, 1 - y), (1 - x, y), (1 - x, 1 - y)]
        peers = [sib] + [(*ch, c) for ch in chips[1:]]
        blk = lambda ch, core: wfull.at[2 * ch[0] + ch[1], :, pl.ds(pl.multiple_of(core * ns, 128), ns)]
        sends = [_remote(blk(chips[0], c), blk(chips[0], c), send.at[k], recv.at[k], peers[k]) for k in range(4)]
        arrivals = [_remote(blk(chips[0], 1 - c), blk(chips[0], 1 - c), send.at[0], recv.at[0], sib)] + [
            _remote(blk(chips[k], c), blk(chips[k], c), send.at[k], recv.at[k], peers[k]) for k in range(1, 4)
        ]
        passes = [_remote(blk(chips[k], c), blk(chips[k], c), fsend.at[k - 1], frecv.at[k - 1], sib) for k in range(1, 4)]
        passed = [_remote(blk(chips[k], 1 - c), blk(chips[k], 1 - c), fsend.at[k - 1], frecv.at[k - 1], sib) for k in range(1, 4)]
        mine = lambda: pltpu.make_async_copy(wsh_ref, blk(chips[0], c), loc.at[0])

        def to_hbm(unit):
            q = order_ref[unit]
            return pltpu.make_async_copy(wfull.at[q], win_ref.at[:, pl.ds(pl.multiple_of(q * pw, 128), pw)], osem.at[unit])

        a, b, e = n_early
        early_refs = (ci[:a], co[:b], cs[:e])
        late_refs = (ci[a:], co[b:], cs[e:])

        @pl.when(jnp.logical_and(u == 0, i == 0))
        def _():
            mine().start()
            mine().wait()
            for snd in sends[:3]:
                snd().start()
            early["start"](*early_refs)
            arrivals[0]().wait_recv()

        @pl.when(jnp.logical_and(u == 1, i == 0))
        def _():
            sends[3]().start()

        @pl.when(jnp.logical_and(u == 2, i == 0))
        def _():
            late["start"](*late_refs)

        for nxt in range(1, 4):

            @pl.when(jnp.logical_and(u == nxt - 1, i == n_i - 1))
            def _(nxt=nxt):
                passed[nxt - 1]().wait_recv()

        proj_ref[...] = jnp.dot(n1_ref[...], wfull[order_ref[u]], preferred_element_type=F32).astype(BF16)

        for nxt in range(1, 4):

            @pl.when(jnp.logical_and(u == nxt - 1, i == n_i - 2))
            def _(nxt=nxt):
                arrivals[nxt]().wait_recv()
                passes[nxt - 1]().start()

        for unit in range(4):

            @pl.when(jnp.logical_and(u == unit, i == n_i - 1))
            def _(unit=unit):
                to_hbm(unit).start()

        @pl.when(jnp.logical_and(u == 3, i == n_i - 1))
        def _():
            for snd in sends + passes:
                snd().wait_send()
            for unit in range(4):
                to_hbm(unit).wait()
            comm["finish"](ci, co, cs)

    hbm = pl.BlockSpec(memory_space=pl.ANY)
    dma = pltpu.SemaphoreType.DMA
    res = pl.pallas_call(
        body,
        name="proj",
        grid_spec=pltpu.PrefetchScalarGridSpec(
            num_scalar_prefetch=1,
            grid=(4, n_i),
            in_specs=[pl.BlockSpec((tm, d), lambda u, i, order_ref: (i, 0)), hbm] + [hbm] * n_ci,
            out_specs=[pl.BlockSpec((tm, pw), lambda u, i, order_ref: (i, order_ref[u])), hbm] + [hbm] * n_co,
            scratch_shapes=[pltpu.VMEM((4, d, pw), BF16), dma((4,)), dma((4,)), dma((3,)), dma((3,)), dma((1,)), dma((4,))]
            + list(comm["sems"]),
        ),
        out_shape=[SDS((s, NDEV * ns), BF16), SDS((d, NDEV * ns), BF16)] + list(comm["outs"]),
        compiler_params=_params(("arbitrary", "arbitrary")),
    )(order, n1, w_shard, *comm["ins"])
    return res[0], res[1], list(res[2:])


def _peer(me, r):
    x, y, c = me
    return (1 - x if r & 4 else x, 1 - y if r & 2 else y, 1 - c if r & 1 else c)


def _remote(src, dst, send_sem, recv_sem, to):
    return lambda: pltpu.make_async_remote_copy(
        src_ref=src, dst_ref=dst, send_sem=send_sem, recv_sem=recv_sem, device_id=to, device_id_type=MESH
    )


def _run(pairs, locals_, start):
    if start:
        for cp in locals_:
            cp.start()
        for snd, _ in pairs:
            snd().start()
    else:
        for snd, arr in pairs:
            arr().wait_recv()
            snd().wait_send()
        for cp in locals_:
            cp.wait()


def _stage(ins, outs, alias, sems, build):
    return dict(
        ins=list(ins), outs=list(outs), alias=alias, sems=list(sems),
        start=lambda i, o, s: _run(*build(i, o, s), True),
        finish=lambda i, o, s: _run(*build(i, o, s), False),
    )


def _ag1(shards, kinds):
    n_t = len(shards)
    shapes = [tuple(sh.shape) for sh in shards]

    def build(srcs, dsts, sems):
        send, recv, loc = sems
        x, y, c = _place()
        me = (x, y, c)
        peers = [(x, y, 1 - c), (1 - x, y, c), (x, 1 - y, c), (1 - x, 1 - y, c)]
        reg = lambda t, dev: _region(dsts[t], kinds[t], _dev_index(dev), shapes[t])
        pairs = []
        for t in range(n_t):
            for k, peer in enumerate(peers):
                snd = _remote(srcs[t], reg(t, me), send.at[t, k], recv.at[t, k], peer)
                arr = _remote(reg(t, peer), reg(t, peer), send.at[t, k], recv.at[t, k], peer)
                pairs.append((snd, arr))
        mine = [pltpu.make_async_copy(srcs[t], reg(t, me), loc.at[t]) for t in range(n_t)]
        return pairs, mine

    outs = [SDS(_whole_shape(kinds[t], shapes[t]), shards[t].dtype) for t in range(n_t)]
    dma = pltpu.SemaphoreType.DMA
    return _stage(shards, outs, {}, [dma((n_t, 4)), dma((n_t, 4)), dma((n_t,))], build)


def _ag_direct(shards, kinds):
    n_t = len(shards)
    shapes = [tuple(sh.shape) for sh in shards]

    def build(srcs, dsts, sems):
        send, recv, loc = sems
        me = _place()
        reg = lambda t, dev: _region(dsts[t], kinds[t], _dev_index(dev), shapes[t])
        pairs = []
        for t in range(n_t):
            for r in range(1, NDEV):
                peer = _peer(me, r)
                snd = _remote(srcs[t], reg(t, me), send.at[t, r - 1], recv.at[t, r - 1], peer)
                arr = _remote(reg(t, peer), reg(t, peer), send.at[t, r - 1], recv.at[t, r - 1], peer)
                pairs.append((snd, arr))
        mine = [pltpu.make_async_copy(srcs[t], reg(t, me), loc.at[t]) for t in range(n_t)]
        return pairs, mine

    outs = [SDS(_whole_shape(kinds[t], shapes[t]), shards[t].dtype) for t in range(n_t)]
    dma = pltpu.SemaphoreType.DMA
    return _stage(shards, outs, {}, [dma((n_t, 7)), dma((n_t, 7)), dma((n_t,))], build)


def _ag2(wholes, kinds, shapes):
    n_t = len(wholes)

    def build(_, dsts, sems):
        send, recv = sems
        x, y, c = _place()
        sib = (x, y, 1 - c)
        chips = [(1 - x, y), (x, 1 - y), (1 - x, 1 - y)]
        reg = lambda t, dev: _region(dsts[t], kinds[t], _dev_index(dev), shapes[t])
        pairs = []
        for t in range(n_t):
            for j, chip in enumerate(chips):
                snd = _remote(reg(t, (*chip, c)), reg(t, (*chip, c)), send.at[t, j], recv.at[t, j], sib)
                arr = _remote(reg(t, (*chip, 1 - c)), reg(t, (*chip, 1 - c)), send.at[t, j], recv.at[t, j], sib)
                pairs.append((snd, arr))
        return pairs, []

    outs = [SDS(w.shape, w.dtype) for w in wholes]
    dma = pltpu.SemaphoreType.DMA
    return _stage(wholes, outs, {t: t for t in range(n_t)}, [dma((n_t, 3)), dma((n_t, 3))], build)


def _chip_of(q):
    return (q >> 1, q & 1)


def _rs1(wholes, kinds, shapes):
    n_t = len(wholes)

    def build(srcs, outs, sems):
        send, recv = sems
        x, y, c = _place()
        sib = (x, y, 1 - c)
        pairs = []
        for t in range(n_t):
            for q in range(4):
                theirs = _region(srcs[t], kinds[t], _dev_index((*_chip_of(q), 1 - c)), shapes[t])
                pairs.append((
                    _remote(theirs, outs[t].at[q], send.at[t, q], recv.at[t, q], sib),
                    _remote(outs[t].at[q], outs[t].at[q], send.at[t, q], recv.at[t, q], sib),
                ))
        return pairs, []

    slabs = [SDS((4,) + tuple(shapes[t]), wholes[t].dtype) for t in range(n_t)]
    dma = pltpu.SemaphoreType.DMA
    return _stage(wholes, slabs, {}, [dma((n_t, 4)), dma((n_t, 4))], build)


def _rs2(pair_sums):
    n_t = len(pair_sums)

    def build(srcs, lands, sems):
        send, recv, loc = sems
        x, y, c = _place()
        my_chip = 2 * x + y
        pairs, mine = [], []
        for t in range(n_t):
            for j, (px, py) in enumerate([(1 - x, y), (x, 1 - y), (1 - x, 1 - y)]):
                q = 2 * px + py
                pairs.append((
                    _remote(srcs[t].at[q], lands[t].at[my_chip], send.at[t, j], recv.at[t, j], (px, py, c)),
                    _remote(lands[t].at[q], lands[t].at[q], send.at[t, j], recv.at[t, j], (px, py, c)),
                ))
            mine.append(pltpu.make_async_copy(srcs[t].at[my_chip], lands[t].at[my_chip], loc.at[t]))
        return pairs, mine

    outs = [SDS(q.shape, q.dtype) for q in pair_sums]
    dma = pltpu.SemaphoreType.DMA
    return _stage(pair_sums, outs, {}, [dma((n_t, 3)), dma((n_t, 3)), dma((n_t,))], build)


def _pair_sum(name, whole, kind, got):
    _, rows, cols = got.shape
    tr = _tile(rows, 256)
    n_r = rows // tr
    core = lax.axis_index("c").astype(jnp.int32).reshape(1)

    def body(_, a, b, o):
        o[...] = (a[...].astype(F32) + b[...].astype(F32)).astype(BF16)

    if kind == "col":
        own = pl.BlockSpec((tr, cols), lambda q, i, c_ref: (i, 2 * q + c_ref[0]))
    elif kind == "row":
        own = pl.BlockSpec((tr, cols), lambda q, i, c_ref: ((2 * q + c_ref[0]) * n_r + i, 0))
    else:
        own = pl.BlockSpec((None, tr, cols), lambda q, i, c_ref: (2 * q + c_ref[0], i, 0))
    slab = pl.BlockSpec((None, tr, cols), lambda q, i, c_ref: (q, i, 0))
    return pl.pallas_call(
        body,
        name=name,
        grid_spec=pltpu.PrefetchScalarGridSpec(
            num_scalar_prefetch=1, grid=(4, n_r), in_specs=[own, slab], out_specs=slab
        ),
        out_shape=SDS(got.shape, BF16),
        compiler_params=_params(("parallel", "parallel")),
    )(core, whole, got)


def _all_reduce_small(part):
    r_, c_ = part.shape

    def body(src, land, total, send_sems, recv_sems):
        me = _place()
        my = _dev_index(me)
        land[my] = src[...]

        def copy(r):
            peer = _peer(me, r)
            return pltpu.make_async_remote_copy(
                src_ref=src,
                dst_ref=land.at[my],
                send_sem=send_sems.at[r - 1],
                recv_sem=recv_sems.at[r - 1],
                device_id=peer,
                device_id_type=MESH,
            )

        def arrival(r):
            peer = _peer(me, r)
            slab = land.at[_dev_index(peer)]
            return pltpu.make_async_remote_copy(
                src_ref=slab,
                dst_ref=slab,
                send_sem=send_sems.at[r - 1],
                recv_sem=recv_sems.at[r - 1],
                device_id=peer,
                device_id_type=MESH,
            )

        sends = [copy(r) for r in range(1, NDEV)]
        for cp in sends:
            cp.start()
        for r in range(1, NDEV):
            arrival(r).wait_recv()
        for cp in sends:
            cp.wait_send()
        acc = land[0]
        for d in range(1, NDEV):
            acc = acc + land[d]
        total[...] = acc

    vmem = pl.BlockSpec(memory_space=pltpu.VMEM)
    return pl.pallas_call(
        body,
        name="all_reduce_small",
        in_specs=[vmem],
        out_specs=[vmem, vmem],
        out_shape=[SDS((NDEV, r_, c_), F32), SDS((r_, c_), F32)],
        scratch_shapes=[pltpu.SemaphoreType.DMA((7,)), pltpu.SemaphoreType.DMA((7,))],
    )(part)[1]


def _adamw_math(g, w, m, v):
    m2 = ADAM_B1 * m + (1.0 - ADAM_B1) * g
    v2 = ADAM_B2 * v + (1.0 - ADAM_B2) * (g * g)
    m_hat = m2 / (1.0 - ADAM_B1**ADAM_STEP)
    v_hat = v2 / (1.0 - ADAM_B2**ADAM_STEP)
    delta = -ADAM_LR * (m_hat / (jnp.sqrt(v_hat) + ADAM_EPS) + ADAM_WD * w)
    return delta, m2, v2


def _adamw_big(name, land, w, m, v):
    rows, cols = w.shape
    tr = _tile(rows, 256)
    n_slab = land.shape[0]

    def body(l_ref, w_ref, m_ref, v_ref, g_o, d_o, m_o, v_o):
        g = l_ref[0].astype(F32)
        for d in range(1, n_slab):
            g = g + l_ref[d].astype(F32)
        delta, m2, v2 = _adamw_math(g, w_ref[...], m_ref[...], v_ref[...])
        g_o[...] = g
        d_o[...] = delta
        m_o[...] = m2
        v_o[...] = v2

    blk = pl.BlockSpec((tr, cols), lambda i: (i, 0))
    return pl.pallas_call(
        body,
        name=name,
        grid=(rows // tr,),
        in_specs=[pl.BlockSpec((n_slab, tr, cols), lambda i: (0, i, 0)), blk, blk, blk],
        out_specs=[blk] * 4,
        out_shape=[SDS((rows, cols), F32)] * 4,
        compiler_params=_params(("parallel",)),
    )(land, w, m, v)


def _adamw_small(g, w, m, v):
    def body(g_ref, w_ref, m_ref, v_ref, d_o, m_o, v_o):
        delta, m2, v2 = _adamw_math(g_ref[...], w_ref[...], m_ref[...], v_ref[...])
        d_o[...] = delta
        m_o[...] = m2
        v_o[...] = v2

    vmem = pl.BlockSpec(memory_space=pltpu.VMEM)
    return pl.pallas_call(
        body,
        name="adamw_small",
        in_specs=[vmem] * 4,
        out_specs=[vmem] * 3,
        out_shape=[SDS(g.shape, F32)] * 3,
    )(g, w, m, v)


def _pack(pieces, width):
    flat = jnp.concatenate([p.reshape(-1) for p in pieces])
    rows = -(-flat.shape[0] // (8 * width)) * 8
    flat = jnp.pad(flat, (0, rows * width - flat.shape[0]))
    return flat.reshape(rows, width)


def _unpack(packed, shapes):
    flat = packed.reshape(-1)
    out, off = [], 0
    for shp in shapes:
        n = 1
        for d in shp:
            n *= d
        out.append(flat[off : off + n].reshape(shp))
        off += n
    return out


def kernel(x, p, g_mix, w_in, conv_a_w, w_out_a, b_glu, conf_dw_w, conf_dw_b, conf_ln_g, conf_ln_b, w_pw_b, b_pw_b, w_o, g_ffn, w_gate, w_up, w_down, g_ple, w_ple_gate, w_ple_proj, g_final, loss_target, m_g_mix, m_w_in, m_conv_a_w, m_w_out_a, m_b_glu, m_conf_dw_w, m_conf_dw_b, m_conf_ln_g, m_conf_ln_b, m_w_pw_b, m_b_pw_b, m_w_o, m_g_ffn, m_w_gate, m_w_up, m_w_down, m_g_ple, m_w_ple_gate, m_w_ple_proj, m_g_final, v_g_mix, v_w_in, v_conv_a_w, v_w_out_a, v_b_glu, v_conf_dw_w, v_conf_dw_b, v_conf_ln_g, v_conf_ln_b, v_w_pw_b, v_b_pw_b, v_w_o, v_g_ffn, v_w_gate, v_w_up, v_w_down, v_g_ple, v_w_ple_gate, v_w_ple_proj, v_g_final):
    s, d = x.shape[1], x.shape[2]
    c = conf_ln_g.shape[-1]
    pdim = w_ple_proj.shape[1]
    fs = w_gate.shape[-1]
    nin = NDEV * w_in.shape[-1]
    assert d == 2 * c and nin == 5 * c + 2 * d, (d, c, nin)
    x2, p2, tgt = x[0], p[0, 0], loss_target[0]
    gfin = g_final.reshape(1, d)

    kpa, kpb = 8, HALO_B
    wa_sh = jnp.pad(conv_a_w[0], ((0, kpa - CONV_A_K), (0, 0)))
    wd_sh = jnp.pad(conf_dw_w[0], ((0, kpb - CONF_K), (0, 0)))
    kind_of = dict(w_in="col", w_out_a="col", w_pw_b="col", w_ple_proj="col", w_o="row", w_ple_gate="row",
                   w_gate="blk", w_up="blk", w_down="blk")
    weight = dict(w_in=w_in, w_out_a=w_out_a, w_pw_b=w_pw_b, w_ple_proj=w_ple_proj, w_o=w_o, w_ple_gate=w_ple_gate,
                  w_gate=w_gate, w_up=w_up, w_down=w_down)
    shard_of = {nm: tuple(w.shape[1:]) for nm, w in weight.items()}
    bf16_shard = lambda nm: weight[nm][0].astype(BF16)
    kinds_ = lambda grp: [kind_of[nm] for nm in grp]
    shapes_ = lambda grp: [shard_of[nm] for nm in grp]
    first_stage = lambda grp: _ag1([bf16_shard(nm) for nm in grp], kinds_(grp))
    second_stage = lambda grp, parts: _ag2(parts, kinds_(grp), shapes_(grp))
    grp_1 = ["w_out_a", "w_pw_b"]
    grp_2 = ["w_o", "w_gate"]
    grp_3 = ["w_up"]
    grp_4 = ["w_down"]
    grp_5 = ["w_ple_gate", "w_ple_proj"]

    tm = _tile(s, 1024)
    tn = _tile(d, 1024)
    assert (5 * c) % tn == 0 and d % tn == 0 and c % tn == 0
    ga_blk, gb_blk = (5 * c) // tn, (5 * c + d) // tn
    ij = lambda i, j, k: (i, j)
    row_i = lambda i, j, k: (i, 0)

    n1 = _rms_fwd("rms1", x2, g_mix)
    proj, win, got = _proj_gather(
        n1, bf16_shard("w_in"),
        _join(_ag_direct([wa_sh, wd_sh], ["col", "col"]), first_stage(grp_1)), first_stage(grp_2),
    )
    (wa, wd), part_12 = got[:2], got[2:]
    grp_12 = grp_1 + grp_2
    ya_in = _mix_a_fwd(proj, wa, s, c)
    (v_act, u_act, cv), got = _mix_b_fwd(
        proj, b_glu, wd, conf_dw_b, conf_ln_g, conf_ln_b, s, c,
        comm=_join(second_stage(grp_12, part_12), first_stage(grp_3)),
    )
    (wouta, wpw, wo, wg), part_3 = got[: len(grp_12)], got[len(grp_12) :]

    def ep_merge(accs, ex, os_):
        sa = _sigmoid(ex[0][...].astype(F32))
        sb = _sigmoid(ex[1][...].astype(F32))
        ya = accs[0]
        yb = accs[1] + ex[2][...]
        os_[0][...] = (sa * ya + sb * yb).astype(BF16)
        os_[1][...] = ya.astype(BF16)
        os_[2][...] = yb.astype(BF16)

    gate_a_spec = pl.BlockSpec((tm, tn), lambda i, j, k: (i, ga_blk + j))
    gate_b_spec = pl.BlockSpec((tm, tn), lambda i, j, k: (i, gb_blk + j))
    out_sd = (SDS((s, d), BF16), pl.BlockSpec((tm, tn), ij))
    (m_act, ya, yb), got = _fmm(
        "merge", (s // tm, d // tn, 1),
        [(ya_in, pl.BlockSpec((tm, c), row_i)), (wouta, pl.BlockSpec((c, tn), lambda i, j, k: (0, j))),
         (v_act, pl.BlockSpec((tm, c), row_i)), (wpw, pl.BlockSpec((c, tn), lambda i, j, k: (0, j)))],
        [(0, 1, NN, 0, None), (2, 3, NN, 1, None)], [(tm, tn), (tm, tn)],
        [(proj, gate_a_spec), (proj, gate_b_spec), (b_pw_b, pl.BlockSpec((1, tn), lambda i, j, k: (0, j)))],
        [out_sd, out_sd, out_sd], ep_merge, csplit=EPILOGUE_CHUNK,
        comm=_join(second_stage(grp_3, part_3), first_stage(grp_4)),
    )
    (wu,), part_4 = got[: len(grp_3)], got[len(grp_3) :]

    def ep_residual(accs, ex, os_):
        os_[0][...] = accs[0] + ex[0][...]

    (h1,), got = _fmm(
        "w_o", (s // tm, d // tn, 1),
        [(m_act, pl.BlockSpec((tm, d), row_i)), (wo, pl.BlockSpec((d, tn), lambda i, j, k: (0, j)))],
        [(0, 1, NN, 0, None)], [(tm, tn)], [(x2, pl.BlockSpec((tm, tn), ij))],
        [(SDS((s, d), F32), pl.BlockSpec((tm, tn), ij))], ep_residual, csplit=EPILOGUE_CHUNK,
        comm=_join(second_stage(grp_4, part_4), first_stage(grp_5)),
    )
    (wdn,), part_5 = got[: len(grp_4)], got[len(grp_4) :]
    n2 = _rms_fwd("rms2", h1, g_ffn)

    def ep_gateup(accs, ex, os_):
        g, u = accs
        os_[0][...] = g.astype(BF16)
        os_[1][...] = u.astype(BF16)
        os_[2][...] = (g * _sigmoid(g) * u).astype(BF16)

    ff_sd = (SDS((NDEV, s, fs), BF16), pl.BlockSpec((None, tm, fs), lambda i, j, k: (j, i, 0)))
    w_col_blk = pl.BlockSpec((None, d, fs), lambda i, j, k: (j, 0, 0))
    (g_act, u_ff, f_act), (wpg, wpp) = _fmm(
        "gate_up", (s // tm, NDEV, 1),
        [(n2, pl.BlockSpec((tm, d), row_i)), (wg, w_col_blk), (wu, w_col_blk)],
        [(0, 1, NN, 0, None), (0, 2, NN, 1, None)], [(tm, fs), (tm, fs)], [],
        [ff_sd, ff_sd, ff_sd], ep_gateup, csplit=EPILOGUE_CHUNK,
        comm=second_stage(grp_5, part_5),
    )
    pair = 2
    (h2,), _ = _fmm(
        "down", (s // tm, d // tn, NDEV // pair),
        [(f_act, pl.BlockSpec((pair, tm, fs), lambda i, j, k: (k, i, 0))),
         (wdn, pl.BlockSpec((pair, fs, tn), lambda i, j, k: (k, 0, j)))],
        [(0, 1, NN, 0, None, pair)], [(tm, tn)], [(h1, pl.BlockSpec((tm, tn), ij))],
        [(SDS((s, d), F32), pl.BlockSpec((tm, tn), ij))], ep_residual,
    )
    n3 = _rms_fwd("rms3", h2, g_ple)

    tr = _tile(s, 256)
    n_r = s // tr
    rows = lambda i, j, k: (i, 0)
    whole = lambda i, j, k: (0, 0)
    part_spec = lambda nrow: pl.BlockSpec((None, nrow, d), lambda i, j, k: (i, 0, 0))

    def ep_ple(accs, ex, os_):
        h2_, t_, gf = ex[0][...], ex[1][...], ex[2][...]
        ple = accs[0]
        s3 = _sigmoid(accs[1])
        h3 = h2_ + s3 * ple
        r = lax.rsqrt(jnp.mean(h3 * h3, axis=-1, keepdims=True) + EPS)
        hn = h3 * r
        e = hn * gf - t_
        loss = 0.5 * jnp.sum(jnp.mean(e * e, axis=-1, keepdims=True), axis=0, keepdims=True)
        dy = e * (1.0 / d)
        dn = dy * gf
        dh3 = r * (dn - hn * jnp.mean(dn * hn, axis=-1, keepdims=True))
        os_[0][...] = dh3
        os_[1][...] = (dh3 * s3).astype(BF16)
        os_[2][...] = (dh3 * ple * s3 * (1.0 - s3)).astype(BF16)
        os_[3][0:1, :] = jnp.sum(dy * hn, axis=0, keepdims=True)
        os_[3][1:2, :] = jnp.broadcast_to(loss, (1, d))

    (dh3, d_ple, d_g3, part_fin), _ = _fmm(
        "ple_loss", (n_r, 1, 1),
        [(p2, pl.BlockSpec((tr, pdim), rows)), (wpp, pl.BlockSpec((pdim, d), whole)),
         (n3, pl.BlockSpec((tr, d), rows)), (wpg, pl.BlockSpec((d, d), whole))],
        [(0, 1, NN, 0, None), (2, 3, NN, 1, None)], [(tr, d), (tr, d)],
        [(h2, pl.BlockSpec((tr, d), rows)), (tgt, pl.BlockSpec((tr, d), rows)), (gfin, pl.BlockSpec((1, d), whole))],
        [(SDS((s, d), F32), pl.BlockSpec((tr, d), rows)), (SDS((s, d), BF16), pl.BlockSpec((tr, d), rows)),
         (SDS((s, d), BF16), pl.BlockSpec((tr, d), rows)), (SDS((n_r, 2, d), F32), part_spec(2))],
        ep_ple,
    )

    g_wpp = _mm_tn("d_w_ple_proj", p2, d_ple)
    g_wpg = _mm_tn("d_w_ple_gate", n3, d_g3)

    def ep_norm_bwd(accs, ex, os_):
        dh, dg = _rms_bwd(accs[0], ex[0][...], ex[2][...])
        dh = ex[1][...] + dh
        os_[0][...] = dh
        os_[1][...] = dh.astype(BF16)
        os_[2][...] = dg

    norm_outs = lambda t: [
        (SDS((s, d), F32), pl.BlockSpec((t, d), rows)), (SDS((s, d), BF16), pl.BlockSpec((t, d), rows)),
        (SDS((s // t, 1, d), F32), part_spec(1)),
    ]
    def exchange1(names, wholes):
        return _rs1(wholes, kinds_(names), shapes_(names))

    def pair_sums(names, wholes, got):
        return [_pair_sum("pair_sum_" + nm, wholes[t], kind_of[nm], got[t]) for t, nm in enumerate(names)]

    lands = {}
    grp1 = ["w_ple_proj", "w_ple_gate"]
    (dh2, dh2b, part_ple), got = _fmm(
        "d_n3", (n_r, 1, 1),
        [(d_g3, pl.BlockSpec((tr, d), rows)), (wpg, pl.BlockSpec((d, d), whole))],
        [(0, 1, NT, 0, None)], [(tr, d)],
        [(h2, pl.BlockSpec((tr, d), rows)), (dh3, pl.BlockSpec((tr, d), rows)), (g_ple, pl.BlockSpec((1, d), whole))],
        norm_outs(tr), ep_norm_bwd,
        comm=exchange1(grp1, [g_wpp, g_wpg]),
    )
    sums1 = pair_sums(grp1, [g_wpp, g_wpg], got)

    def ep_ddown(accs, ex, os_):
        g = ex[0][...].astype(F32)
        u = ex[1][...].astype(F32)
        sg = _sigmoid(g)
        df = accs[0]
        os_[0][...] = (df * u * sg * (1.0 + g * (1.0 - sg))).astype(BF16)
        os_[1][...] = (df * g * sg).astype(BF16)

    ff_in = pl.BlockSpec((None, tm, fs), lambda i, j, k: (j, i, 0))
    (d_g, d_u), got = _fmm(
        "d_down", (s // tm, NDEV, 1),
        [(dh2b, pl.BlockSpec((tm, d), row_i)), (wdn, pl.BlockSpec((None, fs, d), lambda i, j, k: (j, 0, 0)))],
        [(0, 1, NT, 0, None)], [(tm, fs)], [(g_act, ff_in), (u_ff, ff_in)],
        [ff_sd, ff_sd], ep_ddown, csplit=EPILOGUE_CHUNK,
        comm=_rs2(sums1),
    )
    lands.update(zip(grp1, got))
    tk = _tile(s, 1024)
    (g_wdn,), _ = _fmm(
        "d_w_down", (NDEV, 1, s // tk),
        [(f_act, pl.BlockSpec((None, tk, fs), lambda i, j, k: (i, k, 0))), (dh2b, pl.BlockSpec((tk, d), lambda i, j, k: (k, 0)))],
        [(0, 1, TN, 0, None)], [(fs, d)], [],
        [(SDS((NDEV, fs, d), BF16), pl.BlockSpec((None, fs, d), lambda i, j, k: (i, 0, 0)))], _ep_bf16,
    )

    def ep_two_bf16(accs, ex, os_):
        os_[0][...] = accs[0].astype(BF16)
        os_[1][...] = accs[1].astype(BF16)

    ff_k = pl.BlockSpec((None, tk, fs), lambda i, j, k: (i, k, 0))
    wcol_sd = (SDS((NDEV, d, fs), BF16), pl.BlockSpec((None, d, fs), lambda i, j, k: (i, 0, 0)))
    grp2 = ["w_down"]
    (g_wg, g_wu), got = _fmm(
        "d_w_gate_up", (NDEV, 1, s // tk),
        [(n2, pl.BlockSpec((tk, d), lambda i, j, k: (k, 0))), (d_g, ff_k), (d_u, ff_k)],
        [(0, 1, TN, 0, None), (0, 2, TN, 1, None)], [(d, fs), (d, fs)], [],
        [wcol_sd, wcol_sd], ep_two_bf16,
        comm=exchange1(grp2, [g_wdn]),
    )
    sums2 = pair_sums(grp2, [g_wdn], got)
    grp3 = ["w_gate", "w_up"]
    th = _tile(s // 2, 1024)
    ff_a = pl.BlockSpec((None, th, fs), lambda i, j, k: (k, i, 0))
    w_k = pl.BlockSpec((None, d, fs), lambda i, j, k: (k, 0, 0))
    (d_n2,), got = _fmm(
        "d_n2", (s // th, 1, NDEV),
        [(d_g, ff_a), (wg, w_k), (d_u, ff_a), (wu, w_k)],
        [(0, 1, NT, 0, None), (2, 3, NT, 0, None)], [(th, d)], [],
        [(SDS((s, d), BF16), pl.BlockSpec((th, d), rows))], _ep_bf16,
        comm=_join(_rs2(sums2), exchange1(grp3, [g_wg, g_wu])),
    )
    lands.update(zip(grp2, got[:1]))
    sums3 = pair_sums(grp3, [g_wg, g_wu], got[1:])
    dh1, dh1b, part_ffn = _norm_bwd("d_h1", d_n2, h1, dh2, g_ffn, True)
    g_wo = _mm_tn("d_w_o", m_act, dh1b)

    def ep_dm(accs, ex, os_):
        ya_, yb_ = ex[0][...].astype(F32), ex[1][...].astype(F32)
        sa = _sigmoid(ex[2][...].astype(F32))
        sb = _sigmoid(ex[3][...].astype(F32))
        dm = accs[0]
        d_yb = dm * sb
        os_[0][...] = (dm * sa).astype(BF16)
        os_[1][...] = d_yb.astype(BF16)
        os_[2][...] = (dm * ya_ * sa * (1.0 - sa)).astype(BF16)
        os_[3][...] = (dm * yb_ * sb * (1.0 - sb)).astype(BF16)
        os_[4][...] = jnp.sum(d_yb, axis=0, keepdims=True)

    tile_ij = pl.BlockSpec((tm, tn), ij)
    grp4 = ["w_o"]
    (d_ya, d_yb, d_ga, d_gb, part_bpw), got = _fmm(
        "d_merge", (s // tm, d // tn, 1),
        [(dh1b, pl.BlockSpec((tm, d), row_i)), (wo, pl.BlockSpec((tn, d), lambda i, j, k: (j, 0)))],
        [(0, 1, NT, 0, None)], [(tm, tn)],
        [(ya, tile_ij), (yb, tile_ij), (proj, gate_a_spec), (proj, gate_b_spec)],
        [out_sd, out_sd, out_sd, out_sd,
         (SDS((s // tm, 1, d), F32), pl.BlockSpec((None, 1, tn), lambda i, j, k: (i, 0, j)))],
        ep_dm, csplit=EPILOGUE_CHUNK,
        comm=exchange1(grp4, [g_wo]),
    )
    sums4 = pair_sums(grp4, [g_wo], got)
    g_wouta = _mm_tn("d_w_out_a", ya_in, d_ya)
    g_wpw = _mm_tn("d_w_pw_b", v_act, d_yb)
    grp5 = ["w_out_a", "w_pw_b"]
    d_ya_in, got = _mm_nt("d_ya_in", d_ya, wouta, comm=exchange1(grp5, [g_wouta, g_wpw]))
    sums5 = pair_sums(grp5, [g_wouta, g_wpw], got)
    d_v, _ = _mm_nt("d_v", d_yb, wpw)
    d_cv, part_ln = _mix_b_bwd1(d_v, cv, conf_ln_g, conf_ln_b, s, c)
    (d_b, part_wd, part_bglu), got = _mix_b_bwd2(d_cv, u_act, proj, b_glu, wd, s, c, comm=_rs2(sums3))
    lands.update(zip(grp3, got))
    d_a, part_wa = _mix_a_bwd(d_ya_in, proj, wa, s, c)

    nb = nin // c
    gblk = d // c
    lo = [0, 3, 5, 5 + gblk]
    hi = [3, 5, 5 + gblk, 5 + 2 * gblk]
    pieces = [d_a, d_b, d_ga, d_gb]

    def active(q, ax):
        return lambda ids: jnp.logical_and(ids[ax] >= lo[q], ids[ax] < hi[q])

    def piece_spec(q, rows_, ax, row0=0):
        def index(i, j, k):
            ids = (i, j, k)
            col = jnp.clip(ids[ax] - lo[q], 0, hi[q] - lo[q] - 1)
            row = i + row0 if ax == 2 else jnp.where(active(q, ax)(ids), k, 0)
            return (row, col)

        return pl.BlockSpec((rows_, c), index)

    tkw = _tile(s, 1024)
    (g_win,), got = _fmm(
        "d_w_in", (1, nb, s // tkw),
        [(n1, pl.BlockSpec((tkw, d), lambda i, j, k: (k, 0)))]
        + [(pieces[q], piece_spec(q, tkw, 1)) for q in range(4)],
        [(0, 1 + q, TN, 0, active(q, 1)) for q in range(4)], [(d, c)], [],
        [(SDS((d, nin), BF16), pl.BlockSpec((d, c), lambda i, j, k: (0, j)))], _ep_bf16,
        comm=_rs2(sums4 + sums5),
    )
    lands.update(zip(grp4 + grp5, got))

    grp6 = ["w_in"]
    n_half = max(1, 3 * (s // th) // 8)

    def d_n1_rows(name, row0, n_tiles, comm, into):
        return _fmm(
            name, (n_tiles, 1, nb),
            [(pieces[q], piece_spec(q, th, 2, row0)) for q in range(4)]
            + [(win, pl.BlockSpec((d, c), lambda i, j, k: (0, k)))],
            [(q, 4, NT, 0, active(q, 2)) for q in range(4)], [(th, d)], [],
            [(SDS((s, d), BF16), pl.BlockSpec((th, d), lambda i, j, k: (i + row0, 0)))], _ep_bf16,
            comm=comm, into=into,
        )

    (d_n1,), got = d_n1_rows("d_n1_a", 0, n_half, exchange1(grp6, [g_win]), None)
    (d_n1,), got = d_n1_rows("d_n1_b", n_half, s // th - n_half, _rs2(pair_sums(grp6, [g_win], got)), d_n1)
    lands.update(zip(grp6, got))
    dx, part_mix = _norm_bwd("d_x", d_n1, x2, dh1, g_mix, False)

    small_parts = [
        jnp.sum(part_mix, axis=0),
        jnp.sum(part_bglu, axis=0),
        jnp.sum(part_ln[:, 2], axis=0),
        jnp.sum(part_ln[:, 0], axis=0),
        jnp.sum(part_ln[:, 1], axis=0),
        jnp.sum(part_bpw, axis=0),
        jnp.sum(part_ffn, axis=0),
        jnp.sum(part_ple, axis=0),
        jnp.sum(part_fin[:, 0], axis=0),
        jnp.sum(part_wa, axis=0),
        jnp.sum(part_wd, axis=0),
        jnp.broadcast_to(jnp.sum(part_fin[:, 1, 0]), (c,)),
    ]
    small_shapes = [(1, d), (1, 2 * c), (1, c), (1, c), (1, c), (1, d), (1, d), (1, d), (d,), (kpa, c), (kpb, c), (c,)]
    total = _all_reduce_small(_pack(small_parts, c))
    (gr_g_mix, gr_b_glu, gr_dw_b, gr_ln_g, gr_ln_b, gr_b_pw, gr_g_ffn, gr_g_ple, gr_g_final, gr_wa, gr_wd, loss_row) = _unpack(total, small_shapes)
    loss = loss_row[0]
    my = _dev_index(_place())
    csh = conv_a_w.shape[-1]
    gr_conv_a = lax.dynamic_slice_in_dim(gr_wa[:CONV_A_K], my * csh, csh, axis=1)[None]
    gr_conf_dw = lax.dynamic_slice_in_dim(gr_wd[:CONF_K], my * csh, csh, axis=1)[None]

    big_m = dict(w_in=m_w_in, w_out_a=m_w_out_a, w_pw_b=m_w_pw_b, w_ple_proj=m_w_ple_proj, w_o=m_w_o,
                 w_ple_gate=m_w_ple_gate, w_gate=m_w_gate, w_up=m_w_up, w_down=m_w_down)
    big_v = dict(w_in=v_w_in, w_out_a=v_w_out_a, w_pw_b=v_w_pw_b, w_ple_proj=v_w_ple_proj, w_o=v_w_o,
                 w_ple_gate=v_w_ple_gate, w_gate=v_w_gate, w_up=v_w_up, w_down=v_w_down)
    big_out = {}
    for nm in weight:
        res = _adamw_big("adamw_" + nm, lands[nm], weight[nm][0], big_m[nm][0], big_v[nm][0])
        big_out[nm] = [r[None] for r in res]

    small_names = ["g_mix", "conv_a_w", "b_glu", "conf_dw_w", "conf_dw_b", "conf_ln_g", "conf_ln_b", "b_pw_b", "g_ffn", "g_ple", "g_final"]
    small_g = [gr_g_mix, gr_conv_a, gr_b_glu, gr_conf_dw, gr_dw_b, gr_ln_g, gr_ln_b, gr_b_pw, gr_g_ffn, gr_g_ple, gr_g_final]
    small_w = [g_mix, conv_a_w, b_glu, conf_dw_w, conf_dw_b, conf_ln_g, conf_ln_b, b_pw_b, g_ffn, g_ple, g_final]
    small_m = [m_g_mix, m_conv_a_w, m_b_glu, m_conf_dw_w, m_conf_dw_b, m_conf_ln_g, m_conf_ln_b, m_b_pw_b, m_g_ffn, m_g_ple, m_g_final]
    small_v = [v_g_mix, v_conv_a_w, v_b_glu, v_conf_dw_w, v_conf_dw_b, v_conf_ln_g, v_conf_ln_b, v_b_pw_b, v_g_ffn, v_g_ple, v_g_final]
    shp = [tuple(w.shape) for w in small_w]
    small_g = [g.reshape(sh) for g, sh in zip(small_g, shp)]
    sd, sm, sv = _adamw_small(_pack(small_g, 128), _pack(small_w, 128), _pack(small_m, 128), _pack(small_v, 128))
    small_out = {}
    for nm, g, dl, mm, vv in zip(small_names, small_g, _unpack(sd, shp), _unpack(sm, shp), _unpack(sv, shp)):
        small_out[nm] = [g, dl, mm, vv]

    order = ["g_mix", "w_in", "conv_a_w", "w_out_a", "b_glu", "conf_dw_w", "conf_dw_b", "conf_ln_g", "conf_ln_b", "w_pw_b", "b_pw_b", "w_o", "g_ffn", "w_gate", "w_up", "w_down", "g_ple", "w_ple_gate", "w_ple_proj", "g_final"]
    allo = {**big_out, **small_out}
    outs = [loss, dx[None]]
    for q in range(4):
        outs += [allo[nm][q] for nm in order]
    return tuple(outs)
```

```python
import jax
import jax.numpy as jnp
from jax import lax
from jax.experimental import pallas as pl
from jax.experimental.pallas import tpu as pltpu

F32, BF16 = jnp.float32, jnp.bfloat16
EPS, LN_EPS = 1e-6, 1e-5
ADAM_LR, ADAM_B1, ADAM_B2, ADAM_EPS, ADAM_WD, ADAM_STEP = 0.001, 0.9, 0.999, 1e-08, 0.01, 10
CONV_A_K, CONF_K = 3, 31
NDEV = 8
NN = (((1,), (0,)), ((), ()))
NT = (((1,), (1,)), ((), ()))
TN = (((0,), (0,)), ((), ()))
V7X_VMEM_LIMIT_BYTES = 56 * 1024 * 1024
MESH = pl.DeviceIdType.MESH
SDS = jax.ShapeDtypeStruct
HALO_A, HALO_B = 16, 32
EPILOGUE_CHUNK = 256
CONV_ROWS = 32
CONF_ROWS = 16


def _tile(n, pref):
    t = min(n, pref)
    while n % t:
        t -= 8
    return t


def _sigmoid(x):
    return jax.nn.sigmoid(x)


def _params(sem=None):
    return pltpu.CompilerParams(vmem_limit_bytes=V7X_VMEM_LIMIT_BYTES, dimension_semantics=sem)


def _edge(grid, last):
    cond = None
    for ax, n in enumerate(grid):
        here = pl.program_id(ax) == (n - 1 if last else 0)
        cond = here if cond is None else jnp.logical_and(cond, here)
    return cond


def _join(*comms):
    ins, outs, alias, sems, spans = [], [], {}, [], []
    for cm in comms:
        spans.append((len(ins), len(outs), len(sems)))
        for i, o in cm["alias"].items():
            alias[len(ins) + i] = len(outs) + o
        ins += cm["ins"]
        outs += cm["outs"]
        sems += cm["sems"]

    def run(which):
        def f(i_refs, o_refs, s_refs):
            for cm, (a, b, c_) in zip(comms, spans):
                cm[which](
                    i_refs[a : a + len(cm["ins"])], o_refs[b : b + len(cm["outs"])], s_refs[c_ : c_ + len(cm["sems"])]
                )

        return f

    return dict(ins=ins, outs=outs, alias=alias, sems=sems, start=run("start"), finish=run("finish"))


def _call(body, name, grid, in_specs, args, out_specs, out_shape, scratch=(), sem=None, comm=None, alias=None):
    n_in, n_out, n_s = len(args), len(out_shape), len(scratch)
    alias = dict(alias or {})
    if comm is None:
        res = pl.pallas_call(
            body, name=name, grid=grid, in_specs=list(in_specs), out_specs=list(out_specs), out_shape=list(out_shape),
            scratch_shapes=list(scratch), input_output_aliases=alias, compiler_params=_params(sem),
        )(*args)
        return list(res), []
    n_ci, n_co = len(comm["ins"]), len(comm["outs"])

    def wrapped(*refs):
        ins = refs[:n_in]
        ci = refs[n_in : n_in + n_ci]
        o0 = n_in + n_ci
        outs = refs[o0 : o0 + n_out]
        co = refs[o0 + n_out : o0 + n_out + n_co]
        s0 = o0 + n_out + n_co
        sc = refs[s0 : s0 + n_s]
        cs = refs[s0 + n_s :]
        pl.when(_edge(grid, False))(lambda: comm["start"](ci, co, cs))
        body(*ins, *outs, *sc)
        pl.when(_edge(grid, True))(lambda: comm["finish"](ci, co, cs))

    hbm = pl.BlockSpec(memory_space=pl.ANY)
    res = pl.pallas_call(
        wrapped,
        name=name,
        grid=grid,
        in_specs=list(in_specs) + [hbm] * n_ci,
        out_specs=list(out_specs) + [hbm] * n_co,
        out_shape=list(out_shape) + list(comm["outs"]),
        scratch_shapes=list(scratch) + list(comm["sems"]),
        input_output_aliases={**alias, **{n_in + i: n_out + o for i, o in comm["alias"].items()}},
        compiler_params=_params(("arbitrary",) * len(grid)),
    )(*args, *comm["ins"])
    return list(res[:n_out]), list(res[n_out:])


def _col_chunks(n, pref):
    out, c0 = [], 0
    while c0 < n:
        w = min(pref, n - c0)
        out.append((c0, w))
        c0 += w
    return out


def _fmm(name, grid, operands, terms, acc_shapes, extras, outs, epilogue, comm=None, csplit=None, into=None):
    n_p, n_e, n_o, n_a = len(operands), len(extras), len(outs), len(acc_shapes)
    nk = grid[-1]
    kax = len(grid) - 1
    simple = nk == 1 and all(t[4] is None for t in terms)
    alias = None
    if into is not None:
        extras = list(extras) + [(into, pl.BlockSpec(memory_space=pl.ANY))]
        alias = {n_p + n_e: 0}
        n_e += 1
    if csplit is not None:
        assert simple and into is None and all(t[2] in (NN, NT) and (len(t) <= 5 or not t[5]) for t in terms)
        tn_ = acc_shapes[0][1]
        chunks = _col_chunks(tn_, csplit)

    def dot(a, b, dims):
        if a.dtype != BF16:
            a = a.astype(BF16)
        if b.dtype != BF16:
            b = b.astype(BF16)
        return lax.dot_general(a, b, dims, preferred_element_type=F32)

    def value(refs, term):
        slabs = term[5] if len(term) > 5 else 0
        if not slabs:
            return dot(refs[term[0]][...], refs[term[1]][...], term[2])
        tot = None
        for sl in range(slabs):
            d = dot(refs[term[0]][sl], refs[term[1]][sl], term[2])
            tot = d if tot is None else tot + d
        return tot

    def always(refs):
        parts = [None] * n_a
        for term in terms:
            if term[4] is None:
                d = value(refs, term)
                parts[term[3]] = d if parts[term[3]] is None else parts[term[3]] + d
        return parts

    def chunked(refs, ex, os_, accs):
        cols = lambda ref, c0, w: ref.at[:, pl.ds(c0, w)] if ref.shape[-1] == tn_ else ref

        def dots(k):
            c0, w = chunks[k]
            parts = [None] * n_a
            for term in terms:
                b_ref = refs[term[1]]
                b = b_ref[:, pl.ds(c0, w)] if term[2] == NN else b_ref[pl.ds(c0, w), :]
                d = dot(refs[term[0]][...], b, term[2])
                parts[term[3]] = d if parts[term[3]] is None else parts[term[3]] + d
            for ai in range(n_a):
                accs[ai][k % 2, :, pl.ds(0, w)] = parts[ai]

        def finish(k):
            c0, w = chunks[k]
            vals = [accs[ai][k % 2, :, pl.ds(0, w)] for ai in range(n_a)]
            epilogue(vals, [cols(e, c0, w) for e in ex], [cols(o, c0, w) for o in os_])

        dots(0)
        for k in range(1, len(chunks)):
            dots(k)
            finish(k - 1)
        finish(len(chunks) - 1)

    def body(*refs):
        ex = refs[n_p : n_p + n_e]
        os_ = refs[n_p + n_e : n_p + n_e + n_o]
        accs = refs[n_p + n_e + n_o :]
        if simple and csplit is not None:
            chunked(refs, ex, os_, accs)
            return
        if simple:
            epilogue(always(refs), ex, os_)
            return
        ids = [pl.program_id(ax) for ax in range(len(grid))]
        k = ids[kax]

        @pl.when(k == 0)
        def _():
            for acc in accs:
                acc[...] = jnp.zeros(acc.shape, F32)

        for ai, part in enumerate(always(refs)):
            if part is not None:
                accs[ai][...] += part
        for term in terms:
            if term[4] is not None:

                def add(term=term):
                    accs[term[3]][...] += value(refs, term)

                pl.when(term[4](ids))(add)

        @pl.when(k == nk - 1)
        def _():
            epilogue([acc[...] for acc in accs], ex, os_)

    return _call(
        body,
        name,
        grid,
        [o[1] for o in operands] + [e[1] for e in extras],
        [o[0] for o in operands] + [e[0] for e in extras],
        [o[1] for o in outs],
        [o[0] for o in outs],
        scratch=[pltpu.VMEM((2, s[0], csplit), F32) for s in acc_shapes] if csplit is not None
        else [] if simple else [pltpu.VMEM(s, F32) for s in acc_shapes],
        sem=("parallel",) * kax + ("arbitrary",),
        comm=comm,
        alias=alias,
    )


def _rms_bwd(dn_raw, h, g):
    r = lax.rsqrt(jnp.mean(h * h, axis=-1, keepdims=True) + EPS)
    hn = h * r
    dg = jnp.sum(dn_raw * hn, axis=0, keepdims=True)
    dn = dn_raw * g
    dh = r * (dn - hn * jnp.mean(dn * hn, axis=-1, keepdims=True))
    return dh, dg


def _rms_fwd(name, h, g):
    s, d = h.shape
    ts = _tile(s, 512)

    def body(h_ref, g_ref, o_ref):
        x = h_ref[...]
        r = lax.rsqrt(jnp.mean(x * x, axis=-1, keepdims=True) + EPS)
        o_ref[...] = (x * r * g_ref[...]).astype(BF16)

    return pl.pallas_call(
        body,
        name=name,
        grid=(s // ts,),
        in_specs=[pl.BlockSpec((ts, d), lambda i: (i, 0)), pl.BlockSpec((1, d), lambda i: (0, 0))],
        out_specs=pl.BlockSpec((ts, d), lambda i: (i, 0)),
        out_shape=SDS((s, d), BF16),
        compiler_params=_params(("parallel",)),
    )(h, g)


def _norm_bwd(name, dn, h, dres, g, want_bf16):
    s, d = h.shape
    ts = _tile(s, 512)

    def body(dn_r, h_r, dres_r, g_r, *outs):
        dh, dg = _rms_bwd(dn_r[...].astype(F32), h_r[...], g_r[...])
        dh = dres_r[...] + dh
        outs[0][...] = dh
        if want_bf16:
            outs[1][...] = dh.astype(BF16)
        outs[-1][...] = dg

    blk = pl.BlockSpec((ts, d), lambda i: (i, 0))
    part = pl.BlockSpec((None, 1, d), lambda i: (i, 0, 0))
    return pl.pallas_call(
        body,
        name=name,
        grid=(s // ts,),
        in_specs=[blk, blk, blk, pl.BlockSpec((1, d), lambda i: (0, 0))],
        out_specs=[blk] + ([blk] if want_bf16 else []) + [part],
        out_shape=[SDS((s, d), F32)] + ([SDS((s, d), BF16)] if want_bf16 else []) + [SDS((s // ts, 1, d), F32)],
        compiler_params=_params(("parallel",)),
    )(dn, h, dres, g)


def _prev_halo(ts, hb):
    r = ts // hb
    return lambda i: jnp.maximum(i * r - 1, 0)


def _next_halo(ts, hb, s):
    r = ts // hb
    last = s // hb - 1
    return lambda i: jnp.minimum((i + 1) * r, last)


def _shift_copies(buf, sh):
    n = sh.shape[1]
    for j in range(1, 8):
        sh[j - 1, pl.ds(0, n), :] = buf[pl.ds(j, n), :]


def _tap(buf, sh, r0, off, rows):
    j = off % 8
    start = pl.multiple_of(r0 + (off - j), 8)
    if j == 0:
        return buf[pl.ds(start, rows), :]
    return sh[j - 1, pl.ds(start, rows), :]


def _mix_a_fwd(proj, wa, s, c):
    ts, hb = _tile(s, 256), HALO_A
    prev = _prev_halo(ts, hb)

    def body(ah, ab, ac, hh, hc, w, o, buf):
        i = pl.program_id(0)
        zh = hc[...].astype(F32) * hh[...].astype(F32)
        buf[pl.ds(0, hb), :] = jnp.where(i == 0, 0.0, zh)
        buf[pl.ds(hb, ts), :] = ac[...].astype(F32) * ah[...].astype(F32)
        for r0 in range(0, ts, CONV_ROWS):
            cz = jnp.zeros((CONV_ROWS, c), F32)
            for k in range(CONV_A_K):
                cz = cz + w[k : k + 1, :] * buf[pl.ds(hb + r0 - (CONV_A_K - 1) + k, CONV_ROWS), :]
            o[pl.ds(r0, CONV_ROWS), :] = (ab[pl.ds(r0, CONV_ROWS), :].astype(F32) * cz).astype(BF16)

    main = lambda cb: pl.BlockSpec((ts, c), lambda i: (i, cb))
    halo = lambda cb: pl.BlockSpec((hb, c), lambda i: (prev(i), cb))
    return pl.pallas_call(
        body,
        name="mix_a_fwd",
        grid=(s // ts,),
        in_specs=[main(0), main(1), main(2), halo(0), halo(2), pl.BlockSpec(wa.shape, lambda i: (0, 0))],
        out_specs=pl.BlockSpec((ts, c), lambda i: (i, 0)),
        out_shape=SDS((s, c), BF16),
        scratch_shapes=[pltpu.VMEM((hb + ts, c), F32)],
        compiler_params=_params(("parallel",)),
    )(proj, proj, proj, proj, proj, wa)


def _mix_b_fwd(proj, b_glu, wd, bd, lg, lb, s, c, comm=None):
    ts, hb = _tile(s, 256), HALO_B
    prev = _prev_halo(ts, hb)

    def body(gv, gg, hv, hg, bglu, w, bd_r, lg_r, lb_r, v_o, u_o, cv_o, buf, sh):
        i = pl.program_id(0)
        bv, bg = bglu[:, 0:c], bglu[:, c : 2 * c]
        uh = (hv[...].astype(F32) + bv) * _sigmoid(hg[...].astype(F32) + bg)
        buf[pl.ds(0, hb), :] = jnp.where(i == 0, 0.0, uh)
        u = (gv[...].astype(F32) + bv) * _sigmoid(gg[...].astype(F32) + bg)
        buf[pl.ds(hb, ts), :] = u
        u_o[...] = u.astype(BF16)
        _shift_copies(buf, sh)

        def chunk(ci, carry):
            r0 = pl.multiple_of(ci * CONF_ROWS, CONF_ROWS)
            acc = jnp.zeros((CONF_ROWS, c), F32)
            for k in range(CONF_K):
                acc = acc + w[k : k + 1, :] * _tap(buf, sh, r0, hb - (CONF_K - 1) + k, CONF_ROWS)
            cv_o[pl.ds(r0, CONF_ROWS), :] = acc + bd_r[...]
            return carry

        lax.fori_loop(0, ts // CONF_ROWS, chunk, 0)
        cv = cv_o[...]
        mu = jnp.mean(cv, axis=-1, keepdims=True)
        xc = cv - mu
        rs = lax.rsqrt(jnp.mean(xc * xc, axis=-1, keepdims=True) + LN_EPS)
        ln = xc * rs * lg_r[...] + lb_r[...]
        v_o[...] = (ln * _sigmoid(ln)).astype(BF16)

    main = lambda cb: pl.BlockSpec((ts, c), lambda i: (i, cb))
    halo = lambda cb: pl.BlockSpec((hb, c), lambda i: (prev(i), cb))
    full = lambda a: pl.BlockSpec(a.shape, lambda i: (0, 0))
    out = pl.BlockSpec((ts, c), lambda i: (i, 0))
    return _call(
        body,
        "mix_b_fwd",
        (s // ts,),
        [main(3), main(4), halo(3), halo(4), full(b_glu), full(wd), full(bd), full(lg), full(lb)],
        [proj, proj, proj, proj, b_glu, wd, bd, lg, lb],
        [out, out, out],
        [SDS((s, c), BF16), SDS((s, c), BF16), SDS((s, c), F32)],
        scratch=[pltpu.VMEM((hb + ts, c), F32), pltpu.VMEM((7, hb + ts - 8, c), F32)],
        sem=("parallel",),
        comm=comm,
    )


def _mix_b_bwd1(d_v, cv, lg, lb, s, c):
    ts = _tile(s, 256)

    def body(dv_r, cv_r, lg_r, lb_r, dcv_o, part_o):
        cv_ = cv_r[...]
        mu = jnp.mean(cv_, axis=-1, keepdims=True)
        xc = cv_ - mu
        rs = lax.rsqrt(jnp.mean(xc * xc, axis=-1, keepdims=True) + LN_EPS)
        xh = xc * rs
        ln = xh * lg_r[...] + lb_r[...]
        sg = _sigmoid(ln)
        d_ln = dv_r[...].astype(F32) * (sg * (1.0 + ln * (1.0 - sg)))
        dy = d_ln * lg_r[...]
        d_cv = rs * (dy - jnp.mean(dy, axis=-1, keepdims=True) - xh * jnp.mean(dy * xh, axis=-1, keepdims=True))
        dcv_o[...] = d_cv
        part_o[0:1, :] = jnp.sum(d_ln * xh, axis=0, keepdims=True)
        part_o[1:2, :] = jnp.sum(d_ln, axis=0, keepdims=True)
        part_o[2:3, :] = jnp.sum(d_cv, axis=0, keepdims=True)

    blk = pl.BlockSpec((ts, c), lambda i: (i, 0))
    full = lambda a: pl.BlockSpec(a.shape, lambda i: (0, 0))
    return pl.pallas_call(
        body,
        name="mix_b_bwd_ln",
        grid=(s // ts,),
        in_specs=[blk, blk, full(lg), full(lb)],
        out_specs=[blk, pl.BlockSpec((None, 3, c), lambda i: (i, 0, 0))],
        out_shape=[SDS((s, c), F32), SDS((s // ts, 3, c), F32)],
        compiler_params=_params(("parallel",)),
    )(d_v, cv, lg, lb)


def _mix_b_bwd2(d_cv, u, proj, b_glu, wd, s, c, comm=None):
    ts, hb = _tile(s, 256), HALO_B
    prev, nxt = _prev_halo(ts, hb), _next_halo(ts, hb, s)
    n_t = s // ts
    kp = wd.shape[0]

    def body(dcv, dcv_n, u_m, u_p, gv, gg, bglu, w, d_o, dwd_o, dbglu_o, dbuf, ubuf, dub, dsh, ush, dwacc):
        i = pl.program_id(0)
        dbuf[pl.ds(0, ts), :] = dcv[...]
        dbuf[pl.ds(ts, hb), :] = jnp.where(i == n_t - 1, 0.0, dcv_n[...])
        ubuf[pl.ds(0, hb), :] = jnp.where(i == 0, 0.0, u_p[...].astype(F32))
        ubuf[pl.ds(hb, ts), :] = u_m[...].astype(F32)
        _shift_copies(dbuf, dsh)
        _shift_copies(ubuf, ush)
        dwacc[...] = jnp.zeros(dwacc.shape, F32)

        def chunk(ci, carry):
            r0 = pl.multiple_of(ci * CONF_ROWS, CONF_ROWS)
            acc = jnp.zeros((CONF_ROWS, c), F32)
            dc = dbuf[pl.ds(r0, CONF_ROWS), :]
            for k in range(CONF_K):
                acc = acc + w[k : k + 1, :] * _tap(dbuf, dsh, r0, (CONF_K - 1) - k, CONF_ROWS)
                prod = dc * _tap(ubuf, ush, r0, hb - (CONF_K - 1) + k, CONF_ROWS)
                fold = prod[0:8]
                for a in range(1, CONF_ROWS // 8):
                    fold = fold + prod[8 * a : 8 * a + 8]
                dwacc[pl.ds(8 * k, 8), :] += fold
            dub[pl.ds(r0, CONF_ROWS), :] = acc
            return carry

        lax.fori_loop(0, ts // CONF_ROWS, chunk, 0)
        for k in range(CONF_K):
            dwd_o[k : k + 1, :] = jnp.sum(dwacc[pl.ds(8 * k, 8), :], axis=0, keepdims=True)
        dwd_o[CONF_K:kp, :] = jnp.zeros((kp - CONF_K, c), F32)
        bv, bg = bglu[:, 0:c], bglu[:, c : 2 * c]
        d_u = dub[...]
        sg = _sigmoid(gg[...].astype(F32) + bg)
        d_gv = d_u * sg
        d_gg = d_u * (gv[...].astype(F32) + bv) * sg * (1.0 - sg)
        d_o[:, 0:c] = d_gv.astype(BF16)
        d_o[:, c : 2 * c] = d_gg.astype(BF16)
        dbglu_o[:, 0:c] = jnp.sum(d_gv, axis=0, keepdims=True)
        dbglu_o[:, c : 2 * c] = jnp.sum(d_gg, axis=0, keepdims=True)

    blk = lambda cb: pl.BlockSpec((ts, c), lambda i: (i, cb))
    full = lambda a: pl.BlockSpec(a.shape, lambda i: (0, 0))
    return _call(
        body,
        "mix_b_bwd_conv",
        (n_t,),
        [
            blk(0),
            pl.BlockSpec((hb, c), lambda i: (nxt(i), 0)),
            blk(0),
            pl.BlockSpec((hb, c), lambda i: (prev(i), 0)),
            blk(3),
            blk(4),
            full(b_glu),
            full(wd),
        ],
        [d_cv, d_cv, u, u, proj, proj, b_glu, wd],
        [
            pl.BlockSpec((ts, 2 * c), lambda i: (i, 0)),
            pl.BlockSpec((None, kp, c), lambda i: (i, 0, 0)),
            pl.BlockSpec((None, 1, 2 * c), lambda i: (i, 0, 0)),
        ],
        [SDS((s, 2 * c), BF16), SDS((n_t, kp, c), F32), SDS((n_t, 1, 2 * c), F32)],
        scratch=[
            pltpu.VMEM((ts + hb, c), F32), pltpu.VMEM((hb + ts, c), F32), pltpu.VMEM((ts, c), F32),
            pltpu.VMEM((7, hb + ts - 8, c), F32), pltpu.VMEM((7, hb + ts - 8, c), F32), pltpu.VMEM((8 * CONF_K, c), F32),
        ],
        sem=("parallel",),
        comm=comm,
    )


def _mix_a_bwd(d_ya, proj, wa, s, c):
    ts, hb = _tile(s, 256), HALO_A
    prev, nxt = _prev_halo(ts, hb), _next_halo(ts, hb, s)
    n_t = s // ts
    kp = wa.shape[0]

    def body(dya, dya_n, ah, ab, ac, ah_p, ac_p, ab_n, w, d_o, dwa_o, zbuf, dbuf, dzb):
        i = pl.program_id(0)
        zbuf[pl.ds(0, hb), :] = jnp.where(i == 0, 0.0, ac_p[...].astype(F32) * ah_p[...].astype(F32))
        zbuf[pl.ds(hb, ts), :] = ac[...].astype(F32) * ah[...].astype(F32)
        dbuf[pl.ds(0, ts), :] = dya[...].astype(F32) * ab[...].astype(F32)
        dbuf[pl.ds(ts, hb), :] = jnp.where(i == n_t - 1, 0.0, dya_n[...].astype(F32) * ab_n[...].astype(F32))
        dw_rows = [jnp.zeros((1, c), F32) for _ in range(CONV_A_K)]
        for r0 in range(0, ts, CONV_ROWS):
            cz = jnp.zeros((CONV_ROWS, c), F32)
            dz = jnp.zeros((CONV_ROWS, c), F32)
            dc = dbuf[pl.ds(r0, CONV_ROWS), :]
            for k in range(CONV_A_K):
                zk = zbuf[pl.ds(hb + r0 - (CONV_A_K - 1) + k, CONV_ROWS), :]
                cz = cz + w[k : k + 1, :] * zk
                dz = dz + w[k : k + 1, :] * dbuf[pl.ds(r0 + (CONV_A_K - 1) - k, CONV_ROWS), :]
                dw_rows[k] = dw_rows[k] + jnp.sum(dc * zk, axis=0, keepdims=True)
            d_o[pl.ds(r0, CONV_ROWS), c : 2 * c] = (dya[pl.ds(r0, CONV_ROWS), :].astype(F32) * cz).astype(BF16)
            dzb[pl.ds(r0, CONV_ROWS), :] = dz
        d_z = dzb[...]
        d_o[:, 0:c] = (d_z * ac[...].astype(F32)).astype(BF16)
        d_o[:, 2 * c : 3 * c] = (d_z * ah[...].astype(F32)).astype(BF16)
        for k in range(CONV_A_K):
            dwa_o[k : k + 1, :] = dw_rows[k]
        dwa_o[CONV_A_K:kp, :] = jnp.zeros((kp - CONV_A_K, c), F32)

    blk = lambda cb: pl.BlockSpec((ts, c), lambda i: (i, cb))
    hp = lambda cb: pl.BlockSpec((hb, c), lambda i: (prev(i), cb))
    hn = lambda cb: pl.BlockSpec((hb, c), lambda i: (nxt(i), cb))
    return pl.pallas_call(
        body,
        name="mix_a_bwd",
        grid=(n_t,),
        in_specs=[blk(0), hn(0), blk(0), blk(1), blk(2), hp(0), hp(2), hn(1), pl.BlockSpec(wa.shape, lambda i: (0, 0))],
        out_specs=[pl.BlockSpec((ts, 3 * c), lambda i: (i, 0)), pl.BlockSpec((None, kp, c), lambda i: (i, 0, 0))],
        out_shape=[SDS((s, 3 * c), BF16), SDS((n_t, kp, c), F32)],
        scratch_shapes=[pltpu.VMEM((hb + ts, c), F32), pltpu.VMEM((ts + hb, c), F32), pltpu.VMEM((ts, c), F32)],
        compiler_params=_params(("parallel",)),
    )(d_ya, d_ya, proj, proj, proj, proj, proj, proj, wa)


def _ep_bf16(accs, ex, os_):
    os_[0][...] = accs[0].astype(BF16)


def _mm_tn(name, a, b, tm=2048, tn=1024, tk=1024):
    m, k1 = a.shape
    n = b.shape[1]
    tm, tn, tk = _tile(k1, tm), _tile(n, tn), _tile(m, tk)
    return _fmm(
        name,
        (k1 // tm, n // tn, m // tk),
        [(a, pl.BlockSpec((tk, tm), lambda i, j, k: (k, i))), (b, pl.BlockSpec((tk, tn), lambda i, j, k: (k, j)))],
        [(0, 1, TN, 0, None)],
        [(tm, tn)],
        [],
        [(SDS((k1, n), BF16), pl.BlockSpec((tm, tn), lambda i, j, k: (i, j)))],
        _ep_bf16,
    )[0][0]


def _mm_nt(name, a, b, tm=1024, tn=1024, comm=None):
    m, kk = a.shape
    n = b.shape[0]
    tm, tn = _tile(m, tm), _tile(n, tn)
    outs, couts = _fmm(
        name,
        (m // tm, n // tn, 1),
        [(a, pl.BlockSpec((tm, kk), lambda i, j, k: (i, 0))), (b, pl.BlockSpec((tn, kk), lambda i, j, k: (j, 0)))],
        [(0, 1, NT, 0, None)],
        [(tm, tn)],
        [],
        [(SDS((m, n), BF16), pl.BlockSpec((tm, tn), lambda i, j, k: (i, j)))],
        _ep_bf16,
        comm=comm,
    )
    return outs[0], couts


def _dev_index(dev):
    return 4 * dev[0] + 2 * dev[1] + dev[2]


def _region(ref, kind, j, shard_shape):
    if kind == "col":
        ns = shard_shape[1]
        return ref.at[:, pl.ds(pl.multiple_of(j * ns, 128), ns)]
    if kind == "row":
        rs = shard_shape[0]
        return ref.at[pl.ds(pl.multiple_of(j * rs, 8), rs), :]
    return ref.at[j]


def _whole_shape(kind, shard_shape):
    if kind == "col":
        return (shard_shape[0], NDEV * shard_shape[1])
    if kind == "row":
        return (NDEV * shard_shape[0], shard_shape[1])
    return (NDEV,) + tuple(shard_shape)


def _place():
    return lax.axis_index("x"), lax.axis_index("y"), lax.axis_index("c")


def _proj_gather(n1, w_shard, early, late):
    s, d = n1.shape
    ns = w_shard.shape[1]
    pw = 2 * ns
    tm = _tile(s // 2, 512)
    n_i = s // tm
    comm = _join(early, late)
    n_early = (len(early["ins"]), len(early["outs"]), len(early["sems"]))
    assert n_i >= 2 and not comm["alias"]
    x0, y0, _ = _place()
    order = jnp.stack([2 * x0 + y0, 2 * x0 + (1 - y0), 2 * (1 - x0) + y0, 2 * (1 - x0) + (1 - y0)]).astype(jnp.int32)
    n_ci, n_co = len(comm["ins"]), len(comm["outs"])

    def body(order_ref, n1_ref, wsh_ref, *rest):
        ci = rest[:n_ci]
        proj_ref, win_ref = rest[n_ci], rest[n_ci + 1]
        co = rest[n_ci + 2 : n_ci + 2 + n_co]
        wfull, send, recv, fsend, frecv, loc, osem = rest[n_ci + 2 + n_co : n_ci + 9 + n_co]
        cs = rest[n_ci + 9 + n_co :]
        u, i = pl.program_id(0), pl.program_id(1)
        x, y, c = _place()
        sib = (x, y, 1 - c)
        chips = [(x, y), (x, 1 - y), (1 - x, y), (1 - x, 1 - y)]
        peers = [sib] + [(*ch, c) for ch in chips[1:]]
        blk = lambda ch, core: wfull.at[2 * ch[0] + ch[1], :, pl.ds(pl.multiple_of(core * ns, 128), ns)]
        sends = [_remote(blk(chips[0], c), blk(chips[0], c), send.at[k], recv.at[k], peers[k]) for k in range(4)]
        arrivals = [_remote(blk(chips[0], 1 - c), blk(chips[0], 1 - c), send.at[0], recv.at[0], sib)] + [
            _remote(blk(chips[k], c), blk(chips[k], c), send.at[k], recv.at[k], peers[k]) for k in range(1, 4)
        ]
        passes = [_remote(blk(chips[k], c), blk(chips[k], c), fsend.at[k - 1], frecv.at[k - 1], sib) for k in range(1, 4)]
        passed = [_remote(blk(chips[k], 1 - c), blk(chips[k], 1 - c), fsend.at[k - 1], frecv.at[k - 1], sib) for k in range(1, 4)]
        mine = lambda: pltpu.make_async_copy(wsh_ref, blk(chips[0], c), loc.at[0])

        def to_hbm(unit):
            q = order_ref[unit]
            return pltpu.make_async_copy(wfull.at[q], win_ref.at[:, pl.ds(pl.multiple_of(q * pw, 128), pw)], osem.at[unit])

        a, b, e = n_early
        early_refs = (ci[:a], co[:b], cs[:e])
        late_refs = (ci[a:], co[b:], cs[e:])

        @pl.when(jnp.logical_and(u == 0, i == 0))
        def _():
            mine().start()
            mine().wait()
            for snd in sends[:3]:
                snd().start()
            early["start"](*early_refs)
            arrivals[0]().wait_recv()

        @pl.when(jnp.logical_and(u == 1, i == 0))
        def _():
            sends[3]().start()

        @pl.when(jnp.logical_and(u == 2, i == 0))
        def _():
            late["start"](*late_refs)

        for nxt in range(1, 4):

            @pl.when(jnp.logical_and(u == nxt - 1, i == n_i - 1))
            def _(nxt=nxt):
                passed[nxt - 1]().wait_recv()

        proj_ref[...] = jnp.dot(n1_ref[...], wfull[order_ref[u]], preferred_element_type=F32).astype(BF16)

        for nxt in range(1, 4):

            @pl.when(jnp.logical_and(u == nxt - 1, i == n_i - 2))
            def _(nxt=nxt):
                arrivals[nxt]().wait_recv()
                passes[nxt - 1]().start()

        for unit in range(4):

            @pl.when(jnp.logical_and(u == unit, i == n_i - 1))
            def _(unit=unit):
                to_hbm(unit).start()

        @pl.when(jnp.logical_and(u == 3, i == n_i - 1))
        def _():
            for snd in sends + passes:
                snd().wait_send()
            for unit in range(4):
                to_hbm(unit).wait()
            comm["finish"](ci, co, cs)

    hbm = pl.BlockSpec(memory_space=pl.ANY)
    dma = pltpu.SemaphoreType.DMA
    res = pl.pallas_call(
        body,
        name="proj",
        grid_spec=pltpu.PrefetchScalarGridSpec(
            num_scalar_prefetch=1,
            grid=(4, n_i),
            in_specs=[pl.BlockSpec((tm, d), lambda u, i, order_ref: (i, 0)), hbm] + [hbm] * n_ci,
            out_specs=[pl.BlockSpec((tm, pw), lambda u, i, order_ref: (i, order_ref[u])), hbm] + [hbm] * n_co,
            scratch_shapes=[pltpu.VMEM((4, d, pw), BF16), dma((4,)), dma((4,)), dma((3,)), dma((3,)), dma((1,)), dma((4,))]
            + list(comm["sems"]),
        ),
        out_shape=[SDS((s, NDEV * ns), BF16), SDS((d, NDEV * ns), BF16)] + list(comm["outs"]),
        compiler_params=_params(("arbitrary", "arbitrary")),
    )(order, n1, w_shard, *comm["ins"])
    return res[0], res[1], list(res[2:])


def _peer(me, r):
    x, y, c = me
    return (1 - x if r & 4 else x, 1 - y if r & 2 else y, 1 - c if r & 1 else c)


def _remote(src, dst, send_sem, recv_sem, to):
    return lambda: pltpu.make_async_remote_copy(
        src_ref=src, dst_ref=dst, send_sem=send_sem, recv_sem=recv_sem, device_id=to, device_id_type=MESH
    )


def _run(pairs, locals_, start):
    if start:
        for cp in locals_:
            cp.start()
        for snd, _ in pairs:
            snd().start()
    else:
        for snd, arr in pairs:
            arr().wait_recv()
            snd().wait_send()
        for cp in locals_:
            cp.wait()


def _stage(ins, outs, alias, sems, build):
    return dict(
        ins=list(ins), outs=list(outs), alias=alias, sems=list(sems),
        start=lambda i, o, s: _run(*build(i, o, s), True),
        finish=lambda i, o, s: _run(*build(i, o, s), False),
    )


def _ag1(shards, kinds):
    n_t = len(shards)
    shapes = [tuple(sh.shape) for sh in shards]

    def build(srcs, dsts, sems):
        send, recv, loc = sems
        x, y, c = _place()
        me = (x, y, c)
        peers = [(x, y, 1 - c), (1 - x, y, c), (x, 1 - y, c), (1 - x, 1 - y, c)]
        reg = lambda t, dev: _region(dsts[t], kinds[t], _dev_index(dev), shapes[t])
        pairs = []
        for t in range(n_t):
            for k, peer in enumerate(peers):
                snd = _remote(srcs[t], reg(t, me), send.at[t, k], recv.at[t, k], peer)
                arr = _remote(reg(t, peer), reg(t, peer), send.at[t, k], recv.at[t, k], peer)
                pairs.append((snd, arr))
        mine = [pltpu.make_async_copy(srcs[t], reg(t, me), loc.at[t]) for t in range(n_t)]
        return pairs, mine

    outs = [SDS(_whole_shape(kinds[t], shapes[t]), shards[t].dtype) for t in range(n_t)]
    dma = pltpu.SemaphoreType.DMA
    return _stage(shards, outs, {}, [dma((n_t, 4)), dma((n_t, 4)), dma((n_t,))], build)


def _ag_direct(shards, kinds):
    n_t = len(shards)
    shapes = [tuple(sh.shape) for sh in shards]

    def build(srcs, dsts, sems):
        send, recv, loc = sems
        me = _place()
        reg = lambda t, dev: _region(dsts[t], kinds[t], _dev_index(dev), shapes[t])
        pairs = []
        for t in range(n_t):
            for r in range(1, NDEV):
                peer = _peer(me, r)
                snd = _remote(srcs[t], reg(t, me), send.at[t, r - 1], recv.at[t, r - 1], peer)
                arr = _remote(reg(t, peer), reg(t, peer), send.at[t, r - 1], recv.at[t, r - 1], peer)
                pairs.append((snd, arr))
        mine = [pltpu.make_async_copy(srcs[t], reg(t, me), loc.at[t]) for t in range(n_t)]
        return pairs, mine

    outs = [SDS(_whole_shape(kinds[t], shapes[t]), shards[t].dtype) for t in range(n_t)]
    dma = pltpu.SemaphoreType.DMA
    return _stage(shards, outs, {}, [dma((n_t, 7)), dma((n_t, 7)), dma((n_t,))], build)


def _ag2(wholes, kinds, shapes):
    n_t = len(wholes)

    def build(_, dsts, sems):
        send, recv = sems
        x, y, c = _place()
        sib = (x, y, 1 - c)
        chips = [(1 - x, y), (x, 1 - y), (1 - x, 1 - y)]
        reg = lambda t, dev: _region(dsts[t], kinds[t], _dev_index(dev), shapes[t])
        pairs = []
        for t in range(n_t):
            for j, chip in enumerate(chips):
                snd = _remote(reg(t, (*chip, c)), reg(t, (*chip, c)), send.at[t, j], recv.at[t, j], sib)
                arr = _remote(reg(t, (*chip, 1 - c)), reg(t, (*chip, 1 - c)), send.at[t, j], recv.at[t, j], sib)
                pairs.append((snd, arr))
        return pairs, []

    outs = [SDS(w.shape, w.dtype) for w in wholes]
    dma = pltpu.SemaphoreType.DMA
    return _stage(wholes, outs, {t: t for t in range(n_t)}, [dma((n_t, 3)), dma((n_t, 3))], build)


def _chip_of(q):
    return (q >> 1, q & 1)


def _rs1(wholes, kinds, shapes):
    n_t = len(wholes)

    def build(srcs, outs, sems):
        send, recv = sems
        x, y, c = _place()
        sib = (x, y, 1 - c)
        pairs = []
        for t in range(n_t):
            for q in range(4):
                theirs = _region(srcs[t], kinds[t], _dev_index((*_chip_of(q), 1 - c)), shapes[t])
                pairs.append((
                    _remote(theirs, outs[t].at[q], send.at[t, q], recv.at[t, q], sib),
                    _remote(outs[t].at[q], outs[t].at[q], send.at[t, q], recv.at[t, q], sib),
                ))
        return pairs, []

    slabs = [SDS((4,) + tuple(shapes[t]), wholes[t].dtype) for t in range(n_t)]
    dma = pltpu.SemaphoreType.DMA
    return _stage(wholes, slabs, {}, [dma((n_t, 4)), dma((n_t, 4))], build)


def _rs2(pair_sums):
    n_t = len(pair_sums)

    def build(srcs, lands, sems):
        send, recv, loc = sems
        x, y, c = _place()
        my_chip = 2 * x + y
        pairs, mine = [], []
        for t in range(n_t):
            for j, (px, py) in enumerate([(1 - x, y), (x, 1 - y), (1 - x, 1 - y)]):
                q = 2 * px + py
                pairs.append((
                    _remote(srcs[t].at[q], lands[t].at[my_chip], send.at[t, j], recv.at[t, j], (px, py, c)),
                    _remote(lands[t].at[q], lands[t].at[q], send.at[t, j], recv.at[t, j], (px, py, c)),
                ))
            mine.append(pltpu.make_async_copy(srcs[t].at[my_chip], lands[t].at[my_chip], loc.at[t]))
        return pairs, mine

    outs = [SDS(q.shape, q.dtype) for q in pair_sums]
    dma = pltpu.SemaphoreType.DMA
    return _stage(pair_sums, outs, {}, [dma((n_t, 3)), dma((n_t, 3)), dma((n_t,))], build)


def _pair_sum(name, whole, kind, got):
    _, rows, cols = got.shape
    tr = _tile(rows, 256)
    n_r = rows // tr
    core = lax.axis_index("c").astype(jnp.int32).reshape(1)

    def body(_, a, b, o):
        o[...] = (a[...].astype(F32) + b[...].astype(F32)).astype(BF16)

    if kind == "col":
        own = pl.BlockSpec((tr, cols), lambda q, i, c_ref: (i, 2 * q + c_ref[0]))
    elif kind == "row":
        own = pl.BlockSpec((tr, cols), lambda q, i, c_ref: ((2 * q + c_ref[0]) * n_r + i, 0))
    else:
        own = pl.BlockSpec((None, tr, cols), lambda q, i, c_ref: (2 * q + c_ref[0], i, 0))
    slab = pl.BlockSpec((None, tr, cols), lambda q, i, c_ref: (q, i, 0))
    return pl.pallas_call(
        body,
        name=name,
        grid_spec=pltpu.PrefetchScalarGridSpec(
            num_scalar_prefetch=1, grid=(4, n_r), in_specs=[own, slab], out_specs=slab
        ),
        out_shape=SDS(got.shape, BF16),
        compiler_params=_params(("parallel", "parallel")),
    )(core, whole, got)


def _all_reduce_small(part):
    r_, c_ = part.shape

    def body(src, land, total, send_sems, recv_sems):
        me = _place()
        my = _dev_index(me)
        land[my] = src[...]

        def copy(r):
            peer = _peer(me, r)
            return pltpu.make_async_remote_copy(
                src_ref=src,
                dst_ref=land.at[my],
                send_sem=send_sems.at[r - 1],
                recv_sem=recv_sems.at[r - 1],
                device_id=peer,
                device_id_type=MESH,
            )

        def arrival(r):
            peer = _peer(me, r)
            slab = land.at[_dev_index(peer)]
            return pltpu.make_async_remote_copy(
                src_ref=slab,
                dst_ref=slab,
                send_sem=send_sems.at[r - 1],
                recv_sem=recv_sems.at[r - 1],
                device_id=peer,
                device_id_type=MESH,
            )

        sends = [copy(r) for r in range(1, NDEV)]
        for cp in sends:
            cp.start()
        for r in range(1, NDEV):
            arrival(r).wait_recv()
        for cp in sends:
            cp.wait_send()
        acc = land[0]
        for d in range(1, NDEV):
            acc = acc + land[d]
        total[...] = acc

    vmem = pl.BlockSpec(memory_space=pltpu.VMEM)
    return pl.pallas_call(
        body,
        name="all_reduce_small",
        in_specs=[vmem],
        out_specs=[vmem, vmem],
        out_shape=[SDS((NDEV, r_, c_), F32), SDS((r_, c_), F32)],
        scratch_shapes=[pltpu.SemaphoreType.DMA((7,)), pltpu.SemaphoreType.DMA((7,))],
    )(part)[1]


def _adamw_math(g, w, m, v):
    m2 = ADAM_B1 * m + (1.0 - ADAM_B1) * g
    v2 = ADAM_B2 * v + (1.0 - ADAM_B2) * (g * g)
    m_hat = m2 / (1.0 - ADAM_B1**ADAM_STEP)
    v_hat = v2 / (1.0 - ADAM_B2**ADAM_STEP)
    delta = -ADAM_LR * (m_hat / (jnp.sqrt(v_hat) + ADAM_EPS) + ADAM_WD * w)
    return delta, m2, v2


def _adamw_big(name, land, w, m, v):
    rows, cols = w.shape
    tr = _tile(rows, 256)
    n_slab = land.shape[0]

    def body(l_ref, w_ref, m_ref, v_ref, g_o, d_o, m_o, v_o):
        g = l_ref[0].astype(F32)
        for d in range(1, n_slab):
            g = g + l_ref[d].astype(F32)
        delta, m2, v2 = _adamw_math(g, w_ref[...], m_ref[...], v_ref[...])
        g_o[...] = g
        d_o[...] = delta
        m_o[...] = m2
        v_o[...] = v2

    blk = pl.BlockSpec((tr, cols), lambda i: (i, 0))
    return pl.pallas_call(
        body,
        name=name,
        grid=(rows // tr,),
        in_specs=[pl.BlockSpec((n_slab, tr, cols), lambda i: (0, i, 0)), blk, blk, blk],
        out_specs=[blk] * 4,
        out_shape=[SDS((rows, cols), F32)] * 4,
        compiler_params=_params(("parallel",)),
    )(land, w, m, v)


def _adamw_small(g, w, m, v):
    def body(g_ref, w_ref, m_ref, v_ref, d_o, m_o, v_o):
        delta, m2, v2 = _adamw_math(g_ref[...], w_ref[...], m_ref[...], v_ref[...])
        d_o[...] = delta
        m_o[...] = m2
        v_o[...] = v2

    vmem = pl.BlockSpec(memory_space=pltpu.VMEM)
    return pl.pallas_call(
        body,
        name="adamw_small",
        in_specs=[vmem] * 4,
        out_specs=[vmem] * 3,
        out_shape=[SDS(g.shape, F32)] * 3,
    )(g, w, m, v)


def _pack(pieces, width):
    flat = jnp.concatenate([p.reshape(-1) for p in pieces])
    rows = -(-flat.shape[0] // (8 * width)) * 8
    flat = jnp.pad(flat, (0, rows * width - flat.shape[0]))
    return flat.reshape(rows, width)


def _unpack(packed, shapes):
    flat = packed.reshape(-1)
    out, off = [], 0
    for shp in shapes:
        n = 1
        for d in shp:
            n *= d
        out.append(flat[off : off + n].reshape(shp))
        off += n
    return out


def kernel(x, p, g_mix, w_in, conv_a_w, w_out_a, b_glu, conf_dw_w, conf_dw_b, conf_ln_g, conf_ln_b, w_pw_b, b_pw_b, w_o, g_ffn, w_gate, w_up, w_down, g_ple, w_ple_gate, w_ple_proj, g_final, loss_target, m_g_mix, m_w_in, m_conv_a_w, m_w_out_a, m_b_glu, m_conf_dw_w, m_conf_dw_b, m_conf_ln_g, m_conf_ln_b, m_w_pw_b, m_b_pw_b, m_w_o, m_g_ffn, m_w_gate, m_w_up, m_w_down, m_g_ple, m_w_ple_gate, m_w_ple_proj, m_g_final, v_g_mix, v_w_in, v_conv_a_w, v_w_out_a, v_b_glu, v_conf_dw_w, v_conf_dw_b, v_conf_ln_g, v_conf_ln_b, v_w_pw_b, v_b_pw_b, v_w_o, v_g_ffn, v_w_gate, v_w_up, v_w_down, v_g_ple, v_w_ple_gate, v_w_ple_proj, v_g_final):
    s, d = x.shape[1], x.shape[2]
    c = conf_ln_g.shape[-1]
    pdim = w_ple_proj.shape[1]
    fs = w_gate.shape[-1]
    nin = NDEV * w_in.shape[-1]
    assert d == 2 * c and nin == 5 * c + 2 * d, (d, c, nin)
    x2, p2, tgt = x[0], p[0, 0], loss_target[0]
    gfin = g_final.reshape(1, d)

    kpa, kpb = 8, HALO_B
    wa_sh = jnp.pad(conv_a_w[0], ((0, kpa - CONV_A_K), (0, 0)))
    wd_sh = jnp.pad(conf_dw_w[0], ((0, kpb - CONF_K), (0, 0)))
    kind_of = dict(w_in="col", w_out_a="col", w_pw_b="col", w_ple_proj="col", w_o="row", w_ple_gate="row",
                   w_gate="blk", w_up="blk", w_down="blk")
    weight = dict(w_in=w_in, w_out_a=w_out_a, w_pw_b=w_pw_b, w_ple_proj=w_ple_proj, w_o=w_o, w_ple_gate=w_ple_gate,
                  w_gate=w_gate, w_up=w_up, w_down=w_down)
    shard_of = {nm: tuple(w.shape[1:]) for nm, w in weight.items()}
    bf16_shard = lambda nm: weight[nm][0].astype(BF16)
    kinds_ = lambda grp: [kind_of[nm] for nm in grp]
    shapes_ = lambda grp: [shard_of[nm] for nm in grp]
    first_stage = lambda grp: _ag1([bf16_shard(nm) for nm in grp], kinds_(grp))
    second_stage = lambda grp, parts: _ag2(parts, kinds_(grp), shapes_(grp))
    grp_1 = ["w_out_a", "w_pw_b"]
    grp_2 = ["w_o", "w_gate"]
    grp_3 = ["w_up"]
    grp_4 = ["w_down"]
    grp_5 = ["w_ple_gate", "w_ple_proj"]

    tm = _tile(s, 1024)
    tn = _tile(d, 1024)
    assert (5 * c) % tn == 0 and d % tn == 0 and c % tn == 0
    ga_blk, gb_blk = (5 * c) // tn, (5 * c + d) // tn
    ij = lambda i, j, k: (i, j)
    row_i = lambda i, j, k: (i, 0)

    n1 = _rms_fwd("rms1", x2, g_mix)
    proj, win, got = _proj_gather(
        n1, bf16_shard("w_in"),
        _join(_ag_direct([wa_sh, wd_sh], ["col", "col"]), first_stage(grp_1)), first_stage(grp_2),
    )
    (wa, wd), part_12 = got[:2], got[2:]
    grp_12 = grp_1 + grp_2
    ya_in = _mix_a_fwd(proj, wa, s, c)
    (v_act, u_act, cv), got = _mix_b_fwd(
        proj, b_glu, wd, conf_dw_b, conf_ln_g, conf_ln_b, s, c,
        comm=_join(second_stage(grp_12, part_12), first_stage(grp_3)),
    )
    (wouta, wpw, wo, wg), part_3 = got[: len(grp_12)], got[len(grp_12) :]

    def ep_merge(accs, ex, os_):
        sa = _sigmoid(ex[0][...].astype(F32))
        sb = _sigmoid(ex[1][...].astype(F32))
        ya = accs[0]
        yb = accs[1] + ex[2][...]
        os_[0][...] = (sa * ya + sb * yb).astype(BF16)
        os_[1][...] = ya.astype(BF16)
        os_[2][...] = yb.astype(BF16)

    gate_a_spec = pl.BlockSpec((tm, tn), lambda i, j, k: (i, ga_blk + j))
    gate_b_spec = pl.BlockSpec((tm, tn), lambda i, j, k: (i, gb_blk + j))
    out_sd = (SDS((s, d), BF16), pl.BlockSpec((tm, tn), ij))
    (m_act, ya, yb), got = _fmm(
        "merge", (s // tm, d // tn, 1),
        [(ya_in, pl.BlockSpec((tm, c), row_i)), (wouta, pl.BlockSpec((c, tn), lambda i, j, k: (0, j))),
         (v_act, pl.BlockSpec((tm, c), row_i)), (wpw, pl.BlockSpec((c, tn), lambda i, j, k: (0, j)))],
        [(0, 1, NN, 0, None), (2, 3, NN, 1, None)], [(tm, tn), (tm, tn)],
        [(proj, gate_a_spec), (proj, gate_b_spec), (b_pw_b, pl.BlockSpec((1, tn), lambda i, j, k: (0, j)))],
        [out_sd, out_sd, out_sd], ep_merge, csplit=EPILOGUE_CHUNK,
        comm=_join(second_stage(grp_3, part_3), first_stage(grp_4)),
    )
    (wu,), part_4 = got[: len(grp_3)], got[len(grp_3) :]

    def ep_residual(accs, ex, os_):
        os_[0][...] = accs[0] + ex[0][...]

    (h1,), got = _fmm(
        "w_o", (s // tm, d // tn, 1),
        [(m_act, pl.BlockSpec((tm, d), row_i)), (wo, pl.BlockSpec((d, tn), lambda i, j, k: (0, j)))],
        [(0, 1, NN, 0, None)], [(tm, tn)], [(x2, pl.BlockSpec((tm, tn), ij))],
        [(SDS((s, d), F32), pl.BlockSpec((tm, tn), ij))], ep_residual, csplit=EPILOGUE_CHUNK,
        comm=_join(second_stage(grp_4, part_4), first_stage(grp_5)),
    )
    (wdn,), part_5 = got[: len(grp_4)], got[len(grp_4) :]
    n2 = _rms_fwd("rms2", h1, g_ffn)

    def ep_gateup(accs, ex, os_):
        g, u = accs
        os_[0][...] = g.astype(BF16)
        os_[1][...] = u.astype(BF16)
        os_[2][...] = (g * _sigmoid(g) * u).astype(BF16)

    ff_sd = (SDS((NDEV, s, fs), BF16), pl.BlockSpec((None, tm, fs), lambda i, j, k: (j, i, 0)))
    w_col_blk = pl.BlockSpec((None, d, fs), lambda i, j, k: (j, 0, 0))
    (g_act, u_ff, f_act), (wpg, wpp) = _fmm(
        "gate_up", (s // tm, NDEV, 1),
        [(n2, pl.BlockSpec((tm, d), row_i)), (wg, w_col_blk), (wu, w_col_blk)],
        [(0, 1, NN, 0, None), (0, 2, NN, 1, None)], [(tm, fs), (tm, fs)], [],
        [ff_sd, ff_sd, ff_sd], ep_gateup,
        comm=second_stage(grp_5, part_5),
    )
    pair = 2
    (h2,), _ = _fmm(
        "down", (s // tm, d // tn, NDEV // pair),
        [(f_act, pl.BlockSpec((pair, tm, fs), lambda i, j, k: (k, i, 0))),
         (wdn, pl.BlockSpec((pair, fs, tn), lambda i, j, k: (k, 0, j)))],
        [(0, 1, NN, 0, None, pair)], [(tm, tn)], [(h1, pl.BlockSpec((tm, tn), ij))],
        [(SDS((s, d), F32), pl.BlockSpec((tm, tn), ij))], ep_residual,
    )
    n3 = _rms_fwd("rms3", h2, g_ple)

    tr = _tile(s, 256)
    n_r = s // tr
    rows = lambda i, j, k: (i, 0)
    whole = lambda i, j, k: (0, 0)
    part_spec = lambda nrow: pl.BlockSpec((None, nrow, d), lambda i, j, k: (i, 0, 0))

    def ep_ple(accs, ex, os_):
        h2_, t_, gf = ex[0][...], ex[1][...], ex[2][...]
        ple = accs[0]
        s3 = _sigmoid(accs[1])
        h3 = h2_ + s3 * ple
        r = lax.rsqrt(jnp.mean(h3 * h3, axis=-1, keepdims=True) + EPS)
        hn = h3 * r
        e = hn * gf - t_
        loss = 0.5 * jnp.sum(jnp.mean(e * e, axis=-1, keepdims=True), axis=0, keepdims=True)
        dy = e * (1.0 / d)
        dn = dy * gf
        dh3 = r * (dn - hn * jnp.mean(dn * hn, axis=-1, keepdims=True))
        os_[0][...] = dh3
        os_[1][...] = (dh3 * s3).astype(BF16)
        os_[2][...] = (dh3 * ple * s3 * (1.0 - s3)).astype(BF16)
        os_[3][0:1, :] = jnp.sum(dy * hn, axis=0, keepdims=True)
        os_[3][1:2, :] = jnp.broadcast_to(loss, (1, d))

    (dh3, d_ple, d_g3, part_fin), _ = _fmm(
        "ple_loss", (n_r, 1, 1),
        [(p2, pl.BlockSpec((tr, pdim), rows)), (wpp, pl.BlockSpec((pdim, d), whole)),
         (n3, pl.BlockSpec((tr, d), rows)), (wpg, pl.BlockSpec((d, d), whole))],
        [(0, 1, NN, 0, None), (2, 3, NN, 1, None)], [(tr, d), (tr, d)],
        [(h2, pl.BlockSpec((tr, d), rows)), (tgt, pl.BlockSpec((tr, d), rows)), (gfin, pl.BlockSpec((1, d), whole))],
        [(SDS((s, d), F32), pl.BlockSpec((tr, d), rows)), (SDS((s, d), BF16), pl.BlockSpec((tr, d), rows)),
         (SDS((s, d), BF16), pl.BlockSpec((tr, d), rows)), (SDS((n_r, 2, d), F32), part_spec(2))],
        ep_ple,
    )

    g_wpp = _mm_tn("d_w_ple_proj", p2, d_ple)
    g_wpg = _mm_tn("d_w_ple_gate", n3, d_g3)

    def ep_norm_bwd(accs, ex, os_):
        dh, dg = _rms_bwd(accs[0], ex[0][...], ex[2][...])
        dh = ex[1][...] + dh
        os_[0][...] = dh
        os_[1][...] = dh.astype(BF16)
        os_[2][...] = dg

    norm_outs = lambda t: [
        (SDS((s, d), F32), pl.BlockSpec((t, d), rows)), (SDS((s, d), BF16), pl.BlockSpec((t, d), rows)),
        (SDS((s // t, 1, d), F32), part_spec(1)),
    ]
    def exchange1(names, wholes):
        return _rs1(wholes, kinds_(names), shapes_(names))

    def pair_sums(names, wholes, got):
        return [_pair_sum("pair_sum_" + nm, wholes[t], kind_of[nm], got[t]) for t, nm in enumerate(names)]

    lands = {}
    grp1 = ["w_ple_proj", "w_ple_gate"]
    (dh2, dh2b, part_ple), got = _fmm(
        "d_n3", (n_r, 1, 1),
        [(d_g3, pl.BlockSpec((tr, d), rows)), (wpg, pl.BlockSpec((d, d), whole))],
        [(0, 1, NT, 0, None)], [(tr, d)],
        [(h2, pl.BlockSpec((tr, d), rows)), (dh3, pl.BlockSpec((tr, d), rows)), (g_ple, pl.BlockSpec((1, d), whole))],
        norm_outs(tr), ep_norm_bwd,
        comm=exchange1(grp1, [g_wpp, g_wpg]),
    )
    sums1 = pair_sums(grp1, [g_wpp, g_wpg], got)

    def ep_ddown(accs, ex, os_):
        g = ex[0][...].astype(F32)
        u = ex[1][...].astype(F32)
        sg = _sigmoid(g)
        df = accs[0]
        os_[0][...] = (df * u * sg * (1.0 + g * (1.0 - sg))).astype(BF16)
        os_[1][...] = (df * g * sg).astype(BF16)

    ff_in = pl.BlockSpec((None, tm, fs), lambda i, j, k: (j, i, 0))
    (d_g, d_u), got = _fmm(
        "d_down", (s // tm, NDEV, 1),
        [(dh2b, pl.BlockSpec((tm, d), row_i)), (wdn, pl.BlockSpec((None, fs, d), lambda i, j, k: (j, 0, 0)))],
        [(0, 1, NT, 0, None)], [(tm, fs)], [(g_act, ff_in), (u_ff, ff_in)],
        [ff_sd, ff_sd], ep_ddown, csplit=EPILOGUE_CHUNK,
        comm=_rs2(sums1),
    )
    lands.update(zip(grp1, got))
    tk = _tile(s, 1024)
    (g_wdn,), _ = _fmm(
        "d_w_down", (NDEV, 1, s // tk),
        [(f_act, pl.BlockSpec((None, tk, fs), lambda i, j, k: (i, k, 0))), (dh2b, pl.BlockSpec((tk, d), lambda i, j, k: (k, 0)))],
        [(0, 1, TN, 0, None)], [(fs, d)], [],
        [(SDS((NDEV, fs, d), BF16), pl.BlockSpec((None, fs, d), lambda i, j, k: (i, 0, 0)))], _ep_bf16,
    )

    def ep_two_bf16(accs, ex, os_):
        os_[0][...] = accs[0].astype(BF16)
        os_[1][...] = accs[1].astype(BF16)

    ff_k = pl.BlockSpec((None, tk, fs), lambda i, j, k: (i, k, 0))
    wcol_sd = (SDS((NDEV, d, fs), BF16), pl.BlockSpec((None, d, fs), lambda i, j, k: (i, 0, 0)))
    grp2 = ["w_down"]
    (g_wg, g_wu), got = _fmm(
        "d_w_gate_up", (NDEV, 1, s // tk),
        [(n2, pl.BlockSpec((tk, d), lambda i, j, k: (k, 0))), (d_g, ff_k), (d_u, ff_k)],
        [(0, 1, TN, 0, None), (0, 2, TN, 1, None)], [(d, fs), (d, fs)], [],
        [wcol_sd, wcol_sd], ep_two_bf16,
        comm=exchange1(grp2, [g_wdn]),
    )
    sums2 = pair_sums(grp2, [g_wdn], got)
    grp3 = ["w_gate", "w_up"]
    th = _tile(s // 2, 1024)
    ff_a = pl.BlockSpec((None, th, fs), lambda i, j, k: (k, i, 0))
    w_k = pl.BlockSpec((None, d, fs), lambda i, j, k: (k, 0, 0))
    (d_n2,), got = _fmm(
        "d_n2", (s // th, 1, NDEV),
        [(d_g, ff_a), (wg, w_k), (d_u, ff_a), (wu, w_k)],
        [(0, 1, NT, 0, None), (2, 3, NT, 0, None)], [(th, d)], [],
        [(SDS((s, d), BF16), pl.BlockSpec((th, d), rows))], _ep_bf16,
        comm=_join(_rs2(sums2), exchange1(grp3, [g_wg, g_wu])),
    )
    lands.update(zip(grp2, got[:1]))
    sums3 = pair_sums(grp3, [g_wg, g_wu], got[1:])
    dh1, dh1b, part_ffn = _norm_bwd("d_h1", d_n2, h1, dh2, g_ffn, True)
    g_wo = _mm_tn("d_w_o", m_act, dh1b)

    def ep_dm(accs, ex, os_):
        ya_, yb_ = ex[0][...].astype(F32), ex[1][...].astype(F32)
        sa = _sigmoid(ex[2][...].astype(F32))
        sb = _sigmoid(ex[3][...].astype(F32))
        dm = accs[0]
        d_yb = dm * sb
        os_[0][...] = (dm * sa).astype(BF16)
        os_[1][...] = d_yb.astype(BF16)
        os_[2][...] = (dm * ya_ * sa * (1.0 - sa)).astype(BF16)
        os_[3][...] = (dm * yb_ * sb * (1.0 - sb)).astype(BF16)
        os_[4][...] = jnp.sum(d_yb, axis=0, keepdims=True)

    tile_ij = pl.BlockSpec((tm, tn), ij)
    grp4 = ["w_o"]
    (d_ya, d_yb, d_ga, d_gb, part_bpw), got = _fmm(
        "d_merge", (s // tm, d // tn, 1),
        [(dh1b, pl.BlockSpec((tm, d), row_i)), (wo, pl.BlockSpec((tn, d), lambda i, j, k: (j, 0)))],
        [(0, 1, NT, 0, None)], [(tm, tn)],
        [(ya, tile_ij), (yb, tile_ij), (proj, gate_a_spec), (proj, gate_b_spec)],
        [out_sd, out_sd, out_sd, out_sd,
         (SDS((s // tm, 1, d), F32), pl.BlockSpec((None, 1, tn), lambda i, j, k: (i, 0, j)))],
        ep_dm, csplit=EPILOGUE_CHUNK,
        comm=exchange1(grp4, [g_wo]),
    )
    sums4 = pair_sums(grp4, [g_wo], got)
    g_wouta = _mm_tn("d_w_out_a", ya_in, d_ya)
    g_wpw = _mm_tn("d_w_pw_b", v_act, d_yb)
    grp5 = ["w_out_a", "w_pw_b"]
    d_ya_in, got = _mm_nt("d_ya_in", d_ya, wouta, comm=exchange1(grp5, [g_wouta, g_wpw]))
    sums5 = pair_sums(grp5, [g_wouta, g_wpw], got)
    d_v, _ = _mm_nt("d_v", d_yb, wpw)
    d_cv, part_ln = _mix_b_bwd1(d_v, cv, conf_ln_g, conf_ln_b, s, c)
    (d_b, part_wd, part_bglu), got = _mix_b_bwd2(d_cv, u_act, proj, b_glu, wd, s, c, comm=_rs2(sums3))
    lands.update(zip(grp3, got))
    d_a, part_wa = _mix_a_bwd(d_ya_in, proj, wa, s, c)

    nb = nin // c
    gblk = d // c
    lo = [0, 3, 5, 5 + gblk]
    hi = [3, 5, 5 + gblk, 5 + 2 * gblk]
    pieces = [d_a, d_b, d_ga, d_gb]

    def active(q, ax):
        return lambda ids: jnp.logical_and(ids[ax] >= lo[q], ids[ax] < hi[q])

    def piece_spec(q, rows_, ax, row0=0):
        def index(i, j, k):
            ids = (i, j, k)
            col = jnp.clip(ids[ax] - lo[q], 0, hi[q] - lo[q] - 1)
            row = i + row0 if ax == 2 else jnp.where(active(q, ax)(ids), k, 0)
            return (row, col)

        return pl.BlockSpec((rows_, c), index)

    tkw = _tile(s, 1024)
    (g_win,), got = _fmm(
        "d_w_in", (1, nb, s // tkw),
        [(n1, pl.BlockSpec((tkw, d), lambda i, j, k: (k, 0)))]
        + [(pieces[q], piece_spec(q, tkw, 1)) for q in range(4)],
        [(0, 1 + q, TN, 0, active(q, 1)) for q in range(4)], [(d, c)], [],
        [(SDS((d, nin), BF16), pl.BlockSpec((d, c), lambda i, j, k: (0, j)))], _ep_bf16,
        comm=_rs2(sums4 + sums5),
    )
    lands.update(zip(grp4 + grp5, got))

    grp6 = ["w_in"]
    n_half = max(1, 3 * (s // th) // 8)

    def d_n1_rows(name, row0, n_tiles, comm, into):
        return _fmm(
            name, (n_tiles, 1, nb),
            [(pieces[q], piece_spec(q, th, 2, row0)) for q in range(4)]
            + [(win, pl.BlockSpec((d, c), lambda i, j, k: (0, k)))],
            [(q, 4, NT, 0, active(q, 2)) for q in range(4)], [(th, d)], [],
            [(SDS((s, d), BF16), pl.BlockSpec((th, d), lambda i, j, k: (i + row0, 0)))], _ep_bf16,
            comm=comm, into=into,
        )

    (d_n1,), got = d_n1_rows("d_n1_a", 0, n_half, exchange1(grp6, [g_win]), None)
    (d_n1,), got = d_n1_rows("d_n1_b", n_half, s // th - n_half, _rs2(pair_sums(grp6, [g_win], got)), d_n1)
    lands.update(zip(grp6, got))
    dx, part_mix = _norm_bwd("d_x", d_n1, x2, dh1, g_mix, False)

    small_parts = [
        jnp.sum(part_mix, axis=0),
        jnp.sum(part_bglu, axis=0),
        jnp.sum(part_ln[:, 2], axis=0),
        jnp.sum(part_ln[:, 0], axis=0),
        jnp.sum(part_ln[:, 1], axis=0),
        jnp.sum(part_bpw, axis=0),
        jnp.sum(part_ffn, axis=0),
        jnp.sum(part_ple, axis=0),
        jnp.sum(part_fin[:, 0], axis=0),
        jnp.sum(part_wa, axis=0),
        jnp.sum(part_wd, axis=0),
        jnp.broadcast_to(jnp.sum(part_fin[:, 1, 0]), (c,)),
    ]
    small_shapes = [(1, d), (1, 2 * c), (1, c), (1, c), (1, c), (1, d), (1, d), (1, d), (d,), (kpa, c), (kpb, c), (c,)]
    total = _all_reduce_small(_pack(small_parts, c))
    (gr_g_mix, gr_b_glu, gr_dw_b, gr_ln_g, gr_ln_b, gr_b_pw, gr_g_ffn, gr_g_ple, gr_g_final, gr_wa, gr_wd, loss_row) = _unpack(total, small_shapes)
    loss = loss_row[0]
    my = _dev_index(_place())
    csh = conv_a_w.shape[-1]
    gr_conv_a = lax.dynamic_slice_in_dim(gr_wa[:CONV_A_K], my * csh, csh, axis=1)[None]
    gr_conf_dw = lax.dynamic_slice_in_dim(gr_wd[:CONF_K], my * csh, csh, axis=1)[None]

    big_m = dict(w_in=m_w_in, w_out_a=m_w_out_a, w_pw_b=m_w_pw_b, w_ple_proj=m_w_ple_proj, w_o=m_w_o,
                 w_ple_gate=m_w_ple_gate, w_gate=m_w_gate, w_up=m_w_up, w_down=m_w_down)
    big_v = dict(w_in=v_w_in, w_out_a=v_w_out_a, w_pw_b=v_w_pw_b, w_ple_proj=v_w_ple_proj, w_o=v_w_o,
                 w_ple_gate=v_w_ple_gate, w_gate=v_w_gate, w_up=v_w_up, w_down=v_w_down)
    big_out = {}
    for nm in weight:
        res = _adamw_big("adamw_" + nm, lands[nm], weight[nm][0], big_m[nm][0], big_v[nm][0])
        big_out[nm] = [r[None] for r in res]

    small_names = ["g_mix", "conv_a_w", "b_glu", "conf_dw_w", "conf_dw_b", "conf_ln_g", "conf_ln_b", "b_pw_b", "g_ffn", "g_ple", "g_final"]
    small_g = [gr_g_mix, gr_conv_a, gr_b_glu, gr_conf_dw, gr_dw_b, gr_ln_g, gr_ln_b, gr_b_pw, gr_g_ffn, gr_g_ple, gr_g_final]
    small_w = [g_mix, conv_a_w, b_glu, conf_dw_w, conf_dw_b, conf_ln_g, conf_ln_b, b_pw_b, g_ffn, g_ple, g_final]
    small_m = [m_g_mix, m_conv_a_w, m_b_glu, m_conf_dw_w, m_conf_dw_b, m_conf_ln_g, m_conf_ln_b, m_b_pw_b, m_g_ffn, m_g_ple, m_g_final]
    small_v = [v_g_mix, v_conv_a_w, v_b_glu, v_conf_dw_w, v_conf_dw_b, v_conf_ln_g, v_conf_ln_b, v_b_pw_b, v_g_ffn, v_g_ple, v_g_final]
    shp = [tuple(w.shape) for w in small_w]
    small_g = [g.reshape(sh) for g, sh in zip(small_g, shp)]
    sd, sm, sv = _adamw_small(_pack(small_g, 128), _pack(small_w, 128), _pack(small_m, 128), _pack(small_v, 128))
    small_out = {}
    for nm, g, dl, mm, vv in zip(small_names, small_g, _unpack(sd, shp), _unpack(sm, shp), _unpack(sv, shp)):
        small_out[nm] = [g, dl, mm, vv]

    order = ["g_mix", "w_in", "conv_a_w", "w_out_a", "b_glu", "conf_dw_w", "conf_dw_b", "conf_ln_g", "conf_ln_b", "w_pw_b", "b_pw_b", "w_o", "g_ffn", "w_gate", "w_up", "w_down", "g_ple", "w_ple_gate", "w_ple_proj", "g_final"]
    allo = {**big_out, **small_out}
    outs = [loss, dx[None]]
    for q in range(4):
        outs += [allo[nm][q] for nm in order]
    return tuple(outs)
```

```python
import jax
import jax.numpy as jnp
from jax import lax
from jax.experimental import pallas as pl
from jax.experimental.pallas import tpu as pltpu

F32, BF16 = jnp.float32, jnp.bfloat16
EPS, LN_EPS = 1e-6, 1e-5
ADAM_LR, ADAM_B1, ADAM_B2, ADAM_EPS, ADAM_WD, ADAM_STEP = 0.001, 0.9, 0.999, 1e-08, 0.01, 10
CONV_A_K, CONF_K = 3, 31
NDEV = 8
NN = (((1,), (0,)), ((), ()))
NT = (((1,), (1,)), ((), ()))
TN = (((0,), (0,)), ((), ()))
V7X_VMEM_LIMIT_BYTES = 56 * 1024 * 1024
MESH = pl.DeviceIdType.MESH
SDS = jax.ShapeDtypeStruct
HALO_A, HALO_B = 16, 32
EPILOGUE_CHUNK = 256
CONV_ROWS = 32
CONF_ROWS = 16


def _tile(n, pref):
    t = min(n, pref)
    while n % t:
        t -= 8
    return t


def _sigmoid(x):
    return jax.nn.sigmoid(x)


def _params(sem=None):
    return pltpu.CompilerParams(vmem_limit_bytes=V7X_VMEM_LIMIT_BYTES, dimension_semantics=sem)


def _edge(grid, last):
    cond = None
    for ax, n in enumerate(grid):
        here = pl.program_id(ax) == (n - 1 if last else 0)
        cond = here if cond is None else jnp.logical_and(cond, here)
    return cond


def _join(*comms):
    ins, outs, alias, sems, spans = [], [], {}, [], []
    for cm in comms:
        spans.append((len(ins), len(outs), len(sems)))
        for i, o in cm["alias"].items():
            alias[len(ins) + i] = len(outs) + o
        ins += cm["ins"]
        outs += cm["outs"]
        sems += cm["sems"]

    def run(which):
        def f(i_refs, o_refs, s_refs):
            for cm, (a, b, c_) in zip(comms, spans):
                cm[which](
                    i_refs[a : a + len(cm["ins"])], o_refs[b : b + len(cm["outs"])], s_refs[c_ : c_ + len(cm["sems"])]
                )

        return f

    return dict(ins=ins, outs=outs, alias=alias, sems=sems, start=run("start"), finish=run("finish"))


def _call(body, name, grid, in_specs, args, out_specs, out_shape, scratch=(), sem=None, comm=None, alias=None):
    n_in, n_out, n_s = len(args), len(out_shape), len(scratch)
    alias = dict(alias or {})
    if comm is None:
        res = pl.pallas_call(
            body, name=name, grid=grid, in_specs=list(in_specs), out_specs=list(out_specs), out_shape=list(out_shape),
            scratch_shapes=list(scratch), input_output_aliases=alias, compiler_params=_params(sem),
        )(*args)
        return list(res), []
    n_ci, n_co = len(comm["ins"]), len(comm["outs"])

    def wrapped(*refs):
        ins = refs[:n_in]
        ci = refs[n_in : n_in + n_ci]
        o0 = n_in + n_ci
        outs = refs[o0 : o0 + n_out]
        co = refs[o0 + n_out : o0 + n_out + n_co]
        s0 = o0 + n_out + n_co
        sc = refs[s0 : s0 + n_s]
        cs = refs[s0 + n_s :]
        pl.when(_edge(grid, False))(lambda: comm["start"](ci, co, cs))
        body(*ins, *outs, *sc)
        pl.when(_edge(grid, True))(lambda: comm["finish"](ci, co, cs))

    hbm = pl.BlockSpec(memory_space=pl.ANY)
    res = pl.pallas_call(
        wrapped,
        name=name,
        grid=grid,
        in_specs=list(in_specs) + [hbm] * n_ci,
        out_specs=list(out_specs) + [hbm] * n_co,
        out_shape=list(out_shape) + list(comm["outs"]),
        scratch_shapes=list(scratch) + list(comm["sems"]),
        input_output_aliases={**alias, **{n_in + i: n_out + o for i, o in comm["alias"].items()}},
        compiler_params=_params(("arbitrary",) * len(grid)),
    )(*args, *comm["ins"])
    return list(res[:n_out]), list(res[n_out:])


def _col_chunks(n, pref):
    out, c0 = [], 0
    while c0 < n:
        w = min(pref, n - c0)
        out.append((c0, w))
        c0 += w
    return out


def _fmm(name, grid, operands, terms, acc_shapes, extras, outs, epilogue, comm=None, csplit=None, into=None):
    n_p, n_e, n_o, n_a = len(operands), len(extras), len(outs), len(acc_shapes)
    nk = grid[-1]
    kax = len(grid) - 1
    simple = nk == 1 and all(t[4] is None for t in terms)
    alias = None
    if into is not None:
        extras = list(extras) + [(into, pl.BlockSpec(memory_space=pl.ANY))]
        alias = {n_p + n_e: 0}
        n_e += 1
    if csplit is not None:
        assert simple and into is None and all(t[2] in (NN, NT) and (len(t) <= 5 or not t[5]) for t in terms)
        tn_ = acc_shapes[0][1]
        chunks = _col_chunks(tn_, csplit)

    def dot(a, b, dims):
        if a.dtype != BF16:
            a = a.astype(BF16)
        if b.dtype != BF16:
            b = b.astype(BF16)
        return lax.dot_general(a, b, dims, preferred_element_type=F32)

    def value(refs, term):
        slabs = term[5] if len(term) > 5 else 0
        if not slabs:
            return dot(refs[term[0]][...], refs[term[1]][...], term[2])
        tot = None
        for sl in range(slabs):
            d = dot(refs[term[0]][sl], refs[term[1]][sl], term[2])
            tot = d if tot is None else tot + d
        return tot

    def always(refs):
        parts = [None] * n_a
        for term in terms:
            if term[4] is None:
                d = value(refs, term)
                parts[term[3]] = d if parts[term[3]] is None else parts[term[3]] + d
        return parts

    def chunked(refs, ex, os_, accs):
        cols = lambda ref, c0, w: ref.at[:, pl.ds(c0, w)] if ref.shape[-1] == tn_ else ref

        def dots(k):
            c0, w = chunks[k]
            parts = [None] * n_a
            for term in terms:
                b_ref = refs[term[1]]
                b = b_ref[:, pl.ds(c0, w)] if term[2] == NN else b_ref[pl.ds(c0, w), :]
                d = dot(refs[term[0]][...], b, term[2])
                parts[term[3]] = d if parts[term[3]] is None else parts[term[3]] + d
            for ai in range(n_a):
                accs[ai][k % 2, :, pl.ds(0, w)] = parts[ai]

        def finish(k):
            c0, w = chunks[k]
            vals = [accs[ai][k % 2, :, pl.ds(0, w)] for ai in range(n_a)]
            epilogue(vals, [cols(e, c0, w) for e in ex], [cols(o, c0, w) for o in os_])

        dots(0)
        for k in range(1, len(chunks)):
            dots(k)
            finish(k - 1)
        finish(len(chunks) - 1)

    def body(*refs):
        ex = refs[n_p : n_p + n_e]
        os_ = refs[n_p + n_e : n_p + n_e + n_o]
        accs = refs[n_p + n_e + n_o :]
        if simple and csplit is not None:
            chunked(refs, ex, os_, accs)
            return
        if simple:
            epilogue(always(refs), ex, os_)
            return
        ids = [pl.program_id(ax) for ax in range(len(grid))]
        k = ids[kax]

        @pl.when(k == 0)
        def _():
            for acc in accs:
                acc[...] = jnp.zeros(acc.shape, F32)

        for ai, part in enumerate(always(refs)):
            if part is not None:
                accs[ai][...] += part
        for term in terms:
            if term[4] is not None:

                def add(term=term):
                    accs[term[3]][...] += value(refs, term)

                pl.when(term[4](ids))(add)

        @pl.when(k == nk - 1)
        def _():
            epilogue([acc[...] for acc in accs], ex, os_)

    return _call(
        body,
        name,
        grid,
        [o[1] for o in operands] + [e[1] for e in extras],
        [o[0] for o in operands] + [e[0] for e in extras],
        [o[1] for o in outs],
        [o[0] for o in outs],
        scratch=[pltpu.VMEM((2, s[0], csplit), F32) for s in acc_shapes] if csplit is not None
        else [] if simple else [pltpu.VMEM(s, F32) for s in acc_shapes],
        sem=("parallel",) * kax + ("arbitrary",),
        comm=comm,
        alias=alias,
    )


def _rms_bwd(dn_raw, h, g):
    r = lax.rsqrt(jnp.mean(h * h, axis=-1, keepdims=True) + EPS)
    hn = h * r
    dg = jnp.sum(dn_raw * hn, axis=0, keepdims=True)
    dn = dn_raw * g
    dh = r * (dn - hn * jnp.mean(dn * hn, axis=-1, keepdims=True))
    return dh, dg


def _rms_fwd(name, h, g):
    s, d = h.shape
    ts = _tile(s, 512)

    def body(h_ref, g_ref, o_ref):
        x = h_ref[...]
        r = lax.rsqrt(jnp.mean(x * x, axis=-1, keepdims=True) + EPS)
        o_ref[...] = (x * r * g_ref[...]).astype(BF16)

    return pl.pallas_call(
        body,
        name=name,
        grid=(s // ts,),
        in_specs=[pl.BlockSpec((ts, d), lambda i: (i, 0)), pl.BlockSpec((1, d), lambda i: (0, 0))],
        out_specs=pl.BlockSpec((ts, d), lambda i: (i, 0)),
        out_shape=SDS((s, d), BF16),
        compiler_params=_params(("parallel",)),
    )(h, g)


def _norm_bwd(name, dn, h, dres, g, want_bf16):
    s, d = h.shape
    ts = _tile(s, 512)

    def body(dn_r, h_r, dres_r, g_r, *outs):
        dh, dg = _rms_bwd(dn_r[...].astype(F32), h_r[...], g_r[...])
        dh = dres_r[...] + dh
        outs[0][...] = dh
        if want_bf16:
            outs[1][...] = dh.astype(BF16)
        outs[-1][...] = dg

    blk = pl.BlockSpec((ts, d), lambda i: (i, 0))
    part = pl.BlockSpec((None, 1, d), lambda i: (i, 0, 0))
    return pl.pallas_call(
        body,
        name=name,
        grid=(s // ts,),
        in_specs=[blk, blk, blk, pl.BlockSpec((1, d), lambda i: (0, 0))],
        out_specs=[blk] + ([blk] if want_bf16 else []) + [part],
        out_shape=[SDS((s, d), F32)] + ([SDS((s, d), BF16)] if want_bf16 else []) + [SDS((s // ts, 1, d), F32)],
        compiler_params=_params(("parallel",)),
    )(dn, h, dres, g)


def _prev_halo(ts, hb):
    r = ts // hb
    return lambda i: jnp.maximum(i * r - 1, 0)


def _next_halo(ts, hb, s):
    r = ts // hb
    last = s // hb - 1
    return lambda i: jnp.minimum((i + 1) * r, last)


def _shift_copies(buf, sh):
    n = sh.shape[1]
    for j in range(1, 8):
        sh[j - 1, pl.ds(0, n), :] = buf[pl.ds(j, n), :]


def _tap(buf, sh, r0, off, rows):
    j = off % 8
    start = pl.multiple_of(r0 + (off - j), 8)
    if j == 0:
        return buf[pl.ds(start, rows), :]
    return sh[j - 1, pl.ds(start, rows), :]


def _mix_a_fwd(proj, wa, s, c):
    ts, hb = _tile(s, 256), HALO_A
    prev = _prev_halo(ts, hb)

    def body(ah, ab, ac, hh, hc, w, o, buf):
        i = pl.program_id(0)
        zh = hc[...].astype(F32) * hh[...].astype(F32)
        buf[pl.ds(0, hb), :] = jnp.where(i == 0, 0.0, zh)
        buf[pl.ds(hb, ts), :] = ac[...].astype(F32) * ah[...].astype(F32)
        for r0 in range(0, ts, CONV_ROWS):
            cz = jnp.zeros((CONV_ROWS, c), F32)
            for k in range(CONV_A_K):
                cz = cz + w[k : k + 1, :] * buf[pl.ds(hb + r0 - (CONV_A_K - 1) + k, CONV_ROWS), :]
            o[pl.ds(r0, CONV_ROWS), :] = (ab[pl.ds(r0, CONV_ROWS), :].astype(F32) * cz).astype(BF16)

    main = lambda cb: pl.BlockSpec((ts, c), lambda i: (i, cb))
    halo = lambda cb: pl.BlockSpec((hb, c), lambda i: (prev(i), cb))
    return pl.pallas_call(
        body,
        name="mix_a_fwd",
        grid=(s // ts,),
        in_specs=[main(0), main(1), main(2), halo(0), halo(2), pl.BlockSpec(wa.shape, lambda i: (0, 0))],
        out_specs=pl.BlockSpec((ts, c), lambda i: (i, 0)),
        out_shape=SDS((s, c), BF16),
        scratch_shapes=[pltpu.VMEM((hb + ts, c), F32)],
        compiler_params=_params(("parallel",)),
    )(proj, proj, proj, proj, proj, wa)


def _mix_b_fwd(proj, b_glu, wd, bd, lg, lb, s, c, comm=None):
    ts, hb = _tile(s, 256), HALO_B
    prev = _prev_halo(ts, hb)

    def body(gv, gg, hv, hg, bglu, w, bd_r, lg_r, lb_r, v_o, u_o, cv_o, buf, sh):
        i = pl.program_id(0)
        bv, bg = bglu[:, 0:c], bglu[:, c : 2 * c]
        uh = (hv[...].astype(F32) + bv) * _sigmoid(hg[...].astype(F32) + bg)
        buf[pl.ds(0, hb), :] = jnp.where(i == 0, 0.0, uh)
        u = (gv[...].astype(F32) + bv) * _sigmoid(gg[...].astype(F32) + bg)
        buf[pl.ds(hb, ts), :] = u
        u_o[...] = u.astype(BF16)
        _shift_copies(buf, sh)

        def chunk(ci, carry):
            r0 = pl.multiple_of(ci * CONF_ROWS, CONF_ROWS)
            acc = jnp.zeros((CONF_ROWS, c), F32)
            for k in range(CONF_K):
                acc = acc + w[k : k + 1, :] * _tap(buf, sh, r0, hb - (CONF_K - 1) + k, CONF_ROWS)
            cv_o[pl.ds(r0, CONF_ROWS), :] = acc + bd_r[...]
            return carry

        lax.fori_loop(0, ts // CONF_ROWS, chunk, 0)
        cv = cv_o[...]
        mu = jnp.mean(cv, axis=-1, keepdims=True)
        xc = cv - mu
        rs = lax.rsqrt(jnp.mean(xc * xc, axis=-1, keepdims=True) + LN_EPS)
        ln = xc * rs * lg_r[...] + lb_r[...]
        v_o[...] = (ln * _sigmoid(ln)).astype(BF16)

    main = lambda cb: pl.BlockSpec((ts, c), lambda i: (i, cb))
    halo = lambda cb: pl.BlockSpec((hb, c), lambda i: (prev(i), cb))
    full = lambda a: pl.BlockSpec(a.shape, lambda i: (0, 0))
    out = pl.BlockSpec((ts, c), lambda i: (i, 0))
    return _call(
        body,
        "mix_b_fwd",
        (s // ts,),
        [main(3), main(4), halo(3), halo(4), full(b_glu), full(wd), full(bd), full(lg), full(lb)],
        [proj, proj, proj, proj, b_glu, wd, bd, lg, lb],
        [out, out, out],
        [SDS((s, c), BF16), SDS((s, c), BF16), SDS((s, c), F32)],
        scratch=[pltpu.VMEM((hb + ts, c), F32), pltpu.VMEM((7, hb + ts - 8, c), F32)],
        sem=("parallel",),
        comm=comm,
    )


def _mix_b_bwd1(d_v, cv, lg, lb, s, c):
    ts = _tile(s, 256)

    def body(dv_r, cv_r, lg_r, lb_r, dcv_o, part_o):
        cv_ = cv_r[...]
        mu = jnp.mean(cv_, axis=-1, keepdims=True)
        xc = cv_ - mu
        rs = lax.rsqrt(jnp.mean(xc * xc, axis=-1, keepdims=True) + LN_EPS)
        xh = xc * rs
        ln = xh * lg_r[...] + lb_r[...]
        sg = _sigmoid(ln)
        d_ln = dv_r[...].astype(F32) * (sg * (1.0 + ln * (1.0 - sg)))
        dy = d_ln * lg_r[...]
        d_cv = rs * (dy - jnp.mean(dy, axis=-1, keepdims=True) - xh * jnp.mean(dy * xh, axis=-1, keepdims=True))
        dcv_o[...] = d_cv
        part_o[0:1, :] = jnp.sum(d_ln * xh, axis=0, keepdims=True)
        part_o[1:2, :] = jnp.sum(d_ln, axis=0, keepdims=True)
        part_o[2:3, :] = jnp.sum(d_cv, axis=0, keepdims=True)

    blk = pl.BlockSpec((ts, c), lambda i: (i, 0))
    full = lambda a: pl.BlockSpec(a.shape, lambda i: (0, 0))
    return pl.pallas_call(
        body,
        name="mix_b_bwd_ln",
        grid=(s // ts,),
        in_specs=[blk, blk, full(lg), full(lb)],
        out_specs=[blk, pl.BlockSpec((None, 3, c), lambda i: (i, 0, 0))],
        out_shape=[SDS((s, c), F32), SDS((s // ts, 3, c), F32)],
        compiler_params=_params(("parallel",)),
    )(d_v, cv, lg, lb)


def _mix_b_bwd2(d_cv, u, proj, b_glu, wd, s, c, comm=None):
    ts, hb = _tile(s, 256), HALO_B
    prev, nxt = _prev_halo(ts, hb), _next_halo(ts, hb, s)
    n_t = s // ts
    kp = wd.shape[0]

    def body(dcv, dcv_n, u_m, u_p, gv, gg, bglu, w, d_o, dwd_o, dbglu_o, dbuf, ubuf, dub, dsh, ush, dwacc):
        i = pl.program_id(0)
        dbuf[pl.ds(0, ts), :] = dcv[...]
        dbuf[pl.ds(ts, hb), :] = jnp.where(i == n_t - 1, 0.0, dcv_n[...])
        ubuf[pl.ds(0, hb), :] = jnp.where(i == 0, 0.0, u_p[...].astype(F32))
        ubuf[pl.ds(hb, ts), :] = u_m[...].astype(F32)
        _shift_copies(dbuf, dsh)
        _shift_copies(ubuf, ush)
        dwacc[...] = jnp.zeros(dwacc.shape, F32)

        def chunk(ci, carry):
            r0 = pl.multiple_of(ci * CONF_ROWS, CONF_ROWS)
            acc = jnp.zeros((CONF_ROWS, c), F32)
            dc = dbuf[pl.ds(r0, CONF_ROWS), :]
            for k in range(CONF_K):
                acc = acc + w[k : k + 1, :] * _tap(dbuf, dsh, r0, (CONF_K - 1) - k, CONF_ROWS)
                prod = dc * _tap(ubuf, ush, r0, hb - (CONF_K - 1) + k, CONF_ROWS)
                fold = prod[0:8]
                for a in range(1, CONF_ROWS // 8):
                    fold = fold + prod[8 * a : 8 * a + 8]
                dwacc[pl.ds(8 * k, 8), :] += fold
            dub[pl.ds(r0, CONF_ROWS), :] = acc
            return carry

        lax.fori_loop(0, ts // CONF_ROWS, chunk, 0)
        for k in range(CONF_K):
            dwd_o[k : k + 1, :] = jnp.sum(dwacc[pl.ds(8 * k, 8), :], axis=0, keepdims=True)
        dwd_o[CONF_K:kp, :] = jnp.zeros((kp - CONF_K, c), F32)
        bv, bg = bglu[:, 0:c], bglu[:, c : 2 * c]
        d_u = dub[...]
        sg = _sigmoid(gg[...].astype(F32) + bg)
        d_gv = d_u * sg
        d_gg = d_u * (gv[...].astype(F32) + bv) * sg * (1.0 - sg)
        d_o[:, 0:c] = d_gv.astype(BF16)
        d_o[:, c : 2 * c] = d_gg.astype(BF16)
        dbglu_o[:, 0:c] = jnp.sum(d_gv, axis=0, keepdims=True)
        dbglu_o[:, c : 2 * c] = jnp.sum(d_gg, axis=0, keepdims=True)

    blk = lambda cb: pl.BlockSpec((ts, c), lambda i: (i, cb))
    full = lambda a: pl.BlockSpec(a.shape, lambda i: (0, 0))
    return _call(
        body,
        "mix_b_bwd_conv",
        (n_t,),
        [
            blk(0),
            pl.BlockSpec((hb, c), lambda i: (nxt(i), 0)),
            blk(0),
            pl.BlockSpec((hb, c), lambda i: (prev(i), 0)),
            blk(3),
            blk(4),
            full(b_glu),
            full(wd),
        ],
        [d_cv, d_cv, u, u, proj, proj, b_glu, wd],
        [
            pl.BlockSpec((ts, 2 * c), lambda i: (i, 0)),
            pl.BlockSpec((None, kp, c), lambda i: (i, 0, 0)),
            pl.BlockSpec((None, 1, 2 * c), lambda i: (i, 0, 0)),
        ],
        [SDS((s, 2 * c), BF16), SDS((n_t, kp, c), F32), SDS((n_t, 1, 2 * c), F32)],
        scratch=[
            pltpu.VMEM((ts + hb, c), F32), pltpu.VMEM((hb + ts, c), F32), pltpu.VMEM((ts, c), F32),
            pltpu.VMEM((7, hb + ts - 8, c), F32), pltpu.VMEM((7, hb + ts - 8, c), F32), pltpu.VMEM((8 * CONF_K, c), F32),
        ],
        sem=("parallel",),
        comm=comm,
    )


def _mix_a_bwd(d_ya, proj, wa, s, c):
    ts, hb = _tile(s, 256), HALO_A
    prev, nxt = _prev_halo(ts, hb), _next_halo(ts, hb, s)
    n_t = s // ts
    kp = wa.shape[0]

    def body(dya, dya_n, ah, ab, ac, ah_p, ac_p, ab_n, w, d_o, dwa_o, zbuf, dbuf, dzb):
        i = pl.program_id(0)
        zbuf[pl.ds(0, hb), :] = jnp.where(i == 0, 0.0, ac_p[...].astype(F32) * ah_p[...].astype(F32))
        zbuf[pl.ds(hb, ts), :] = ac[...].astype(F32) * ah[...].astype(F32)
        dbuf[pl.ds(0, ts), :] = dya[...].astype(F32) * ab[...].astype(F32)
        dbuf[pl.ds(ts, hb), :] = jnp.where(i == n_t - 1, 0.0, dya_n[...].astype(F32) * ab_n[...].astype(F32))
        dw_rows = [jnp.zeros((1, c), F32) for _ in range(CONV_A_K)]
        for r0 in range(0, ts, CONV_ROWS):
            cz = jnp.zeros((CONV_ROWS, c), F32)
            dz = jnp.zeros((CONV_ROWS, c), F32)
            dc = dbuf[pl.ds(r0, CONV_ROWS), :]
            for k in range(CONV_A_K):
                zk = zbuf[pl.ds(hb + r0 - (CONV_A_K - 1) + k, CONV_ROWS), :]
                cz = cz + w[k : k + 1, :] * zk
                dz = dz + w[k : k + 1, :] * dbuf[pl.ds(r0 + (CONV_A_K - 1) - k, CONV_ROWS), :]
                dw_rows[k] = dw_rows[k] + jnp.sum(dc * zk, axis=0, keepdims=True)
            d_o[pl.ds(r0, CONV_ROWS), c : 2 * c] = (dya[pl.ds(r0, CONV_ROWS), :].astype(F32) * cz).astype(BF16)
            dzb[pl.ds(r0, CONV_ROWS), :] = dz
        d_z = dzb[...]
        d_o[:, 0:c] = (d_z * ac[...].astype(F32)).astype(BF16)
        d_o[:, 2 * c : 3 * c] = (d_z * ah[...].astype(F32)).astype(BF16)
        for k in range(CONV_A_K):
            dwa_o[k : k + 1, :] = dw_rows[k]
        dwa_o[CONV_A_K:kp, :] = jnp.zeros((kp - CONV_A_K, c), F32)

    blk = lambda cb: pl.BlockSpec((ts, c), lambda i: (i, cb))
    hp = lambda cb: pl.BlockSpec((hb, c), lambda i: (prev(i), cb))
    hn = lambda cb: pl.BlockSpec((hb, c), lambda i: (nxt(i), cb))
    return pl.pallas_call(
        body,
        name="mix_a_bwd",
        grid=(n_t,),
        in_specs=[blk(0), hn(0), blk(0), blk(1), blk(2), hp(0), hp(2), hn(1), pl.BlockSpec(wa.shape, lambda i: (0, 0))],
        out_specs=[pl.BlockSpec((ts, 3 * c), lambda i: (i, 0)), pl.BlockSpec((None, kp, c), lambda i: (i, 0, 0))],
        out_shape=[SDS((s, 3 * c), BF16), SDS((n_t, kp, c), F32)],
        scratch_shapes=[pltpu.VMEM((hb + ts, c), F32), pltpu.VMEM((ts + hb, c), F32), pltpu.VMEM((ts, c), F32)],
        compiler_params=_params(("parallel",)),
    )(d_ya, d_ya, proj, proj, proj, proj, proj, proj, wa)


def _ep_bf16(accs, ex, os_):
    os_[0][...] = accs[0].astype(BF16)


def _mm_tn(name, a, b, tm=2048, tn=1024, tk=1024):
    m, k1 = a.shape
    n = b.shape[1]
    tm, tn, tk = _tile(k1, tm), _tile(n, tn), _tile(m, tk)
    return _fmm(
        name,
        (k1 // tm, n // tn, m // tk),
        [(a, pl.BlockSpec((tk, tm), lambda i, j, k: (k, i))), (b, pl.BlockSpec((tk, tn), lambda i, j, k: (k, j)))],
        [(0, 1, TN, 0, None)],
        [(tm, tn)],
        [],
        [(SDS((k1, n), BF16), pl.BlockSpec((tm, tn), lambda i, j, k: (i, j)))],
        _ep_bf16,
    )[0][0]


def _mm_nt(name, a, b, tm=1024, tn=1024, comm=None):
    m, kk = a.shape
    n = b.shape[0]
    tm, tn = _tile(m, tm), _tile(n, tn)
    outs, couts = _fmm(
        name,
        (m // tm, n // tn, 1),
        [(a, pl.BlockSpec((tm, kk), lambda i, j, k: (i, 0))), (b, pl.BlockSpec((tn, kk), lambda i, j, k: (j, 0)))],
        [(0, 1, NT, 0, None)],
        [(tm, tn)],
        [],
        [(SDS((m, n), BF16), pl.BlockSpec((tm, tn), lambda i, j, k: (i, j)))],
        _ep_bf16,
        comm=comm,
    )
    return outs[0], couts


def _dev_index(dev):
    return 4 * dev[0] + 2 * dev[1] + dev[2]


def _region(ref, kind, j, shard_shape):
    if kind == "col":
        ns = shard_shape[1]
        return ref.at[:, pl.ds(pl.multiple_of(j * ns, 128), ns)]
    if kind == "row":
        rs = shard_shape[0]
        return ref.at[pl.ds(pl.multiple_of(j * rs, 8), rs), :]
    return ref.at[j]


def _whole_shape(kind, shard_shape):
    if kind == "col":
        return (shard_shape[0], NDEV * shard_shape[1])
    if kind == "row":
        return (NDEV * shard_shape[0], shard_shape[1])
    return (NDEV,) + tuple(shard_shape)


def _place():
    return lax.axis_index("x"), lax.axis_index("y"), lax.axis_index("c")


def _proj_gather(n1, w_shard, early, late):
    s, d = n1.shape
    ns = w_shard.shape[1]
    pw = 2 * ns
    tm = _tile(s // 2, 512)
    n_i = s // tm
    comm = _join(early, late)
    n_early = (len(early["ins"]), len(early["outs"]), len(early["sems"]))
    assert n_i >= 2 and not comm["alias"]
    x0, y0, _ = _place()
    order = jnp.stack([2 * x0 + y0, 2 * x0 + (1 - y0), 2 * (1 - x0) + y0, 2 * (1 - x0) + (1 - y0)]).astype(jnp.int32)
    n_ci, n_co = len(comm["ins"]), len(comm["outs"])

    def body(order_ref, n1_ref, wsh_ref, *rest):
        ci = rest[:n_ci]
        proj_ref, win_ref = rest[n_ci], rest[n_ci + 1]
        co = rest[n_ci + 2 : n_ci + 2 + n_co]
        wfull, send, recv, fsend, frecv, loc, osem = rest[n_ci + 2 + n_co : n_ci + 9 + n_co]
        cs = rest[n_ci + 9 + n_co :]
        u, i = pl.program_id(0), pl.program_id(1)
        x, y, c = _place()
        sib = (x, y, 1 - c)
        chips = [(x, y), (x, 1 - y), (1 - x, y), (1 - x, 1 - y)]
        peers = [sib] + [(*ch, c) for ch in chips[1:]]
        blk = lambda ch, core: wfull.at[2 * ch[0] + ch[1], :, pl.ds(pl.multiple_of(core * ns, 128), ns)]
        sends = [_remote(blk(chips[0], c), blk(chips[0], c), send.at[k], recv.at[k], peers[k]) for k in range(4)]
        arrivals = [_remote(blk(chips[0], 1 - c), blk(chips[0], 1 - c), send.at[0], recv.at[0], sib)] + [
            _remote(blk(chips[k], c), blk(chips[k], c), send.at[k], recv.at[k], peers[k]) for k in range(1, 4)
        ]
        passes = [_remote(blk(chips[k], c), blk(chips[k], c), fsend.at[k - 1], frecv.at[k - 1], sib) for k in range(1, 4)]
        passed = [_remote(blk(chips[k], 1 - c), blk(chips[k], 1 - c), fsend.at[k - 1], frecv.at[k - 1], sib) for k in range(1, 4)]
        mine = lambda: pltpu.make_async_copy(wsh_ref, blk(chips[0], c), loc.at[0])

        def to_hbm(unit):
            q = order_ref[unit]
            return pltpu.make_async_copy(wfull.at[q], win_ref.at[:, pl.ds(pl.multiple_of(q * pw, 128), pw)], osem.at[unit])

        a, b, e = n_early
        early_refs = (ci[:a], co[:b], cs[:e])
        late_refs = (ci[a:], co[b:], cs[e:])

        @pl.when(jnp.logical_and(u == 0, i == 0))
        def _():
            mine().start()
            mine().wait()
            for snd in sends[:3]:
                snd().start()
            early["start"](*early_refs)
            arrivals[0]().wait_recv()

        @pl.when(jnp.logical_and(u == 1, i == 0))
        def _():
            sends[3]().start()

        @pl.when(jnp.logical_and(u == 2, i == 0))
        def _():
            late["start"](*late_refs)

        for nxt in range(1, 4):

            @pl.when(jnp.logical_and(u == nxt - 1, i == n_i - 1))
            def _(nxt=nxt):
                passed[nxt - 1]().wait_recv()

        proj_ref[...] = jnp.dot(n1_ref[...], wfull[order_ref[u]], preferred_element_type=F32).astype(BF16)

        for nxt in range(1, 4):

            @pl.when(jnp.logical_and(u == nxt - 1, i == n_i - 2))
            def _(nxt=nxt):
                arrivals[nxt]().wait_recv()
                passes[nxt - 1]().start()

        for unit in range(4):

            @pl.when(jnp.logical_and(u == unit, i == n_i - 1))
            def _(unit=unit):
                to_hbm(unit).start()

        @pl.when(jnp.logical_and(u == 3, i == n_i - 1))
        def _():
            for snd in sends + passes:
                snd().wait_send()
            for unit in range(4):
                to_hbm(unit).wait()
            comm["finish"](ci, co, cs)

    hbm = pl.BlockSpec(memory_space=pl.ANY)
    dma = pltpu.SemaphoreType.DMA
    res = pl.pallas_call(
        body,
        name="proj",
        grid_spec=pltpu.PrefetchScalarGridSpec(
            num_scalar_prefetch=1,
            grid=(4, n_i),
            in_specs=[pl.BlockSpec((tm, d), lambda u, i, order_ref: (i, 0)), hbm] + [hbm] * n_ci,
            out_specs=[pl.BlockSpec((tm, pw), lambda u, i, order_ref: (i, order_ref[u])), hbm] + [hbm] * n_co,
            scratch_shapes=[pltpu.VMEM((4, d, pw), BF16), dma((4,)), dma((4,)), dma((3,)), dma((3,)), dma((1,)), dma((4,))]
            + list(comm["sems"]),
        ),
        out_shape=[SDS((s, NDEV * ns), BF16), SDS((d, NDEV * ns), BF16)] + list(comm["outs"]),
        compiler_params=_params(("arbitrary", "arbitrary")),
    )(order, n1, w_shard, *comm["ins"])
    return res[0], res[1], list(res[2:])


def _peer(me, r):
    x, y, c = me
    return (1 - x if r & 4 else x, 1 - y if r & 2 else y, 1 - c if r & 1 else c)


def _remote(src, dst, send_sem, recv_sem, to):
    return lambda: pltpu.make_async_remote_copy(
        src_ref=src, dst_ref=dst, send_sem=send_sem, recv_sem=recv_sem, device_id=to, device_id_type=MESH
    )


def _run(pairs, locals_, start):
    if start:
        for cp in locals_:
            cp.start()
        for snd, _ in pairs:
            snd().start()
    else:
        for snd, arr in pairs:
            arr().wait_recv()
            snd().wait_send()
        for cp in locals_:
            cp.wait()


def _stage(ins, outs, alias, sems, build):
    return dict(
        ins=list(ins), outs=list(outs), alias=alias, sems=list(sems),
        start=lambda i, o, s: _run(*build(i, o, s), True),
        finish=lambda i, o, s: _run(*build(i, o, s), False),
    )


def _ag1(shards, kinds):
    n_t = len(shards)
    shapes = [tuple(sh.shape) for sh in shards]

    def build(srcs, dsts, sems):
        send, recv, loc = sems
        x, y, c = _place()
        me = (x, y, c)
        peers = [(x, y, 1 - c), (1 - x, y, c), (x, 1 - y, c), (1 - x, 1 - y, c)]
        reg = lambda t, dev: _region(dsts[t], kinds[t], _dev_index(dev), shapes[t])
        pairs = []
        for t in range(n_t):
            for k, peer in enumerate(peers):
                snd = _remote(srcs[t], reg(t, me), send.at[t, k], recv.at[t, k], peer)
                arr = _remote(reg(t, peer), reg(t, peer), send.at[t, k], recv.at[t, k], peer)
                pairs.append((snd, arr))
        mine = [pltpu.make_async_copy(srcs[t], reg(t, me), loc.at[t]) for t in range(n_t)]
        return pairs, mine

    outs = [SDS(_whole_shape(kinds[t], shapes[t]), shards[t].dtype) for t in range(n_t)]
    dma = pltpu.SemaphoreType.DMA
    return _stage(shards, outs, {}, [dma((n_t, 4)), dma((n_t, 4)), dma((n_t,))], build)


def _ag_direct(shards, kinds):
    n_t = len(shards)
    shapes = [tuple(sh.shape) for sh in shards]

    def build(srcs, dsts, sems):
        send, recv, loc = sems
        me = _place()
        reg = lambda t, dev: _region(dsts[t], kinds[t], _dev_index(dev), shapes[t])
        pairs = []
        for t in range(n_t):
            for r in range(1, NDEV):
                peer = _peer(me, r)
                snd = _remote(srcs[t], reg(t, me), send.at[t, r - 1], recv.at[t, r - 1], peer)
                arr = _remote(reg(t, peer), reg(t, peer), send.at[t, r - 1], recv.at[t, r - 1], peer)
                pairs.append((snd, arr))
        mine = [pltpu.make_async_copy(srcs[t], reg(t, me), loc.at[t]) for t in range(n_t)]
        return pairs, mine

    outs = [SDS(_whole_shape(kinds[t], shapes[t]), shards[t].dtype) for t in range(n_t)]
    dma = pltpu.SemaphoreType.DMA
    return _stage(shards, outs, {}, [dma((n_t, 7)), dma((n_t, 7)), dma((n_t,))], build)


def _ag2(wholes, kinds, shapes):
    n_t = len(wholes)

    def build(_, dsts, sems):
        send, recv = sems
        x, y, c = _place()
        sib = (x, y, 1 - c)
        chips = [(1 - x, y), (x, 1 - y), (1 - x, 1 - y)]
        reg = lambda t, dev: _region(dsts[t], kinds[t], _dev_index(dev), shapes[t])
        pairs = []
        for t in range(n_t):
            for j, chip in enumerate(chips):
                snd = _remote(reg(t, (*chip, c)), reg(t, (*chip, c)), send.at[t, j], recv.at[t, j], sib)
                arr = _remote(reg(t, (*chip, 1 - c)), reg(t, (*chip, 1 - c)), send.at[t, j], recv.at[t, j], sib)
                pairs.append((snd, arr))
        return pairs, []

    outs = [SDS(w.shape, w.dtype) for w in wholes]
    dma = pltpu.SemaphoreType.DMA
    return _stage(wholes, outs, {t: t for t in range(n_t)}, [dma((n_t, 3)), dma((n_t, 3))], build)


def _chip_of(q):
    return (q >> 1, q & 1)


def _rs1(wholes, kinds, shapes):
    n_t = len(wholes)

    def build(srcs, outs, sems):
        send, recv = sems
        x, y, c = _place()
        sib = (x, y, 1 - c)
        pairs = []
        for t in range(n_t):
            for q in range(4):
                theirs = _region(srcs[t], kinds[t], _dev_index((*_chip_of(q), 1 - c)), shapes[t])
                pairs.append((
                    _remote(theirs, outs[t].at[q], send.at[t, q], recv.at[t, q], sib),
                    _remote(outs[t].at[q], outs[t].at[q], send.at[t, q], recv.at[t, q], sib),
                ))
        return pairs, []

    slabs = [SDS((4,) + tuple(shapes[t]), wholes[t].dtype) for t in range(n_t)]
    dma = pltpu.SemaphoreType.DMA
    return _stage(wholes, slabs, {}, [dma((n_t, 4)), dma((n_t, 4))], build)


def _rs2(pair_sums):
    n_t = len(pair_sums)

    def build(srcs, lands, sems):
        send, recv, loc = sems
        x, y, c = _place()
        my_chip = 2 * x + y
        pairs, mine = [], []
        for t in range(n_t):
            for j, (px, py) in enumerate([(1 - x, y), (x, 1 - y), (1 - x, 1 - y)]):
                q = 2 * px + py
                pairs.append((
                    _remote(srcs[t].at[q], lands[t].at[my_chip], send.at[t, j], recv.at[t, j], (px, py, c)),
                    _remote(lands[t].at[q], lands[t].at[q], send.at[t, j], recv.at[t, j], (px, py, c)),
                ))
            mine.append(pltpu.make_async_copy(srcs[t].at[my_chip], lands[t].at[my_chip], loc.at[t]))
        return pairs, mine

    outs = [SDS(q.shape, q.dtype) for q in pair_sums]
    dma = pltpu.SemaphoreType.DMA
    return _stage(pair_sums, outs, {}, [dma((n_t, 3)), dma((n_t, 3)), dma((n_t,))], build)


def _pair_sum(name, whole, kind, got):
    _, rows, cols = got.shape
    tr = _tile(rows, 256)
    n_r = rows // tr
    core = lax.axis_index("c").astype(jnp.int32).reshape(1)

    def body(_, a, b, o):
        o[...] = (a[...].astype(F32) + b[...].astype(F32)).astype(BF16)

    if kind == "col":
        own = pl.BlockSpec((tr, cols), lambda q, i, c_ref: (i, 2 * q + c_ref[0]))
    elif kind == "row":
        own = pl.BlockSpec((tr, cols), lambda q, i, c_ref: ((2 * q + c_ref[0]) * n_r + i, 0))
    else:
        own = pl.BlockSpec((None, tr, cols), lambda q, i, c_ref: (2 * q + c_ref[0], i, 0))
    slab = pl.BlockSpec((None, tr, cols), lambda q, i, c_ref: (q, i, 0))
    return pl.pallas_call(
        body,
        name=name,
        grid_spec=pltpu.PrefetchScalarGridSpec(
            num_scalar_prefetch=1, grid=(4, n_r), in_specs=[own, slab], out_specs=slab
        ),
        out_shape=SDS(got.shape, BF16),
        compiler_params=_params(("parallel", "parallel")),
    )(core, whole, got)


def _packed_rows(parts, c_):
    offs, r0 = [], 0
    for p in parts:
        offs.append(r0)
        r0 += p.shape[1] * (p.shape[2] // c_)
    return offs, -(-r0 // 8) * 8


def _all_reduce_small(parts, c_):
    n_p = len(parts)
    offs, r_ = _packed_rows(parts, c_)

    def body(*refs):
        p_refs = refs[:n_p]
        land, total, src, send_sems, recv_sems = refs[n_p:]
        me = _place()
        my = _dev_index(me)
        src[...] = jnp.zeros((r_, c_), F32)
        for p_ref, r0 in zip(p_refs, offs):
            v = jnp.sum(p_ref[...], axis=0)
            k = v.shape[1] // c_
            for ri in range(v.shape[0]):
                for q in range(k):
                    src[r0 + ri * k + q : r0 + ri * k + q + 1, :] = v[ri : ri + 1, q * c_ : (q + 1) * c_]
        land[my] = src[...]

        def copy(r):
            peer = _peer(me, r)
            return pltpu.make_async_remote_copy(
                src_ref=src,
                dst_ref=land.at[my],
                send_sem=send_sems.at[r - 1],
                recv_sem=recv_sems.at[r - 1],
                device_id=peer,
                device_id_type=MESH,
            )

        def arrival(r):
            peer = _peer(me, r)
            slab = land.at[_dev_index(peer)]
            return pltpu.make_async_remote_copy(
                src_ref=slab,
                dst_ref=slab,
                send_sem=send_sems.at[r - 1],
                recv_sem=recv_sems.at[r - 1],
                device_id=peer,
                device_id_type=MESH,
            )

        sends = [copy(r) for r in range(1, NDEV)]
        for cp in sends:
            cp.start()
        for r in range(1, NDEV):
            arrival(r).wait_recv()
        for cp in sends:
            cp.wait_send()
        acc = land[0]
        for d in range(1, NDEV):
            acc = acc + land[d]
        total[...] = acc

    vmem = pl.BlockSpec(memory_space=pltpu.VMEM)
    return pl.pallas_call(
        body,
        name="all_reduce_small",
        in_specs=[vmem] * n_p,
        out_specs=[vmem, vmem],
        out_shape=[SDS((NDEV, r_, c_), F32), SDS((r_, c_), F32)],
        scratch_shapes=[pltpu.VMEM((r_, c_), F32), pltpu.SemaphoreType.DMA((7,)), pltpu.SemaphoreType.DMA((7,))],
        compiler_params=_params(),
    )(*parts)[1]


def _adamw_math(g, w, m, v):
    m2 = ADAM_B1 * m + (1.0 - ADAM_B1) * g
    v2 = ADAM_B2 * v + (1.0 - ADAM_B2) * (g * g)
    m_hat = m2 / (1.0 - ADAM_B1**ADAM_STEP)
    v_hat = v2 / (1.0 - ADAM_B2**ADAM_STEP)
    delta = -ADAM_LR * (m_hat / (jnp.sqrt(v_hat) + ADAM_EPS) + ADAM_WD * w)
    return delta, m2, v2


def _adamw_big(name, land, w, m, v):
    rows, cols = w.shape
    tr = _tile(rows, 256)
    n_slab = land.shape[0]

    def body(l_ref, w_ref, m_ref, v_ref, g_o, d_o, m_o, v_o):
        g = l_ref[0].astype(F32)
        for d in range(1, n_slab):
            g = g + l_ref[d].astype(F32)
        delta, m2, v2 = _adamw_math(g, w_ref[...], m_ref[...], v_ref[...])
        g_o[...] = g
        d_o[...] = delta
        m_o[...] = m2
        v_o[...] = v2

    blk = pl.BlockSpec((tr, cols), lambda i: (i, 0))
    return pl.pallas_call(
        body,
        name=name,
        grid=(rows // tr,),
        in_specs=[pl.BlockSpec((n_slab, tr, cols), lambda i: (0, i, 0)), blk, blk, blk],
        out_specs=[blk] * 4,
        out_shape=[SDS((rows, cols), F32)] * 4,
        compiler_params=_params(("parallel",)),
    )(land, w, m, v)


def _adamw_small(total, items):
    c_ = total.shape[1]
    n_it = len(items)

    def body(*refs):
        t_ref = refs[0]
        ins, outs = refs[1 : 1 + 3 * n_it], refs[1 + 3 * n_it :]
        my = _dev_index(_place())
        for q, (row0, taps, w, _, _) in enumerate(items):
            w_ref, m_ref, v_ref = ins[3 * q : 3 * q + 3]
            if taps:
                lanes = w.shape[-1]
                g = t_ref[pl.ds(row0, taps), pl.ds(pl.multiple_of(my * lanes, 128), lanes)][None]
            else:
                k = w.shape[-1] // c_
                g = jnp.concatenate([t_ref[row0 + j : row0 + j + 1, :] for j in range(k)], axis=1)
            delta, m2, v2 = _adamw_math(g, w_ref[...], m_ref[...], v_ref[...])
            for o_ref, val in zip(outs[4 * q : 4 * q + 4], (g, delta, m2, v2)):
                o_ref[...] = val

    vmem = pl.BlockSpec(memory_space=pltpu.VMEM)
    flat = [a for (_, _, w, m, v) in items for a in (w, m, v)]
    res = pl.pallas_call(
        body,
        name="adamw_small",
        in_specs=[vmem] * (1 + 3 * n_it),
        out_specs=[vmem] * (4 * n_it),
        out_shape=[SDS(w.shape, F32) for (_, _, w, _, _) in items for _ in range(4)],
    )(total, *flat)
    return [list(res[4 * q : 4 * q + 4]) for q in range(n_it)]


def kernel(x, p, g_mix, w_in, conv_a_w, w_out_a, b_glu, conf_dw_w, conf_dw_b, conf_ln_g, conf_ln_b, w_pw_b, b_pw_b, w_o, g_ffn, w_gate, w_up, w_down, g_ple, w_ple_gate, w_ple_proj, g_final, loss_target, m_g_mix, m_w_in, m_conv_a_w, m_w_out_a, m_b_glu, m_conf_dw_w, m_conf_dw_b, m_conf_ln_g, m_conf_ln_b, m_w_pw_b, m_b_pw_b, m_w_o, m_g_ffn, m_w_gate, m_w_up, m_w_down, m_g_ple, m_w_ple_gate, m_w_ple_proj, m_g_final, v_g_mix, v_w_in, v_conv_a_w, v_w_out_a, v_b_glu, v_conf_dw_w, v_conf_dw_b, v_conf_ln_g, v_conf_ln_b, v_w_pw_b, v_b_pw_b, v_w_o, v_g_ffn, v_w_gate, v_w_up, v_w_down, v_g_ple, v_w_ple_gate, v_w_ple_proj, v_g_final):
    s, d = x.shape[1], x.shape[2]
    c = conf_ln_g.shape[-1]
    pdim = w_ple_proj.shape[1]
    fs = w_gate.shape[-1]
    nin = NDEV * w_in.shape[-1]
    assert d == 2 * c and nin == 5 * c + 2 * d, (d, c, nin)
    x2, p2, tgt = x[0], p[0, 0], loss_target[0]
    gfin = g_final.reshape(1, d)

    kpa, kpb = 8, HALO_B
    wa_sh = jnp.pad(conv_a_w[0], ((0, kpa - CONV_A_K), (0, 0)))
    wd_sh = jnp.pad(conf_dw_w[0], ((0, kpb - CONF_K), (0, 0)))
    kind_of = dict(w_in="col", w_out_a="col", w_pw_b="col", w_ple_proj="col", w_o="row", w_ple_gate="row",
                   w_gate="blk", w_up="blk", w_down="blk")
    weight = dict(w_in=w_in, w_out_a=w_out_a, w_pw_b=w_pw_b, w_ple_proj=w_ple_proj, w_o=w_o, w_ple_gate=w_ple_gate,
                  w_gate=w_gate, w_up=w_up, w_down=w_down)
    shard_of = {nm: tuple(w.shape[1:]) for nm, w in weight.items()}
    bf16_shard = lambda nm: weight[nm][0].astype(BF16)
    kinds_ = lambda grp: [kind_of[nm] for nm in grp]
    shapes_ = lambda grp: [shard_of[nm] for nm in grp]
    first_stage = lambda grp: _ag1([bf16_shard(nm) for nm in grp], kinds_(grp))
    second_stage = lambda grp, parts: _ag2(parts, kinds_(grp), shapes_(grp))
    grp_1 = ["w_out_a", "w_pw_b"]
    grp_2 = ["w_o", "w_gate"]
    grp_3 = ["w_up"]
    grp_4 = ["w_down"]
    grp_5 = ["w_ple_gate", "w_ple_proj"]

    tm = _tile(s, 1024)
    tn = _tile(d, 1024)
    assert (5 * c) % tn == 0 and d % tn == 0 and c % tn == 0
    ga_blk, gb_blk = (5 * c) // tn, (5 * c + d) // tn
    ij = lambda i, j, k: (i, j)
    row_i = lambda i, j, k: (i, 0)

    n1 = _rms_fwd("rms1", x2, g_mix)
    proj, win, got = _proj_gather(
        n1, bf16_shard("w_in"),
        _join(_ag_direct([wa_sh, wd_sh], ["col", "col"]), first_stage(grp_1)), first_stage(grp_2),
    )
    (wa, wd), part_12 = got[:2], got[2:]
    grp_12 = grp_1 + grp_2
    ya_in = _mix_a_fwd(proj, wa, s, c)
    (v_act, u_act, cv), got = _mix_b_fwd(
        proj, b_glu, wd, conf_dw_b, conf_ln_g, conf_ln_b, s, c,
        comm=_join(second_stage(grp_12, part_12), first_stage(grp_3)),
    )
    (wouta, wpw, wo, wg), part_3 = got[: len(grp_12)], got[len(grp_12) :]

    def ep_merge(accs, ex, os_):
        sa = _sigmoid(ex[0][...].astype(F32))
        sb = _sigmoid(ex[1][...].astype(F32))
        ya = accs[0]
        yb = accs[1] + ex[2][...]
        os_[0][...] = (sa * ya + sb * yb).astype(BF16)
        os_[1][...] = ya.astype(BF16)
        os_[2][...] = yb.astype(BF16)

    gate_a_spec = pl.BlockSpec((tm, tn), lambda i, j, k: (i, ga_blk + j))
    gate_b_spec = pl.BlockSpec((tm, tn), lambda i, j, k: (i, gb_blk + j))
    out_sd = (SDS((s, d), BF16), pl.BlockSpec((tm, tn), ij))
    (m_act, ya, yb), got = _fmm(
        "merge", (s // tm, d // tn, 1),
        [(ya_in, pl.BlockSpec((tm, c), row_i)), (wouta, pl.BlockSpec((c, tn), lambda i, j, k: (0, j))),
         (v_act, pl.BlockSpec((tm, c), row_i)), (wpw, pl.BlockSpec((c, tn), lambda i, j, k: (0, j)))],
        [(0, 1, NN, 0, None), (2, 3, NN, 1, None)], [(tm, tn), (tm, tn)],
        [(proj, gate_a_spec), (proj, gate_b_spec), (b_pw_b, pl.BlockSpec((1, tn), lambda i, j, k: (0, j)))],
        [out_sd, out_sd, out_sd], ep_merge, csplit=EPILOGUE_CHUNK,
        comm=_join(second_stage(grp_3, part_3), first_stage(grp_4)),
    )
    (wu,), part_4 = got[: len(grp_3)], got[len(grp_3) :]

    def ep_residual(accs, ex, os_):
        os_[0][...] = accs[0] + ex[0][...]

    (h1,), got = _fmm(
        "w_o", (s // tm, d // tn, 1),
        [(m_act, pl.BlockSpec((tm, d), row_i)), (wo, pl.BlockSpec((d, tn), lambda i, j, k: (0, j)))],
        [(0, 1, NN, 0, None)], [(tm, tn)], [(x2, pl.BlockSpec((tm, tn), ij))],
        [(SDS((s, d), F32), pl.BlockSpec((tm, tn), ij))], ep_residual, csplit=EPILOGUE_CHUNK,
        comm=_join(second_stage(grp_4, part_4), first_stage(grp_5)),
    )
    (wdn,), part_5 = got[: len(grp_4)], got[len(grp_4) :]
    n2 = _rms_fwd("rms2", h1, g_ffn)

    def ep_gateup(accs, ex, os_):
        g, u = accs
        os_[0][...] = g.astype(BF16)
        os_[1][...] = u.astype(BF16)
        os_[2][...] = (g * _sigmoid(g) * u).astype(BF16)

    ff_sd = (SDS((NDEV, s, fs), BF16), pl.BlockSpec((None, tm, fs), lambda i, j, k: (j, i, 0)))
    w_col_blk = pl.BlockSpec((None, d, fs), lambda i, j, k: (j, 0, 0))
    (g_act, u_ff, f_act), (wpg, wpp) = _fmm(
        "gate_up", (s // tm, NDEV, 1),
        [(n2, pl.BlockSpec((tm, d), row_i)), (wg, w_col_blk), (wu, w_col_blk)],
        [(0, 1, NN, 0, None), (0, 2, NN, 1, None)], [(tm, fs), (tm, fs)], [],
        [ff_sd, ff_sd, ff_sd], ep_gateup,
        comm=second_stage(grp_5, part_5),
    )
    pair = 2
    (h2,), _ = _fmm(
        "down", (s // tm, d // tn, NDEV // pair),
        [(f_act, pl.BlockSpec((pair, tm, fs), lambda i, j, k: (k, i, 0))),
         (wdn, pl.BlockSpec((pair, fs, tn), lambda i, j, k: (k, 0, j)))],
        [(0, 1, NN, 0, None, pair)], [(tm, tn)], [(h1, pl.BlockSpec((tm, tn), ij))],
        [(SDS((s, d), F32), pl.BlockSpec((tm, tn), ij))], ep_residual,
    )
    n3 = _rms_fwd("rms3", h2, g_ple)

    tr = _tile(s, 256)
    n_r = s // tr
    rows = lambda i, j, k: (i, 0)
    whole = lambda i, j, k: (0, 0)
    part_spec = lambda nrow: pl.BlockSpec((None, nrow, d), lambda i, j, k: (i, 0, 0))

    def ep_ple(accs, ex, os_):
        h2_, t_, gf = ex[0][...], ex[1][...], ex[2][...]
        ple = accs[0]
        s3 = _sigmoid(accs[1])
        h3 = h2_ + s3 * ple
        r = lax.rsqrt(jnp.mean(h3 * h3, axis=-1, keepdims=True) + EPS)
        hn = h3 * r
        e = hn * gf - t_
        loss = 0.5 * jnp.sum(jnp.mean(e * e, axis=-1, keepdims=True), axis=0, keepdims=True)
        dy = e * (1.0 / d)
        dn = dy * gf
        dh3 = r * (dn - hn * jnp.mean(dn * hn, axis=-1, keepdims=True))
        os_[0][...] = dh3
        os_[1][...] = (dh3 * s3).astype(BF16)
        os_[2][...] = (dh3 * ple * s3 * (1.0 - s3)).astype(BF16)
        os_[3][0:1, :] = jnp.sum(dy * hn, axis=0, keepdims=True)
        os_[3][1:2, :] = jnp.broadcast_to(loss, (1, d))

    (dh3, d_ple, d_g3, part_fin), _ = _fmm(
        "ple_loss", (n_r, 1, 1),
        [(p2, pl.BlockSpec((tr, pdim), rows)), (wpp, pl.BlockSpec((pdim, d), whole)),
         (n3, pl.BlockSpec((tr, d), rows)), (wpg, pl.BlockSpec((d, d), whole))],
        [(0, 1, NN, 0, None), (2, 3, NN, 1, None)], [(tr, d), (tr, d)],
        [(h2, pl.BlockSpec((tr, d), rows)), (tgt, pl.BlockSpec((tr, d), rows)), (gfin, pl.BlockSpec((1, d), whole))],
        [(SDS((s, d), F32), pl.BlockSpec((tr, d), rows)), (SDS((s, d), BF16), pl.BlockSpec((tr, d), rows)),
         (SDS((s, d), BF16), pl.BlockSpec((tr, d), rows)), (SDS((n_r, 2, d), F32), part_spec(2))],
        ep_ple,
    )

    g_wpp = _mm_tn("d_w_ple_proj", p2, d_ple)
    g_wpg = _mm_tn("d_w_ple_gate", n3, d_g3)

    def ep_norm_bwd(accs, ex, os_):
        dh, dg = _rms_bwd(accs[0], ex[0][...], ex[2][...])
        dh = ex[1][...] + dh
        os_[0][...] = dh
        os_[1][...] = dh.astype(BF16)
        os_[2][...] = dg

    norm_outs = lambda t: [
        (SDS((s, d), F32), pl.BlockSpec((t, d), rows)), (SDS((s, d), BF16), pl.BlockSpec((t, d), rows)),
        (SDS((s // t, 1, d), F32), part_spec(1)),
    ]
    def exchange1(names, wholes):
        return _rs1(wholes, kinds_(names), shapes_(names))

    def pair_sums(names, wholes, got):
        return [_pair_sum("pair_sum_" + nm, wholes[t], kind_of[nm], got[t]) for t, nm in enumerate(names)]

    lands = {}
    grp1 = ["w_ple_proj", "w_ple_gate"]
    (dh2, dh2b, part_ple), got = _fmm(
        "d_n3", (n_r, 1, 1),
        [(d_g3, pl.BlockSpec((tr, d), rows)), (wpg, pl.BlockSpec((d, d), whole))],
        [(0, 1, NT, 0, None)], [(tr, d)],
        [(h2, pl.BlockSpec((tr, d), rows)), (dh3, pl.BlockSpec((tr, d), rows)), (g_ple, pl.BlockSpec((1, d), whole))],
        norm_outs(tr), ep_norm_bwd,
        comm=exchange1(grp1, [g_wpp, g_wpg]),
    )
    sums1 = pair_sums(grp1, [g_wpp, g_wpg], got)

    def ep_ddown(accs, ex, os_):
        g = ex[0][...].astype(F32)
        u = ex[1][...].astype(F32)
        sg = _sigmoid(g)
        df = accs[0]
        os_[0][...] = (df * u * sg * (1.0 + g * (1.0 - sg))).astype(BF16)
        os_[1][...] = (df * g * sg).astype(BF16)

    ff_in = pl.BlockSpec((None, tm, fs), lambda i, j, k: (j, i, 0))
    (d_g, d_u), got = _fmm(
        "d_down", (s // tm, NDEV, 1),
        [(dh2b, pl.BlockSpec((tm, d), row_i)), (wdn, pl.BlockSpec((None, fs, d), lambda i, j, k: (j, 0, 0)))],
        [(0, 1, NT, 0, None)], [(tm, fs)], [(g_act, ff_in), (u_ff, ff_in)],
        [ff_sd, ff_sd], ep_ddown, csplit=EPILOGUE_CHUNK,
        comm=_rs2(sums1),
    )
    lands.update(zip(grp1, got))
    tk = _tile(s, 1024)
    (g_wdn,), _ = _fmm(
        "d_w_down", (NDEV, 1, s // tk),
        [(f_act, pl.BlockSpec((None, tk, fs), lambda i, j, k: (i, k, 0))), (dh2b, pl.BlockSpec((tk, d), lambda i, j, k: (k, 0)))],
        [(0, 1, TN, 0, None)], [(fs, d)], [],
        [(SDS((NDEV, fs, d), BF16), pl.BlockSpec((None, fs, d), lambda i, j, k: (i, 0, 0)))], _ep_bf16,
    )

    def ep_two_bf16(accs, ex, os_):
        os_[0][...] = accs[0].astype(BF16)
        os_[1][...] = accs[1].astype(BF16)

    ff_k = pl.BlockSpec((None, tk, fs), lambda i, j, k: (i, k, 0))
    wcol_sd = (SDS((NDEV, d, fs), BF16), pl.BlockSpec((None, d, fs), lambda i, j, k: (i, 0, 0)))
    grp2 = ["w_down"]
    (g_wg, g_wu), got = _fmm(
        "d_w_gate_up", (NDEV, 1, s // tk),
        [(n2, pl.BlockSpec((tk, d), lambda i, j, k: (k, 0))), (d_g, ff_k), (d_u, ff_k)],
        [(0, 1, TN, 0, None), (0, 2, TN, 1, None)], [(d, fs), (d, fs)], [],
        [wcol_sd, wcol_sd], ep_two_bf16,
        comm=exchange1(grp2, [g_wdn]),
    )
    sums2 = pair_sums(grp2, [g_wdn], got)
    grp3 = ["w_gate", "w_up"]
    th = _tile(s // 2, 1024)
    ff_a = pl.BlockSpec((None, th, fs), lambda i, j, k: (k, i, 0))
    w_k = pl.BlockSpec((None, d, fs), lambda i, j, k: (k, 0, 0))
    (d_n2,), got = _fmm(
        "d_n2", (s // th, 1, NDEV),
        [(d_g, ff_a), (wg, w_k), (d_u, ff_a), (wu, w_k)],
        [(0, 1, NT, 0, None), (2, 3, NT, 0, None)], [(th, d)], [],
        [(SDS((s, d), BF16), pl.BlockSpec((th, d), rows))], _ep_bf16,
        comm=_join(_rs2(sums2), exchange1(grp3, [g_wg, g_wu])),
    )
    lands.update(zip(grp2, got[:1]))
    sums3 = pair_sums(grp3, [g_wg, g_wu], got[1:])
    dh1, dh1b, part_ffn = _norm_bwd("d_h1", d_n2, h1, dh2, g_ffn, True)
    g_wo = _mm_tn("d_w_o", m_act, dh1b)

    def ep_dm(accs, ex, os_):
        ya_, yb_ = ex[0][...].astype(F32), ex[1][...].astype(F32)
        sa = _sigmoid(ex[2][...].astype(F32))
        sb = _sigmoid(ex[3][...].astype(F32))
        dm = accs[0]
        d_yb = dm * sb
        os_[0][...] = (dm * sa).astype(BF16)
        os_[1][...] = d_yb.astype(BF16)
        os_[2][...] = (dm * ya_ * sa * (1.0 - sa)).astype(BF16)
        os_[3][...] = (dm * yb_ * sb * (1.0 - sb)).astype(BF16)
        os_[4][...] = jnp.sum(d_yb, axis=0, keepdims=True)

    tile_ij = pl.BlockSpec((tm, tn), ij)
    grp4 = ["w_o"]
    (d_ya, d_yb, d_ga, d_gb, part_bpw), got = _fmm(
        "d_merge", (s // tm, d // tn, 1),
        [(dh1b, pl.BlockSpec((tm, d), row_i)), (wo, pl.BlockSpec((tn, d), lambda i, j, k: (j, 0)))],
        [(0, 1, NT, 0, None)], [(tm, tn)],
        [(ya, tile_ij), (yb, tile_ij), (proj, gate_a_spec), (proj, gate_b_spec)],
        [out_sd, out_sd, out_sd, out_sd,
         (SDS((s // tm, 1, d), F32), pl.BlockSpec((None, 1, tn), lambda i, j, k: (i, 0, j)))],
        ep_dm, csplit=EPILOGUE_CHUNK,
        comm=exchange1(grp4, [g_wo]),
    )
    sums4 = pair_sums(grp4, [g_wo], got)
    g_wouta = _mm_tn("d_w_out_a", ya_in, d_ya)
    g_wpw = _mm_tn("d_w_pw_b", v_act, d_yb)
    grp5 = ["w_out_a", "w_pw_b"]
    d_ya_in, got = _mm_nt("d_ya_in", d_ya, wouta, comm=exchange1(grp5, [g_wouta, g_wpw]))
    sums5 = pair_sums(grp5, [g_wouta, g_wpw], got)
    d_v, _ = _mm_nt("d_v", d_yb, wpw)
    d_cv, part_ln = _mix_b_bwd1(d_v, cv, conf_ln_g, conf_ln_b, s, c)
    (d_b, part_wd, part_bglu), got = _mix_b_bwd2(d_cv, u_act, proj, b_glu, wd, s, c, comm=_rs2(sums3))
    lands.update(zip(grp3, got))
    d_a, part_wa = _mix_a_bwd(d_ya_in, proj, wa, s, c)

    nb = nin // c
    gblk = d // c
    lo = [0, 3, 5, 5 + gblk]
    hi = [3, 5, 5 + gblk, 5 + 2 * gblk]
    pieces = [d_a, d_b, d_ga, d_gb]

    def active(q, ax):
        return lambda ids: jnp.logical_and(ids[ax] >= lo[q], ids[ax] < hi[q])

    def piece_spec(q, rows_, ax, row0=0):
        def index(i, j, k):
            ids = (i, j, k)
            col = jnp.clip(ids[ax] - lo[q], 0, hi[q] - lo[q] - 1)
            row = i + row0 if ax == 2 else jnp.where(active(q, ax)(ids), k, 0)
            return (row, col)

        return pl.BlockSpec((rows_, c), index)

    tkw = _tile(s, 1024)
    (g_win,), got = _fmm(
        "d_w_in", (1, nb, s // tkw),
        [(n1, pl.BlockSpec((tkw, d), lambda i, j, k: (k, 0)))]
        + [(pieces[q], piece_spec(q, tkw, 1)) for q in range(4)],
        [(0, 1 + q, TN, 0, active(q, 1)) for q in range(4)], [(d, c)], [],
        [(SDS((d, nin), BF16), pl.BlockSpec((d, c), lambda i, j, k: (0, j)))], _ep_bf16,
        comm=_rs2(sums4 + sums5),
    )
    lands.update(zip(grp4 + grp5, got))

    grp6 = ["w_in"]
    n_half = max(1, 3 * (s // th) // 8)

    def d_n1_rows(name, row0, n_tiles, comm, into):
        return _fmm(
            name, (n_tiles, 1, nb),
            [(pieces[q], piece_spec(q, th, 2, row0)) for q in range(4)]
            + [(win, pl.BlockSpec((d, c), lambda i, j, k: (0, k)))],
            [(q, 4, NT, 0, active(q, 2)) for q in range(4)], [(th, d)], [],
            [(SDS((s, d), BF16), pl.BlockSpec((th, d), lambda i, j, k: (i + row0, 0)))], _ep_bf16,
            comm=comm, into=into,
        )

    (d_n1,), got = d_n1_rows("d_n1_a", 0, n_half, exchange1(grp6, [g_win]), None)
    (d_n1,), got = d_n1_rows("d_n1_b", n_half, s // th - n_half, _rs2(pair_sums(grp6, [g_win], got)), d_n1)
    lands.update(zip(grp6, got))
    dx, part_mix = _norm_bwd("d_x", d_n1, x2, dh1, g_mix, False)

    small_parts = [part_mix, part_bglu, part_ln, part_bpw, part_ffn, part_ple, part_fin, part_wa, part_wd]
    (o_mix, o_bglu, o_ln, o_bpw, o_ffn, o_ple, o_fin, o_wa, o_wd), _ = _packed_rows(small_parts, c)
    total = _all_reduce_small(small_parts, c)
    loss = total[o_fin + d // c, 0]

    big_m = dict(w_in=m_w_in, w_out_a=m_w_out_a, w_pw_b=m_w_pw_b, w_ple_proj=m_w_ple_proj, w_o=m_w_o,
                 w_ple_gate=m_w_ple_gate, w_gate=m_w_gate, w_up=m_w_up, w_down=m_w_down)
    big_v = dict(w_in=v_w_in, w_out_a=v_w_out_a, w_pw_b=v_w_pw_b, w_ple_proj=v_w_ple_proj, w_o=v_w_o,
                 w_ple_gate=v_w_ple_gate, w_gate=v_w_gate, w_up=v_w_up, w_down=v_w_down)
    big_out = {}
    for nm in weight:
        res = _adamw_big("adamw_" + nm, lands[nm], weight[nm][0], big_m[nm][0], big_v[nm][0])
        big_out[nm] = [r[None] for r in res]

    row = lambda a: a.reshape(1, d)
    small = dict(
        g_mix=(o_mix, 0, g_mix, m_g_mix, v_g_mix),
        conv_a_w=(o_wa, CONV_A_K, conv_a_w, m_conv_a_w, v_conv_a_w),
        b_glu=(o_bglu, 0, b_glu, m_b_glu, v_b_glu),
        conf_dw_w=(o_wd, CONF_K, conf_dw_w, m_conf_dw_w, v_conf_dw_w),
        conf_dw_b=(o_ln + 2, 0, conf_dw_b, m_conf_dw_b, v_conf_dw_b),
        conf_ln_g=(o_ln, 0, conf_ln_g, m_conf_ln_g, v_conf_ln_g),
        conf_ln_b=(o_ln + 1, 0, conf_ln_b, m_conf_ln_b, v_conf_ln_b),
        b_pw_b=(o_bpw, 0, b_pw_b, m_b_pw_b, v_b_pw_b),
        g_ffn=(o_ffn, 0, g_ffn, m_g_ffn, v_g_ffn),
        g_ple=(o_ple, 0, g_ple, m_g_ple, v_g_ple),
        g_final=(o_fin, 0, row(g_final), row(m_g_final), row(v_g_final)),
    )
    small_out = dict(zip(small, _adamw_small(total, list(small.values()))))
    small_out["g_final"] = [a.reshape(d) for a in small_out["g_final"]]

    order = ["g_mix", "w_in", "conv_a_w", "w_out_a", "b_glu", "conf_dw_w", "conf_dw_b", "conf_ln_g", "conf_ln_b", "w_pw_b", "b_pw_b", "w_o", "g_ffn", "w_gate", "w_up", "w_down", "g_ple", "w_ple_gate", "w_ple_proj", "g_final"]
    allo = {**big_out, **small_out}
    outs = [loss, dx[None]]
    for q in range(4):
        outs += [allo[nm][q] for nm in order]
    return tuple(outs)
```

```python
import jax
import jax.numpy as jnp
from jax import lax
from jax.experimental import pallas as pl
from jax.experimental.pallas import tpu as pltpu

F32, BF16 = jnp.float32, jnp.bfloat16
EPS, LN_EPS = 1e-6, 1e-5
ADAM_LR, ADAM_B1, ADAM_B2, ADAM_EPS, ADAM_WD, ADAM_STEP = 0.001, 0.9, 0.999, 1e-08, 0.01, 10
CONV_A_K, CONF_K = 3, 31
NDEV = 8
NN = (((1,), (0,)), ((), ()))
NT = (((1,), (1,)), ((), ()))
TN = (((0,), (0,)), ((), ()))
V7X_VMEM_LIMIT_BYTES = 56 * 1024 * 1024
MESH = pl.DeviceIdType.MESH
SDS = jax.ShapeDtypeStruct
HALO_A, HALO_B = 16, 32
EPILOGUE_CHUNK = 256
CONV_ROWS = 32
CONF_ROWS = 16


def _tile(n, pref):
    t = min(n, pref)
    while n % t:
        t -= 8
    return t


def _sigmoid(x):
    return jax.nn.sigmoid(x)


def _params(sem=None):
    return pltpu.CompilerParams(vmem_limit_bytes=V7X_VMEM_LIMIT_BYTES, dimension_semantics=sem)


def _edge(grid, last):
    cond = None
    for ax, n in enumerate(grid):
        here = pl.program_id(ax) == (n - 1 if last else 0)
        cond = here if cond is None else jnp.logical_and(cond, here)
    return cond


def _join(*comms):
    ins, outs, alias, sems, spans = [], [], {}, [], []
    for cm in comms:
        spans.append((len(ins), len(outs), len(sems)))
        for i, o in cm["alias"].items():
            alias[len(ins) + i] = len(outs) + o
        ins += cm["ins"]
        outs += cm["outs"]
        sems += cm["sems"]

    def run(which):
        def f(i_refs, o_refs, s_refs):
            for cm, (a, b, c_) in zip(comms, spans):
                cm[which](
                    i_refs[a : a + len(cm["ins"])], o_refs[b : b + len(cm["outs"])], s_refs[c_ : c_ + len(cm["sems"])]
                )

        return f

    return dict(ins=ins, outs=outs, alias=alias, sems=sems, start=run("start"), finish=run("finish"))


def _call(body, name, grid, in_specs, args, out_specs, out_shape, scratch=(), sem=None, comm=None, alias=None):
    n_in, n_out, n_s = len(args), len(out_shape), len(scratch)
    alias = dict(alias or {})
    if comm is None:
        res = pl.pallas_call(
            body, name=name, grid=grid, in_specs=list(in_specs), out_specs=list(out_specs), out_shape=list(out_shape),
            scratch_shapes=list(scratch), input_output_aliases=alias, compiler_params=_params(sem),
        )(*args)
        return list(res), []
    n_ci, n_co = len(comm["ins"]), len(comm["outs"])

    def wrapped(*refs):
        ins = refs[:n_in]
        ci = refs[n_in : n_in + n_ci]
        o0 = n_in + n_ci
        outs = refs[o0 : o0 + n_out]
        co = refs[o0 + n_out : o0 + n_out + n_co]
        s0 = o0 + n_out + n_co
        sc = refs[s0 : s0 + n_s]
        cs = refs[s0 + n_s :]
        pl.when(_edge(grid, False))(lambda: comm["start"](ci, co, cs))
        body(*ins, *outs, *sc)
        pl.when(_edge(grid, True))(lambda: comm["finish"](ci, co, cs))

    hbm = pl.BlockSpec(memory_space=pl.ANY)
    res = pl.pallas_call(
        wrapped,
        name=name,
        grid=grid,
        in_specs=list(in_specs) + [hbm] * n_ci,
        out_specs=list(out_specs) + [hbm] * n_co,
        out_shape=list(out_shape) + list(comm["outs"]),
        scratch_shapes=list(scratch) + list(comm["sems"]),
        input_output_aliases={**alias, **{n_in + i: n_out + o for i, o in comm["alias"].items()}},
        compiler_params=_params(("arbitrary",) * len(grid)),
    )(*args, *comm["ins"])
    return list(res[:n_out]), list(res[n_out:])


def _col_chunks(n, pref):
    out, c0 = [], 0
    while c0 < n:
        w = min(pref, n - c0)
        out.append((c0, w))
        c0 += w
    return out


def _fmm(name, grid, operands, terms, acc_shapes, extras, outs, epilogue, comm=None, csplit=None, into=None):
    n_p, n_e, n_o, n_a = len(operands), len(extras), len(outs), len(acc_shapes)
    nk = grid[-1]
    kax = len(grid) - 1
    simple = nk == 1 and all(t[4] is None for t in terms)
    alias = None
    if into is not None:
        extras = list(extras) + [(into, pl.BlockSpec(memory_space=pl.ANY))]
        alias = {n_p + n_e: 0}
        n_e += 1
    if csplit is not None:
        assert simple and into is None and all(t[2] in (NN, NT) and (len(t) <= 5 or not t[5]) for t in terms)
        tn_ = acc_shapes[0][1]
        chunks = _col_chunks(tn_, csplit)

    def dot(a, b, dims):
        if a.dtype != BF16:
            a = a.astype(BF16)
        if b.dtype != BF16:
            b = b.astype(BF16)
        return lax.dot_general(a, b, dims, preferred_element_type=F32)

    def value(refs, term):
        slabs = term[5] if len(term) > 5 else 0
        if not slabs:
            return dot(refs[term[0]][...], refs[term[1]][...], term[2])
        tot = None
        for sl in range(slabs):
            d = dot(refs[term[0]][sl], refs[term[1]][sl], term[2])
            tot = d if tot is None else tot + d
        return tot

    def always(refs):
        parts = [None] * n_a
        for term in terms:
            if term[4] is None:
                d = value(refs, term)
                parts[term[3]] = d if parts[term[3]] is None else parts[term[3]] + d
        return parts

    def chunked(refs, ex, os_, accs):
        cols = lambda ref, c0, w: ref.at[:, pl.ds(c0, w)] if ref.shape[-1] == tn_ else ref

        def dots(k):
            c0, w = chunks[k]
            parts = [None] * n_a
            for term in terms:
                b_ref = refs[term[1]]
                b = b_ref[:, pl.ds(c0, w)] if term[2] == NN else b_ref[pl.ds(c0, w), :]
                d = dot(refs[term[0]][...], b, term[2])
                parts[term[3]] = d if parts[term[3]] is None else parts[term[3]] + d
            for ai in range(n_a):
                accs[ai][k % 2, :, pl.ds(0, w)] = parts[ai]

        def finish(k):
            c0, w = chunks[k]
            vals = [accs[ai][k % 2, :, pl.ds(0, w)] for ai in range(n_a)]
            epilogue(vals, [cols(e, c0, w) for e in ex], [cols(o, c0, w) for o in os_])

        dots(0)
        for k in range(1, len(chunks)):
            dots(k)
            finish(k - 1)
        finish(len(chunks) - 1)

    def body(*refs):
        ex = refs[n_p : n_p + n_e]
        os_ = refs[n_p + n_e : n_p + n_e + n_o]
        accs = refs[n_p + n_e + n_o :]
        if simple and csplit is not None:
            chunked(refs, ex, os_, accs)
            return
        if simple:
            epilogue(always(refs), ex, os_)
            return
        ids = [pl.program_id(ax) for ax in range(len(grid))]
        k = ids[kax]

        @pl.when(k == 0)
        def _():
            for acc in accs:
                acc[...] = jnp.zeros(acc.shape, F32)

        for ai, part in enumerate(always(refs)):
            if part is not None:
                accs[ai][...] += part
        for term in terms:
            if term[4] is not None:

                def add(term=term):
                    accs[term[3]][...] += value(refs, term)

                pl.when(term[4](ids))(add)

        @pl.when(k == nk - 1)
        def _():
            epilogue([acc[...] for acc in accs], ex, os_)

    return _call(
        body,
        name,
        grid,
        [o[1] for o in operands] + [e[1] for e in extras],
        [o[0] for o in operands] + [e[0] for e in extras],
        [o[1] for o in outs],
        [o[0] for o in outs],
        scratch=[pltpu.VMEM((2, s[0], csplit), F32) for s in acc_shapes] if csplit is not None
        else [] if simple else [pltpu.VMEM(s, F32) for s in acc_shapes],
        sem=("parallel",) * kax + ("arbitrary",),
        comm=comm,
        alias=alias,
    )


def _rms_bwd(dn_raw, h, g):
    r = lax.rsqrt(jnp.mean(h * h, axis=-1, keepdims=True) + EPS)
    hn = h * r
    dg = jnp.sum(dn_raw * hn, axis=0, keepdims=True)
    dn = dn_raw * g
    dh = r * (dn - hn * jnp.mean(dn * hn, axis=-1, keepdims=True))
    return dh, dg


def _rms_fwd(name, h, g):
    s, d = h.shape
    ts = _tile(s, 512)

    def body(h_ref, g_ref, o_ref):
        x = h_ref[...]
        r = lax.rsqrt(jnp.mean(x * x, axis=-1, keepdims=True) + EPS)
        o_ref[...] = (x * r * g_ref[...]).astype(BF16)

    return pl.pallas_call(
        body,
        name=name,
        grid=(s // ts,),
        in_specs=[pl.BlockSpec((ts, d), lambda i: (i, 0)), pl.BlockSpec((1, d), lambda i: (0, 0))],
        out_specs=pl.BlockSpec((ts, d), lambda i: (i, 0)),
        out_shape=SDS((s, d), BF16),
        compiler_params=_params(("parallel",)),
    )(h, g)


def _norm_bwd(name, dn, h, dres, g, want_bf16):
    s, d = h.shape
    ts = _tile(s, 512)

    def body(dn_r, h_r, dres_r, g_r, *outs):
        dh, dg = _rms_bwd(dn_r[...].astype(F32), h_r[...], g_r[...])
        dh = dres_r[...] + dh
        outs[0][...] = dh
        if want_bf16:
            outs[1][...] = dh.astype(BF16)
        outs[-1][...] = dg

    blk = pl.BlockSpec((ts, d), lambda i: (i, 0))
    part = pl.BlockSpec((None, 1, d), lambda i: (i, 0, 0))
    return pl.pallas_call(
        body,
        name=name,
        grid=(s // ts,),
        in_specs=[blk, blk, blk, pl.BlockSpec((1, d), lambda i: (0, 0))],
        out_specs=[blk] + ([blk] if want_bf16 else []) + [part],
        out_shape=[SDS((s, d), F32)] + ([SDS((s, d), BF16)] if want_bf16 else []) + [SDS((s // ts, 1, d), F32)],
        compiler_params=_params(("parallel",)),
    )(dn, h, dres, g)


def _slabs_to_plain(name, x):
    n, rows, w = x.shape
    tr = _tile(rows, 256)

    def body(x_ref, o_ref):
        for j in range(n):
            o_ref[:, j * w : (j + 1) * w] = x_ref[j]

    return pl.pallas_call(
        body,
        name=name,
        grid=(rows // tr,),
        in_specs=[pl.BlockSpec((n, tr, w), lambda i: (0, i, 0))],
        out_specs=pl.BlockSpec((tr, n * w), lambda i: (i, 0)),
        out_shape=SDS((rows, n * w), x.dtype),
        compiler_params=_params(("parallel",)),
    )(x)


def _plain_to_slabs(name, x, n):
    rows, nw = x.shape
    w = nw // n
    tr = _tile(rows, 256)

    def body(x_ref, o_ref):
        for j in range(n):
            o_ref[j] = x_ref[:, j * w : (j + 1) * w]

    return pl.pallas_call(
        body,
        name=name,
        grid=(rows // tr,),
        in_specs=[pl.BlockSpec((tr, nw), lambda i: (i, 0))],
        out_specs=pl.BlockSpec((n, tr, w), lambda i: (0, i, 0)),
        out_shape=SDS((n, rows, w), x.dtype),
        compiler_params=_params(("parallel",)),
    )(x)


def _prev_halo(ts, hb):
    r = ts // hb
    return lambda i: jnp.maximum(i * r - 1, 0)


def _next_halo(ts, hb, s):
    r = ts // hb
    last = s // hb - 1
    return lambda i: jnp.minimum((i + 1) * r, last)


def _shift_copies(buf, sh):
    n = sh.shape[1]
    for j in range(1, 8):
        sh[j - 1, pl.ds(0, n), :] = buf[pl.ds(j, n), :]


def _tap(buf, sh, r0, off, rows):
    j = off % 8
    start = pl.multiple_of(r0 + (off - j), 8)
    if j == 0:
        return buf[pl.ds(start, rows), :]
    return sh[j - 1, pl.ds(start, rows), :]


def _mix_a_fwd(proj, wa, s, c):
    ts, hb = _tile(s, 256), HALO_A
    prev = _prev_halo(ts, hb)

    def body(ah, ab, ac, hh, hc, w, o, buf):
        i = pl.program_id(0)
        zh = hc[...].astype(F32) * hh[...].astype(F32)
        buf[pl.ds(0, hb), :] = jnp.where(i == 0, 0.0, zh)
        buf[pl.ds(hb, ts), :] = ac[...].astype(F32) * ah[...].astype(F32)
        for r0 in range(0, ts, CONV_ROWS):
            cz = jnp.zeros((CONV_ROWS, c), F32)
            for k in range(CONV_A_K):
                cz = cz + w[k : k + 1, :] * buf[pl.ds(hb + r0 - (CONV_A_K - 1) + k, CONV_ROWS), :]
            o[pl.ds(r0, CONV_ROWS), :] = (ab[pl.ds(r0, CONV_ROWS), :].astype(F32) * cz).astype(BF16)

    main = lambda cb: pl.BlockSpec((ts, c), lambda i: (i, cb))
    halo = lambda cb: pl.BlockSpec((hb, c), lambda i: (prev(i), cb))
    return pl.pallas_call(
        body,
        name="mix_a_fwd",
        grid=(s // ts,),
        in_specs=[main(0), main(1), main(2), halo(0), halo(2), pl.BlockSpec(wa.shape, lambda i: (0, 0))],
        out_specs=pl.BlockSpec((ts, c), lambda i: (i, 0)),
        out_shape=SDS((s, c), BF16),
        scratch_shapes=[pltpu.VMEM((hb + ts, c), F32)],
        compiler_params=_params(("parallel",)),
    )(proj, proj, proj, proj, proj, wa)


def _mix_b_fwd(proj, b_glu, wd, bd, lg, lb, s, c, comm=None):
    ts, hb = _tile(s, 256), HALO_B
    prev = _prev_halo(ts, hb)

    def body(gv, gg, hv, hg, bglu, w, bd_r, lg_r, lb_r, v_o, u_o, cv_o, buf, sh):
        i = pl.program_id(0)
        bv, bg = bglu[:, 0:c], bglu[:, c : 2 * c]
        uh = (hv[...].astype(F32) + bv) * _sigmoid(hg[...].astype(F32) + bg)
        buf[pl.ds(0, hb), :] = jnp.where(i == 0, 0.0, uh)
        u = (gv[...].astype(F32) + bv) * _sigmoid(gg[...].astype(F32) + bg)
        buf[pl.ds(hb, ts), :] = u
        u_o[...] = u.astype(BF16)
        _shift_copies(buf, sh)

        def chunk(ci, carry):
            r0 = pl.multiple_of(ci * CONF_ROWS, CONF_ROWS)
            acc = jnp.zeros((CONF_ROWS, c), F32)
            for k in range(CONF_K):
                acc = acc + w[k : k + 1, :] * _tap(buf, sh, r0, hb - (CONF_K - 1) + k, CONF_ROWS)
            cv_o[pl.ds(r0, CONF_ROWS), :] = acc + bd_r[...]
            return carry

        lax.fori_loop(0, ts // CONF_ROWS, chunk, 0)
        cv = cv_o[...]
        mu = jnp.mean(cv, axis=-1, keepdims=True)
        xc = cv - mu
        rs = lax.rsqrt(jnp.mean(xc * xc, axis=-1, keepdims=True) + LN_EPS)
        ln = xc * rs * lg_r[...] + lb_r[...]
        v_o[...] = (ln * _sigmoid(ln)).astype(BF16)

    main = lambda cb: pl.BlockSpec((ts, c), lambda i: (i, cb))
    halo = lambda cb: pl.BlockSpec((hb, c), lambda i: (prev(i), cb))
    full = lambda a: pl.BlockSpec(a.shape, lambda i: (0, 0))
    out = pl.BlockSpec((ts, c), lambda i: (i, 0))
    return _call(
        body,
        "mix_b_fwd",
        (s // ts,),
        [main(3), main(4), halo(3), halo(4), full(b_glu), full(wd), full(bd), full(lg), full(lb)],
        [proj, proj, proj, proj, b_glu, wd, bd, lg, lb],
        [out, out, out],
        [SDS((s, c), BF16), SDS((s, c), BF16), SDS((s, c), F32)],
        scratch=[pltpu.VMEM((hb + ts, c), F32), pltpu.VMEM((7, hb + ts - 8, c), F32)],
        sem=("parallel",),
        comm=comm,
    )


def _mix_b_bwd1(d_v, cv, lg, lb, s, c):
    ts = _tile(s, 256)

    def body(dv_r, cv_r, lg_r, lb_r, dcv_o, part_o):
        cv_ = cv_r[...]
        mu = jnp.mean(cv_, axis=-1, keepdims=True)
        xc = cv_ - mu
        rs = lax.rsqrt(jnp.mean(xc * xc, axis=-1, keepdims=True) + LN_EPS)
        xh = xc * rs
        ln = xh * lg_r[...] + lb_r[...]
        sg = _sigmoid(ln)
        d_ln = dv_r[...].astype(F32) * (sg * (1.0 + ln * (1.0 - sg)))
        dy = d_ln * lg_r[...]
        d_cv = rs * (dy - jnp.mean(dy, axis=-1, keepdims=True) - xh * jnp.mean(dy * xh, axis=-1, keepdims=True))
        dcv_o[...] = d_cv
        part_o[0:1, :] = jnp.sum(d_ln * xh, axis=0, keepdims=True)
        part_o[1:2, :] = jnp.sum(d_ln, axis=0, keepdims=True)
        part_o[2:3, :] = jnp.sum(d_cv, axis=0, keepdims=True)

    blk = pl.BlockSpec((ts, c), lambda i: (i, 0))
    full = lambda a: pl.BlockSpec(a.shape, lambda i: (0, 0))
    return pl.pallas_call(
        body,
        name="mix_b_bwd_ln",
        grid=(s // ts,),
        in_specs=[blk, blk, full(lg), full(lb)],
        out_specs=[blk, pl.BlockSpec((None, 3, c), lambda i: (i, 0, 0))],
        out_shape=[SDS((s, c), F32), SDS((s // ts, 3, c), F32)],
        compiler_params=_params(("parallel",)),
    )(d_v, cv, lg, lb)


def _mix_b_bwd2(d_cv, u, proj, b_glu, wd, s, c, comm=None):
    ts, hb = _tile(s, 256), HALO_B
    prev, nxt = _prev_halo(ts, hb), _next_halo(ts, hb, s)
    n_t = s // ts
    kp = wd.shape[0]

    def body(dcv, dcv_n, u_m, u_p, gv, gg, bglu, w, d_o, dwd_o, dbglu_o, dbuf, ubuf, dub, dsh, ush, dwacc):
        i = pl.program_id(0)
        dbuf[pl.ds(0, ts), :] = dcv[...]
        dbuf[pl.ds(ts, hb), :] = jnp.where(i == n_t - 1, 0.0, dcv_n[...])
        ubuf[pl.ds(0, hb), :] = jnp.where(i == 0, 0.0, u_p[...].astype(F32))
        ubuf[pl.ds(hb, ts), :] = u_m[...].astype(F32)
        _shift_copies(dbuf, dsh)
        _shift_copies(ubuf, ush)
        dwacc[...] = jnp.zeros(dwacc.shape, F32)

        def chunk(ci, carry):
            r0 = pl.multiple_of(ci * CONF_ROWS, CONF_ROWS)
            acc = jnp.zeros((CONF_ROWS, c), F32)
            dc = dbuf[pl.ds(r0, CONF_ROWS), :]
            for k in range(CONF_K):
                acc = acc + w[k : k + 1, :] * _tap(dbuf, dsh, r0, (CONF_K - 1) - k, CONF_ROWS)
                prod = dc * _tap(ubuf, ush, r0, hb - (CONF_K - 1) + k, CONF_ROWS)
                fold = prod[0:8]
                for a in range(1, CONF_ROWS // 8):
                    fold = fold + prod[8 * a : 8 * a + 8]
                dwacc[pl.ds(8 * k, 8), :] += fold
            dub[pl.ds(r0, CONF_ROWS), :] = acc
            return carry

        lax.fori_loop(0, ts // CONF_ROWS, chunk, 0)
        for k in range(CONF_K):
            dwd_o[k : k + 1, :] = jnp.sum(dwacc[pl.ds(8 * k, 8), :], axis=0, keepdims=True)
        dwd_o[CONF_K:kp, :] = jnp.zeros((kp - CONF_K, c), F32)
        bv, bg = bglu[:, 0:c], bglu[:, c : 2 * c]
        d_u = dub[...]
        sg = _sigmoid(gg[...].astype(F32) + bg)
        d_gv = d_u * sg
        d_gg = d_u * (gv[...].astype(F32) + bv) * sg * (1.0 - sg)
        d_o[:, 0:c] = d_gv.astype(BF16)
        d_o[:, c : 2 * c] = d_gg.astype(BF16)
        dbglu_o[:, 0:c] = jnp.sum(d_gv, axis=0, keepdims=True)
        dbglu_o[:, c : 2 * c] = jnp.sum(d_gg, axis=0, keepdims=True)

    blk = lambda cb: pl.BlockSpec((ts, c), lambda i: (i, cb))
    full = lambda a: pl.BlockSpec(a.shape, lambda i: (0, 0))
    return _call(
        body,
        "mix_b_bwd_conv",
        (n_t,),
        [
            blk(0),
            pl.BlockSpec((hb, c), lambda i: (nxt(i), 0)),
            blk(0),
            pl.BlockSpec((hb, c), lambda i: (prev(i), 0)),
            blk(3),
            blk(4),
            full(b_glu),
            full(wd),
        ],
        [d_cv, d_cv, u, u, proj, proj, b_glu, wd],
        [
            pl.BlockSpec((ts, 2 * c), lambda i: (i, 0)),
            pl.BlockSpec((None, kp, c), lambda i: (i, 0, 0)),
            pl.BlockSpec((None, 1, 2 * c), lambda i: (i, 0, 0)),
        ],
        [SDS((s, 2 * c), BF16), SDS((n_t, kp, c), F32), SDS((n_t, 1, 2 * c), F32)],
        scratch=[
            pltpu.VMEM((ts + hb, c), F32), pltpu.VMEM((hb + ts, c), F32), pltpu.VMEM((ts, c), F32),
            pltpu.VMEM((7, hb + ts - 8, c), F32), pltpu.VMEM((7, hb + ts - 8, c), F32), pltpu.VMEM((8 * CONF_K, c), F32),
        ],
        sem=("parallel",),
        comm=comm,
    )


def _mix_a_bwd(d_ya, proj, wa, s, c):
    ts, hb = _tile(s, 256), HALO_A
    prev, nxt = _prev_halo(ts, hb), _next_halo(ts, hb, s)
    n_t = s // ts
    kp = wa.shape[0]

    def body(dya, dya_n, ah, ab, ac, ah_p, ac_p, ab_n, w, d_o, dwa_o, zbuf, dbuf, dzb):
        i = pl.program_id(0)
        zbuf[pl.ds(0, hb), :] = jnp.where(i == 0, 0.0, ac_p[...].astype(F32) * ah_p[...].astype(F32))
        zbuf[pl.ds(hb, ts), :] = ac[...].astype(F32) * ah[...].astype(F32)
        dbuf[pl.ds(0, ts), :] = dya[...].astype(F32) * ab[...].astype(F32)
        dbuf[pl.ds(ts, hb), :] = jnp.where(i == n_t - 1, 0.0, dya_n[...].astype(F32) * ab_n[...].astype(F32))
        dw_rows = [jnp.zeros((1, c), F32) for _ in range(CONV_A_K)]
        for r0 in range(0, ts, CONV_ROWS):
            cz = jnp.zeros((CONV_ROWS, c), F32)
            dz = jnp.zeros((CONV_ROWS, c), F32)
            dc = dbuf[pl.ds(r0, CONV_ROWS), :]
            for k in range(CONV_A_K):
                zk = zbuf[pl.ds(hb + r0 - (CONV_A_K - 1) + k, CONV_ROWS), :]
                cz = cz + w[k : k + 1, :] * zk
                dz = dz + w[k : k + 1, :] * dbuf[pl.ds(r0 + (CONV_A_K - 1) - k, CONV_ROWS), :]
                dw_rows[k] = dw_rows[k] + jnp.sum(dc * zk, axis=0, keepdims=True)
            d_o[pl.ds(r0, CONV_ROWS), c : 2 * c] = (dya[pl.ds(r0, CONV_ROWS), :].astype(F32) * cz).astype(BF16)
            dzb[pl.ds(r0, CONV_ROWS), :] = dz
        d_z = dzb[...]
        d_o[:, 0:c] = (d_z * ac[...].astype(F32)).astype(BF16)
        d_o[:, 2 * c : 3 * c] = (d_z * ah[...].astype(F32)).astype(BF16)
        for k in range(CONV_A_K):
            dwa_o[k : k + 1, :] = dw_rows[k]
        dwa_o[CONV_A_K:kp, :] = jnp.zeros((kp - CONV_A_K, c), F32)

    blk = lambda cb: pl.BlockSpec((ts, c), lambda i: (i, cb))
    hp = lambda cb: pl.BlockSpec((hb, c), lambda i: (prev(i), cb))
    hn = lambda cb: pl.BlockSpec((hb, c), lambda i: (nxt(i), cb))
    return pl.pallas_call(
        body,
        name="mix_a_bwd",
        grid=(n_t,),
        in_specs=[blk(0), hn(0), blk(0), blk(1), blk(2), hp(0), hp(2), hn(1), pl.BlockSpec(wa.shape, lambda i: (0, 0))],
        out_specs=[pl.BlockSpec((ts, 3 * c), lambda i: (i, 0)), pl.BlockSpec((None, kp, c), lambda i: (i, 0, 0))],
        out_shape=[SDS((s, 3 * c), BF16), SDS((n_t, kp, c), F32)],
        scratch_shapes=[pltpu.VMEM((hb + ts, c), F32), pltpu.VMEM((ts + hb, c), F32), pltpu.VMEM((ts, c), F32)],
        compiler_params=_params(("parallel",)),
    )(d_ya, d_ya, proj, proj, proj, proj, proj, proj, wa)


def _ep_bf16(accs, ex, os_):
    os_[0][...] = accs[0].astype(BF16)


def _mm_tn(name, a, b, tm=2048, tn=1024, tk=1024):
    m, k1 = a.shape
    n = b.shape[1]
    tm, tn, tk = _tile(k1, tm), _tile(n, tn), _tile(m, tk)
    return _fmm(
        name,
        (k1 // tm, n // tn, m // tk),
        [(a, pl.BlockSpec((tk, tm), lambda i, j, k: (k, i))), (b, pl.BlockSpec((tk, tn), lambda i, j, k: (k, j)))],
        [(0, 1, TN, 0, None)],
        [(tm, tn)],
        [],
        [(SDS((k1, n), BF16), pl.BlockSpec((tm, tn), lambda i, j, k: (i, j)))],
        _ep_bf16,
    )[0][0]


def _mm_nt(name, a, b, tm=1024, tn=1024, comm=None):
    m, kk = a.shape
    n = b.shape[0]
    tm, tn = _tile(m, tm), _tile(n, tn)
    outs, couts = _fmm(
        name,
        (m // tm, n // tn, 1),
        [(a, pl.BlockSpec((tm, kk), lambda i, j, k: (i, 0))), (b, pl.BlockSpec((tn, kk), lambda i, j, k: (j, 0)))],
        [(0, 1, NT, 0, None)],
        [(tm, tn)],
        [],
        [(SDS((m, n), BF16), pl.BlockSpec((tm, tn), lambda i, j, k: (i, j)))],
        _ep_bf16,
        comm=comm,
    )
    return outs[0], couts


def _dev_index(dev):
    return 4 * dev[0] + 2 * dev[1] + dev[2]


def _region(ref, kind, j, shard_shape):
    if kind == "col":
        ns = shard_shape[1]
        return ref.at[:, pl.ds(pl.multiple_of(j * ns, 128), ns)]
    if kind == "row":
        rs = shard_shape[0]
        return ref.at[pl.ds(pl.multiple_of(j * rs, 8), rs), :]
    return ref.at[j]


def _whole_shape(kind, shard_shape):
    if kind == "col":
        return (shard_shape[0], NDEV * shard_shape[1])
    if kind == "row":
        return (NDEV * shard_shape[0], shard_shape[1])
    return (NDEV,) + tuple(shard_shape)


def _place():
    return lax.axis_index("x"), lax.axis_index("y"), lax.axis_index("c")


def _proj_gather(n1, w_shard, early, late):
    s, d = n1.shape
    ns = w_shard.shape[1]
    pw = 2 * ns
    tm = _tile(s // 2, 512)
    n_i = s // tm
    comm = _join(early, late)
    n_early = (len(early["ins"]), len(early["outs"]), len(early["sems"]))
    assert n_i >= 2 and not comm["alias"]
    x0, y0, _ = _place()
    order = jnp.stack([2 * x0 + y0, 2 * x0 + (1 - y0), 2 * (1 - x0) + y0, 2 * (1 - x0) + (1 - y0)]).astype(jnp.int32)
    n_ci, n_co = len(comm["ins"]), len(comm["outs"])

    def body(order_ref, n1_ref, wsh_ref, *rest):
        ci = rest[:n_ci]
        proj_ref, win_ref = rest[n_ci], rest[n_ci + 1]
        co = rest[n_ci + 2 : n_ci + 2 + n_co]
        wfull, send, recv, fsend, frecv, loc, osem = rest[n_ci + 2 + n_co : n_ci + 9 + n_co]
        cs = rest[n_ci + 9 + n_co :]
        u, i = pl.program_id(0), pl.program_id(1)
        x, y, c = _place()
        sib = (x, y, 1 - c)
        chips = [(x, y), (x, 1 - y), (1 - x, y), (1 - x, 1 - y)]
        peers = [sib] + [(*ch, c) for ch in chips[1:]]
        blk = lambda ch, core: wfull.at[2 * ch[0] + ch[1], :, pl.ds(pl.multiple_of(core * ns, 128), ns)]
        sends = [_remote(blk(chips[0], c), blk(chips[0], c), send.at[k], recv.at[k], peers[k]) for k in range(4)]
        arrivals = [_remote(blk(chips[0], 1 - c), blk(chips[0], 1 - c), send.at[0], recv.at[0], sib)] + [
            _remote(blk(chips[k], c), blk(chips[k], c), send.at[k], recv.at[k], peers[k]) for k in range(1, 4)
        ]
        passes = [_remote(blk(chips[k], c), blk(chips[k], c), fsend.at[k - 1], frecv.at[k - 1], sib) for k in range(1, 4)]
        passed = [_remote(blk(chips[k], 1 - c), blk(chips[k], 1 - c), fsend.at[k - 1], frecv.at[k - 1], sib) for k in range(1, 4)]
        mine = lambda: pltpu.make_async_copy(wsh_ref, blk(chips[0], c), loc.at[0])

        def to_hbm(unit):
            q = order_ref[unit]
            return pltpu.make_async_copy(wfull.at[q], win_ref.at[:, pl.ds(pl.multiple_of(q * pw, 128), pw)], osem.at[unit])

        a, b, e = n_early
        early_refs = (ci[:a], co[:b], cs[:e])
        late_refs = (ci[a:], co[b:], cs[e:])

        @pl.when(jnp.logical_and(u == 0, i == 0))
        def _():
            mine().start()
            mine().wait()
            for snd in sends[:3]:
                snd().start()
            early["start"](*early_refs)
            arrivals[0]().wait_recv()

        @pl.when(jnp.logical_and(u == 1, i == 0))
        def _():
            sends[3]().start()

        @pl.when(jnp.logical_and(u == 2, i == 0))
        def _():
            late["start"](*late_refs)

        for nxt in range(1, 4):

            @pl.when(jnp.logical_and(u == nxt - 1, i == n_i - 1))
            def _(nxt=nxt):
                passed[nxt - 1]().wait_recv()

        proj_ref[...] = jnp.dot(n1_ref[...], wfull[order_ref[u]], preferred_element_type=F32).astype(BF16)

        for nxt in range(1, 4):

            @pl.when(jnp.logical_and(u == nxt - 1, i == n_i - 2))
            def _(nxt=nxt):
                arrivals[nxt]().wait_recv()
                passes[nxt - 1]().start()

        for unit in range(4):

            @pl.when(jnp.logical_and(u == unit, i == n_i - 1))
            def _(unit=unit):
                to_hbm(unit).start()

        @pl.when(jnp.logical_and(u == 3, i == n_i - 1))
        def _():
            for snd in sends + passes:
                snd().wait_send()
            for unit in range(4):
                to_hbm(unit).wait()
            comm["finish"](ci, co, cs)

    hbm = pl.BlockSpec(memory_space=pl.ANY)
    dma = pltpu.SemaphoreType.DMA
    res = pl.pallas_call(
        body,
        name="proj",
        grid_spec=pltpu.PrefetchScalarGridSpec(
            num_scalar_prefetch=1,
            grid=(4, n_i),
            in_specs=[pl.BlockSpec((tm, d), lambda u, i, order_ref: (i, 0)), hbm] + [hbm] * n_ci,
            out_specs=[pl.BlockSpec((tm, pw), lambda u, i, order_ref: (i, order_ref[u])), hbm] + [hbm] * n_co,
            scratch_shapes=[pltpu.VMEM((4, d, pw), BF16), dma((4,)), dma((4,)), dma((3,)), dma((3,)), dma((1,)), dma((4,))]
            + list(comm["sems"]),
        ),
        out_shape=[SDS((s, NDEV * ns), BF16), SDS((d, NDEV * ns), BF16)] + list(comm["outs"]),
        compiler_params=_params(("arbitrary", "arbitrary")),
    )(order, n1, w_shard, *comm["ins"])
    return res[0], res[1], list(res[2:])


def _peer(me, r):
    x, y, c = me
    return (1 - x if r & 4 else x, 1 - y if r & 2 else y, 1 - c if r & 1 else c)


def _remote(src, dst, send_sem, recv_sem, to):
    return lambda: pltpu.make_async_remote_copy(
        src_ref=src, dst_ref=dst, send_sem=send_sem, recv_sem=recv_sem, device_id=to, device_id_type=MESH
    )


def _run(pairs, locals_, start):
    if start:
        for cp in locals_:
            cp.start()
        for snd, _ in pairs:
            snd().start()
    else:
        for snd, arr in pairs:
            arr().wait_recv()
            snd().wait_send()
        for cp in locals_:
            cp.wait()


def _stage(ins, outs, alias, sems, build):
    return dict(
        ins=list(ins), outs=list(outs), alias=alias, sems=list(sems),
        start=lambda i, o, s: _run(*build(i, o, s), True),
        finish=lambda i, o, s: _run(*build(i, o, s), False),
    )


def _ag1(shards, kinds):
    n_t = len(shards)
    shapes = [tuple(sh.shape) for sh in shards]

    def build(srcs, dsts, sems):
        send, recv, loc = sems
        x, y, c = _place()
        me = (x, y, c)
        peers = [(x, y, 1 - c), (1 - x, y, c), (x, 1 - y, c), (1 - x, 1 - y, c)]
        reg = lambda t, dev: _region(dsts[t], kinds[t], _dev_index(dev), shapes[t])
        pairs = []
        for t in range(n_t):
            for k, peer in enumerate(peers):
                snd = _remote(srcs[t], reg(t, me), send.at[t, k], recv.at[t, k], peer)
                arr = _remote(reg(t, peer), reg(t, peer), send.at[t, k], recv.at[t, k], peer)
                pairs.append((snd, arr))
        mine = [pltpu.make_async_copy(srcs[t], reg(t, me), loc.at[t]) for t in range(n_t)]
        return pairs, mine

    outs = [SDS(_whole_shape(kinds[t], shapes[t]), shards[t].dtype) for t in range(n_t)]
    dma = pltpu.SemaphoreType.DMA
    return _stage(shards, outs, {}, [dma((n_t, 4)), dma((n_t, 4)), dma((n_t,))], build)


def _ag_direct(shards, kinds):
    n_t = len(shards)
    shapes = [tuple(sh.shape) for sh in shards]

    def build(srcs, dsts, sems):
        send, recv, loc = sems
        me = _place()
        reg = lambda t, dev: _region(dsts[t], kinds[t], _dev_index(dev), shapes[t])
        pairs = []
        for t in range(n_t):
            for r in range(1, NDEV):
                peer = _peer(me, r)
                snd = _remote(srcs[t], reg(t, me), send.at[t, r - 1], recv.at[t, r - 1], peer)
                arr = _remote(reg(t, peer), reg(t, peer), send.at[t, r - 1], recv.at[t, r - 1], peer)
                pairs.append((snd, arr))
        mine = [pltpu.make_async_copy(srcs[t], reg(t, me), loc.at[t]) for t in range(n_t)]
        return pairs, mine

    outs = [SDS(_whole_shape(kinds[t], shapes[t]), shards[t].dtype) for t in range(n_t)]
    dma = pltpu.SemaphoreType.DMA
    return _stage(shards, outs, {}, [dma((n_t, 7)), dma((n_t, 7)), dma((n_t,))], build)


def _ag2(wholes, kinds, shapes):
    n_t = len(wholes)

    def build(_, dsts, sems):
        send, recv = sems
        x, y, c = _place()
        sib = (x, y, 1 - c)
        chips = [(1 - x, y), (x, 1 - y), (1 - x, 1 - y)]
        reg = lambda t, dev: _region(dsts[t], kinds[t], _dev_index(dev), shapes[t])
        pairs = []
        for t in range(n_t):
            for j, chip in enumerate(chips):
                snd = _remote(reg(t, (*chip, c)), reg(t, (*chip, c)), send.at[t, j], recv.at[t, j], sib)
                arr = _remote(reg(t, (*chip, 1 - c)), reg(t, (*chip, 1 - c)), send.at[t, j], recv.at[t, j], sib)
                pairs.append((snd, arr))
        return pairs, []

    outs = [SDS(w.shape, w.dtype) for w in wholes]
    dma = pltpu.SemaphoreType.DMA
    return _stage(wholes, outs, {t: t for t in range(n_t)}, [dma((n_t, 3)), dma((n_t, 3))], build)


def _chip_of(q):
    return (q >> 1, q & 1)


def _rs1(wholes, kinds, shapes):
    n_t = len(wholes)

    def build(srcs, outs, sems):
        send, recv = sems
        x, y, c = _place()
        sib = (x, y, 1 - c)
        pairs = []
        for t in range(n_t):
            for q in range(4):
                theirs = _region(srcs[t], kinds[t], _dev_index((*_chip_of(q), 1 - c)), shapes[t])
                pairs.append((
                    _remote(theirs, outs[t].at[q], send.at[t, q], recv.at[t, q], sib),
                    _remote(outs[t].at[q], outs[t].at[q], send.at[t, q], recv.at[t, q], sib),
                ))
        return pairs, []

    slabs = [SDS((4,) + tuple(shapes[t]), wholes[t].dtype) for t in range(n_t)]
    dma = pltpu.SemaphoreType.DMA
    return _stage(wholes, slabs, {}, [dma((n_t, 4)), dma((n_t, 4))], build)


def _rs2(pair_sums):
    n_t = len(pair_sums)

    def build(srcs, lands, sems):
        send, recv, loc = sems
        x, y, c = _place()
        my_chip = 2 * x + y
        pairs, mine = [], []
        for t in range(n_t):
            for j, (px, py) in enumerate([(1 - x, y), (x, 1 - y), (1 - x, 1 - y)]):
                q = 2 * px + py
                pairs.append((
                    _remote(srcs[t].at[q], lands[t].at[my_chip], send.at[t, j], recv.at[t, j], (px, py, c)),
                    _remote(lands[t].at[q], lands[t].at[q], send.at[t, j], recv.at[t, j], (px, py, c)),
                ))
            mine.append(pltpu.make_async_copy(srcs[t].at[my_chip], lands[t].at[my_chip], loc.at[t]))
        return pairs, mine

    outs = [SDS(q.shape, q.dtype) for q in pair_sums]
    dma = pltpu.SemaphoreType.DMA
    return _stage(pair_sums, outs, {}, [dma((n_t, 3)), dma((n_t, 3)), dma((n_t,))], build)


def _pair_sum(name, whole, kind, got):
    _, rows, cols = got.shape
    tr = _tile(rows, 256)
    n_r = rows // tr
    core = lax.axis_index("c").astype(jnp.int32).reshape(1)

    def body(_, a, b, o):
        o[...] = (a[...].astype(F32) + b[...].astype(F32)).astype(BF16)

    if kind == "col":
        own = pl.BlockSpec((tr, cols), lambda q, i, c_ref: (i, 2 * q + c_ref[0]))
    elif kind == "row":
        own = pl.BlockSpec((tr, cols), lambda q, i, c_ref: ((2 * q + c_ref[0]) * n_r + i, 0))
    else:
        own = pl.BlockSpec((None, tr, cols), lambda q, i, c_ref: (2 * q + c_ref[0], i, 0))
    slab = pl.BlockSpec((None, tr, cols), lambda q, i, c_ref: (q, i, 0))
    return pl.pallas_call(
        body,
        name=name,
        grid_spec=pltpu.PrefetchScalarGridSpec(
            num_scalar_prefetch=1, grid=(4, n_r), in_specs=[own, slab], out_specs=slab
        ),
        out_shape=SDS(got.shape, BF16),
        compiler_params=_params(("parallel", "parallel")),
    )(core, whole, got)


def _packed_rows(parts, c_):
    offs, r0 = [], 0
    for p in parts:
        offs.append(r0)
        r0 += p.shape[1] * (p.shape[2] // c_)
    return offs, -(-r0 // 8) * 8


def _all_reduce_small(parts, c_):
    n_p = len(parts)
    offs, r_ = _packed_rows(parts, c_)

    def body(*refs):
        p_refs = refs[:n_p]
        land, total, src, send_sems, recv_sems = refs[n_p:]
        me = _place()
        my = _dev_index(me)
        src[...] = jnp.zeros((r_, c_), F32)
        for p_ref, r0 in zip(p_refs, offs):
            v = jnp.sum(p_ref[...], axis=0)
            k = v.shape[1] // c_
            for ri in range(v.shape[0]):
                for q in range(k):
                    src[r0 + ri * k + q : r0 + ri * k + q + 1, :] = v[ri : ri + 1, q * c_ : (q + 1) * c_]
        land[my] = src[...]

        def copy(r):
            peer = _peer(me, r)
            return pltpu.make_async_remote_copy(
                src_ref=src,
                dst_ref=land.at[my],
                send_sem=send_sems.at[r - 1],
                recv_sem=recv_sems.at[r - 1],
                device_id=peer,
                device_id_type=MESH,
            )

        def arrival(r):
            peer = _peer(me, r)
            slab = land.at[_dev_index(peer)]
            return pltpu.make_async_remote_copy(
                src_ref=slab,
                dst_ref=slab,
                send_sem=send_sems.at[r - 1],
                recv_sem=recv_sems.at[r - 1],
                device_id=peer,
                device_id_type=MESH,
            )

        sends = [copy(r) for r in range(1, NDEV)]
        for cp in sends:
            cp.start()
        for r in range(1, NDEV):
            arrival(r).wait_recv()
        for cp in sends:
            cp.wait_send()
        acc = land[0]
        for d in range(1, NDEV):
            acc = acc + land[d]
        total[...] = acc

    vmem = pl.BlockSpec(memory_space=pltpu.VMEM)
    return pl.pallas_call(
        body,
        name="all_reduce_small",
        in_specs=[vmem] * n_p,
        out_specs=[vmem, vmem],
        out_shape=[SDS((NDEV, r_, c_), F32), SDS((r_, c_), F32)],
        scratch_shapes=[pltpu.VMEM((r_, c_), F32), pltpu.SemaphoreType.DMA((7,)), pltpu.SemaphoreType.DMA((7,))],
        compiler_params=_params(),
    )(*parts)[1]


def _adamw_math(g, w, m, v):
    m2 = ADAM_B1 * m + (1.0 - ADAM_B1) * g
    v2 = ADAM_B2 * v + (1.0 - ADAM_B2) * (g * g)
    m_hat = m2 / (1.0 - ADAM_B1**ADAM_STEP)
    v_hat = v2 / (1.0 - ADAM_B2**ADAM_STEP)
    delta = -ADAM_LR * (m_hat / (jnp.sqrt(v_hat) + ADAM_EPS) + ADAM_WD * w)
    return delta, m2, v2


def _adamw_big(name, land, w, m, v):
    rows, cols = w.shape
    tr = _tile(rows, 256)
    n_slab = land.shape[0]

    def body(l_ref, w_ref, m_ref, v_ref, g_o, d_o, m_o, v_o):
        g = l_ref[0].astype(F32)
        for d in range(1, n_slab):
            g = g + l_ref[d].astype(F32)
        delta, m2, v2 = _adamw_math(g, w_ref[...], m_ref[...], v_ref[...])
        g_o[...] = g
        d_o[...] = delta
        m_o[...] = m2
        v_o[...] = v2

    blk = pl.BlockSpec((tr, cols), lambda i: (i, 0))
    return pl.pallas_call(
        body,
        name=name,
        grid=(rows // tr,),
        in_specs=[pl.BlockSpec((n_slab, tr, cols), lambda i: (0, i, 0)), blk, blk, blk],
        out_specs=[blk] * 4,
        out_shape=[SDS((rows, cols), F32)] * 4,
        compiler_params=_params(("parallel",)),
    )(land, w, m, v)


def _adamw_small(total, items):
    c_ = total.shape[1]
    n_it = len(items)

    def body(*refs):
        t_ref = refs[0]
        ins, outs = refs[1 : 1 + 3 * n_it], refs[1 + 3 * n_it :]
        my = _dev_index(_place())
        for q, (row0, taps, w, _, _) in enumerate(items):
            w_ref, m_ref, v_ref = ins[3 * q : 3 * q + 3]
            if taps:
                lanes = w.shape[-1]
                g = t_ref[pl.ds(row0, taps), pl.ds(pl.multiple_of(my * lanes, 128), lanes)][None]
            else:
                k = w.shape[-1] // c_
                g = jnp.concatenate([t_ref[row0 + j : row0 + j + 1, :] for j in range(k)], axis=1)
            delta, m2, v2 = _adamw_math(g, w_ref[...], m_ref[...], v_ref[...])
            for o_ref, val in zip(outs[4 * q : 4 * q + 4], (g, delta, m2, v2)):
                o_ref[...] = val

    vmem = pl.BlockSpec(memory_space=pltpu.VMEM)
    flat = [a for (_, _, w, m, v) in items for a in (w, m, v)]
    res = pl.pallas_call(
        body,
        name="adamw_small",
        in_specs=[vmem] * (1 + 3 * n_it),
        out_specs=[vmem] * (4 * n_it),
        out_shape=[SDS(w.shape, F32) for (_, _, w, _, _) in items for _ in range(4)],
    )(total, *flat)
    return [list(res[4 * q : 4 * q + 4]) for q in range(n_it)]


def kernel(x, p, g_mix, w_in, conv_a_w, w_out_a, b_glu, conf_dw_w, conf_dw_b, conf_ln_g, conf_ln_b, w_pw_b, b_pw_b, w_o, g_ffn, w_gate, w_up, w_down, g_ple, w_ple_gate, w_ple_proj, g_final, loss_target, m_g_mix, m_w_in, m_conv_a_w, m_w_out_a, m_b_glu, m_conf_dw_w, m_conf_dw_b, m_conf_ln_g, m_conf_ln_b, m_w_pw_b, m_b_pw_b, m_w_o, m_g_ffn, m_w_gate, m_w_up, m_w_down, m_g_ple, m_w_ple_gate, m_w_ple_proj, m_g_final, v_g_mix, v_w_in, v_conv_a_w, v_w_out_a, v_b_glu, v_conf_dw_w, v_conf_dw_b, v_conf_ln_g, v_conf_ln_b, v_w_pw_b, v_b_pw_b, v_w_o, v_g_ffn, v_w_gate, v_w_up, v_w_down, v_g_ple, v_w_ple_gate, v_w_ple_proj, v_g_final):
    s, d = x.shape[1], x.shape[2]
    c = conf_ln_g.shape[-1]
    pdim = w_ple_proj.shape[1]
    fs = w_gate.shape[-1]
    nin = NDEV * w_in.shape[-1]
    assert d == 2 * c and nin == 5 * c + 2 * d, (d, c, nin)
    x2, p2, tgt = x[0], p[0, 0], loss_target[0]
    gfin = g_final.reshape(1, d)

    kpa, kpb = 8, HALO_B
    wa_sh = jnp.pad(conv_a_w[0], ((0, kpa - CONV_A_K), (0, 0)))
    wd_sh = jnp.pad(conf_dw_w[0], ((0, kpb - CONF_K), (0, 0)))
    kind_of = dict(w_in="col", w_out_a="col", w_pw_b="col", w_ple_proj="col", w_o="row", w_ple_gate="row",
                   w_gate="blk", w_up="blk", w_down="blk")
    weight = dict(w_in=w_in, w_out_a=w_out_a, w_pw_b=w_pw_b, w_ple_proj=w_ple_proj, w_o=w_o, w_ple_gate=w_ple_gate,
                  w_gate=w_gate, w_up=w_up, w_down=w_down)
    shard_of = {nm: tuple(w.shape[1:]) for nm, w in weight.items()}
    bf16_shard = lambda nm: weight[nm][0].astype(BF16)
    kinds_ = lambda grp: [kind_of[nm] for nm in grp]
    shapes_ = lambda grp: [shard_of[nm] for nm in grp]
    first_stage = lambda grp: _ag1([bf16_shard(nm) for nm in grp], kinds_(grp))
    second_stage = lambda grp, parts: _ag2(parts, kinds_(grp), shapes_(grp))
    grp_1 = ["w_out_a", "w_pw_b"]
    grp_2 = ["w_o", "w_gate"]
    grp_3 = ["w_up"]
    grp_4 = ["w_down"]
    grp_5 = ["w_ple_gate", "w_ple_proj"]

    tm = _tile(s, 1024)
    tn = _tile(d, 1024)
    assert (5 * c) % tn == 0 and d % tn == 0 and c % tn == 0
    ga_blk, gb_blk = (5 * c) // tn, (5 * c + d) // tn
    ij = lambda i, j, k: (i, j)
    row_i = lambda i, j, k: (i, 0)

    n1 = _rms_fwd("rms1", x2, g_mix)
    proj, win, got = _proj_gather(
        n1, bf16_shard("w_in"),
        _join(_ag_direct([wa_sh, wd_sh], ["col", "col"]), first_stage(grp_1)), first_stage(grp_2),
    )
    (wa, wd), part_12 = got[:2], got[2:]
    grp_12 = grp_1 + grp_2
    ya_in = _mix_a_fwd(proj, wa, s, c)
    (v_act, u_act, cv), got = _mix_b_fwd(
        proj, b_glu, wd, conf_dw_b, conf_ln_g, conf_ln_b, s, c,
        comm=_join(second_stage(grp_12, part_12), first_stage(grp_3)),
    )
    (wouta, wpw, wo, wg), part_3 = got[: len(grp_12)], got[len(grp_12) :]

    def ep_merge(accs, ex, os_):
        sa = _sigmoid(ex[0][...].astype(F32))
        sb = _sigmoid(ex[1][...].astype(F32))
        ya = accs[0]
        yb = accs[1] + ex[2][...]
        os_[0][...] = (sa * ya + sb * yb).astype(BF16)
        os_[1][...] = ya.astype(BF16)
        os_[2][...] = yb.astype(BF16)

    gate_a_spec = pl.BlockSpec((tm, tn), lambda i, j, k: (i, ga_blk + j))
    gate_b_spec = pl.BlockSpec((tm, tn), lambda i, j, k: (i, gb_blk + j))
    out_sd = (SDS((s, d), BF16), pl.BlockSpec((tm, tn), ij))
    (m_act, ya, yb), got = _fmm(
        "merge", (s // tm, d // tn, 1),
        [(ya_in, pl.BlockSpec((tm, c), row_i)), (wouta, pl.BlockSpec((c, tn), lambda i, j, k: (0, j))),
         (v_act, pl.BlockSpec((tm, c), row_i)), (wpw, pl.BlockSpec((c, tn), lambda i, j, k: (0, j)))],
        [(0, 1, NN, 0, None), (2, 3, NN, 1, None)], [(tm, tn), (tm, tn)],
        [(proj, gate_a_spec), (proj, gate_b_spec), (b_pw_b, pl.BlockSpec((1, tn), lambda i, j, k: (0, j)))],
        [out_sd, out_sd, out_sd], ep_merge, csplit=EPILOGUE_CHUNK,
        comm=_join(second_stage(grp_3, part_3), first_stage(grp_4)),
    )
    (wu,), part_4 = got[: len(grp_3)], got[len(grp_3) :]

    def ep_residual(accs, ex, os_):
        os_[0][...] = accs[0] + ex[0][...]

    (h1,), got = _fmm(
        "w_o", (s // tm, d // tn, 1),
        [(m_act, pl.BlockSpec((tm, d), row_i)), (wo, pl.BlockSpec((d, tn), lambda i, j, k: (0, j)))],
        [(0, 1, NN, 0, None)], [(tm, tn)], [(x2, pl.BlockSpec((tm, tn), ij))],
        [(SDS((s, d), F32), pl.BlockSpec((tm, tn), ij))], ep_residual, csplit=EPILOGUE_CHUNK,
        comm=_join(second_stage(grp_4, part_4), first_stage(grp_5)),
    )
    (wdn,), part_5 = got[: len(grp_4)], got[len(grp_4) :]
    n2 = _rms_fwd("rms2", h1, g_ffn)

    hidden = NDEV * fs
    wg_p = _slabs_to_plain("w_gate_plain", wg)
    wu_p = _slabs_to_plain("w_up_plain", wu)
    wdn_p = wdn.reshape(hidden, d)
    tf = _tile(hidden, 512)
    tkf = _tile(hidden, 2816)

    def ep_gateup(accs, ex, os_):
        g, u = accs
        sg = _sigmoid(g)
        silu = g * sg
        os_[0][...] = (u * sg * (1.0 + g * (1.0 - sg))).astype(BF16)
        os_[1][...] = silu.astype(BF16)
        os_[2][...] = (silu * u).astype(BF16)

    ff_sd = (SDS((s, hidden), BF16), pl.BlockSpec((tm, tf), ij))
    w_col_blk = pl.BlockSpec((d, tf), lambda i, j, k: (0, j))
    (df_dg, df_du, f_act), (wpg, wpp) = _fmm(
        "gate_up", (s // tm, hidden // tf, 1),
        [(n2, pl.BlockSpec((tm, d), row_i)), (wg_p, w_col_blk), (wu_p, w_col_blk)],
        [(0, 1, NN, 0, None), (0, 2, NN, 1, None)], [(tm, tf), (tm, tf)], [],
        [ff_sd, ff_sd, ff_sd], ep_gateup,
        comm=second_stage(grp_5, part_5),
    )
    (h2,), _ = _fmm(
        "down", (s // tm, d // tn, hidden // tkf),
        [(f_act, pl.BlockSpec((tm, tkf), lambda i, j, k: (i, k))),
         (wdn_p, pl.BlockSpec((tkf, tn), lambda i, j, k: (k, j)))],
        [(0, 1, NN, 0, None)], [(tm, tn)], [(h1, pl.BlockSpec((tm, tn), ij))],
        [(SDS((s, d), F32), pl.BlockSpec((tm, tn), ij))], ep_residual,
    )
    n3 = _rms_fwd("rms3", h2, g_ple)

    tr = _tile(s, 256)
    n_r = s // tr
    rows = lambda i, j, k: (i, 0)
    whole = lambda i, j, k: (0, 0)
    part_spec = lambda nrow: pl.BlockSpec((None, nrow, d), lambda i, j, k: (i, 0, 0))

    def ep_ple(accs, ex, os_):
        h2_, t_, gf = ex[0][...], ex[1][...], ex[2][...]
        ple = accs[0]
        s3 = _sigmoid(accs[1])
        h3 = h2_ + s3 * ple
        r = lax.rsqrt(jnp.mean(h3 * h3, axis=-1, keepdims=True) + EPS)
        hn = h3 * r
        e = hn * gf - t_
        loss = 0.5 * jnp.sum(jnp.mean(e * e, axis=-1, keepdims=True), axis=0, keepdims=True)
        dy = e * (1.0 / d)
        dn = dy * gf
        dh3 = r * (dn - hn * jnp.mean(dn * hn, axis=-1, keepdims=True))
        os_[0][...] = dh3
        os_[1][...] = (dh3 * s3).astype(BF16)
        os_[2][...] = (dh3 * ple * s3 * (1.0 - s3)).astype(BF16)
        os_[3][0:1, :] = jnp.sum(dy * hn, axis=0, keepdims=True)
        os_[3][1:2, :] = jnp.broadcast_to(loss, (1, d))

    (dh3, d_ple, d_g3, part_fin), _ = _fmm(
        "ple_loss", (n_r, 1, 1),
        [(p2, pl.BlockSpec((tr, pdim), rows)), (wpp, pl.BlockSpec((pdim, d), whole)),
         (n3, pl.BlockSpec((tr, d), rows)), (wpg, pl.BlockSpec((d, d), whole))],
        [(0, 1, NN, 0, None), (2, 3, NN, 1, None)], [(tr, d), (tr, d)],
        [(h2, pl.BlockSpec((tr, d), rows)), (tgt, pl.BlockSpec((tr, d), rows)), (gfin, pl.BlockSpec((1, d), whole))],
        [(SDS((s, d), F32), pl.BlockSpec((tr, d), rows)), (SDS((s, d), BF16), pl.BlockSpec((tr, d), rows)),
         (SDS((s, d), BF16), pl.BlockSpec((tr, d), rows)), (SDS((n_r, 2, d), F32), part_spec(2))],
        ep_ple,
    )

    g_wpp = _mm_tn("d_w_ple_proj", p2, d_ple)
    g_wpg = _mm_tn("d_w_ple_gate", n3, d_g3)

    def ep_norm_bwd(accs, ex, os_):
        dh, dg = _rms_bwd(accs[0], ex[0][...], ex[2][...])
        dh = ex[1][...] + dh
        os_[0][...] = dh
        os_[1][...] = dh.astype(BF16)
        os_[2][...] = dg

    norm_outs = lambda t: [
        (SDS((s, d), F32), pl.BlockSpec((t, d), rows)), (SDS((s, d), BF16), pl.BlockSpec((t, d), rows)),
        (SDS((s // t, 1, d), F32), part_spec(1)),
    ]
    def exchange1(names, wholes):
        return _rs1(wholes, kinds_(names), shapes_(names))

    def pair_sums(names, wholes, got):
        return [_pair_sum("pair_sum_" + nm, wholes[t], kind_of[nm], got[t]) for t, nm in enumerate(names)]

    lands = {}
    grp1 = ["w_ple_proj", "w_ple_gate"]
    (dh2, dh2b, part_ple), got = _fmm(
        "d_n3", (n_r, 1, 1),
        [(d_g3, pl.BlockSpec((tr, d), rows)), (wpg, pl.BlockSpec((d, d), whole))],
        [(0, 1, NT, 0, None)], [(tr, d)],
        [(h2, pl.BlockSpec((tr, d), rows)), (dh3, pl.BlockSpec((tr, d), rows)), (g_ple, pl.BlockSpec((1, d), whole))],
        norm_outs(tr), ep_norm_bwd,
        comm=exchange1(grp1, [g_wpp, g_wpg]),
    )
    sums1 = pair_sums(grp1, [g_wpp, g_wpg], got)

    def ep_ddown(accs, ex, os_):
        df = accs[0]
        os_[0][...] = (df * ex[0][...].astype(F32)).astype(BF16)
        os_[1][...] = (df * ex[1][...].astype(F32)).astype(BF16)

    ff_in = pl.BlockSpec((tm, tf), ij)
    (d_g, d_u), got = _fmm(
        "d_down", (s // tm, hidden // tf, 1),
        [(dh2b, pl.BlockSpec((tm, d), row_i)), (wdn_p, pl.BlockSpec((tf, d), lambda i, j, k: (j, 0)))],
        [(0, 1, NT, 0, None)], [(tm, tf)], [(df_dg, ff_in), (df_du, ff_in)],
        [ff_sd, ff_sd], ep_ddown, csplit=EPILOGUE_CHUNK,
        comm=_rs2(sums1),
    )
    lands.update(zip(grp1, got))
    tk = _tile(s, 1024)
    g_wdn = _mm_tn("d_w_down", f_act, dh2b, tm=1408, tn=2048).reshape(NDEV, fs, d)

    def ep_two_bf16(accs, ex, os_):
        os_[0][...] = accs[0].astype(BF16)
        os_[1][...] = accs[1].astype(BF16)

    ff_k = pl.BlockSpec((tk, tf), lambda i, j, k: (k, j))
    wcol_sd = (SDS((d, hidden), BF16), pl.BlockSpec((d, tf), lambda i, j, k: (0, j)))
    grp2 = ["w_down"]
    (g_wg_p, g_wu_p), got = _fmm(
        "d_w_gate_up", (1, hidden // tf, s // tk),
        [(n2, pl.BlockSpec((tk, d), lambda i, j, k: (k, 0))), (d_g, ff_k), (d_u, ff_k)],
        [(0, 1, TN, 0, None), (0, 2, TN, 1, None)], [(d, tf), (d, tf)], [],
        [wcol_sd, wcol_sd], ep_two_bf16,
        comm=exchange1(grp2, [g_wdn]),
    )
    g_wg = _plain_to_slabs("d_w_gate_slabs", g_wg_p, NDEV)
    g_wu = _plain_to_slabs("d_w_up_slabs", g_wu_p, NDEV)
    sums2 = pair_sums(grp2, [g_wdn], got)
    grp3 = ["w_gate", "w_up"]
    th = _tile(s // 2, 1024)
    ff_a = pl.BlockSpec((th, tf), lambda i, j, k: (i, k))
    w_k = pl.BlockSpec((d, tf), lambda i, j, k: (0, k))
    (d_n2,), got = _fmm(
        "d_n2", (s // th, 1, hidden // tf),
        [(d_g, ff_a), (wg_p, w_k), (d_u, ff_a), (wu_p, w_k)],
        [(0, 1, NT, 0, None), (2, 3, NT, 0, None)], [(th, d)], [],
        [(SDS((s, d), BF16), pl.BlockSpec((th, d), rows))], _ep_bf16,
        comm=_join(_rs2(sums2), exchange1(grp3, [g_wg, g_wu])),
    )
    lands.update(zip(grp2, got[:1]))
    sums3 = pair_sums(grp3, [g_wg, g_wu], got[1:])
    dh1, dh1b, part_ffn = _norm_bwd("d_h1", d_n2, h1, dh2, g_ffn, True)
    g_wo = _mm_tn("d_w_o", m_act, dh1b)

    def ep_dm(accs, ex, os_):
        ya_, yb_ = ex[0][...].astype(F32), ex[1][...].astype(F32)
        sa = _sigmoid(ex[2][...].astype(F32))
        sb = _sigmoid(ex[3][...].astype(F32))
        dm = accs[0]
        d_yb = dm * sb
        os_[0][...] = (dm * sa).astype(BF16)
        os_[1][...] = d_yb.astype(BF16)
        os_[2][...] = (dm * ya_ * sa * (1.0 - sa)).astype(BF16)
        os_[3][...] = (dm * yb_ * sb * (1.0 - sb)).astype(BF16)
        os_[4][...] = jnp.sum(d_yb, axis=0, keepdims=True)

    tile_ij = pl.BlockSpec((tm, tn), ij)
    grp4 = ["w_o"]
    (d_ya, d_yb, d_ga, d_gb, part_bpw), got = _fmm(
        "d_merge", (s // tm, d // tn, 1),
        [(dh1b, pl.BlockSpec((tm, d), row_i)), (wo, pl.BlockSpec((tn, d), lambda i, j, k: (j, 0)))],
        [(0, 1, NT, 0, None)], [(tm, tn)],
        [(ya, tile_ij), (yb, tile_ij), (proj, gate_a_spec), (proj, gate_b_spec)],
        [out_sd, out_sd, out_sd, out_sd,
         (SDS((s // tm, 1, d), F32), pl.BlockSpec((None, 1, tn), lambda i, j, k: (i, 0, j)))],
        ep_dm, csplit=EPILOGUE_CHUNK,
        comm=exchange1(grp4, [g_wo]),
    )
    sums4 = pair_sums(grp4, [g_wo], got)
    g_wouta = _mm_tn("d_w_out_a", ya_in, d_ya)
    g_wpw = _mm_tn("d_w_pw_b", v_act, d_yb)
    grp5 = ["w_out_a", "w_pw_b"]
    d_ya_in, got = _mm_nt("d_ya_in", d_ya, wouta, comm=exchange1(grp5, [g_wouta, g_wpw]))
    sums5 = pair_sums(grp5, [g_wouta, g_wpw], got)
    d_v, _ = _mm_nt("d_v", d_yb, wpw)
    d_cv, part_ln = _mix_b_bwd1(d_v, cv, conf_ln_g, conf_ln_b, s, c)
    (d_b, part_wd, part_bglu), got = _mix_b_bwd2(d_cv, u_act, proj, b_glu, wd, s, c, comm=_rs2(sums3))
    lands.update(zip(grp3, got))
    d_a, part_wa = _mix_a_bwd(d_ya_in, proj, wa, s, c)

    nb = nin // c
    gblk = d // c
    lo = [0, 3, 5, 5 + gblk]
    hi = [3, 5, 5 + gblk, 5 + 2 * gblk]
    pieces = [d_a, d_b, d_ga, d_gb]

    def active(q, ax):
        return lambda ids: jnp.logical_and(ids[ax] >= lo[q], ids[ax] < hi[q])

    def piece_spec(q, rows_, ax, row0=0):
        def index(i, j, k):
            ids = (i, j, k)
            col = jnp.clip(ids[ax] - lo[q], 0, hi[q] - lo[q] - 1)
            row = i + row0 if ax == 2 else jnp.where(active(q, ax)(ids), k, 0)
            return (row, col)

        return pl.BlockSpec((rows_, c), index)

    tkw = _tile(s, 1024)
    (g_win,), got = _fmm(
        "d_w_in", (1, nb, s // tkw),
        [(n1, pl.BlockSpec((tkw, d), lambda i, j, k: (k, 0)))]
        + [(pieces[q], piece_spec(q, tkw, 1)) for q in range(4)],
        [(0, 1 + q, TN, 0, active(q, 1)) for q in range(4)], [(d, c)], [],
        [(SDS((d, nin), BF16), pl.BlockSpec((d, c), lambda i, j, k: (0, j)))], _ep_bf16,
        comm=_rs2(sums4 + sums5),
    )
    lands.update(zip(grp4 + grp5, got))

    grp6 = ["w_in"]
    n_half = max(1, 3 * (s // th) // 8)

    def d_n1_rows(name, row0, n_tiles, comm, into):
        return _fmm(
            name, (n_tiles, 1, nb),
            [(pieces[q], piece_spec(q, th, 2, row0)) for q in range(4)]
            + [(win, pl.BlockSpec((d, c), lambda i, j, k: (0, k)))],
            [(q, 4, NT, 0, active(q, 2)) for q in range(4)], [(th, d)], [],
            [(SDS((s, d), BF16), pl.BlockSpec((th, d), lambda i, j, k: (i + row0, 0)))], _ep_bf16,
            comm=comm, into=into,
        )

    (d_n1,), got = d_n1_rows("d_n1_a", 0, n_half, exchange1(grp6, [g_win]), None)
    (d_n1,), got = d_n1_rows("d_n1_b", n_half, s // th - n_half, _rs2(pair_sums(grp6, [g_win], got)), d_n1)
    lands.update(zip(grp6, got))
    dx, part_mix = _norm_bwd("d_x", d_n1, x2, dh1, g_mix, False)

    small_parts = [part_mix, part_bglu, part_ln, part_bpw, part_ffn, part_ple, part_fin, part_wa, part_wd]
    (o_mix, o_bglu, o_ln, o_bpw, o_ffn, o_ple, o_fin, o_wa, o_wd), _ = _packed_rows(small_parts, c)
    total = _all_reduce_small(small_parts, c)
    loss = total[o_fin + d // c, 0]

    big_m = dict(w_in=m_w_in, w_out_a=m_w_out_a, w_pw_b=m_w_pw_b, w_ple_proj=m_w_ple_proj, w_o=m_w_o,
                 w_ple_gate=m_w_ple_gate, w_gate=m_w_gate, w_up=m_w_up, w_down=m_w_down)
    big_v = dict(w_in=v_w_in, w_out_a=v_w_out_a, w_pw_b=v_w_pw_b, w_ple_proj=v_w_ple_proj, w_o=v_w_o,
                 w_ple_gate=v_w_ple_gate, w_gate=v_w_gate, w_up=v_w_up, w_down=v_w_down)
    big_out = {}
    for nm in weight:
        res = _adamw_big("adamw_" + nm, lands[nm], weight[nm][0], big_m[nm][0], big_v[nm][0])
        big_out[nm] = [r[None] for r in res]

    row = lambda a: a.reshape(1, d)
    small = dict(
        g_mix=(o_mix, 0, g_mix, m_g_mix, v_g_mix),
        conv_a_w=(o_wa, CONV_A_K, conv_a_w, m_conv_a_w, v_conv_a_w),
        b_glu=(o_bglu, 0, b_glu, m_b_glu, v_b_glu),
        conf_dw_w=(o_wd, CONF_K, conf_dw_w, m_conf_dw_w, v_conf_dw_w),
        conf_dw_b=(o_ln + 2, 0, conf_dw_b, m_conf_dw_b, v_conf_dw_b),
        conf_ln_g=(o_ln, 0, conf_ln_g, m_conf_ln_g, v_conf_ln_g),
        conf_ln_b=(o_ln + 1, 0, conf_ln_b, m_conf_ln_b, v_conf_ln_b),
        b_pw_b=(o_bpw, 0, b_pw_b, m_b_pw_b, v_b_pw_b),
        g_ffn=(o_ffn, 0, g_ffn, m_g_ffn, v_g_ffn),
        g_ple=(o_ple, 0, g_ple, m_g_ple, v_g_ple),
        g_final=(o_fin, 0, row(g_final), row(m_g_final), row(v_g_final)),
    )
    small_out = dict(zip(small, _adamw_small(total, list(small.values()))))
    small_out["g_final"] = [a.reshape(d) for a in small_out["g_final"]]

    order = ["g_mix", "w_in", "conv_a_w", "w_out_a", "b_glu", "conf_dw_w", "conf_dw_b", "conf_ln_g", "conf_ln_b", "w_pw_b", "b_pw_b", "w_o", "g_ffn", "w_gate", "w_up", "w_down", "g_ple", "w_ple_gate", "w_ple_proj", "g_final"]
    allo = {**big_out, **small_out}
    outs = [loss, dx[None]]
    for q in range(4):
        outs += [allo[nm][q] for nm in order]
    return tuple(outs)
```

```python
import jax
import jax.numpy as jnp
from jax import lax
from jax.experimental import pallas as pl
from jax.experimental.pallas import tpu as pltpu

F32, BF16 = jnp.float32, jnp.bfloat16
EPS, LN_EPS = 1e-6, 1e-5
ADAM_LR, ADAM_B1, ADAM_B2, ADAM_EPS, ADAM_WD, ADAM_STEP = 0.001, 0.9, 0.999, 1e-08, 0.01, 10
CONV_A_K, CONF_K = 3, 31
NDEV = 8
NN = (((1,), (0,)), ((), ()))
NT = (((1,), (1,)), ((), ()))
TN = (((0,), (0,)), ((), ()))
V7X_VMEM_LIMIT_BYTES = 56 * 1024 * 1024
MESH = pl.DeviceIdType.MESH
SDS = jax.ShapeDtypeStruct
HALO_A, HALO_B = 16, 32
EPILOGUE_CHUNK = 256
CONV_ROWS = 32
CONF_ROWS = 16


def _tile(n, pref):
    t = min(n, pref)
    while n % t:
        t -= 8
    return t


def _sigmoid(x):
    return jax.nn.sigmoid(x)


def _params(sem=None):
    return pltpu.CompilerParams(vmem_limit_bytes=V7X_VMEM_LIMIT_BYTES, dimension_semantics=sem)


def _edge(grid, last):
    cond = None
    for ax, n in enumerate(grid):
        here = pl.program_id(ax) == (n - 1 if last else 0)
        cond = here if cond is None else jnp.logical_and(cond, here)
    return cond


def _join(*comms):
    ins, outs, alias, sems, spans = [], [], {}, [], []
    for cm in comms:
        spans.append((len(ins), len(outs), len(sems)))
        for i, o in cm["alias"].items():
            alias[len(ins) + i] = len(outs) + o
        ins += cm["ins"]
        outs += cm["outs"]
        sems += cm["sems"]

    def run(which):
        def f(i_refs, o_refs, s_refs):
            for cm, (a, b, c_) in zip(comms, spans):
                cm[which](
                    i_refs[a : a + len(cm["ins"])], o_refs[b : b + len(cm["outs"])], s_refs[c_ : c_ + len(cm["sems"])]
                )

        return f

    return dict(ins=ins, outs=outs, alias=alias, sems=sems, start=run("start"), finish=run("finish"))


def _call(body, name, grid, in_specs, args, out_specs, out_shape, scratch=(), sem=None, comm=None, alias=None):
    n_in, n_out, n_s = len(args), len(out_shape), len(scratch)
    alias = dict(alias or {})
    if comm is None:
        res = pl.pallas_call(
            body, name=name, grid=grid, in_specs=list(in_specs), out_specs=list(out_specs), out_shape=list(out_shape),
            scratch_shapes=list(scratch), input_output_aliases=alias, compiler_params=_params(sem),
        )(*args)
        return list(res), []
    n_ci, n_co = len(comm["ins"]), len(comm["outs"])

    def wrapped(*refs):
        ins = refs[:n_in]
        ci = refs[n_in : n_in + n_ci]
        o0 = n_in + n_ci
        outs = refs[o0 : o0 + n_out]
        co = refs[o0 + n_out : o0 + n_out + n_co]
        s0 = o0 + n_out + n_co
        sc = refs[s0 : s0 + n_s]
        cs = refs[s0 + n_s :]
        pl.when(_edge(grid, False))(lambda: comm["start"](ci, co, cs))
        body(*ins, *outs, *sc)
        pl.when(_edge(grid, True))(lambda: comm["finish"](ci, co, cs))

    hbm = pl.BlockSpec(memory_space=pl.ANY)
    res = pl.pallas_call(
        wrapped,
        name=name,
        grid=grid,
        in_specs=list(in_specs) + [hbm] * n_ci,
        out_specs=list(out_specs) + [hbm] * n_co,
        out_shape=list(out_shape) + list(comm["outs"]),
        scratch_shapes=list(scratch) + list(comm["sems"]),
        input_output_aliases={**alias, **{n_in + i: n_out + o for i, o in comm["alias"].items()}},
        compiler_params=_params(("arbitrary",) * len(grid)),
    )(*args, *comm["ins"])
    return list(res[:n_out]), list(res[n_out:])


def _col_chunks(n, pref):
    out, c0 = [], 0
    while c0 < n:
        w = min(pref, n - c0)
        out.append((c0, w))
        c0 += w
    return out


def _fmm(name, grid, operands, terms, acc_shapes, extras, outs, epilogue, comm=None, csplit=None, into=None):
    n_p, n_e, n_o, n_a = len(operands), len(extras), len(outs), len(acc_shapes)
    nk = grid[-1]
    kax = len(grid) - 1
    simple = nk == 1 and all(t[4] is None for t in terms)
    alias = None
    if into is not None:
        extras = list(extras) + [(into, pl.BlockSpec(memory_space=pl.ANY))]
        alias = {n_p + n_e: 0}
        n_e += 1
    if csplit is not None:
        assert simple and into is None and all(t[2] in (NN, NT) and (len(t) <= 5 or not t[5]) for t in terms)
        tn_ = acc_shapes[0][1]
        chunks = _col_chunks(tn_, csplit)

    def dot(a, b, dims):
        if a.dtype != BF16:
            a = a.astype(BF16)
        if b.dtype != BF16:
            b = b.astype(BF16)
        return lax.dot_general(a, b, dims, preferred_element_type=F32)

    def value(refs, term):
        slabs = term[5] if len(term) > 5 else 0
        if not slabs:
            return dot(refs[term[0]][...], refs[term[1]][...], term[2])
        tot = None
        for sl in range(slabs):
            d = dot(refs[term[0]][sl], refs[term[1]][sl], term[2])
            tot = d if tot is None else tot + d
        return tot

    def always(refs):
        parts = [None] * n_a
        for term in terms:
            if term[4] is None:
                d = value(refs, term)
                parts[term[3]] = d if parts[term[3]] is None else parts[term[3]] + d
        return parts

    def chunked(refs, ex, os_, accs):
        cols = lambda ref, c0, w: ref.at[:, pl.ds(c0, w)] if ref.shape[-1] == tn_ else ref

        def dots(k):
            c0, w = chunks[k]
            parts = [None] * n_a
            for term in terms:
                b_ref = refs[term[1]]
                b = b_ref[:, pl.ds(c0, w)] if term[2] == NN else b_ref[pl.ds(c0, w), :]
                d = dot(refs[term[0]][...], b, term[2])
                parts[term[3]] = d if parts[term[3]] is None else parts[term[3]] + d
            for ai in range(n_a):
                accs[ai][k % 2, :, pl.ds(0, w)] = parts[ai]

        def finish(k):
            c0, w = chunks[k]
            vals = [accs[ai][k % 2, :, pl.ds(0, w)] for ai in range(n_a)]
            epilogue(vals, [cols(e, c0, w) for e in ex], [cols(o, c0, w) for o in os_])

        dots(0)
        for k in range(1, len(chunks)):
            dots(k)
            finish(k - 1)
        finish(len(chunks) - 1)

    def body(*refs):
        ex = refs[n_p : n_p + n_e]
        os_ = refs[n_p + n_e : n_p + n_e + n_o]
        accs = refs[n_p + n_e + n_o :]
        if simple and csplit is not None:
            chunked(refs, ex, os_, accs)
            return
        if simple:
            epilogue(always(refs), ex, os_)
            return
        ids = [pl.program_id(ax) for ax in range(len(grid))]
        k = ids[kax]

        @pl.when(k == 0)
        def _():
            for acc in accs:
                acc[...] = jnp.zeros(acc.shape, F32)

        for ai, part in enumerate(always(refs)):
            if part is not None:
                accs[ai][...] += part
        for term in terms:
            if term[4] is not None:

                def add(term=term):
                    accs[term[3]][...] += value(refs, term)

                pl.when(term[4](ids))(add)

        @pl.when(k == nk - 1)
        def _():
            epilogue([acc[...] for acc in accs], ex, os_)

    return _call(
        body,
        name,
        grid,
        [o[1] for o in operands] + [e[1] for e in extras],
        [o[0] for o in operands] + [e[0] for e in extras],
        [o[1] for o in outs],
        [o[0] for o in outs],
        scratch=[pltpu.VMEM((2, s[0], csplit), F32) for s in acc_shapes] if csplit is not None
        else [] if simple else [pltpu.VMEM(s, F32) for s in acc_shapes],
        sem=("parallel",) * kax + ("arbitrary",),
        comm=comm,
        alias=alias,
    )


def _rms_bwd(dn_raw, h, g):
    r = lax.rsqrt(jnp.mean(h * h, axis=-1, keepdims=True) + EPS)
    hn = h * r
    dg = jnp.sum(dn_raw * hn, axis=0, keepdims=True)
    dn = dn_raw * g
    dh = r * (dn - hn * jnp.mean(dn * hn, axis=-1, keepdims=True))
    return dh, dg


def _rms_fwd(name, h, g):
    s, d = h.shape
    ts = _tile(s, 512)

    def body(h_ref, g_ref, o_ref):
        x = h_ref[...]
        r = lax.rsqrt(jnp.mean(x * x, axis=-1, keepdims=True) + EPS)
        o_ref[...] = (x * r * g_ref[...]).astype(BF16)

    return pl.pallas_call(
        body,
        name=name,
        grid=(s // ts,),
        in_specs=[pl.BlockSpec((ts, d), lambda i: (i, 0)), pl.BlockSpec((1, d), lambda i: (0, 0))],
        out_specs=pl.BlockSpec((ts, d), lambda i: (i, 0)),
        out_shape=SDS((s, d), BF16),
        compiler_params=_params(("parallel",)),
    )(h, g)


def _norm_bwd(name, dn, h, dres, g, out_dtype):
    s, d = h.shape
    ts = _tile(s, 512)

    def body(dn_r, h_r, dres_r, g_r, dh_o, part_o):
        dh, dg = _rms_bwd(dn_r[...].astype(F32), h_r[...], g_r[...])
        dh_o[...] = (dres_r[...].astype(F32) + dh).astype(out_dtype)
        part_o[...] = dg

    blk = pl.BlockSpec((ts, d), lambda i: (i, 0))
    part = pl.BlockSpec((None, 1, d), lambda i: (i, 0, 0))
    return pl.pallas_call(
        body,
        name=name,
        grid=(s // ts,),
        in_specs=[blk, blk, blk, pl.BlockSpec((1, d), lambda i: (0, 0))],
        out_specs=[blk, part],
        out_shape=[SDS((s, d), out_dtype), SDS((s // ts, 1, d), F32)],
        compiler_params=_params(("parallel",)),
    )(dn, h, dres, g)


def _slabs_to_plain(name, x):
    n, rows, w = x.shape
    tr = _tile(rows, 256)

    def body(x_ref, o_ref):
        for j in range(n):
            o_ref[:, j * w : (j + 1) * w] = x_ref[j]

    return pl.pallas_call(
        body,
        name=name,
        grid=(rows // tr,),
        in_specs=[pl.BlockSpec((n, tr, w), lambda i: (0, i, 0))],
        out_specs=pl.BlockSpec((tr, n * w), lambda i: (i, 0)),
        out_shape=SDS((rows, n * w), x.dtype),
        compiler_params=_params(("parallel",)),
    )(x)


def _plain_to_slabs(name, x, n):
    rows, nw = x.shape
    w = nw // n
    tr = _tile(rows, 256)

    def body(x_ref, o_ref):
        for j in range(n):
            o_ref[j] = x_ref[:, j * w : (j + 1) * w]

    return pl.pallas_call(
        body,
        name=name,
        grid=(rows // tr,),
        in_specs=[pl.BlockSpec((tr, nw), lambda i: (i, 0))],
        out_specs=pl.BlockSpec((n, tr, w), lambda i: (0, i, 0)),
        out_shape=SDS((n, rows, w), x.dtype),
        compiler_params=_params(("parallel",)),
    )(x)


def _prev_halo(ts, hb):
    r = ts // hb
    return lambda i: jnp.maximum(i * r - 1, 0)


def _next_halo(ts, hb, s):
    r = ts // hb
    last = s // hb - 1
    return lambda i: jnp.minimum((i + 1) * r, last)


def _shift_copies(buf, sh):
    n = sh.shape[1]
    for j in range(1, 8):
        sh[j - 1, pl.ds(0, n), :] = buf[pl.ds(j, n), :]


def _tap(buf, sh, r0, off, rows):
    j = off % 8
    start = pl.multiple_of(r0 + (off - j), 8)
    if j == 0:
        return buf[pl.ds(start, rows), :]
    return sh[j - 1, pl.ds(start, rows), :]


def _mix_a_fwd(proj, wa, s, c):
    ts, hb = _tile(s, 256), HALO_A
    prev = _prev_halo(ts, hb)

    def body(ah, ab, ac, hh, hc, w, o, buf):
        i = pl.program_id(0)
        zh = hc[...].astype(F32) * hh[...].astype(F32)
        buf[pl.ds(0, hb), :] = jnp.where(i == 0, 0.0, zh)
        buf[pl.ds(hb, ts), :] = ac[...].astype(F32) * ah[...].astype(F32)
        for r0 in range(0, ts, CONV_ROWS):
            cz = jnp.zeros((CONV_ROWS, c), F32)
            for k in range(CONV_A_K):
                cz = cz + w[k : k + 1, :] * buf[pl.ds(hb + r0 - (CONV_A_K - 1) + k, CONV_ROWS), :]
            o[pl.ds(r0, CONV_ROWS), :] = (ab[pl.ds(r0, CONV_ROWS), :].astype(F32) * cz).astype(BF16)

    main = lambda cb: pl.BlockSpec((ts, c), lambda i: (i, cb))
    halo = lambda cb: pl.BlockSpec((hb, c), lambda i: (prev(i), cb))
    return pl.pallas_call(
        body,
        name="mix_a_fwd",
        grid=(s // ts,),
        in_specs=[main(0), main(1), main(2), halo(0), halo(2), pl.BlockSpec(wa.shape, lambda i: (0, 0))],
        out_specs=pl.BlockSpec((ts, c), lambda i: (i, 0)),
        out_shape=SDS((s, c), BF16),
        scratch_shapes=[pltpu.VMEM((hb + ts, c), F32)],
        compiler_params=_params(("parallel",)),
    )(proj, proj, proj, proj, proj, wa)


def _mix_b_fwd(proj, b_glu, wd, bd, lg, lb, s, c, comm=None):
    ts, hb = _tile(s, 256), HALO_B
    prev = _prev_halo(ts, hb)

    def body(gv, gg, hv, hg, bglu, w, bd_r, lg_r, lb_r, v_o, u_o, cv_o, buf, sh):
        i = pl.program_id(0)
        bv, bg = bglu[:, 0:c], bglu[:, c : 2 * c]
        uh = (hv[...].astype(F32) + bv) * _sigmoid(hg[...].astype(F32) + bg)
        buf[pl.ds(0, hb), :] = jnp.where(i == 0, 0.0, uh)
        u = (gv[...].astype(F32) + bv) * _sigmoid(gg[...].astype(F32) + bg)
        buf[pl.ds(hb, ts), :] = u
        u_o[...] = u.astype(BF16)
        _shift_copies(buf, sh)

        def chunk(ci, carry):
            r0 = pl.multiple_of(ci * CONF_ROWS, CONF_ROWS)
            acc = jnp.zeros((CONF_ROWS, c), F32)
            for k in range(CONF_K):
                acc = acc + w[k : k + 1, :] * _tap(buf, sh, r0, hb - (CONF_K - 1) + k, CONF_ROWS)
            cv_o[pl.ds(r0, CONF_ROWS), :] = acc + bd_r[...]
            return carry

        lax.fori_loop(0, ts // CONF_ROWS, chunk, 0)
        cv = cv_o[...]
        mu = jnp.mean(cv, axis=-1, keepdims=True)
        xc = cv - mu
        rs = lax.rsqrt(jnp.mean(xc * xc, axis=-1, keepdims=True) + LN_EPS)
        ln = xc * rs * lg_r[...] + lb_r[...]
        v_o[...] = (ln * _sigmoid(ln)).astype(BF16)

    main = lambda cb: pl.BlockSpec((ts, c), lambda i: (i, cb))
    halo = lambda cb: pl.BlockSpec((hb, c), lambda i: (prev(i), cb))
    full = lambda a: pl.BlockSpec(a.shape, lambda i: (0, 0))
    out = pl.BlockSpec((ts, c), lambda i: (i, 0))
    return _call(
        body,
        "mix_b_fwd",
        (s // ts,),
        [main(3), main(4), halo(3), halo(4), full(b_glu), full(wd), full(bd), full(lg), full(lb)],
        [proj, proj, proj, proj, b_glu, wd, bd, lg, lb],
        [out, out, out],
        [SDS((s, c), BF16), SDS((s, c), BF16), SDS((s, c), F32)],
        scratch=[pltpu.VMEM((hb + ts, c), F32), pltpu.VMEM((7, hb + ts - 8, c), F32)],
        sem=("parallel",),
        comm=comm,
    )


def _mix_b_bwd1(d_v, cv, lg, lb, s, c):
    ts = _tile(s, 256)

    def body(dv_r, cv_r, lg_r, lb_r, dcv_o, part_o):
        cv_ = cv_r[...]
        mu = jnp.mean(cv_, axis=-1, keepdims=True)
        xc = cv_ - mu
        rs = lax.rsqrt(jnp.mean(xc * xc, axis=-1, keepdims=True) + LN_EPS)
        xh = xc * rs
        ln = xh * lg_r[...] + lb_r[...]
        sg = _sigmoid(ln)
        d_ln = dv_r[...].astype(F32) * (sg * (1.0 + ln * (1.0 - sg)))
        dy = d_ln * lg_r[...]
        d_cv = rs * (dy - jnp.mean(dy, axis=-1, keepdims=True) - xh * jnp.mean(dy * xh, axis=-1, keepdims=True))
        dcv_o[...] = d_cv
        part_o[0:1, :] = jnp.sum(d_ln * xh, axis=0, keepdims=True)
        part_o[1:2, :] = jnp.sum(d_ln, axis=0, keepdims=True)
        part_o[2:3, :] = jnp.sum(d_cv, axis=0, keepdims=True)

    blk = pl.BlockSpec((ts, c), lambda i: (i, 0))
    full = lambda a: pl.BlockSpec(a.shape, lambda i: (0, 0))
    return pl.pallas_call(
        body,
        name="mix_b_bwd_ln",
        grid=(s // ts,),
        in_specs=[blk, blk, full(lg), full(lb)],
        out_specs=[blk, pl.BlockSpec((None, 3, c), lambda i: (i, 0, 0))],
        out_shape=[SDS((s, c), F32), SDS((s // ts, 3, c), F32)],
        compiler_params=_params(("parallel",)),
    )(d_v, cv, lg, lb)


def _mix_b_bwd2(d_cv, u, proj, b_glu, wd, s, c, comm=None):
    ts, hb = _tile(s, 256), HALO_B
    prev, nxt = _prev_halo(ts, hb), _next_halo(ts, hb, s)
    n_t = s // ts
    kp = wd.shape[0]

    def body(dcv, dcv_n, u_m, u_p, gv, gg, bglu, w, d_o, dwd_o, dbglu_o, dbuf, ubuf, dub, dsh, ush, dwacc):
        i = pl.program_id(0)
        dbuf[pl.ds(0, ts), :] = dcv[...]
        dbuf[pl.ds(ts, hb), :] = jnp.where(i == n_t - 1, 0.0, dcv_n[...])
        ubuf[pl.ds(0, hb), :] = jnp.where(i == 0, 0.0, u_p[...].astype(F32))
        ubuf[pl.ds(hb, ts), :] = u_m[...].astype(F32)
        _shift_copies(dbuf, dsh)
        _shift_copies(ubuf, ush)
        dwacc[...] = jnp.zeros(dwacc.shape, F32)

        def chunk(ci, carry):
            r0 = pl.multiple_of(ci * CONF_ROWS, CONF_ROWS)
            acc = jnp.zeros((CONF_ROWS, c), F32)
            dc = dbuf[pl.ds(r0, CONF_ROWS), :]
            for k in range(CONF_K):
                acc = acc + w[k : k + 1, :] * _tap(dbuf, dsh, r0, (CONF_K - 1) - k, CONF_ROWS)
                prod = dc * _tap(ubuf, ush, r0, hb - (CONF_K - 1) + k, CONF_ROWS)
                fold = prod[0:8]
                for a in range(1, CONF_ROWS // 8):
                    fold = fold + prod[8 * a : 8 * a + 8]
                dwacc[pl.ds(8 * k, 8), :] += fold
            dub[pl.ds(r0, CONF_ROWS), :] = acc
            return carry

        lax.fori_loop(0, ts // CONF_ROWS, chunk, 0)
        for k in range(CONF_K):
            dwd_o[k : k + 1, :] = jnp.sum(dwacc[pl.ds(8 * k, 8), :], axis=0, keepdims=True)
        dwd_o[CONF_K:kp, :] = jnp.zeros((kp - CONF_K, c), F32)
        bv, bg = bglu[:, 0:c], bglu[:, c : 2 * c]
        d_u = dub[...]
        sg = _sigmoid(gg[...].astype(F32) + bg)
        d_gv = d_u * sg
        d_gg = d_u * (gv[...].astype(F32) + bv) * sg * (1.0 - sg)
        d_o[:, 0:c] = d_gv.astype(BF16)
        d_o[:, c : 2 * c] = d_gg.astype(BF16)
        dbglu_o[:, 0:c] = jnp.sum(d_gv, axis=0, keepdims=True)
        dbglu_o[:, c : 2 * c] = jnp.sum(d_gg, axis=0, keepdims=True)

    blk = lambda cb: pl.BlockSpec((ts, c), lambda i: (i, cb))
    full = lambda a: pl.BlockSpec(a.shape, lambda i: (0, 0))
    return _call(
        body,
        "mix_b_bwd_conv",
        (n_t,),
        [
            blk(0),
            pl.BlockSpec((hb, c), lambda i: (nxt(i), 0)),
            blk(0),
            pl.BlockSpec((hb, c), lambda i: (prev(i), 0)),
            blk(3),
            blk(4),
            full(b_glu),
            full(wd),
        ],
        [d_cv, d_cv, u, u, proj, proj, b_glu, wd],
        [
            pl.BlockSpec((ts, 2 * c), lambda i: (i, 0)),
            pl.BlockSpec((None, kp, c), lambda i: (i, 0, 0)),
            pl.BlockSpec((None, 1, 2 * c), lambda i: (i, 0, 0)),
        ],
        [SDS((s, 2 * c), BF16), SDS((n_t, kp, c), F32), SDS((n_t, 1, 2 * c), F32)],
        scratch=[
            pltpu.VMEM((ts + hb, c), F32), pltpu.VMEM((hb + ts, c), F32), pltpu.VMEM((ts, c), F32),
            pltpu.VMEM((7, hb + ts - 8, c), F32), pltpu.VMEM((7, hb + ts - 8, c), F32), pltpu.VMEM((8 * CONF_K, c), F32),
        ],
        sem=("parallel",),
        comm=comm,
    )


def _mix_a_bwd(d_ya, proj, wa, s, c):
    ts, hb = _tile(s, 256), HALO_A
    prev, nxt = _prev_halo(ts, hb), _next_halo(ts, hb, s)
    n_t = s // ts
    kp = wa.shape[0]

    def body(dya, dya_n, ah, ab, ac, ah_p, ac_p, ab_n, w, d_o, dwa_o, zbuf, dbuf, dzb):
        i = pl.program_id(0)
        zbuf[pl.ds(0, hb), :] = jnp.where(i == 0, 0.0, ac_p[...].astype(F32) * ah_p[...].astype(F32))
        zbuf[pl.ds(hb, ts), :] = ac[...].astype(F32) * ah[...].astype(F32)
        dbuf[pl.ds(0, ts), :] = dya[...].astype(F32) * ab[...].astype(F32)
        dbuf[pl.ds(ts, hb), :] = jnp.where(i == n_t - 1, 0.0, dya_n[...].astype(F32) * ab_n[...].astype(F32))
        dw_rows = [jnp.zeros((1, c), F32) for _ in range(CONV_A_K)]
        for r0 in range(0, ts, CONV_ROWS):
            cz = jnp.zeros((CONV_ROWS, c), F32)
            dz = jnp.zeros((CONV_ROWS, c), F32)
            dc = dbuf[pl.ds(r0, CONV_ROWS), :]
            for k in range(CONV_A_K):
                zk = zbuf[pl.ds(hb + r0 - (CONV_A_K - 1) + k, CONV_ROWS), :]
                cz = cz + w[k : k + 1, :] * zk
                dz = dz + w[k : k + 1, :] * dbuf[pl.ds(r0 + (CONV_A_K - 1) - k, CONV_ROWS), :]
                dw_rows[k] = dw_rows[k] + jnp.sum(dc * zk, axis=0, keepdims=True)
            d_o[pl.ds(r0, CONV_ROWS), c : 2 * c] = (dya[pl.ds(r0, CONV_ROWS), :].astype(F32) * cz).astype(BF16)
            dzb[pl.ds(r0, CONV_ROWS), :] = dz
        d_z = dzb[...]
        d_o[:, 0:c] = (d_z * ac[...].astype(F32)).astype(BF16)
        d_o[:, 2 * c : 3 * c] = (d_z * ah[...].astype(F32)).astype(BF16)
        for k in range(CONV_A_K):
            dwa_o[k : k + 1, :] = dw_rows[k]
        dwa_o[CONV_A_K:kp, :] = jnp.zeros((kp - CONV_A_K, c), F32)

    blk = lambda cb: pl.BlockSpec((ts, c), lambda i: (i, cb))
    hp = lambda cb: pl.BlockSpec((hb, c), lambda i: (prev(i), cb))
    hn = lambda cb: pl.BlockSpec((hb, c), lambda i: (nxt(i), cb))
    return pl.pallas_call(
        body,
        name="mix_a_bwd",
        grid=(n_t,),
        in_specs=[blk(0), hn(0), blk(0), blk(1), blk(2), hp(0), hp(2), hn(1), pl.BlockSpec(wa.shape, lambda i: (0, 0))],
        out_specs=[pl.BlockSpec((ts, 3 * c), lambda i: (i, 0)), pl.BlockSpec((None, kp, c), lambda i: (i, 0, 0))],
        out_shape=[SDS((s, 3 * c), BF16), SDS((n_t, kp, c), F32)],
        scratch_shapes=[pltpu.VMEM((hb + ts, c), F32), pltpu.VMEM((ts + hb, c), F32), pltpu.VMEM((ts, c), F32)],
        compiler_params=_params(("parallel",)),
    )(d_ya, d_ya, proj, proj, proj, proj, proj, proj, wa)


def _ep_bf16(accs, ex, os_):
    os_[0][...] = accs[0].astype(BF16)


def _mm_tn(name, a, b, tm=2048, tn=1024, tk=1024):
    m, k1 = a.shape
    n = b.shape[1]
    tm, tn, tk = _tile(k1, tm), _tile(n, tn), _tile(m, tk)
    return _fmm(
        name,
        (k1 // tm, n // tn, m // tk),
        [(a, pl.BlockSpec((tk, tm), lambda i, j, k: (k, i))), (b, pl.BlockSpec((tk, tn), lambda i, j, k: (k, j)))],
        [(0, 1, TN, 0, None)],
        [(tm, tn)],
        [],
        [(SDS((k1, n), BF16), pl.BlockSpec((tm, tn), lambda i, j, k: (i, j)))],
        _ep_bf16,
    )[0][0]


def _mm_nt(name, a, b, tm=1024, tn=1024, comm=None):
    m, kk = a.shape
    n = b.shape[0]
    tm, tn = _tile(m, tm), _tile(n, tn)
    outs, couts = _fmm(
        name,
        (m // tm, n // tn, 1),
        [(a, pl.BlockSpec((tm, kk), lambda i, j, k: (i, 0))), (b, pl.BlockSpec((tn, kk), lambda i, j, k: (j, 0)))],
        [(0, 1, NT, 0, None)],
        [(tm, tn)],
        [],
        [(SDS((m, n), BF16), pl.BlockSpec((tm, tn), lambda i, j, k: (i, j)))],
        _ep_bf16,
        comm=comm,
    )
    return outs[0], couts


def _dev_index(dev):
    return 4 * dev[0] + 2 * dev[1] + dev[2]


def _region(ref, kind, j, shard_shape):
    if kind == "col":
        ns = shard_shape[1]
        return ref.at[:, pl.ds(pl.multiple_of(j * ns, 128), ns)]
    if kind == "row":
        rs = shard_shape[0]
        return ref.at[pl.ds(pl.multiple_of(j * rs, 8), rs), :]
    return ref.at[j]


def _whole_shape(kind, shard_shape):
    if kind == "col":
        return (shard_shape[0], NDEV * shard_shape[1])
    if kind == "row":
        return (NDEV * shard_shape[0], shard_shape[1])
    return (NDEV,) + tuple(shard_shape)


def _place():
    return lax.axis_index("x"), lax.axis_index("y"), lax.axis_index("c")


def _proj_gather(n1, w_shard, early, late):
    s, d = n1.shape
    ns = w_shard.shape[1]
    pw = 2 * ns
    tm = _tile(s // 2, 512)
    n_i = s // tm
    comm = _join(early, late)
    n_early = (len(early["ins"]), len(early["outs"]), len(early["sems"]))
    assert n_i >= 2 and not comm["alias"]
    x0, y0, _ = _place()
    order = jnp.stack([2 * x0 + y0, 2 * x0 + (1 - y0), 2 * (1 - x0) + y0, 2 * (1 - x0) + (1 - y0)]).astype(jnp.int32)
    n_ci, n_co = len(comm["ins"]), len(comm["outs"])

    def body(order_ref, n1_ref, wsh_ref, *rest):
        ci = rest[:n_ci]
        proj_ref, win_ref = rest[n_ci], rest[n_ci + 1]
        co = rest[n_ci + 2 : n_ci + 2 + n_co]
        wfull, send, recv, fsend, frecv, loc, osem = rest[n_ci + 2 + n_co : n_ci + 9 + n_co]
        cs = rest[n_ci + 9 + n_co :]
        u, i = pl.program_id(0), pl.program_id(1)
        x, y, c = _place()
        sib = (x, y, 1 - c)
        chips = [(x, y), (x, 1 - y), (1 - x, y), (1 - x, 1 - y)]
        peers = [sib] + [(*ch, c) for ch in chips[1:]]
        blk = lambda ch, core: wfull.at[2 * ch[0] + ch[1], :, pl.ds(pl.multiple_of(core * ns, 128), ns)]
        sends = [_remote(blk(chips[0], c), blk(chips[0], c), send.at[k], recv.at[k], peers[k]) for k in range(4)]
        arrivals = [_remote(blk(chips[0], 1 - c), blk(chips[0], 1 - c), send.at[0], recv.at[0], sib)] + [
            _remote(blk(chips[k], c), blk(chips[k], c), send.at[k], recv.at[k], peers[k]) for k in range(1, 4)
        ]
        passes = [_remote(blk(chips[k], c), blk(chips[k], c), fsend.at[k - 1], frecv.at[k - 1], sib) for k in range(1, 4)]
        passed = [_remote(blk(chips[k], 1 - c), blk(chips[k], 1 - c), fsend.at[k - 1], frecv.at[k - 1], sib) for k in range(1, 4)]
        mine = lambda: pltpu.make_async_copy(wsh_ref, blk(chips[0], c), loc.at[0])

        def to_hbm(unit):
            q = order_ref[unit]
            return pltpu.make_async_copy(wfull.at[q], win_ref.at[:, pl.ds(pl.multiple_of(q * pw, 128), pw)], osem.at[unit])

        a, b, e = n_early
        early_refs = (ci[:a], co[:b], cs[:e])
        late_refs = (ci[a:], co[b:], cs[e:])

        @pl.when(jnp.logical_and(u == 0, i == 0))
        def _():
            mine().start()
            mine().wait()
            for snd in sends[:3]:
                snd().start()
            early["start"](*early_refs)
            arrivals[0]().wait_recv()

        @pl.when(jnp.logical_and(u == 1, i == 0))
        def _():
            sends[3]().start()

        @pl.when(jnp.logical_and(u == 2, i == 0))
        def _():
            late["start"](*late_refs)

        for nxt in range(1, 4):

            @pl.when(jnp.logical_and(u == nxt - 1, i == n_i - 1))
            def _(nxt=nxt):
                passed[nxt - 1]().wait_recv()

        proj_ref[...] = jnp.dot(n1_ref[...], wfull[order_ref[u]], preferred_element_type=F32).astype(BF16)

        for nxt in range(1, 4):

            @pl.when(jnp.logical_and(u == nxt - 1, i == n_i - 2))
            def _(nxt=nxt):
                arrivals[nxt]().wait_recv()
                passes[nxt - 1]().start()

        for unit in range(4):

            @pl.when(jnp.logical_and(u == unit, i == n_i - 1))
            def _(unit=unit):
                to_hbm(unit).start()

        @pl.when(jnp.logical_and(u == 3, i == n_i - 1))
        def _():
            for snd in sends + passes:
                snd().wait_send()
            for unit in range(4):
                to_hbm(unit).wait()
            comm["finish"](ci, co, cs)

    hbm = pl.BlockSpec(memory_space=pl.ANY)
    dma = pltpu.SemaphoreType.DMA
    res = pl.pallas_call(
        body,
        name="proj",
        grid_spec=pltpu.PrefetchScalarGridSpec(
            num_scalar_prefetch=1,
            grid=(4, n_i),
            in_specs=[pl.BlockSpec((tm, d), lambda u, i, order_ref: (i, 0)), hbm] + [hbm] * n_ci,
            out_specs=[pl.BlockSpec((tm, pw), lambda u, i, order_ref: (i, order_ref[u])), hbm] + [hbm] * n_co,
            scratch_shapes=[pltpu.VMEM((4, d, pw), BF16), dma((4,)), dma((4,)), dma((3,)), dma((3,)), dma((1,)), dma((4,))]
            + list(comm["sems"]),
        ),
        out_shape=[SDS((s, NDEV * ns), BF16), SDS((d, NDEV * ns), BF16)] + list(comm["outs"]),
        compiler_params=_params(("arbitrary", "arbitrary")),
    )(order, n1, w_shard, *comm["ins"])
    return res[0], res[1], list(res[2:])


def _peer(me, r):
    x, y, c = me
    return (1 - x if r & 4 else x, 1 - y if r & 2 else y, 1 - c if r & 1 else c)


def _remote(src, dst, send_sem, recv_sem, to):
    return lambda: pltpu.make_async_remote_copy(
        src_ref=src, dst_ref=dst, send_sem=send_sem, recv_sem=recv_sem, device_id=to, device_id_type=MESH
    )


def _run(pairs, locals_, start):
    if start:
        for cp in locals_:
            cp.start()
        for snd, _ in pairs:
            snd().start()
    else:
        for snd, arr in pairs:
            arr().wait_recv()
            snd().wait_send()
        for cp in locals_:
            cp.wait()


def _stage(ins, outs, alias, sems, build):
    return dict(
        ins=list(ins), outs=list(outs), alias=alias, sems=list(sems),
        start=lambda i, o, s: _run(*build(i, o, s), True),
        finish=lambda i, o, s: _run(*build(i, o, s), False),
    )


def _ag1(shards, kinds):
    n_t = len(shards)
    shapes = [tuple(sh.shape) for sh in shards]

    def build(srcs, dsts, sems):
        send, recv, loc = sems
        x, y, c = _place()
        me = (x, y, c)
        peers = [(x, y, 1 - c), (1 - x, y, c), (x, 1 - y, c), (1 - x, 1 - y, c)]
        reg = lambda t, dev: _region(dsts[t], kinds[t], _dev_index(dev), shapes[t])
        pairs = []
        for t in range(n_t):
            for k, peer in enumerate(peers):
                snd = _remote(srcs[t], reg(t, me), send.at[t, k], recv.at[t, k], peer)
                arr = _remote(reg(t, peer), reg(t, peer), send.at[t, k], recv.at[t, k], peer)
                pairs.append((snd, arr))
        mine = [pltpu.make_async_copy(srcs[t], reg(t, me), loc.at[t]) for t in range(n_t)]
        return pairs, mine

    outs = [SDS(_whole_shape(kinds[t], shapes[t]), shards[t].dtype) for t in range(n_t)]
    dma = pltpu.SemaphoreType.DMA
    return _stage(shards, outs, {}, [dma((n_t, 4)), dma((n_t, 4)), dma((n_t,))], build)


def _ag_direct(shards, kinds):
    n_t = len(shards)
    shapes = [tuple(sh.shape) for sh in shards]

    def build(srcs, dsts, sems):
        send, recv, loc = sems
        me = _place()
        reg = lambda t, dev: _region(dsts[t], kinds[t], _dev_index(dev), shapes[t])
        pairs = []
        for t in range(n_t):
            for r in range(1, NDEV):
                peer = _peer(me, r)
                snd = _remote(srcs[t], reg(t, me), send.at[t, r - 1], recv.at[t, r - 1], peer)
                arr = _remote(reg(t, peer), reg(t, peer), send.at[t, r - 1], recv.at[t, r - 1], peer)
                pairs.append((snd, arr))
        mine = [pltpu.make_async_copy(srcs[t], reg(t, me), loc.at[t]) for t in range(n_t)]
        return pairs, mine

    outs = [SDS(_whole_shape(kinds[t], shapes[t]), shards[t].dtype) for t in range(n_t)]
    dma = pltpu.SemaphoreType.DMA
    return _stage(shards, outs, {}, [dma((n_t, 7)), dma((n_t, 7)), dma((n_t,))], build)


def _ag2(wholes, kinds, shapes):
    n_t = len(wholes)

    def build(_, dsts, sems):
        send, recv = sems
        x, y, c = _place()
        sib = (x, y, 1 - c)
        chips = [(1 - x, y), (x, 1 - y), (1 - x, 1 - y)]
        reg = lambda t, dev: _region(dsts[t], kinds[t], _dev_index(dev), shapes[t])
        pairs = []
        for t in range(n_t):
            for j, chip in enumerate(chips):
                snd = _remote(reg(t, (*chip, c)), reg(t, (*chip, c)), send.at[t, j], recv.at[t, j], sib)
                arr = _remote(reg(t, (*chip, 1 - c)), reg(t, (*chip, 1 - c)), send.at[t, j], recv.at[t, j], sib)
                pairs.append((snd, arr))
        return pairs, []

    outs = [SDS(w.shape, w.dtype) for w in wholes]
    dma = pltpu.SemaphoreType.DMA
    return _stage(wholes, outs, {t: t for t in range(n_t)}, [dma((n_t, 3)), dma((n_t, 3))], build)


def _chip_of(q):
    return (q >> 1, q & 1)


def _rs1(wholes, kinds, shapes):
    n_t = len(wholes)

    def build(srcs, outs, sems):
        send, recv = sems
        x, y, c = _place()
        sib = (x, y, 1 - c)
        pairs = []
        for t in range(n_t):
            for q in range(4):
                theirs = _region(srcs[t], kinds[t], _dev_index((*_chip_of(q), 1 - c)), shapes[t])
                pairs.append((
                    _remote(theirs, outs[t].at[q], send.at[t, q], recv.at[t, q], sib),
                    _remote(outs[t].at[q], outs[t].at[q], send.at[t, q], recv.at[t, q], sib),
                ))
        return pairs, []

    slabs = [SDS((4,) + tuple(shapes[t]), wholes[t].dtype) for t in range(n_t)]
    dma = pltpu.SemaphoreType.DMA
    return _stage(wholes, slabs, {}, [dma((n_t, 4)), dma((n_t, 4))], build)


def _rs2(pair_sums):
    n_t = len(pair_sums)

    def build(srcs, lands, sems):
        send, recv, loc = sems
        x, y, c = _place()
        my_chip = 2 * x + y
        pairs, mine = [], []
        for t in range(n_t):
            for j, (px, py) in enumerate([(1 - x, y), (x, 1 - y), (1 - x, 1 - y)]):
                q = 2 * px + py
                pairs.append((
                    _remote(srcs[t].at[q], lands[t].at[my_chip], send.at[t, j], recv.at[t, j], (px, py, c)),
                    _remote(lands[t].at[q], lands[t].at[q], send.at[t, j], recv.at[t, j], (px, py, c)),
                ))
            mine.append(pltpu.make_async_copy(srcs[t].at[my_chip], lands[t].at[my_chip], loc.at[t]))
        return pairs, mine

    outs = [SDS(q.shape, q.dtype) for q in pair_sums]
    dma = pltpu.SemaphoreType.DMA
    return _stage(pair_sums, outs, {}, [dma((n_t, 3)), dma((n_t, 3)), dma((n_t,))], build)


def _pair_sum(name, whole, kind, got):
    _, rows, cols = got.shape
    tr = _tile(rows, 256)
    n_r = rows // tr
    core = lax.axis_index("c").astype(jnp.int32).reshape(1)

    def body(_, a, b, o):
        o[...] = (a[...].astype(F32) + b[...].astype(F32)).astype(BF16)

    if kind == "col":
        own = pl.BlockSpec((tr, cols), lambda q, i, c_ref: (i, 2 * q + c_ref[0]))
    elif kind == "row":
        own = pl.BlockSpec((tr, cols), lambda q, i, c_ref: ((2 * q + c_ref[0]) * n_r + i, 0))
    else:
        own = pl.BlockSpec((None, tr, cols), lambda q, i, c_ref: (2 * q + c_ref[0], i, 0))
    slab = pl.BlockSpec((None, tr, cols), lambda q, i, c_ref: (q, i, 0))
    return pl.pallas_call(
        body,
        name=name,
        grid_spec=pltpu.PrefetchScalarGridSpec(
            num_scalar_prefetch=1, grid=(4, n_r), in_specs=[own, slab], out_specs=slab
        ),
        out_shape=SDS(got.shape, BF16),
        compiler_params=_params(("parallel", "parallel")),
    )(core, whole, got)


def _packed_rows(parts, c_):
    offs, r0 = [], 0
    for p in parts:
        offs.append(r0)
        r0 += p.shape[1] * (p.shape[2] // c_)
    return offs, -(-r0 // 8) * 8


def _all_reduce_small(parts, c_):
    n_p = len(parts)
    offs, r_ = _packed_rows(parts, c_)

    def body(*refs):
        p_refs = refs[:n_p]
        land, total, src, send_sems, recv_sems = refs[n_p:]
        me = _place()
        my = _dev_index(me)
        src[...] = jnp.zeros((r_, c_), F32)
        for p_ref, r0 in zip(p_refs, offs):
            v = jnp.sum(p_ref[...], axis=0)
            k = v.shape[1] // c_
            for ri in range(v.shape[0]):
                for q in range(k):
                    src[r0 + ri * k + q : r0 + ri * k + q + 1, :] = v[ri : ri + 1, q * c_ : (q + 1) * c_]
        land[my] = src[...]

        def copy(r):
            peer = _peer(me, r)
            return pltpu.make_async_remote_copy(
                src_ref=src,
                dst_ref=land.at[my],
                send_sem=send_sems.at[r - 1],
                recv_sem=recv_sems.at[r - 1],
                device_id=peer,
                device_id_type=MESH,
            )

        def arrival(r):
            peer = _peer(me, r)
            slab = land.at[_dev_index(peer)]
            return pltpu.make_async_remote_copy(
                src_ref=slab,
                dst_ref=slab,
                send_sem=send_sems.at[r - 1],
                recv_sem=recv_sems.at[r - 1],
                device_id=peer,
                device_id_type=MESH,
            )

        sends = [copy(r) for r in range(1, NDEV)]
        for cp in sends:
            cp.start()
        for r in range(1, NDEV):
            arrival(r).wait_recv()
        for cp in sends:
            cp.wait_send()
        acc = land[0]
        for d in range(1, NDEV):
            acc = acc + land[d]
        total[...] = acc

    vmem = pl.BlockSpec(memory_space=pltpu.VMEM)
    return pl.pallas_call(
        body,
        name="all_reduce_small",
        in_specs=[vmem] * n_p,
        out_specs=[vmem, vmem],
        out_shape=[SDS((NDEV, r_, c_), F32), SDS((r_, c_), F32)],
        scratch_shapes=[pltpu.VMEM((r_, c_), F32), pltpu.SemaphoreType.DMA((7,)), pltpu.SemaphoreType.DMA((7,))],
        compiler_params=_params(),
    )(*parts)[1]


def _adamw_math(g, w, m, v):
    m2 = ADAM_B1 * m + (1.0 - ADAM_B1) * g
    v2 = ADAM_B2 * v + (1.0 - ADAM_B2) * (g * g)
    m_hat = m2 / (1.0 - ADAM_B1**ADAM_STEP)
    v_hat = v2 / (1.0 - ADAM_B2**ADAM_STEP)
    delta = -ADAM_LR * (m_hat / (jnp.sqrt(v_hat) + ADAM_EPS) + ADAM_WD * w)
    return delta, m2, v2


def _adamw_big(name, land, w, m, v):
    rows, cols = w.shape
    tr = _tile(rows, 256)
    n_slab = land.shape[0]

    def body(l_ref, w_ref, m_ref, v_ref, g_o, d_o, m_o, v_o):
        g = l_ref[0].astype(F32)
        for d in range(1, n_slab):
            g = g + l_ref[d].astype(F32)
        delta, m2, v2 = _adamw_math(g, w_ref[...], m_ref[...], v_ref[...])
        g_o[...] = g
        d_o[...] = delta
        m_o[...] = m2
        v_o[...] = v2

    blk = pl.BlockSpec((tr, cols), lambda i: (i, 0))
    return pl.pallas_call(
        body,
        name=name,
        grid=(rows // tr,),
        in_specs=[pl.BlockSpec((n_slab, tr, cols), lambda i: (0, i, 0)), blk, blk, blk],
        out_specs=[blk] * 4,
        out_shape=[SDS((rows, cols), F32)] * 4,
        compiler_params=_params(("parallel",)),
    )(land, w, m, v)


def _adamw_small(total, items):
    c_ = total.shape[1]
    n_it = len(items)

    def body(*refs):
        t_ref = refs[0]
        ins, outs = refs[1 : 1 + 3 * n_it], refs[1 + 3 * n_it :]
        my = _dev_index(_place())
        for q, (row0, taps, w, _, _) in enumerate(items):
            w_ref, m_ref, v_ref = ins[3 * q : 3 * q + 3]
            if taps:
                lanes = w.shape[-1]
                g = t_ref[pl.ds(row0, taps), pl.ds(pl.multiple_of(my * lanes, 128), lanes)][None]
            else:
                k = w.shape[-1] // c_
                g = jnp.concatenate([t_ref[row0 + j : row0 + j + 1, :] for j in range(k)], axis=1)
            delta, m2, v2 = _adamw_math(g, w_ref[...], m_ref[...], v_ref[...])
            for o_ref, val in zip(outs[4 * q : 4 * q + 4], (g, delta, m2, v2)):
                o_ref[...] = val

    vmem = pl.BlockSpec(memory_space=pltpu.VMEM)
    flat = [a for (_, _, w, m, v) in items for a in (w, m, v)]
    res = pl.pallas_call(
        body,
        name="adamw_small",
        in_specs=[vmem] * (1 + 3 * n_it),
        out_specs=[vmem] * (4 * n_it),
        out_shape=[SDS(w.shape, F32) for (_, _, w, _, _) in items for _ in range(4)],
    )(total, *flat)
    return [list(res[4 * q : 4 * q + 4]) for q in range(n_it)]


def kernel(x, p, g_mix, w_in, conv_a_w, w_out_a, b_glu, conf_dw_w, conf_dw_b, conf_ln_g, conf_ln_b, w_pw_b, b_pw_b, w_o, g_ffn, w_gate, w_up, w_down, g_ple, w_ple_gate, w_ple_proj, g_final, loss_target, m_g_mix, m_w_in, m_conv_a_w, m_w_out_a, m_b_glu, m_conf_dw_w, m_conf_dw_b, m_conf_ln_g, m_conf_ln_b, m_w_pw_b, m_b_pw_b, m_w_o, m_g_ffn, m_w_gate, m_w_up, m_w_down, m_g_ple, m_w_ple_gate, m_w_ple_proj, m_g_final, v_g_mix, v_w_in, v_conv_a_w, v_w_out_a, v_b_glu, v_conf_dw_w, v_conf_dw_b, v_conf_ln_g, v_conf_ln_b, v_w_pw_b, v_b_pw_b, v_w_o, v_g_ffn, v_w_gate, v_w_up, v_w_down, v_g_ple, v_w_ple_gate, v_w_ple_proj, v_g_final):
    s, d = x.shape[1], x.shape[2]
    c = conf_ln_g.shape[-1]
    pdim = w_ple_proj.shape[1]
    fs = w_gate.shape[-1]
    nin = NDEV * w_in.shape[-1]
    assert d == 2 * c and nin == 5 * c + 2 * d, (d, c, nin)
    x2, p2, tgt = x[0], p[0, 0], loss_target[0]
    gfin = g_final.reshape(1, d)

    kpa, kpb = 8, HALO_B
    wa_sh = jnp.pad(conv_a_w[0], ((0, kpa - CONV_A_K), (0, 0)))
    wd_sh = jnp.pad(conf_dw_w[0], ((0, kpb - CONF_K), (0, 0)))
    kind_of = dict(w_in="col", w_out_a="col", w_pw_b="col", w_ple_proj="col", w_o="row", w_ple_gate="row",
                   w_gate="blk", w_up="blk", w_down="blk")
    weight = dict(w_in=w_in, w_out_a=w_out_a, w_pw_b=w_pw_b, w_ple_proj=w_ple_proj, w_o=w_o, w_ple_gate=w_ple_gate,
                  w_gate=w_gate, w_up=w_up, w_down=w_down)
    shard_of = {nm: tuple(w.shape[1:]) for nm, w in weight.items()}
    bf16_shard = lambda nm: weight[nm][0].astype(BF16)
    kinds_ = lambda grp: [kind_of[nm] for nm in grp]
    shapes_ = lambda grp: [shard_of[nm] for nm in grp]
    first_stage = lambda grp: _ag1([bf16_shard(nm) for nm in grp], kinds_(grp))
    second_stage = lambda grp, parts: _ag2(parts, kinds_(grp), shapes_(grp))
    grp_1 = ["w_out_a", "w_pw_b"]
    grp_2 = ["w_o", "w_gate"]
    grp_3 = ["w_up"]
    grp_4 = ["w_down"]
    grp_5 = ["w_ple_gate", "w_ple_proj"]

    tm = _tile(s, 1024)
    tn = _tile(d, 1024)
    assert (5 * c) % tn == 0 and d % tn == 0 and c % tn == 0
    ga_blk, gb_blk = (5 * c) // tn, (5 * c + d) // tn
    ij = lambda i, j, k: (i, j)
    row_i = lambda i, j, k: (i, 0)

    n1 = _rms_fwd("rms1", x2, g_mix)
    proj, win, got = _proj_gather(
        n1, bf16_shard("w_in"),
        _join(_ag_direct([wa_sh, wd_sh], ["col", "col"]), first_stage(grp_1)), first_stage(grp_2),
    )
    (wa, wd), part_12 = got[:2], got[2:]
    grp_12 = grp_1 + grp_2
    ya_in = _mix_a_fwd(proj, wa, s, c)
    (v_act, u_act, cv), got = _mix_b_fwd(
        proj, b_glu, wd, conf_dw_b, conf_ln_g, conf_ln_b, s, c,
        comm=_join(second_stage(grp_12, part_12), first_stage(grp_3)),
    )
    (wouta, wpw, wo, wg), part_3 = got[: len(grp_12)], got[len(grp_12) :]

    def ep_merge(accs, ex, os_):
        sa = _sigmoid(ex[0][...].astype(F32))
        sb = _sigmoid(ex[1][...].astype(F32))
        ya = accs[0]
        yb = accs[1] + ex[2][...]
        os_[0][...] = (sa * ya + sb * yb).astype(BF16)
        os_[1][...] = ya.astype(BF16)
        os_[2][...] = yb.astype(BF16)

    gate_a_spec = pl.BlockSpec((tm, tn), lambda i, j, k: (i, ga_blk + j))
    gate_b_spec = pl.BlockSpec((tm, tn), lambda i, j, k: (i, gb_blk + j))
    out_sd = (SDS((s, d), BF16), pl.BlockSpec((tm, tn), ij))
    (m_act, ya, yb), got = _fmm(
        "merge", (s // tm, d // tn, 1),
        [(ya_in, pl.BlockSpec((tm, c), row_i)), (wouta, pl.BlockSpec((c, tn), lambda i, j, k: (0, j))),
         (v_act, pl.BlockSpec((tm, c), row_i)), (wpw, pl.BlockSpec((c, tn), lambda i, j, k: (0, j)))],
        [(0, 1, NN, 0, None), (2, 3, NN, 1, None)], [(tm, tn), (tm, tn)],
        [(proj, gate_a_spec), (proj, gate_b_spec), (b_pw_b, pl.BlockSpec((1, tn), lambda i, j, k: (0, j)))],
        [out_sd, out_sd, out_sd], ep_merge, csplit=EPILOGUE_CHUNK,
        comm=_join(second_stage(grp_3, part_3), first_stage(grp_4)),
    )
    (wu,), part_4 = got[: len(grp_3)], got[len(grp_3) :]

    def ep_residual(accs, ex, os_):
        os_[0][...] = accs[0] + ex[0][...]

    (h1,), got = _fmm(
        "w_o", (s // tm, d // tn, 1),
        [(m_act, pl.BlockSpec((tm, d), row_i)), (wo, pl.BlockSpec((d, tn), lambda i, j, k: (0, j)))],
        [(0, 1, NN, 0, None)], [(tm, tn)], [(x2, pl.BlockSpec((tm, tn), ij))],
        [(SDS((s, d), F32), pl.BlockSpec((tm, tn), ij))], ep_residual, csplit=EPILOGUE_CHUNK,
        comm=_join(second_stage(grp_4, part_4), first_stage(grp_5)),
    )
    (wdn,), part_5 = got[: len(grp_4)], got[len(grp_4) :]
    n2 = _rms_fwd("rms2", h1, g_ffn)

    hidden = NDEV * fs
    wg_p = _slabs_to_plain("w_gate_plain", wg)
    wu_p = _slabs_to_plain("w_up_plain", wu)
    wdn_p = wdn.reshape(hidden, d)
    tf = _tile(hidden, 512)
    tkf = _tile(hidden, 2816)

    def ep_gateup(accs, ex, os_):
        g, u = accs
        sg = _sigmoid(g)
        silu = g * sg
        os_[0][...] = (u * sg * (1.0 + g * (1.0 - sg))).astype(BF16)
        os_[1][...] = silu.astype(BF16)
        os_[2][...] = (silu * u).astype(BF16)

    ff_sd = (SDS((s, hidden), BF16), pl.BlockSpec((tm, tf), ij))
    w_col_blk = pl.BlockSpec((d, tf), lambda i, j, k: (0, j))
    (df_dg, df_du, f_act), (wpg, wpp) = _fmm(
        "gate_up", (s // tm, hidden // tf, 1),
        [(n2, pl.BlockSpec((tm, d), row_i)), (wg_p, w_col_blk), (wu_p, w_col_blk)],
        [(0, 1, NN, 0, None), (0, 2, NN, 1, None)], [(tm, tf), (tm, tf)], [],
        [ff_sd, ff_sd, ff_sd], ep_gateup,
        comm=second_stage(grp_5, part_5),
    )
    (h2,), _ = _fmm(
        "down", (s // tm, d // tn, hidden // tkf),
        [(f_act, pl.BlockSpec((tm, tkf), lambda i, j, k: (i, k))),
         (wdn_p, pl.BlockSpec((tkf, tn), lambda i, j, k: (k, j)))],
        [(0, 1, NN, 0, None)], [(tm, tn)], [(h1, pl.BlockSpec((tm, tn), ij))],
        [(SDS((s, d), F32), pl.BlockSpec((tm, tn), ij))], ep_residual,
    )
    n3 = _rms_fwd("rms3", h2, g_ple)

    tr = _tile(s, 256)
    n_r = s // tr
    rows = lambda i, j, k: (i, 0)
    whole = lambda i, j, k: (0, 0)
    part_spec = lambda nrow: pl.BlockSpec((None, nrow, d), lambda i, j, k: (i, 0, 0))

    def ep_ple(accs, ex, os_):
        h2_, t_, gf = ex[0][...], ex[1][...], ex[2][...]
        ple = accs[0]
        s3 = _sigmoid(accs[1])
        h3 = h2_ + s3 * ple
        r = lax.rsqrt(jnp.mean(h3 * h3, axis=-1, keepdims=True) + EPS)
        hn = h3 * r
        e = hn * gf - t_
        loss = 0.5 * jnp.sum(jnp.mean(e * e, axis=-1, keepdims=True), axis=0, keepdims=True)
        dy = e * (1.0 / d)
        dn = dy * gf
        dh3 = r * (dn - hn * jnp.mean(dn * hn, axis=-1, keepdims=True))
        os_[0][...] = dh3.astype(BF16)
        os_[1][...] = (dh3 * s3).astype(BF16)
        os_[2][...] = (dh3 * ple * s3 * (1.0 - s3)).astype(BF16)
        os_[3][0:1, :] = jnp.sum(dy * hn, axis=0, keepdims=True)
        os_[3][1:2, :] = jnp.broadcast_to(loss, (1, d))

    (dh3, d_ple, d_g3, part_fin), _ = _fmm(
        "ple_loss", (n_r, 1, 1),
        [(p2, pl.BlockSpec((tr, pdim), rows)), (wpp, pl.BlockSpec((pdim, d), whole)),
         (n3, pl.BlockSpec((tr, d), rows)), (wpg, pl.BlockSpec((d, d), whole))],
        [(0, 1, NN, 0, None), (2, 3, NN, 1, None)], [(tr, d), (tr, d)],
        [(h2, pl.BlockSpec((tr, d), rows)), (tgt, pl.BlockSpec((tr, d), rows)), (gfin, pl.BlockSpec((1, d), whole))],
        [(SDS((s, d), BF16), pl.BlockSpec((tr, d), rows)), (SDS((s, d), BF16), pl.BlockSpec((tr, d), rows)),
         (SDS((s, d), BF16), pl.BlockSpec((tr, d), rows)), (SDS((n_r, 2, d), F32), part_spec(2))],
        ep_ple,
    )

    g_wpp = _mm_tn("d_w_ple_proj", p2, d_ple)
    g_wpg = _mm_tn("d_w_ple_gate", n3, d_g3)

    def ep_norm_bwd(accs, ex, os_):
        dh, dg = _rms_bwd(accs[0], ex[0][...], ex[2][...])
        os_[0][...] = (ex[1][...].astype(F32) + dh).astype(BF16)
        os_[1][...] = dg

    norm_outs = lambda t: [(SDS((s, d), BF16), pl.BlockSpec((t, d), rows)), (SDS((s // t, 1, d), F32), part_spec(1))]
    def exchange1(names, wholes):
        return _rs1(wholes, kinds_(names), shapes_(names))

    def pair_sums(names, wholes, got):
        return [_pair_sum("pair_sum_" + nm, wholes[t], kind_of[nm], got[t]) for t, nm in enumerate(names)]

    lands = {}
    grp1 = ["w_ple_proj", "w_ple_gate"]
    (dh2b, part_ple), got = _fmm(
        "d_n3", (n_r, 1, 1),
        [(d_g3, pl.BlockSpec((tr, d), rows)), (wpg, pl.BlockSpec((d, d), whole))],
        [(0, 1, NT, 0, None)], [(tr, d)],
        [(h2, pl.BlockSpec((tr, d), rows)), (dh3, pl.BlockSpec((tr, d), rows)), (g_ple, pl.BlockSpec((1, d), whole))],
        norm_outs(tr), ep_norm_bwd,
        comm=exchange1(grp1, [g_wpp, g_wpg]),
    )
    sums1 = pair_sums(grp1, [g_wpp, g_wpg], got)

    def ep_ddown(accs, ex, os_):
        df = accs[0]
        os_[0][...] = (df * ex[0][...].astype(F32)).astype(BF16)
        os_[1][...] = (df * ex[1][...].astype(F32)).astype(BF16)

    ff_in = pl.BlockSpec((tm, tf), ij)
    (d_g, d_u), got = _fmm(
        "d_down", (s // tm, hidden // tf, 1),
        [(dh2b, pl.BlockSpec((tm, d), row_i)), (wdn_p, pl.BlockSpec((tf, d), lambda i, j, k: (j, 0)))],
        [(0, 1, NT, 0, None)], [(tm, tf)], [(df_dg, ff_in), (df_du, ff_in)],
        [ff_sd, ff_sd], ep_ddown, csplit=EPILOGUE_CHUNK,
        comm=_rs2(sums1),
    )
    lands.update(zip(grp1, got))
    tk = _tile(s, 1024)
    g_wdn = _mm_tn("d_w_down", f_act, dh2b, tm=1408, tn=2048).reshape(NDEV, fs, d)

    def ep_two_bf16(accs, ex, os_):
        os_[0][...] = accs[0].astype(BF16)
        os_[1][...] = accs[1].astype(BF16)

    ff_k = pl.BlockSpec((tk, tf), lambda i, j, k: (k, j))
    wcol_sd = (SDS((d, hidden), BF16), pl.BlockSpec((d, tf), lambda i, j, k: (0, j)))
    grp2 = ["w_down"]
    (g_wg_p, g_wu_p), got = _fmm(
        "d_w_gate_up", (1, hidden // tf, s // tk),
        [(n2, pl.BlockSpec((tk, d), lambda i, j, k: (k, 0))), (d_g, ff_k), (d_u, ff_k)],
        [(0, 1, TN, 0, None), (0, 2, TN, 1, None)], [(d, tf), (d, tf)], [],
        [wcol_sd, wcol_sd], ep_two_bf16,
        comm=exchange1(grp2, [g_wdn]),
    )
    g_wg = _plain_to_slabs("d_w_gate_slabs", g_wg_p, NDEV)
    g_wu = _plain_to_slabs("d_w_up_slabs", g_wu_p, NDEV)
    sums2 = pair_sums(grp2, [g_wdn], got)
    grp3 = ["w_gate", "w_up"]
    th = _tile(s // 2, 1024)
    ff_a = pl.BlockSpec((th, tf), lambda i, j, k: (i, k))
    w_k = pl.BlockSpec((d, tf), lambda i, j, k: (0, k))
    (d_n2,), got = _fmm(
        "d_n2", (s // th, 1, hidden // tf),
        [(d_g, ff_a), (wg_p, w_k), (d_u, ff_a), (wu_p, w_k)],
        [(0, 1, NT, 0, None), (2, 3, NT, 0, None)], [(th, d)], [],
        [(SDS((s, d), BF16), pl.BlockSpec((th, d), rows))], _ep_bf16,
        comm=_join(_rs2(sums2), exchange1(grp3, [g_wg, g_wu])),
    )
    lands.update(zip(grp2, got[:1]))
    sums3 = pair_sums(grp3, [g_wg, g_wu], got[1:])
    dh1b, part_ffn = _norm_bwd("d_h1", d_n2, h1, dh2b, g_ffn, BF16)
    g_wo = _mm_tn("d_w_o", m_act, dh1b)

    def ep_dm(accs, ex, os_):
        ya_, yb_ = ex[0][...].astype(F32), ex[1][...].astype(F32)
        sa = _sigmoid(ex[2][...].astype(F32))
        sb = _sigmoid(ex[3][...].astype(F32))
        dm = accs[0]
        d_yb = dm * sb
        os_[0][...] = (dm * sa).astype(BF16)
        os_[1][...] = d_yb.astype(BF16)
        os_[2][...] = (dm * ya_ * sa * (1.0 - sa)).astype(BF16)
        os_[3][...] = (dm * yb_ * sb * (1.0 - sb)).astype(BF16)
        os_[4][...] = jnp.sum(d_yb, axis=0, keepdims=True)

    tile_ij = pl.BlockSpec((tm, tn), ij)
    grp4 = ["w_o"]
    (d_ya, d_yb, d_ga, d_gb, part_bpw), got = _fmm(
        "d_merge", (s // tm, d // tn, 1),
        [(dh1b, pl.BlockSpec((tm, d), row_i)), (wo, pl.BlockSpec((tn, d), lambda i, j, k: (j, 0)))],
        [(0, 1, NT, 0, None)], [(tm, tn)],
        [(ya, tile_ij), (yb, tile_ij), (proj, gate_a_spec), (proj, gate_b_spec)],
        [out_sd, out_sd, out_sd, out_sd,
         (SDS((s // tm, 1, d), F32), pl.BlockSpec((None, 1, tn), lambda i, j, k: (i, 0, j)))],
        ep_dm, csplit=EPILOGUE_CHUNK,
        comm=exchange1(grp4, [g_wo]),
    )
    sums4 = pair_sums(grp4, [g_wo], got)
    g_wouta = _mm_tn("d_w_out_a", ya_in, d_ya)
    g_wpw = _mm_tn("d_w_pw_b", v_act, d_yb)
    grp5 = ["w_out_a", "w_pw_b"]
    d_ya_in, got = _mm_nt("d_ya_in", d_ya, wouta, comm=exchange1(grp5, [g_wouta, g_wpw]))
    sums5 = pair_sums(grp5, [g_wouta, g_wpw], got)
    d_v, _ = _mm_nt("d_v", d_yb, wpw)
    d_cv, part_ln = _mix_b_bwd1(d_v, cv, conf_ln_g, conf_ln_b, s, c)
    (d_b, part_wd, part_bglu), got = _mix_b_bwd2(d_cv, u_act, proj, b_glu, wd, s, c, comm=_rs2(sums3))
    lands.update(zip(grp3, got))
    d_a, part_wa = _mix_a_bwd(d_ya_in, proj, wa, s, c)

    nb = nin // c
    gblk = d // c
    lo = [0, 3, 5, 5 + gblk]
    hi = [3, 5, 5 + gblk, 5 + 2 * gblk]
    pieces = [d_a, d_b, d_ga, d_gb]

    def active(q, ax):
        return lambda ids: jnp.logical_and(ids[ax] >= lo[q], ids[ax] < hi[q])

    def piece_spec(q, rows_, ax, row0=0):
        def index(i, j, k):
            ids = (i, j, k)
            col = jnp.clip(ids[ax] - lo[q], 0, hi[q] - lo[q] - 1)
            row = i + row0 if ax == 2 else jnp.where(active(q, ax)(ids), k, 0)
            return (row, col)

        return pl.BlockSpec((rows_, c), index)

    tkw = _tile(s, 1024)
    (g_win,), got = _fmm(
        "d_w_in", (1, nb, s // tkw),
        [(n1, pl.BlockSpec((tkw, d), lambda i, j, k: (k, 0)))]
        + [(pieces[q], piece_spec(q, tkw, 1)) for q in range(4)],
        [(0, 1 + q, TN, 0, active(q, 1)) for q in range(4)], [(d, c)], [],
        [(SDS((d, nin), BF16), pl.BlockSpec((d, c), lambda i, j, k: (0, j)))], _ep_bf16,
        comm=_rs2(sums4 + sums5),
    )
    lands.update(zip(grp4 + grp5, got))

    grp6 = ["w_in"]
    n_half = max(1, 3 * (s // th) // 8)

    def d_n1_rows(name, row0, n_tiles, comm, into):
        return _fmm(
            name, (n_tiles, 1, nb),
            [(pieces[q], piece_spec(q, th, 2, row0)) for q in range(4)]
            + [(win, pl.BlockSpec((d, c), lambda i, j, k: (0, k)))],
            [(q, 4, NT, 0, active(q, 2)) for q in range(4)], [(th, d)], [],
            [(SDS((s, d), BF16), pl.BlockSpec((th, d), lambda i, j, k: (i + row0, 0)))], _ep_bf16,
            comm=comm, into=into,
        )

    (d_n1,), got = d_n1_rows("d_n1_a", 0, n_half, exchange1(grp6, [g_win]), None)
    (d_n1,), got = d_n1_rows("d_n1_b", n_half, s // th - n_half, _rs2(pair_sums(grp6, [g_win], got)), d_n1)
    lands.update(zip(grp6, got))
    dx, part_mix = _norm_bwd("d_x", d_n1, x2, dh1b, g_mix, F32)

    small_parts = [part_mix, part_bglu, part_ln, part_bpw, part_ffn, part_ple, part_fin, part_wa, part_wd]
    (o_mix, o_bglu, o_ln, o_bpw, o_ffn, o_ple, o_fin, o_wa, o_wd), _ = _packed_rows(small_parts, c)
    total = _all_reduce_small(small_parts, c)
    loss = total[o_fin + d // c, 0]

    big_m = dict(w_in=m_w_in, w_out_a=m_w_out_a, w_pw_b=m_w_pw_b, w_ple_proj=m_w_ple_proj, w_o=m_w_o,
                 w_ple_gate=m_w_ple_gate, w_gate=m_w_gate, w_up=m_w_up, w_down=m_w_down)
    big_v = dict(w_in=v_w_in, w_out_a=v_w_out_a, w_pw_b=v_w_pw_b, w_ple_proj=v_w_ple_proj, w_o=v_w_o,
                 w_ple_gate=v_w_ple_gate, w_gate=v_w_gate, w_up=v_w_up, w_down=v_w_down)
    big_out = {}
    for nm in weight:
        res = _adamw_big("adamw_" + nm, lands[nm], weight[nm][0], big_m[nm][0], big_v[nm][0])
        big_out[nm] = [r[None] for r in res]

    row = lambda a: a.reshape(1, d)
    small = dict(
        g_mix=(o_mix, 0, g_mix, m_g_mix, v_g_mix),
        conv_a_w=(o_wa, CONV_A_K, conv_a_w, m_conv_a_w, v_conv_a_w),
        b_glu=(o_bglu, 0, b_glu, m_b_glu, v_b_glu),
        conf_dw_w=(o_wd, CONF_K, conf_dw_w, m_conf_dw_w, v_conf_dw_w),
        conf_dw_b=(o_ln + 2, 0, conf_dw_b, m_conf_dw_b, v_conf_dw_b),
        conf_ln_g=(o_ln, 0, conf_ln_g, m_conf_ln_g, v_conf_ln_g),
        conf_ln_b=(o_ln + 1, 0, conf_ln_b, m_conf_ln_b, v_conf_ln_b),
        b_pw_b=(o_bpw, 0, b_pw_b, m_b_pw_b, v_b_pw_b),
        g_ffn=(o_ffn, 0, g_ffn, m_g_ffn, v_g_ffn),
        g_ple=(o_ple, 0, g_ple, m_g_ple, v_g_ple),
        g_final=(o_fin, 0, row(g_final), row(m_g_final), row(v_g_final)),
    )
    small_out = dict(zip(small, _adamw_small(total, list(small.values()))))
    small_out["g_final"] = [a.reshape(d) for a in small_out["g_final"]]

    order = ["g_mix", "w_in", "conv_a_w", "w_out_a", "b_glu", "conf_dw_w", "conf_dw_b", "conf_ln_g", "conf_ln_b", "w_pw_b", "b_pw_b", "w_o", "g_ffn", "w_gate", "w_up", "w_down", "g_ple", "w_ple_gate", "w_ple_proj", "g_final"]
    allo = {**big_out, **small_out}
    outs = [loss, dx[None]]
    for q in range(4):
        outs += [allo[nm][q] for nm in order]
    return tuple(outs)
```

```python
import jax
import jax.numpy as jnp
from jax import lax
from jax.experimental import pallas as pl
from jax.experimental.pallas import tpu as pltpu

F32, BF16 = jnp.float32, jnp.bfloat16
EPS, LN_EPS = 1e-6, 1e-5
ADAM_LR, ADAM_B1, ADAM_B2, ADAM_EPS, ADAM_WD, ADAM_STEP = 0.001, 0.9, 0.999, 1e-08, 0.01, 10
CONV_A_K, CONF_K = 3, 31
NDEV = 8
NN = (((1,), (0,)), ((), ()))
NT = (((1,), (1,)), ((), ()))
TN = (((0,), (0,)), ((), ()))
V7X_VMEM_LIMIT_BYTES = 56 * 1024 * 1024
MESH = pl.DeviceIdType.MESH
SDS = jax.ShapeDtypeStruct
HALO_A, HALO_B = 16, 32
EPILOGUE_CHUNK = 256
CONV_ROWS = 32
CONF_ROWS = 16


def _tile(n, pref):
    t = min(n, pref)
    while n % t:
        t -= 8
    return t


def _sigmoid(x):
    return jax.nn.sigmoid(x)


def _params(sem=None):
    return pltpu.CompilerParams(vmem_limit_bytes=V7X_VMEM_LIMIT_BYTES, dimension_semantics=sem)


def _edge(grid, last):
    cond = None
    for ax, n in enumerate(grid):
        here = pl.program_id(ax) == (n - 1 if last else 0)
        cond = here if cond is None else jnp.logical_and(cond, here)
    return cond


def _join(*comms):
    ins, outs, alias, sems, spans = [], [], {}, [], []
    for cm in comms:
        spans.append((len(ins), len(outs), len(sems)))
        for i, o in cm["alias"].items():
            alias[len(ins) + i] = len(outs) + o
        ins += cm["ins"]
        outs += cm["outs"]
        sems += cm["sems"]

    def run(which):
        def f(i_refs, o_refs, s_refs):
            for cm, (a, b, c_) in zip(comms, spans):
                cm[which](
                    i_refs[a : a + len(cm["ins"])], o_refs[b : b + len(cm["outs"])], s_refs[c_ : c_ + len(cm["sems"])]
                )

        return f

    return dict(ins=ins, outs=outs, alias=alias, sems=sems, start=run("start"), finish=run("finish"))


def _call(body, name, grid, in_specs, args, out_specs, out_shape, scratch=(), sem=None, comm=None, alias=None):
    n_in, n_out, n_s = len(args), len(out_shape), len(scratch)
    alias = dict(alias or {})
    if comm is None:
        res = pl.pallas_call(
            body, name=name, grid=grid, in_specs=list(in_specs), out_specs=list(out_specs), out_shape=list(out_shape),
            scratch_shapes=list(scratch), input_output_aliases=alias, compiler_params=_params(sem),
        )(*args)
        return list(res), []
    n_ci, n_co = len(comm["ins"]), len(comm["outs"])

    def wrapped(*refs):
        ins = refs[:n_in]
        ci = refs[n_in : n_in + n_ci]
        o0 = n_in + n_ci
        outs = refs[o0 : o0 + n_out]
        co = refs[o0 + n_out : o0 + n_out + n_co]
        s0 = o0 + n_out + n_co
        sc = refs[s0 : s0 + n_s]
        cs = refs[s0 + n_s :]
        pl.when(_edge(grid, False))(lambda: comm["start"](ci, co, cs))
        body(*ins, *outs, *sc)
        pl.when(_edge(grid, True))(lambda: comm["finish"](ci, co, cs))

    hbm = pl.BlockSpec(memory_space=pl.ANY)
    res = pl.pallas_call(
        wrapped,
        name=name,
        grid=grid,
        in_specs=list(in_specs) + [hbm] * n_ci,
        out_specs=list(out_specs) + [hbm] * n_co,
        out_shape=list(out_shape) + list(comm["outs"]),
        scratch_shapes=list(scratch) + list(comm["sems"]),
        input_output_aliases={**alias, **{n_in + i: n_out + o for i, o in comm["alias"].items()}},
        compiler_params=_params(("arbitrary",) * len(grid)),
    )(*args, *comm["ins"])
    return list(res[:n_out]), list(res[n_out:])


def _col_chunks(n, pref):
    out, c0 = [], 0
    while c0 < n:
        w = min(pref, n - c0)
        out.append((c0, w))
        c0 += w
    return out


def _fmm(name, grid, operands, terms, acc_shapes, extras, outs, epilogue, comm=None, csplit=None, into=None):
    n_p, n_e, n_o, n_a = len(operands), len(extras), len(outs), len(acc_shapes)
    nk = grid[-1]
    kax = len(grid) - 1
    simple = nk == 1 and all(t[4] is None for t in terms)
    alias = None
    if into is not None:
        extras = list(extras) + [(into, pl.BlockSpec(memory_space=pl.ANY))]
        alias = {n_p + n_e: 0}
        n_e += 1
    if csplit is not None:
        assert simple and into is None and all(t[2] in (NN, NT) and (len(t) <= 5 or not t[5]) for t in terms)
        tn_ = acc_shapes[0][1]
        chunks = _col_chunks(tn_, csplit)

    def dot(a, b, dims):
        if a.dtype != BF16:
            a = a.astype(BF16)
        if b.dtype != BF16:
            b = b.astype(BF16)
        return lax.dot_general(a, b, dims, preferred_element_type=F32)

    def value(refs, term):
        slabs = term[5] if len(term) > 5 else 0
        if not slabs:
            return dot(refs[term[0]][...], refs[term[1]][...], term[2])
        tot = None
        for sl in range(slabs):
            d = dot(refs[term[0]][sl], refs[term[1]][sl], term[2])
            tot = d if tot is None else tot + d
        return tot

    def always(refs):
        parts = [None] * n_a
        for term in terms:
            if term[4] is None:
                d = value(refs, term)
                parts[term[3]] = d if parts[term[3]] is None else parts[term[3]] + d
        return parts

    def chunked(refs, ex, os_, accs):
        cols = lambda ref, c0, w: ref.at[:, pl.ds(c0, w)] if ref.shape[-1] == tn_ else ref

        def dots(k):
            c0, w = chunks[k]
            parts = [None] * n_a
            for term in terms:
                b_ref = refs[term[1]]
                b = b_ref[:, pl.ds(c0, w)] if term[2] == NN else b_ref[pl.ds(c0, w), :]
                d = dot(refs[term[0]][...], b, term[2])
                parts[term[3]] = d if parts[term[3]] is None else parts[term[3]] + d
            for ai in range(n_a):
                accs[ai][k % 2, :, pl.ds(0, w)] = parts[ai]

        def finish(k):
            c0, w = chunks[k]
            vals = [accs[ai][k % 2, :, pl.ds(0, w)] for ai in range(n_a)]
            epilogue(vals, [cols(e, c0, w) for e in ex], [cols(o, c0, w) for o in os_])

        dots(0)
        for k in range(1, len(chunks)):
            dots(k)
            finish(k - 1)
        finish(len(chunks) - 1)

    def body(*refs):
        ex = refs[n_p : n_p + n_e]
        os_ = refs[n_p + n_e : n_p + n_e + n_o]
        accs = refs[n_p + n_e + n_o :]
        if simple and csplit is not None:
            chunked(refs, ex, os_, accs)
            return
        if simple:
            epilogue(always(refs), ex, os_)
            return
        ids = [pl.program_id(ax) for ax in range(len(grid))]
        k = ids[kax]

        @pl.when(k == 0)
        def _():
            for acc in accs:
                acc[...] = jnp.zeros(acc.shape, F32)

        for ai, part in enumerate(always(refs)):
            if part is not None:
                accs[ai][...] += part
        for term in terms:
            if term[4] is not None:

                def add(term=term):
                    accs[term[3]][...] += value(refs, term)

                pl.when(term[4](ids))(add)

        @pl.when(k == nk - 1)
        def _():
            epilogue([acc[...] for acc in accs], ex, os_)

    return _call(
        body,
        name,
        grid,
        [o[1] for o in operands] + [e[1] for e in extras],
        [o[0] for o in operands] + [e[0] for e in extras],
        [o[1] for o in outs],
        [o[0] for o in outs],
        scratch=[pltpu.VMEM((2, s[0], csplit), F32) for s in acc_shapes] if csplit is not None
        else [] if simple else [pltpu.VMEM(s, F32) for s in acc_shapes],
        sem=("parallel",) * kax + ("arbitrary",),
        comm=comm,
        alias=alias,
    )


def _rms_bwd(dn_raw, h, g):
    r = lax.rsqrt(jnp.mean(h * h, axis=-1, keepdims=True) + EPS)
    hn = h * r
    dg = jnp.sum(dn_raw * hn, axis=0, keepdims=True)
    dn = dn_raw * g
    dh = r * (dn - hn * jnp.mean(dn * hn, axis=-1, keepdims=True))
    return dh, dg


def _rms_fwd(name, h, g):
    s, d = h.shape
    ts = _tile(s, 512)

    def body(h_ref, g_ref, o_ref):
        x = h_ref[...].astype(F32)
        r = lax.rsqrt(jnp.mean(x * x, axis=-1, keepdims=True) + EPS)
        o_ref[...] = (x * r * g_ref[...]).astype(BF16)

    return pl.pallas_call(
        body,
        name=name,
        grid=(s // ts,),
        in_specs=[pl.BlockSpec((ts, d), lambda i: (i, 0)), pl.BlockSpec((1, d), lambda i: (0, 0))],
        out_specs=pl.BlockSpec((ts, d), lambda i: (i, 0)),
        out_shape=SDS((s, d), BF16),
        compiler_params=_params(("parallel",)),
    )(h, g)


def _norm_bwd(name, dn, h, dres, g, out_dtype):
    s, d = h.shape
    ts = _tile(s, 512)

    def body(dn_r, h_r, dres_r, g_r, dh_o, part_o):
        dh, dg = _rms_bwd(dn_r[...].astype(F32), h_r[...].astype(F32), g_r[...])
        dh_o[...] = (dres_r[...].astype(F32) + dh).astype(out_dtype)
        part_o[...] = dg

    blk = pl.BlockSpec((ts, d), lambda i: (i, 0))
    part = pl.BlockSpec((None, 1, d), lambda i: (i, 0, 0))
    return pl.pallas_call(
        body,
        name=name,
        grid=(s // ts,),
        in_specs=[blk, blk, blk, pl.BlockSpec((1, d), lambda i: (0, 0))],
        out_specs=[blk, part],
        out_shape=[SDS((s, d), out_dtype), SDS((s // ts, 1, d), F32)],
        compiler_params=_params(("parallel",)),
    )(dn, h, dres, g)


def _slabs_to_plain(name, x):
    n, rows, w = x.shape
    tr = _tile(rows, 256)

    def body(x_ref, o_ref):
        for j in range(n):
            o_ref[:, j * w : (j + 1) * w] = x_ref[j]

    return pl.pallas_call(
        body,
        name=name,
        grid=(rows // tr,),
        in_specs=[pl.BlockSpec((n, tr, w), lambda i: (0, i, 0))],
        out_specs=pl.BlockSpec((tr, n * w), lambda i: (i, 0)),
        out_shape=SDS((rows, n * w), x.dtype),
        compiler_params=_params(("parallel",)),
    )(x)


def _plain_to_slabs(name, x, n):
    rows, nw = x.shape
    w = nw // n
    tr = _tile(rows, 256)

    def body(x_ref, o_ref):
        for j in range(n):
            o_ref[j] = x_ref[:, j * w : (j + 1) * w]

    return pl.pallas_call(
        body,
        name=name,
        grid=(rows // tr,),
        in_specs=[pl.BlockSpec((tr, nw), lambda i: (i, 0))],
        out_specs=pl.BlockSpec((n, tr, w), lambda i: (0, i, 0)),
        out_shape=SDS((n, rows, w), x.dtype),
        compiler_params=_params(("parallel",)),
    )(x)


def _prev_halo(ts, hb):
    r = ts // hb
    return lambda i: jnp.maximum(i * r - 1, 0)


def _next_halo(ts, hb, s):
    r = ts // hb
    last = s // hb - 1
    return lambda i: jnp.minimum((i + 1) * r, last)


def _shift_copies(buf, sh):
    n = sh.shape[1]
    for j in range(1, 8):
        sh[j - 1, pl.ds(0, n), :] = buf[pl.ds(j, n), :]


def _tap(buf, sh, r0, off, rows):
    j = off % 8
    start = pl.multiple_of(r0 + (off - j), 8)
    if j == 0:
        return buf[pl.ds(start, rows), :]
    return sh[j - 1, pl.ds(start, rows), :]


def _mix_a_fwd(proj, wa, s, c):
    ts, hb = _tile(s, 256), HALO_A
    prev = _prev_halo(ts, hb)

    def body(ah, ab, ac, hh, hc, w, o, buf):
        i = pl.program_id(0)
        zh = hc[...].astype(F32) * hh[...].astype(F32)
        buf[pl.ds(0, hb), :] = jnp.where(i == 0, 0.0, zh)
        buf[pl.ds(hb, ts), :] = ac[...].astype(F32) * ah[...].astype(F32)
        for r0 in range(0, ts, CONV_ROWS):
            cz = jnp.zeros((CONV_ROWS, c), F32)
            for k in range(CONV_A_K):
                cz = cz + w[k : k + 1, :] * buf[pl.ds(hb + r0 - (CONV_A_K - 1) + k, CONV_ROWS), :]
            o[pl.ds(r0, CONV_ROWS), :] = (ab[pl.ds(r0, CONV_ROWS), :].astype(F32) * cz).astype(BF16)

    main = lambda cb: pl.BlockSpec((ts, c), lambda i: (i, cb))
    halo = lambda cb: pl.BlockSpec((hb, c), lambda i: (prev(i), cb))
    return pl.pallas_call(
        body,
        name="mix_a_fwd",
        grid=(s // ts,),
        in_specs=[main(0), main(1), main(2), halo(0), halo(2), pl.BlockSpec(wa.shape, lambda i: (0, 0))],
        out_specs=pl.BlockSpec((ts, c), lambda i: (i, 0)),
        out_shape=SDS((s, c), BF16),
        scratch_shapes=[pltpu.VMEM((hb + ts, c), F32)],
        compiler_params=_params(("parallel",)),
    )(proj, proj, proj, proj, proj, wa)


def _mix_b_fwd(proj, b_glu, wd, bd, lg, lb, s, c, comm=None):
    ts, hb = _tile(s, 256), HALO_B
    prev = _prev_halo(ts, hb)

    def body(gv, gg, hv, hg, bglu, w, bd_r, lg_r, lb_r, v_o, u_o, cv_o, buf, sh):
        i = pl.program_id(0)
        bv, bg = bglu[:, 0:c], bglu[:, c : 2 * c]
        uh = (hv[...].astype(F32) + bv) * _sigmoid(hg[...].astype(F32) + bg)
        buf[pl.ds(0, hb), :] = jnp.where(i == 0, 0.0, uh)
        u = (gv[...].astype(F32) + bv) * _sigmoid(gg[...].astype(F32) + bg)
        buf[pl.ds(hb, ts), :] = u
        u_o[...] = u.astype(BF16)
        _shift_copies(buf, sh)

        def chunk(ci, carry):
            r0 = pl.multiple_of(ci * CONF_ROWS, CONF_ROWS)
            acc = jnp.zeros((CONF_ROWS, c), F32)
            for k in range(CONF_K):
                acc = acc + w[k : k + 1, :] * _tap(buf, sh, r0, hb - (CONF_K - 1) + k, CONF_ROWS)
            cv_o[pl.ds(r0, CONF_ROWS), :] = acc + bd_r[...]
            return carry

        lax.fori_loop(0, ts // CONF_ROWS, chunk, 0)
        cv = cv_o[...]
        mu = jnp.mean(cv, axis=-1, keepdims=True)
        xc = cv - mu
        rs = lax.rsqrt(jnp.mean(xc * xc, axis=-1, keepdims=True) + LN_EPS)
        ln = xc * rs * lg_r[...] + lb_r[...]
        v_o[...] = (ln * _sigmoid(ln)).astype(BF16)

    main = lambda cb: pl.BlockSpec((ts, c), lambda i: (i, cb))
    halo = lambda cb: pl.BlockSpec((hb, c), lambda i: (prev(i), cb))
    full = lambda a: pl.BlockSpec(a.shape, lambda i: (0, 0))
    out = pl.BlockSpec((ts, c), lambda i: (i, 0))
    return _call(
        body,
        "mix_b_fwd",
        (s // ts,),
        [main(3), main(4), halo(3), halo(4), full(b_glu), full(wd), full(bd), full(lg), full(lb)],
        [proj, proj, proj, proj, b_glu, wd, bd, lg, lb],
        [out, out, out],
        [SDS((s, c), BF16), SDS((s, c), BF16), SDS((s, c), F32)],
        scratch=[pltpu.VMEM((hb + ts, c), F32), pltpu.VMEM((7, hb + ts - 8, c), F32)],
        sem=("parallel",),
        comm=comm,
    )


def _mix_b_bwd1(d_v, cv, lg, lb, s, c):
    ts = _tile(s, 256)

    def body(dv_r, cv_r, lg_r, lb_r, dcv_o, part_o):
        cv_ = cv_r[...]
        mu = jnp.mean(cv_, axis=-1, keepdims=True)
        xc = cv_ - mu
        rs = lax.rsqrt(jnp.mean(xc * xc, axis=-1, keepdims=True) + LN_EPS)
        xh = xc * rs
        ln = xh * lg_r[...] + lb_r[...]
        sg = _sigmoid(ln)
        d_ln = dv_r[...].astype(F32) * (sg * (1.0 + ln * (1.0 - sg)))
        dy = d_ln * lg_r[...]
        d_cv = rs * (dy - jnp.mean(dy, axis=-1, keepdims=True) - xh * jnp.mean(dy * xh, axis=-1, keepdims=True))
        dcv_o[...] = d_cv
        part_o[0:1, :] = jnp.sum(d_ln * xh, axis=0, keepdims=True)
        part_o[1:2, :] = jnp.sum(d_ln, axis=0, keepdims=True)
        part_o[2:3, :] = jnp.sum(d_cv, axis=0, keepdims=True)

    blk = pl.BlockSpec((ts, c), lambda i: (i, 0))
    full = lambda a: pl.BlockSpec(a.shape, lambda i: (0, 0))
    return pl.pallas_call(
        body,
        name="mix_b_bwd_ln",
        grid=(s // ts,),
        in_specs=[blk, blk, full(lg), full(lb)],
        out_specs=[blk, pl.BlockSpec((None, 3, c), lambda i: (i, 0, 0))],
        out_shape=[SDS((s, c), F32), SDS((s // ts, 3, c), F32)],
        compiler_params=_params(("parallel",)),
    )(d_v, cv, lg, lb)


def _mix_b_bwd2(d_cv, u, proj, b_glu, wd, s, c, comm=None):
    ts, hb = _tile(s, 256), HALO_B
    prev, nxt = _prev_halo(ts, hb), _next_halo(ts, hb, s)
    n_t = s // ts
    kp = wd.shape[0]

    def body(dcv, dcv_n, u_m, u_p, gv, gg, bglu, w, d_o, dwd_o, dbglu_o, dbuf, ubuf, dub, dsh, ush, dwacc):
        i = pl.program_id(0)
        dbuf[pl.ds(0, ts), :] = dcv[...]
        dbuf[pl.ds(ts, hb), :] = jnp.where(i == n_t - 1, 0.0, dcv_n[...])
        ubuf[pl.ds(0, hb), :] = jnp.where(i == 0, 0.0, u_p[...].astype(F32))
        ubuf[pl.ds(hb, ts), :] = u_m[...].astype(F32)
        _shift_copies(dbuf, dsh)
        _shift_copies(ubuf, ush)
        dwacc[...] = jnp.zeros(dwacc.shape, F32)

        def chunk(ci, carry):
            r0 = pl.multiple_of(ci * CONF_ROWS, CONF_ROWS)
            acc = jnp.zeros((CONF_ROWS, c), F32)
            dc = dbuf[pl.ds(r0, CONF_ROWS), :]
            for k in range(CONF_K):
                acc = acc + w[k : k + 1, :] * _tap(dbuf, dsh, r0, (CONF_K - 1) - k, CONF_ROWS)
                prod = dc * _tap(ubuf, ush, r0, hb - (CONF_K - 1) + k, CONF_ROWS)
                fold = prod[0:8]
                for a in range(1, CONF_ROWS // 8):
                    fold = fold + prod[8 * a : 8 * a + 8]
                dwacc[pl.ds(8 * k, 8), :] += fold
            dub[pl.ds(r0, CONF_ROWS), :] = acc
            return carry

        lax.fori_loop(0, ts // CONF_ROWS, chunk, 0)
        for k in range(CONF_K):
            dwd_o[k : k + 1, :] = jnp.sum(dwacc[pl.ds(8 * k, 8), :], axis=0, keepdims=True)
        dwd_o[CONF_K:kp, :] = jnp.zeros((kp - CONF_K, c), F32)
        bv, bg = bglu[:, 0:c], bglu[:, c : 2 * c]
        d_u = dub[...]
        sg = _sigmoid(gg[...].astype(F32) + bg)
        d_gv = d_u * sg
        d_gg = d_u * (gv[...].astype(F32) + bv) * sg * (1.0 - sg)
        d_o[:, 0:c] = d_gv.astype(BF16)
        d_o[:, c : 2 * c] = d_gg.astype(BF16)
        dbglu_o[:, 0:c] = jnp.sum(d_gv, axis=0, keepdims=True)
        dbglu_o[:, c : 2 * c] = jnp.sum(d_gg, axis=0, keepdims=True)

    blk = lambda cb: pl.BlockSpec((ts, c), lambda i: (i, cb))
    full = lambda a: pl.BlockSpec(a.shape, lambda i: (0, 0))
    return _call(
        body,
        "mix_b_bwd_conv",
        (n_t,),
        [
            blk(0),
            pl.BlockSpec((hb, c), lambda i: (nxt(i), 0)),
            blk(0),
            pl.BlockSpec((hb, c), lambda i: (prev(i), 0)),
            blk(3),
            blk(4),
            full(b_glu),
            full(wd),
        ],
        [d_cv, d_cv, u, u, proj, proj, b_glu, wd],
        [
            pl.BlockSpec((ts, 2 * c), lambda i: (i, 0)),
            pl.BlockSpec((None, kp, c), lambda i: (i, 0, 0)),
            pl.BlockSpec((None, 1, 2 * c), lambda i: (i, 0, 0)),
        ],
        [SDS((s, 2 * c), BF16), SDS((n_t, kp, c), F32), SDS((n_t, 1, 2 * c), F32)],
        scratch=[
            pltpu.VMEM((ts + hb, c), F32), pltpu.VMEM((hb + ts, c), F32), pltpu.VMEM((ts, c), F32),
            pltpu.VMEM((7, hb + ts - 8, c), F32), pltpu.VMEM((7, hb + ts - 8, c), F32), pltpu.VMEM((8 * CONF_K, c), F32),
        ],
        sem=("parallel",),
        comm=comm,
    )


def _mix_a_bwd(d_ya, proj, wa, s, c):
    ts, hb = _tile(s, 256), HALO_A
    prev, nxt = _prev_halo(ts, hb), _next_halo(ts, hb, s)
    n_t = s // ts
    kp = wa.shape[0]

    def body(dya, dya_n, ah, ab, ac, ah_p, ac_p, ab_n, w, d_o, dwa_o, zbuf, dbuf, dzb):
        i = pl.program_id(0)
        zbuf[pl.ds(0, hb), :] = jnp.where(i == 0, 0.0, ac_p[...].astype(F32) * ah_p[...].astype(F32))
        zbuf[pl.ds(hb, ts), :] = ac[...].astype(F32) * ah[...].astype(F32)
        dbuf[pl.ds(0, ts), :] = dya[...].astype(F32) * ab[...].astype(F32)
        dbuf[pl.ds(ts, hb), :] = jnp.where(i == n_t - 1, 0.0, dya_n[...].astype(F32) * ab_n[...].astype(F32))
        dw_rows = [jnp.zeros((1, c), F32) for _ in range(CONV_A_K)]
        for r0 in range(0, ts, CONV_ROWS):
            cz = jnp.zeros((CONV_ROWS, c), F32)
            dz = jnp.zeros((CONV_ROWS, c), F32)
            dc = dbuf[pl.ds(r0, CONV_ROWS), :]
            for k in range(CONV_A_K):
                zk = zbuf[pl.ds(hb + r0 - (CONV_A_K - 1) + k, CONV_ROWS), :]
                cz = cz + w[k : k + 1, :] * zk
                dz = dz + w[k : k + 1, :] * dbuf[pl.ds(r0 + (CONV_A_K - 1) - k, CONV_ROWS), :]
                dw_rows[k] = dw_rows[k] + jnp.sum(dc * zk, axis=0, keepdims=True)
            d_o[pl.ds(r0, CONV_ROWS), c : 2 * c] = (dya[pl.ds(r0, CONV_ROWS), :].astype(F32) * cz).astype(BF16)
            dzb[pl.ds(r0, CONV_ROWS), :] = dz
        d_z = dzb[...]
        d_o[:, 0:c] = (d_z * ac[...].astype(F32)).astype(BF16)
        d_o[:, 2 * c : 3 * c] = (d_z * ah[...].astype(F32)).astype(BF16)
        for k in range(CONV_A_K):
            dwa_o[k : k + 1, :] = dw_rows[k]
        dwa_o[CONV_A_K:kp, :] = jnp.zeros((kp - CONV_A_K, c), F32)

    blk = lambda cb: pl.BlockSpec((ts, c), lambda i: (i, cb))
    hp = lambda cb: pl.BlockSpec((hb, c), lambda i: (prev(i), cb))
    hn = lambda cb: pl.BlockSpec((hb, c), lambda i: (nxt(i), cb))
    return pl.pallas_call(
        body,
        name="mix_a_bwd",
        grid=(n_t,),
        in_specs=[blk(0), hn(0), blk(0), blk(1), blk(2), hp(0), hp(2), hn(1), pl.BlockSpec(wa.shape, lambda i: (0, 0))],
        out_specs=[pl.BlockSpec((ts, 3 * c), lambda i: (i, 0)), pl.BlockSpec((None, kp, c), lambda i: (i, 0, 0))],
        out_shape=[SDS((s, 3 * c), BF16), SDS((n_t, kp, c), F32)],
        scratch_shapes=[pltpu.VMEM((hb + ts, c), F32), pltpu.VMEM((ts + hb, c), F32), pltpu.VMEM((ts, c), F32)],
        compiler_params=_params(("parallel",)),
    )(d_ya, d_ya, proj, proj, proj, proj, proj, proj, wa)


def _ep_bf16(accs, ex, os_):
    os_[0][...] = accs[0].astype(BF16)


def _mm_tn(name, a, b, tm=2048, tn=1024, tk=1024):
    m, k1 = a.shape
    n = b.shape[1]
    tm, tn, tk = _tile(k1, tm), _tile(n, tn), _tile(m, tk)
    return _fmm(
        name,
        (k1 // tm, n // tn, m // tk),
        [(a, pl.BlockSpec((tk, tm), lambda i, j, k: (k, i))), (b, pl.BlockSpec((tk, tn), lambda i, j, k: (k, j)))],
        [(0, 1, TN, 0, None)],
        [(tm, tn)],
        [],
        [(SDS((k1, n), BF16), pl.BlockSpec((tm, tn), lambda i, j, k: (i, j)))],
        _ep_bf16,
    )[0][0]


def _mm_nt(name, a, b, tm=1024, tn=1024, comm=None):
    m, kk = a.shape
    n = b.shape[0]
    tm, tn = _tile(m, tm), _tile(n, tn)
    outs, couts = _fmm(
        name,
        (m // tm, n // tn, 1),
        [(a, pl.BlockSpec((tm, kk), lambda i, j, k: (i, 0))), (b, pl.BlockSpec((tn, kk), lambda i, j, k: (j, 0)))],
        [(0, 1, NT, 0, None)],
        [(tm, tn)],
        [],
        [(SDS((m, n), BF16), pl.BlockSpec((tm, tn), lambda i, j, k: (i, j)))],
        _ep_bf16,
        comm=comm,
    )
    return outs[0], couts


def _dev_index(dev):
    return 4 * dev[0] + 2 * dev[1] + dev[2]


def _region(ref, kind, j, shard_shape):
    if kind == "col":
        ns = shard_shape[1]
        return ref.at[:, pl.ds(pl.multiple_of(j * ns, 128), ns)]
    if kind == "row":
        rs = shard_shape[0]
        return ref.at[pl.ds(pl.multiple_of(j * rs, 8), rs), :]
    return ref.at[j]


def _whole_shape(kind, shard_shape):
    if kind == "col":
        return (shard_shape[0], NDEV * shard_shape[1])
    if kind == "row":
        return (NDEV * shard_shape[0], shard_shape[1])
    return (NDEV,) + tuple(shard_shape)


def _place():
    return lax.axis_index("x"), lax.axis_index("y"), lax.axis_index("c")


def _proj_gather(n1, w_shard, early, late):
    s, d = n1.shape
    ns = w_shard.shape[1]
    pw = 2 * ns
    tm = _tile(s // 2, 512)
    n_i = s // tm
    comm = _join(early, late)
    n_early = (len(early["ins"]), len(early["outs"]), len(early["sems"]))
    assert n_i >= 2 and not comm["alias"]
    x0, y0, _ = _place()
    order = jnp.stack([2 * x0 + y0, 2 * x0 + (1 - y0), 2 * (1 - x0) + y0, 2 * (1 - x0) + (1 - y0)]).astype(jnp.int32)
    n_ci, n_co = len(comm["ins"]), len(comm["outs"])

    def body(order_ref, n1_ref, wsh_ref, *rest):
        ci = rest[:n_ci]
        proj_ref, win_ref = rest[n_ci], rest[n_ci + 1]
        co = rest[n_ci + 2 : n_ci + 2 + n_co]
        wfull, send, recv, fsend, frecv, loc, osem = rest[n_ci + 2 + n_co : n_ci + 9 + n_co]
        cs = rest[n_ci + 9 + n_co :]
        u, i = pl.program_id(0), pl.program_id(1)
        x, y, c = _place()
        sib = (x, y, 1 - c)
        chips = [(x, y), (x, 1 - y), (1 - x, y), (1 - x, 1 - y)]
        peers = [sib] + [(*ch, c) for ch in chips[1:]]
        blk = lambda ch, core: wfull.at[2 * ch[0] + ch[1], :, pl.ds(pl.multiple_of(core * ns, 128), ns)]
        sends = [_remote(blk(chips[0], c), blk(chips[0], c), send.at[k], recv.at[k], peers[k]) for k in range(4)]
        arrivals = [_remote(blk(chips[0], 1 - c), blk(chips[0], 1 - c), send.at[0], recv.at[0], sib)] + [
            _remote(blk(chips[k], c), blk(chips[k], c), send.at[k], recv.at[k], peers[k]) for k in range(1, 4)
        ]
        passes = [_remote(blk(chips[k], c), blk(chips[k], c), fsend.at[k - 1], frecv.at[k - 1], sib) for k in range(1, 4)]
        passed = [_remote(blk(chips[k], 1 - c), blk(chips[k], 1 - c), fsend.at[k - 1], frecv.at[k - 1], sib) for k in range(1, 4)]
        mine = lambda: pltpu.make_async_copy(wsh_ref, blk(chips[0], c), loc.at[0])

        def to_hbm(unit):
            q = order_ref[unit]
            return pltpu.make_async_copy(wfull.at[q], win_ref.at[:, pl.ds(pl.multiple_of(q * pw, 128), pw)], osem.at[unit])

        a, b, e = n_early
        early_refs = (ci[:a], co[:b], cs[:e])
        late_refs = (ci[a:], co[b:], cs[e:])

        @pl.when(jnp.logical_and(u == 0, i == 0))
        def _():
            mine().start()
            mine().wait()
            for snd in sends[:3]:
                snd().start()
            early["start"](*early_refs)
            arrivals[0]().wait_recv()

        @pl.when(jnp.logical_and(u == 1, i == 0))
        def _():
            sends[3]().start()

        @pl.when(jnp.logical_and(u == 2, i == 0))
        def _():
            late["start"](*late_refs)

        for nxt in range(1, 4):

            @pl.when(jnp.logical_and(u == nxt - 1, i == n_i - 1))
            def _(nxt=nxt):
                passed[nxt - 1]().wait_recv()

        proj_ref[...] = jnp.dot(n1_ref[...], wfull[order_ref[u]], preferred_element_type=F32).astype(BF16)

        for nxt in range(1, 4):

            @pl.when(jnp.logical_and(u == nxt - 1, i == n_i - 2))
            def _(nxt=nxt):
                arrivals[nxt]().wait_recv()
                passes[nxt - 1]().start()

        for unit in range(4):

            @pl.when(jnp.logical_and(u == unit, i == n_i - 1))
            def _(unit=unit):
                to_hbm(unit).start()

        @pl.when(jnp.logical_and(u == 3, i == n_i - 1))
        def _():
            for snd in sends + passes:
                snd().wait_send()
            for unit in range(4):
                to_hbm(unit).wait()
            comm["finish"](ci, co, cs)

    hbm = pl.BlockSpec(memory_space=pl.ANY)
    dma = pltpu.SemaphoreType.DMA
    res = pl.pallas_call(
        body,
        name="proj",
        grid_spec=pltpu.PrefetchScalarGridSpec(
            num_scalar_prefetch=1,
            grid=(4, n_i),
            in_specs=[pl.BlockSpec((tm, d), lambda u, i, order_ref: (i, 0)), hbm] + [hbm] * n_ci,
            out_specs=[pl.BlockSpec((tm, pw), lambda u, i, order_ref: (i, order_ref[u])), hbm] + [hbm] * n_co,
            scratch_shapes=[pltpu.VMEM((4, d, pw), BF16), dma((4,)), dma((4,)), dma((3,)), dma((3,)), dma((1,)), dma((4,))]
            + list(comm["sems"]),
        ),
        out_shape=[SDS((s, NDEV * ns), BF16), SDS((d, NDEV * ns), BF16)] + list(comm["outs"]),
        compiler_params=_params(("arbitrary", "arbitrary")),
    )(order, n1, w_shard, *comm["ins"])
    return res[0], res[1], list(res[2:])


def _peer(me, r):
    x, y, c = me
    return (1 - x if r & 4 else x, 1 - y if r & 2 else y, 1 - c if r & 1 else c)


def _remote(src, dst, send_sem, recv_sem, to):
    return lambda: pltpu.make_async_remote_copy(
        src_ref=src, dst_ref=dst, send_sem=send_sem, recv_sem=recv_sem, device_id=to, device_id_type=MESH
    )


def _run(pairs, locals_, start):
    if start:
        for cp in locals_:
            cp.start()
        for snd, _ in pairs:
            snd().start()
    else:
        for snd, arr in pairs:
            arr().wait_recv()
            snd().wait_send()
        for cp in locals_:
            cp.wait()


def _stage(ins, outs, alias, sems, build):
    return dict(
        ins=list(ins), outs=list(outs), alias=alias, sems=list(sems),
        start=lambda i, o, s: _run(*build(i, o, s), True),
        finish=lambda i, o, s: _run(*build(i, o, s), False),
    )


def _ag1(shards, kinds):
    n_t = len(shards)
    shapes = [tuple(sh.shape) for sh in shards]

    def build(srcs, dsts, sems):
        send, recv, loc = sems
        x, y, c = _place()
        me = (x, y, c)
        peers = [(x, y, 1 - c), (1 - x, y, c), (x, 1 - y, c), (1 - x, 1 - y, c)]
        reg = lambda t, dev: _region(dsts[t], kinds[t], _dev_index(dev), shapes[t])
        pairs = []
        for t in range(n_t):
            for k, peer in enumerate(peers):
                snd = _remote(srcs[t], reg(t, me), send.at[t, k], recv.at[t, k], peer)
                arr = _remote(reg(t, peer), reg(t, peer), send.at[t, k], recv.at[t, k], peer)
                pairs.append((snd, arr))
        mine = [pltpu.make_async_copy(srcs[t], reg(t, me), loc.at[t]) for t in range(n_t)]
        return pairs, mine

    outs = [SDS(_whole_shape(kinds[t], shapes[t]), shards[t].dtype) for t in range(n_t)]
    dma = pltpu.SemaphoreType.DMA
    return _stage(shards, outs, {}, [dma((n_t, 4)), dma((n_t, 4)), dma((n_t,))], build)


def _ag_direct(shards, kinds):
    n_t = len(shards)
    shapes = [tuple(sh.shape) for sh in shards]

    def build(srcs, dsts, sems):
        send, recv, loc = sems
        me = _place()
        reg = lambda t, dev: _region(dsts[t], kinds[t], _dev_index(dev), shapes[t])
        pairs = []
        for t in range(n_t):
            for r in range(1, NDEV):
                peer = _peer(me, r)
                snd = _remote(srcs[t], reg(t, me), send.at[t, r - 1], recv.at[t, r - 1], peer)
                arr = _remote(reg(t, peer), reg(t, peer), send.at[t, r - 1], recv.at[t, r - 1], peer)
                pairs.append((snd, arr))
        mine = [pltpu.make_async_copy(srcs[t], reg(t, me), loc.at[t]) for t in range(n_t)]
        return pairs, mine

    outs = [SDS(_whole_shape(kinds[t], shapes[t]), shards[t].dtype) for t in range(n_t)]
    dma = pltpu.SemaphoreType.DMA
    return _stage(shards, outs, {}, [dma((n_t, 7)), dma((n_t, 7)), dma((n_t,))], build)


def _ag2(wholes, kinds, shapes):
    n_t = len(wholes)

    def build(_, dsts, sems):
        send, recv = sems
        x, y, c = _place()
        sib = (x, y, 1 - c)
        chips = [(1 - x, y), (x, 1 - y), (1 - x, 1 - y)]
        reg = lambda t, dev: _region(dsts[t], kinds[t], _dev_index(dev), shapes[t])
        pairs = []
        for t in range(n_t):
            for j, chip in enumerate(chips):
                snd = _remote(reg(t, (*chip, c)), reg(t, (*chip, c)), send.at[t, j], recv.at[t, j], sib)
                arr = _remote(reg(t, (*chip, 1 - c)), reg(t, (*chip, 1 - c)), send.at[t, j], recv.at[t, j], sib)
                pairs.append((snd, arr))
        return pairs, []

    outs = [SDS(w.shape, w.dtype) for w in wholes]
    dma = pltpu.SemaphoreType.DMA
    return _stage(wholes, outs, {t: t for t in range(n_t)}, [dma((n_t, 3)), dma((n_t, 3))], build)


def _chip_of(q):
    return (q >> 1, q & 1)


def _rs1(wholes, kinds, shapes):
    n_t = len(wholes)

    def build(srcs, outs, sems):
        send, recv = sems
        x, y, c = _place()
        sib = (x, y, 1 - c)
        pairs = []
        for t in range(n_t):
            for q in range(4):
                theirs = _region(srcs[t], kinds[t], _dev_index((*_chip_of(q), 1 - c)), shapes[t])
                pairs.append((
                    _remote(theirs, outs[t].at[q], send.at[t, q], recv.at[t, q], sib),
                    _remote(outs[t].at[q], outs[t].at[q], send.at[t, q], recv.at[t, q], sib),
                ))
        return pairs, []

    slabs = [SDS((4,) + tuple(shapes[t]), wholes[t].dtype) for t in range(n_t)]
    dma = pltpu.SemaphoreType.DMA
    return _stage(wholes, slabs, {}, [dma((n_t, 4)), dma((n_t, 4))], build)


def _rs2(pair_sums):
    n_t = len(pair_sums)

    def build(srcs, lands, sems):
        send, recv, loc = sems
        x, y, c = _place()
        my_chip = 2 * x + y
        pairs, mine = [], []
        for t in range(n_t):
            for j, (px, py) in enumerate([(1 - x, y), (x, 1 - y), (1 - x, 1 - y)]):
                q = 2 * px + py
                pairs.append((
                    _remote(srcs[t].at[q], lands[t].at[my_chip], send.at[t, j], recv.at[t, j], (px, py, c)),
                    _remote(lands[t].at[q], lands[t].at[q], send.at[t, j], recv.at[t, j], (px, py, c)),
                ))
            mine.append(pltpu.make_async_copy(srcs[t].at[my_chip], lands[t].at[my_chip], loc.at[t]))
        return pairs, mine

    outs = [SDS(q.shape, q.dtype) for q in pair_sums]
    dma = pltpu.SemaphoreType.DMA
    return _stage(pair_sums, outs, {}, [dma((n_t, 3)), dma((n_t, 3)), dma((n_t,))], build)


def _pair_sum(name, whole, kind, got):
    _, rows, cols = got.shape
    tr = _tile(rows, 256)
    n_r = rows // tr
    core = lax.axis_index("c").astype(jnp.int32).reshape(1)

    def body(_, a, b, o):
        o[...] = (a[...].astype(F32) + b[...].astype(F32)).astype(BF16)

    if kind == "col":
        own = pl.BlockSpec((tr, cols), lambda q, i, c_ref: (i, 2 * q + c_ref[0]))
    elif kind == "row":
        own = pl.BlockSpec((tr, cols), lambda q, i, c_ref: ((2 * q + c_ref[0]) * n_r + i, 0))
    else:
        own = pl.BlockSpec((None, tr, cols), lambda q, i, c_ref: (2 * q + c_ref[0], i, 0))
    slab = pl.BlockSpec((None, tr, cols), lambda q, i, c_ref: (q, i, 0))
    return pl.pallas_call(
        body,
        name=name,
        grid_spec=pltpu.PrefetchScalarGridSpec(
            num_scalar_prefetch=1, grid=(4, n_r), in_specs=[own, slab], out_specs=slab
        ),
        out_shape=SDS(got.shape, BF16),
        compiler_params=_params(("parallel", "parallel")),
    )(core, whole, got)


def _packed_rows(parts, c_):
    offs, r0 = [], 0
    for p in parts:
        offs.append(r0)
        r0 += p.shape[1] * (p.shape[2] // c_)
    return offs, -(-r0 // 8) * 8


def _all_reduce_small(parts, c_):
    n_p = len(parts)
    offs, r_ = _packed_rows(parts, c_)

    def body(*refs):
        p_refs = refs[:n_p]
        land, total, src, send_sems, recv_sems = refs[n_p:]
        me = _place()
        my = _dev_index(me)
        src[...] = jnp.zeros((r_, c_), F32)
        for p_ref, r0 in zip(p_refs, offs):
            v = jnp.sum(p_ref[...], axis=0)
            k = v.shape[1] // c_
            for ri in range(v.shape[0]):
                for q in range(k):
                    src[r0 + ri * k + q : r0 + ri * k + q + 1, :] = v[ri : ri + 1, q * c_ : (q + 1) * c_]
        land[my] = src[...]

        def copy(r):
            peer = _peer(me, r)
            return pltpu.make_async_remote_copy(
                src_ref=src,
                dst_ref=land.at[my],
                send_sem=send_sems.at[r - 1],
                recv_sem=recv_sems.at[r - 1],
                device_id=peer,
                device_id_type=MESH,
            )

        def arrival(r):
            peer = _peer(me, r)
            slab = land.at[_dev_index(peer)]
            return pltpu.make_async_remote_copy(
                src_ref=slab,
                dst_ref=slab,
                send_sem=send_sems.at[r - 1],
                recv_sem=recv_sems.at[r - 1],
                device_id=peer,
                device_id_type=MESH,
            )

        sends = [copy(r) for r in range(1, NDEV)]
        for cp in sends:
            cp.start()
        for r in range(1, NDEV):
            arrival(r).wait_recv()
        for cp in sends:
            cp.wait_send()
        acc = land[0]
        for d in range(1, NDEV):
            acc = acc + land[d]
        total[...] = acc

    vmem = pl.BlockSpec(memory_space=pltpu.VMEM)
    return pl.pallas_call(
        body,
        name="all_reduce_small",
        in_specs=[vmem] * n_p,
        out_specs=[vmem, vmem],
        out_shape=[SDS((NDEV, r_, c_), F32), SDS((r_, c_), F32)],
        scratch_shapes=[pltpu.VMEM((r_, c_), F32), pltpu.SemaphoreType.DMA((7,)), pltpu.SemaphoreType.DMA((7,))],
        compiler_params=_params(),
    )(*parts)[1]


def _adamw_math(g, w, m, v):
    m2 = ADAM_B1 * m + (1.0 - ADAM_B1) * g
    v2 = ADAM_B2 * v + (1.0 - ADAM_B2) * (g * g)
    m_hat = m2 / (1.0 - ADAM_B1**ADAM_STEP)
    v_hat = v2 / (1.0 - ADAM_B2**ADAM_STEP)
    delta = -ADAM_LR * (m_hat / (jnp.sqrt(v_hat) + ADAM_EPS) + ADAM_WD * w)
    return delta, m2, v2


def _adamw_big(name, land, w, m, v):
    rows, cols = w.shape
    tr = _tile(rows, 256)
    n_slab = land.shape[0]

    def body(l_ref, w_ref, m_ref, v_ref, g_o, d_o, m_o, v_o):
        g = l_ref[0].astype(F32)
        for d in range(1, n_slab):
            g = g + l_ref[d].astype(F32)
        delta, m2, v2 = _adamw_math(g, w_ref[...], m_ref[...], v_ref[...])
        g_o[...] = g
        d_o[...] = delta
        m_o[...] = m2
        v_o[...] = v2

    blk = pl.BlockSpec((tr, cols), lambda i: (i, 0))
    return pl.pallas_call(
        body,
        name=name,
        grid=(rows // tr,),
        in_specs=[pl.BlockSpec((n_slab, tr, cols), lambda i: (0, i, 0)), blk, blk, blk],
        out_specs=[blk] * 4,
        out_shape=[SDS((rows, cols), F32)] * 4,
        compiler_params=_params(("parallel",)),
    )(land, w, m, v)


def _adamw_small(total, items):
    c_ = total.shape[1]
    n_it = len(items)

    def body(*refs):
        t_ref = refs[0]
        ins, outs = refs[1 : 1 + 3 * n_it], refs[1 + 3 * n_it :]
        my = _dev_index(_place())
        for q, (row0, taps, w, _, _) in enumerate(items):
            w_ref, m_ref, v_ref = ins[3 * q : 3 * q + 3]
            if taps:
                lanes = w.shape[-1]
                g = t_ref[pl.ds(row0, taps), pl.ds(pl.multiple_of(my * lanes, 128), lanes)][None]
            else:
                k = w.shape[-1] // c_
                g = jnp.concatenate([t_ref[row0 + j : row0 + j + 1, :] for j in range(k)], axis=1)
            delta, m2, v2 = _adamw_math(g, w_ref[...], m_ref[...], v_ref[...])
            for o_ref, val in zip(outs[4 * q : 4 * q + 4], (g, delta, m2, v2)):
                o_ref[...] = val

    vmem = pl.BlockSpec(memory_space=pltpu.VMEM)
    flat = [a for (_, _, w, m, v) in items for a in (w, m, v)]
    res = pl.pallas_call(
        body,
        name="adamw_small",
        in_specs=[vmem] * (1 + 3 * n_it),
        out_specs=[vmem] * (4 * n_it),
        out_shape=[SDS(w.shape, F32) for (_, _, w, _, _) in items for _ in range(4)],
    )(total, *flat)
    return [list(res[4 * q : 4 * q + 4]) for q in range(n_it)]


def kernel(x, p, g_mix, w_in, conv_a_w, w_out_a, b_glu, conf_dw_w, conf_dw_b, conf_ln_g, conf_ln_b, w_pw_b, b_pw_b, w_o, g_ffn, w_gate, w_up, w_down, g_ple, w_ple_gate, w_ple_proj, g_final, loss_target, m_g_mix, m_w_in, m_conv_a_w, m_w_out_a, m_b_glu, m_conf_dw_w, m_conf_dw_b, m_conf_ln_g, m_conf_ln_b, m_w_pw_b, m_b_pw_b, m_w_o, m_g_ffn, m_w_gate, m_w_up, m_w_down, m_g_ple, m_w_ple_gate, m_w_ple_proj, m_g_final, v_g_mix, v_w_in, v_conv_a_w, v_w_out_a, v_b_glu, v_conf_dw_w, v_conf_dw_b, v_conf_ln_g, v_conf_ln_b, v_w_pw_b, v_b_pw_b, v_w_o, v_g_ffn, v_w_gate, v_w_up, v_w_down, v_g_ple, v_w_ple_gate, v_w_ple_proj, v_g_final):
    s, d = x.shape[1], x.shape[2]
    c = conf_ln_g.shape[-1]
    pdim = w_ple_proj.shape[1]
    fs = w_gate.shape[-1]
    nin = NDEV * w_in.shape[-1]
    assert d == 2 * c and nin == 5 * c + 2 * d, (d, c, nin)
    x2, p2, tgt = x[0], p[0, 0], loss_target[0]
    gfin = g_final.reshape(1, d)

    kpa, kpb = 8, HALO_B
    wa_sh = jnp.pad(conv_a_w[0], ((0, kpa - CONV_A_K), (0, 0)))
    wd_sh = jnp.pad(conf_dw_w[0], ((0, kpb - CONF_K), (0, 0)))
    kind_of = dict(w_in="col", w_out_a="col", w_pw_b="col", w_ple_proj="col", w_o="row", w_ple_gate="row",
                   w_gate="blk", w_up="blk", w_down="blk")
    weight = dict(w_in=w_in, w_out_a=w_out_a, w_pw_b=w_pw_b, w_ple_proj=w_ple_proj, w_o=w_o, w_ple_gate=w_ple_gate,
                  w_gate=w_gate, w_up=w_up, w_down=w_down)
    shard_of = {nm: tuple(w.shape[1:]) for nm, w in weight.items()}
    bf16_shard = lambda nm: weight[nm][0].astype(BF16)
    kinds_ = lambda grp: [kind_of[nm] for nm in grp]
    shapes_ = lambda grp: [shard_of[nm] for nm in grp]
    first_stage = lambda grp: _ag1([bf16_shard(nm) for nm in grp], kinds_(grp))
    second_stage = lambda grp, parts: _ag2(parts, kinds_(grp), shapes_(grp))
    grp_1 = ["w_out_a", "w_pw_b"]
    grp_2 = ["w_o", "w_gate"]
    grp_3 = ["w_up"]
    grp_4 = ["w_down"]
    grp_5 = ["w_ple_gate", "w_ple_proj"]

    tm = _tile(s, 1024)
    tn = _tile(d, 1024)
    assert (5 * c) % tn == 0 and d % tn == 0 and c % tn == 0
    ga_blk, gb_blk = (5 * c) // tn, (5 * c + d) // tn
    ij = lambda i, j, k: (i, j)
    row_i = lambda i, j, k: (i, 0)

    n1 = _rms_fwd("rms1", x2, g_mix)
    proj, win, got = _proj_gather(
        n1, bf16_shard("w_in"),
        _join(_ag_direct([wa_sh, wd_sh], ["col", "col"]), first_stage(grp_1)), first_stage(grp_2),
    )
    (wa, wd), part_12 = got[:2], got[2:]
    grp_12 = grp_1 + grp_2
    ya_in = _mix_a_fwd(proj, wa, s, c)
    (v_act, u_act, cv), got = _mix_b_fwd(
        proj, b_glu, wd, conf_dw_b, conf_ln_g, conf_ln_b, s, c,
        comm=_join(second_stage(grp_12, part_12), first_stage(grp_3)),
    )
    (wouta, wpw, wo, wg), part_3 = got[: len(grp_12)], got[len(grp_12) :]

    def ep_merge(accs, ex, os_):
        sa = _sigmoid(ex[0][...].astype(F32))
        sb = _sigmoid(ex[1][...].astype(F32))
        ya = accs[0]
        yb = accs[1] + ex[2][...]
        os_[0][...] = (sa * ya + sb * yb).astype(BF16)
        os_[1][...] = ya.astype(BF16)
        os_[2][...] = yb.astype(BF16)

    gate_a_spec = pl.BlockSpec((tm, tn), lambda i, j, k: (i, ga_blk + j))
    gate_b_spec = pl.BlockSpec((tm, tn), lambda i, j, k: (i, gb_blk + j))
    out_sd = (SDS((s, d), BF16), pl.BlockSpec((tm, tn), ij))
    (m_act, ya, yb), got = _fmm(
        "merge", (s // tm, d // tn, 1),
        [(ya_in, pl.BlockSpec((tm, c), row_i)), (wouta, pl.BlockSpec((c, tn), lambda i, j, k: (0, j))),
         (v_act, pl.BlockSpec((tm, c), row_i)), (wpw, pl.BlockSpec((c, tn), lambda i, j, k: (0, j)))],
        [(0, 1, NN, 0, None), (2, 3, NN, 1, None)], [(tm, tn), (tm, tn)],
        [(proj, gate_a_spec), (proj, gate_b_spec), (b_pw_b, pl.BlockSpec((1, tn), lambda i, j, k: (0, j)))],
        [out_sd, out_sd, out_sd], ep_merge, csplit=EPILOGUE_CHUNK,
        comm=_join(second_stage(grp_3, part_3), first_stage(grp_4)),
    )
    (wu,), part_4 = got[: len(grp_3)], got[len(grp_3) :]

    def ep_residual(accs, ex, os_):
        os_[0][...] = (accs[0] + ex[0][...].astype(F32)).astype(BF16)

    (h1,), got = _fmm(
        "w_o", (s // tm, d // tn, 1),
        [(m_act, pl.BlockSpec((tm, d), row_i)), (wo, pl.BlockSpec((d, tn), lambda i, j, k: (0, j)))],
        [(0, 1, NN, 0, None)], [(tm, tn)], [(x2, pl.BlockSpec((tm, tn), ij))],
        [(SDS((s, d), BF16), pl.BlockSpec((tm, tn), ij))], ep_residual, csplit=EPILOGUE_CHUNK,
        comm=_join(second_stage(grp_4, part_4), first_stage(grp_5)),
    )
    (wdn,), part_5 = got[: len(grp_4)], got[len(grp_4) :]
    n2 = _rms_fwd("rms2", h1, g_ffn)

    hidden = NDEV * fs
    wg_p = _slabs_to_plain("w_gate_plain", wg)
    wu_p = _slabs_to_plain("w_up_plain", wu)
    wdn_p = wdn.reshape(hidden, d)
    tf = _tile(hidden, 512)
    tkf = _tile(hidden, 2816)

    def ep_gateup(accs, ex, os_):
        g, u = accs
        sg = _sigmoid(g)
        silu = g * sg
        os_[0][...] = (u * sg * (1.0 + g * (1.0 - sg))).astype(BF16)
        os_[1][...] = silu.astype(BF16)
        os_[2][...] = (silu * u).astype(BF16)

    ff_sd = (SDS((s, hidden), BF16), pl.BlockSpec((tm, tf), ij))
    w_col_blk = pl.BlockSpec((d, tf), lambda i, j, k: (0, j))
    (df_dg, df_du, f_act), (wpg, wpp) = _fmm(
        "gate_up", (s // tm, hidden // tf, 1),
        [(n2, pl.BlockSpec((tm, d), row_i)), (wg_p, w_col_blk), (wu_p, w_col_blk)],
        [(0, 1, NN, 0, None), (0, 2, NN, 1, None)], [(tm, tf), (tm, tf)], [],
        [ff_sd, ff_sd, ff_sd], ep_gateup,
        comm=second_stage(grp_5, part_5),
    )
    (h2,), _ = _fmm(
        "down", (s // tm, d // tn, hidden // tkf),
        [(f_act, pl.BlockSpec((tm, tkf), lambda i, j, k: (i, k))),
         (wdn_p, pl.BlockSpec((tkf, tn), lambda i, j, k: (k, j)))],
        [(0, 1, NN, 0, None)], [(tm, tn)], [(h1, pl.BlockSpec((tm, tn), ij))],
        [(SDS((s, d), BF16), pl.BlockSpec((tm, tn), ij))], ep_residual,
    )
    n3 = _rms_fwd("rms3", h2, g_ple)

    tr = _tile(s, 256)
    n_r = s // tr
    rows = lambda i, j, k: (i, 0)
    whole = lambda i, j, k: (0, 0)
    part_spec = lambda nrow: pl.BlockSpec((None, nrow, d), lambda i, j, k: (i, 0, 0))

    def ep_ple(accs, ex, os_):
        h2_, t_, gf = ex[0][...].astype(F32), ex[1][...], ex[2][...]
        ple = accs[0]
        s3 = _sigmoid(accs[1])
        h3 = h2_ + s3 * ple
        r = lax.rsqrt(jnp.mean(h3 * h3, axis=-1, keepdims=True) + EPS)
        hn = h3 * r
        e = hn * gf - t_
        loss = 0.5 * jnp.sum(jnp.mean(e * e, axis=-1, keepdims=True), axis=0, keepdims=True)
        dy = e * (1.0 / d)
        dn = dy * gf
        dh3 = r * (dn - hn * jnp.mean(dn * hn, axis=-1, keepdims=True))
        os_[0][...] = dh3.astype(BF16)
        os_[1][...] = (dh3 * s3).astype(BF16)
        os_[2][...] = (dh3 * ple * s3 * (1.0 - s3)).astype(BF16)
        os_[3][0:1, :] = jnp.sum(dy * hn, axis=0, keepdims=True)
        os_[3][1:2, :] = jnp.broadcast_to(loss, (1, d))

    (dh3, d_ple, d_g3, part_fin), _ = _fmm(
        "ple_loss", (n_r, 1, 1),
        [(p2, pl.BlockSpec((tr, pdim), rows)), (wpp, pl.BlockSpec((pdim, d), whole)),
         (n3, pl.BlockSpec((tr, d), rows)), (wpg, pl.BlockSpec((d, d), whole))],
        [(0, 1, NN, 0, None), (2, 3, NN, 1, None)], [(tr, d), (tr, d)],
        [(h2, pl.BlockSpec((tr, d), rows)), (tgt, pl.BlockSpec((tr, d), rows)), (gfin, pl.BlockSpec((1, d), whole))],
        [(SDS((s, d), BF16), pl.BlockSpec((tr, d), rows)), (SDS((s, d), BF16), pl.BlockSpec((tr, d), rows)),
         (SDS((s, d), BF16), pl.BlockSpec((tr, d), rows)), (SDS((n_r, 2, d), F32), part_spec(2))],
        ep_ple,
    )

    g_wpp = _mm_tn("d_w_ple_proj", p2, d_ple)
    g_wpg = _mm_tn("d_w_ple_gate", n3, d_g3)

    def ep_norm_bwd(accs, ex, os_):
        dh, dg = _rms_bwd(accs[0], ex[0][...].astype(F32), ex[2][...])
        os_[0][...] = (ex[1][...].astype(F32) + dh).astype(BF16)
        os_[1][...] = dg

    norm_outs = lambda t: [(SDS((s, d), BF16), pl.BlockSpec((t, d), rows)), (SDS((s // t, 1, d), F32), part_spec(1))]
    def exchange1(names, wholes):
        return _rs1(wholes, kinds_(names), shapes_(names))

    def pair_sums(names, wholes, got):
        return [_pair_sum("pair_sum_" + nm, wholes[t], kind_of[nm], got[t]) for t, nm in enumerate(names)]

    lands = {}
    grp1 = ["w_ple_proj", "w_ple_gate"]
    (dh2b, part_ple), got = _fmm(
        "d_n3", (n_r, 1, 1),
        [(d_g3, pl.BlockSpec((tr, d), rows)), (wpg, pl.BlockSpec((d, d), whole))],
        [(0, 1, NT, 0, None)], [(tr, d)],
        [(h2, pl.BlockSpec((tr, d), rows)), (dh3, pl.BlockSpec((tr, d), rows)), (g_ple, pl.BlockSpec((1, d), whole))],
        norm_outs(tr), ep_norm_bwd,
        comm=exchange1(grp1, [g_wpp, g_wpg]),
    )
    sums1 = pair_sums(grp1, [g_wpp, g_wpg], got)

    def ep_ddown(accs, ex, os_):
        df = accs[0]
        os_[0][...] = (df * ex[0][...].astype(F32)).astype(BF16)
        os_[1][...] = (df * ex[1][...].astype(F32)).astype(BF16)

    ff_in = pl.BlockSpec((tm, tf), ij)
    (d_g, d_u), got = _fmm(
        "d_down", (s // tm, hidden // tf, 1),
        [(dh2b, pl.BlockSpec((tm, d), row_i)), (wdn_p, pl.BlockSpec((tf, d), lambda i, j, k: (j, 0)))],
        [(0, 1, NT, 0, None)], [(tm, tf)], [(df_dg, ff_in), (df_du, ff_in)],
        [ff_sd, ff_sd], ep_ddown, csplit=EPILOGUE_CHUNK,
        comm=_rs2(sums1),
    )
    lands.update(zip(grp1, got))
    tk = _tile(s, 1024)
    g_wdn = _mm_tn("d_w_down", f_act, dh2b, tm=1408, tn=2048).reshape(NDEV, fs, d)

    def ep_two_bf16(accs, ex, os_):
        os_[0][...] = accs[0].astype(BF16)
        os_[1][...] = accs[1].astype(BF16)

    ff_k = pl.BlockSpec((tk, tf), lambda i, j, k: (k, j))
    wcol_sd = (SDS((d, hidden), BF16), pl.BlockSpec((d, tf), lambda i, j, k: (0, j)))
    grp2 = ["w_down"]
    (g_wg_p, g_wu_p), got = _fmm(
        "d_w_gate_up", (1, hidden // tf, s // tk),
        [(n2, pl.BlockSpec((tk, d), lambda i, j, k: (k, 0))), (d_g, ff_k), (d_u, ff_k)],
        [(0, 1, TN, 0, None), (0, 2, TN, 1, None)], [(d, tf), (d, tf)], [],
        [wcol_sd, wcol_sd], ep_two_bf16,
        comm=exchange1(grp2, [g_wdn]),
    )
    g_wg = _plain_to_slabs("d_w_gate_slabs", g_wg_p, NDEV)
    g_wu = _plain_to_slabs("d_w_up_slabs", g_wu_p, NDEV)
    sums2 = pair_sums(grp2, [g_wdn], got)
    grp3 = ["w_gate", "w_up"]
    th = _tile(s // 2, 1024)
    ff_a = pl.BlockSpec((th, tf), lambda i, j, k: (i, k))
    w_k = pl.BlockSpec((d, tf), lambda i, j, k: (0, k))
    (d_n2,), got = _fmm(
        "d_n2", (s // th, 1, hidden // tf),
        [(d_g, ff_a), (wg_p, w_k), (d_u, ff_a), (wu_p, w_k)],
        [(0, 1, NT, 0, None), (2, 3, NT, 0, None)], [(th, d)], [],
        [(SDS((s, d), BF16), pl.BlockSpec((th, d), rows))], _ep_bf16,
        comm=_join(_rs2(sums2), exchange1(grp3, [g_wg, g_wu])),
    )
    lands.update(zip(grp2, got[:1]))
    sums3 = pair_sums(grp3, [g_wg, g_wu], got[1:])
    dh1b, part_ffn = _norm_bwd("d_h1", d_n2, h1, dh2b, g_ffn, BF16)
    g_wo = _mm_tn("d_w_o", m_act, dh1b)

    def ep_dm(accs, ex, os_):
        ya_, yb_ = ex[0][...].astype(F32), ex[1][...].astype(F32)
        sa = _sigmoid(ex[2][...].astype(F32))
        sb = _sigmoid(ex[3][...].astype(F32))
        dm = accs[0]
        d_yb = dm * sb
        os_[0][...] = (dm * sa).astype(BF16)
        os_[1][...] = d_yb.astype(BF16)
        os_[2][...] = (dm * ya_ * sa * (1.0 - sa)).astype(BF16)
        os_[3][...] = (dm * yb_ * sb * (1.0 - sb)).astype(BF16)
        os_[4][...] = jnp.sum(d_yb, axis=0, keepdims=True)

    tile_ij = pl.BlockSpec((tm, tn), ij)
    grp4 = ["w_o"]
    (d_ya, d_yb, d_ga, d_gb, part_bpw), got = _fmm(
        "d_merge", (s // tm, d // tn, 1),
        [(dh1b, pl.BlockSpec((tm, d), row_i)), (wo, pl.BlockSpec((tn, d), lambda i, j, k: (j, 0)))],
        [(0, 1, NT, 0, None)], [(tm, tn)],
        [(ya, tile_ij), (yb, tile_ij), (proj, gate_a_spec), (proj, gate_b_spec)],
        [out_sd, out_sd, out_sd, out_sd,
         (SDS((s // tm, 1, d), F32), pl.BlockSpec((None, 1, tn), lambda i, j, k: (i, 0, j)))],
        ep_dm, csplit=EPILOGUE_CHUNK,
        comm=exchange1(grp4, [g_wo]),
    )
    sums4 = pair_sums(grp4, [g_wo], got)
    g_wouta = _mm_tn("d_w_out_a", ya_in, d_ya)
    g_wpw = _mm_tn("d_w_pw_b", v_act, d_yb)
    grp5 = ["w_out_a", "w_pw_b"]
    d_ya_in, got = _mm_nt("d_ya_in", d_ya, wouta, comm=exchange1(grp5, [g_wouta, g_wpw]))
    sums5 = pair_sums(grp5, [g_wouta, g_wpw], got)
    d_v, _ = _mm_nt("d_v", d_yb, wpw)
    d_cv, part_ln = _mix_b_bwd1(d_v, cv, conf_ln_g, conf_ln_b, s, c)
    (d_b, part_wd, part_bglu), got = _mix_b_bwd2(d_cv, u_act, proj, b_glu, wd, s, c, comm=_rs2(sums3))
    lands.update(zip(grp3, got))
    d_a, part_wa = _mix_a_bwd(d_ya_in, proj, wa, s, c)

    nb = nin // c
    gblk = d // c
    lo = [0, 3, 5, 5 + gblk]
    hi = [3, 5, 5 + gblk, 5 + 2 * gblk]
    pieces = [d_a, d_b, d_ga, d_gb]

    def active(q, ax):
        return lambda ids: jnp.logical_and(ids[ax] >= lo[q], ids[ax] < hi[q])

    def piece_spec(q, rows_, ax, row0=0):
        def index(i, j, k):
            ids = (i, j, k)
            col = jnp.clip(ids[ax] - lo[q], 0, hi[q] - lo[q] - 1)
            row = i + row0 if ax == 2 else jnp.where(active(q, ax)(ids), k, 0)
            return (row, col)

        return pl.BlockSpec((rows_, c), index)

    tkw = _tile(s, 1024)
    (g_win,), got = _fmm(
        "d_w_in", (1, nb, s // tkw),
        [(n1, pl.BlockSpec((tkw, d), lambda i, j, k: (k, 0)))]
        + [(pieces[q], piece_spec(q, tkw, 1)) for q in range(4)],
        [(0, 1 + q, TN, 0, active(q, 1)) for q in range(4)], [(d, c)], [],
        [(SDS((d, nin), BF16), pl.BlockSpec((d, c), lambda i, j, k: (0, j)))], _ep_bf16,
        comm=_rs2(sums4 + sums5),
    )
    lands.update(zip(grp4 + grp5, got))

    grp6 = ["w_in"]
    n_half = max(1, 3 * (s // th) // 8)

    def d_n1_rows(name, row0, n_tiles, comm, into):
        return _fmm(
            name, (n_tiles, 1, nb),
            [(pieces[q], piece_spec(q, th, 2, row0)) for q in range(4)]
            + [(win, pl.BlockSpec((d, c), lambda i, j, k: (0, k)))],
            [(q, 4, NT, 0, active(q, 2)) for q in range(4)], [(th, d)], [],
            [(SDS((s, d), BF16), pl.BlockSpec((th, d), lambda i, j, k: (i + row0, 0)))], _ep_bf16,
            comm=comm, into=into,
        )

    (d_n1,), got = d_n1_rows("d_n1_a", 0, n_half, exchange1(grp6, [g_win]), None)
    (d_n1,), got = d_n1_rows("d_n1_b", n_half, s // th - n_half, _rs2(pair_sums(grp6, [g_win], got)), d_n1)
    lands.update(zip(grp6, got))
    dx, part_mix = _norm_bwd("d_x", d_n1, x2, dh1b, g_mix, F32)

    small_parts = [part_mix, part_bglu, part_ln, part_bpw, part_ffn, part_ple, part_fin, part_wa, part_wd]
    (o_mix, o_bglu, o_ln, o_bpw, o_ffn, o_ple, o_fin, o_wa, o_wd), _ = _packed_rows(small_parts, c)
    total = _all_reduce_small(small_parts, c)
    loss = total[o_fin + d // c, 0]

    big_m = dict(w_in=m_w_in, w_out_a=m_w_out_a, w_pw_b=m_w_pw_b, w_ple_proj=m_w_ple_proj, w_o=m_w_o,
                 w_ple_gate=m_w_ple_gate, w_gate=m_w_gate, w_up=m_w_up, w_down=m_w_down)
    big_v = dict(w_in=v_w_in, w_out_a=v_w_out_a, w_pw_b=v_w_pw_b, w_ple_proj=v_w_ple_proj, w_o=v_w_o,
                 w_ple_gate=v_w_ple_gate, w_gate=v_w_gate, w_up=v_w_up, w_down=v_w_down)
    big_out = {}
    for nm in weight:
        res = _adamw_big("adamw_" + nm, lands[nm], weight[nm][0], big_m[nm][0], big_v[nm][0])
        big_out[nm] = [r[None] for r in res]

    row = lambda a: a.reshape(1, d)
    small = dict(
        g_mix=(o_mix, 0, g_mix, m_g_mix, v_g_mix),
        conv_a_w=(o_wa, CONV_A_K, conv_a_w, m_conv_a_w, v_conv_a_w),
        b_glu=(o_bglu, 0, b_glu, m_b_glu, v_b_glu),
        conf_dw_w=(o_wd, CONF_K, conf_dw_w, m_conf_dw_w, v_conf_dw_w),
        conf_dw_b=(o_ln + 2, 0, conf_dw_b, m_conf_dw_b, v_conf_dw_b),
        conf_ln_g=(o_ln, 0, conf_ln_g, m_conf_ln_g, v_conf_ln_g),
        conf_ln_b=(o_ln + 1, 0, conf_ln_b, m_conf_ln_b, v_conf_ln_b),
        b_pw_b=(o_bpw, 0, b_pw_b, m_b_pw_b, v_b_pw_b),
        g_ffn=(o_ffn, 0, g_ffn, m_g_ffn, v_g_ffn),
        g_ple=(o_ple, 0, g_ple, m_g_ple, v_g_ple),
        g_final=(o_fin, 0, row(g_final), row(m_g_final), row(v_g_final)),
    )
    small_out = dict(zip(small, _adamw_small(total, list(small.values()))))
    small_out["g_final"] = [a.reshape(d) for a in small_out["g_final"]]

    order = ["g_mix", "w_in", "conv_a_w", "w_out_a", "b_glu", "conf_dw_w", "conf_dw_b", "conf_ln_g", "conf_ln_b", "w_pw_b", "b_pw_b", "w_o", "g_ffn", "w_gate", "w_up", "w_down", "g_ple", "w_ple_gate", "w_ple_proj", "g_final"]
    allo = {**big_out, **small_out}
    outs = [loss, dx[None]]
    for q in range(4):
        outs += [allo[nm][q] for nm in order]
    return tuple(outs)
```

```python
import jax
import jax.numpy as jnp
from jax import lax
from jax.experimental import pallas as pl
from jax.experimental.pallas import tpu as pltpu

F32, BF16 = jnp.float32, jnp.bfloat16
EPS, LN_EPS = 1e-6, 1e-5
ADAM_LR, ADAM_B1, ADAM_B2, ADAM_EPS, ADAM_WD, ADAM_STEP = 0.001, 0.9, 0.999, 1e-08, 0.01, 10
CONV_A_K, CONF_K = 3, 31
NDEV = 8
NN = (((1,), (0,)), ((), ()))
NT = (((1,), (1,)), ((), ()))
TN = (((0,), (0,)), ((), ()))
V7X_VMEM_LIMIT_BYTES = 56 * 1024 * 1024
V7X_PROJ_VMEM_LIMIT_BYTES = 60 * 1024 * 1024
MESH = pl.DeviceIdType.MESH
SDS = jax.ShapeDtypeStruct
HALO_A, HALO_B = 16, 32
EPILOGUE_CHUNK = 256
CONV_ROWS = 32
CONF_ROWS = 16


def _tile(n, pref):
    t = min(n, pref)
    while n % t:
        t -= 8
    return t


def _sigmoid(x):
    return jax.nn.sigmoid(x)


def _params(sem=None):
    return pltpu.CompilerParams(vmem_limit_bytes=V7X_VMEM_LIMIT_BYTES, dimension_semantics=sem)


def _edge(grid, last):
    cond = None
    for ax, n in enumerate(grid):
        here = pl.program_id(ax) == (n - 1 if last else 0)
        cond = here if cond is None else jnp.logical_and(cond, here)
    return cond


def _join(*comms):
    ins, outs, alias, sems, spans = [], [], {}, [], []
    for cm in comms:
        spans.append((len(ins), len(outs), len(sems)))
        for i, o in cm["alias"].items():
            alias[len(ins) + i] = len(outs) + o
        ins += cm["ins"]
        outs += cm["outs"]
        sems += cm["sems"]

    def run(which):
        def f(i_refs, o_refs, s_refs):
            for cm, (a, b, c_) in zip(comms, spans):
                cm[which](
                    i_refs[a : a + len(cm["ins"])], o_refs[b : b + len(cm["outs"])], s_refs[c_ : c_ + len(cm["sems"])]
                )

        return f

    return dict(ins=ins, outs=outs, alias=alias, sems=sems, start=run("start"), finish=run("finish"))


def _call(body, name, grid, in_specs, args, out_specs, out_shape, scratch=(), sem=None, comm=None, alias=None):
    n_in, n_out, n_s = len(args), len(out_shape), len(scratch)
    alias = dict(alias or {})
    if comm is None:
        res = pl.pallas_call(
            body, name=name, grid=grid, in_specs=list(in_specs), out_specs=list(out_specs), out_shape=list(out_shape),
            scratch_shapes=list(scratch), input_output_aliases=alias, compiler_params=_params(sem),
        )(*args)
        return list(res), []
    n_ci, n_co = len(comm["ins"]), len(comm["outs"])

    def wrapped(*refs):
        ins = refs[:n_in]
        ci = refs[n_in : n_in + n_ci]
        o0 = n_in + n_ci
        outs = refs[o0 : o0 + n_out]
        co = refs[o0 + n_out : o0 + n_out + n_co]
        s0 = o0 + n_out + n_co
        sc = refs[s0 : s0 + n_s]
        cs = refs[s0 + n_s :]
        pl.when(_edge(grid, False))(lambda: comm["start"](ci, co, cs))
        body(*ins, *outs, *sc)
        pl.when(_edge(grid, True))(lambda: comm["finish"](ci, co, cs))

    hbm = pl.BlockSpec(memory_space=pl.ANY)
    res = pl.pallas_call(
        wrapped,
        name=name,
        grid=grid,
        in_specs=list(in_specs) + [hbm] * n_ci,
        out_specs=list(out_specs) + [hbm] * n_co,
        out_shape=list(out_shape) + list(comm["outs"]),
        scratch_shapes=list(scratch) + list(comm["sems"]),
        input_output_aliases={**alias, **{n_in + i: n_out + o for i, o in comm["alias"].items()}},
        compiler_params=_params(("arbitrary",) * len(grid)),
    )(*args, *comm["ins"])
    return list(res[:n_out]), list(res[n_out:])


def _col_chunks(n, pref):
    out, c0 = [], 0
    while c0 < n:
        w = min(pref, n - c0)
        out.append((c0, w))
        c0 += w
    return out


def _fmm(name, grid, operands, terms, acc_shapes, extras, outs, epilogue, comm=None, csplit=None, into=None):
    n_p, n_e, n_o, n_a = len(operands), len(extras), len(outs), len(acc_shapes)
    nk = grid[-1]
    kax = len(grid) - 1
    simple = nk == 1 and all(t[4] is None for t in terms)
    alias = None
    if into is not None:
        extras = list(extras) + [(into, pl.BlockSpec(memory_space=pl.ANY))]
        alias = {n_p + n_e: 0}
        n_e += 1
    if csplit is not None:
        assert simple and into is None and all(t[2] in (NN, NT) and (len(t) <= 5 or not t[5]) for t in terms)
        tn_ = acc_shapes[0][1]
        chunks = _col_chunks(tn_, csplit)

    def dot(a, b, dims):
        if a.dtype != BF16:
            a = a.astype(BF16)
        if b.dtype != BF16:
            b = b.astype(BF16)
        return lax.dot_general(a, b, dims, preferred_element_type=F32)

    def value(refs, term):
        slabs = term[5] if len(term) > 5 else 0
        if not slabs:
            return dot(refs[term[0]][...], refs[term[1]][...], term[2])
        tot = None
        for sl in range(slabs):
            d = dot(refs[term[0]][sl], refs[term[1]][sl], term[2])
            tot = d if tot is None else tot + d
        return tot

    def always(refs):
        parts = [None] * n_a
        for term in terms:
            if term[4] is None:
                d = value(refs, term)
                parts[term[3]] = d if parts[term[3]] is None else parts[term[3]] + d
        return parts

    def chunked(refs, ex, os_, accs):
        cols = lambda ref, c0, w: ref.at[:, pl.ds(c0, w)] if ref.shape[-1] == tn_ else ref

        def dots(k):
            c0, w = chunks[k]
            parts = [None] * n_a
            for term in terms:
                b_ref = refs[term[1]]
                b = b_ref[:, pl.ds(c0, w)] if term[2] == NN else b_ref[pl.ds(c0, w), :]
                d = dot(refs[term[0]][...], b, term[2])
                parts[term[3]] = d if parts[term[3]] is None else parts[term[3]] + d
            for ai in range(n_a):
                accs[ai][k % 2, :, pl.ds(0, w)] = parts[ai]

        def finish(k):
            c0, w = chunks[k]
            vals = [accs[ai][k % 2, :, pl.ds(0, w)] for ai in range(n_a)]
            epilogue(vals, [cols(e, c0, w) for e in ex], [cols(o, c0, w) for o in os_])

        dots(0)
        for k in range(1, len(chunks)):
            dots(k)
            finish(k - 1)
        finish(len(chunks) - 1)

    def body(*refs):
        ex = refs[n_p : n_p + n_e]
        os_ = refs[n_p + n_e : n_p + n_e + n_o]
        accs = refs[n_p + n_e + n_o :]
        if simple and csplit is not None:
            chunked(refs, ex, os_, accs)
            return
        if simple:
            epilogue(always(refs), ex, os_)
            return
        ids = [pl.program_id(ax) for ax in range(len(grid))]
        k = ids[kax]

        @pl.when(k == 0)
        def _():
            for acc in accs:
                acc[...] = jnp.zeros(acc.shape, F32)

        for ai, part in enumerate(always(refs)):
            if part is not None:
                accs[ai][...] += part
        for term in terms:
            if term[4] is not None:

                def add(term=term):
                    accs[term[3]][...] += value(refs, term)

                pl.when(term[4](ids))(add)

        @pl.when(k == nk - 1)
        def _():
            epilogue([acc[...] for acc in accs], ex, os_)

    return _call(
        body,
        name,
        grid,
        [o[1] for o in operands] + [e[1] for e in extras],
        [o[0] for o in operands] + [e[0] for e in extras],
        [o[1] for o in outs],
        [o[0] for o in outs],
        scratch=[pltpu.VMEM((2, s[0], csplit), F32) for s in acc_shapes] if csplit is not None
        else [] if simple else [pltpu.VMEM(s, F32) for s in acc_shapes],
        sem=("parallel",) * kax + ("arbitrary",),
        comm=comm,
        alias=alias,
    )


def _rms_bwd(dn_raw, h, g):
    r = lax.rsqrt(jnp.mean(h * h, axis=-1, keepdims=True) + EPS)
    hn = h * r
    dg = jnp.sum(dn_raw * hn, axis=0, keepdims=True)
    dn = dn_raw * g
    dh = r * (dn - hn * jnp.mean(dn * hn, axis=-1, keepdims=True))
    return dh, dg


def _rms_fwd(name, h, g):
    s, d = h.shape
    ts = _tile(s, 512)

    def body(h_ref, g_ref, o_ref):
        x = h_ref[...].astype(F32)
        r = lax.rsqrt(jnp.mean(x * x, axis=-1, keepdims=True) + EPS)
        o_ref[...] = (x * r * g_ref[...]).astype(BF16)

    return pl.pallas_call(
        body,
        name=name,
        grid=(s // ts,),
        in_specs=[pl.BlockSpec((ts, d), lambda i: (i, 0)), pl.BlockSpec((1, d), lambda i: (0, 0))],
        out_specs=pl.BlockSpec((ts, d), lambda i: (i, 0)),
        out_shape=SDS((s, d), BF16),
        compiler_params=_params(("parallel",)),
    )(h, g)


def _norm_bwd(name, dn, h, dres, g, out_dtype):
    s, d = h.shape
    ts = _tile(s, 512)

    def body(dn_r, h_r, dres_r, g_r, dh_o, part_o):
        dh, dg = _rms_bwd(dn_r[...].astype(F32), h_r[...].astype(F32), g_r[...])
        dh_o[...] = (dres_r[...].astype(F32) + dh).astype(out_dtype)
        part_o[...] = dg

    blk = pl.BlockSpec((ts, d), lambda i: (i, 0))
    part = pl.BlockSpec((None, 1, d), lambda i: (i, 0, 0))
    return pl.pallas_call(
        body,
        name=name,
        grid=(s // ts,),
        in_specs=[blk, blk, blk, pl.BlockSpec((1, d), lambda i: (0, 0))],
        out_specs=[blk, part],
        out_shape=[SDS((s, d), out_dtype), SDS((s // ts, 1, d), F32)],
        compiler_params=_params(("parallel",)),
    )(dn, h, dres, g)


def _slabs_to_plain(name, x):
    n, rows, w = x.shape
    tr = _tile(rows, 256)

    def body(x_ref, o_ref):
        for j in range(n):
            o_ref[:, j * w : (j + 1) * w] = x_ref[j]

    return pl.pallas_call(
        body,
        name=name,
        grid=(rows // tr,),
        in_specs=[pl.BlockSpec((n, tr, w), lambda i: (0, i, 0))],
        out_specs=pl.BlockSpec((tr, n * w), lambda i: (i, 0)),
        out_shape=SDS((rows, n * w), x.dtype),
        compiler_params=_params(("parallel",)),
    )(x)


def _plain_to_slabs(name, x, n):
    rows, nw = x.shape
    w = nw // n
    tr = _tile(rows, 256)

    def body(x_ref, o_ref):
        for j in range(n):
            o_ref[j] = x_ref[:, j * w : (j + 1) * w]

    return pl.pallas_call(
        body,
        name=name,
        grid=(rows // tr,),
        in_specs=[pl.BlockSpec((tr, nw), lambda i: (i, 0))],
        out_specs=pl.BlockSpec((n, tr, w), lambda i: (0, i, 0)),
        out_shape=SDS((n, rows, w), x.dtype),
        compiler_params=_params(("parallel",)),
    )(x)


def _prev_halo(ts, hb):
    r = ts // hb
    return lambda i: jnp.maximum(i * r - 1, 0)


def _next_halo(ts, hb, s):
    r = ts // hb
    last = s // hb - 1
    return lambda i: jnp.minimum((i + 1) * r, last)


def _shift_copies(buf, sh):
    n = sh.shape[1]
    for j in range(1, 8):
        sh[j - 1, pl.ds(0, n), :] = buf[pl.ds(j, n), :]


def _tap(buf, sh, r0, off, rows):
    j = off % 8
    start = pl.multiple_of(r0 + (off - j), 8)
    if j == 0:
        return buf[pl.ds(start, rows), :]
    return sh[j - 1, pl.ds(start, rows), :]


def _mix_a_fwd(proj, wa, s, c):
    ts, hb = _tile(s, 256), HALO_A
    prev = _prev_halo(ts, hb)

    def body(ah, ab, ac, hh, hc, w, o, buf):
        i = pl.program_id(0)
        zh = hc[...].astype(F32) * hh[...].astype(F32)
        buf[pl.ds(0, hb), :] = jnp.where(i == 0, 0.0, zh)
        buf[pl.ds(hb, ts), :] = ac[...].astype(F32) * ah[...].astype(F32)
        for r0 in range(0, ts, CONV_ROWS):
            cz = jnp.zeros((CONV_ROWS, c), F32)
            for k in range(CONV_A_K):
                cz = cz + w[k : k + 1, :] * buf[pl.ds(hb + r0 - (CONV_A_K - 1) + k, CONV_ROWS), :]
            o[pl.ds(r0, CONV_ROWS), :] = (ab[pl.ds(r0, CONV_ROWS), :].astype(F32) * cz).astype(BF16)

    main = lambda cb: pl.BlockSpec((ts, c), lambda i: (i, cb))
    halo = lambda cb: pl.BlockSpec((hb, c), lambda i: (prev(i), cb))
    return pl.pallas_call(
        body,
        name="mix_a_fwd",
        grid=(s // ts,),
        in_specs=[main(0), main(1), main(2), halo(0), halo(2), pl.BlockSpec(wa.shape, lambda i: (0, 0))],
        out_specs=pl.BlockSpec((ts, c), lambda i: (i, 0)),
        out_shape=SDS((s, c), BF16),
        scratch_shapes=[pltpu.VMEM((hb + ts, c), F32)],
        compiler_params=_params(("parallel",)),
    )(proj, proj, proj, proj, proj, wa)


def _mix_b_fwd(proj, b_glu, wd, bd, lg, lb, s, c, comm=None):
    ts, hb = _tile(s, 256), HALO_B
    prev = _prev_halo(ts, hb)

    def body(gv, gg, hv, hg, bglu, w, bd_r, lg_r, lb_r, v_o, u_o, cv_o, buf, sh):
        i = pl.program_id(0)
        bv, bg = bglu[:, 0:c], bglu[:, c : 2 * c]
        uh = (hv[...].astype(F32) + bv) * _sigmoid(hg[...].astype(F32) + bg)
        buf[pl.ds(0, hb), :] = jnp.where(i == 0, 0.0, uh)
        u = (gv[...].astype(F32) + bv) * _sigmoid(gg[...].astype(F32) + bg)
        buf[pl.ds(hb, ts), :] = u
        u_o[...] = u.astype(BF16)
        _shift_copies(buf, sh)

        def chunk(ci, carry):
            r0 = pl.multiple_of(ci * CONF_ROWS, CONF_ROWS)
            acc = jnp.zeros((CONF_ROWS, c), F32)
            for k in range(CONF_K):
                acc = acc + w[k : k + 1, :] * _tap(buf, sh, r0, hb - (CONF_K - 1) + k, CONF_ROWS)
            cv_o[pl.ds(r0, CONF_ROWS), :] = acc + bd_r[...]
            return carry

        lax.fori_loop(0, ts // CONF_ROWS, chunk, 0)
        cv = cv_o[...]
        mu = jnp.mean(cv, axis=-1, keepdims=True)
        xc = cv - mu
        rs = lax.rsqrt(jnp.mean(xc * xc, axis=-1, keepdims=True) + LN_EPS)
        ln = xc * rs * lg_r[...] + lb_r[...]
        v_o[...] = (ln * _sigmoid(ln)).astype(BF16)

    main = lambda cb: pl.BlockSpec((ts, c), lambda i: (i, cb))
    halo = lambda cb: pl.BlockSpec((hb, c), lambda i: (prev(i), cb))
    full = lambda a: pl.BlockSpec(a.shape, lambda i: (0, 0))
    out = pl.BlockSpec((ts, c), lambda i: (i, 0))
    return _call(
        body,
        "mix_b_fwd",
        (s // ts,),
        [main(3), main(4), halo(3), halo(4), full(b_glu), full(wd), full(bd), full(lg), full(lb)],
        [proj, proj, proj, proj, b_glu, wd, bd, lg, lb],
        [out, out, out],
        [SDS((s, c), BF16), SDS((s, c), BF16), SDS((s, c), F32)],
        scratch=[pltpu.VMEM((hb + ts, c), F32), pltpu.VMEM((7, hb + ts - 8, c), F32)],
        sem=("parallel",),
        comm=comm,
    )


def _mix_b_bwd1(d_v, cv, lg, lb, s, c):
    ts = _tile(s, 256)

    def body(dv_r, cv_r, lg_r, lb_r, dcv_o, part_o):
        cv_ = cv_r[...]
        mu = jnp.mean(cv_, axis=-1, keepdims=True)
        xc = cv_ - mu
        rs = lax.rsqrt(jnp.mean(xc * xc, axis=-1, keepdims=True) + LN_EPS)
        xh = xc * rs
        ln = xh * lg_r[...] + lb_r[...]
        sg = _sigmoid(ln)
        d_ln = dv_r[...].astype(F32) * (sg * (1.0 + ln * (1.0 - sg)))
        dy = d_ln * lg_r[...]
        d_cv = rs * (dy - jnp.mean(dy, axis=-1, keepdims=True) - xh * jnp.mean(dy * xh, axis=-1, keepdims=True))
        dcv_o[...] = d_cv
        part_o[0:1, :] = jnp.sum(d_ln * xh, axis=0, keepdims=True)
        part_o[1:2, :] = jnp.sum(d_ln, axis=0, keepdims=True)
        part_o[2:3, :] = jnp.sum(d_cv, axis=0, keepdims=True)

    blk = pl.BlockSpec((ts, c), lambda i: (i, 0))
    full = lambda a: pl.BlockSpec(a.shape, lambda i: (0, 0))
    return pl.pallas_call(
        body,
        name="mix_b_bwd_ln",
        grid=(s // ts,),
        in_specs=[blk, blk, full(lg), full(lb)],
        out_specs=[blk, pl.BlockSpec((None, 3, c), lambda i: (i, 0, 0))],
        out_shape=[SDS((s, c), F32), SDS((s // ts, 3, c), F32)],
        compiler_params=_params(("parallel",)),
    )(d_v, cv, lg, lb)


def _mix_b_bwd2(d_cv, u, proj, b_glu, wd, s, c, comm=None):
    ts, hb = _tile(s, 256), HALO_B
    prev, nxt = _prev_halo(ts, hb), _next_halo(ts, hb, s)
    n_t = s // ts
    kp = wd.shape[0]

    def body(dcv, dcv_n, u_m, u_p, gv, gg, bglu, w, d_o, dwd_o, dbglu_o, dbuf, ubuf, dub, dsh, ush, dwacc):
        i = pl.program_id(0)
        dbuf[pl.ds(0, ts), :] = dcv[...]
        dbuf[pl.ds(ts, hb), :] = jnp.where(i == n_t - 1, 0.0, dcv_n[...])
        ubuf[pl.ds(0, hb), :] = jnp.where(i == 0, 0.0, u_p[...].astype(F32))
        ubuf[pl.ds(hb, ts), :] = u_m[...].astype(F32)
        _shift_copies(dbuf, dsh)
        _shift_copies(ubuf, ush)
        dwacc[...] = jnp.zeros(dwacc.shape, F32)

        def chunk(ci, carry):
            r0 = pl.multiple_of(ci * CONF_ROWS, CONF_ROWS)
            acc = jnp.zeros((CONF_ROWS, c), F32)
            dc = dbuf[pl.ds(r0, CONF_ROWS), :]
            for k in range(CONF_K):
                acc = acc + w[k : k + 1, :] * _tap(dbuf, dsh, r0, (CONF_K - 1) - k, CONF_ROWS)
                prod = dc * _tap(ubuf, ush, r0, hb - (CONF_K - 1) + k, CONF_ROWS)
                fold = prod[0:8]
                for a in range(1, CONF_ROWS // 8):
                    fold = fold + prod[8 * a : 8 * a + 8]
                dwacc[pl.ds(8 * k, 8), :] += fold
            dub[pl.ds(r0, CONF_ROWS), :] = acc
            return carry

        lax.fori_loop(0, ts // CONF_ROWS, chunk, 0)
        for k in range(CONF_K):
            dwd_o[k : k + 1, :] = jnp.sum(dwacc[pl.ds(8 * k, 8), :], axis=0, keepdims=True)
        dwd_o[CONF_K:kp, :] = jnp.zeros((kp - CONF_K, c), F32)
        bv, bg = bglu[:, 0:c], bglu[:, c : 2 * c]
        d_u = dub[...]
        sg = _sigmoid(gg[...].astype(F32) + bg)
        d_gv = d_u * sg
        d_gg = d_u * (gv[...].astype(F32) + bv) * sg * (1.0 - sg)
        d_o[:, 0:c] = d_gv.astype(BF16)
        d_o[:, c : 2 * c] = d_gg.astype(BF16)
        dbglu_o[:, 0:c] = jnp.sum(d_gv, axis=0, keepdims=True)
        dbglu_o[:, c : 2 * c] = jnp.sum(d_gg, axis=0, keepdims=True)

    blk = lambda cb: pl.BlockSpec((ts, c), lambda i: (i, cb))
    full = lambda a: pl.BlockSpec(a.shape, lambda i: (0, 0))
    return _call(
        body,
        "mix_b_bwd_conv",
        (n_t,),
        [
            blk(0),
            pl.BlockSpec((hb, c), lambda i: (nxt(i), 0)),
            blk(0),
            pl.BlockSpec((hb, c), lambda i: (prev(i), 0)),
            blk(3),
            blk(4),
            full(b_glu),
            full(wd),
        ],
        [d_cv, d_cv, u, u, proj, proj, b_glu, wd],
        [
            pl.BlockSpec((ts, 2 * c), lambda i: (i, 0)),
            pl.BlockSpec((None, kp, c), lambda i: (i, 0, 0)),
            pl.BlockSpec((None, 1, 2 * c), lambda i: (i, 0, 0)),
        ],
        [SDS((s, 2 * c), BF16), SDS((n_t, kp, c), F32), SDS((n_t, 1, 2 * c), F32)],
        scratch=[
            pltpu.VMEM((ts + hb, c), F32), pltpu.VMEM((hb + ts, c), F32), pltpu.VMEM((ts, c), F32),
            pltpu.VMEM((7, hb + ts - 8, c), F32), pltpu.VMEM((7, hb + ts - 8, c), F32), pltpu.VMEM((8 * CONF_K, c), F32),
        ],
        sem=("parallel",),
        comm=comm,
    )


def _mix_a_bwd(d_ya, proj, wa, s, c):
    ts, hb = _tile(s, 256), HALO_A
    prev, nxt = _prev_halo(ts, hb), _next_halo(ts, hb, s)
    n_t = s // ts
    kp = wa.shape[0]

    def body(dya, dya_n, ah, ab, ac, ah_p, ac_p, ab_n, w, d_o, dwa_o, zbuf, dbuf, dzb):
        i = pl.program_id(0)
        zbuf[pl.ds(0, hb), :] = jnp.where(i == 0, 0.0, ac_p[...].astype(F32) * ah_p[...].astype(F32))
        zbuf[pl.ds(hb, ts), :] = ac[...].astype(F32) * ah[...].astype(F32)
        dbuf[pl.ds(0, ts), :] = dya[...].astype(F32) * ab[...].astype(F32)
        dbuf[pl.ds(ts, hb), :] = jnp.where(i == n_t - 1, 0.0, dya_n[...].astype(F32) * ab_n[...].astype(F32))
        dw_rows = [jnp.zeros((1, c), F32) for _ in range(CONV_A_K)]
        for r0 in range(0, ts, CONV_ROWS):
            cz = jnp.zeros((CONV_ROWS, c), F32)
            dz = jnp.zeros((CONV_ROWS, c), F32)
            dc = dbuf[pl.ds(r0, CONV_ROWS), :]
            for k in range(CONV_A_K):
                zk = zbuf[pl.ds(hb + r0 - (CONV_A_K - 1) + k, CONV_ROWS), :]
                cz = cz + w[k : k + 1, :] * zk
                dz = dz + w[k : k + 1, :] * dbuf[pl.ds(r0 + (CONV_A_K - 1) - k, CONV_ROWS), :]
                dw_rows[k] = dw_rows[k] + jnp.sum(dc * zk, axis=0, keepdims=True)
            d_o[pl.ds(r0, CONV_ROWS), c : 2 * c] = (dya[pl.ds(r0, CONV_ROWS), :].astype(F32) * cz).astype(BF16)
            dzb[pl.ds(r0, CONV_ROWS), :] = dz
        d_z = dzb[...]
        d_o[:, 0:c] = (d_z * ac[...].astype(F32)).astype(BF16)
        d_o[:, 2 * c : 3 * c] = (d_z * ah[...].astype(F32)).astype(BF16)
        for k in range(CONV_A_K):
            dwa_o[k : k + 1, :] = dw_rows[k]
        dwa_o[CONV_A_K:kp, :] = jnp.zeros((kp - CONV_A_K, c), F32)

    blk = lambda cb: pl.BlockSpec((ts, c), lambda i: (i, cb))
    hp = lambda cb: pl.BlockSpec((hb, c), lambda i: (prev(i), cb))
    hn = lambda cb: pl.BlockSpec((hb, c), lambda i: (nxt(i), cb))
    return pl.pallas_call(
        body,
        name="mix_a_bwd",
        grid=(n_t,),
        in_specs=[blk(0), hn(0), blk(0), blk(1), blk(2), hp(0), hp(2), hn(1), pl.BlockSpec(wa.shape, lambda i: (0, 0))],
        out_specs=[pl.BlockSpec((ts, 3 * c), lambda i: (i, 0)), pl.BlockSpec((None, kp, c), lambda i: (i, 0, 0))],
        out_shape=[SDS((s, 3 * c), BF16), SDS((n_t, kp, c), F32)],
        scratch_shapes=[pltpu.VMEM((hb + ts, c), F32), pltpu.VMEM((ts + hb, c), F32), pltpu.VMEM((ts, c), F32)],
        compiler_params=_params(("parallel",)),
    )(d_ya, d_ya, proj, proj, proj, proj, proj, proj, wa)


def _ep_bf16(accs, ex, os_):
    os_[0][...] = accs[0].astype(BF16)


def _mm_tn(name, a, b, tm=2048, tn=1024, tk=1024):
    m, k1 = a.shape
    n = b.shape[1]
    tm, tn, tk = _tile(k1, tm), _tile(n, tn), _tile(m, tk)
    return _fmm(
        name,
        (k1 // tm, n // tn, m // tk),
        [(a, pl.BlockSpec((tk, tm), lambda i, j, k: (k, i))), (b, pl.BlockSpec((tk, tn), lambda i, j, k: (k, j)))],
        [(0, 1, TN, 0, None)],
        [(tm, tn)],
        [],
        [(SDS((k1, n), BF16), pl.BlockSpec((tm, tn), lambda i, j, k: (i, j)))],
        _ep_bf16,
    )[0][0]


def _mm_nt(name, a, b, tm=1024, tn=1024, comm=None):
    m, kk = a.shape
    n = b.shape[0]
    tm, tn = _tile(m, tm), _tile(n, tn)
    outs, couts = _fmm(
        name,
        (m // tm, n // tn, 1),
        [(a, pl.BlockSpec((tm, kk), lambda i, j, k: (i, 0))), (b, pl.BlockSpec((tn, kk), lambda i, j, k: (j, 0)))],
        [(0, 1, NT, 0, None)],
        [(tm, tn)],
        [],
        [(SDS((m, n), BF16), pl.BlockSpec((tm, tn), lambda i, j, k: (i, j)))],
        _ep_bf16,
        comm=comm,
    )
    return outs[0], couts


def _dev_index(dev):
    return 4 * dev[0] + 2 * dev[1] + dev[2]


def _region(ref, kind, j, shard_shape):
    if kind == "col":
        ns = shard_shape[1]
        return ref.at[:, pl.ds(pl.multiple_of(j * ns, 128), ns)]
    if kind == "row":
        rs = shard_shape[0]
        return ref.at[pl.ds(pl.multiple_of(j * rs, 8), rs), :]
    return ref.at[j]


def _whole_shape(kind, shard_shape):
    if kind == "col":
        return (shard_shape[0], NDEV * shard_shape[1])
    if kind == "row":
        return (NDEV * shard_shape[0], shard_shape[1])
    return (NDEV,) + tuple(shard_shape)


def _place():
    return lax.axis_index("x"), lax.axis_index("y"), lax.axis_index("c")


def _proj_gather(n1, w_shard, early, late):
    s, d = n1.shape
    ns = w_shard.shape[1]
    pw = 2 * ns
    tm = _tile(s // 2, 1024)
    n_rows = s // tm
    n_i = 2 * n_rows
    comm = _join(early, late)
    n_early = (len(early["ins"]), len(early["outs"]), len(early["sems"]))
    assert n_i >= 2 and not comm["alias"]
    x0, y0, _ = _place()
    order = jnp.stack([2 * x0 + y0, 2 * x0 + (1 - y0), 2 * (1 - x0) + y0, 2 * (1 - x0) + (1 - y0)]).astype(jnp.int32)
    n_ci, n_co = len(comm["ins"]), len(comm["outs"])

    def body(order_ref, n1_ref, wsh_ref, *rest):
        ci = rest[:n_ci]
        proj_ref, win_ref = rest[n_ci], rest[n_ci + 1]
        co = rest[n_ci + 2 : n_ci + 2 + n_co]
        wfull, send, recv, fsend, frecv, loc, osem = rest[n_ci + 2 + n_co : n_ci + 9 + n_co]
        cs = rest[n_ci + 9 + n_co :]
        u, i = pl.program_id(0), pl.program_id(1)
        x, y, c = _place()
        sib = (x, y, 1 - c)
        chips = [(x, y), (x, 1 - y), (1 - x, y), (1 - x, 1 - y)]
        peers = [sib] + [(*ch, c) for ch in chips[1:]]
        blk = lambda ch, core: wfull.at[2 * ch[0] + ch[1], :, pl.ds(pl.multiple_of(core * ns, 128), ns)]
        sends = [_remote(blk(chips[0], c), blk(chips[0], c), send.at[k], recv.at[k], peers[k]) for k in range(4)]
        arrivals = [_remote(blk(chips[0], 1 - c), blk(chips[0], 1 - c), send.at[0], recv.at[0], sib)] + [
            _remote(blk(chips[k], c), blk(chips[k], c), send.at[k], recv.at[k], peers[k]) for k in range(1, 4)
        ]
        passes = [_remote(blk(chips[k], c), blk(chips[k], c), fsend.at[k - 1], frecv.at[k - 1], sib) for k in range(1, 4)]
        passed = [_remote(blk(chips[k], 1 - c), blk(chips[k], 1 - c), fsend.at[k - 1], frecv.at[k - 1], sib) for k in range(1, 4)]
        mine = lambda: pltpu.make_async_copy(wsh_ref, blk(chips[0], c), loc.at[0])

        def to_hbm(unit):
            q = order_ref[unit]
            return pltpu.make_async_copy(wfull.at[q], win_ref.at[:, pl.ds(pl.multiple_of(q * pw, 128), pw)], osem.at[unit])

        a, b, e = n_early
        early_refs = (ci[:a], co[:b], cs[:e])
        late_refs = (ci[a:], co[b:], cs[e:])

        @pl.when(jnp.logical_and(u == 0, i == 0))
        def _():
            mine().start()
            mine().wait()
            for snd in sends[:3]:
                snd().start()
            early["start"](*early_refs)
            arrivals[0]().wait_recv()

        @pl.when(jnp.logical_and(u == 1, i == 0))
        def _():
            sends[3]().start()

        @pl.when(jnp.logical_and(u == 2, i == 0))
        def _():
            late["start"](*late_refs)

        for nxt in range(1, 4):

            @pl.when(jnp.logical_and(u == nxt - 1, i == n_i - 1))
            def _(nxt=nxt):
                passed[nxt - 1]().wait_recv()

        half = pl.multiple_of((i // n_rows) * ns, 128)
        proj_ref[...] = jnp.dot(
            n1_ref[...], wfull[order_ref[u], :, pl.ds(half, ns)], preferred_element_type=F32
        ).astype(BF16)

        for nxt in range(1, 4):

            @pl.when(jnp.logical_and(u == nxt - 1, i == n_i - 2))
            def _(nxt=nxt):
                arrivals[nxt]().wait_recv()
                passes[nxt - 1]().start()

        for unit in range(4):

            @pl.when(jnp.logical_and(u == unit, i == n_i - 1))
            def _(unit=unit):
                to_hbm(unit).start()

        @pl.when(jnp.logical_and(u == 3, i == n_i - 1))
        def _():
            for snd in sends + passes:
                snd().wait_send()
            for unit in range(4):
                to_hbm(unit).wait()
            comm["finish"](ci, co, cs)

    hbm = pl.BlockSpec(memory_space=pl.ANY)
    dma = pltpu.SemaphoreType.DMA
    res = pl.pallas_call(
        body,
        name="proj",
        grid_spec=pltpu.PrefetchScalarGridSpec(
            num_scalar_prefetch=1,
            grid=(4, n_i),
            in_specs=[pl.BlockSpec((tm, d), lambda u, i, order_ref: (i % n_rows, 0)), hbm] + [hbm] * n_ci,
            out_specs=[pl.BlockSpec((tm, ns), lambda u, i, order_ref: (i % n_rows, 2 * order_ref[u] + i // n_rows)), hbm]
            + [hbm] * n_co,
            scratch_shapes=[pltpu.VMEM((4, d, pw), BF16), dma((4,)), dma((4,)), dma((3,)), dma((3,)), dma((1,)), dma((4,))]
            + list(comm["sems"]),
        ),
        out_shape=[SDS((s, NDEV * ns), BF16), SDS((d, NDEV * ns), BF16)] + list(comm["outs"]),
        compiler_params=pltpu.CompilerParams(
            vmem_limit_bytes=V7X_PROJ_VMEM_LIMIT_BYTES, dimension_semantics=("arbitrary", "arbitrary")
        ),
    )(order, n1, w_shard, *comm["ins"])
    return res[0], res[1], list(res[2:])


def _peer(me, r):
    x, y, c = me
    return (1 - x if r & 4 else x, 1 - y if r & 2 else y, 1 - c if r & 1 else c)


def _remote(src, dst, send_sem, recv_sem, to):
    return lambda: pltpu.make_async_remote_copy(
        src_ref=src, dst_ref=dst, send_sem=send_sem, recv_sem=recv_sem, device_id=to, device_id_type=MESH
    )


def _run(pairs, locals_, start):
    if start:
        for cp in locals_:
            cp.start()
        for snd, _ in pairs:
            snd().start()
    else:
        for snd, arr in pairs:
            arr().wait_recv()
            snd().wait_send()
        for cp in locals_:
            cp.wait()


def _stage(ins, outs, alias, sems, build):
    return dict(
        ins=list(ins), outs=list(outs), alias=alias, sems=list(sems),
        start=lambda i, o, s: _run(*build(i, o, s), True),
        finish=lambda i, o, s: _run(*build(i, o, s), False),
    )


def _ag1(shards, kinds):
    n_t = len(shards)
    shapes = [tuple(sh.shape) for sh in shards]

    def build(srcs, dsts, sems):
        send, recv, loc = sems
        x, y, c = _place()
        me = (x, y, c)
        peers = [(x, y, 1 - c), (1 - x, y, c), (x, 1 - y, c), (1 - x, 1 - y, c)]
        reg = lambda t, dev: _region(dsts[t], kinds[t], _dev_index(dev), shapes[t])
        pairs = []
        for t in range(n_t):
            for k, peer in enumerate(peers):
                snd = _remote(srcs[t], reg(t, me), send.at[t, k], recv.at[t, k], peer)
                arr = _remote(reg(t, peer), reg(t, peer), send.at[t, k], recv.at[t, k], peer)
                pairs.append((snd, arr))
        mine = [pltpu.make_async_copy(srcs[t], reg(t, me), loc.at[t]) for t in range(n_t)]
        return pairs, mine

    outs = [SDS(_whole_shape(kinds[t], shapes[t]), shards[t].dtype) for t in range(n_t)]
    dma = pltpu.SemaphoreType.DMA
    return _stage(shards, outs, {}, [dma((n_t, 4)), dma((n_t, 4)), dma((n_t,))], build)


def _ag_direct(shards, kinds):
    n_t = len(shards)
    shapes = [tuple(sh.shape) for sh in shards]

    def build(srcs, dsts, sems):
        send, recv, loc = sems
        me = _place()
        reg = lambda t, dev: _region(dsts[t], kinds[t], _dev_index(dev), shapes[t])
        pairs = []
        for t in range(n_t):
            for r in range(1, NDEV):
                peer = _peer(me, r)
                snd = _remote(srcs[t], reg(t, me), send.at[t, r - 1], recv.at[t, r - 1], peer)
                arr = _remote(reg(t, peer), reg(t, peer), send.at[t, r - 1], recv.at[t, r - 1], peer)
                pairs.append((snd, arr))
        mine = [pltpu.make_async_copy(srcs[t], reg(t, me), loc.at[t]) for t in range(n_t)]
        return pairs, mine

    outs = [SDS(_whole_shape(kinds[t], shapes[t]), shards[t].dtype) for t in range(n_t)]
    dma = pltpu.SemaphoreType.DMA
    return _stage(shards, outs, {}, [dma((n_t, 7)), dma((n_t, 7)), dma((n_t,))], build)


def _ag2(wholes, kinds, shapes):
    n_t = len(wholes)

    def build(_, dsts, sems):
        send, recv = sems
        x, y, c = _place()
        sib = (x, y, 1 - c)
        chips = [(1 - x, y), (x, 1 - y), (1 - x, 1 - y)]
        reg = lambda t, dev: _region(dsts[t], kinds[t], _dev_index(dev), shapes[t])
        pairs = []
        for t in range(n_t):
            for j, chip in enumerate(chips):
                snd = _remote(reg(t, (*chip, c)), reg(t, (*chip, c)), send.at[t, j], recv.at[t, j], sib)
                arr = _remote(reg(t, (*chip, 1 - c)), reg(t, (*chip, 1 - c)), send.at[t, j], recv.at[t, j], sib)
                pairs.append((snd, arr))
        return pairs, []

    outs = [SDS(w.shape, w.dtype) for w in wholes]
    dma = pltpu.SemaphoreType.DMA
    return _stage(wholes, outs, {t: t for t in range(n_t)}, [dma((n_t, 3)), dma((n_t, 3))], build)


def _chip_of(q):
    return (q >> 1, q & 1)


def _rs1(wholes, kinds, shapes):
    n_t = len(wholes)

    def build(srcs, outs, sems):
        send, recv = sems
        x, y, c = _place()
        sib = (x, y, 1 - c)
        pairs = []
        for t in range(n_t):
            for q in range(4):
                theirs = _region(srcs[t], kinds[t], _dev_index((*_chip_of(q), 1 - c)), shapes[t])
                pairs.append((
                    _remote(theirs, outs[t].at[q], send.at[t, q], recv.at[t, q], sib),
                    _remote(outs[t].at[q], outs[t].at[q], send.at[t, q], recv.at[t, q], sib),
                ))
        return pairs, []

    slabs = [SDS((4,) + tuple(shapes[t]), wholes[t].dtype) for t in range(n_t)]
    dma = pltpu.SemaphoreType.DMA
    return _stage(wholes, slabs, {}, [dma((n_t, 4)), dma((n_t, 4))], build)


def _rs2(pair_sums):
    n_t = len(pair_sums)

    def build(srcs, lands, sems):
        send, recv, loc = sems
        x, y, c = _place()
        my_chip = 2 * x + y
        pairs, mine = [], []
        for t in range(n_t):
            for j, (px, py) in enumerate([(1 - x, y), (x, 1 - y), (1 - x, 1 - y)]):
                q = 2 * px + py
                pairs.append((
                    _remote(srcs[t].at[q], lands[t].at[my_chip], send.at[t, j], recv.at[t, j], (px, py, c)),
                    _remote(lands[t].at[q], lands[t].at[q], send.at[t, j], recv.at[t, j], (px, py, c)),
                ))
            mine.append(pltpu.make_async_copy(srcs[t].at[my_chip], lands[t].at[my_chip], loc.at[t]))
        return pairs, mine

    outs = [SDS(q.shape, q.dtype) for q in pair_sums]
    dma = pltpu.SemaphoreType.DMA
    return _stage(pair_sums, outs, {}, [dma((n_t, 3)), dma((n_t, 3)), dma((n_t,))], build)


def _pair_sum(name, whole, kind, got):
    _, rows, cols = got.shape
    tr = _tile(rows, 256)
    n_r = rows // tr
    core = lax.axis_index("c").astype(jnp.int32).reshape(1)

    def body(_, a, b, o):
        o[...] = (a[...].astype(F32) + b[...].astype(F32)).astype(BF16)

    if kind == "col":
        own = pl.BlockSpec((tr, cols), lambda q, i, c_ref: (i, 2 * q + c_ref[0]))
    elif kind == "row":
        own = pl.BlockSpec((tr, cols), lambda q, i, c_ref: ((2 * q + c_ref[0]) * n_r + i, 0))
    else:
        own = pl.BlockSpec((None, tr, cols), lambda q, i, c_ref: (2 * q + c_ref[0], i, 0))
    slab = pl.BlockSpec((None, tr, cols), lambda q, i, c_ref: (q, i, 0))
    return pl.pallas_call(
        body,
        name=name,
        grid_spec=pltpu.PrefetchScalarGridSpec(
            num_scalar_prefetch=1, grid=(4, n_r), in_specs=[own, slab], out_specs=slab
        ),
        out_shape=SDS(got.shape, BF16),
        compiler_params=_params(("parallel", "parallel")),
    )(core, whole, got)


def _packed_rows(parts, c_):
    offs, r0 = [], 0
    for p in parts:
        offs.append(r0)
        r0 += p.shape[1] * (p.shape[2] // c_)
    return offs, -(-r0 // 8) * 8


def _all_reduce_small(parts, c_):
    n_p = len(parts)
    offs, r_ = _packed_rows(parts, c_)

    def body(*refs):
        p_refs = refs[:n_p]
        land, total, src, send_sems, recv_sems = refs[n_p:]
        me = _place()
        my = _dev_index(me)
        src[...] = jnp.zeros((r_, c_), F32)
        for p_ref, r0 in zip(p_refs, offs):
            v = jnp.sum(p_ref[...], axis=0)
            k = v.shape[1] // c_
            for ri in range(v.shape[0]):
                for q in range(k):
                    src[r0 + ri * k + q : r0 + ri * k + q + 1, :] = v[ri : ri + 1, q * c_ : (q + 1) * c_]
        land[my] = src[...]

        def copy(r):
            peer = _peer(me, r)
            return pltpu.make_async_remote_copy(
                src_ref=src,
                dst_ref=land.at[my],
                send_sem=send_sems.at[r - 1],
                recv_sem=recv_sems.at[r - 1],
                device_id=peer,
                device_id_type=MESH,
            )

        def arrival(r):
            peer = _peer(me, r)
            slab = land.at[_dev_index(peer)]
            return pltpu.make_async_remote_copy(
                src_ref=slab,
                dst_ref=slab,
                send_sem=send_sems.at[r - 1],
                recv_sem=recv_sems.at[r - 1],
                device_id=peer,
                device_id_type=MESH,
            )

        sends = [copy(r) for r in range(1, NDEV)]
        for cp in sends:
            cp.start()
        for r in range(1, NDEV):
            arrival(r).wait_recv()
        for cp in sends:
            cp.wait_send()
        acc = land[0]
        for d in range(1, NDEV):
            acc = acc + land[d]
        total[...] = acc

    vmem = pl.BlockSpec(memory_space=pltpu.VMEM)
    return pl.pallas_call(
        body,
        name="all_reduce_small",
        in_specs=[vmem] * n_p,
        out_specs=[vmem, vmem],
        out_shape=[SDS((NDEV, r_, c_), F32), SDS((r_, c_), F32)],
        scratch_shapes=[pltpu.VMEM((r_, c_), F32), pltpu.SemaphoreType.DMA((7,)), pltpu.SemaphoreType.DMA((7,))],
        compiler_params=_params(),
    )(*parts)[1]


def _adamw_math(g, w, m, v):
    m2 = ADAM_B1 * m + (1.0 - ADAM_B1) * g
    v2 = ADAM_B2 * v + (1.0 - ADAM_B2) * (g * g)
    m_hat = m2 / (1.0 - ADAM_B1**ADAM_STEP)
    v_hat = v2 / (1.0 - ADAM_B2**ADAM_STEP)
    delta = -ADAM_LR * (m_hat / (jnp.sqrt(v_hat) + ADAM_EPS) + ADAM_WD * w)
    return delta, m2, v2


def _adamw_big(name, land, w, m, v):
    rows, cols = w.shape
    tr = _tile(rows, 256)
    n_slab = land.shape[0]

    def body(l_ref, w_ref, m_ref, v_ref, g_o, d_o, m_o, v_o):
        g = l_ref[0].astype(F32)
        for d in range(1, n_slab):
            g = g + l_ref[d].astype(F32)
        delta, m2, v2 = _adamw_math(g, w_ref[...], m_ref[...], v_ref[...])
        g_o[...] = g
        d_o[...] = delta
        m_o[...] = m2
        v_o[...] = v2

    blk = pl.BlockSpec((tr, cols), lambda i: (i, 0))
    return pl.pallas_call(
        body,
        name=name,
        grid=(rows // tr,),
        in_specs=[pl.BlockSpec((n_slab, tr, cols), lambda i: (0, i, 0)), blk, blk, blk],
        out_specs=[blk] * 4,
        out_shape=[SDS((rows, cols), F32)] * 4,
        compiler_params=_params(("parallel",)),
    )(land, w, m, v)


def _adamw_small(total, items):
    c_ = total.shape[1]
    n_it = len(items)

    def body(*refs):
        t_ref = refs[0]
        ins, outs = refs[1 : 1 + 3 * n_it], refs[1 + 3 * n_it :]
        my = _dev_index(_place())
        for q, (row0, taps, w, _, _) in enumerate(items):
            w_ref, m_ref, v_ref = ins[3 * q : 3 * q + 3]
            if taps:
                lanes = w.shape[-1]
                g = t_ref[pl.ds(row0, taps), pl.ds(pl.multiple_of(my * lanes, 128), lanes)][None]
            else:
                k = w.shape[-1] // c_
                g = jnp.concatenate([t_ref[row0 + j : row0 + j + 1, :] for j in range(k)], axis=1)
            delta, m2, v2 = _adamw_math(g, w_ref[...], m_ref[...], v_ref[...])
            for o_ref, val in zip(outs[4 * q : 4 * q + 4], (g, delta, m2, v2)):
                o_ref[...] = val

    vmem = pl.BlockSpec(memory_space=pltpu.VMEM)
    flat = [a for (_, _, w, m, v) in items for a in (w, m, v)]
    res = pl.pallas_call(
        body,
        name="adamw_small",
        in_specs=[vmem] * (1 + 3 * n_it),
        out_specs=[vmem] * (4 * n_it),
        out_shape=[SDS(w.shape, F32) for (_, _, w, _, _) in items for _ in range(4)],
    )(total, *flat)
    return [list(res[4 * q : 4 * q + 4]) for q in range(n_it)]


def kernel(x, p, g_mix, w_in, conv_a_w, w_out_a, b_glu, conf_dw_w, conf_dw_b, conf_ln_g, conf_ln_b, w_pw_b, b_pw_b, w_o, g_ffn, w_gate, w_up, w_down, g_ple, w_ple_gate, w_ple_proj, g_final, loss_target, m_g_mix, m_w_in, m_conv_a_w, m_w_out_a, m_b_glu, m_conf_dw_w, m_conf_dw_b, m_conf_ln_g, m_conf_ln_b, m_w_pw_b, m_b_pw_b, m_w_o, m_g_ffn, m_w_gate, m_w_up, m_w_down, m_g_ple, m_w_ple_gate, m_w_ple_proj, m_g_final, v_g_mix, v_w_in, v_conv_a_w, v_w_out_a, v_b_glu, v_conf_dw_w, v_conf_dw_b, v_conf_ln_g, v_conf_ln_b, v_w_pw_b, v_b_pw_b, v_w_o, v_g_ffn, v_w_gate, v_w_up, v_w_down, v_g_ple, v_w_ple_gate, v_w_ple_proj, v_g_final):
    s, d = x.shape[1], x.shape[2]
    c = conf_ln_g.shape[-1]
    pdim = w_ple_proj.shape[1]
    fs = w_gate.shape[-1]
    nin = NDEV * w_in.shape[-1]
    assert d == 2 * c and nin == 5 * c + 2 * d, (d, c, nin)
    x2, p2, tgt = x[0], p[0, 0], loss_target[0]
    gfin = g_final.reshape(1, d)

    kpa, kpb = 8, HALO_B
    wa_sh = jnp.pad(conv_a_w[0], ((0, kpa - CONV_A_K), (0, 0)))
    wd_sh = jnp.pad(conf_dw_w[0], ((0, kpb - CONF_K), (0, 0)))
    kind_of = dict(w_in="col", w_out_a="col", w_pw_b="col", w_ple_proj="col", w_o="row", w_ple_gate="row",
                   w_gate="blk", w_up="blk", w_down="blk")
    weight = dict(w_in=w_in, w_out_a=w_out_a, w_pw_b=w_pw_b, w_ple_proj=w_ple_proj, w_o=w_o, w_ple_gate=w_ple_gate,
                  w_gate=w_gate, w_up=w_up, w_down=w_down)
    shard_of = {nm: tuple(w.shape[1:]) for nm, w in weight.items()}
    bf16_shard = lambda nm: weight[nm][0].astype(BF16)
    kinds_ = lambda grp: [kind_of[nm] for nm in grp]
    shapes_ = lambda grp: [shard_of[nm] for nm in grp]
    first_stage = lambda grp: _ag1([bf16_shard(nm) for nm in grp], kinds_(grp))
    second_stage = lambda grp, parts: _ag2(parts, kinds_(grp), shapes_(grp))
    grp_1 = ["w_out_a", "w_pw_b"]
    grp_2 = ["w_o", "w_gate"]
    grp_3 = ["w_up"]
    grp_4 = ["w_down"]
    grp_5 = ["w_ple_gate", "w_ple_proj"]

    tm = _tile(s, 1024)
    tn = _tile(d, 1024)
    assert (5 * c) % tn == 0 and d % tn == 0 and c % tn == 0
    ga_blk, gb_blk = (5 * c) // tn, (5 * c + d) // tn
    ij = lambda i, j, k: (i, j)
    row_i = lambda i, j, k: (i, 0)

    n1 = _rms_fwd("rms1", x2, g_mix)
    proj, win, got = _proj_gather(
        n1, bf16_shard("w_in"),
        _join(_ag_direct([wa_sh, wd_sh], ["col", "col"]), first_stage(grp_1)), first_stage(grp_2),
    )
    (wa, wd), part_12 = got[:2], got[2:]
    grp_12 = grp_1 + grp_2
    ya_in = _mix_a_fwd(proj, wa, s, c)
    (v_act, u_act, cv), got = _mix_b_fwd(
        proj, b_glu, wd, conf_dw_b, conf_ln_g, conf_ln_b, s, c,
        comm=_join(second_stage(grp_12, part_12), first_stage(grp_3)),
    )
    (wouta, wpw, wo, wg), part_3 = got[: len(grp_12)], got[len(grp_12) :]

    def ep_merge(accs, ex, os_):
        sa = _sigmoid(ex[0][...].astype(F32))
        sb = _sigmoid(ex[1][...].astype(F32))
        ya = accs[0]
        yb = accs[1] + ex[2][...]
        os_[0][...] = (sa * ya + sb * yb).astype(BF16)
        os_[1][...] = ya.astype(BF16)
        os_[2][...] = yb.astype(BF16)

    gate_a_spec = pl.BlockSpec((tm, tn), lambda i, j, k: (i, ga_blk + j))
    gate_b_spec = pl.BlockSpec((tm, tn), lambda i, j, k: (i, gb_blk + j))
    out_sd = (SDS((s, d), BF16), pl.BlockSpec((tm, tn), ij))
    (m_act, ya, yb), got = _fmm(
        "merge", (s // tm, d // tn, 1),
        [(ya_in, pl.BlockSpec((tm, c), row_i)), (wouta, pl.BlockSpec((c, tn), lambda i, j, k: (0, j))),
         (v_act, pl.BlockSpec((tm, c), row_i)), (wpw, pl.BlockSpec((c, tn), lambda i, j, k: (0, j)))],
        [(0, 1, NN, 0, None), (2, 3, NN, 1, None)], [(tm, tn), (tm, tn)],
        [(proj, gate_a_spec), (proj, gate_b_spec), (b_pw_b, pl.BlockSpec((1, tn), lambda i, j, k: (0, j)))],
        [out_sd, out_sd, out_sd], ep_merge, csplit=EPILOGUE_CHUNK,
        comm=_join(second_stage(grp_3, part_3), first_stage(grp_4)),
    )
    (wu,), part_4 = got[: len(grp_3)], got[len(grp_3) :]

    def ep_residual(accs, ex, os_):
        os_[0][...] = (accs[0] + ex[0][...].astype(F32)).astype(BF16)

    (h1,), got = _fmm(
        "w_o", (s // tm, d // tn, 1),
        [(m_act, pl.BlockSpec((tm, d), row_i)), (wo, pl.BlockSpec((d, tn), lambda i, j, k: (0, j)))],
        [(0, 1, NN, 0, None)], [(tm, tn)], [(x2, pl.BlockSpec((tm, tn), ij))],
        [(SDS((s, d), BF16), pl.BlockSpec((tm, tn), ij))], ep_residual, csplit=EPILOGUE_CHUNK,
        comm=_join(second_stage(grp_4, part_4), first_stage(grp_5)),
    )
    (wdn,), part_5 = got[: len(grp_4)], got[len(grp_4) :]
    n2 = _rms_fwd("rms2", h1, g_ffn)

    hidden = NDEV * fs
    wg_p = _slabs_to_plain("w_gate_plain", wg)
    wu_p = _slabs_to_plain("w_up_plain", wu)
    wdn_p = wdn.reshape(hidden, d)
    tf = _tile(hidden, 512)
    tkf = _tile(hidden, 2816)

    def ep_gateup(accs, ex, os_):
        g, u = accs
        sg = _sigmoid(g)
        silu = g * sg
        os_[0][...] = (u * sg * (1.0 + g * (1.0 - sg))).astype(BF16)
        os_[1][...] = silu.astype(BF16)
        os_[2][...] = (silu * u).astype(BF16)

    ff_sd = (SDS((s, hidden), BF16), pl.BlockSpec((tm, tf), ij))
    w_col_blk = pl.BlockSpec((d, tf), lambda i, j, k: (0, j))
    (df_dg, df_du, f_act), (wpg, wpp) = _fmm(
        "gate_up", (s // tm, hidden // tf, 1),
        [(n2, pl.BlockSpec((tm, d), row_i)), (wg_p, w_col_blk), (wu_p, w_col_blk)],
        [(0, 1, NN, 0, None), (0, 2, NN, 1, None)], [(tm, tf), (tm, tf)], [],
        [ff_sd, ff_sd, ff_sd], ep_gateup,
        comm=second_stage(grp_5, part_5),
    )
    (h2,), _ = _fmm(
        "down", (s // tm, d // tn, hidden // tkf),
        [(f_act, pl.BlockSpec((tm, tkf), lambda i, j, k: (i, k))),
         (wdn_p, pl.BlockSpec((tkf, tn), lambda i, j, k: (k, j)))],
        [(0, 1, NN, 0, None)], [(tm, tn)], [(h1, pl.BlockSpec((tm, tn), ij))],
        [(SDS((s, d), BF16), pl.BlockSpec((tm, tn), ij))], ep_residual,
    )
    n3 = _rms_fwd("rms3", h2, g_ple)

    tr = _tile(s, 256)
    n_r = s // tr
    rows = lambda i, j, k: (i, 0)
    whole = lambda i, j, k: (0, 0)
    part_spec = lambda nrow: pl.BlockSpec((None, nrow, d), lambda i, j, k: (i, 0, 0))

    def ep_ple(accs, ex, os_):
        h2_, t_, gf = ex[0][...].astype(F32), ex[1][...], ex[2][...]
        ple = accs[0]
        s3 = _sigmoid(accs[1])
        h3 = h2_ + s3 * ple
        r = lax.rsqrt(jnp.mean(h3 * h3, axis=-1, keepdims=True) + EPS)
        hn = h3 * r
        e = hn * gf - t_
        loss = 0.5 * jnp.sum(jnp.mean(e * e, axis=-1, keepdims=True), axis=0, keepdims=True)
        dy = e * (1.0 / d)
        dn = dy * gf
        dh3 = r * (dn - hn * jnp.mean(dn * hn, axis=-1, keepdims=True))
        os_[0][...] = dh3.astype(BF16)
        os_[1][...] = (dh3 * s3).astype(BF16)
        os_[2][...] = (dh3 * ple * s3 * (1.0 - s3)).astype(BF16)
        os_[3][0:1, :] = jnp.sum(dy * hn, axis=0, keepdims=True)
        os_[3][1:2, :] = jnp.broadcast_to(loss, (1, d))

    (dh3, d_ple, d_g3, part_fin), _ = _fmm(
        "ple_loss", (n_r, 1, 1),
        [(p2, pl.BlockSpec((tr, pdim), rows)), (wpp, pl.BlockSpec((pdim, d), whole)),
         (n3, pl.BlockSpec((tr, d), rows)), (wpg, pl.BlockSpec((d, d), whole))],
        [(0, 1, NN, 0, None), (2, 3, NN, 1, None)], [(tr, d), (tr, d)],
        [(h2, pl.BlockSpec((tr, d), rows)), (tgt, pl.BlockSpec((tr, d), rows)), (gfin, pl.BlockSpec((1, d), whole))],
        [(SDS((s, d), BF16), pl.BlockSpec((tr, d), rows)), (SDS((s, d), BF16), pl.BlockSpec((tr, d), rows)),
         (SDS((s, d), BF16), pl.BlockSpec((tr, d), rows)), (SDS((n_r, 2, d), F32), part_spec(2))],
        ep_ple,
    )

    g_wpp = _mm_tn("d_w_ple_proj", p2, d_ple)
    g_wpg = _mm_tn("d_w_ple_gate", n3, d_g3)

    def ep_norm_bwd(accs, ex, os_):
        dh, dg = _rms_bwd(accs[0], ex[0][...].astype(F32), ex[2][...])
        os_[0][...] = (ex[1][...].astype(F32) + dh).astype(BF16)
        os_[1][...] = dg

    norm_outs = lambda t: [(SDS((s, d), BF16), pl.BlockSpec((t, d), rows)), (SDS((s // t, 1, d), F32), part_spec(1))]
    def exchange1(names, wholes):
        return _rs1(wholes, kinds_(names), shapes_(names))

    def pair_sums(names, wholes, got):
        return [_pair_sum("pair_sum_" + nm, wholes[t], kind_of[nm], got[t]) for t, nm in enumerate(names)]

    lands = {}
    grp1 = ["w_ple_proj", "w_ple_gate"]
    (dh2b, part_ple), got = _fmm(
        "d_n3", (n_r, 1, 1),
        [(d_g3, pl.BlockSpec((tr, d), rows)), (wpg, pl.BlockSpec((d, d), whole))],
        [(0, 1, NT, 0, None)], [(tr, d)],
        [(h2, pl.BlockSpec((tr, d), rows)), (dh3, pl.BlockSpec((tr, d), rows)), (g_ple, pl.BlockSpec((1, d), whole))],
        norm_outs(tr), ep_norm_bwd,
        comm=exchange1(grp1, [g_wpp, g_wpg]),
    )
    sums1 = pair_sums(grp1, [g_wpp, g_wpg], got)

    def ep_ddown(accs, ex, os_):
        df = accs[0]
        os_[0][...] = (df * ex[0][...].astype(F32)).astype(BF16)
        os_[1][...] = (df * ex[1][...].astype(F32)).astype(BF16)

    ff_in = pl.BlockSpec((tm, tf), ij)
    (d_g, d_u), got = _fmm(
        "d_down", (s // tm, hidden // tf, 1),
        [(dh2b, pl.BlockSpec((tm, d), row_i)), (wdn_p, pl.BlockSpec((tf, d), lambda i, j, k: (j, 0)))],
        [(0, 1, NT, 0, None)], [(tm, tf)], [(df_dg, ff_in), (df_du, ff_in)],
        [ff_sd, ff_sd], ep_ddown, csplit=EPILOGUE_CHUNK,
        comm=_rs2(sums1),
    )
    lands.update(zip(grp1, got))
    tk = _tile(s, 1024)
    g_wdn = _mm_tn("d_w_down", f_act, dh2b, tm=1408, tn=2048).reshape(NDEV, fs, d)

    def ep_two_bf16(accs, ex, os_):
        os_[0][...] = accs[0].astype(BF16)
        os_[1][...] = accs[1].astype(BF16)

    ff_k = pl.BlockSpec((tk, tf), lambda i, j, k: (k, j))
    wcol_sd = (SDS((d, hidden), BF16), pl.BlockSpec((d, tf), lambda i, j, k: (0, j)))
    grp2 = ["w_down"]
    (g_wg_p, g_wu_p), got = _fmm(
        "d_w_gate_up", (1, hidden // tf, s // tk),
        [(n2, pl.BlockSpec((tk, d), lambda i, j, k: (k, 0))), (d_g, ff_k), (d_u, ff_k)],
        [(0, 1, TN, 0, None), (0, 2, TN, 1, None)], [(d, tf), (d, tf)], [],
        [wcol_sd, wcol_sd], ep_two_bf16,
        comm=exchange1(grp2, [g_wdn]),
    )
    g_wg = _plain_to_slabs("d_w_gate_slabs", g_wg_p, NDEV)
    g_wu = _plain_to_slabs("d_w_up_slabs", g_wu_p, NDEV)
    sums2 = pair_sums(grp2, [g_wdn], got)
    grp3 = ["w_gate", "w_up"]
    th = _tile(s // 2, 1024)
    ff_a = pl.BlockSpec((th, tf), lambda i, j, k: (i, k))
    w_k = pl.BlockSpec((d, tf), lambda i, j, k: (0, k))
    (d_n2,), got = _fmm(
        "d_n2", (s // th, 1, hidden // tf),
        [(d_g, ff_a), (wg_p, w_k), (d_u, ff_a), (wu_p, w_k)],
        [(0, 1, NT, 0, None), (2, 3, NT, 0, None)], [(th, d)], [],
        [(SDS((s, d), BF16), pl.BlockSpec((th, d), rows))], _ep_bf16,
        comm=_join(_rs2(sums2), exchange1(grp3, [g_wg, g_wu])),
    )
    lands.update(zip(grp2, got[:1]))
    sums3 = pair_sums(grp3, [g_wg, g_wu], got[1:])
    dh1b, part_ffn = _norm_bwd("d_h1", d_n2, h1, dh2b, g_ffn, BF16)
    g_wo = _mm_tn("d_w_o", m_act, dh1b)

    def ep_dm(accs, ex, os_):
        ya_, yb_ = ex[0][...].astype(F32), ex[1][...].astype(F32)
        sa = _sigmoid(ex[2][...].astype(F32))
        sb = _sigmoid(ex[3][...].astype(F32))
        dm = accs[0]
        d_yb = dm * sb
        os_[0][...] = (dm * sa).astype(BF16)
        os_[1][...] = d_yb.astype(BF16)
        os_[2][...] = (dm * ya_ * sa * (1.0 - sa)).astype(BF16)
        os_[3][...] = (dm * yb_ * sb * (1.0 - sb)).astype(BF16)
        os_[4][...] = jnp.sum(d_yb, axis=0, keepdims=True)

    tile_ij = pl.BlockSpec((tm, tn), ij)
    grp4 = ["w_o"]
    (d_ya, d_yb, d_ga, d_gb, part_bpw), got = _fmm(
        "d_merge", (s // tm, d // tn, 1),
        [(dh1b, pl.BlockSpec((tm, d), row_i)), (wo, pl.BlockSpec((tn, d), lambda i, j, k: (j, 0)))],
        [(0, 1, NT, 0, None)], [(tm, tn)],
        [(ya, tile_ij), (yb, tile_ij), (proj, gate_a_spec), (proj, gate_b_spec)],
        [out_sd, out_sd, out_sd, out_sd,
         (SDS((s // tm, 1, d), F32), pl.BlockSpec((None, 1, tn), lambda i, j, k: (i, 0, j)))],
        ep_dm, csplit=EPILOGUE_CHUNK,
        comm=exchange1(grp4, [g_wo]),
    )
    sums4 = pair_sums(grp4, [g_wo], got)
    g_wouta = _mm_tn("d_w_out_a", ya_in, d_ya)
    g_wpw = _mm_tn("d_w_pw_b", v_act, d_yb)
    grp5 = ["w_out_a", "w_pw_b"]
    d_ya_in, got = _mm_nt("d_ya_in", d_ya, wouta, comm=exchange1(grp5, [g_wouta, g_wpw]))
    sums5 = pair_sums(grp5, [g_wouta, g_wpw], got)
    d_v, _ = _mm_nt("d_v", d_yb, wpw)
    d_cv, part_ln = _mix_b_bwd1(d_v, cv, conf_ln_g, conf_ln_b, s, c)
    (d_b, part_wd, part_bglu), got = _mix_b_bwd2(d_cv, u_act, proj, b_glu, wd, s, c, comm=_rs2(sums3))
    lands.update(zip(grp3, got))
    d_a, part_wa = _mix_a_bwd(d_ya_in, proj, wa, s, c)

    nb = nin // c
    gblk = d // c
    lo = [0, 3, 5, 5 + gblk]
    hi = [3, 5, 5 + gblk, 5 + 2 * gblk]
    pieces = [d_a, d_b, d_ga, d_gb]

    def active(q, ax):
        return lambda ids: jnp.logical_and(ids[ax] >= lo[q], ids[ax] < hi[q])

    def piece_spec(q, rows_, ax, row0=0):
        def index(i, j, k):
            ids = (i, j, k)
            col = jnp.clip(ids[ax] - lo[q], 0, hi[q] - lo[q] - 1)
            row = i + row0 if ax == 2 else jnp.where(active(q, ax)(ids), k, 0)
            return (row, col)

        return pl.BlockSpec((rows_, c), index)

    tkw = _tile(s, 1024)
    (g_win,), got = _fmm(
        "d_w_in", (1, nb, s // tkw),
        [(n1, pl.BlockSpec((tkw, d), lambda i, j, k: (k, 0)))]
        + [(pieces[q], piece_spec(q, tkw, 1)) for q in range(4)],
        [(0, 1 + q, TN, 0, active(q, 1)) for q in range(4)], [(d, c)], [],
        [(SDS((d, nin), BF16), pl.BlockSpec((d, c), lambda i, j, k: (0, j)))], _ep_bf16,
        comm=_rs2(sums4 + sums5),
    )
    lands.update(zip(grp4 + grp5, got))

    grp6 = ["w_in"]
    n_half = max(1, 3 * (s // th) // 8)

    def d_n1_rows(name, row0, n_tiles, comm, into):
        return _fmm(
            name, (n_tiles, 1, nb),
            [(pieces[q], piece_spec(q, th, 2, row0)) for q in range(4)]
            + [(win, pl.BlockSpec((d, c), lambda i, j, k: (0, k)))],
            [(q, 4, NT, 0, active(q, 2)) for q in range(4)], [(th, d)], [],
            [(SDS((s, d), BF16), pl.BlockSpec((th, d), lambda i, j, k: (i + row0, 0)))], _ep_bf16,
            comm=comm, into=into,
        )

    (d_n1,), got = d_n1_rows("d_n1_a", 0, n_half, exchange1(grp6, [g_win]), None)
    (d_n1,), got = d_n1_rows("d_n1_b", n_half, s // th - n_half, _rs2(pair_sums(grp6, [g_win], got)), d_n1)
    lands.update(zip(grp6, got))
    dx, part_mix = _norm_bwd("d_x", d_n1, x2, dh1b, g_mix, F32)

    small_parts = [part_mix, part_bglu, part_ln, part_bpw, part_ffn, part_ple, part_fin, part_wa, part_wd]
    (o_mix, o_bglu, o_ln, o_bpw, o_ffn, o_ple, o_fin, o_wa, o_wd), _ = _packed_rows(small_parts, c)
    total = _all_reduce_small(small_parts, c)
    loss = total[o_fin + d // c, 0]

    big_m = dict(w_in=m_w_in, w_out_a=m_w_out_a, w_pw_b=m_w_pw_b, w_ple_proj=m_w_ple_proj, w_o=m_w_o,
                 w_ple_gate=m_w_ple_gate, w_gate=m_w_gate, w_up=m_w_up, w_down=m_w_down)
    big_v = dict(w_in=v_w_in, w_out_a=v_w_out_a, w_pw_b=v_w_pw_b, w_ple_proj=v_w_ple_proj, w_o=v_w_o,
                 w_ple_gate=v_w_ple_gate, w_gate=v_w_gate, w_up=v_w_up, w_down=v_w_down)
    big_out = {}
    for nm in weight:
        res = _adamw_big("adamw_" + nm, lands[nm], weight[nm][0], big_m[nm][0], big_v[nm][0])
        big_out[nm] = [r[None] for r in res]

    row = lambda a: a.reshape(1, d)
    small = dict(
        g_mix=(o_mix, 0, g_mix, m_g_mix, v_g_mix),
        conv_a_w=(o_wa, CONV_A_K, conv_a_w, m_conv_a_w, v_conv_a_w),
        b_glu=(o_bglu, 0, b_glu, m_b_glu, v_b_glu),
        conf_dw_w=(o_wd, CONF_K, conf_dw_w, m_conf_dw_w, v_conf_dw_w),
        conf_dw_b=(o_ln + 2, 0, conf_dw_b, m_conf_dw_b, v_conf_dw_b),
        conf_ln_g=(o_ln, 0, conf_ln_g, m_conf_ln_g, v_conf_ln_g),
        conf_ln_b=(o_ln + 1, 0, conf_ln_b, m_conf_ln_b, v_conf_ln_b),
        b_pw_b=(o_bpw, 0, b_pw_b, m_b_pw_b, v_b_pw_b),
        g_ffn=(o_ffn, 0, g_ffn, m_g_ffn, v_g_ffn),
        g_ple=(o_ple, 0, g_ple, m_g_ple, v_g_ple),
        g_final=(o_fin, 0, row(g_final), row(m_g_final), row(v_g_final)),
    )
    small_out = dict(zip(small, _adamw_small(total, list(small.values()))))
    small_out["g_final"] = [a.reshape(d) for a in small_out["g_final"]]

    order = ["g_mix", "w_in", "conv_a_w", "w_out_a", "b_glu", "conf_dw_w", "conf_dw_b", "conf_ln_g", "conf_ln_b", "w_pw_b", "b_pw_b", "w_o", "g_ffn", "w_gate", "w_up", "w_down", "g_ple", "w_ple_gate", "w_ple_proj", "g_final"]
    allo = {**big_out, **small_out}
    outs = [loss, dx[None]]
    for q in range(4):
        outs += [allo[nm][q] for nm in order]
    return tuple(outs)
```

```python
import jax
import jax.numpy as jnp
from jax import lax
from jax.experimental import pallas as pl
from jax.experimental.pallas import tpu as pltpu

F32, BF16 = jnp.float32, jnp.bfloat16
EPS, LN_EPS = 1e-6, 1e-5
ADAM_LR, ADAM_B1, ADAM_B2, ADAM_EPS, ADAM_WD, ADAM_STEP = 0.001, 0.9, 0.999, 1e-08, 0.01, 10
CONV_A_K, CONF_K = 3, 31
NDEV = 8
NN = (((1,), (0,)), ((), ()))
NT = (((1,), (1,)), ((), ()))
TN = (((0,), (0,)), ((), ()))
V7X_VMEM_LIMIT_BYTES = 56 * 1024 * 1024
MESH = pl.DeviceIdType.MESH
SDS = jax.ShapeDtypeStruct
HALO_A, HALO_B = 16, 32
EPILOGUE_CHUNK = 256
CONV_ROWS = 32
CONF_ROWS = 16


def _tile(n, pref):
    t = min(n, pref)
    while n % t:
        t -= 8
    return t


def _sigmoid(x):
    return jax.nn.sigmoid(x)


def _params(sem=None):
    return pltpu.CompilerParams(vmem_limit_bytes=V7X_VMEM_LIMIT_BYTES, dimension_semantics=sem)


def _edge(grid, last):
    cond = None
    for ax, n in enumerate(grid):
        here = pl.program_id(ax) == (n - 1 if last else 0)
        cond = here if cond is None else jnp.logical_and(cond, here)
    return cond


def _join(*comms):
    ins, outs, alias, sems, spans = [], [], {}, [], []
    for cm in comms:
        spans.append((len(ins), len(outs), len(sems)))
        for i, o in cm["alias"].items():
            alias[len(ins) + i] = len(outs) + o
        ins += cm["ins"]
        outs += cm["outs"]
        sems += cm["sems"]

    def run(which):
        def f(i_refs, o_refs, s_refs):
            for cm, (a, b, c_) in zip(comms, spans):
                cm[which](
                    i_refs[a : a + len(cm["ins"])], o_refs[b : b + len(cm["outs"])], s_refs[c_ : c_ + len(cm["sems"])]
                )

        return f

    return dict(ins=ins, outs=outs, alias=alias, sems=sems, start=run("start"), finish=run("finish"))


def _call(body, name, grid, in_specs, args, out_specs, out_shape, scratch=(), sem=None, comm=None, alias=None):
    n_in, n_out, n_s = len(args), len(out_shape), len(scratch)
    alias = dict(alias or {})
    if comm is None:
        res = pl.pallas_call(
            body, name=name, grid=grid, in_specs=list(in_specs), out_specs=list(out_specs), out_shape=list(out_shape),
            scratch_shapes=list(scratch), input_output_aliases=alias, compiler_params=_params(sem),
        )(*args)
        return list(res), []
    n_ci, n_co = len(comm["ins"]), len(comm["outs"])

    def wrapped(*refs):
        ins = refs[:n_in]
        ci = refs[n_in : n_in + n_ci]
        o0 = n_in + n_ci
        outs = refs[o0 : o0 + n_out]
        co = refs[o0 + n_out : o0 + n_out + n_co]
        s0 = o0 + n_out + n_co
        sc = refs[s0 : s0 + n_s]
        cs = refs[s0 + n_s :]
        pl.when(_edge(grid, False))(lambda: comm["start"](ci, co, cs))
        body(*ins, *outs, *sc)
        pl.when(_edge(grid, True))(lambda: comm["finish"](ci, co, cs))

    hbm = pl.BlockSpec(memory_space=pl.ANY)
    res = pl.pallas_call(
        wrapped,
        name=name,
        grid=grid,
        in_specs=list(in_specs) + [hbm] * n_ci,
        out_specs=list(out_specs) + [hbm] * n_co,
        out_shape=list(out_shape) + list(comm["outs"]),
        scratch_shapes=list(scratch) + list(comm["sems"]),
        input_output_aliases={**alias, **{n_in + i: n_out + o for i, o in comm["alias"].items()}},
        compiler_params=_params(("arbitrary",) * len(grid)),
    )(*args, *comm["ins"])
    return list(res[:n_out]), list(res[n_out:])


def _col_chunks(n, pref):
    out, c0 = [], 0
    while c0 < n:
        w = min(pref, n - c0)
        out.append((c0, w))
        c0 += w
    return out


def _fmm(name, grid, operands, terms, acc_shapes, extras, outs, epilogue, comm=None, csplit=None, into=None):
    n_p, n_e, n_o, n_a = len(operands), len(extras), len(outs), len(acc_shapes)
    nk = grid[-1]
    kax = len(grid) - 1
    simple = nk == 1 and all(t[4] is None for t in terms)
    alias = None
    if into is not None:
        extras = list(extras) + [(into, pl.BlockSpec(memory_space=pl.ANY))]
        alias = {n_p + n_e: 0}
        n_e += 1
    if csplit is not None:
        assert simple and into is None and all(t[2] in (NN, NT) and (len(t) <= 5 or not t[5]) for t in terms)
        tn_ = acc_shapes[0][1]
        chunks = _col_chunks(tn_, csplit)

    def dot(a, b, dims):
        if a.dtype != BF16:
            a = a.astype(BF16)
        if b.dtype != BF16:
            b = b.astype(BF16)
        return lax.dot_general(a, b, dims, preferred_element_type=F32)

    def value(refs, term):
        slabs = term[5] if len(term) > 5 else 0
        if not slabs:
            return dot(refs[term[0]][...], refs[term[1]][...], term[2])
        tot = None
        for sl in range(slabs):
            d = dot(refs[term[0]][sl], refs[term[1]][sl], term[2])
            tot = d if tot is None else tot + d
        return tot

    def always(refs):
        parts = [None] * n_a
        for term in terms:
            if term[4] is None:
                d = value(refs, term)
                parts[term[3]] = d if parts[term[3]] is None else parts[term[3]] + d
        return parts

    def chunked(refs, ex, os_, accs):
        cols = lambda ref, c0, w: ref.at[:, pl.ds(c0, w)] if ref.shape[-1] == tn_ else ref

        def dots(k):
            c0, w = chunks[k]
            parts = [None] * n_a
            for term in terms:
                b_ref = refs[term[1]]
                b = b_ref[:, pl.ds(c0, w)] if term[2] == NN else b_ref[pl.ds(c0, w), :]
                d = dot(refs[term[0]][...], b, term[2])
                parts[term[3]] = d if parts[term[3]] is None else parts[term[3]] + d
            for ai in range(n_a):
                accs[ai][k % 2, :, pl.ds(0, w)] = parts[ai]

        def finish(k):
            c0, w = chunks[k]
            vals = [accs[ai][k % 2, :, pl.ds(0, w)] for ai in range(n_a)]
            epilogue(vals, [cols(e, c0, w) for e in ex], [cols(o, c0, w) for o in os_])

        dots(0)
        for k in range(1, len(chunks)):
            dots(k)
            finish(k - 1)
        finish(len(chunks) - 1)

    def body(*refs):
        ex = refs[n_p : n_p + n_e]
        os_ = refs[n_p + n_e : n_p + n_e + n_o]
        accs = refs[n_p + n_e + n_o :]
        if simple and csplit is not None:
            chunked(refs, ex, os_, accs)
            return
        if simple:
            epilogue(always(refs), ex, os_)
            return
        ids = [pl.program_id(ax) for ax in range(len(grid))]
        k = ids[kax]

        @pl.when(k == 0)
        def _():
            for acc in accs:
                acc[...] = jnp.zeros(acc.shape, F32)

        for ai, part in enumerate(always(refs)):
            if part is not None:
                accs[ai][...] += part
        for term in terms:
            if term[4] is not None:

                def add(term=term):
                    accs[term[3]][...] += value(refs, term)

                pl.when(term[4](ids))(add)

        @pl.when(k == nk - 1)
        def _():
            epilogue([acc[...] for acc in accs], ex, os_)

    return _call(
        body,
        name,
        grid,
        [o[1] for o in operands] + [e[1] for e in extras],
        [o[0] for o in operands] + [e[0] for e in extras],
        [o[1] for o in outs],
        [o[0] for o in outs],
        scratch=[pltpu.VMEM((2, s[0], csplit), F32) for s in acc_shapes] if csplit is not None
        else [] if simple else [pltpu.VMEM(s, F32) for s in acc_shapes],
        sem=("parallel",) * kax + ("arbitrary",),
        comm=comm,
        alias=alias,
    )


def _rms_bwd(dn_raw, h, g):
    r = lax.rsqrt(jnp.mean(h * h, axis=-1, keepdims=True) + EPS)
    hn = h * r
    dg = jnp.sum(dn_raw * hn, axis=0, keepdims=True)
    dn = dn_raw * g
    dh = r * (dn - hn * jnp.mean(dn * hn, axis=-1, keepdims=True))
    return dh, dg


def _rms_fwd(name, h, g):
    s, d = h.shape
    ts = _tile(s, 512)

    def body(h_ref, g_ref, o_ref):
        x = h_ref[...].astype(F32)
        r = lax.rsqrt(jnp.mean(x * x, axis=-1, keepdims=True) + EPS)
        o_ref[...] = (x * r * g_ref[...]).astype(BF16)

    return pl.pallas_call(
        body,
        name=name,
        grid=(s // ts,),
        in_specs=[pl.BlockSpec((ts, d), lambda i: (i, 0)), pl.BlockSpec((1, d), lambda i: (0, 0))],
        out_specs=pl.BlockSpec((ts, d), lambda i: (i, 0)),
        out_shape=SDS((s, d), BF16),
        compiler_params=_params(("parallel",)),
    )(h, g)


def _norm_bwd(name, dn, h, dres, g, out_dtype):
    s, d = h.shape
    ts = _tile(s, 512)

    def body(dn_r, h_r, dres_r, g_r, dh_o, part_o):
        dh, dg = _rms_bwd(dn_r[...].astype(F32), h_r[...].astype(F32), g_r[...])
        dh_o[...] = (dres_r[...].astype(F32) + dh).astype(out_dtype)
        part_o[...] = dg

    blk = pl.BlockSpec((ts, d), lambda i: (i, 0))
    part = pl.BlockSpec((None, 1, d), lambda i: (i, 0, 0))
    return pl.pallas_call(
        body,
        name=name,
        grid=(s // ts,),
        in_specs=[blk, blk, blk, pl.BlockSpec((1, d), lambda i: (0, 0))],
        out_specs=[blk, part],
        out_shape=[SDS((s, d), out_dtype), SDS((s // ts, 1, d), F32)],
        compiler_params=_params(("parallel",)),
    )(dn, h, dres, g)


def _slabs_to_plain(name, x):
    n, rows, w = x.shape
    tr = _tile(rows, 256)

    def body(x_ref, o_ref):
        for j in range(n):
            o_ref[:, j * w : (j + 1) * w] = x_ref[j]

    return pl.pallas_call(
        body,
        name=name,
        grid=(rows // tr,),
        in_specs=[pl.BlockSpec((n, tr, w), lambda i: (0, i, 0))],
        out_specs=pl.BlockSpec((tr, n * w), lambda i: (i, 0)),
        out_shape=SDS((rows, n * w), x.dtype),
        compiler_params=_params(("parallel",)),
    )(x)


def _plain_to_slabs(name, x, n):
    rows, nw = x.shape
    w = nw // n
    tr = _tile(rows, 256)

    def body(x_ref, o_ref):
        for j in range(n):
            o_ref[j] = x_ref[:, j * w : (j + 1) * w]

    return pl.pallas_call(
        body,
        name=name,
        grid=(rows // tr,),
        in_specs=[pl.BlockSpec((tr, nw), lambda i: (i, 0))],
        out_specs=pl.BlockSpec((n, tr, w), lambda i: (0, i, 0)),
        out_shape=SDS((n, rows, w), x.dtype),
        compiler_params=_params(("parallel",)),
    )(x)


def _prev_halo(ts, hb):
    r = ts // hb
    return lambda i: jnp.maximum(i * r - 1, 0)


def _next_halo(ts, hb, s):
    r = ts // hb
    last = s // hb - 1
    return lambda i: jnp.minimum((i + 1) * r, last)


def _shift_copies(buf, sh):
    n = sh.shape[1]
    for j in range(1, 8):
        sh[j - 1, pl.ds(0, n), :] = buf[pl.ds(j, n), :]


def _tap(buf, sh, r0, off, rows):
    j = off % 8
    start = pl.multiple_of(r0 + (off - j), 8)
    if j == 0:
        return buf[pl.ds(start, rows), :]
    return sh[j - 1, pl.ds(start, rows), :]


def _mix_a_fwd(proj, wa, s, c):
    ts, hb = _tile(s, 256), HALO_A
    prev = _prev_halo(ts, hb)

    def body(ah, ab, ac, hh, hc, w, o, buf):
        i = pl.program_id(0)
        zh = hc[...].astype(F32) * hh[...].astype(F32)
        buf[pl.ds(0, hb), :] = jnp.where(i == 0, 0.0, zh)
        buf[pl.ds(hb, ts), :] = ac[...].astype(F32) * ah[...].astype(F32)
        for r0 in range(0, ts, CONV_ROWS):
            cz = jnp.zeros((CONV_ROWS, c), F32)
            for k in range(CONV_A_K):
                cz = cz + w[k : k + 1, :] * buf[pl.ds(hb + r0 - (CONV_A_K - 1) + k, CONV_ROWS), :]
            o[pl.ds(r0, CONV_ROWS), :] = (ab[pl.ds(r0, CONV_ROWS), :].astype(F32) * cz).astype(BF16)

    main = lambda cb: pl.BlockSpec((ts, c), lambda i: (i, cb))
    halo = lambda cb: pl.BlockSpec((hb, c), lambda i: (prev(i), cb))
    return pl.pallas_call(
        body,
        name="mix_a_fwd",
        grid=(s // ts,),
        in_specs=[main(0), main(1), main(2), halo(0), halo(2), pl.BlockSpec(wa.shape, lambda i: (0, 0))],
        out_specs=pl.BlockSpec((ts, c), lambda i: (i, 0)),
        out_shape=SDS((s, c), BF16),
        scratch_shapes=[pltpu.VMEM((hb + ts, c), F32)],
        compiler_params=_params(("parallel",)),
    )(proj, proj, proj, proj, proj, wa)


def _mix_b_fwd(proj, b_glu, wd, bd, lg, lb, s, c, comm=None):
    ts, hb = _tile(s, 256), HALO_B
    prev = _prev_halo(ts, hb)

    def body(gv, gg, hv, hg, bglu, w, bd_r, lg_r, lb_r, v_o, u_o, cv_o, buf, sh):
        i = pl.program_id(0)
        bv, bg = bglu[:, 0:c], bglu[:, c : 2 * c]
        uh = (hv[...].astype(F32) + bv) * _sigmoid(hg[...].astype(F32) + bg)
        buf[pl.ds(0, hb), :] = jnp.where(i == 0, 0.0, uh)
        u = (gv[...].astype(F32) + bv) * _sigmoid(gg[...].astype(F32) + bg)
        buf[pl.ds(hb, ts), :] = u
        u_o[...] = u.astype(BF16)
        _shift_copies(buf, sh)

        def chunk(ci, carry):
            r0 = pl.multiple_of(ci * CONF_ROWS, CONF_ROWS)
            acc = jnp.zeros((CONF_ROWS, c), F32)
            for k in range(CONF_K):
                acc = acc + w[k : k + 1, :] * _tap(buf, sh, r0, hb - (CONF_K - 1) + k, CONF_ROWS)
            cv_o[pl.ds(r0, CONF_ROWS), :] = acc + bd_r[...]
            return carry

        lax.fori_loop(0, ts // CONF_ROWS, chunk, 0)
        cv = cv_o[...]
        mu = jnp.mean(cv, axis=-1, keepdims=True)
        xc = cv - mu
        rs = lax.rsqrt(jnp.mean(xc * xc, axis=-1, keepdims=True) + LN_EPS)
        ln = xc * rs * lg_r[...] + lb_r[...]
        v_o[...] = (ln * _sigmoid(ln)).astype(BF16)

    main = lambda cb: pl.BlockSpec((ts, c), lambda i: (i, cb))
    halo = lambda cb: pl.BlockSpec((hb, c), lambda i: (prev(i), cb))
    full = lambda a: pl.BlockSpec(a.shape, lambda i: (0, 0))
    out = pl.BlockSpec((ts, c), lambda i: (i, 0))
    return _call(
        body,
        "mix_b_fwd",
        (s // ts,),
        [main(3), main(4), halo(3), halo(4), full(b_glu), full(wd), full(bd), full(lg), full(lb)],
        [proj, proj, proj, proj, b_glu, wd, bd, lg, lb],
        [out, out, out],
        [SDS((s, c), BF16), SDS((s, c), BF16), SDS((s, c), F32)],
        scratch=[pltpu.VMEM((hb + ts, c), F32), pltpu.VMEM((7, hb + ts - 8, c), F32)],
        sem=("parallel",),
        comm=comm,
    )


def _mix_b_bwd1(d_v, cv, lg, lb, s, c):
    ts = _tile(s, 256)

    def body(dv_r, cv_r, lg_r, lb_r, dcv_o, part_o):
        cv_ = cv_r[...]
        mu = jnp.mean(cv_, axis=-1, keepdims=True)
        xc = cv_ - mu
        rs = lax.rsqrt(jnp.mean(xc * xc, axis=-1, keepdims=True) + LN_EPS)
        xh = xc * rs
        ln = xh * lg_r[...] + lb_r[...]
        sg = _sigmoid(ln)
        d_ln = dv_r[...].astype(F32) * (sg * (1.0 + ln * (1.0 - sg)))
        dy = d_ln * lg_r[...]
        d_cv = rs * (dy - jnp.mean(dy, axis=-1, keepdims=True) - xh * jnp.mean(dy * xh, axis=-1, keepdims=True))
        dcv_o[...] = d_cv
        part_o[0:1, :] = jnp.sum(d_ln * xh, axis=0, keepdims=True)
        part_o[1:2, :] = jnp.sum(d_ln, axis=0, keepdims=True)
        part_o[2:3, :] = jnp.sum(d_cv, axis=0, keepdims=True)

    blk = pl.BlockSpec((ts, c), lambda i: (i, 0))
    full = lambda a: pl.BlockSpec(a.shape, lambda i: (0, 0))
    return pl.pallas_call(
        body,
        name="mix_b_bwd_ln",
        grid=(s // ts,),
        in_specs=[blk, blk, full(lg), full(lb)],
        out_specs=[blk, pl.BlockSpec((None, 3, c), lambda i: (i, 0, 0))],
        out_shape=[SDS((s, c), F32), SDS((s // ts, 3, c), F32)],
        compiler_params=_params(("parallel",)),
    )(d_v, cv, lg, lb)


def _mix_b_bwd2(d_cv, u, proj, b_glu, wd, s, c, comm=None):
    ts, hb = _tile(s, 256), HALO_B
    prev, nxt = _prev_halo(ts, hb), _next_halo(ts, hb, s)
    n_t = s // ts
    kp = wd.shape[0]

    def body(dcv, dcv_n, u_m, u_p, gv, gg, bglu, w, d_o, dwd_o, dbglu_o, dbuf, ubuf, dub, dsh, ush, dwacc):
        i = pl.program_id(0)
        dbuf[pl.ds(0, ts), :] = dcv[...]
        dbuf[pl.ds(ts, hb), :] = jnp.where(i == n_t - 1, 0.0, dcv_n[...])
        ubuf[pl.ds(0, hb), :] = jnp.where(i == 0, 0.0, u_p[...].astype(F32))
        ubuf[pl.ds(hb, ts), :] = u_m[...].astype(F32)
        _shift_copies(dbuf, dsh)
        _shift_copies(ubuf, ush)
        dwacc[...] = jnp.zeros(dwacc.shape, F32)

        def chunk(ci, carry):
            r0 = pl.multiple_of(ci * CONF_ROWS, CONF_ROWS)
            acc = jnp.zeros((CONF_ROWS, c), F32)
            dc = dbuf[pl.ds(r0, CONF_ROWS), :]
            for k in range(CONF_K):
                acc = acc + w[k : k + 1, :] * _tap(dbuf, dsh, r0, (CONF_K - 1) - k, CONF_ROWS)
                prod = dc * _tap(ubuf, ush, r0, hb - (CONF_K - 1) + k, CONF_ROWS)
                fold = prod[0:8]
                for a in range(1, CONF_ROWS // 8):
                    fold = fold + prod[8 * a : 8 * a + 8]
                dwacc[pl.ds(8 * k, 8), :] += fold
            dub[pl.ds(r0, CONF_ROWS), :] = acc
            return carry

        lax.fori_loop(0, ts // CONF_ROWS, chunk, 0)
        for k in range(CONF_K):
            dwd_o[k : k + 1, :] = jnp.sum(dwacc[pl.ds(8 * k, 8), :], axis=0, keepdims=True)
        dwd_o[CONF_K:kp, :] = jnp.zeros((kp - CONF_K, c), F32)
        bv, bg = bglu[:, 0:c], bglu[:, c : 2 * c]
        d_u = dub[...]
        sg = _sigmoid(gg[...].astype(F32) + bg)
        d_gv = d_u * sg
        d_gg = d_u * (gv[...].astype(F32) + bv) * sg * (1.0 - sg)
        d_o[:, 0:c] = d_gv.astype(BF16)
        d_o[:, c : 2 * c] = d_gg.astype(BF16)
        dbglu_o[:, 0:c] = jnp.sum(d_gv, axis=0, keepdims=True)
        dbglu_o[:, c : 2 * c] = jnp.sum(d_gg, axis=0, keepdims=True)

    blk = lambda cb: pl.BlockSpec((ts, c), lambda i: (i, cb))
    full = lambda a: pl.BlockSpec(a.shape, lambda i: (0, 0))
    return _call(
        body,
        "mix_b_bwd_conv",
        (n_t,),
        [
            blk(0),
            pl.BlockSpec((hb, c), lambda i: (nxt(i), 0)),
            blk(0),
            pl.BlockSpec((hb, c), lambda i: (prev(i), 0)),
            blk(3),
            blk(4),
            full(b_glu),
            full(wd),
        ],
        [d_cv, d_cv, u, u, proj, proj, b_glu, wd],
        [
            pl.BlockSpec((ts, 2 * c), lambda i: (i, 0)),
            pl.BlockSpec((None, kp, c), lambda i: (i, 0, 0)),
            pl.BlockSpec((None, 1, 2 * c), lambda i: (i, 0, 0)),
        ],
        [SDS((s, 2 * c), BF16), SDS((n_t, kp, c), F32), SDS((n_t, 1, 2 * c), F32)],
        scratch=[
            pltpu.VMEM((ts + hb, c), F32), pltpu.VMEM((hb + ts, c), F32), pltpu.VMEM((ts, c), F32),
            pltpu.VMEM((7, hb + ts - 8, c), F32), pltpu.VMEM((7, hb + ts - 8, c), F32), pltpu.VMEM((8 * CONF_K, c), F32),
        ],
        sem=("parallel",),
        comm=comm,
    )


def _mix_a_bwd(d_ya, proj, wa, s, c):
    ts, hb = _tile(s, 256), HALO_A
    prev, nxt = _prev_halo(ts, hb), _next_halo(ts, hb, s)
    n_t = s // ts
    kp = wa.shape[0]

    def body(dya, dya_n, ah, ab, ac, ah_p, ac_p, ab_n, w, d_o, dwa_o, zbuf, dbuf, dzb):
        i = pl.program_id(0)
        zbuf[pl.ds(0, hb), :] = jnp.where(i == 0, 0.0, ac_p[...].astype(F32) * ah_p[...].astype(F32))
        zbuf[pl.ds(hb, ts), :] = ac[...].astype(F32) * ah[...].astype(F32)
        dbuf[pl.ds(0, ts), :] = dya[...].astype(F32) * ab[...].astype(F32)
        dbuf[pl.ds(ts, hb), :] = jnp.where(i == n_t - 1, 0.0, dya_n[...].astype(F32) * ab_n[...].astype(F32))
        dw_rows = [jnp.zeros((1, c), F32) for _ in range(CONV_A_K)]
        for r0 in range(0, ts, CONV_ROWS):
            cz = jnp.zeros((CONV_ROWS, c), F32)
            dz = jnp.zeros((CONV_ROWS, c), F32)
            dc = dbuf[pl.ds(r0, CONV_ROWS), :]
            for k in range(CONV_A_K):
                zk = zbuf[pl.ds(hb + r0 - (CONV_A_K - 1) + k, CONV_ROWS), :]
                cz = cz + w[k : k + 1, :] * zk
                dz = dz + w[k : k + 1, :] * dbuf[pl.ds(r0 + (CONV_A_K - 1) - k, CONV_ROWS), :]
                dw_rows[k] = dw_rows[k] + jnp.sum(dc * zk, axis=0, keepdims=True)
            d_o[pl.ds(r0, CONV_ROWS), c : 2 * c] = (dya[pl.ds(r0, CONV_ROWS), :].astype(F32) * cz).astype(BF16)
            dzb[pl.ds(r0, CONV_ROWS), :] = dz
        d_z = dzb[...]
        d_o[:, 0:c] = (d_z * ac[...].astype(F32)).astype(BF16)
        d_o[:, 2 * c : 3 * c] = (d_z * ah[...].astype(F32)).astype(BF16)
        for k in range(CONV_A_K):
            dwa_o[k : k + 1, :] = dw_rows[k]
        dwa_o[CONV_A_K:kp, :] = jnp.zeros((kp - CONV_A_K, c), F32)

    blk = lambda cb: pl.BlockSpec((ts, c), lambda i: (i, cb))
    hp = lambda cb: pl.BlockSpec((hb, c), lambda i: (prev(i), cb))
    hn = lambda cb: pl.BlockSpec((hb, c), lambda i: (nxt(i), cb))
    return pl.pallas_call(
        body,
        name="mix_a_bwd",
        grid=(n_t,),
        in_specs=[blk(0), hn(0), blk(0), blk(1), blk(2), hp(0), hp(2), hn(1), pl.BlockSpec(wa.shape, lambda i: (0, 0))],
        out_specs=[pl.BlockSpec((ts, 3 * c), lambda i: (i, 0)), pl.BlockSpec((None, kp, c), lambda i: (i, 0, 0))],
        out_shape=[SDS((s, 3 * c), BF16), SDS((n_t, kp, c), F32)],
        scratch_shapes=[pltpu.VMEM((hb + ts, c), F32), pltpu.VMEM((ts + hb, c), F32), pltpu.VMEM((ts, c), F32)],
        compiler_params=_params(("parallel",)),
    )(d_ya, d_ya, proj, proj, proj, proj, proj, proj, wa)


def _ep_bf16(accs, ex, os_):
    os_[0][...] = accs[0].astype(BF16)


def _mm_tn(name, a, b, tm=2048, tn=1024, tk=1024):
    m, k1 = a.shape
    n = b.shape[1]
    tm, tn, tk = _tile(k1, tm), _tile(n, tn), _tile(m, tk)
    return _fmm(
        name,
        (k1 // tm, n // tn, m // tk),
        [(a, pl.BlockSpec((tk, tm), lambda i, j, k: (k, i))), (b, pl.BlockSpec((tk, tn), lambda i, j, k: (k, j)))],
        [(0, 1, TN, 0, None)],
        [(tm, tn)],
        [],
        [(SDS((k1, n), BF16), pl.BlockSpec((tm, tn), lambda i, j, k: (i, j)))],
        _ep_bf16,
    )[0][0]


def _mm_nt(name, a, b, tm=1024, tn=1024, comm=None):
    m, kk = a.shape
    n = b.shape[0]
    tm, tn = _tile(m, tm), _tile(n, tn)
    outs, couts = _fmm(
        name,
        (m // tm, n // tn, 1),
        [(a, pl.BlockSpec((tm, kk), lambda i, j, k: (i, 0))), (b, pl.BlockSpec((tn, kk), lambda i, j, k: (j, 0)))],
        [(0, 1, NT, 0, None)],
        [(tm, tn)],
        [],
        [(SDS((m, n), BF16), pl.BlockSpec((tm, tn), lambda i, j, k: (i, j)))],
        _ep_bf16,
        comm=comm,
    )
    return outs[0], couts


def _dev_index(dev):
    return 4 * dev[0] + 2 * dev[1] + dev[2]


def _region(ref, kind, j, shard_shape):
    if kind == "col":
        ns = shard_shape[1]
        return ref.at[:, pl.ds(pl.multiple_of(j * ns, 128), ns)]
    if kind == "row":
        rs = shard_shape[0]
        return ref.at[pl.ds(pl.multiple_of(j * rs, 8), rs), :]
    return ref.at[j]


def _whole_shape(kind, shard_shape):
    if kind == "col":
        return (shard_shape[0], NDEV * shard_shape[1])
    if kind == "row":
        return (NDEV * shard_shape[0], shard_shape[1])
    return (NDEV,) + tuple(shard_shape)


def _place():
    return lax.axis_index("x"), lax.axis_index("y"), lax.axis_index("c")


def _proj_gather(n1, w_shard, early, late):
    s, d = n1.shape
    ns = w_shard.shape[1]
    pw = 2 * ns
    tm = _tile(s // 2, 512)
    n_i = s // tm
    comm = _join(early, late)
    n_early = (len(early["ins"]), len(early["outs"]), len(early["sems"]))
    assert n_i >= 2 and not comm["alias"]
    x0, y0, _ = _place()
    order = jnp.stack([2 * x0 + y0, 2 * x0 + (1 - y0), 2 * (1 - x0) + y0, 2 * (1 - x0) + (1 - y0)]).astype(jnp.int32)
    n_ci, n_co = len(comm["ins"]), len(comm["outs"])

    def body(order_ref, n1_ref, wsh_ref, *rest):
        ci = rest[:n_ci]
        proj_ref, win_ref = rest[n_ci], rest[n_ci + 1]
        co = rest[n_ci + 2 : n_ci + 2 + n_co]
        wfull, send, recv, fsend, frecv, loc, osem = rest[n_ci + 2 + n_co : n_ci + 9 + n_co]
        cs = rest[n_ci + 9 + n_co :]
        u, i = pl.program_id(0), pl.program_id(1)
        x, y, c = _place()
        sib = (x, y, 1 - c)
        chips = [(x, y), (x, 1 - y), (1 - x, y), (1 - x, 1 - y)]
        peers = [sib] + [(*ch, c) for ch in chips[1:]]
        blk = lambda ch, core: wfull.at[2 * ch[0] + ch[1], :, pl.ds(pl.multiple_of(core * ns, 128), ns)]
        sends = [_remote(blk(chips[0], c), blk(chips[0], c), send.at[k], recv.at[k], peers[k]) for k in range(4)]
        arrivals = [_remote(blk(chips[0], 1 - c), blk(chips[0], 1 - c), send.at[0], recv.at[0], sib)] + [
            _remote(blk(chips[k], c), blk(chips[k], c), send.at[k], recv.at[k], peers[k]) for k in range(1, 4)
        ]
        passes = [_remote(blk(chips[k], c), blk(chips[k], c), fsend.at[k - 1], frecv.at[k - 1], sib) for k in range(1, 4)]
        passed = [_remote(blk(chips[k], 1 - c), blk(chips[k], 1 - c), fsend.at[k - 1], frecv.at[k - 1], sib) for k in range(1, 4)]
        mine = lambda: pltpu.make_async_copy(wsh_ref, blk(chips[0], c), loc.at[0])

        def to_hbm(unit):
            q = order_ref[unit]
            return pltpu.make_async_copy(wfull.at[q], win_ref.at[:, pl.ds(pl.multiple_of(q * pw, 128), pw)], osem.at[unit])

        a, b, e = n_early
        early_refs = (ci[:a], co[:b], cs[:e])
        late_refs = (ci[a:], co[b:], cs[e:])

        @pl.when(jnp.logical_and(u == 0, i == 0))
        def _():
            mine().start()
            mine().wait()
            for snd in sends[:3]:
                snd().start()
            early["start"](*early_refs)
            arrivals[0]().wait_recv()

        @pl.when(jnp.logical_and(u == 1, i == 0))
        def _():
            sends[3]().start()

        @pl.when(jnp.logical_and(u == 2, i == 0))
        def _():
            late["start"](*late_refs)

        for nxt in range(1, 4):

            @pl.when(jnp.logical_and(u == nxt - 1, i == n_i - 1))
            def _(nxt=nxt):
                passed[nxt - 1]().wait_recv()

        proj_ref[...] = jnp.dot(n1_ref[...], wfull[order_ref[u]], preferred_element_type=F32).astype(BF16)

        for nxt in range(1, 4):

            @pl.when(jnp.logical_and(u == nxt - 1, i == n_i - 2))
            def _(nxt=nxt):
                arrivals[nxt]().wait_recv()
                passes[nxt - 1]().start()

        for unit in range(4):

            @pl.when(jnp.logical_and(u == unit, i == n_i - 1))
            def _(unit=unit):
                to_hbm(unit).start()

        @pl.when(jnp.logical_and(u == 3, i == n_i - 1))
        def _():
            for snd in sends + passes:
                snd().wait_send()
            for unit in range(4):
                to_hbm(unit).wait()
            comm["finish"](ci, co, cs)

    hbm = pl.BlockSpec(memory_space=pl.ANY)
    dma = pltpu.SemaphoreType.DMA
    res = pl.pallas_call(
        body,
        name="proj",
        grid_spec=pltpu.PrefetchScalarGridSpec(
            num_scalar_prefetch=1,
            grid=(4, n_i),
            in_specs=[pl.BlockSpec((tm, d), lambda u, i, order_ref: (i, 0)), hbm] + [hbm] * n_ci,
            out_specs=[pl.BlockSpec((tm, pw), lambda u, i, order_ref: (i, order_ref[u])), hbm] + [hbm] * n_co,
            scratch_shapes=[pltpu.VMEM((4, d, pw), BF16), dma((4,)), dma((4,)), dma((3,)), dma((3,)), dma((1,)), dma((4,))]
            + list(comm["sems"]),
        ),
        out_shape=[SDS((s, NDEV * ns), BF16), SDS((d, NDEV * ns), BF16)] + list(comm["outs"]),
        compiler_params=_params(("arbitrary", "arbitrary")),
    )(order, n1, w_shard, *comm["ins"])
    return res[0], res[1], list(res[2:])


def _peer(me, r):
    x, y, c = me
    return (1 - x if r & 4 else x, 1 - y if r & 2 else y, 1 - c if r & 1 else c)


def _remote(src, dst, send_sem, recv_sem, to):
    return lambda: pltpu.make_async_remote_copy(
        src_ref=src, dst_ref=dst, send_sem=send_sem, recv_sem=recv_sem, device_id=to, device_id_type=MESH
    )


def _run(pairs, locals_, start):
    if start:
        for cp in locals_:
            cp.start()
        for snd, _ in pairs:
            snd().start()
    else:
        for snd, arr in pairs:
            arr().wait_recv()
            snd().wait_send()
        for cp in locals_:
            cp.wait()


def _stage(ins, outs, alias, sems, build):
    return dict(
        ins=list(ins), outs=list(outs), alias=alias, sems=list(sems),
        start=lambda i, o, s: _run(*build(i, o, s), True),
        finish=lambda i, o, s: _run(*build(i, o, s), False),
    )


def _ag1(shards, kinds):
    n_t = len(shards)
    shapes = [tuple(sh.shape) for sh in shards]

    def build(srcs, dsts, sems):
        send, recv, loc = sems
        x, y, c = _place()
        me = (x, y, c)
        peers = [(x, y, 1 - c), (1 - x, y, c), (x, 1 - y, c), (1 - x, 1 - y, c)]
        reg = lambda t, dev: _region(dsts[t], kinds[t], _dev_index(dev), shapes[t])
        pairs = []
        for t in range(n_t):
            for k, peer in enumerate(peers):
                snd = _remote(srcs[t], reg(t, me), send.at[t, k], recv.at[t, k], peer)
                arr = _remote(reg(t, peer), reg(t, peer), send.at[t, k], recv.at[t, k], peer)
                pairs.append((snd, arr))
        mine = [pltpu.make_async_copy(srcs[t], reg(t, me), loc.at[t]) for t in range(n_t)]
        return pairs, mine

    outs = [SDS(_whole_shape(kinds[t], shapes[t]), shards[t].dtype) for t in range(n_t)]
    dma = pltpu.SemaphoreType.DMA
    return _stage(shards, outs, {}, [dma((n_t, 4)), dma((n_t, 4)), dma((n_t,))], build)


def _ag_direct(shards, kinds):
    n_t = len(shards)
    shapes = [tuple(sh.shape) for sh in shards]

    def build(srcs, dsts, sems):
        send, recv, loc = sems
        me = _place()
        reg = lambda t, dev: _region(dsts[t], kinds[t], _dev_index(dev), shapes[t])
        pairs = []
        for t in range(n_t):
            for r in range(1, NDEV):
                peer = _peer(me, r)
                snd = _remote(srcs[t], reg(t, me), send.at[t, r - 1], recv.at[t, r - 1], peer)
                arr = _remote(reg(t, peer), reg(t, peer), send.at[t, r - 1], recv.at[t, r - 1], peer)
                pairs.append((snd, arr))
        mine = [pltpu.make_async_copy(srcs[t], reg(t, me), loc.at[t]) for t in range(n_t)]
        return pairs, mine

    outs = [SDS(_whole_shape(kinds[t], shapes[t]), shards[t].dtype) for t in range(n_t)]
    dma = pltpu.SemaphoreType.DMA
    return _stage(shards, outs, {}, [dma((n_t, 7)), dma((n_t, 7)), dma((n_t,))], build)


def _ag2(wholes, kinds, shapes):
    n_t = len(wholes)

    def build(_, dsts, sems):
        send, recv = sems
        x, y, c = _place()
        sib = (x, y, 1 - c)
        chips = [(1 - x, y), (x, 1 - y), (1 - x, 1 - y)]
        reg = lambda t, dev: _region(dsts[t], kinds[t], _dev_index(dev), shapes[t])
        pairs = []
        for t in range(n_t):
            for j, chip in enumerate(chips):
                snd = _remote(reg(t, (*chip, c)), reg(t, (*chip, c)), send.at[t, j], recv.at[t, j], sib)
                arr = _remote(reg(t, (*chip, 1 - c)), reg(t, (*chip, 1 - c)), send.at[t, j], recv.at[t, j], sib)
                pairs.append((snd, arr))
        return pairs, []

    outs = [SDS(w.shape, w.dtype) for w in wholes]
    dma = pltpu.SemaphoreType.DMA
    return _stage(wholes, outs, {t: t for t in range(n_t)}, [dma((n_t, 3)), dma((n_t, 3))], build)


def _chip_of(q):
    return (q >> 1, q & 1)


def _rs1(wholes, kinds, shapes):
    n_t = len(wholes)

    def build(srcs, outs, sems):
        send, recv = sems
        x, y, c = _place()
        sib = (x, y, 1 - c)
        pairs = []
        for t in range(n_t):
            for q in range(4):
                theirs = _region(srcs[t], kinds[t], _dev_index((*_chip_of(q), 1 - c)), shapes[t])
                pairs.append((
                    _remote(theirs, outs[t].at[q], send.at[t, q], recv.at[t, q], sib),
                    _remote(outs[t].at[q], outs[t].at[q], send.at[t, q], recv.at[t, q], sib),
                ))
        return pairs, []

    slabs = [SDS((4,) + tuple(shapes[t]), wholes[t].dtype) for t in range(n_t)]
    dma = pltpu.SemaphoreType.DMA
    return _stage(wholes, slabs, {}, [dma((n_t, 4)), dma((n_t, 4))], build)


def _rs2(pair_sums):
    n_t = len(pair_sums)

    def build(srcs, lands, sems):
        send, recv, loc = sems
        x, y, c = _place()
        my_chip = 2 * x + y
        pairs, mine = [], []
        for t in range(n_t):
            for j, (px, py) in enumerate([(1 - x, y), (x, 1 - y), (1 - x, 1 - y)]):
                q = 2 * px + py
                pairs.append((
                    _remote(srcs[t].at[q], lands[t].at[my_chip], send.at[t, j], recv.at[t, j], (px, py, c)),
                    _remote(lands[t].at[q], lands[t].at[q], send.at[t, j], recv.at[t, j], (px, py, c)),
                ))
            mine.append(pltpu.make_async_copy(srcs[t].at[my_chip], lands[t].at[my_chip], loc.at[t]))
        return pairs, mine

    outs = [SDS(q.shape, q.dtype) for q in pair_sums]
    dma = pltpu.SemaphoreType.DMA
    return _stage(pair_sums, outs, {}, [dma((n_t, 3)), dma((n_t, 3)), dma((n_t,))], build)


def _pair_sum(name, whole, kind, got):
    _, rows, cols = got.shape
    tr = _tile(rows, 512)
    n_r = rows // tr
    core = lax.axis_index("c").astype(jnp.int32).reshape(1)

    def body(_, a, b, o):
        o[...] = (a[...].astype(F32) + b[...].astype(F32)).astype(BF16)

    if kind == "col":
        own = pl.BlockSpec((tr, cols), lambda q, i, c_ref: (i, 2 * q + c_ref[0]))
    elif kind == "row":
        own = pl.BlockSpec((tr, cols), lambda q, i, c_ref: ((2 * q + c_ref[0]) * n_r + i, 0))
    else:
        own = pl.BlockSpec((None, tr, cols), lambda q, i, c_ref: (2 * q + c_ref[0], i, 0))
    slab = pl.BlockSpec((None, tr, cols), lambda q, i, c_ref: (q, i, 0))
    return pl.pallas_call(
        body,
        name=name,
        grid_spec=pltpu.PrefetchScalarGridSpec(
            num_scalar_prefetch=1, grid=(4, n_r), in_specs=[own, slab], out_specs=slab
        ),
        out_shape=SDS(got.shape, BF16),
        compiler_params=_params(("parallel", "parallel")),
    )(core, whole, got)


def _packed_rows(parts, c_):
    offs, r0 = [], 0
    for p in parts:
        offs.append(r0)
        r0 += p.shape[1] * (p.shape[2] // c_)
    return offs, -(-r0 // 8) * 8


def _all_reduce_small(parts, c_):
    n_p = len(parts)
    offs, r_ = _packed_rows(parts, c_)

    def body(*refs):
        p_refs = refs[:n_p]
        land, total, src, send_sems, recv_sems = refs[n_p:]
        me = _place()
        my = _dev_index(me)
        src[...] = jnp.zeros((r_, c_), F32)
        for p_ref, r0 in zip(p_refs, offs):
            v = jnp.sum(p_ref[...], axis=0)
            k = v.shape[1] // c_
            for ri in range(v.shape[0]):
                for q in range(k):
                    src[r0 + ri * k + q : r0 + ri * k + q + 1, :] = v[ri : ri + 1, q * c_ : (q + 1) * c_]
        land[my] = src[...]

        def copy(r):
            peer = _peer(me, r)
            return pltpu.make_async_remote_copy(
                src_ref=src,
                dst_ref=land.at[my],
                send_sem=send_sems.at[r - 1],
                recv_sem=recv_sems.at[r - 1],
                device_id=peer,
                device_id_type=MESH,
            )

        def arrival(r):
            peer = _peer(me, r)
            slab = land.at[_dev_index(peer)]
            return pltpu.make_async_remote_copy(
                src_ref=slab,
                dst_ref=slab,
                send_sem=send_sems.at[r - 1],
                recv_sem=recv_sems.at[r - 1],
                device_id=peer,
                device_id_type=MESH,
            )

        sends = [copy(r) for r in range(1, NDEV)]
        for cp in sends:
            cp.start()
        for r in range(1, NDEV):
            arrival(r).wait_recv()
        for cp in sends:
            cp.wait_send()
        acc = land[0]
        for d in range(1, NDEV):
            acc = acc + land[d]
        total[...] = acc

    vmem = pl.BlockSpec(memory_space=pltpu.VMEM)
    return pl.pallas_call(
        body,
        name="all_reduce_small",
        in_specs=[vmem] * n_p,
        out_specs=[vmem, vmem],
        out_shape=[SDS((NDEV, r_, c_), F32), SDS((r_, c_), F32)],
        scratch_shapes=[pltpu.VMEM((r_, c_), F32), pltpu.SemaphoreType.DMA((7,)), pltpu.SemaphoreType.DMA((7,))],
        compiler_params=_params(),
    )(*parts)[1]


def _adamw_math(g, w, m, v):
    m2 = ADAM_B1 * m + (1.0 - ADAM_B1) * g
    v2 = ADAM_B2 * v + (1.0 - ADAM_B2) * (g * g)
    m_hat = m2 / (1.0 - ADAM_B1**ADAM_STEP)
    v_hat = v2 / (1.0 - ADAM_B2**ADAM_STEP)
    delta = -ADAM_LR * (m_hat / (jnp.sqrt(v_hat) + ADAM_EPS) + ADAM_WD * w)
    return delta, m2, v2


def _adamw_big(name, land, w, m, v):
    rows, cols = w.shape
    n_slab = land.shape[0]
    row_bytes = 2 * cols * (2 * n_slab + 4 * 7)
    tr = _tile(rows, 512 if 512 * row_bytes <= V7X_VMEM_LIMIT_BYTES * 3 // 4 else 256)

    def body(l_ref, w_ref, m_ref, v_ref, g_o, d_o, m_o, v_o):
        g = l_ref[0].astype(F32)
        for d in range(1, n_slab):
            g = g + l_ref[d].astype(F32)
        delta, m2, v2 = _adamw_math(g, w_ref[...], m_ref[...], v_ref[...])
        g_o[...] = g
        d_o[...] = delta
        m_o[...] = m2
        v_o[...] = v2

    blk = pl.BlockSpec((tr, cols), lambda i: (i, 0))
    return pl.pallas_call(
        body,
        name=name,
        grid=(rows // tr,),
        in_specs=[pl.BlockSpec((n_slab, tr, cols), lambda i: (0, i, 0)), blk, blk, blk],
        out_specs=[blk] * 4,
        out_shape=[SDS((rows, cols), F32)] * 4,
        compiler_params=_params(("parallel",)),
    )(land, w, m, v)


def _adamw_small(total, items):
    c_ = total.shape[1]
    n_it = len(items)

    def body(*refs):
        t_ref = refs[0]
        ins, outs = refs[1 : 1 + 3 * n_it], refs[1 + 3 * n_it :]
        my = _dev_index(_place())
        for q, (row0, taps, w, _, _) in enumerate(items):
            w_ref, m_ref, v_ref = ins[3 * q : 3 * q + 3]
            if taps:
                lanes = w.shape[-1]
                g = t_ref[pl.ds(row0, taps), pl.ds(pl.multiple_of(my * lanes, 128), lanes)][None]
            else:
                k = w.shape[-1] // c_
                g = jnp.concatenate([t_ref[row0 + j : row0 + j + 1, :] for j in range(k)], axis=1)
            delta, m2, v2 = _adamw_math(g, w_ref[...], m_ref[...], v_ref[...])
            for o_ref, val in zip(outs[4 * q : 4 * q + 4], (g, delta, m2, v2)):
                o_ref[...] = val

    vmem = pl.BlockSpec(memory_space=pltpu.VMEM)
    flat = [a for (_, _, w, m, v) in items for a in (w, m, v)]
    res = pl.pallas_call(
        body,
        name="adamw_small",
        in_specs=[vmem] * (1 + 3 * n_it),
        out_specs=[vmem] * (4 * n_it),
        out_shape=[SDS(w.shape, F32) for (_, _, w, _, _) in items for _ in range(4)],
    )(total, *flat)
    return [list(res[4 * q : 4 * q + 4]) for q in range(n_it)]


def kernel(x, p, g_mix, w_in, conv_a_w, w_out_a, b_glu, conf_dw_w, conf_dw_b, conf_ln_g, conf_ln_b, w_pw_b, b_pw_b, w_o, g_ffn, w_gate, w_up, w_down, g_ple, w_ple_gate, w_ple_proj, g_final, loss_target, m_g_mix, m_w_in, m_conv_a_w, m_w_out_a, m_b_glu, m_conf_dw_w, m_conf_dw_b, m_conf_ln_g, m_conf_ln_b, m_w_pw_b, m_b_pw_b, m_w_o, m_g_ffn, m_w_gate, m_w_up, m_w_down, m_g_ple, m_w_ple_gate, m_w_ple_proj, m_g_final, v_g_mix, v_w_in, v_conv_a_w, v_w_out_a, v_b_glu, v_conf_dw_w, v_conf_dw_b, v_conf_ln_g, v_conf_ln_b, v_w_pw_b, v_b_pw_b, v_w_o, v_g_ffn, v_w_gate, v_w_up, v_w_down, v_g_ple, v_w_ple_gate, v_w_ple_proj, v_g_final):
    s, d = x.shape[1], x.shape[2]
    c = conf_ln_g.shape[-1]
    pdim = w_ple_proj.shape[1]
    fs = w_gate.shape[-1]
    nin = NDEV * w_in.shape[-1]
    assert d == 2 * c and nin == 5 * c + 2 * d, (d, c, nin)
    x2, p2, tgt = x[0], p[0, 0], loss_target[0]
    gfin = g_final.reshape(1, d)

    kpa, kpb = 8, HALO_B
    wa_sh = jnp.pad(conv_a_w[0], ((0, kpa - CONV_A_K), (0, 0)))
    wd_sh = jnp.pad(conf_dw_w[0], ((0, kpb - CONF_K), (0, 0)))
    kind_of = dict(w_in="col", w_out_a="col", w_pw_b="col", w_ple_proj="col", w_o="row", w_ple_gate="row",
                   w_gate="blk", w_up="blk", w_down="blk")
    weight = dict(w_in=w_in, w_out_a=w_out_a, w_pw_b=w_pw_b, w_ple_proj=w_ple_proj, w_o=w_o, w_ple_gate=w_ple_gate,
                  w_gate=w_gate, w_up=w_up, w_down=w_down)
    shard_of = {nm: tuple(w.shape[1:]) for nm, w in weight.items()}
    bf16_shard = lambda nm: weight[nm][0].astype(BF16)
    kinds_ = lambda grp: [kind_of[nm] for nm in grp]
    shapes_ = lambda grp: [shard_of[nm] for nm in grp]
    first_stage = lambda grp: _ag1([bf16_shard(nm) for nm in grp], kinds_(grp))
    second_stage = lambda grp, parts: _ag2(parts, kinds_(grp), shapes_(grp))
    grp_1 = ["w_out_a", "w_pw_b"]
    grp_2 = ["w_o", "w_gate"]
    grp_3 = ["w_up"]
    grp_4 = ["w_down"]
    grp_5 = ["w_ple_gate", "w_ple_proj"]

    tm = _tile(s, 1024)
    tn = _tile(d, 1024)
    assert (5 * c) % tn == 0 and d % tn == 0 and c % tn == 0
    ga_blk, gb_blk = (5 * c) // tn, (5 * c + d) // tn
    ij = lambda i, j, k: (i, j)
    row_i = lambda i, j, k: (i, 0)

    n1 = _rms_fwd("rms1", x2, g_mix)
    proj, win, got = _proj_gather(
        n1, bf16_shard("w_in"),
        _join(_ag_direct([wa_sh, wd_sh], ["col", "col"]), first_stage(grp_1)), first_stage(grp_2),
    )
    (wa, wd), part_12 = got[:2], got[2:]
    grp_12 = grp_1 + grp_2
    ya_in = _mix_a_fwd(proj, wa, s, c)
    (v_act, u_act, cv), got = _mix_b_fwd(
        proj, b_glu, wd, conf_dw_b, conf_ln_g, conf_ln_b, s, c,
        comm=_join(second_stage(grp_12, part_12), first_stage(grp_3)),
    )
    (wouta, wpw, wo, wg), part_3 = got[: len(grp_12)], got[len(grp_12) :]

    def ep_merge(accs, ex, os_):
        sa = _sigmoid(ex[0][...].astype(F32))
        sb = _sigmoid(ex[1][...].astype(F32))
        ya = accs[0]
        yb = accs[1] + ex[2][...]
        os_[0][...] = (sa * ya + sb * yb).astype(BF16)
        os_[1][...] = ya.astype(BF16)
        os_[2][...] = yb.astype(BF16)

    gate_a_spec = pl.BlockSpec((tm, tn), lambda i, j, k: (i, ga_blk + j))
    gate_b_spec = pl.BlockSpec((tm, tn), lambda i, j, k: (i, gb_blk + j))
    out_sd = (SDS((s, d), BF16), pl.BlockSpec((tm, tn), ij))
    (m_act, ya, yb), got = _fmm(
        "merge", (s // tm, d // tn, 1),
        [(ya_in, pl.BlockSpec((tm, c), row_i)), (wouta, pl.BlockSpec((c, tn), lambda i, j, k: (0, j))),
         (v_act, pl.BlockSpec((tm, c), row_i)), (wpw, pl.BlockSpec((c, tn), lambda i, j, k: (0, j)))],
        [(0, 1, NN, 0, None), (2, 3, NN, 1, None)], [(tm, tn), (tm, tn)],
        [(proj, gate_a_spec), (proj, gate_b_spec), (b_pw_b, pl.BlockSpec((1, tn), lambda i, j, k: (0, j)))],
        [out_sd, out_sd, out_sd], ep_merge, csplit=EPILOGUE_CHUNK,
        comm=_join(second_stage(grp_3, part_3), first_stage(grp_4)),
    )
    (wu,), part_4 = got[: len(grp_3)], got[len(grp_3) :]

    def ep_residual(accs, ex, os_):
        os_[0][...] = (accs[0] + ex[0][...].astype(F32)).astype(BF16)

    (h1,), got = _fmm(
        "w_o", (s // tm, d // tn, 1),
        [(m_act, pl.BlockSpec((tm, d), row_i)), (wo, pl.BlockSpec((d, tn), lambda i, j, k: (0, j)))],
        [(0, 1, NN, 0, None)], [(tm, tn)], [(x2, pl.BlockSpec((tm, tn), ij))],
        [(SDS((s, d), BF16), pl.BlockSpec((tm, tn), ij))], ep_residual, csplit=EPILOGUE_CHUNK,
        comm=_join(second_stage(grp_4, part_4), first_stage(grp_5)),
    )
    (wdn,), part_5 = got[: len(grp_4)], got[len(grp_4) :]
    n2 = _rms_fwd("rms2", h1, g_ffn)

    hidden = NDEV * fs
    wg_p = _slabs_to_plain("w_gate_plain", wg)
    wu_p = _slabs_to_plain("w_up_plain", wu)
    wdn_p = wdn.reshape(hidden, d)
    tf = _tile(hidden, 512)
    tkf = _tile(hidden, 2816)

    def ep_gateup(accs, ex, os_):
        g, u = accs
        sg = _sigmoid(g)
        silu = g * sg
        os_[0][...] = (u * sg * (1.0 + g * (1.0 - sg))).astype(BF16)
        os_[1][...] = silu.astype(BF16)
        os_[2][...] = (silu * u).astype(BF16)

    ff_sd = (SDS((s, hidden), BF16), pl.BlockSpec((tm, tf), ij))
    w_col_blk = pl.BlockSpec((d, tf), lambda i, j, k: (0, j))
    (df_dg, df_du, f_act), (wpg, wpp) = _fmm(
        "gate_up", (s // tm, hidden // tf, 1),
        [(n2, pl.BlockSpec((tm, d), row_i)), (wg_p, w_col_blk), (wu_p, w_col_blk)],
        [(0, 1, NN, 0, None), (0, 2, NN, 1, None)], [(tm, tf), (tm, tf)], [],
        [ff_sd, ff_sd, ff_sd], ep_gateup,
        comm=second_stage(grp_5, part_5),
    )
    (h2,), _ = _fmm(
        "down", (s // tm, d // tn, hidden // tkf),
        [(f_act, pl.BlockSpec((tm, tkf), lambda i, j, k: (i, k))),
         (wdn_p, pl.BlockSpec((tkf, tn), lambda i, j, k: (k, j)))],
        [(0, 1, NN, 0, None)], [(tm, tn)], [(h1, pl.BlockSpec((tm, tn), ij))],
        [(SDS((s, d), BF16), pl.BlockSpec((tm, tn), ij))], ep_residual,
    )
    n3 = _rms_fwd("rms3", h2, g_ple)

    tr = _tile(s, 256)
    n_r = s // tr
    rows = lambda i, j, k: (i, 0)
    whole = lambda i, j, k: (0, 0)
    part_spec = lambda nrow: pl.BlockSpec((None, nrow, d), lambda i, j, k: (i, 0, 0))

    def ep_ple(accs, ex, os_):
        h2_, t_, gf = ex[0][...].astype(F32), ex[1][...], ex[2][...]
        ple = accs[0]
        s3 = _sigmoid(accs[1])
        h3 = h2_ + s3 * ple
        r = lax.rsqrt(jnp.mean(h3 * h3, axis=-1, keepdims=True) + EPS)
        hn = h3 * r
        e = hn * gf - t_
        loss = 0.5 * jnp.sum(jnp.mean(e * e, axis=-1, keepdims=True), axis=0, keepdims=True)
        dy = e * (1.0 / d)
        dn = dy * gf
        dh3 = r * (dn - hn * jnp.mean(dn * hn, axis=-1, keepdims=True))
        os_[0][...] = dh3.astype(BF16)
        os_[1][...] = (dh3 * s3).astype(BF16)
        os_[2][...] = (dh3 * ple * s3 * (1.0 - s3)).astype(BF16)
        os_[3][0:1, :] = jnp.sum(dy * hn, axis=0, keepdims=True)
        os_[3][1:2, :] = jnp.broadcast_to(loss, (1, d))

    (dh3, d_ple, d_g3, part_fin), _ = _fmm(
        "ple_loss", (n_r, 1, 1),
        [(p2, pl.BlockSpec((tr, pdim), rows)), (wpp, pl.BlockSpec((pdim, d), whole)),
         (n3, pl.BlockSpec((tr, d), rows)), (wpg, pl.BlockSpec((d, d), whole))],
        [(0, 1, NN, 0, None), (2, 3, NN, 1, None)], [(tr, d), (tr, d)],
        [(h2, pl.BlockSpec((tr, d), rows)), (tgt, pl.BlockSpec((tr, d), rows)), (gfin, pl.BlockSpec((1, d), whole))],
        [(SDS((s, d), BF16), pl.BlockSpec((tr, d), rows)), (SDS((s, d), BF16), pl.BlockSpec((tr, d), rows)),
         (SDS((s, d), BF16), pl.BlockSpec((tr, d), rows)), (SDS((n_r, 2, d), F32), part_spec(2))],
        ep_ple,
    )

    g_wpp = _mm_tn("d_w_ple_proj", p2, d_ple)
    g_wpg = _mm_tn("d_w_ple_gate", n3, d_g3)

    def ep_norm_bwd(accs, ex, os_):
        dh, dg = _rms_bwd(accs[0], ex[0][...].astype(F32), ex[2][...])
        os_[0][...] = (ex[1][...].astype(F32) + dh).astype(BF16)
        os_[1][...] = dg

    norm_outs = lambda t: [(SDS((s, d), BF16), pl.BlockSpec((t, d), rows)), (SDS((s // t, 1, d), F32), part_spec(1))]
    def exchange1(names, wholes):
        return _rs1(wholes, kinds_(names), shapes_(names))

    def pair_sums(names, wholes, got):
        return [_pair_sum("pair_sum_" + nm, wholes[t], kind_of[nm], got[t]) for t, nm in enumerate(names)]

    lands = {}
    grp1 = ["w_ple_proj", "w_ple_gate"]
    (dh2b, part_ple), got = _fmm(
        "d_n3", (n_r, 1, 1),
        [(d_g3, pl.BlockSpec((tr, d), rows)), (wpg, pl.BlockSpec((d, d), whole))],
        [(0, 1, NT, 0, None)], [(tr, d)],
        [(h2, pl.BlockSpec((tr, d), rows)), (dh3, pl.BlockSpec((tr, d), rows)), (g_ple, pl.BlockSpec((1, d), whole))],
        norm_outs(tr), ep_norm_bwd,
        comm=exchange1(grp1, [g_wpp, g_wpg]),
    )
    sums1 = pair_sums(grp1, [g_wpp, g_wpg], got)

    def ep_ddown(accs, ex, os_):
        df = accs[0]
        os_[0][...] = (df * ex[0][...].astype(F32)).astype(BF16)
        os_[1][...] = (df * ex[1][...].astype(F32)).astype(BF16)

    ff_in = pl.BlockSpec((tm, tf), ij)
    (d_g, d_u), got = _fmm(
        "d_down", (s // tm, hidden // tf, 1),
        [(dh2b, pl.BlockSpec((tm, d), row_i)), (wdn_p, pl.BlockSpec((tf, d), lambda i, j, k: (j, 0)))],
        [(0, 1, NT, 0, None)], [(tm, tf)], [(df_dg, ff_in), (df_du, ff_in)],
        [ff_sd, ff_sd], ep_ddown, csplit=EPILOGUE_CHUNK,
        comm=_rs2(sums1),
    )
    lands.update(zip(grp1, got))
    tk = _tile(s, 1024)
    g_wdn = _mm_tn("d_w_down", f_act, dh2b, tm=1408, tn=2048).reshape(NDEV, fs, d)

    def ep_two_bf16(accs, ex, os_):
        os_[0][...] = accs[0].astype(BF16)
        os_[1][...] = accs[1].astype(BF16)

    ff_k = pl.BlockSpec((tk, tf), lambda i, j, k: (k, j))
    wcol_sd = (SDS((d, hidden), BF16), pl.BlockSpec((d, tf), lambda i, j, k: (0, j)))
    grp2 = ["w_down"]
    (g_wg_p, g_wu_p), got = _fmm(
        "d_w_gate_up", (1, hidden // tf, s // tk),
        [(n2, pl.BlockSpec((tk, d), lambda i, j, k: (k, 0))), (d_g, ff_k), (d_u, ff_k)],
        [(0, 1, TN, 0, None), (0, 2, TN, 1, None)], [(d, tf), (d, tf)], [],
        [wcol_sd, wcol_sd], ep_two_bf16,
        comm=exchange1(grp2, [g_wdn]),
    )
    g_wg = _plain_to_slabs("d_w_gate_slabs", g_wg_p, NDEV)
    g_wu = _plain_to_slabs("d_w_up_slabs", g_wu_p, NDEV)
    sums2 = pair_sums(grp2, [g_wdn], got)
    grp3 = ["w_gate", "w_up"]
    th = _tile(s // 2, 1024)
    ff_a = pl.BlockSpec((th, tf), lambda i, j, k: (i, k))
    w_k = pl.BlockSpec((d, tf), lambda i, j, k: (0, k))
    (d_n2,), got = _fmm(
        "d_n2", (s // th, 1, hidden // tf),
        [(d_g, ff_a), (wg_p, w_k), (d_u, ff_a), (wu_p, w_k)],
        [(0, 1, NT, 0, None), (2, 3, NT, 0, None)], [(th, d)], [],
        [(SDS((s, d), BF16), pl.BlockSpec((th, d), rows))], _ep_bf16,
        comm=_join(_rs2(sums2), exchange1(grp3, [g_wg, g_wu])),
    )
    lands.update(zip(grp2, got[:1]))
    sums3 = pair_sums(grp3, [g_wg, g_wu], got[1:])
    dh1b, part_ffn = _norm_bwd("d_h1", d_n2, h1, dh2b, g_ffn, BF16)
    g_wo = _mm_tn("d_w_o", m_act, dh1b)

    def ep_dm(accs, ex, os_):
        ya_, yb_ = ex[0][...].astype(F32), ex[1][...].astype(F32)
        sa = _sigmoid(ex[2][...].astype(F32))
        sb = _sigmoid(ex[3][...].astype(F32))
        dm = accs[0]
        d_yb = dm * sb
        os_[0][...] = (dm * sa).astype(BF16)
        os_[1][...] = d_yb.astype(BF16)
        os_[2][...] = (dm * ya_ * sa * (1.0 - sa)).astype(BF16)
        os_[3][...] = (dm * yb_ * sb * (1.0 - sb)).astype(BF16)
        os_[4][...] = jnp.sum(d_yb, axis=0, keepdims=True)

    tile_ij = pl.BlockSpec((tm, tn), ij)
    grp4 = ["w_o"]
    (d_ya, d_yb, d_ga, d_gb, part_bpw), got = _fmm(
        "d_merge", (s // tm, d // tn, 1),
        [(dh1b, pl.BlockSpec((tm, d), row_i)), (wo, pl.BlockSpec((tn, d), lambda i, j, k: (j, 0)))],
        [(0, 1, NT, 0, None)], [(tm, tn)],
        [(ya, tile_ij), (yb, tile_ij), (proj, gate_a_spec), (proj, gate_b_spec)],
        [out_sd, out_sd, out_sd, out_sd,
         (SDS((s // tm, 1, d), F32), pl.BlockSpec((None, 1, tn), lambda i, j, k: (i, 0, j)))],
        ep_dm, csplit=EPILOGUE_CHUNK,
        comm=exchange1(grp4, [g_wo]),
    )
    sums4 = pair_sums(grp4, [g_wo], got)
    g_wouta = _mm_tn("d_w_out_a", ya_in, d_ya)
    g_wpw = _mm_tn("d_w_pw_b", v_act, d_yb)
    grp5 = ["w_out_a", "w_pw_b"]
    d_ya_in, got = _mm_nt("d_ya_in", d_ya, wouta, comm=exchange1(grp5, [g_wouta, g_wpw]))
    sums5 = pair_sums(grp5, [g_wouta, g_wpw], got)
    d_v, _ = _mm_nt("d_v", d_yb, wpw)
    d_cv, part_ln = _mix_b_bwd1(d_v, cv, conf_ln_g, conf_ln_b, s, c)
    (d_b, part_wd, part_bglu), got = _mix_b_bwd2(d_cv, u_act, proj, b_glu, wd, s, c, comm=_rs2(sums3))
    lands.update(zip(grp3, got))
    d_a, part_wa = _mix_a_bwd(d_ya_in, proj, wa, s, c)

    nb = nin // c
    gblk = d // c
    lo = [0, 3, 5, 5 + gblk]
    hi = [3, 5, 5 + gblk, 5 + 2 * gblk]
    pieces = [d_a, d_b, d_ga, d_gb]

    def active(q, ax):
        return lambda ids: jnp.logical_and(ids[ax] >= lo[q], ids[ax] < hi[q])

    def piece_spec(q, rows_, ax, row0=0):
        def index(i, j, k):
            ids = (i, j, k)
            col = jnp.clip(ids[ax] - lo[q], 0, hi[q] - lo[q] - 1)
            row = i + row0 if ax == 2 else jnp.where(active(q, ax)(ids), k, 0)
            return (row, col)

        return pl.BlockSpec((rows_, c), index)

    tkw = _tile(s, 1024)
    (g_win,), got = _fmm(
        "d_w_in", (1, nb, s // tkw),
        [(n1, pl.BlockSpec((tkw, d), lambda i, j, k: (k, 0)))]
        + [(pieces[q], piece_spec(q, tkw, 1)) for q in range(4)],
        [(0, 1 + q, TN, 0, active(q, 1)) for q in range(4)], [(d, c)], [],
        [(SDS((d, nin), BF16), pl.BlockSpec((d, c), lambda i, j, k: (0, j)))], _ep_bf16,
        comm=_rs2(sums4 + sums5),
    )
    lands.update(zip(grp4 + grp5, got))

    grp6 = ["w_in"]
    n_half = max(1, 3 * (s // th) // 8)

    def d_n1_rows(name, row0, n_tiles, comm, into):
        return _fmm(
            name, (n_tiles, 1, nb),
            [(pieces[q], piece_spec(q, th, 2, row0)) for q in range(4)]
            + [(win, pl.BlockSpec((d, c), lambda i, j, k: (0, k)))],
            [(q, 4, NT, 0, active(q, 2)) for q in range(4)], [(th, d)], [],
            [(SDS((s, d), BF16), pl.BlockSpec((th, d), lambda i, j, k: (i + row0, 0)))], _ep_bf16,
            comm=comm, into=into,
        )

    (d_n1,), got = d_n1_rows("d_n1_a", 0, n_half, exchange1(grp6, [g_win]), None)
    (d_n1,), got = d_n1_rows("d_n1_b", n_half, s // th - n_half, _rs2(pair_sums(grp6, [g_win], got)), d_n1)
    lands.update(zip(grp6, got))
    dx, part_mix = _norm_bwd("d_x", d_n1, x2, dh1b, g_mix, F32)

    small_parts = [part_mix, part_bglu, part_ln, part_bpw, part_ffn, part_ple, part_fin, part_wa, part_wd]
    (o_mix, o_bglu, o_ln, o_bpw, o_ffn, o_ple, o_fin, o_wa, o_wd), _ = _packed_rows(small_parts, c)
    total = _all_reduce_small(small_parts, c)
    loss = total[o_fin + d // c, 0]

    big_m = dict(w_in=m_w_in, w_out_a=m_w_out_a, w_pw_b=m_w_pw_b, w_ple_proj=m_w_ple_proj, w_o=m_w_o,
                 w_ple_gate=m_w_ple_gate, w_gate=m_w_gate, w_up=m_w_up, w_down=m_w_down)
    big_v = dict(w_in=v_w_in, w_out_a=v_w_out_a, w_pw_b=v_w_pw_b, w_ple_proj=v_w_ple_proj, w_o=v_w_o,
                 w_ple_gate=v_w_ple_gate, w_gate=v_w_gate, w_up=v_w_up, w_down=v_w_down)
    big_out = {}
    for nm in weight:
        res = _adamw_big("adamw_" + nm, lands[nm], weight[nm][0], big_m[nm][0], big_v[nm][0])
        big_out[nm] = [r[None] for r in res]

    row = lambda a: a.reshape(1, d)
    small = dict(
        g_mix=(o_mix, 0, g_mix, m_g_mix, v_g_mix),
        conv_a_w=(o_wa, CONV_A_K, conv_a_w, m_conv_a_w, v_conv_a_w),
        b_glu=(o_bglu, 0, b_glu, m_b_glu, v_b_glu),
        conf_dw_w=(o_wd, CONF_K, conf_dw_w, m_conf_dw_w, v_conf_dw_w),
        conf_dw_b=(o_ln + 2, 0, conf_dw_b, m_conf_dw_b, v_conf_dw_b),
        conf_ln_g=(o_ln, 0, conf_ln_g, m_conf_ln_g, v_conf_ln_g),
        conf_ln_b=(o_ln + 1, 0, conf_ln_b, m_conf_ln_b, v_conf_ln_b),
        b_pw_b=(o_bpw, 0, b_pw_b, m_b_pw_b, v_b_pw_b),
        g_ffn=(o_ffn, 0, g_ffn, m_g_ffn, v_g_ffn),
        g_ple=(o_ple, 0, g_ple, m_g_ple, v_g_ple),
        g_final=(o_fin, 0, row(g_final), row(m_g_final), row(v_g_final)),
    )
    small_out = dict(zip(small, _adamw_small(total, list(small.values()))))
    small_out["g_final"] = [a.reshape(d) for a in small_out["g_final"]]

    order = ["g_mix", "w_in", "conv_a_w", "w_out_a", "b_glu", "conf_dw_w", "conf_dw_b", "conf_ln_g", "conf_ln_b", "w_pw_b", "b_pw_b", "w_o", "g_ffn", "w_gate", "w_up", "w_down", "g_ple", "w_ple_gate", "w_ple_proj", "g_final"]
    allo = {**big_out, **small_out}
    outs = [loss, dx[None]]
    for q in range(4):
        outs += [allo[nm][q] for nm in order]
    return tuple(outs)
```
